```python
import math
import jax, jax.numpy as jnp
from jax import lax
import numpy as np

D_MODEL = 2048
BATCH = 8
SEQ = 2048
DEPTH = 4

N_MIXERS = 3
D_INNER = D_MODEL
S5_GROUP = 16
S5_STATE = 64
S5_GROUPS = D_INNER // S5_GROUP
FOX_HEAD_DIM = 128
FOX_HEADS = D_INNER // FOX_HEAD_DIM
Q_BLOCK = 128
POOL_WINDOWS = (2, 4, 8, 16)
POOL_GROUPS = len(POOL_WINDOWS)
POOL_GROUP_DIM = D_INNER // POOL_GROUPS
N_S5 = (DEPTH + 2) // 3
N_FOX = (DEPTH + 1) // 3
N_POOL = DEPTH // 3
EPS = 1e-6
DT_MIN = 1e-3
DT_MAX = 1e-1

kernel_name = "hybrid_s5_fox_pool_interleaved"


def rmsnorm(x, w):
    xf = x.astype(jnp.float32)
    y = xf * lax.rsqrt(jnp.mean(xf * xf, axis=-1, keepdims=True) + EPS)
    return (y * w.astype(jnp.float32)).astype(x.dtype)


def _s5_combine(left, right):
    a1r, a1i, b1r, b1i = left
    a2r, a2i, b2r, b2i = right
    ar = a2r * a1r - a2i * a1i
    ai = a2r * a1i + a2i * a1r
    br = a2r * b1r - a2i * b1i + b2r
    bi = a2r * b1i + a2i * b1r + b2i
    return (ar, ai, br, bi)


def s5_mixer(u, a_re, a_im, log_dt, b_re, b_im, c_re, c_im, d_skip, w_glu, b_glu):
    f32 = jnp.float32
    bsz, L, E = u.shape
    ar = a_re.astype(f32)
    ai = a_im.astype(f32)
    dt = jnp.exp(log_dt.astype(f32))[:, None]
    mag = jnp.exp(ar * dt)
    abar_r = mag * jnp.cos(ai * dt)
    abar_i = mag * jnp.sin(ai * dt)
    den = ar * ar + ai * ai
    xr = abar_r - 1.0
    fr = (xr * ar + abar_i * ai) / den
    fi = (abar_i * ar - xr * ai) / den
    br = b_re.astype(f32)
    bi = b_im.astype(f32)
    bbar_r = fr[..., None] * br - fi[..., None] * bi
    bbar_i = fr[..., None] * bi + fi[..., None] * br
    ug = u.astype(f32).reshape(bsz, L, S5_GROUPS, S5_GROUP)
    bu_r = jnp.einsum('blgc,gpc->blgp', ug, bbar_r)
    bu_i = jnp.einsum('blgc,gpc->blgp', ug, bbar_i)
    a_r_el = jnp.broadcast_to(abar_r[None, None], (1, L, S5_GROUPS, S5_STATE))
    a_i_el = jnp.broadcast_to(abar_i[None, None], (1, L, S5_GROUPS, S5_STATE))
    _, _, h_r, h_i = lax.associative_scan(_s5_combine, (a_r_el, a_i_el, bu_r, bu_i), axis=1)
    y = (jnp.einsum('blgp,gcp->blgc', h_r, c_re.astype(f32))
         - jnp.einsum('blgp,gcp->blgc', h_i, c_im.astype(f32)))
    y = y.reshape(bsz, L, E) + d_skip.astype(f32) * u.astype(f32)
    g = jax.nn.gelu(y)
    y = g * jax.nn.sigmoid(g @ w_glu.astype(f32) + b_glu.astype(f32))
    return y.astype(u.dtype)


def fox_mixer(q, k, v, f_logit, q_norm_w, k_norm_w):
    f32 = jnp.float32
    bsz, L, H, Dh = q.shape
    q = rmsnorm(q, q_norm_w)
    k = rmsnorm(k, k_norm_w)
    cum = jnp.cumsum(jax.nn.log_sigmoid(f_logit.astype(f32)), axis=1)
    cum_k = cum.transpose(0, 2, 1)[:, :, None, :]
    scale = Dh ** -0.5
    nb = L // Q_BLOCK
    qb = q.reshape(bsz, nb, Q_BLOCK, H, Dh).transpose(1, 0, 2, 3, 4)
    cb = cum.reshape(bsz, nb, Q_BLOCK, H).transpose(1, 0, 2, 3)
    kpos = jnp.arange(L)

    def block(args):
        i, q_i, c_i = args
        s = jnp.einsum('bqhd,bkhd->bhqk', q_i, k).astype(f32) * scale
        s = s + c_i.transpose(0, 2, 1)[..., None] - cum_k
        qpos = i * Q_BLOCK + jnp.arange(Q_BLOCK)
        s = jnp.where(kpos[None, :] <= qpos[:, None], s, -jnp.inf)
        p = jax.nn.softmax(s, axis=-1)
        return jnp.einsum('bhqk,bkhd->bqhd', p.astype(v.dtype), v)

    out = lax.map(block, (jnp.arange(nb), qb, cb))
    return out.transpose(1, 0, 2, 3, 4).reshape(bsz, L, H * Dh)


def pool_mixer(u, w_group, layer_scale):
    f32 = jnp.float32
    bsz, L, E = u.shape
    uf = u.astype(f32)
    cs = jnp.concatenate([jnp.zeros((bsz, 1, E), f32), jnp.cumsum(uf, axis=1)], axis=1)
    t = jnp.arange(L)
    pooled = []
    for g, w in enumerate(POOL_WINDOWS):
        sl = slice(g * POOL_GROUP_DIM, (g + 1) * POOL_GROUP_DIM)
        lo = jnp.maximum(t + 1 - w, 0)
        s = cs[:, 1:, sl] - cs[:, lo, sl]
        cnt = jnp.minimum(t + 1, w).astype(f32)
        pooled.append(s / cnt[None, :, None])
    pooled = jnp.stack(pooled, axis=2)
    ug = uf.reshape(bsz, L, POOL_GROUPS, POOL_GROUP_DIM)
    mixed = jnp.einsum('blgc,gcd->blgd', pooled - ug, w_group.astype(f32))
    return (mixed.reshape(bsz, L, E) * layer_scale.astype(f32)).astype(u.dtype)


def _fwd_setup_inputs(seed: int = 0) -> dict:
    key = jax.random.key(seed)
    ks = jax.random.split(key, 24)
    f32 = jnp.float32
    nrm = lambda k, shape, s: jax.random.normal(k, shape, f32) * s
    E, G, P, C = D_INNER, S5_GROUPS, S5_STATE, S5_GROUP
    x = nrm(ks[0], (BATCH, SEQ, D_MODEL), 1.0)
    norm_w = 1.0 + nrm(ks[1], (DEPTH, D_MODEL), 0.02)
    out_proj = nrm(ks[2], (DEPTH, E, D_MODEL), E ** -0.5 / math.sqrt(DEPTH))
    s5_in_proj = nrm(ks[3], (N_S5, D_MODEL, 2 * E), D_MODEL ** -0.5)
    n_idx = jnp.arange(P, dtype=f32)
    s5_a_re = -0.5 + nrm(ks[4], (N_S5, G, P), 0.01)
    s5_a_im = math.pi * n_idx[None, None, :] + nrm(ks[5], (N_S5, G, P), 0.01)
    s5_log_dt = jax.random.uniform(ks[6], (N_S5, G), f32, math.log(DT_MIN), math.log(DT_MAX))
    s5_b_re = nrm(ks[7], (N_S5, G, P, C), (2 * C) ** -0.5)
    s5_b_im = nrm(ks[8], (N_S5, G, P, C), (2 * C) ** -0.5)
    s5_c_re = nrm(ks[9], (N_S5, G, C, P), (2 * P) ** -0.5 * 2.0)
    s5_c_im = nrm(ks[10], (N_S5, G, C, P), (2 * P) ** -0.5 * 2.0)
    s5_d = nrm(ks[11], (N_S5, E), 1.0)
    s5_w_glu = nrm(ks[12], (N_S5, E, E), E ** -0.5)
    s5_b_glu = nrm(ks[13], (N_S5, E), 0.01)
    fox_in_proj = nrm(ks[14], (N_FOX, D_MODEL, 4 * E + FOX_HEADS), D_MODEL ** -0.5)
    fox_q_norm = 1.0 + nrm(ks[15], (N_FOX, FOX_HEAD_DIM), 0.02)
    fox_k_norm = 1.0 + nrm(ks[16], (N_FOX, FOX_HEAD_DIM), 0.02)
    fox_f_bias = jax.random.uniform(ks[17], (N_FOX, FOX_HEADS), f32, 1.0, 4.0)
    pool_in_proj = nrm(ks[18], (N_POOL, D_MODEL, 2 * E), D_MODEL ** -0.5)
    pool_w_group = nrm(ks[19], (N_POOL, POOL_GROUPS, POOL_GROUP_DIM, POOL_GROUP_DIM), POOL_GROUP_DIM ** -0.5)
    pool_scale = 1.0 + nrm(ks[20], (N_POOL, E), 0.1)
    return {"x": x, "norm_w": norm_w, "out_proj": out_proj,
            "s5_in_proj": s5_in_proj, "s5_a_re": s5_a_re, "s5_a_im": s5_a_im,
            "s5_log_dt": s5_log_dt, "s5_b_re": s5_b_re, "s5_b_im": s5_b_im,
            "s5_c_re": s5_c_re, "s5_c_im": s5_c_im, "s5_d": s5_d,
            "s5_w_glu": s5_w_glu, "s5_b_glu": s5_b_glu,
            "fox_in_proj": fox_in_proj, "fox_q_norm": fox_q_norm,
            "fox_k_norm": fox_k_norm, "fox_f_bias": fox_f_bias,
            "pool_in_proj": pool_in_proj, "pool_w_group": pool_w_group,
            "pool_scale": pool_scale}


def _fwd_reference(x, norm_w, out_proj, s5_in_proj, s5_a_re, s5_a_im, s5_log_dt, s5_b_re, s5_b_im,
              s5_c_re, s5_c_im, s5_d, s5_w_glu, s5_b_glu, fox_in_proj, fox_q_norm, fox_k_norm,
              fox_f_bias, pool_in_proj, pool_w_group, pool_scale):
    E = D_INNER
    bsz, L, _ = x.shape
    h = x
    for i in range(DEPTH):
        kind = i % N_MIXERS
        j = i // N_MIXERS
        xn = rmsnorm(h, norm_w[i])
        if kind == 0:
            proj = xn @ s5_in_proj[j]
            u, z = proj[..., :E], proj[..., E:]
            y = s5_mixer(u, s5_a_re[j], s5_a_im[j], s5_log_dt[j], s5_b_re[j], s5_b_im[j],
                         s5_c_re[j], s5_c_im[j], s5_d[j], s5_w_glu[j], s5_b_glu[j])
        elif kind == 1:
            proj = xn @ fox_in_proj[j]
            hs = (bsz, L, FOX_HEADS, FOX_HEAD_DIM)
            q = proj[..., :E].reshape(hs)
            k = proj[..., E:2 * E].reshape(hs)
            v = proj[..., 2 * E:3 * E].reshape(hs)
            z = proj[..., 3 * E:4 * E]
            f_logit = proj[..., 4 * E:] + fox_f_bias[j]
            y = fox_mixer(q, k, v, f_logit, fox_q_norm[j], fox_k_norm[j])
        else:
            proj = xn @ pool_in_proj[j]
            u, z = proj[..., :E], proj[..., E:]
            y = pool_mixer(u, pool_w_group[j], pool_scale[j])
        h = h + (y * jax.nn.silu(z)) @ out_proj[i]
    return h


import jax as _jax
import jax.numpy as _jnp

TWIN_FORMAT = 'train_step'
FWD_PARAMS = ['x', 'norm_w', 'out_proj', 's5_in_proj', 's5_a_re', 's5_a_im', 's5_log_dt', 's5_b_re', 's5_b_im', 's5_c_re', 's5_c_im', 's5_d', 's5_w_glu', 's5_b_glu', 'fox_in_proj', 'fox_q_norm', 'fox_k_norm', 'fox_f_bias', 'pool_in_proj', 'pool_w_group', 'pool_scale']
TWIN_WEIGHTS = ['norm_w', 'out_proj', 's5_in_proj', 's5_a_re', 's5_a_im', 's5_log_dt', 's5_b_re', 's5_b_im', 's5_c_re', 's5_c_im', 's5_d', 's5_w_glu', 's5_b_glu', 'fox_in_proj', 'fox_q_norm', 'fox_k_norm', 'fox_f_bias', 'pool_in_proj', 'pool_w_group', 'pool_scale']
TWIN_DIFF_INPUT = 'x'
TWIN_INPUTS = ['x', 'norm_w', 'out_proj', 's5_in_proj', 's5_a_re', 's5_a_im', 's5_log_dt', 's5_b_re', 's5_b_im', 's5_c_re', 's5_c_im', 's5_d', 's5_w_glu', 's5_b_glu', 'fox_in_proj', 'fox_q_norm', 'fox_k_norm', 'fox_f_bias', 'pool_in_proj', 'pool_w_group', 'pool_scale', 'loss_target', 'm_norm_w', 'm_out_proj', 'm_s5_in_proj', 'm_s5_a_re', 'm_s5_a_im', 'm_s5_log_dt', 'm_s5_b_re', 'm_s5_b_im', 'm_s5_c_re', 'm_s5_c_im', 'm_s5_d', 'm_s5_w_glu', 'm_s5_b_glu', 'm_fox_in_proj', 'm_fox_q_norm', 'm_fox_k_norm', 'm_fox_f_bias', 'm_pool_in_proj', 'm_pool_w_group', 'm_pool_scale', 'v_norm_w', 'v_out_proj', 'v_s5_in_proj', 'v_s5_a_re', 'v_s5_a_im', 'v_s5_log_dt', 'v_s5_b_re', 'v_s5_b_im', 'v_s5_c_re', 'v_s5_c_im', 'v_s5_d', 'v_s5_w_glu', 'v_s5_b_glu', 'v_fox_in_proj', 'v_fox_q_norm', 'v_fox_k_norm', 'v_fox_f_bias', 'v_pool_in_proj', 'v_pool_w_group', 'v_pool_scale']
TWIN_OUTPUTS = ['loss', 'grad_x', 'grad_norm_w', 'grad_out_proj', 'grad_s5_in_proj', 'grad_s5_a_re', 'grad_s5_a_im', 'grad_s5_log_dt', 'grad_s5_b_re', 'grad_s5_b_im', 'grad_s5_c_re', 'grad_s5_c_im', 'grad_s5_d', 'grad_s5_w_glu', 'grad_s5_b_glu', 'grad_fox_in_proj', 'grad_fox_q_norm', 'grad_fox_k_norm', 'grad_fox_f_bias', 'grad_pool_in_proj', 'grad_pool_w_group', 'grad_pool_scale', 'delta_norm_w', 'delta_out_proj', 'delta_s5_in_proj', 'delta_s5_a_re', 'delta_s5_a_im', 'delta_s5_log_dt', 'delta_s5_b_re', 'delta_s5_b_im', 'delta_s5_c_re', 'delta_s5_c_im', 'delta_s5_d', 'delta_s5_w_glu', 'delta_s5_b_glu', 'delta_fox_in_proj', 'delta_fox_q_norm', 'delta_fox_k_norm', 'delta_fox_f_bias', 'delta_pool_in_proj', 'delta_pool_w_group', 'delta_pool_scale', 'new_m_norm_w', 'new_m_out_proj', 'new_m_s5_in_proj', 'new_m_s5_a_re', 'new_m_s5_a_im', 'new_m_s5_log_dt', 'new_m_s5_b_re', 'new_m_s5_b_im', 'new_m_s5_c_re', 'new_m_s5_c_im', 'new_m_s5_d', 'new_m_s5_w_glu', 'new_m_s5_b_glu', 'new_m_fox_in_proj', 'new_m_fox_q_norm', 'new_m_fox_k_norm', 'new_m_fox_f_bias', 'new_m_pool_in_proj', 'new_m_pool_w_group', 'new_m_pool_scale', 'new_v_norm_w', 'new_v_out_proj', 'new_v_s5_in_proj', 'new_v_s5_a_re', 'new_v_s5_a_im', 'new_v_s5_log_dt', 'new_v_s5_b_re', 'new_v_s5_b_im', 'new_v_s5_c_re', 'new_v_s5_c_im', 'new_v_s5_d', 'new_v_s5_w_glu', 'new_v_s5_b_glu', 'new_v_fox_in_proj', 'new_v_fox_q_norm', 'new_v_fox_k_norm', 'new_v_fox_f_bias', 'new_v_pool_in_proj', 'new_v_pool_w_group', 'new_v_pool_scale']
TWIN_LEAF_KINDS = {'loss': 'loss', 'grad_x': 'grad_x', 'grad_norm_w': 'grad_w', 'grad_out_proj': 'grad_w', 'grad_s5_in_proj': 'grad_w', 'grad_s5_a_re': 'grad_w', 'grad_s5_a_im': 'grad_w', 'grad_s5_log_dt': 'grad_w', 'grad_s5_b_re': 'grad_w', 'grad_s5_b_im': 'grad_w', 'grad_s5_c_re': 'grad_w', 'grad_s5_c_im': 'grad_w', 'grad_s5_d': 'grad_w', 'grad_s5_w_glu': 'grad_w', 'grad_s5_b_glu': 'grad_w', 'grad_fox_in_proj': 'grad_w', 'grad_fox_q_norm': 'grad_w', 'grad_fox_k_norm': 'grad_w', 'grad_fox_f_bias': 'grad_w', 'grad_pool_in_proj': 'grad_w', 'grad_pool_w_group': 'grad_w', 'grad_pool_scale': 'grad_w', 'delta_norm_w': 'delta_w', 'delta_out_proj': 'delta_w', 'delta_s5_in_proj': 'delta_w', 'delta_s5_a_re': 'delta_w', 'delta_s5_a_im': 'delta_w', 'delta_s5_log_dt': 'delta_w', 'delta_s5_b_re': 'delta_w', 'delta_s5_b_im': 'delta_w', 'delta_s5_c_re': 'delta_w', 'delta_s5_c_im': 'delta_w', 'delta_s5_d': 'delta_w', 'delta_s5_w_glu': 'delta_w', 'delta_s5_b_glu': 'delta_w', 'delta_fox_in_proj': 'delta_w', 'delta_fox_q_norm': 'delta_w', 'delta_fox_k_norm': 'delta_w', 'delta_fox_f_bias': 'delta_w', 'delta_pool_in_proj': 'delta_w', 'delta_pool_w_group': 'delta_w', 'delta_pool_scale': 'delta_w', 'new_m_norm_w': 'new_m', 'new_m_out_proj': 'new_m', 'new_m_s5_in_proj': 'new_m', 'new_m_s5_a_re': 'new_m', 'new_m_s5_a_im': 'new_m', 'new_m_s5_log_dt': 'new_m', 'new_m_s5_b_re': 'new_m', 'new_m_s5_b_im': 'new_m', 'new_m_s5_c_re': 'new_m', 'new_m_s5_c_im': 'new_m', 'new_m_s5_d': 'new_m', 'new_m_s5_w_glu': 'new_m', 'new_m_s5_b_glu': 'new_m', 'new_m_fox_in_proj': 'new_m', 'new_m_fox_q_norm': 'new_m', 'new_m_fox_k_norm': 'new_m', 'new_m_fox_f_bias': 'new_m', 'new_m_pool_in_proj': 'new_m', 'new_m_pool_w_group': 'new_m', 'new_m_pool_scale': 'new_m', 'new_v_norm_w': 'new_v', 'new_v_out_proj': 'new_v', 'new_v_s5_in_proj': 'new_v', 'new_v_s5_a_re': 'new_v', 'new_v_s5_a_im': 'new_v', 'new_v_s5_log_dt': 'new_v', 'new_v_s5_b_re': 'new_v', 'new_v_s5_b_im': 'new_v', 'new_v_s5_c_re': 'new_v', 'new_v_s5_c_im': 'new_v', 'new_v_s5_d': 'new_v', 'new_v_s5_w_glu': 'new_v', 'new_v_s5_b_glu': 'new_v', 'new_v_fox_in_proj': 'new_v', 'new_v_fox_q_norm': 'new_v', 'new_v_fox_k_norm': 'new_v', 'new_v_fox_f_bias': 'new_v', 'new_v_pool_in_proj': 'new_v', 'new_v_pool_w_group': 'new_v', 'new_v_pool_scale': 'new_v'}


def _forward(args):
    return _fwd_reference(*[args[k] for k in FWD_PARAMS])


def _output_shape():
    out = _jax.eval_shape(lambda: _forward(_fwd_setup_inputs(0)))
    return out.shape, out.dtype

N_MICROBATCH = 1
ADAM_LR = 0.001
ADAM_B1 = 0.9
ADAM_B2 = 0.999
ADAM_EPS = 1e-08
ADAM_WD = 0.01
ADAM_STEP = 10
PER_EXAMPLE_BATCH_AXIS = {'x': 0, 'loss_target': 0}
SHARED_INPUTS = []
_WEIGHT_DTYPES = {'norm_w': _jnp.float32, 'out_proj': _jnp.float32, 's5_in_proj': _jnp.float32, 's5_a_re': _jnp.float32, 's5_a_im': _jnp.float32, 's5_log_dt': _jnp.float32, 's5_b_re': _jnp.float32, 's5_b_im': _jnp.float32, 's5_c_re': _jnp.float32, 's5_c_im': _jnp.float32, 's5_d': _jnp.float32, 's5_w_glu': _jnp.float32, 's5_b_glu': _jnp.float32, 'fox_in_proj': _jnp.float32, 'fox_q_norm': _jnp.float32, 'fox_k_norm': _jnp.float32, 'fox_f_bias': _jnp.float32, 'pool_in_proj': _jnp.float32, 'pool_w_group': _jnp.float32, 'pool_scale': _jnp.float32}
MOMENT_SCALE = {'norm_w': 6.334877e-01, 'out_proj': 3.524902e-02, 's5_in_proj': 1.387300e-02, 's5_a_re': 1.293111e-03, 's5_a_im': 1.191688e-03, 's5_log_dt': 7.349461e-01, 's5_b_re': 8.266796e-04, 's5_b_im': 8.158809e-04, 's5_c_re': 8.311682e-04, 's5_c_im': 8.184127e-04, 's5_d': 1.053229e-01, 's5_w_glu': 2.155074e-02, 's5_b_glu': 6.778472e-02, 'fox_in_proj': 1.085028e-02, 'fox_q_norm': 5.661823e-01, 'fox_k_norm': 5.658551e-01, 'fox_f_bias': 3.691170e+00, 'pool_in_proj': 3.139122e-02, 'pool_w_group': 3.546353e-02, 'pool_scale': 5.687004e-01}


def _to_microbatches(a, axis):
    t = _jnp.moveaxis(a, axis, 0)
    t = t.reshape((N_MICROBATCH, t.shape[0] // N_MICROBATCH) + t.shape[1:])
    return _jnp.moveaxis(t, 1, axis + 1)


def setup_inputs(seed: int = 0) -> dict:
    inp = _fwd_setup_inputs(seed)
    key = _jax.random.fold_in(_jax.random.key(seed), 7919)
    shape, _ = _output_shape()
    out = dict(inp)
    out["loss_target"] = _jax.random.normal(_jax.random.fold_in(key, 0), shape, _jnp.float32)
    for i, name in enumerate(TWIN_WEIGHTS):
        w = inp[name].astype(_jnp.float32)
        if MOMENT_SCALE is None:
            s = _jnp.sqrt(_jnp.mean(_jnp.square(w)) + 1e-30)
        else:
            s = MOMENT_SCALE[name]
        km, kv = _jax.random.split(_jax.random.fold_in(key, i + 1))
        out[name] = w
        out["m_" + name] = s * _jax.random.normal(km, w.shape, _jnp.float32)
        out["v_" + name] = (s * s) * _jax.random.uniform(kv, w.shape, _jnp.float32, 0.5, 1.5)
    if N_MICROBATCH > 1:
        for name, axis in PER_EXAMPLE_BATCH_AXIS.items():
            out[name] = _to_microbatches(out[name], axis)
    return {'x': out['x'], 'norm_w': out['norm_w'], 'out_proj': out['out_proj'], 's5_in_proj': out['s5_in_proj'], 's5_a_re': out['s5_a_re'], 's5_a_im': out['s5_a_im'], 's5_log_dt': out['s5_log_dt'], 's5_b_re': out['s5_b_re'], 's5_b_im': out['s5_b_im'], 's5_c_re': out['s5_c_re'], 's5_c_im': out['s5_c_im'], 's5_d': out['s5_d'], 's5_w_glu': out['s5_w_glu'], 's5_b_glu': out['s5_b_glu'], 'fox_in_proj': out['fox_in_proj'], 'fox_q_norm': out['fox_q_norm'], 'fox_k_norm': out['fox_k_norm'], 'fox_f_bias': out['fox_f_bias'], 'pool_in_proj': out['pool_in_proj'], 'pool_w_group': out['pool_w_group'], 'pool_scale': out['pool_scale'], 'loss_target': out['loss_target'], 'm_norm_w': out['m_norm_w'], 'm_out_proj': out['m_out_proj'], 'm_s5_in_proj': out['m_s5_in_proj'], 'm_s5_a_re': out['m_s5_a_re'], 'm_s5_a_im': out['m_s5_a_im'], 'm_s5_log_dt': out['m_s5_log_dt'], 'm_s5_b_re': out['m_s5_b_re'], 'm_s5_b_im': out['m_s5_b_im'], 'm_s5_c_re': out['m_s5_c_re'], 'm_s5_c_im': out['m_s5_c_im'], 'm_s5_d': out['m_s5_d'], 'm_s5_w_glu': out['m_s5_w_glu'], 'm_s5_b_glu': out['m_s5_b_glu'], 'm_fox_in_proj': out['m_fox_in_proj'], 'm_fox_q_norm': out['m_fox_q_norm'], 'm_fox_k_norm': out['m_fox_k_norm'], 'm_fox_f_bias': out['m_fox_f_bias'], 'm_pool_in_proj': out['m_pool_in_proj'], 'm_pool_w_group': out['m_pool_w_group'], 'm_pool_scale': out['m_pool_scale'], 'v_norm_w': out['v_norm_w'], 'v_out_proj': out['v_out_proj'], 'v_s5_in_proj': out['v_s5_in_proj'], 'v_s5_a_re': out['v_s5_a_re'], 'v_s5_a_im': out['v_s5_a_im'], 'v_s5_log_dt': out['v_s5_log_dt'], 'v_s5_b_re': out['v_s5_b_re'], 'v_s5_b_im': out['v_s5_b_im'], 'v_s5_c_re': out['v_s5_c_re'], 'v_s5_c_im': out['v_s5_c_im'], 'v_s5_d': out['v_s5_d'], 'v_s5_w_glu': out['v_s5_w_glu'], 'v_s5_b_glu': out['v_s5_b_glu'], 'v_fox_in_proj': out['v_fox_in_proj'], 'v_fox_q_norm': out['v_fox_q_norm'], 'v_fox_k_norm': out['v_fox_k_norm'], 'v_fox_f_bias': out['v_fox_f_bias'], 'v_pool_in_proj': out['v_pool_in_proj'], 'v_pool_w_group': out['v_pool_w_group'], 'v_pool_scale': out['v_pool_scale']}


def _loss(weights, diff, rest, loss_target):
    with _jax.named_scope("forward"):
        args = {**rest, TWIN_DIFF_INPUT: diff, **{k: w.astype(_WEIGHT_DTYPES[k]) for k, w in weights.items()}}
        y = _forward(args)
    with _jax.named_scope("loss_head"):
        err = _jnp.square(y.astype(_jnp.float32) - loss_target)
        return 0.5 * _jnp.sum(_jnp.mean(err, axis=-1)) if err.ndim else 0.5 * err


def _adamw(w, g, m, v):
    m = ADAM_B1 * m + (1.0 - ADAM_B1) * g
    v = ADAM_B2 * v + (1.0 - ADAM_B2) * _jnp.square(g)
    m_hat = m / (1.0 - ADAM_B1 ** ADAM_STEP)
    v_hat = v / (1.0 - ADAM_B2 ** ADAM_STEP)
    delta = -ADAM_LR * (m_hat / (_jnp.sqrt(v_hat) + ADAM_EPS) + ADAM_WD * w)
    return delta, m, v


def reference(x, norm_w, out_proj, s5_in_proj, s5_a_re, s5_a_im, s5_log_dt, s5_b_re, s5_b_im, s5_c_re, s5_c_im, s5_d, s5_w_glu, s5_b_glu, fox_in_proj, fox_q_norm, fox_k_norm, fox_f_bias, pool_in_proj, pool_w_group, pool_scale, loss_target, m_norm_w, m_out_proj, m_s5_in_proj, m_s5_a_re, m_s5_a_im, m_s5_log_dt, m_s5_b_re, m_s5_b_im, m_s5_c_re, m_s5_c_im, m_s5_d, m_s5_w_glu, m_s5_b_glu, m_fox_in_proj, m_fox_q_norm, m_fox_k_norm, m_fox_f_bias, m_pool_in_proj, m_pool_w_group, m_pool_scale, v_norm_w, v_out_proj, v_s5_in_proj, v_s5_a_re, v_s5_a_im, v_s5_log_dt, v_s5_b_re, v_s5_b_im, v_s5_c_re, v_s5_c_im, v_s5_d, v_s5_w_glu, v_s5_b_glu, v_fox_in_proj, v_fox_q_norm, v_fox_k_norm, v_fox_f_bias, v_pool_in_proj, v_pool_w_group, v_pool_scale):
    given = dict(x=x, norm_w=norm_w, out_proj=out_proj, s5_in_proj=s5_in_proj, s5_a_re=s5_a_re, s5_a_im=s5_a_im, s5_log_dt=s5_log_dt, s5_b_re=s5_b_re, s5_b_im=s5_b_im, s5_c_re=s5_c_re, s5_c_im=s5_c_im, s5_d=s5_d, s5_w_glu=s5_w_glu, s5_b_glu=s5_b_glu, fox_in_proj=fox_in_proj, fox_q_norm=fox_q_norm, fox_k_norm=fox_k_norm, fox_f_bias=fox_f_bias, pool_in_proj=pool_in_proj, pool_w_group=pool_w_group, pool_scale=pool_scale, loss_target=loss_target, m_norm_w=m_norm_w, m_out_proj=m_out_proj, m_s5_in_proj=m_s5_in_proj, m_s5_a_re=m_s5_a_re, m_s5_a_im=m_s5_a_im, m_s5_log_dt=m_s5_log_dt, m_s5_b_re=m_s5_b_re, m_s5_b_im=m_s5_b_im, m_s5_c_re=m_s5_c_re, m_s5_c_im=m_s5_c_im, m_s5_d=m_s5_d, m_s5_w_glu=m_s5_w_glu, m_s5_b_glu=m_s5_b_glu, m_fox_in_proj=m_fox_in_proj, m_fox_q_norm=m_fox_q_norm, m_fox_k_norm=m_fox_k_norm, m_fox_f_bias=m_fox_f_bias, m_pool_in_proj=m_pool_in_proj, m_pool_w_group=m_pool_w_group, m_pool_scale=m_pool_scale, v_norm_w=v_norm_w, v_out_proj=v_out_proj, v_s5_in_proj=v_s5_in_proj, v_s5_a_re=v_s5_a_re, v_s5_a_im=v_s5_a_im, v_s5_log_dt=v_s5_log_dt, v_s5_b_re=v_s5_b_re, v_s5_b_im=v_s5_b_im, v_s5_c_re=v_s5_c_re, v_s5_c_im=v_s5_c_im, v_s5_d=v_s5_d, v_s5_w_glu=v_s5_w_glu, v_s5_b_glu=v_s5_b_glu, v_fox_in_proj=v_fox_in_proj, v_fox_q_norm=v_fox_q_norm, v_fox_k_norm=v_fox_k_norm, v_fox_f_bias=v_fox_f_bias, v_pool_in_proj=v_pool_in_proj, v_pool_w_group=v_pool_w_group, v_pool_scale=v_pool_scale)
    weights = {n: given[n] for n in TWIN_WEIGHTS}
    shared = {n: given[n] for n in SHARED_INPUTS}
    per_example = {n: given[n] for n in ['x']}
    grad_fn = _jax.value_and_grad(_loss, argnums=(0, 1))

    def one_microbatch(ex, loss_target):
        ex = dict(ex)
        diff = ex.pop(TWIN_DIFF_INPUT)
        return grad_fn(weights, diff, {**shared, **ex}, loss_target)

    if N_MICROBATCH == 1:
        loss, (grad_w, grad_x) = one_microbatch(per_example, given["loss_target"])
    else:
        def body(carry, xs):
            loss_sum, grad_sum = carry
            l_k, (gw_k, gx_k) = one_microbatch(xs[0], xs[1])
            with _jax.named_scope("update"):
                return (loss_sum + l_k, _jax.tree.map(_jnp.add, grad_sum, gw_k)), gx_k

        init = (_jnp.zeros((), _jnp.float32), _jax.tree.map(_jnp.zeros_like, weights))
        (loss, grad_w), grad_x = _jax.lax.scan(body, init, (per_example, given["loss_target"]))
    with _jax.named_scope("update"):
        delta_w, new_m, new_v = {}, {}, {}
        for n in TWIN_WEIGHTS:
            delta_w[n], new_m[n], new_v[n] = _adamw(weights[n], grad_w[n], given["m_" + n], given["v_" + n])
    return (loss, grad_x, *[grad_w[n] for n in TWIN_WEIGHTS], *[delta_w[n] for n in TWIN_WEIGHTS],
            *[new_m[n] for n in TWIN_WEIGHTS], *[new_v[n] for n in TWIN_WEIGHTS])
```

```python
import functools
import math

import jax
import jax.numpy as jnp
from jax import lax
from jax.experimental import pallas as pl
from jax.experimental.pallas import tpu as pltpu

F32 = jnp.float32
BF16 = jnp.bfloat16
MESH = pl.DeviceIdType.MESH

N_CHIPS = 4
VMEM_LIMIT = 56 * 1024 * 1024
LANES = 128
SUB = 8

EPS = 1e-6
S5_GROUP = 16
S5_STATE = 64
GROUPS_PER_CHUNK = 16
FOX_HEAD_DIM = 128
POOL_WINDOWS = (2, 4, 8, 16)
POOL_HALO = 16
ADAM_LR, ADAM_B1, ADAM_B2, ADAM_EPS, ADAM_WD, ADAM_STEP = 0.001, 0.9, 0.999, 1e-08, 0.01, 10
NEG = -1e30


def _t(pref, dim):
    if dim <= pref:
        return dim
    t = pref - pref % 16
    while t > 16 and dim % t:
        t -= 16
    assert dim % t == 0, (pref, dim)
    return t


def _params(sem):
    return pltpu.CompilerParams(dimension_semantics=sem, vmem_limit_bytes=VMEM_LIMIT)


def _sigmoid(x):
    return 1.0 / (1.0 + jnp.exp(-x))


def _silu(z):
    return z * _sigmoid(z)


def _dsilu(z):
    s = _sigmoid(z)
    return s * (1.0 + z * (1.0 - s))


_GELU_C = math.sqrt(2.0 / math.pi)


def _gelu(x):
    return 0.5 * x * (1.0 + jnp.tanh(_GELU_C * (x + 0.044715 * (x * x * x))))


def _dgelu(x):
    t = jnp.tanh(_GELU_C * (x + 0.044715 * (x * x * x)))
    return 0.5 * (1.0 + t) + 0.5 * x * (1.0 - t * t) * (_GELU_C * (1.0 + 3.0 * 0.044715 * x * x))


def _log_sigmoid(x):
    return jnp.minimum(x, 0.0) - jnp.log(1.0 + jnp.exp(-jnp.abs(x)))


def _rms(x):
    return lax.rsqrt(jnp.mean(x * x, axis=-1, keepdims=True) + EPS)


def _rms_bwd(x, w, dy):
    r = _rms(x)
    xhat = x * r
    dxh = dy * w
    dx = r * (dxh - xhat * jnp.mean(dxh * xhat, axis=-1, keepdims=True))
    return dx, dy * xhat


def _rows(name, fn, ins, outs, tr, pre=None):
    rows = None
    for arr, kind, cols, cb in ins:
        if kind == 'r':
            rows = arr.shape[0]
        elif kind == 's' and rows is None:
            rows = arr.shape[1]
    tr = _t(tr, rows)
    n_in = len(ins)
    has_acc = any(o[0] == 'a' for o in outs)

    def spec(kind, cols, cb):
        if kind == 'r':
            return pl.BlockSpec((tr, cols), lambda r, *p: (r, cb))
        if kind == 'b':
            return pl.BlockSpec((1, cols), lambda r, *p: (0, cb))
        return pl.BlockSpec((None, tr, cols), lambda r, p: (p[cb], r, 0))

    in_specs = [spec(kind, cols, cb) for _, kind, cols, cb in ins]
    out_specs, out_shape = [], []
    for kind, cols, dt in outs:
        if kind == 'r':
            out_specs.append(pl.BlockSpec((tr, cols), lambda r, *p: (r, 0)))
            out_shape.append(jax.ShapeDtypeStruct((rows, cols), dt))
        else:
            out_specs.append(pl.BlockSpec((1, cols), lambda r, *p: (0, 0)))
            out_shape.append(jax.ShapeDtypeStruct((1, cols), dt))

    def body(*refs):
        if pre is not None:
            refs = refs[1:]
        res = fn(*[r[...] for r in refs[:n_in]])
        for (kind, cols, dt), o, v in zip(outs, refs[n_in:], res):
            if kind == 'r':
                o[...] = v.astype(dt)
            else:
                @pl.when(pl.program_id(0) == 0)
                def _():
                    o[...] = jnp.zeros_like(o)
                o[...] += v.astype(dt)

    grid_spec = pltpu.PrefetchScalarGridSpec(
        num_scalar_prefetch=0 if pre is None else 1, grid=(rows // tr,), in_specs=in_specs, out_specs=out_specs)
    args = [a[0] for a in ins]
    if pre is not None:
        args = [pre] + args
    return pl.pallas_call(body, name=name, grid_spec=grid_spec, out_shape=out_shape,
                          compiler_params=_params(("arbitrary" if has_acc else "parallel",)))(*args)


def _colsum(v):
    return jnp.sum(v, axis=0, keepdims=True)


def _mm(name, a, b, *, M, N, K, tm, tn, tk, a_spec, b_spec, outs, epi=None, extras=(), groups=1, ta=False, tb=False):
    nk = K // tk
    assert M % tm == 0 and N % tn == 0 and K % tk == 0, (name, M, N, K, tm, tn, tk)
    dims = (((0 if ta else 1,), (1 if tb else 0,)), ((), ()))
    n_ex = len(extras)

    def body(*refs):
        a_ref, b_ref = refs[0], refs[1]
        ex = refs[2:2 + n_ex]
        out_refs = refs[2 + n_ex:2 + n_ex + len(outs)]
        acc = refs[-1]
        k = pl.program_id(3)

        @pl.when(k == 0)
        def _():
            acc[...] = jnp.zeros_like(acc)

        acc[...] += lax.dot_general(a_ref[...].astype(BF16), b_ref[...].astype(BF16), dims, preferred_element_type=F32)

        @pl.when(k == nk - 1)
        def _():
            r = acc[...]
            res = (r,) if epi is None else epi(r, *[e[...] for e in ex])
            for o, v in zip(out_refs, res):
                o[...] = v.astype(o.dtype)

    return pl.pallas_call(
        body, name=name, grid=(groups, M // tm, N // tn, nk),
        in_specs=[a_spec, b_spec] + [s for _, s in extras],
        out_specs=[s for _, _, s in outs],
        out_shape=[jax.ShapeDtypeStruct(sh, dt) for sh, dt, _ in outs],
        scratch_shapes=[pltpu.VMEM((tm, tn), F32)],
        compiler_params=_params(("parallel", "parallel", "parallel", "arbitrary")),
    )(a, b, *[e for e, _ in extras])


def _bs(shape, f):
    return pl.BlockSpec(shape, f)


def _tile(tm, tn, coff=0):
    return _bs((tm, tn), lambda g, m, n, k: (m, n + coff))


def _rowvec(tn, coff=0):
    return _bs((1, tn), lambda g, m, n, k: (0, n + coff))


def _mm_proj(name, xn, w, j, *, epi=None, extras=(), out_dtype=F32):
    T, D = xn.shape
    sw = w.shape[3]
    N = N_CHIPS * sw
    tm, tn, tk = _t(512, T), _t(1024, sw), _t(512, D)
    nb = sw // tn
    return _mm(name, xn, w, M=T, N=N, K=D, tm=tm, tn=tn, tk=tk,
               a_spec=_bs((tm, tk), lambda g, m, n, k: (m, k)),
               b_spec=_bs((None, None, tk, tn), lambda g, m, n, k: (n // nb, j, k, n % nb)),
               outs=[((T, N), out_dtype, _tile(tm, tn))], epi=epi, extras=extras)[0]


def _mm_plain(name, a, b, *, out_dtype=F32, epi=None, extras=(), outs=None, tn_pref=1024):
    M, K = a.shape
    N = b.shape[1]
    tm, tn, tk = _t(512, M), _t(tn_pref, N), _t(512, K)
    if outs is None:
        outs = [((M, N), out_dtype, _tile(tm, tn))]
    return _mm(name, a, b, M=M, N=N, K=K, tm=tm, tn=tn, tk=tk,
               a_spec=_bs((tm, tk), lambda g, m, n, k: (m, k)),
               b_spec=_bs((tk, tn), lambda g, m, n, k: (k, n)),
               outs=outs, epi=epi, extras=extras)


def _mm_rowsharded(name, a, w, i, *, epi, extras, outs_fn):
    T, E = a.shape
    tk = w.shape[2]
    N = w.shape[3]
    tm, tn = _t(512, T), _t(1024, N)
    return _mm(name, a, w, M=T, N=N, K=E, tm=tm, tn=tn, tk=tk,
               a_spec=_bs((tm, tk), lambda g, m, n, k: (m, k)),
               b_spec=_bs((None, None, tk, tn), lambda g, m, n, k: (k, i, 0, n)),
               outs=outs_fn(tm, tn), epi=epi, extras=extras(tm, tn))


def _mm_rowsharded_t(name, d, w, i, *, epi, extras, outs_fn):
    T, N = d.shape
    tn = w.shape[2]
    E = N_CHIPS * tn
    tm, tk = _t(512, T), _t(512, N)
    return _mm(name, d, w, M=T, N=E, K=N, tm=tm, tn=tn, tk=tk, tb=True,
               a_spec=_bs((tm, tk), lambda g, m, n, k: (m, k)),
               b_spec=_bs((None, None, tn, tk), lambda g, m, n, k: (n, i, 0, k)),
               outs=outs_fn(tm, tn), epi=epi, extras=extras(tm, tn))


def _mm_colsharded_t(name, d, w, j):
    T, N = d.shape
    D, sw = w.shape[2], w.shape[3]
    tm, tn, tk = _t(512, T), _t(1024, D), _t(1024, sw)
    kb = sw // tk
    return _mm(name, d, w, M=T, N=D, K=N, tm=tm, tn=tn, tk=tk, tb=True,
               a_spec=_bs((tm, tk), lambda g, m, n, k: (m, k)),
               b_spec=_bs((None, None, tn, tk), lambda g, m, n, k: (k // kb, j, n, k % kb)),
               outs=[((T, D), F32, _tile(tm, tn))])[0]


def _mm_dw_rows(name, a, d):
    T, E = a.shape
    N = d.shape[1]
    tm, tn, tk = E // (2 * N_CHIPS), _t(1024, N), _t(512, T)
    return _mm(name, a, d, M=E, N=N, K=T, tm=tm, tn=tn, tk=tk, ta=True,
               a_spec=_bs((tk, tm), lambda g, m, n, k: (k, m)),
               b_spec=_bs((tk, tn), lambda g, m, n, k: (k, n)),
               outs=[((2, N_CHIPS, tm, N), BF16, _bs((None, None, tm, tn), lambda g, m, n, k: (m % 2, m // 2, 0, n)))])[0]


def _mm_dw_cols(name, xn, d):
    T, D = xn.shape
    N = d.shape[1]
    sw = N // N_CHIPS
    tm, tn, tk = _t(512, D // 2), _t(1024, sw), _t(512, T)
    mh, nb = (D // 2) // tm, sw // tn
    return _mm(name, xn, d, M=D, N=N, K=T, tm=tm, tn=tn, tk=tk, ta=True,
               a_spec=_bs((tk, tm), lambda g, m, n, k: (k, m)),
               b_spec=_bs((tk, tn), lambda g, m, n, k: (k, n)),
               outs=[((2, N_CHIPS, D // 2, sw), BF16,
                      _bs((None, None, tm, tn), lambda g, m, n, k: (m // mh, n // nb, m % mh, n % nb)))])[0]


def _norm_fwd(name, h, w):
    D = h.shape[1]
    return _rows(name, lambda x, g: ((x * _rms(x)) * g,), [(h, 'r', D, 0), (w, 'b', D, 0)], [('r', D, BF16)], 256)[0]


def _norm_bwd(name, dxn, h, w, dh):
    D = h.shape[1]

    def fn(dy, x, g, up):
        dx, dwt = _rms_bwd(x, g, dy)
        return up + dx, _colsum(dwt)

    return _rows(name, fn, [(dxn, 'r', D, 0), (h, 'r', D, 0), (w, 'b', D, 0), (dh, 'r', D, 0)],
                 [('r', D, F32), ('a', D, F32)], 256)


def _loss(h, target):
    D = h.shape[1]

    def fn(y, t):
        e = y - t
        return e * (1.0 / D), _colsum(e * e) * (0.5 / D)

    return _rows("loss", fn, [(h, 'r', D, 0), (target, 'r', D, 0)], [('r', D, F32), ('a', D, F32)], 256)


def _adamw(name, w, g, m, v):
    cols = w.shape[1]

    def fn(w, g, m, v):
        m = ADAM_B1 * m + (1.0 - ADAM_B1) * g
        v = ADAM_B2 * v + (1.0 - ADAM_B2) * (g * g)
        m_hat = m / (1.0 - ADAM_B1 ** ADAM_STEP)
        v_hat = v / (1.0 - ADAM_B2 ** ADAM_STEP)
        delta = -ADAM_LR * (m_hat / (jnp.sqrt(v_hat) + ADAM_EPS) + ADAM_WD * w)
        return delta, m, v

    return _rows(name, fn, [(x, 'r', cols, 0) for x in (w, g, m, v)], [('r', cols, F32)] * 3, 256)


def _s5_disc(a_re, a_im, log_dt):
    dt = jnp.exp(log_dt)
    mag = jnp.exp(a_re * dt)
    abar_r = mag * jnp.cos(a_im * dt)
    abar_i = mag * jnp.sin(a_im * dt)
    den = a_re * a_re + a_im * a_im
    xr = abar_r - 1.0
    fr = (xr * a_re + abar_i * a_im) / den
    fi = (abar_i * a_re - xr * a_im) / den
    return abar_r, abar_i, fr, fi


def _s5_disc_fwd(name, a_re, a_im, log_dt):
    G, P = a_re.shape

    def body(ar, ai, ld, o0, o1, o2, o3):
        for o, v in zip((o0, o1, o2, o3), _s5_disc(ar[...], ai[...], ld[...])):
            o[...] = v

    return pl.pallas_call(body, name=name, out_shape=[jax.ShapeDtypeStruct((G, P), F32)] * 4)(a_re, a_im, log_dt)


def _s5_disc_bwd(name, a_re, a_im, log_dt, cts):
    G, P = a_re.shape

    def body(ar, ai, ld, c0, c1, c2, c3, d0, d1, d2):
        _, vjp = jax.vjp(_s5_disc, ar[...], ai[...], ld[...])
        g0, g1, g2 = vjp((c0[...], c1[...], c2[...], c3[...]))
        d0[...] = g0
        d1[...] = g1
        d2[...] = g2

    return pl.pallas_call(body, name=name, out_shape=[jax.ShapeDtypeStruct((G, P), F32)] * 2 + [jax.ShapeDtypeStruct((G, 1), F32)])(
        a_re, a_im, log_dt, *cts)


def _s5_bbar(name, fr, fi, br, bi):
    return _rows(name, lambda fr, fi, br, bi: (fr * br - fi * bi, fr * bi + fi * br),
                 [(fr, 'r', 1, 0), (fi, 'r', 1, 0), (br, 'r', S5_GROUP, 0), (bi, 'r', S5_GROUP, 0)],
                 [('r', S5_GROUP, F32)] * 2, 2048)


def _s5_bbar_bwd(name, fr, fi, br, bi, dr, di):
    def fn(fr, fi, br, bi, dr, di):
        return (fr * dr + fi * di, fr * di - fi * dr,
                jnp.sum(br * dr + bi * di, axis=1, keepdims=True), jnp.sum(br * di - bi * dr, axis=1, keepdims=True))

    return _rows(name, fn, [(fr, 'r', 1, 0), (fi, 'r', 1, 0)] + [(x, 'r', S5_GROUP, 0) for x in (br, bi, dr, di)],
                 [('r', S5_GROUP, F32)] * 2 + [('r', 1, F32)] * 2, 2048)


def _scan_mults(m_ref, ar, ai, reverse):
    L = ar.shape[1]
    row = lax.broadcasted_iota(jnp.int32, (SUB, L), 0)
    if reverse:
        row = (SUB - 1) - row
    ar = jnp.broadcast_to(ar, (SUB, L))
    ai = jnp.broadcast_to(ai, (SUB, L))
    a2r, a2i = ar * ar - ai * ai, 2.0 * ar * ai
    a4r, a4i = a2r * a2r - a2i * a2i, 2.0 * a2r * a2i
    zero = jnp.zeros((SUB, L), F32)
    for s, (pr, pi, d) in enumerate(((ar, ai, 1), (a2r, a2i, 2), (a4r, a4i, 4))):
        m_ref[2 * s] = jnp.where(row >= d, pr, zero)
        m_ref[2 * s + 1] = jnp.where(row >= d, pi, zero)
    pr, pi = ar, ai
    for bit, (qr, qi) in ((1, (ar, ai)), (2, (a2r, a2i)), (4, (a4r, a4i))):
        on = (row & bit) != 0
        nr, ni = pr * qr - pi * qi, pr * qi + pi * qr
        pr, pi = jnp.where(on, nr, pr), jnp.where(on, ni, pi)
    m_ref[6] = pr
    m_ref[7] = pi


def _scan8(xr, xi, m_ref, cr, ci, reverse):
    for s, d in enumerate((1, 2, 4)):
        sh = (SUB - d) if reverse else d
        sr, si = pltpu.roll(xr, sh, 0), pltpu.roll(xi, sh, 0)
        mr, mi = m_ref[2 * s], m_ref[2 * s + 1]
        xr, xi = xr + mr * sr - mi * si, xi + mr * si + mi * sr
    pr, pi = m_ref[6], m_ref[7]
    return xr + pr * cr - pi * ci, xi + pr * ci + pi * cr


def _s5_fwd(name, proj, bbd, cbd, abar_r, abar_i, dskip, E):
    T = proj.shape[0]
    NC, CH, L2 = bbd.shape
    L = L2 // 2
    tT = _t(256, T)

    def body(u_ref, b_ref, c_ref, ar_ref, ai_ref, d_ref, y_ref, g_ref, h_ref, bu, carry, mult):
        tb = pl.program_id(1)

        @pl.when(tb == 0)
        def _():
            carry[...] = jnp.zeros_like(carry)

        u = u_ref[...]
        bu[...] = jnp.dot(u.astype(BF16), b_ref[...], preferred_element_type=F32)
        _scan_mults(mult, ar_ref[...], ai_ref[...], False)

        def step(jb, c):
            cr, ci = c
            r0 = pl.multiple_of(jb * SUB, SUB)
            hr, hi = _scan8(bu[pl.ds(r0, SUB), 0:L], bu[pl.ds(r0, SUB), L:L2], mult, cr, ci, False)
            h_ref[pl.ds(r0, SUB), 0:L] = hr
            h_ref[pl.ds(r0, SUB), L:L2] = hi
            return (jnp.broadcast_to(hr[SUB - 1:SUB, :], (SUB, L)), jnp.broadcast_to(hi[SUB - 1:SUB, :], (SUB, L)))

        cr, ci = lax.fori_loop(0, tT // SUB, step, (carry[:, 0:L], carry[:, L:L2]))
        carry[:, 0:L] = cr
        carry[:, L:L2] = ci
        y1 = jnp.dot(h_ref[...].astype(BF16), c_ref[...], preferred_element_type=F32) + d_ref[...] * u
        y_ref[...] = y1
        g_ref[...] = _gelu(y1).astype(BF16)

    return pl.pallas_call(
        body, name=name, grid=(NC, T // tT),
        in_specs=[_bs((tT, CH), lambda c, t: (t, c)), _bs((None, CH, L2), lambda c, t: (c, 0, 0)),
                  _bs((None, L2, CH), lambda c, t: (c, 0, 0)), _bs((None, 1, L), lambda c, t: (c, 0, 0)),
                  _bs((None, 1, L), lambda c, t: (c, 0, 0)), _bs((1, CH), lambda c, t: (0, c))],
        out_specs=[_bs((tT, CH), lambda c, t: (t, c)), _bs((tT, CH), lambda c, t: (t, c)),
                   _bs((None, tT, L2), lambda c, t: (c, t, 0))],
        out_shape=[jax.ShapeDtypeStruct((T, E), F32), jax.ShapeDtypeStruct((T, E), BF16),
                   jax.ShapeDtypeStruct((NC, T, L2), F32)],
        scratch_shapes=[pltpu.VMEM((tT, L2), F32), pltpu.VMEM((SUB, L2), F32), pltpu.VMEM((8, SUB, L), F32)],
        compiler_params=_params(("parallel", "arbitrary")),
    )(proj, bbd, cbd, abar_r, abar_i, dskip)


def _s5_bwd(name, dy1, proj, hs, bbd, cbd, abar_r, abar_i, dskip, E):
    T = proj.shape[0]
    NC, CH, L2 = bbd.shape
    L = L2 // 2
    tT = _t(256, T)
    nT = T // tT
    tn = (((0,), (0,)), ((), ()))
    nt = (((1,), (1,)), ((), ()))

    def body(dy_ref, u_ref, h_ref, b_ref, c_ref, ar_ref, ai_ref, d_ref, du_ref, db_ref, dc_ref, da_ref, dd_ref, gb, carry, mult):
        tb = pl.program_id(1)

        @pl.when(tb == 0)
        def _():
            carry[...] = jnp.zeros_like(carry)
            db_ref[...] = jnp.zeros_like(db_ref)
            dc_ref[...] = jnp.zeros_like(dc_ref)
            da_ref[...] = jnp.zeros_like(da_ref)
            dd_ref[...] = jnp.zeros_like(dd_ref)

        dy = dy_ref[...]
        u = u_ref[...]
        dy16 = dy.astype(BF16)
        dc_ref[...] += lax.dot_general(h_ref[...].astype(BF16), dy16, tn, preferred_element_type=F32)
        gb[...] = lax.dot_general(dy16, c_ref[...], nt, preferred_element_type=F32)
        _scan_mults(mult, ar_ref[...], -ai_ref[...], True)
        row = lax.broadcasted_iota(jnp.int32, (SUB, L), 0)
        nblk = tT // SUB

        def step(jj, c):
            cr, ci, sr, si = c
            r0 = pl.multiple_of((nblk - 1 - jj) * SUB, SUB)
            gr, gi = _scan8(gb[pl.ds(r0, SUB), 0:L], gb[pl.ds(r0, SUB), L:L2], mult, cr, ci, True)
            gb[pl.ds(r0, SUB), 0:L] = gr
            gb[pl.ds(r0, SUB), L:L2] = gi
            nr = jnp.where(row == SUB - 1, cr, pltpu.roll(gr, SUB - 1, 0))
            ni = jnp.where(row == SUB - 1, ci, pltpu.roll(gi, SUB - 1, 0))
            hr, hi = h_ref[pl.ds(r0, SUB), 0:L], h_ref[pl.ds(r0, SUB), L:L2]
            sr = sr + nr * hr + ni * hi
            si = si + ni * hr - nr * hi
            return (jnp.broadcast_to(gr[0:1, :], (SUB, L)), jnp.broadcast_to(gi[0:1, :], (SUB, L)), sr, si)

        z = jnp.zeros((SUB, L), F32)
        cr, ci, sr, si = lax.fori_loop(0, nblk, step, (carry[:, 0:L], carry[:, L:L2], z, z))
        carry[:, 0:L] = cr
        carry[:, L:L2] = ci
        da_ref[:, 0:L] += sr
        da_ref[:, L:L2] += si
        g16 = gb[...].astype(BF16)
        du = lax.dot_general(g16, b_ref[...], nt, preferred_element_type=F32) + d_ref[...] * dy
        du_ref[...] = du.astype(BF16)
        db_ref[...] += lax.dot_general(u.astype(BF16), g16, tn, preferred_element_type=F32)
        dd_ref[...] += _colsum(dy * u)

    rev = lambda c, t: (nT - 1 - t, c)
    return pl.pallas_call(
        body, name=name, grid=(NC, nT),
        in_specs=[_bs((tT, CH), rev), _bs((tT, CH), rev), _bs((None, tT, L2), lambda c, t: (c, nT - 1 - t, 0)),
                  _bs((None, CH, L2), lambda c, t: (c, 0, 0)), _bs((None, L2, CH), lambda c, t: (c, 0, 0)),
                  _bs((None, 1, L), lambda c, t: (c, 0, 0)), _bs((None, 1, L), lambda c, t: (c, 0, 0)),
                  _bs((1, CH), lambda c, t: (0, c))],
        out_specs=[_bs((tT, CH), rev), _bs((None, CH, L2), lambda c, t: (c, 0, 0)), _bs((None, L2, CH), lambda c, t: (c, 0, 0)),
                   _bs((None, SUB, L2), lambda c, t: (c, 0, 0)), _bs((None, 1, CH), lambda c, t: (c, 0, 0))],
        out_shape=[jax.ShapeDtypeStruct((T, E), BF16), jax.ShapeDtypeStruct((NC, CH, L2), F32),
                   jax.ShapeDtypeStruct((NC, L2, CH), F32), jax.ShapeDtypeStruct((NC, SUB, L2), F32),
                   jax.ShapeDtypeStruct((NC, 1, CH), F32)],
        scratch_shapes=[pltpu.VMEM((tT, L2), F32), pltpu.VMEM((SUB, L2), F32), pltpu.VMEM((8, SUB, L), F32)],
        compiler_params=_params(("parallel", "arbitrary")),
    )(dy1, proj, hs, bbd, cbd, abar_r, abar_i, dskip)


def _blockdiag(x, NC):
    G, a, b = x.shape
    gpc = G // NC
    eye = jnp.eye(gpc, dtype=x.dtype)
    return jnp.einsum('ngab,gh->ngahb', x.reshape(NC, gpc, a, b), eye).reshape(NC, gpc * a, gpc * b)


def _blockdiag_extract(d, G):
    NC, A, B = d.shape
    gpc = G // NC
    a, b = A // gpc, B // gpc
    d5 = d.reshape(NC, gpc, a, gpc, b)
    eye = jnp.eye(gpc, dtype=d.dtype)
    return jnp.einsum('ngahb,gh->ngab', d5, eye).reshape(G, a, b)


def _cum_rows(name, x, bias, reverse, log_sig):
    T, L = x.shape

    def body(x_ref, b_ref, o_ref):
        row = lax.broadcasted_iota(jnp.int32, (SUB, L), 0)
        if reverse:
            row = (SUB - 1) - row
        nblk = T // SUB

        def step(jj, c):
            r0 = pl.multiple_of(((nblk - 1 - jj) if reverse else jj) * SUB, SUB)
            v = x_ref[pl.ds(r0, SUB), :] + b_ref[...]
            if log_sig:
                v = _log_sigmoid(v)
            for d in (1, 2, 4):
                v = v + jnp.where(row >= d, pltpu.roll(v, (SUB - d) if reverse else d, 0), 0.0)
            v = v + c
            o_ref[pl.ds(r0, SUB), :] = v
            e = 0 if reverse else SUB - 1
            return jnp.broadcast_to(v[e:e + 1, :], (SUB, L))

        lax.fori_loop(0, nblk, step, jnp.zeros((SUB, L), F32))

    return pl.pallas_call(body, name=name, out_shape=jax.ShapeDtypeStruct((T, L), F32),
                          compiler_params=pltpu.CompilerParams(vmem_limit_bytes=VMEM_LIMIT))(x, bias)


def _qk_norm(name, proj, wq, wk, H):
    T = proj.shape[0]
    Dh = FOX_HEAD_DIM
    tT = _t(512, T)

    def body(q_ref, k_ref, wq_ref, wk_ref, qn_ref, kn_ref):
        q, k = q_ref[...], k_ref[...]
        qn_ref[...] = ((q * _rms(q)) * wq_ref[...]).astype(BF16)
        kn_ref[...] = ((k * _rms(k)) * wk_ref[...]).astype(BF16)

    blk = lambda off: _bs((tT, Dh), lambda t, h: (t, h + off))
    return pl.pallas_call(
        body, name=name, grid=(T // tT, H),
        in_specs=[blk(0), blk(H), _bs((1, Dh), lambda t, h: (0, 0)), _bs((1, Dh), lambda t, h: (0, 0))],
        out_specs=[blk(0), blk(0)], out_shape=[jax.ShapeDtypeStruct((T, H * Dh), BF16)] * 2,
        compiler_params=_params(("parallel", "parallel")))(proj, proj, wq, wk)


def _qk_norm_bwd(name, proj, wq, wk, dqn, dkn, H):
    T = proj.shape[0]
    Dh = FOX_HEAD_DIM
    tT = _t(512, T)

    def body(q_ref, k_ref, wq_ref, wk_ref, dqn_ref, dkn_ref, dq_ref, dk_ref, dwq_ref, dwk_ref):
        @pl.when((pl.program_id(0) == 0) & (pl.program_id(1) == 0))
        def _():
            dwq_ref[...] = jnp.zeros_like(dwq_ref)
            dwk_ref[...] = jnp.zeros_like(dwk_ref)

        dq, tq = _rms_bwd(q_ref[...], wq_ref[...], dqn_ref[...])
        dk, tk = _rms_bwd(k_ref[...], wk_ref[...], dkn_ref[...])
        dq_ref[...] = dq.astype(BF16)
        dk_ref[...] = dk.astype(BF16)
        dwq_ref[...] += _colsum(tq)
        dwk_ref[...] += _colsum(tk)

    blk = lambda off: _bs((tT, Dh), lambda t, h: (t, h + off))
    one = _bs((1, Dh), lambda t, h: (0, 0))
    return pl.pallas_call(
        body, name=name, grid=(T // tT, H),
        in_specs=[blk(0), blk(H), one, one, blk(0), blk(0)],
        out_specs=[blk(0), blk(0), one, one],
        out_shape=[jax.ShapeDtypeStruct((T, H * Dh), BF16)] * 2 + [jax.ShapeDtypeStruct((1, Dh), F32)] * 2,
        compiler_params=_params(("arbitrary", "arbitrary")))(proj, proj, wq, wk, dqn, dkn)


def _attn_fwd(name, qn, kn, proj, cum_q, cum_k, H):
    T = qn.shape[0]
    Dh = FOX_HEAD_DIM
    tq = cum_k.shape[3]
    nq = T // tq
    scale = Dh ** -0.5
    nt = (((1,), (1,)), ((), ()))

    def body(q_ref, k_ref, v_ref, cq_ref, ck_ref, o_ref, lse_ref):
        i = pl.program_id(1)
        q = q_ref[...]
        cq = cq_ref[:, 0:1]
        qpos = i * tq + lax.broadcasted_iota(jnp.int32, (tq, tq), 0)
        kloc = lax.broadcasted_iota(jnp.int32, (tq, tq), 1)

        def chunk(kc, c):
            m, l, acc = c
            ks = pl.multiple_of(kc * tq, tq)
            s = lax.dot_general(q, k_ref[pl.ds(ks, tq), :], nt, preferred_element_type=F32) * scale + (cq - ck_ref[kc])
            s = jnp.where(ks + kloc <= qpos, s, NEG)
            m_new = jnp.maximum(m, jnp.max(s, axis=1, keepdims=True))
            alpha = jnp.exp(m - m_new)
            p = jnp.exp(s - m_new)
            l = alpha * l + jnp.sum(p, axis=1, keepdims=True)
            acc = alpha * acc + jnp.dot(p.astype(BF16), v_ref[pl.ds(ks, tq), :].astype(BF16), preferred_element_type=F32)
            return m_new, l, acc

        m, l, acc = lax.fori_loop(0, i + 1, chunk, (jnp.full((tq, 1), NEG, F32), jnp.zeros((tq, 1), F32), jnp.zeros((tq, Dh), F32)))
        o_ref[...] = acc / l
        lse_ref[...] = jnp.broadcast_to(m + jnp.log(l), (tq, LANES))

    return pl.pallas_call(
        body, name=name, grid=(H, nq),
        in_specs=[_bs((tq, Dh), lambda h, i: (i, h)), _bs((T, Dh), lambda h, i: (0, h)), _bs((T, Dh), lambda h, i: (0, 2 * H + h)),
                  _bs((None, tq, LANES), lambda h, i: (h, i, 0)), _bs((None, nq, 1, tq), lambda h, i: (h, 0, 0, 0))],
        out_specs=[_bs((tq, Dh), lambda h, i: (i, h)), _bs((None, tq, LANES), lambda h, i: (h, i, 0))],
        out_shape=[jax.ShapeDtypeStruct((T, H * Dh), F32), jax.ShapeDtypeStruct((H, T, LANES), F32)],
        compiler_params=_params(("parallel", "parallel")))(qn, kn, proj, cum_q, cum_k)


def _attn_bwd(name, qn, kn, proj, do, o, lse, cum_q, cum_k, H):
    T = qn.shape[0]
    Dh = FOX_HEAD_DIM
    tq = cum_k.shape[3]
    nq = T // tq
    scale = Dh ** -0.5
    nt = (((1,), (1,)), ((), ()))
    tn = (((0,), (0,)), ((), ()))

    def body(q_ref, k_ref, v_ref, do_ref, o_ref, lse_ref, cq_ref, ck_ref, dq_ref, dk_ref, dv_ref, dcq_ref, dck_ref, delta):
        j = pl.program_id(1)

        @pl.when(j == 0)
        def _():
            dq_ref[...] = jnp.zeros_like(dq_ref)
            dcq_ref[...] = jnp.zeros_like(dcq_ref)
            delta[...] = jnp.sum(do_ref[...] * o_ref[...], axis=1, keepdims=True)

        k = k_ref[...]
        v = v_ref[...].astype(BF16)
        ck = ck_ref[...]
        kpos = j * tq + lax.broadcasted_iota(jnp.int32, (tq, tq), 1)
        qloc = lax.broadcasted_iota(jnp.int32, (tq, tq), 0)

        def qblk(i, c):
            dk, dv, dck = c
            qs = pl.multiple_of(i * tq, tq)
            q = q_ref[pl.ds(qs, tq), :]
            do16 = do_ref[pl.ds(qs, tq), :].astype(BF16)
            s = lax.dot_general(q, k, nt, preferred_element_type=F32) * scale + (cq_ref[pl.ds(qs, tq), 0:1] - ck)
            p = jnp.where(kpos <= qs + qloc, jnp.exp(s - lse_ref[pl.ds(qs, tq), 0:1]), 0.0)
            dv = dv + lax.dot_general(p.astype(BF16), do16, tn, preferred_element_type=F32)
            dp = lax.dot_general(do16, v, nt, preferred_element_type=F32)
            ds = p * (dp - delta[pl.ds(qs, tq), :])
            ds16 = ds.astype(BF16)
            dk = dk + lax.dot_general(ds16, q, tn, preferred_element_type=F32) * scale
            dq_ref[pl.ds(qs, tq), :] += jnp.dot(ds16, k, preferred_element_type=F32) * scale
            dcq_ref[pl.ds(qs, tq), :] += jnp.broadcast_to(jnp.sum(ds, axis=1, keepdims=True), (tq, LANES))
            return dk, dv, dck + jnp.sum(ds, axis=0, keepdims=True)

        dk, dv, dck = lax.fori_loop(j, nq, qblk, (jnp.zeros((tq, Dh), F32), jnp.zeros((tq, Dh), F32), jnp.zeros((1, tq), F32)))
        dk_ref[...] = dk
        dv_ref[...] = dv.astype(BF16)
        dck_ref[...] = -dck

    whole = lambda off: _bs((T, Dh), lambda h, j: (0, h + off))
    blk = lambda off: _bs((tq, Dh), lambda h, j: (j, h + off))
    return pl.pallas_call(
        body, name=name, grid=(H, nq),
        in_specs=[whole(0), blk(0), blk(2 * H), whole(0), whole(0), _bs((None, T, LANES), lambda h, j: (h, 0, 0)),
                  _bs((None, T, LANES), lambda h, j: (h, 0, 0)), _bs((None, None, 1, tq), lambda h, j: (h, j, 0, 0))],
        out_specs=[whole(0), blk(0), blk(0), _bs((None, T, LANES), lambda h, j: (h, 0, 0)),
                   _bs((None, None, 1, tq), lambda h, j: (h, j, 0, 0))],
        out_shape=[jax.ShapeDtypeStruct((T, H * Dh), F32), jax.ShapeDtypeStruct((T, H * Dh), F32), jax.ShapeDtypeStruct((T, H * Dh), BF16),
                   jax.ShapeDtypeStruct((H, T, LANES), F32), jax.ShapeDtypeStruct((H, nq, 1, tq), F32)],
        scratch_shapes=[pltpu.VMEM((T, 1), F32)],
        compiler_params=_params(("parallel", "arbitrary")))(qn, kn, proj, do, o, lse, cum_q, cum_k)


def _pool_fwd(name, proj, E):
    T = proj.shape[0]
    PG = len(POOL_WINDOWS)
    PD = E // PG
    tT = _t(256, T)
    hb = tT // POOL_HALO

    def body(u_ref, halo_ref, o_ref, buf):
        g, tb = pl.program_id(0), pl.program_id(1)
        u = u_ref[...]
        buf[pl.ds(POOL_HALO, tT), :] = u
        buf[pl.ds(0, POOL_HALO), :] = jnp.where(tb == 0, 0.0, halo_ref[...])
        t = tb * tT + lax.broadcasted_iota(jnp.int32, (tT, 1), 0)
        for gi, w in enumerate(POOL_WINDOWS):
            @pl.when(g == gi)
            def _():
                acc = u
                for d in range(1, w):
                    acc = acc + buf[pl.ds(POOL_HALO - d, tT), :]
                cnt = jnp.minimum(t + 1, w).astype(F32)
                o_ref[...] = (acc / cnt - u).astype(BF16)

    return pl.pallas_call(
        body, name=name, grid=(PG, T // tT),
        in_specs=[_bs((tT, PD), lambda g, t: (t, g)), _bs((POOL_HALO, PD), lambda g, t: (jnp.maximum(t * hb - 1, 0), g))],
        out_specs=_bs((tT, PD), lambda g, t: (t, g)), out_shape=jax.ShapeDtypeStruct((T, E), BF16),
        scratch_shapes=[pltpu.VMEM((tT + POOL_HALO, PD), F32)],
        compiler_params=_params(("parallel", "parallel")))(proj, proj)


def _pool_bwd(name, dpm, E):
    T = dpm.shape[0]
    PG = len(POOL_WINDOWS)
    PD = E // PG
    tT = _t(256, T)
    hb = tT // POOL_HALO
    nT = T // tT

    def body(d_ref, halo_ref, o_ref, buf):
        g, tb = pl.program_id(0), pl.program_id(1)
        d = d_ref[...]
        t = tb * tT + lax.broadcasted_iota(jnp.int32, (tT, 1), 0)
        th = (tb + 1) * tT + lax.broadcasted_iota(jnp.int32, (POOL_HALO, 1), 0)
        for gi, w in enumerate(POOL_WINDOWS):
            @pl.when(g == gi)
            def _():
                dn = d / jnp.minimum(t + 1, w).astype(F32)
                buf[pl.ds(0, tT), :] = dn
                buf[pl.ds(tT, POOL_HALO), :] = jnp.where(tb == nT - 1, 0.0, halo_ref[...] / jnp.minimum(th + 1, w).astype(F32))
                acc = dn
                for s in range(1, w):
                    acc = acc + buf[pl.ds(s, tT), :]
                o_ref[...] = (acc - d).astype(BF16)

    return pl.pallas_call(
        body, name=name, grid=(PG, nT),
        in_specs=[_bs((tT, PD), lambda g, t: (t, g)), _bs((POOL_HALO, PD), lambda g, t: (jnp.minimum((t + 1) * hb, T // POOL_HALO - 1), g))],
        out_specs=_bs((tT, PD), lambda g, t: (t, g)), out_shape=jax.ShapeDtypeStruct((T, E), BF16),
        scratch_shapes=[pltpu.VMEM((tT + POOL_HALO, PD), F32)],
        compiler_params=_params(("parallel", "parallel")))(dpm, dpm)


def _coords():
    x, y, c = lax.axis_index("x"), lax.axis_index("y"), lax.axis_index("c")
    chips = [(1 - x, y), (x, 1 - y), (1 - x, 1 - y)]
    return x, y, c, 2 * x + y, (x, y, 1 - c), chips


ANY = pl.BlockSpec(memory_space=pl.ANY)


def _chip_allgather(name, arrs):
    n = len(arrs)

    def body(*refs):
        ins, outs = refs[:n], refs[n:2 * n]
        send, recv, fsend, frecv, lsem = refs[2 * n:]
        x, y, c, p, sib, chips = _coords()
        local = [pltpu.make_async_copy(ins[t], outs[t].at[p], lsem.at[t]) for t in range(n)]
        for cp in local:
            cp.start()

        def direct(t, j, chip):
            return pltpu.make_async_remote_copy(src_ref=ins[t].at[c], dst_ref=outs[t].at[p, c], send_sem=send.at[t, j],
                                                recv_sem=recv.at[t, j], device_id=(*chip, c), device_id_type=MESH)

        def landed(t, j, chip):
            return pltpu.make_async_remote_copy(src_ref=ins[t].at[c], dst_ref=outs[t].at[2 * chip[0] + chip[1], c], send_sem=send.at[t, j],
                                                recv_sem=recv.at[t, j], device_id=(*chip, c), device_id_type=MESH)

        def passed(t, j, chip, half):
            blk = outs[t].at[2 * chip[0] + chip[1], half]
            return pltpu.make_async_remote_copy(src_ref=blk, dst_ref=blk, send_sem=fsend.at[t, j], recv_sem=frecv.at[t, j],
                                                device_id=sib, device_id_type=MESH)

        first = [direct(t, j, chip) for t in range(n) for j, chip in enumerate(chips)]
        for cp in first:
            cp.start()
        fwd = []
        for j, chip in enumerate(chips):
            for t in range(n):
                landed(t, j, chip).wait_recv()
                f = passed(t, j, chip, c)
                f.start()
                fwd.append(f)
        for j, chip in enumerate(chips):
            for t in range(n):
                passed(t, j, chip, 1 - c).wait_recv()
        for cp in first + fwd:
            cp.wait_send()
        for cp in local:
            cp.wait()

    return pl.pallas_call(
        body, name=name, in_specs=[ANY] * n, out_specs=[ANY] * n,
        out_shape=[jax.ShapeDtypeStruct((N_CHIPS,) + a.shape, a.dtype) for a in arrs],
        scratch_shapes=[pltpu.SemaphoreType.DMA((n, 3))] * 4 + [pltpu.SemaphoreType.DMA((n,))],
    )(*arrs)


def _pair_exchange(name, parts):
    n = len(parts)

    def body(*refs):
        ins, outs = refs[:n], refs[n:2 * n]
        send, recv = refs[2 * n:]
        x, y, c, p, sib, chips = _coords()
        cps = [pltpu.make_async_remote_copy(src_ref=ins[t].at[1 - c], dst_ref=outs[t], send_sem=send.at[t], recv_sem=recv.at[t],
                                            device_id=sib, device_id_type=MESH) for t in range(n)]
        for cp in cps:
            cp.start()
        for cp in cps:
            cp.wait()

    return pl.pallas_call(
        body, name=name, in_specs=[ANY] * n, out_specs=[ANY] * n,
        out_shape=[jax.ShapeDtypeStruct(a.shape[1:], a.dtype) for a in parts],
        scratch_shapes=[pltpu.SemaphoreType.DMA((n,))] * 2,
    )(*parts)


def _chip_exchange(name, sums):
    n = len(sums)

    def body(*refs):
        ins, outs = refs[:n], refs[n:2 * n]
        send, recv = refs[2 * n:]
        x, y, c, p, sib, chips = _coords()
        cps = [pltpu.make_async_remote_copy(src_ref=ins[t].at[2 * chip[0] + chip[1]], dst_ref=outs[t].at[j], send_sem=send.at[t, j],
                                            recv_sem=recv.at[t, j], device_id=(*chip, c), device_id_type=MESH)
               for t in range(n) for j, chip in enumerate(chips)]
        for cp in cps:
            cp.start()
        for cp in cps:
            cp.wait()

    return pl.pallas_call(
        body, name=name, in_specs=[ANY] * n, out_specs=[ANY] * n,
        out_shape=[jax.ShapeDtypeStruct((3,) + a.shape[1:], a.dtype) for a in sums],
        scratch_shapes=[pltpu.SemaphoreType.DMA((n, 3))] * 2,
    )(*sums)


def _pair_share(name, halves, dests, out_shapes):
    n = len(halves)
    no = len(out_shapes)

    def body(*refs):
        ins, outs = refs[:n], refs[n:n + no]
        send, recv, lsem = refs[n + no:]
        x, y, c, p, sib, chips = _coords()

        def dst(t, half):
            o, pre = dests[t]
            return outs[o].at[half] if pre is None else outs[o].at[pre, half]

        local = [pltpu.make_async_copy(ins[t], dst(t, c), lsem.at[t]) for t in range(n)]
        cps = [pltpu.make_async_remote_copy(src_ref=ins[t], dst_ref=dst(t, c), send_sem=send.at[t], recv_sem=recv.at[t],
                                            device_id=sib, device_id_type=MESH) for t in range(n)]
        for cp in local + cps:
            cp.start()
        for t in range(n):
            pltpu.make_async_remote_copy(src_ref=ins[t], dst_ref=dst(t, 1 - c), send_sem=send.at[t], recv_sem=recv.at[t],
                                         device_id=sib, device_id_type=MESH).wait_recv()
        for cp in cps:
            cp.wait_send()
        for cp in local:
            cp.wait()

    return pl.pallas_call(
        body, name=name, in_specs=[ANY] * n, out_specs=[ANY] * no,
        out_shape=[jax.ShapeDtypeStruct(s, F32) for s in out_shapes],
        scratch_shapes=[pltpu.SemaphoreType.DMA((n,))] * 3,
    )(*halves)


def _flat2(a, lead):
    return a.reshape(a.shape[:lead] + (-1, a.shape[-1]))


def _reduce_scatter(parts, dests, out_shapes):
    c = lax.axis_index("c").astype(jnp.int32)
    p = (2 * lax.axis_index("x") + lax.axis_index("y")).astype(jnp.int32)
    got = _pair_exchange("rs_pair_exchange", parts)
    sums = []
    for t, (mine, theirs) in enumerate(zip(parts, got)):
        m3, t2 = _flat2(mine, 1), theirs.reshape(-1, theirs.shape[-1])
        m3 = m3.reshape(2, -1, m3.shape[-1])
        cols = t2.shape[1]
        s = _rows(f"rs_pair_sum_{t}", lambda a, b: (a.astype(F32) + b.astype(F32),),
                  [(m3, 's', cols, 0), (t2, 'r', cols, 0)], [('r', cols, BF16)], 512, pre=c.reshape(1))[0]
        sums.append(s.reshape(theirs.shape))
    got = _chip_exchange("rs_chip_exchange", sums)
    halves = []
    for t, (mine, theirs) in enumerate(zip(sums, got)):
        m3 = _flat2(mine, 1)
        t3 = _flat2(theirs, 1)
        cols = m3.shape[-1]
        pre = jnp.stack([p, jnp.int32(0), jnp.int32(1), jnp.int32(2)])
        r = _rows(f"rs_chip_sum_{t}", lambda a, b0, b1, b2: (((a.astype(F32) + b0.astype(F32)) + b1.astype(F32)) + b2.astype(F32),),
                  [(m3, 's', cols, 0), (t3, 's', cols, 1), (t3, 's', cols, 2), (t3, 's', cols, 3)], [('r', cols, F32)], 512, pre=pre)[0]
        halves.append(r.reshape(mine.shape[1:]))
    return _pair_share("rs_pair_share", halves, dests, out_shapes)


def kernel(x, norm_w, out_proj, s5_in_proj, s5_a_re, s5_a_im, s5_log_dt, s5_b_re, s5_b_im, s5_c_re, s5_c_im, s5_d, s5_w_glu, s5_b_glu, fox_in_proj, fox_q_norm, fox_k_norm, fox_f_bias, pool_in_proj, pool_w_group, pool_scale, loss_target, m_norm_w, m_out_proj, m_s5_in_proj, m_s5_a_re, m_s5_a_im, m_s5_log_dt, m_s5_b_re, m_s5_b_im, m_s5_c_re, m_s5_c_im, m_s5_d, m_s5_w_glu, m_s5_b_glu, m_fox_in_proj, m_fox_q_norm, m_fox_k_norm, m_fox_f_bias, m_pool_in_proj, m_pool_w_group, m_pool_scale, v_norm_w, v_out_proj, v_s5_in_proj, v_s5_a_re, v_s5_a_im, v_s5_log_dt, v_s5_b_re, v_s5_b_im, v_s5_c_re, v_s5_c_im, v_s5_d, v_s5_w_glu, v_s5_b_glu, v_fox_in_proj, v_fox_q_norm, v_fox_k_norm, v_fox_f_bias, v_pool_in_proj, v_pool_w_group, v_pool_scale):
    weights = dict(norm_w=norm_w, out_proj=out_proj, s5_in_proj=s5_in_proj, s5_a_re=s5_a_re, s5_a_im=s5_a_im, s5_log_dt=s5_log_dt,
                   s5_b_re=s5_b_re, s5_b_im=s5_b_im, s5_c_re=s5_c_re, s5_c_im=s5_c_im, s5_d=s5_d, s5_w_glu=s5_w_glu, s5_b_glu=s5_b_glu,
                   fox_in_proj=fox_in_proj, fox_q_norm=fox_q_norm, fox_k_norm=fox_k_norm, fox_f_bias=fox_f_bias,
                   pool_in_proj=pool_in_proj, pool_w_group=pool_w_group, pool_scale=pool_scale)
    mom_m = dict(norm_w=m_norm_w, out_proj=m_out_proj, s5_in_proj=m_s5_in_proj, s5_a_re=m_s5_a_re, s5_a_im=m_s5_a_im, s5_log_dt=m_s5_log_dt,
                 s5_b_re=m_s5_b_re, s5_b_im=m_s5_b_im, s5_c_re=m_s5_c_re, s5_c_im=m_s5_c_im, s5_d=m_s5_d, s5_w_glu=m_s5_w_glu, s5_b_glu=m_s5_b_glu,
                 fox_in_proj=m_fox_in_proj, fox_q_norm=m_fox_q_norm, fox_k_norm=m_fox_k_norm, fox_f_bias=m_fox_f_bias,
                 pool_in_proj=m_pool_in_proj, pool_w_group=m_pool_w_group, pool_scale=m_pool_scale)
    mom_v = dict(norm_w=v_norm_w, out_proj=v_out_proj, s5_in_proj=v_s5_in_proj, s5_a_re=v_s5_a_re, s5_a_im=v_s5_a_im, s5_log_dt=v_s5_log_dt,
                 s5_b_re=v_s5_b_re, s5_b_im=v_s5_b_im, s5_c_re=v_s5_c_re, s5_c_im=v_s5_c_im, s5_d=v_s5_d, s5_w_glu=v_s5_w_glu, s5_b_glu=v_s5_b_glu,
                 fox_in_proj=v_fox_in_proj, fox_q_norm=v_fox_q_norm, fox_k_norm=v_fox_k_norm, fox_f_bias=v_fox_f_bias,
                 pool_in_proj=v_pool_in_proj, pool_w_group=v_pool_w_group, pool_scale=v_pool_scale)
    return _step(x, loss_target, weights, mom_m, mom_v)


BIG = ('out_proj', 's5_in_proj', 's5_w_glu', 'fox_in_proj', 'pool_in_proj', 'pool_w_group')
SMALL = ('norm_w', 's5_a_re', 's5_a_im', 's5_log_dt', 's5_b_re', 's5_b_im', 's5_c_re', 's5_c_im', 's5_d', 's5_b_glu',
         'fox_q_norm', 'fox_k_norm', 'fox_f_bias', 'pool_scale')
SMALL_SHARDED = ('s5_d', 's5_b_glu', 'pool_scale')
ORDER = ('norm_w', 'out_proj', 's5_in_proj', 's5_a_re', 's5_a_im', 's5_log_dt', 's5_b_re', 's5_b_im', 's5_c_re', 's5_c_im', 's5_d',
         's5_w_glu', 's5_b_glu', 'fox_in_proj', 'fox_q_norm', 'fox_k_norm', 'fox_f_bias', 'pool_in_proj', 'pool_w_group', 'pool_scale')


def _split2(a):
    if a.shape[0] % 2 == 0:
        return a.reshape((2, a.shape[0] // 2) + a.shape[1:])
    assert a.shape[0] == 1 and a.shape[1] % 2 == 0
    return a.reshape((2, a.shape[1] // 2) + a.shape[2:])


def _gather_weights(w):
    halves = []
    for n in BIG:
        a = w[n]
        a2 = a.reshape(-1, a.shape[-1])
        cols = a2.shape[1]
        b = _rows(f"cast_{n}", lambda v: (v,), [(a2, 'r', cols, 0)], [('r', cols, BF16)], 256)[0]
        halves.append(_split2(b.reshape(a.shape)))
    got = _chip_allgather("gather_weights", halves)
    full = {n: g.reshape((N_CHIPS,) + w[n].shape) for n, g in zip(BIG, got)}
    fox = full['fox_in_proj']
    D = fox.shape[2]
    fox = jnp.transpose(fox[:, 0], (1, 0, 2)).reshape(D, -1)
    return full, fox


def _step(x, loss_target, w, mom_m, mom_v):
    T, D = x.shape[1], x.shape[2]
    E = D
    G, P, C = w['s5_a_re'].shape[1], S5_STATE, S5_GROUP
    H = E // FOX_HEAD_DIM
    PG = len(POOL_WINDOWS)
    PD = E // PG
    NC = G // GROUPS_PER_CHUNK
    L = GROUPS_PER_CHUNK * P
    tq = _t(256, T)
    nq = T // tq

    full, fox_w = _gather_weights(w)
    w_out, w_s5in, w_glu, w_pin, w_pg = full['out_proj'], full['s5_in_proj'], full['s5_w_glu'], full['pool_in_proj'], full['pool_w_group']
    w_qkvz = fox_w[:, :4 * E]
    w_f = jnp.pad(fox_w[:, 4 * E:], ((0, 0), (0, LANES - H)))
    small_full = {}
    for n in SMALL_SHARDED:
        small_full[n] = None
    sv = [jnp.stack([w[n], w[n]]) for n in SMALL_SHARDED]
    got = _chip_allgather("gather_vectors", sv)
    for n, g in zip(SMALL_SHARDED, got):
        small_full[n] = jnp.transpose(g[:, 0], (1, 0, 2)).reshape(w[n].shape[0], E)

    norm_w = w['norm_w']
    h = x.reshape(T, D)
    saved = []
    dparts = {}

    def s5_consts(j):
        ar, ai, fr, fi = _s5_disc_fwd(f"s5_disc_{j}", w['s5_a_re'][j], w['s5_a_im'][j], w['s5_log_dt'][j].reshape(G, 1))
        br, bi = w['s5_b_re'][j].reshape(G * P, C), w['s5_b_im'][j].reshape(G * P, C)
        bbr, bbi = _s5_bbar(f"s5_bbar_{j}", fr.reshape(G * P, 1), fi.reshape(G * P, 1), br, bi)
        bt = lambda v: jnp.transpose(v.reshape(G, P, C), (0, 2, 1))
        bbd = jnp.concatenate([_blockdiag(bt(bbr), NC), _blockdiag(bt(bbi), NC)], axis=2).astype(BF16)
        ct = lambda v: jnp.transpose(v, (0, 2, 1))
        cbd = jnp.concatenate([_blockdiag(ct(w['s5_c_re'][j]), NC), -_blockdiag(ct(w['s5_c_im'][j]), NC)], axis=1).astype(BF16)
        return dict(ar=ar, ai=ai, fr=fr, fi=fi, br=br, bi=bi, bbd=bbd, cbd=cbd,
                    ar3=ar.reshape(NC, 1, L), ai3=ai.reshape(NC, 1, L))

    for i in range(4):
        kind, j = i % 3, i // 3
        nw = norm_w[i].reshape(1, D)
        xn = _norm_fwd(f"norm_{i}", h, nw)
        if kind == 0:
            k5 = s5_consts(j)
            proj = _mm_proj(f"s5_proj_{i}", xn, w_s5in, j)
            dsk = small_full['s5_d'][j].reshape(1, E)
            y1, g, hs = _s5_fwd(f"s5_scan_{i}", proj, k5['bbd'], k5['cbd'], k5['ar3'], k5['ai3'], dsk, E)
            bglu = small_full['s5_b_glu'][j].reshape(1, E)

            def glu_epi(acc, b, y1t, z):
                lin = acc + b
                return lin, (_gelu(y1t) * _sigmoid(lin)) * _silu(z)

            lin, a = _mm_rowsharded(
                f"s5_glu_{i}", g, w_glu, j, epi=glu_epi,
                extras=lambda tm, tn: [(bglu, _rowvec(tn)), (y1, _tile(tm, tn)), (proj, _tile(tm, tn, E // tn))],
                outs_fn=lambda tm, tn: [((T, E), F32, _tile(tm, tn)), ((T, E), BF16, _tile(tm, tn))])
            saved.append(dict(h=h, xn=xn, proj=proj, y1=y1, g=g, hs=hs, lin=lin, a=a, k5=k5, dsk=dsk))
        elif kind == 1:
            proj = _mm_plain(f"fox_proj_{i}", xn, w_qkvz)[0]
            flog = _mm_plain(f"fox_gate_proj_{i}", xn, w_f)[0]
            fb = jnp.pad(w['fox_f_bias'][j].reshape(1, H), ((0, 0), (0, LANES - H)))
            wq, wk = w['fox_q_norm'][j].reshape(1, FOX_HEAD_DIM), w['fox_k_norm'][j].reshape(1, FOX_HEAD_DIM)
            qn, kn = _qk_norm(f"fox_qk_norm_{i}", proj, wq, wk, H)
            cum = _cum_rows(f"fox_cum_{i}", flog, fb, False, True)
            cum_t = jnp.transpose(cum)[:H]
            cum_q = jnp.broadcast_to(cum_t[:, :, None], (H, T, LANES))
            cum_k = cum_t.reshape(H, nq, 1, tq)
            y, lse = _attn_fwd(f"fox_attn_{i}", qn, kn, proj, cum_q, cum_k, H)
            a = _rows(f"fox_gate_{i}", lambda yt, z: (yt * _silu(z),), [(y, 'r', E, 0), (proj, 'r', E, 3)], [('r', E, BF16)], 256)[0]
            saved.append(dict(h=h, xn=xn, proj=proj, flog=flog, fb=fb, wq=wq, wk=wk, qn=qn, kn=kn, cum_q=cum_q, cum_k=cum_k, y=y, lse=lse, a=a))
        else:
            proj = _mm_proj(f"pool_proj_{i}", xn, w_pin, j)
            pm = _pool_fwd(f"pool_win_{i}", proj, E)
            scale = small_full['pool_scale'][j].reshape(1, E)
            tm, tn, tk = _t(512, T), _t(512, PD), w_pg.shape[3]
            kb, nb = PD // tk, PD // tn
            mixed, a = _mm(
                f"pool_mix_{i}", pm, w_pg, M=T, N=PD, K=PD, tm=tm, tn=tn, tk=tk, groups=PG,
                a_spec=_bs((tm, tk), lambda g, m, n, k: (m, g * kb + k)),
                b_spec=_bs((None, None, None, tk, tn), lambda g, m, n, k: (k, j, g, 0, n)),
                extras=[(scale, _bs((1, tn), lambda g, m, n, k: (0, g * nb + n))),
                        (proj, _bs((tm, tn), lambda g, m, n, k: (m, E // tn + g * nb + n)))],
                epi=lambda acc, sc, z: (acc, (acc * sc) * _silu(z)),
                outs=[((T, E), F32, _bs((tm, tn), lambda g, m, n, k: (m, g * nb + n))),
                      ((T, E), BF16, _bs((tm, tn), lambda g, m, n, k: (m, g * nb + n)))])
            saved.append(dict(h=h, xn=xn, proj=proj, pm=pm, mixed=mixed, scale=scale, a=a))
        h = _mm_rowsharded(f"out_proj_{i}", saved[-1]['a'], w_out, i, epi=lambda acc, r: (r + acc,),
                           extras=lambda tm, tn: [(h, _tile(tm, tn))],
                           outs_fn=lambda tm, tn: [((T, D), F32, _tile(tm, tn))])[0]

    dh, loss_cols = _loss(h, loss_target.reshape(T, D))
    loss = lax.psum(jnp.sum(loss_cols), ("x", "y", "c"))

    gsmall = {n: [None] * w[n].shape[0] for n in SMALL}
    out_parts = [None] * 4
    for i in reversed(range(4)):
        kind, j = i % 3, i // 3
        sv_ = saved[i]
        nw = norm_w[i].reshape(1, D)
        out_parts[i] = _mm_dw_rows(f"d_out_proj_{i}", sv_['a'], dh)
        if kind == 0:
            proj, y1, lin, k5 = sv_['proj'], sv_['y1'], sv_['lin'], sv_['k5']

            def da_epi(da, y1t, lint, z):
                gt, sg = _gelu(y1t), _sigmoid(lint)
                dy2 = da * _silu(z)
                dlin = (dy2 * gt) * (sg * (1.0 - sg))
                return da * (gt * sg) * _dsilu(z), dlin, dy2 * sg, _colsum(dlin)

            nm = T // _t(512, T)
            dz, dlin, dgd, dbg = _mm_rowsharded_t(
                f"d_s5_act_{i}", dh, w_out, i, epi=da_epi,
                extras=lambda tm, tn: [(y1, _tile(tm, tn)), (lin, _tile(tm, tn)), (proj, _tile(tm, tn, E // tn))],
                outs_fn=lambda tm, tn: [((T, E), BF16, _tile(tm, tn)), ((T, E), BF16, _tile(tm, tn)), ((T, E), F32, _tile(tm, tn)),
                                        ((nm, 1, E), F32, _bs((None, 1, tn), lambda g, m, n, k: (m, 0, n)))])
            gsmall['s5_b_glu'][j] = jnp.sum(dbg, axis=(0, 1))
            dparts.setdefault('s5_w_glu', [None, None])[j] = _mm_dw_rows(f"d_s5_w_glu_{i}", sv_['g'], dlin)
            dy1 = _mm_rowsharded_t(
                f"d_s5_glu_{i}", dlin, w_glu, j, epi=lambda acc, d, y1t: ((acc + d) * _dgelu(y1t),),
                extras=lambda tm, tn: [(dgd, _tile(tm, tn)), (y1, _tile(tm, tn))],
                outs_fn=lambda tm, tn: [((T, E), F32, _tile(tm, tn))])[0]
            du, dbd, dcd, dab, ddk = _s5_bwd(f"d_s5_scan_{i}", dy1, proj, sv_['hs'], k5['bbd'], k5['cbd'], k5['ar3'], k5['ai3'], sv_['dsk'], E)
            gsmall['s5_d'][j] = ddk.reshape(E)
            dcr = _blockdiag_extract(dcd[:, :L], G)
            dci = _blockdiag_extract(dcd[:, L:], G)
            gsmall['s5_c_re'][j] = jnp.transpose(dcr, (0, 2, 1))
            gsmall['s5_c_im'][j] = -jnp.transpose(dci, (0, 2, 1))
            dbbr = jnp.transpose(_blockdiag_extract(dbd[:, :, :L], G), (0, 2, 1)).reshape(G * P, C)
            dbbi = jnp.transpose(_blockdiag_extract(dbd[:, :, L:], G), (0, 2, 1)).reshape(G * P, C)
            dbr, dbi, dfr, dfi = _s5_bbar_bwd(f"d_s5_bbar_{i}", k5['fr'].reshape(G * P, 1), k5['fi'].reshape(G * P, 1), k5['br'], k5['bi'], dbbr, dbbi)
            gsmall['s5_b_re'][j] = dbr.reshape(G, P, C)
            gsmall['s5_b_im'][j] = dbi.reshape(G, P, C)
            dab = jnp.sum(dab, axis=1)
            dare, daim, dldt = _s5_disc_bwd(f"d_s5_disc_{i}", w['s5_a_re'][j], w['s5_a_im'][j], w['s5_log_dt'][j].reshape(G, 1),
                                            (dab[:, :L].reshape(G, P), dab[:, L:].reshape(G, P), dfr.reshape(G, P), dfi.reshape(G, P)))
            gsmall['s5_a_re'][j], gsmall['s5_a_im'][j], gsmall['s5_log_dt'][j] = dare, daim, dldt.reshape(G)
            dproj = jnp.concatenate([du, dz], axis=1)
            dparts.setdefault('s5_in_proj', [None, None])[j] = _mm_dw_cols(f"d_s5_in_proj_{i}", sv_['xn'], dproj)
            dxn = _mm_colsharded_t(f"d_s5_xn_{i}", dproj, w_s5in, j)
        elif kind == 1:
            proj, y = sv_['proj'], sv_['y']
            do, dz = _mm_rowsharded_t(
                f"d_fox_act_{i}", dh, w_out, i, epi=lambda da, yt, z: (da * _silu(z), (da * yt) * _dsilu(z)),
                extras=lambda tm, tn: [(y, _tile(tm, tn)), (proj, _tile(tm, tn, 3 * E // tn))],
                outs_fn=lambda tm, tn: [((T, E), F32, _tile(tm, tn)), ((T, E), BF16, _tile(tm, tn))])
            dqn, dkn, dv, dcq, dck = _attn_bwd(f"d_fox_attn_{i}", sv_['qn'], sv_['kn'], proj, do, y, sv_['lse'], sv_['cum_q'], sv_['cum_k'], H)
            dq, dk, dwq, dwk = _qk_norm_bwd(f"d_fox_qk_norm_{i}", proj, sv_['wq'], sv_['wk'], dqn, dkn, H)
            gsmall['fox_q_norm'][j], gsmall['fox_k_norm'][j] = dwq.reshape(-1), dwk.reshape(-1)
            dcum_t = dcq[:, :, 0] + dck.reshape(H, T)
            dcum = jnp.pad(jnp.transpose(dcum_t), ((0, 0), (0, LANES - H)))
            dls = _cum_rows(f"d_fox_cum_{i}", dcum, jnp.zeros((1, LANES), F32), True, False)
            dflog, dfb = _rows(f"d_fox_gate_{i}", lambda d, f, b: ((lambda r: (r, _colsum(r)))(d * _sigmoid(-(f + b)))),
                               [(dls, 'r', LANES, 0), (sv_['flog'], 'r', LANES, 0), (sv_['fb'], 'b', LANES, 0)],
                               [('r', LANES, BF16), ('a', LANES, F32)], 256)
            gsmall['fox_f_bias'][j] = dfb[0, :H]
            dproj = jnp.concatenate([dq, dk, dv, dz], axis=1)
            dw_qkvz = _mm(f"d_fox_in_proj_{i}", sv_['xn'], dproj, M=D, N=4 * E, K=T, tm=_t(512, D), tn=_t(1024, 4 * E), tk=_t(512, T), ta=True,
                          a_spec=_bs((_t(512, T), _t(512, D)), lambda g, m, n, k: (k, m)),
                          b_spec=_bs((_t(512, T), _t(1024, 4 * E)), lambda g, m, n, k: (k, n)),
                          outs=[((D, 4 * E), BF16, _tile(_t(512, D), _t(1024, 4 * E)))])[0]
            dw_f = _mm(f"d_fox_gate_proj_{i}", sv_['xn'], dflog, M=D, N=LANES, K=T, tm=_t(512, D), tn=LANES, tk=_t(512, T), ta=True,
                       a_spec=_bs((_t(512, T), _t(512, D)), lambda g, m, n, k: (k, m)),
                       b_spec=_bs((_t(512, T), LANES), lambda g, m, n, k: (k, n)),
                       outs=[((D, LANES), BF16, _tile(_t(512, D), LANES))])[0]
            dw_fox = jnp.concatenate([dw_qkvz, dw_f[:, :H]], axis=1)
            sw = dw_fox.shape[1] // N_CHIPS
            dparts['fox_in_proj'] = [jnp.transpose(dw_fox.reshape(2, D // 2, N_CHIPS, sw), (0, 2, 1, 3))]
            dxn_f = _mm(f"d_fox_xn_gate_{i}", dflog, w_f, M=T, N=D, K=LANES, tm=_t(512, T), tn=_t(1024, D), tk=LANES, tb=True,
                        a_spec=_bs((_t(512, T), LANES), lambda g, m, n, k: (m, k)),
                        b_spec=_bs((_t(1024, D), LANES), lambda g, m, n, k: (n, k)),
                        outs=[((T, D), F32, _tile(_t(512, T), _t(1024, D)))])[0]
            tm, tn, tk = _t(512, T), _t(1024, D), _t(1024, 4 * E)
            dxn = _mm(f"d_fox_xn_{i}", dproj, w_qkvz, M=T, N=D, K=4 * E, tm=tm, tn=tn, tk=tk, tb=True,
                      a_spec=_bs((tm, tk), lambda g, m, n, k: (m, k)), b_spec=_bs((tn, tk), lambda g, m, n, k: (n, k)),
                      extras=[(dxn_f, _tile(tm, tn))], epi=lambda acc, e: (acc + e,),
                      outs=[((T, D), F32, _tile(tm, tn))])[0]
        else:
            proj, mixed, scale = sv_['proj'], sv_['mixed'], sv_['scale']
            nm = T // _t(512, T)

            def pool_epi(da, mx, sc, z):
                dy = da * _silu(z)
                return (da * (mx * sc)) * _dsilu(z), dy * sc, _colsum(dy * mx)

            dz, dmix, dsc = _mm_rowsharded_t(
                f"d_pool_act_{i}", dh, w_out, i, epi=pool_epi,
                extras=lambda tm, tn: [(mixed, _tile(tm, tn)), (scale, _rowvec(tn)), (proj, _tile(tm, tn, E // tn))],
                outs_fn=lambda tm, tn: [((T, E), BF16, _tile(tm, tn)), ((T, E), BF16, _tile(tm, tn)),
                                        ((nm, 1, E), F32, _bs((None, 1, tn), lambda g, m, n, k: (m, 0, n)))])
            gsmall['pool_scale'][j] = jnp.sum(dsc, axis=(0, 1))
            tkw = w_pg.shape[3]
            tk = _t(512, T)
            dparts['pool_w_group'] = [_mm(
                f"d_pool_w_group_{i}", sv_['pm'], dmix, M=PD, N=PD, K=T, tm=tkw, tn=PD, tk=tk, groups=PG, ta=True,
                a_spec=_bs((tk, tkw), lambda g, m, n, k: (k, g * (PD // tkw) + m)),
                b_spec=_bs((tk, PD), lambda g, m, n, k: (k, g)),
                outs=[((2, N_CHIPS, PG // 2, tkw, PD), BF16, _bs((None, None, None, tkw, PD), lambda g, m, n, k: (g // (PG // 2), m, g % (PG // 2), 0, 0)))])[0]]
            tm, tk2 = _t(512, T), _t(512, PD)
            dpm = _mm(f"d_pool_mix_{i}", dmix, w_pg, M=T, N=PD, K=PD, tm=tm, tn=tkw, tk=tk2, groups=PG, tb=True,
                      a_spec=_bs((tm, tk2), lambda g, m, n, k: (m, g * (PD // tk2) + k)),
                      b_spec=_bs((None, None, None, tkw, tk2), lambda g, m, n, k: (n, j, g, 0, k)),
                      outs=[((T, E), F32, _bs((tm, tkw), lambda g, m, n, k: (m, g * (PD // tkw) + n)))])[0]
            du = _pool_bwd(f"d_pool_win_{i}", dpm, E)
            dproj = jnp.concatenate([du, dz], axis=1)
            dparts['pool_in_proj'] = [_mm_dw_cols(f"d_pool_in_proj_{i}", sv_['xn'], dproj)]
            dxn = _mm_colsharded_t(f"d_pool_xn_{i}", dproj, w_pin, j)
        dh, dnw = _norm_bwd(f"d_norm_{i}", dxn, sv_['h'], nw, dh)
        gsmall['norm_w'][i] = dnw.reshape(D)
    grad_x = dh.reshape(x.shape)
    dparts['out_proj'] = out_parts

    small_flat = jnp.concatenate([jnp.stack(gsmall[n]).reshape(-1) for n in SMALL])
    n_small = small_flat.shape[0]
    unit = 2 * N_CHIPS * 16 * LANES
    n_pad = -(-n_small // unit) * unit
    R = n_pad // (2 * N_CHIPS * LANES)
    small_part = jnp.pad(small_flat, (0, n_pad - n_small)).astype(BF16).reshape(2, N_CHIPS, R, LANES)
    parts, dests, out_shapes = [], [], []
    for o, n in enumerate(BIG):
        sh = w[n].shape
        ps = dparts[n]
        half = ps[0].shape[2:]
        out_shapes.append(((len(ps), 2) if len(ps) > 1 else (2,)) + tuple(half))
        for li, pt in enumerate(ps):
            parts.append(pt)
            dests.append((o, li if len(ps) > 1 else None))
    parts.append(small_part)
    dests.append((len(BIG), None))
    out_shapes.append((2, R, LANES))
    red = _reduce_scatter(parts, dests, out_shapes)
    grads = {n: r.reshape(w[n].shape) for n, r in zip(BIG, red[:len(BIG)])}
    small_all = _chip_allgather("gather_small_grads", [red[len(BIG)]])[0]
    small_all = jnp.transpose(small_all, (1, 0, 2, 3)).reshape(-1)[:n_small]
    off = 0
    p = 2 * lax.axis_index("x") + lax.axis_index("y")
    for n in SMALL:
        full_shape = (w[n].shape[0], E) if n in SMALL_SHARDED else w[n].shape
        size = math.prod(full_shape)
        gfull = small_all[off:off + size].reshape(full_shape)
        off += size
        if n in SMALL_SHARDED:
            gfull = lax.dynamic_slice_in_dim(gfull, p * (E // N_CHIPS), E // N_CHIPS, axis=1)
        grads[n] = gfull

    delta, new_m, new_v = {}, {}, {}
    for n in BIG:
        f2 = lambda a: a.reshape(-1, a.shape[-1])
        d_, m_, v_ = _adamw(f"adamw_{n}", f2(w[n]), f2(grads[n]), f2(mom_m[n]), f2(mom_v[n]))
        delta[n], new_m[n], new_v[n] = d_.reshape(w[n].shape), m_.reshape(w[n].shape), v_.reshape(w[n].shape)
    sizes = [math.prod(w[n].shape) for n in SMALL]
    tot = sum(sizes)
    tot_pad = -(-tot // (256 * LANES)) * (256 * LANES)
    pack = lambda d: jnp.pad(jnp.concatenate([d[n].reshape(-1) for n in SMALL]), (0, tot_pad - tot)).reshape(-1, LANES)
    d_, m_, v_ = _adamw("adamw_small", pack(w), pack(grads), pack(mom_m), pack(mom_v))
    off = 0
    for n, size in zip(SMALL, sizes):
        for src, dst in ((d_, delta), (m_, new_m), (v_, new_v)):
            dst[n] = src.reshape(-1)[off:off + size].reshape(w[n].shape)
        off += size
    return (loss, grad_x, *[grads[n] for n in ORDER], *[delta[n] for n in ORDER], *[new_m[n] for n in ORDER], *[new_v[n] for n in ORDER])
```

```python
import functools
import math

import jax
import jax.numpy as jnp
from jax import lax
from jax.experimental import pallas as pl
from jax.experimental.pallas import tpu as pltpu

F32 = jnp.float32
BF16 = jnp.bfloat16
MESH = pl.DeviceIdType.MESH

N_CHIPS = 4
VMEM_LIMIT = 56 * 1024 * 1024
LANES = 128
SUB = 8

EPS = 1e-6
S5_GROUP = 16
S5_STATE = 64
GROUPS_PER_CHUNK = 16
FOX_HEAD_DIM = 128
POOL_WINDOWS = (2, 4, 8, 16)
POOL_HALO = 16
ADAM_LR, ADAM_B1, ADAM_B2, ADAM_EPS, ADAM_WD, ADAM_STEP = 0.001, 0.9, 0.999, 1e-08, 0.01, 10
NEG = -1e30
K_STEP = 2048


ANY = pl.BlockSpec(memory_space=pl.ANY)


def _t(pref, dim):
    if dim <= pref:
        return dim
    t = pref - pref % 16
    while t > 16 and dim % t:
        t -= 16
    assert dim % t == 0, (pref, dim)
    return t


def _params(sem):
    return pltpu.CompilerParams(dimension_semantics=sem, vmem_limit_bytes=VMEM_LIMIT)


def _sigmoid(x):
    return 1.0 / (1.0 + jnp.exp(-x))


def _silu(z):
    return z * _sigmoid(z)


def _dsilu(z):
    s = _sigmoid(z)
    return s * (1.0 + z * (1.0 - s))


_GELU_C = math.sqrt(2.0 / math.pi)


def _gelu(x):
    return 0.5 * x * (1.0 + jnp.tanh(_GELU_C * (x + 0.044715 * (x * x * x))))


def _dgelu(x):
    t = jnp.tanh(_GELU_C * (x + 0.044715 * (x * x * x)))
    return 0.5 * (1.0 + t) + 0.5 * x * (1.0 - t * t) * (_GELU_C * (1.0 + 3.0 * 0.044715 * x * x))


def _log_sigmoid(x):
    return jnp.minimum(x, 0.0) - jnp.log(1.0 + jnp.exp(-jnp.abs(x)))


def _rms(x):
    return lax.rsqrt(jnp.mean(x * x, axis=-1, keepdims=True) + EPS)


def _rms_bwd(x, w, dy):
    r = _rms(x)
    xhat = x * r
    dxh = dy * w
    dx = r * (dxh - xhat * jnp.mean(dxh * xhat, axis=-1, keepdims=True))
    return dx, dy * xhat


def _rows(name, fn, ins, outs, tr, pre=None, into=None):
    rows = None
    for arr, kind, cols, cb in ins:
        if kind == 'r':
            rows = arr.shape[0]
        elif kind == 's' and rows is None:
            rows = arr.shape[1]
    tr = _t(tr, rows)
    n_in = len(ins)
    has_acc = any(o[0] == 'a' for o in outs)

    def spec(kind, cols, cb):
        if kind == 'r':
            return pl.BlockSpec((tr, cols), lambda r, *p: (r, cb))
        if kind == 'b':
            return pl.BlockSpec((1, cols), lambda r, *p: (0, cb))
        return pl.BlockSpec((None, tr, cols), lambda r, p: (p[cb], r, 0))

    in_specs = [spec(kind, cols, cb) for _, kind, cols, cb in ins]
    out_specs, out_shape = [], []
    for o in outs:
        if o[0] == 'r':
            out_specs.append(pl.BlockSpec((tr, o[1]), lambda r, *p: (r, 0)))
            out_shape.append(jax.ShapeDtypeStruct((rows, o[1]), o[2]))
        elif o[0] == 'a':
            out_specs.append(pl.BlockSpec((1, o[1]), lambda r, *p: (0, 0)))
            out_shape.append(jax.ShapeDtypeStruct((1, o[1]), o[2]))
        else:
            blk = tuple(tr if d == 'tr' else d for d in o[3])
            out_specs.append(pl.BlockSpec(blk, o[4]))
            out_shape.append(jax.ShapeDtypeStruct(o[1], o[2]))
    n_pre = 0 if pre is None else 1
    args = [a[0] for a in ins]
    aliases = {}
    if into is not None:
        in_specs.append(ANY)
        args.append(into)
        aliases = {n_pre + n_in: 0}
    n_all = len(args)

    def body(*refs):
        refs = refs[n_pre:]
        res = fn(*[r[...] for r in refs[:n_in]])
        for spec_o, o, v in zip(outs, refs[n_all:], res):
            if spec_o[0] == 'a':
                @pl.when(pl.program_id(0) == 0)
                def _():
                    o[...] = jnp.zeros_like(o)
                o[...] += v.astype(o.dtype)
            else:
                o[...] = v.astype(o.dtype)

    grid_spec = pltpu.PrefetchScalarGridSpec(num_scalar_prefetch=n_pre, grid=(rows // tr,), in_specs=in_specs, out_specs=out_specs)
    if pre is not None:
        args = [pre] + args
    return pl.pallas_call(body, name=name, grid_spec=grid_spec, out_shape=out_shape, input_output_aliases=aliases,
                          compiler_params=_params(("arbitrary" if has_acc else "parallel",)))(*args)


def _colsum(v):
    return jnp.sum(v, axis=0, keepdims=True)


def _mm(name, a, b, *, M, N, K, tm, tn, tk, a_spec, b_spec, outs, epi=None, extras=(), groups=1, ta=False, tb=False):
    nk = K // tk
    assert M % tm == 0 and N % tn == 0 and K % tk == 0, (name, M, N, K, tm, tn, tk)
    dims = (((0 if ta else 1,), (1 if tb else 0,)), ((), ()))
    n_ex = len(extras)

    def body(*refs):
        a_ref, b_ref = refs[0], refs[1]
        ex = refs[2:2 + n_ex]
        out_refs = refs[2 + n_ex:2 + n_ex + len(outs)]

        def finish(r):
            res = (r,) if epi is None else epi(r, *[e[...] for e in ex])
            for o, v in zip(out_refs, res):
                o[...] = v.astype(o.dtype)

        part = lax.dot_general(a_ref[...].astype(BF16), b_ref[...].astype(BF16), dims, preferred_element_type=F32)
        if nk == 1:
            finish(part)
            return
        acc = refs[-1]
        k = pl.program_id(3)

        @pl.when(k == 0)
        def _():
            acc[...] = part

        @pl.when(k > 0)
        def _():
            acc[...] += part

        @pl.when(k == nk - 1)
        def _():
            finish(acc[...])

    return pl.pallas_call(
        body, name=name, grid=(groups, M // tm, N // tn, nk),
        in_specs=[a_spec, b_spec] + [s for _, s in extras],
        out_specs=[s for _, _, s in outs],
        out_shape=[jax.ShapeDtypeStruct(sh, dt) for sh, dt, _ in outs],
        scratch_shapes=[] if nk == 1 else [pltpu.VMEM((tm, tn), F32)],
        compiler_params=_params(("parallel", "parallel", "parallel", "arbitrary")),
    )(a, b, *[e for e, _ in extras])


def _bs(shape, f):
    return pl.BlockSpec(shape, f)


def _tile(tm, tn, coff=0):
    return _bs((tm, tn), lambda g, m, n, k: (m, n + coff))


def _rowvec(tn, coff=0):
    return _bs((1, tn), lambda g, m, n, k: (0, n + coff))


def _mm_proj(name, xn, w, j, *, epi=None, extras=(), out_dtype=F32):
    T, D = xn.shape
    sw = w.shape[3]
    N = N_CHIPS * sw
    tm, tn, tk = _t(512, T), _t(1024, sw), _t(K_STEP, D)
    nb = sw // tn
    return _mm(name, xn, w, M=T, N=N, K=D, tm=tm, tn=tn, tk=tk,
               a_spec=_bs((tm, tk), lambda g, m, n, k: (m, k)),
               b_spec=_bs((None, None, tk, tn), lambda g, m, n, k: (n // nb, j, k, n % nb)),
               outs=[((T, N), out_dtype, _tile(tm, tn))], epi=epi, extras=extras)[0]


def _mm_plain(name, a, b, *, out_dtype=F32, epi=None, extras=(), outs=None, tn_pref=1024):
    M, K = a.shape
    N = b.shape[1]
    tm, tn, tk = _t(512, M), _t(tn_pref, N), _t(K_STEP, K)
    if outs is None:
        outs = [((M, N), out_dtype, _tile(tm, tn))]
    return _mm(name, a, b, M=M, N=N, K=K, tm=tm, tn=tn, tk=tk,
               a_spec=_bs((tm, tk), lambda g, m, n, k: (m, k)),
               b_spec=_bs((tk, tn), lambda g, m, n, k: (k, n)),
               outs=outs, epi=epi, extras=extras)


def _mm_rowsharded(name, a, w, i, *, epi, extras, outs_fn):
    T, E = a.shape
    tk = w.shape[2]
    N = w.shape[3]
    tm, tn = _t(512, T), _t(1024, N)
    return _mm(name, a, w, M=T, N=N, K=E, tm=tm, tn=tn, tk=tk,
               a_spec=_bs((tm, tk), lambda g, m, n, k: (m, k)),
               b_spec=_bs((None, None, tk, tn), lambda g, m, n, k: (k, i, 0, n)),
               outs=outs_fn(tm, tn), epi=epi, extras=extras(tm, tn))


def _mm_rowsharded_t(name, d, w, i, *, epi, extras, outs_fn):
    T, N = d.shape
    tn = w.shape[2]
    E = N_CHIPS * tn
    tm, tk = _t(512, T), _t(K_STEP, N)
    return _mm(name, d, w, M=T, N=E, K=N, tm=tm, tn=tn, tk=tk, tb=True,
               a_spec=_bs((tm, tk), lambda g, m, n, k: (m, k)),
               b_spec=_bs((None, None, tn, tk), lambda g, m, n, k: (n, i, 0, k)),
               outs=outs_fn(tm, tn), epi=epi, extras=extras(tm, tn))


def _mm_colsharded_t(name, d, w, j):
    T, N = d.shape
    D, sw = w.shape[2], w.shape[3]
    tm, tn, tk = _t(512, T), _t(1024, D), _t(1024, sw)
    kb = sw // tk
    return _mm(name, d, w, M=T, N=D, K=N, tm=tm, tn=tn, tk=tk, tb=True,
               a_spec=_bs((tm, tk), lambda g, m, n, k: (m, k)),
               b_spec=_bs((None, None, tn, tk), lambda g, m, n, k: (k // kb, j, n, k % kb)),
               outs=[((T, D), F32, _tile(tm, tn))])[0]


def _mm_dw_rows(name, a, d):
    T, E = a.shape
    N = d.shape[1]
    tm, tn, tk = E // (2 * N_CHIPS), _t(2048, N), _t(K_STEP, T)
    return _mm(name, a, d, M=E, N=N, K=T, tm=tm, tn=tn, tk=tk, ta=True,
               a_spec=_bs((tk, tm), lambda g, m, n, k: (k, m)),
               b_spec=_bs((tk, tn), lambda g, m, n, k: (k, n)),
               outs=[((2, N_CHIPS, tm, N), BF16, _bs((None, None, tm, tn), lambda g, m, n, k: (m % 2, m // 2, 0, n)))])[0]


def _mm_dw_cols(name, xn, d):
    T, D = xn.shape
    N = d.shape[1]
    sw = N // N_CHIPS
    tm, tn, tk = _t(512, D // 2), _t(1024, sw), _t(K_STEP, T)
    mh, nb = (D // 2) // tm, sw // tn
    return _mm(name, xn, d, M=D, N=N, K=T, tm=tm, tn=tn, tk=tk, ta=True,
               a_spec=_bs((tk, tm), lambda g, m, n, k: (k, m)),
               b_spec=_bs((tk, tn), lambda g, m, n, k: (k, n)),
               outs=[((2, N_CHIPS, D // 2, sw), BF16,
                      _bs((None, None, tm, tn), lambda g, m, n, k: (m // mh, n // nb, m % mh, n % nb)))])[0]


def _norm_fwd(name, h, w):
    D = h.shape[1]
    return _rows(name, lambda x, g: ((x * _rms(x)) * g,), [(h, 'r', D, 0), (w, 'b', D, 0)], [('r', D, BF16)], 256)[0]


def _norm_bwd(name, dxn, h, w, dh):
    D = h.shape[1]

    def fn(dy, x, g, up):
        dx, dwt = _rms_bwd(x, g, dy)
        r = up + dx
        return r, r, _colsum(dwt)

    return _rows(name, fn, [(dxn, 'r', D, 0), (h, 'r', D, 0), (w, 'b', D, 0), (dh, 'r', D, 0)],
                 [('r', D, F32), ('r', D, BF16), ('a', D, F32)], 256)


def _loss(h, target):
    D = h.shape[1]

    def fn(y, t):
        e = y - t
        d = e * (1.0 / D)
        return d, d, _colsum(e * e) * (0.5 / D)

    return _rows("loss", fn, [(h, 'r', D, 0), (target, 'r', D, 0)], [('r', D, F32), ('r', D, BF16), ('a', D, F32)], 256)


def _adamw(name, w, g, m, v):
    cols = w.shape[1]

    def fn(w, g, m, v):
        m = ADAM_B1 * m + (1.0 - ADAM_B1) * g
        v = ADAM_B2 * v + (1.0 - ADAM_B2) * (g * g)
        m_hat = m / (1.0 - ADAM_B1 ** ADAM_STEP)
        v_hat = v / (1.0 - ADAM_B2 ** ADAM_STEP)
        delta = -ADAM_LR * (m_hat / (jnp.sqrt(v_hat) + ADAM_EPS) + ADAM_WD * w)
        return delta, m, v

    return _rows(name, fn, [(x, 'r', cols, 0) for x in (w, g, m, v)], [('r', cols, F32)] * 3, 256)


def _s5_disc(a_re, a_im, log_dt):
    dt = jnp.exp(log_dt)
    mag = jnp.exp(a_re * dt)
    abar_r = mag * jnp.cos(a_im * dt)
    abar_i = mag * jnp.sin(a_im * dt)
    den = a_re * a_re + a_im * a_im
    xr = abar_r - 1.0
    fr = (xr * a_re + abar_i * a_im) / den
    fi = (abar_i * a_re - xr * a_im) / den
    return abar_r, abar_i, fr, fi


def _s5_disc_fwd(name, a_re, a_im, log_dt):
    G, P = a_re.shape

    def body(ar, ai, ld, o0, o1, o2, o3):
        for o, v in zip((o0, o1, o2, o3), _s5_disc(ar[...], ai[...], ld[...])):
            o[...] = v

    return pl.pallas_call(body, name=name, out_shape=[jax.ShapeDtypeStruct((G, P), F32)] * 4)(a_re, a_im, log_dt)


def _s5_disc_bwd(name, a_re, a_im, log_dt, cts):
    G, P = a_re.shape

    def body(ar, ai, ld, c0, c1, c2, c3, d0, d1, d2):
        _, vjp = jax.vjp(_s5_disc, ar[...], ai[...], ld[...])
        g0, g1, g2 = vjp((c0[...], c1[...], c2[...], c3[...]))
        d0[...] = g0
        d1[...] = g1
        d2[...] = g2

    return pl.pallas_call(body, name=name, out_shape=[jax.ShapeDtypeStruct((G, P), F32)] * 2 + [jax.ShapeDtypeStruct((G, 1), F32)])(
        a_re, a_im, log_dt, *cts)


def _s5_bbar(name, fr, fi, br, bi):
    return _rows(name, lambda fr, fi, br, bi: (fr * br - fi * bi, fr * bi + fi * br),
                 [(fr, 'r', 1, 0), (fi, 'r', 1, 0), (br, 'r', S5_GROUP, 0), (bi, 'r', S5_GROUP, 0)],
                 [('r', S5_GROUP, F32)] * 2, 2048)


def _s5_bbar_bwd(name, fr, fi, br, bi, dr, di):
    def fn(fr, fi, br, bi, dr, di):
        return (fr * dr + fi * di, fr * di - fi * dr,
                jnp.sum(br * dr + bi * di, axis=1, keepdims=True), jnp.sum(br * di - bi * dr, axis=1, keepdims=True))

    return _rows(name, fn, [(fr, 'r', 1, 0), (fi, 'r', 1, 0)] + [(x, 'r', S5_GROUP, 0) for x in (br, bi, dr, di)],
                 [('r', S5_GROUP, F32)] * 2 + [('r', 1, F32)] * 2, 2048)


def _scan_mults(m_ref, ar, ai, reverse):
    L = ar.shape[1]
    row = lax.broadcasted_iota(jnp.int32, (SUB, L), 0)
    if reverse:
        row = (SUB - 1) - row
    ar = jnp.broadcast_to(ar, (SUB, L))
    ai = jnp.broadcast_to(ai, (SUB, L))
    a2r, a2i = ar * ar - ai * ai, 2.0 * ar * ai
    a4r, a4i = a2r * a2r - a2i * a2i, 2.0 * a2r * a2i
    zero = jnp.zeros((SUB, L), F32)
    for s, (pr, pi, d) in enumerate(((ar, ai, 1), (a2r, a2i, 2), (a4r, a4i, 4))):
        m_ref[2 * s] = jnp.where(row >= d, pr, zero)
        m_ref[2 * s + 1] = jnp.where(row >= d, pi, zero)
    pr, pi = ar, ai
    for bit, (qr, qi) in ((1, (ar, ai)), (2, (a2r, a2i)), (4, (a4r, a4i))):
        on = (row & bit) != 0
        nr, ni = pr * qr - pi * qi, pr * qi + pi * qr
        pr, pi = jnp.where(on, nr, pr), jnp.where(on, ni, pi)
    m_ref[6] = pr
    m_ref[7] = pi


def _scan8(xr, xi, m_ref, cr, ci, reverse):
    for s, d in enumerate((1, 2, 4)):
        sh = (SUB - d) if reverse else d
        sr, si = pltpu.roll(xr, sh, 0), pltpu.roll(xi, sh, 0)
        mr, mi = m_ref[2 * s], m_ref[2 * s + 1]
        xr, xi = xr + mr * sr - mi * si, xi + mr * si + mi * sr
    pr, pi = m_ref[6], m_ref[7]
    return xr + pr * cr - pi * ci, xi + pr * ci + pi * cr


def _s5_fwd(name, proj, bbd, cbd, abar_r, abar_i, dskip, E):
    T = proj.shape[0]
    NC, CH, L2 = bbd.shape
    L = L2 // 2
    tT = _t(256, T)

    def body(u_ref, b_ref, c_ref, ar_ref, ai_ref, d_ref, y_ref, g_ref, h_ref, bu, carry, mult):
        tb = pl.program_id(1)

        @pl.when(tb == 0)
        def _():
            carry[...] = jnp.zeros_like(carry)

        u = u_ref[...]
        bu[...] = jnp.dot(u.astype(BF16), b_ref[...], preferred_element_type=F32)
        _scan_mults(mult, ar_ref[...], ai_ref[...], False)

        def step(jb, c):
            cr, ci = c
            r0 = pl.multiple_of(jb * SUB, SUB)
            hr, hi = _scan8(bu[pl.ds(r0, SUB), 0:L], bu[pl.ds(r0, SUB), L:L2], mult, cr, ci, False)
            h_ref[pl.ds(r0, SUB), 0:L] = hr
            h_ref[pl.ds(r0, SUB), L:L2] = hi
            return (jnp.broadcast_to(hr[SUB - 1:SUB, :], (SUB, L)), jnp.broadcast_to(hi[SUB - 1:SUB, :], (SUB, L)))

        cr, ci = lax.fori_loop(0, tT // SUB, step, (carry[:, 0:L], carry[:, L:L2]))
        carry[:, 0:L] = cr
        carry[:, L:L2] = ci
        y1 = jnp.dot(h_ref[...].astype(BF16), c_ref[...], preferred_element_type=F32) + d_ref[...] * u
        y_ref[...] = y1
        g_ref[...] = _gelu(y1).astype(BF16)

    return pl.pallas_call(
        body, name=name, grid=(NC, T // tT),
        in_specs=[_bs((tT, CH), lambda c, t: (t, c)), _bs((None, CH, L2), lambda c, t: (c, 0, 0)),
                  _bs((None, L2, CH), lambda c, t: (c, 0, 0)), _bs((None, 1, L), lambda c, t: (c, 0, 0)),
                  _bs((None, 1, L), lambda c, t: (c, 0, 0)), _bs((1, CH), lambda c, t: (0, c))],
        out_specs=[_bs((tT, CH), lambda c, t: (t, c)), _bs((tT, CH), lambda c, t: (t, c)),
                   _bs((None, tT, L2), lambda c, t: (c, t, 0))],
        out_shape=[jax.ShapeDtypeStruct((T, E), F32), jax.ShapeDtypeStruct((T, E), BF16),
                   jax.ShapeDtypeStruct((NC, T, L2), F32)],
        scratch_shapes=[pltpu.VMEM((tT, L2), F32), pltpu.VMEM((SUB, L2), F32), pltpu.VMEM((8, SUB, L), F32)],
        compiler_params=_params(("parallel", "arbitrary")),
    )(proj, bbd, cbd, abar_r, abar_i, dskip)


def _s5_bwd(name, dy1, proj, hs, bbd, cbd, abar_r, abar_i, dskip, E):
    T = proj.shape[0]
    NC, CH, L2 = bbd.shape
    L = L2 // 2
    tT = _t(256, T)
    nT = T // tT
    tn = (((0,), (0,)), ((), ()))
    nt = (((1,), (1,)), ((), ()))

    def body(dy_ref, u_ref, h_ref, b_ref, c_ref, ar_ref, ai_ref, d_ref, du_ref, db_ref, dc_ref, da_ref, dd_ref, gb, carry, mult):
        tb = pl.program_id(1)

        @pl.when(tb == 0)
        def _():
            carry[...] = jnp.zeros_like(carry)
            db_ref[...] = jnp.zeros_like(db_ref)
            dc_ref[...] = jnp.zeros_like(dc_ref)
            da_ref[...] = jnp.zeros_like(da_ref)
            dd_ref[...] = jnp.zeros_like(dd_ref)

        dy = dy_ref[...]
        u = u_ref[...]
        dy16 = dy.astype(BF16)
        dc_ref[...] += lax.dot_general(h_ref[...].astype(BF16), dy16, tn, preferred_element_type=F32)
        gb[...] = lax.dot_general(dy16, c_ref[...], nt, preferred_element_type=F32)
        _scan_mults(mult, ar_ref[...], -ai_ref[...], True)
        row = lax.broadcasted_iota(jnp.int32, (SUB, L), 0)
        nblk = tT // SUB

        def step(jj, c):
            cr, ci, sr, si = c
            r0 = pl.multiple_of((nblk - 1 - jj) * SUB, SUB)
            gr, gi = _scan8(gb[pl.ds(r0, SUB), 0:L], gb[pl.ds(r0, SUB), L:L2], mult, cr, ci, True)
            gb[pl.ds(r0, SUB), 0:L] = gr
            gb[pl.ds(r0, SUB), L:L2] = gi
            nr = jnp.where(row == SUB - 1, cr, pltpu.roll(gr, SUB - 1, 0))
            ni = jnp.where(row == SUB - 1, ci, pltpu.roll(gi, SUB - 1, 0))
            hr, hi = h_ref[pl.ds(r0, SUB), 0:L], h_ref[pl.ds(r0, SUB), L:L2]
            sr = sr + nr * hr + ni * hi
            si = si + ni * hr - nr * hi
            return (jnp.broadcast_to(gr[0:1, :], (SUB, L)), jnp.broadcast_to(gi[0:1, :], (SUB, L)), sr, si)

        z = jnp.zeros((SUB, L), F32)
        cr, ci, sr, si = lax.fori_loop(0, nblk, step, (carry[:, 0:L], carry[:, L:L2], z, z))
        carry[:, 0:L] = cr
        carry[:, L:L2] = ci
        da_ref[:, 0:L] += sr
        da_ref[:, L:L2] += si
        g16 = gb[...].astype(BF16)
        du = lax.dot_general(g16, b_ref[...], nt, preferred_element_type=F32) + d_ref[...] * dy
        du_ref[...] = du.astype(BF16)
        db_ref[...] += lax.dot_general(u.astype(BF16), g16, tn, preferred_element_type=F32)
        dd_ref[...] += _colsum(dy * u)

    rev = lambda c, t: (nT - 1 - t, c)
    return pl.pallas_call(
        body, name=name, grid=(NC, nT),
        in_specs=[_bs((tT, CH), rev), _bs((tT, CH), rev), _bs((None, tT, L2), lambda c, t: (c, nT - 1 - t, 0)),
                  _bs((None, CH, L2), lambda c, t: (c, 0, 0)), _bs((None, L2, CH), lambda c, t: (c, 0, 0)),
                  _bs((None, 1, L), lambda c, t: (c, 0, 0)), _bs((None, 1, L), lambda c, t: (c, 0, 0)),
                  _bs((1, CH), lambda c, t: (0, c))],
        out_specs=[_bs((tT, CH), rev), _bs((None, CH, L2), lambda c, t: (c, 0, 0)), _bs((None, L2, CH), lambda c, t: (c, 0, 0)),
                   _bs((None, SUB, L2), lambda c, t: (c, 0, 0)), _bs((None, 1, CH), lambda c, t: (c, 0, 0))],
        out_shape=[jax.ShapeDtypeStruct((T, E), BF16), jax.ShapeDtypeStruct((NC, CH, L2), F32),
                   jax.ShapeDtypeStruct((NC, L2, CH), F32), jax.ShapeDtypeStruct((NC, SUB, L2), F32),
                   jax.ShapeDtypeStruct((NC, 1, CH), F32)],
        scratch_shapes=[pltpu.VMEM((tT, L2), F32), pltpu.VMEM((SUB, L2), F32), pltpu.VMEM((8, SUB, L), F32)],
        compiler_params=_params(("parallel", "arbitrary")),
    )(dy1, proj, hs, bbd, cbd, abar_r, abar_i, dskip)


def _blockdiag(x, NC):
    G, a, b = x.shape
    gpc = G // NC
    eye = jnp.eye(gpc, dtype=x.dtype)
    return jnp.einsum('ngab,gh->ngahb', x.reshape(NC, gpc, a, b), eye).reshape(NC, gpc * a, gpc * b)


def _blockdiag_extract(d, G):
    NC, A, B = d.shape
    gpc = G // NC
    a, b = A // gpc, B // gpc
    d5 = d.reshape(NC, gpc, a, gpc, b)
    eye = jnp.eye(gpc, dtype=d.dtype)
    return jnp.einsum('ngahb,gh->ngab', d5, eye).reshape(G, a, b)


def _cum_rows(name, x, bias, reverse, log_sig):
    T, L = x.shape

    def body(x_ref, b_ref, o_ref):
        row = lax.broadcasted_iota(jnp.int32, (SUB, L), 0)
        if reverse:
            row = (SUB - 1) - row
        nblk = T // SUB

        def step(jj, c):
            r0 = pl.multiple_of(((nblk - 1 - jj) if reverse else jj) * SUB, SUB)
            v = x_ref[pl.ds(r0, SUB), :] + b_ref[...]
            if log_sig:
                v = _log_sigmoid(v)
            for d in (1, 2, 4):
                v = v + jnp.where(row >= d, pltpu.roll(v, (SUB - d) if reverse else d, 0), 0.0)
            v = v + c
            o_ref[pl.ds(r0, SUB), :] = v
            e = 0 if reverse else SUB - 1
            return jnp.broadcast_to(v[e:e + 1, :], (SUB, L))

        lax.fori_loop(0, nblk, step, jnp.zeros((SUB, L), F32))

    return pl.pallas_call(body, name=name, out_shape=jax.ShapeDtypeStruct((T, L), F32),
                          compiler_params=pltpu.CompilerParams(vmem_limit_bytes=VMEM_LIMIT))(x, bias)


def _qk_norm(name, proj, wq, wk, H):
    T = proj.shape[0]
    Dh = FOX_HEAD_DIM
    tT = _t(512, T)

    def body(q_ref, k_ref, wq_ref, wk_ref, qn_ref, kn_ref):
        q, k = q_ref[...], k_ref[...]
        qn_ref[...] = ((q * _rms(q)) * wq_ref[...]).astype(BF16)
        kn_ref[...] = ((k * _rms(k)) * wk_ref[...]).astype(BF16)

    blk = lambda off: _bs((tT, Dh), lambda t, h: (t, h + off))
    return pl.pallas_call(
        body, name=name, grid=(T // tT, H),
        in_specs=[blk(0), blk(H), _bs((1, Dh), lambda t, h: (0, 0)), _bs((1, Dh), lambda t, h: (0, 0))],
        out_specs=[blk(0), blk(0)], out_shape=[jax.ShapeDtypeStruct((T, H * Dh), BF16)] * 2,
        compiler_params=_params(("parallel", "parallel")))(proj, proj, wq, wk)


def _qk_norm_bwd(name, proj, wq, wk, dqn, dkn, H):
    T = proj.shape[0]
    Dh = FOX_HEAD_DIM
    tT = _t(512, T)

    def body(q_ref, k_ref, wq_ref, wk_ref, dqn_ref, dkn_ref, dq_ref, dk_ref, dwq_ref, dwk_ref):
        @pl.when((pl.program_id(0) == 0) & (pl.program_id(1) == 0))
        def _():
            dwq_ref[...] = jnp.zeros_like(dwq_ref)
            dwk_ref[...] = jnp.zeros_like(dwk_ref)

        dq, tq = _rms_bwd(q_ref[...], wq_ref[...], dqn_ref[...])
        dk, tk = _rms_bwd(k_ref[...], wk_ref[...], dkn_ref[...])
        dq_ref[...] = dq.astype(BF16)
        dk_ref[...] = dk.astype(BF16)
        dwq_ref[...] += _colsum(tq)
        dwk_ref[...] += _colsum(tk)

    blk = lambda off: _bs((tT, Dh), lambda t, h: (t, h + off))
    one = _bs((1, Dh), lambda t, h: (0, 0))
    return pl.pallas_call(
        body, name=name, grid=(T // tT, H),
        in_specs=[blk(0), blk(H), one, one, blk(0), blk(0)],
        out_specs=[blk(0), blk(0), one, one],
        out_shape=[jax.ShapeDtypeStruct((T, H * Dh), BF16)] * 2 + [jax.ShapeDtypeStruct((1, Dh), F32)] * 2,
        compiler_params=_params(("arbitrary", "arbitrary")))(proj, proj, wq, wk, dqn, dkn)


def _attn_fwd(name, qn, kn, proj, cum_q, cum_k, H):
    T = qn.shape[0]
    Dh = FOX_HEAD_DIM
    tq = cum_k.shape[3]
    nq = T // tq
    scale = Dh ** -0.5
    nt = (((1,), (1,)), ((), ()))

    def body(q_ref, k_ref, v_ref, cq_ref, ck_ref, o_ref, lse_ref):
        i = pl.program_id(1)
        q = q_ref[...]
        cq = cq_ref[:, 0:1]
        qpos = i * tq + lax.broadcasted_iota(jnp.int32, (tq, tq), 0)
        kloc = lax.broadcasted_iota(jnp.int32, (tq, tq), 1)

        def chunk(kc, c):
            m, l, acc = c
            ks = pl.multiple_of(kc * tq, tq)
            s = lax.dot_general(q, k_ref[pl.ds(ks, tq), :], nt, preferred_element_type=F32) * scale + (cq - ck_ref[kc])
            s = jnp.where(ks + kloc <= qpos, s, NEG)
            m_new = jnp.maximum(m, jnp.max(s, axis=1, keepdims=True))
            alpha = jnp.exp(m - m_new)
            p = jnp.exp(s - m_new)
            l = alpha * l + jnp.sum(p, axis=1, keepdims=True)
            acc = alpha * acc + jnp.dot(p.astype(BF16), v_ref[pl.ds(ks, tq), :].astype(BF16), preferred_element_type=F32)
            return m_new, l, acc

        m, l, acc = lax.fori_loop(0, i + 1, chunk, (jnp.full((tq, 1), NEG, F32), jnp.zeros((tq, 1), F32), jnp.zeros((tq, Dh), F32)))
        o_ref[...] = acc / l
        lse_ref[...] = jnp.broadcast_to(m + jnp.log(l), (tq, LANES))

    return pl.pallas_call(
        body, name=name, grid=(H, nq),
        in_specs=[_bs((tq, Dh), lambda h, i: (i, h)), _bs((T, Dh), lambda h, i: (0, h)), _bs((T, Dh), lambda h, i: (0, 2 * H + h)),
                  _bs((None, tq, LANES), lambda h, i: (h, i, 0)), _bs((None, nq, 1, tq), lambda h, i: (h, 0, 0, 0))],
        out_specs=[_bs((tq, Dh), lambda h, i: (i, h)), _bs((None, tq, LANES), lambda h, i: (h, i, 0))],
        out_shape=[jax.ShapeDtypeStruct((T, H * Dh), F32), jax.ShapeDtypeStruct((H, T, LANES), F32)],
        compiler_params=_params(("parallel", "parallel")))(qn, kn, proj, cum_q, cum_k)


def _attn_bwd(name, qn, kn, proj, do, o, lse, cum_q, cum_k, H):
    T = qn.shape[0]
    Dh = FOX_HEAD_DIM
    tq = cum_k.shape[3]
    nq = T // tq
    scale = Dh ** -0.5
    nt = (((1,), (1,)), ((), ()))
    tn = (((0,), (0,)), ((), ()))

    def body(q_ref, k_ref, v_ref, do_ref, o_ref, lse_ref, cq_ref, ck_ref, dq_ref, dk_ref, dv_ref, dcq_ref, dck_ref, delta):
        j = pl.program_id(1)

        @pl.when(j == 0)
        def _():
            dq_ref[...] = jnp.zeros_like(dq_ref)
            dcq_ref[...] = jnp.zeros_like(dcq_ref)
            delta[...] = jnp.sum(do_ref[...] * o_ref[...], axis=1, keepdims=True)

        k = k_ref[...]
        v = v_ref[...].astype(BF16)
        ck = ck_ref[...]
        kpos = j * tq + lax.broadcasted_iota(jnp.int32, (tq, tq), 1)
        qloc = lax.broadcasted_iota(jnp.int32, (tq, tq), 0)

        def qblk(i, c):
            dk, dv, dck = c
            qs = pl.multiple_of(i * tq, tq)
            q = q_ref[pl.ds(qs, tq), :]
            do16 = do_ref[pl.ds(qs, tq), :].astype(BF16)
            s = lax.dot_general(q, k, nt, preferred_element_type=F32) * scale + (cq_ref[pl.ds(qs, tq), 0:1] - ck)
            p = jnp.where(kpos <= qs + qloc, jnp.exp(s - lse_ref[pl.ds(qs, tq), 0:1]), 0.0)
            dv = dv + lax.dot_general(p.astype(BF16), do16, tn, preferred_element_type=F32)
            dp = lax.dot_general(do16, v, nt, preferred_element_type=F32)
            ds = p * (dp - delta[pl.ds(qs, tq), :])
            ds16 = ds.astype(BF16)
            dk = dk + lax.dot_general(ds16, q, tn, preferred_element_type=F32) * scale
            dq_ref[pl.ds(qs, tq), :] += jnp.dot(ds16, k, preferred_element_type=F32) * scale
            dcq_ref[pl.ds(qs, tq), :] += jnp.broadcast_to(jnp.sum(ds, axis=1, keepdims=True), (tq, LANES))
            return dk, dv, dck + jnp.sum(ds, axis=0, keepdims=True)

        dk, dv, dck = lax.fori_loop(j, nq, qblk, (jnp.zeros((tq, Dh), F32), jnp.zeros((tq, Dh), F32), jnp.zeros((1, tq), F32)))
        dk_ref[...] = dk
        dv_ref[...] = dv.astype(BF16)
        dck_ref[...] = -dck

    whole = lambda off: _bs((T, Dh), lambda h, j: (0, h + off))
    blk = lambda off: _bs((tq, Dh), lambda h, j: (j, h + off))
    return pl.pallas_call(
        body, name=name, grid=(H, nq),
        in_specs=[whole(0), blk(0), blk(2 * H), whole(0), whole(0), _bs((None, T, LANES), lambda h, j: (h, 0, 0)),
                  _bs((None, T, LANES), lambda h, j: (h, 0, 0)), _bs((None, None, 1, tq), lambda h, j: (h, j, 0, 0))],
        out_specs=[whole(0), blk(0), blk(0), _bs((None, T, LANES), lambda h, j: (h, 0, 0)),
                   _bs((None, None, 1, tq), lambda h, j: (h, j, 0, 0))],
        out_shape=[jax.ShapeDtypeStruct((T, H * Dh), F32), jax.ShapeDtypeStruct((T, H * Dh), F32), jax.ShapeDtypeStruct((T, H * Dh), BF16),
                   jax.ShapeDtypeStruct((H, T, LANES), F32), jax.ShapeDtypeStruct((H, nq, 1, tq), F32)],
        scratch_shapes=[pltpu.VMEM((T, 1), F32)],
        compiler_params=_params(("parallel", "arbitrary")))(qn, kn, proj, do, o, lse, cum_q, cum_k)


def _pool_fwd(name, proj, E):
    T = proj.shape[0]
    PG = len(POOL_WINDOWS)
    PD = E // PG
    tT = _t(256, T)
    hb = tT // POOL_HALO

    def body(u_ref, halo_ref, o_ref, buf):
        g, tb = pl.program_id(0), pl.program_id(1)
        u = u_ref[...]
        buf[pl.ds(POOL_HALO, tT), :] = u
        buf[pl.ds(0, POOL_HALO), :] = jnp.where(tb == 0, 0.0, halo_ref[...])
        t = tb * tT + lax.broadcasted_iota(jnp.int32, (tT, 1), 0)
        for gi, w in enumerate(POOL_WINDOWS):
            @pl.when(g == gi)
            def _():
                acc = u
                for d in range(1, w):
                    acc = acc + buf[pl.ds(POOL_HALO - d, tT), :]
                cnt = jnp.minimum(t + 1, w).astype(F32)
                o_ref[...] = (acc / cnt - u).astype(BF16)

    return pl.pallas_call(
        body, name=name, grid=(PG, T // tT),
        in_specs=[_bs((tT, PD), lambda g, t: (t, g)), _bs((POOL_HALO, PD), lambda g, t: (jnp.maximum(t * hb - 1, 0), g))],
        out_specs=_bs((tT, PD), lambda g, t: (t, g)), out_shape=jax.ShapeDtypeStruct((T, E), BF16),
        scratch_shapes=[pltpu.VMEM((tT + POOL_HALO, PD), F32)],
        compiler_params=_params(("parallel", "parallel")))(proj, proj)


def _pool_bwd(name, dpm, E):
    T = dpm.shape[0]
    PG = len(POOL_WINDOWS)
    PD = E // PG
    tT = _t(256, T)
    hb = tT // POOL_HALO
    nT = T // tT

    def body(d_ref, halo_ref, o_ref, buf):
        g, tb = pl.program_id(0), pl.program_id(1)
        d = d_ref[...]
        t = tb * tT + lax.broadcasted_iota(jnp.int32, (tT, 1), 0)
        th = (tb + 1) * tT + lax.broadcasted_iota(jnp.int32, (POOL_HALO, 1), 0)
        for gi, w in enumerate(POOL_WINDOWS):
            @pl.when(g == gi)
            def _():
                dn = d / jnp.minimum(t + 1, w).astype(F32)
                buf[pl.ds(0, tT), :] = dn
                buf[pl.ds(tT, POOL_HALO), :] = jnp.where(tb == nT - 1, 0.0, halo_ref[...] / jnp.minimum(th + 1, w).astype(F32))
                acc = dn
                for s in range(1, w):
                    acc = acc + buf[pl.ds(s, tT), :]
                o_ref[...] = (acc - d).astype(BF16)

    return pl.pallas_call(
        body, name=name, grid=(PG, nT),
        in_specs=[_bs((tT, PD), lambda g, t: (t, g)), _bs((POOL_HALO, PD), lambda g, t: (jnp.minimum((t + 1) * hb, T // POOL_HALO - 1), g))],
        out_specs=_bs((tT, PD), lambda g, t: (t, g)), out_shape=jax.ShapeDtypeStruct((T, E), BF16),
        scratch_shapes=[pltpu.VMEM((tT + POOL_HALO, PD), F32)],
        compiler_params=_params(("parallel", "parallel")))(dpm, dpm)


def _coords():
    x, y, c = lax.axis_index("x"), lax.axis_index("y"), lax.axis_index("c")
    chips = [(1 - x, y), (x, 1 - y), (1 - x, 1 - y)]
    return x, y, c, 2 * x + y, (x, y, 1 - c), chips


def _chip_allgather(name, bufs):
    n = len(bufs)

    def body(*refs):
        outs = refs[n:2 * n]
        send, recv, fsend, frecv = refs[2 * n:]
        x, y, c, p, sib, chips = _coords()

        def direct(t, j, chip):
            return pltpu.make_async_remote_copy(src_ref=outs[t].at[p, c], dst_ref=outs[t].at[p, c], send_sem=send.at[t, j],
                                                recv_sem=recv.at[t, j], device_id=(*chip, c), device_id_type=MESH)

        def landed(t, j, chip):
            blk = outs[t].at[2 * chip[0] + chip[1], c]
            return pltpu.make_async_remote_copy(src_ref=blk, dst_ref=blk, send_sem=send.at[t, j],
                                                recv_sem=recv.at[t, j], device_id=(*chip, c), device_id_type=MESH)

        def passed(t, j, chip, half):
            blk = outs[t].at[2 * chip[0] + chip[1], half]
            return pltpu.make_async_remote_copy(src_ref=blk, dst_ref=blk, send_sem=fsend.at[t, j], recv_sem=frecv.at[t, j],
                                                device_id=sib, device_id_type=MESH)

        first = [direct(t, j, chip) for t in range(n) for j, chip in enumerate(chips)]
        for cp in first:
            cp.start()
        fwd = []
        for j, chip in enumerate(chips):
            for t in range(n):
                landed(t, j, chip).wait_recv()
                f = passed(t, j, chip, c)
                f.start()
                fwd.append(f)
        for j, chip in enumerate(chips):
            for t in range(n):
                passed(t, j, chip, 1 - c).wait_recv()
        for cp in first + fwd:
            cp.wait_send()

    return pl.pallas_call(
        body, name=name, in_specs=[ANY] * n, out_specs=[ANY] * n,
        out_shape=[jax.ShapeDtypeStruct(a.shape, a.dtype) for a in bufs],
        input_output_aliases={t: t for t in range(n)},
        scratch_shapes=[pltpu.SemaphoreType.DMA((n, 3))] * 4,
    )(*bufs)


def _pair_exchange(name, parts):
    n = len(parts)

    def body(*refs):
        ins, outs = refs[:n], refs[n:2 * n]
        send, recv = refs[2 * n:]
        x, y, c, p, sib, chips = _coords()
        cps = [pltpu.make_async_remote_copy(src_ref=ins[t].at[1 - c], dst_ref=outs[t], send_sem=send.at[t], recv_sem=recv.at[t],
                                            device_id=sib, device_id_type=MESH) for t in range(n)]
        for cp in cps:
            cp.start()
        for cp in cps:
            cp.wait()

    return pl.pallas_call(
        body, name=name, in_specs=[ANY] * n, out_specs=[ANY] * n,
        out_shape=[jax.ShapeDtypeStruct(a.shape[1:], a.dtype) for a in parts],
        scratch_shapes=[pltpu.SemaphoreType.DMA((n,))] * 2,
    )(*parts)


def _chip_exchange(name, sums):
    n = len(sums)

    def body(*refs):
        ins, outs = refs[:n], refs[n:2 * n]
        send, recv = refs[2 * n:]
        x, y, c, p, sib, chips = _coords()
        cps = [pltpu.make_async_remote_copy(src_ref=ins[t].at[2 * chip[0] + chip[1]], dst_ref=outs[t].at[j], send_sem=send.at[t, j],
                                            recv_sem=recv.at[t, j], device_id=(*chip, c), device_id_type=MESH)
               for t in range(n) for j, chip in enumerate(chips)]
        for cp in cps:
            cp.start()
        for cp in cps:
            cp.wait()

    return pl.pallas_call(
        body, name=name, in_specs=[ANY] * n, out_specs=[ANY] * n,
        out_shape=[jax.ShapeDtypeStruct((3,) + a.shape[1:], a.dtype) for a in sums],
        scratch_shapes=[pltpu.SemaphoreType.DMA((n, 3))] * 2,
    )(*sums)


def _pair_share(name, bufs, items):
    n = len(items)
    nb = len(bufs)

    def body(*refs):
        outs = refs[nb:2 * nb]
        send, recv = refs[2 * nb:]
        x, y, c, p, sib, chips = _coords()

        def blk(t, half):
            o, lead = items[t]
            return outs[o].at[p if lead == 'chip' else lead, half]

        def swap(t, half):
            return pltpu.make_async_remote_copy(src_ref=blk(t, half), dst_ref=blk(t, half), send_sem=send.at[t], recv_sem=recv.at[t],
                                                device_id=sib, device_id_type=MESH)

        cps = [swap(t, c) for t in range(n)]
        for cp in cps:
            cp.start()
        for t in range(n):
            swap(t, 1 - c).wait_recv()
        for cp in cps:
            cp.wait_send()

    return pl.pallas_call(
        body, name=name, in_specs=[ANY] * nb, out_specs=[ANY] * nb,
        out_shape=[jax.ShapeDtypeStruct(b.shape, b.dtype) for b in bufs],
        input_output_aliases={t: t for t in range(nb)},
        scratch_shapes=[pltpu.SemaphoreType.DMA((n,))] * 2,
    )(*bufs)


def _flat2(a, lead):
    return a.reshape(a.shape[:lead] + (-1, a.shape[-1]))


def _reduce_scatter(parts, dests, buf_shapes):
    c = lax.axis_index("c").astype(jnp.int32)
    p = (2 * lax.axis_index("x") + lax.axis_index("y")).astype(jnp.int32)
    got = _pair_exchange("rs_pair_exchange", parts)
    sums = []
    for t, (mine, theirs) in enumerate(zip(parts, got)):
        m3, t2 = _flat2(mine, 1), theirs.reshape(-1, theirs.shape[-1])
        m3 = m3.reshape(2, -1, m3.shape[-1])
        cols = t2.shape[1]
        s = _rows(f"rs_pair_sum_{t}", lambda a, b: (a.astype(F32) + b.astype(F32),),
                  [(m3, 's', cols, 0), (t2, 'r', cols, 0)], [('r', cols, BF16)], 512, pre=c.reshape(1))[0]
        sums.append(s.reshape(theirs.shape))
    got = _chip_exchange("rs_chip_exchange", sums)
    bufs = [None] * len(buf_shapes)
    for t, (mine, theirs) in enumerate(zip(sums, got)):
        o, lead = dests[t]
        shape = buf_shapes[o]
        rows, cols = shape[2], shape[3]
        m3, t3 = mine.reshape(N_CHIPS, rows, cols), theirs.reshape(3, rows, cols)
        pre = jnp.stack([p, jnp.int32(0), jnp.int32(1), jnp.int32(2), c, p if lead == 'chip' else jnp.int32(lead)])
        out = ('x', shape, F32, (None, None, 'tr', cols), lambda r, pr: (pr[5], pr[4], r, 0))
        bufs[o] = _rows(f"rs_chip_sum_{t}", lambda a, b0, b1, b2: (((a.astype(F32) + b0.astype(F32)) + b1.astype(F32)) + b2.astype(F32),),
                        [(m3, 's', cols, 0), (t3, 's', cols, 1), (t3, 's', cols, 2), (t3, 's', cols, 3)], [out], 512, pre=pre, into=bufs[o])[0]
    return _pair_share("rs_pair_share", bufs, dests)


def kernel(x, norm_w, out_proj, s5_in_proj, s5_a_re, s5_a_im, s5_log_dt, s5_b_re, s5_b_im, s5_c_re, s5_c_im, s5_d, s5_w_glu, s5_b_glu, fox_in_proj, fox_q_norm, fox_k_norm, fox_f_bias, pool_in_proj, pool_w_group, pool_scale, loss_target, m_norm_w, m_out_proj, m_s5_in_proj, m_s5_a_re, m_s5_a_im, m_s5_log_dt, m_s5_b_re, m_s5_b_im, m_s5_c_re, m_s5_c_im, m_s5_d, m_s5_w_glu, m_s5_b_glu, m_fox_in_proj, m_fox_q_norm, m_fox_k_norm, m_fox_f_bias, m_pool_in_proj, m_pool_w_group, m_pool_scale, v_norm_w, v_out_proj, v_s5_in_proj, v_s5_a_re, v_s5_a_im, v_s5_log_dt, v_s5_b_re, v_s5_b_im, v_s5_c_re, v_s5_c_im, v_s5_d, v_s5_w_glu, v_s5_b_glu, v_fox_in_proj, v_fox_q_norm, v_fox_k_norm, v_fox_f_bias, v_pool_in_proj, v_pool_w_group, v_pool_scale):
    weights = dict(norm_w=norm_w, out_proj=out_proj, s5_in_proj=s5_in_proj, s5_a_re=s5_a_re, s5_a_im=s5_a_im, s5_log_dt=s5_log_dt,
                   s5_b_re=s5_b_re, s5_b_im=s5_b_im, s5_c_re=s5_c_re, s5_c_im=s5_c_im, s5_d=s5_d, s5_w_glu=s5_w_glu, s5_b_glu=s5_b_glu,
                   fox_in_proj=fox_in_proj, fox_q_norm=fox_q_norm, fox_k_norm=fox_k_norm, fox_f_bias=fox_f_bias,
                   pool_in_proj=pool_in_proj, pool_w_group=pool_w_group, pool_scale=pool_scale)
    mom_m = dict(norm_w=m_norm_w, out_proj=m_out_proj, s5_in_proj=m_s5_in_proj, s5_a_re=m_s5_a_re, s5_a_im=m_s5_a_im, s5_log_dt=m_s5_log_dt,
                 s5_b_re=m_s5_b_re, s5_b_im=m_s5_b_im, s5_c_re=m_s5_c_re, s5_c_im=m_s5_c_im, s5_d=m_s5_d, s5_w_glu=m_s5_w_glu, s5_b_glu=m_s5_b_glu,
                 fox_in_proj=m_fox_in_proj, fox_q_norm=m_fox_q_norm, fox_k_norm=m_fox_k_norm, fox_f_bias=m_fox_f_bias,
                 pool_in_proj=m_pool_in_proj, pool_w_group=m_pool_w_group, pool_scale=m_pool_scale)
    mom_v = dict(norm_w=v_norm_w, out_proj=v_out_proj, s5_in_proj=v_s5_in_proj, s5_a_re=v_s5_a_re, s5_a_im=v_s5_a_im, s5_log_dt=v_s5_log_dt,
                 s5_b_re=v_s5_b_re, s5_b_im=v_s5_b_im, s5_c_re=v_s5_c_re, s5_c_im=v_s5_c_im, s5_d=v_s5_d, s5_w_glu=v_s5_w_glu, s5_b_glu=v_s5_b_glu,
                 fox_in_proj=v_fox_in_proj, fox_q_norm=v_fox_q_norm, fox_k_norm=v_fox_k_norm, fox_f_bias=v_fox_f_bias,
                 pool_in_proj=v_pool_in_proj, pool_w_group=v_pool_w_group, pool_scale=v_pool_scale)
    return _step(x, loss_target, weights, mom_m, mom_v)


BIG = ('out_proj', 's5_in_proj', 's5_w_glu', 'fox_in_proj', 'pool_in_proj', 'pool_w_group')
SMALL = ('norm_w', 's5_a_re', 's5_a_im', 's5_log_dt', 's5_b_re', 's5_b_im', 's5_c_re', 's5_c_im', 's5_d', 's5_b_glu',
         'fox_q_norm', 'fox_k_norm', 'fox_f_bias', 'pool_scale')
SMALL_SHARDED = ('s5_d', 's5_b_glu', 'pool_scale')
ORDER = ('norm_w', 'out_proj', 's5_in_proj', 's5_a_re', 's5_a_im', 's5_log_dt', 's5_b_re', 's5_b_im', 's5_c_re', 's5_c_im', 's5_d',
         's5_w_glu', 's5_b_glu', 'fox_in_proj', 'fox_q_norm', 'fox_k_norm', 'fox_f_bias', 'pool_in_proj', 'pool_w_group', 'pool_scale')


def _split2(shape):
    if shape[0] % 2 == 0:
        return (2, shape[0] // 2) + tuple(shape[1:])
    assert shape[0] == 1 and shape[1] % 2 == 0
    return (2, shape[1] // 2) + tuple(shape[2:])


def _gather_weights(w):
    p = (2 * lax.axis_index("x") + lax.axis_index("y")).astype(jnp.int32).reshape(1)
    bufs = []
    for n in BIG:
        a = w[n]
        a2 = a.reshape(-1, a.shape[-1])
        rows, cols = a2.shape
        out = ('x', (N_CHIPS, rows, cols), BF16, (None, 'tr', cols), lambda r, pr: (pr[0], r, 0))
        b = _rows(f"cast_{n}", lambda v: (v,), [(a2, 'r', cols, 0)], [out], 256, pre=p)[0]
        bufs.append(b.reshape((N_CHIPS,) + _split2(a.shape)))
    got = _chip_allgather("gather_weights", bufs)
    full = {n: g.reshape((N_CHIPS,) + w[n].shape) for n, g in zip(BIG, got)}
    fox = full['fox_in_proj']
    D = fox.shape[2]
    fox = jnp.transpose(fox[:, 0], (1, 0, 2)).reshape(D, -1)
    return full, fox


def _step(x, loss_target, w, mom_m, mom_v):
    T, D = x.shape[1], x.shape[2]
    E = D
    G, P, C = w['s5_a_re'].shape[1], S5_STATE, S5_GROUP
    H = E // FOX_HEAD_DIM
    PG = len(POOL_WINDOWS)
    PD = E // PG
    NC = G // GROUPS_PER_CHUNK
    L = GROUPS_PER_CHUNK * P
    tq = _t(256, T)
    nq = T // tq

    full, fox_w = _gather_weights(w)
    w_out, w_s5in, w_glu, w_pin, w_pg = full['out_proj'], full['s5_in_proj'], full['s5_w_glu'], full['pool_in_proj'], full['pool_w_group']
    w_qkvz = fox_w[:, :4 * E]
    w_f = jnp.pad(fox_w[:, 4 * E:], ((0, 0), (0, LANES - H)))
    small_full = {}
    chip = 2 * lax.axis_index("x") + lax.axis_index("y")
    sv = [lax.dynamic_update_index_in_dim(jnp.zeros((N_CHIPS, 2) + w[n].shape, F32), jnp.stack([w[n], w[n]]), chip, 0)
          for n in SMALL_SHARDED]
    got = _chip_allgather("gather_vectors", sv)
    for n, g in zip(SMALL_SHARDED, got):
        small_full[n] = jnp.transpose(g[:, 0], (1, 0, 2)).reshape(w[n].shape[0], E)

    norm_w = w['norm_w']
    h = x.reshape(T, D)
    saved = []
    dparts = {}

    def s5_consts(j):
        ar, ai, fr, fi = _s5_disc_fwd(f"s5_disc_{j}", w['s5_a_re'][j], w['s5_a_im'][j], w['s5_log_dt'][j].reshape(G, 1))
        br, bi = w['s5_b_re'][j].reshape(G * P, C), w['s5_b_im'][j].reshape(G * P, C)
        bbr, bbi = _s5_bbar(f"s5_bbar_{j}", fr.reshape(G * P, 1), fi.reshape(G * P, 1), br, bi)
        bt = lambda v: jnp.transpose(v.reshape(G, P, C), (0, 2, 1))
        bbd = jnp.concatenate([_blockdiag(bt(bbr), NC), _blockdiag(bt(bbi), NC)], axis=2).astype(BF16)
        ct = lambda v: jnp.transpose(v, (0, 2, 1))
        cbd = jnp.concatenate([_blockdiag(ct(w['s5_c_re'][j]), NC), -_blockdiag(ct(w['s5_c_im'][j]), NC)], axis=1).astype(BF16)
        return dict(ar=ar, ai=ai, fr=fr, fi=fi, br=br, bi=bi, bbd=bbd, cbd=cbd,
                    ar3=ar.reshape(NC, 1, L), ai3=ai.reshape(NC, 1, L))

    for i in range(4):
        kind, j = i % 3, i // 3
        nw = norm_w[i].reshape(1, D)
        xn = _norm_fwd(f"norm_{i}", h, nw)
        if kind == 0:
            k5 = s5_consts(j)
            proj = _mm_proj(f"s5_proj_{i}", xn, w_s5in, j)
            dsk = small_full['s5_d'][j].reshape(1, E)
            y1, g, hs = _s5_fwd(f"s5_scan_{i}", proj, k5['bbd'], k5['cbd'], k5['ar3'], k5['ai3'], dsk, E)
            bglu = small_full['s5_b_glu'][j].reshape(1, E)

            def glu_epi(acc, b, y1t, z):
                lin = acc + b
                return lin, (_gelu(y1t) * _sigmoid(lin)) * _silu(z)

            lin, a = _mm_rowsharded(
                f"s5_glu_{i}", g, w_glu, j, epi=glu_epi,
                extras=lambda tm, tn: [(bglu, _rowvec(tn)), (y1, _tile(tm, tn)), (proj, _tile(tm, tn, E // tn))],
                outs_fn=lambda tm, tn: [((T, E), F32, _tile(tm, tn)), ((T, E), BF16, _tile(tm, tn))])
            saved.append(dict(h=h, xn=xn, proj=proj, y1=y1, g=g, hs=hs, lin=lin, a=a, k5=k5, dsk=dsk))
        elif kind == 1:
            proj = _mm_plain(f"fox_proj_{i}", xn, w_qkvz)[0]
            flog = _mm_plain(f"fox_gate_proj_{i}", xn, w_f)[0]
            fb = jnp.pad(w['fox_f_bias'][j].reshape(1, H), ((0, 0), (0, LANES - H)))
            wq, wk = w['fox_q_norm'][j].reshape(1, FOX_HEAD_DIM), w['fox_k_norm'][j].reshape(1, FOX_HEAD_DIM)
            qn, kn = _qk_norm(f"fox_qk_norm_{i}", proj, wq, wk, H)
            cum = _cum_rows(f"fox_cum_{i}", flog, fb, False, True)
            cum_t = jnp.transpose(cum)[:H]
            cum_q = jnp.broadcast_to(cum_t[:, :, None], (H, T, LANES))
            cum_k = cum_t.reshape(H, nq, 1, tq)
            y, lse = _attn_fwd(f"fox_attn_{i}", qn, kn, proj, cum_q, cum_k, H)
            a = _rows(f"fox_gate_{i}", lambda yt, z: (yt * _silu(z),), [(y, 'r', E, 0), (proj, 'r', E, 3)], [('r', E, BF16)], 256)[0]
            saved.append(dict(h=h, xn=xn, proj=proj, flog=flog, fb=fb, wq=wq, wk=wk, qn=qn, kn=kn, cum_q=cum_q, cum_k=cum_k, y=y, lse=lse, a=a))
        else:
            proj = _mm_proj(f"pool_proj_{i}", xn, w_pin, j)
            pm = _pool_fwd(f"pool_win_{i}", proj, E)
            scale = small_full['pool_scale'][j].reshape(1, E)
            tm, tn, tk = _t(512, T), _t(512, PD), w_pg.shape[3]
            kb, nb = PD // tk, PD // tn
            mixed, a = _mm(
                f"pool_mix_{i}", pm, w_pg, M=T, N=PD, K=PD, tm=tm, tn=tn, tk=tk, groups=PG,
                a_spec=_bs((tm, tk), lambda g, m, n, k: (m, g * kb + k)),
                b_spec=_bs((None, None, None, tk, tn), lambda g, m, n, k: (k, j, g, 0, n)),
                extras=[(scale, _bs((1, tn), lambda g, m, n, k: (0, g * nb + n))),
                        (proj, _bs((tm, tn), lambda g, m, n, k: (m, E // tn + g * nb + n)))],
                epi=lambda acc, sc, z: (acc, (acc * sc) * _silu(z)),
                outs=[((T, E), F32, _bs((tm, tn), lambda g, m, n, k: (m, g * nb + n))),
                      ((T, E), BF16, _bs((tm, tn), lambda g, m, n, k: (m, g * nb + n)))])
            saved.append(dict(h=h, xn=xn, proj=proj, pm=pm, mixed=mixed, scale=scale, a=a))
        h = _mm_rowsharded(f"out_proj_{i}", saved[-1]['a'], w_out, i, epi=lambda acc, r: (r + acc,),
                           extras=lambda tm, tn: [(h, _tile(tm, tn))],
                           outs_fn=lambda tm, tn: [((T, D), F32, _tile(tm, tn))])[0]

    dh, dh16, loss_cols = _loss(h, loss_target.reshape(T, D))
    loss = lax.psum(jnp.sum(loss_cols), ("x", "y", "c"))

    gsmall = {n: [None] * w[n].shape[0] for n in SMALL}
    out_parts = [None] * 4
    for i in reversed(range(4)):
        kind, j = i % 3, i // 3
        sv_ = saved[i]
        nw = norm_w[i].reshape(1, D)
        out_parts[i] = _mm_dw_rows(f"d_out_proj_{i}", sv_['a'], dh16)
        if kind == 0:
            proj, y1, lin, k5 = sv_['proj'], sv_['y1'], sv_['lin'], sv_['k5']

            def da_epi(da, y1t, lint, z):
                gt, sg = _gelu(y1t), _sigmoid(lint)
                dy2 = da * _silu(z)
                dlin = (dy2 * gt) * (sg * (1.0 - sg))
                return da * (gt * sg) * _dsilu(z), dlin, dy2 * sg, _colsum(dlin)

            nm = T // _t(512, T)
            dz, dlin, dgd, dbg = _mm_rowsharded_t(
                f"d_s5_act_{i}", dh16, w_out, i, epi=da_epi,
                extras=lambda tm, tn: [(y1, _tile(tm, tn)), (lin, _tile(tm, tn)), (proj, _tile(tm, tn, E // tn))],
                outs_fn=lambda tm, tn: [((T, E), BF16, _tile(tm, tn)), ((T, E), BF16, _tile(tm, tn)), ((T, E), F32, _tile(tm, tn)),
                                        ((nm, 1, E), F32, _bs((None, 1, tn), lambda g, m, n, k: (m, 0, n)))])
            gsmall['s5_b_glu'][j] = jnp.sum(dbg, axis=(0, 1))
            dparts.setdefault('s5_w_glu', [None, None])[j] = _mm_dw_rows(f"d_s5_w_glu_{i}", sv_['g'], dlin)
            dy1 = _mm_rowsharded_t(
                f"d_s5_glu_{i}", dlin, w_glu, j, epi=lambda acc, d, y1t: ((acc + d) * _dgelu(y1t),),
                extras=lambda tm, tn: [(dgd, _tile(tm, tn)), (y1, _tile(tm, tn))],
                outs_fn=lambda tm, tn: [((T, E), F32, _tile(tm, tn))])[0]
            du, dbd, dcd, dab, ddk = _s5_bwd(f"d_s5_scan_{i}", dy1, proj, sv_['hs'], k5['bbd'], k5['cbd'], k5['ar3'], k5['ai3'], sv_['dsk'], E)
            gsmall['s5_d'][j] = ddk.reshape(E)
            dcr = _blockdiag_extract(dcd[:, :L], G)
            dci = _blockdiag_extract(dcd[:, L:], G)
            gsmall['s5_c_re'][j] = jnp.transpose(dcr, (0, 2, 1))
            gsmall['s5_c_im'][j] = -jnp.transpose(dci, (0, 2, 1))
            dbbr = jnp.transpose(_blockdiag_extract(dbd[:, :, :L], G), (0, 2, 1)).reshape(G * P, C)
            dbbi = jnp.transpose(_blockdiag_extract(dbd[:, :, L:], G), (0, 2, 1)).reshape(G * P, C)
            dbr, dbi, dfr, dfi = _s5_bbar_bwd(f"d_s5_bbar_{i}", k5['fr'].reshape(G * P, 1), k5['fi'].reshape(G * P, 1), k5['br'], k5['bi'], dbbr, dbbi)
            gsmall['s5_b_re'][j] = dbr.reshape(G, P, C)
            gsmall['s5_b_im'][j] = dbi.reshape(G, P, C)
            dab = jnp.sum(dab, axis=1)
            dare, daim, dldt = _s5_disc_bwd(f"d_s5_disc_{i}", w['s5_a_re'][j], w['s5_a_im'][j], w['s5_log_dt'][j].reshape(G, 1),
                                            (dab[:, :L].reshape(G, P), dab[:, L:].reshape(G, P), dfr.reshape(G, P), dfi.reshape(G, P)))
            gsmall['s5_a_re'][j], gsmall['s5_a_im'][j], gsmall['s5_log_dt'][j] = dare, daim, dldt.reshape(G)
            dproj = jnp.concatenate([du, dz], axis=1)
            dparts.setdefault('s5_in_proj', [None, None])[j] = _mm_dw_cols(f"d_s5_in_proj_{i}", sv_['xn'], dproj)
            dxn = _mm_colsharded_t(f"d_s5_xn_{i}", dproj, w_s5in, j)
        elif kind == 1:
            proj, y = sv_['proj'], sv_['y']
            do, dz = _mm_rowsharded_t(
                f"d_fox_act_{i}", dh16, w_out, i, epi=lambda da, yt, z: (da * _silu(z), (da * yt) * _dsilu(z)),
                extras=lambda tm, tn: [(y, _tile(tm, tn)), (proj, _tile(tm, tn, 3 * E // tn))],
                outs_fn=lambda tm, tn: [((T, E), F32, _tile(tm, tn)), ((T, E), BF16, _tile(tm, tn))])
            dqn, dkn, dv, dcq, dck = _attn_bwd(f"d_fox_attn_{i}", sv_['qn'], sv_['kn'], proj, do, y, sv_['lse'], sv_['cum_q'], sv_['cum_k'], H)
            dq, dk, dwq, dwk = _qk_norm_bwd(f"d_fox_qk_norm_{i}", proj, sv_['wq'], sv_['wk'], dqn, dkn, H)
            gsmall['fox_q_norm'][j], gsmall['fox_k_norm'][j] = dwq.reshape(-1), dwk.reshape(-1)
            dcum_t = dcq[:, :, 0] + dck.reshape(H, T)
            dcum = jnp.pad(jnp.transpose(dcum_t), ((0, 0), (0, LANES - H)))
            dls = _cum_rows(f"d_fox_cum_{i}", dcum, jnp.zeros((1, LANES), F32), True, False)
            dflog, dfb = _rows(f"d_fox_gate_{i}", lambda d, f, b: ((lambda r: (r, _colsum(r)))(d * _sigmoid(-(f + b)))),
                               [(dls, 'r', LANES, 0), (sv_['flog'], 'r', LANES, 0), (sv_['fb'], 'b', LANES, 0)],
                               [('r', LANES, BF16), ('a', LANES, F32)], 256)
            gsmall['fox_f_bias'][j] = dfb[0, :H]
            dproj = jnp.concatenate([dq, dk, dv, dz], axis=1)
            tkT = _t(K_STEP, T)
            dw_qkvz = _mm(f"d_fox_in_proj_{i}", sv_['xn'], dproj, M=D, N=4 * E, K=T, tm=_t(512, D), tn=_t(1024, 4 * E), tk=tkT, ta=True,
                          a_spec=_bs((tkT, _t(512, D)), lambda g, m, n, k: (k, m)),
                          b_spec=_bs((tkT, _t(1024, 4 * E)), lambda g, m, n, k: (k, n)),
                          outs=[((D, 4 * E), BF16, _tile(_t(512, D), _t(1024, 4 * E)))])[0]
            dw_f = _mm(f"d_fox_gate_proj_{i}", sv_['xn'], dflog, M=D, N=LANES, K=T, tm=_t(512, D), tn=LANES, tk=tkT, ta=True,
                       a_spec=_bs((tkT, _t(512, D)), lambda g, m, n, k: (k, m)),
                       b_spec=_bs((tkT, LANES), lambda g, m, n, k: (k, n)),
                       outs=[((D, LANES), BF16, _tile(_t(512, D), LANES))])[0]
            dw_fox = jnp.concatenate([dw_qkvz, dw_f[:, :H]], axis=1)
            sw = dw_fox.shape[1] // N_CHIPS
            dparts['fox_in_proj'] = [jnp.transpose(dw_fox.reshape(2, D // 2, N_CHIPS, sw), (0, 2, 1, 3))]
            dxn_f = _mm(f"d_fox_xn_gate_{i}", dflog, w_f, M=T, N=D, K=LANES, tm=_t(512, T), tn=_t(1024, D), tk=LANES, tb=True,
                        a_spec=_bs((_t(512, T), LANES), lambda g, m, n, k: (m, k)),
                        b_spec=_bs((_t(1024, D), LANES), lambda g, m, n, k: (n, k)),
                        outs=[((T, D), F32, _tile(_t(512, T), _t(1024, D)))])[0]
            tm, tn, tk = _t(512, T), _t(1024, D), _t(1024, 4 * E)
            dxn = _mm(f"d_fox_xn_{i}", dproj, w_qkvz, M=T, N=D, K=4 * E, tm=tm, tn=tn, tk=tk, tb=True,
                      a_spec=_bs((tm, tk), lambda g, m, n, k: (m, k)), b_spec=_bs((tn, tk), lambda g, m, n, k: (n, k)),
                      extras=[(dxn_f, _tile(tm, tn))], epi=lambda acc, e: (acc + e,),
                      outs=[((T, D), F32, _tile(tm, tn))])[0]
        else:
            proj, mixed, scale = sv_['proj'], sv_['mixed'], sv_['scale']
            nm = T // _t(512, T)

            def pool_epi(da, mx, sc, z):
                dy = da * _silu(z)
                return (da * (mx * sc)) * _dsilu(z), dy * sc, _colsum(dy * mx)

            dz, dmix, dsc = _mm_rowsharded_t(
                f"d_pool_act_{i}", dh16, w_out, i, epi=pool_epi,
                extras=lambda tm, tn: [(mixed, _tile(tm, tn)), (scale, _rowvec(tn)), (proj, _tile(tm, tn, E // tn))],
                outs_fn=lambda tm, tn: [((T, E), BF16, _tile(tm, tn)), ((T, E), BF16, _tile(tm, tn)),
                                        ((nm, 1, E), F32, _bs((None, 1, tn), lambda g, m, n, k: (m, 0, n)))])
            gsmall['pool_scale'][j] = jnp.sum(dsc, axis=(0, 1))
            tkw = w_pg.shape[3]
            tk = _t(K_STEP, T)
            dparts['pool_w_group'] = [_mm(
                f"d_pool_w_group_{i}", sv_['pm'], dmix, M=PD, N=PD, K=T, tm=tkw, tn=PD, tk=tk, groups=PG, ta=True,
                a_spec=_bs((tk, tkw), lambda g, m, n, k: (k, g * (PD // tkw) + m)),
                b_spec=_bs((tk, PD), lambda g, m, n, k: (k, g)),
                outs=[((2, N_CHIPS, PG // 2, tkw, PD), BF16, _bs((None, None, None, tkw, PD), lambda g, m, n, k: (g // (PG // 2), m, g % (PG // 2), 0, 0)))])[0]]
            tm, tk2 = _t(512, T), _t(512, PD)
            dpm = _mm(f"d_pool_mix_{i}", dmix, w_pg, M=T, N=PD, K=PD, tm=tm, tn=tkw, tk=tk2, groups=PG, tb=True,
                      a_spec=_bs((tm, tk2), lambda g, m, n, k: (m, g * (PD // tk2) + k)),
                      b_spec=_bs((None, None, None, tkw, tk2), lambda g, m, n, k: (n, j, g, 0, k)),
                      outs=[((T, E), F32, _bs((tm, tkw), lambda g, m, n, k: (m, g * (PD // tkw) + n)))])[0]
            du = _pool_bwd(f"d_pool_win_{i}", dpm, E)
            dproj = jnp.concatenate([du, dz], axis=1)
            dparts['pool_in_proj'] = [_mm_dw_cols(f"d_pool_in_proj_{i}", sv_['xn'], dproj)]
            dxn = _mm_colsharded_t(f"d_pool_xn_{i}", dproj, w_pin, j)
        dh, dh16, dnw = _norm_bwd(f"d_norm_{i}", dxn, sv_['h'], nw, dh)
        gsmall['norm_w'][i] = dnw.reshape(D)
    grad_x = dh.reshape(x.shape)
    dparts['out_proj'] = out_parts

    small_flat = jnp.concatenate([jnp.stack(gsmall[n]).reshape(-1) for n in SMALL])
    n_small = small_flat.shape[0]
    unit = 2 * N_CHIPS * 16 * LANES
    n_pad = -(-n_small // unit) * unit
    R = n_pad // (2 * N_CHIPS * LANES)
    small_part = jnp.pad(small_flat, (0, n_pad - n_small)).astype(BF16).reshape(2, N_CHIPS, R, LANES)
    parts, dests, buf_shapes = [], [], []
    for o, n in enumerate(BIG):
        ps = dparts[n]
        half = ps[0].shape[2:]
        buf_shapes.append((len(ps), 2, math.prod(half[:-1]), half[-1]))
        for li, pt in enumerate(ps):
            parts.append(pt)
            dests.append((o, li))
    parts.append(small_part)
    dests.append((len(BIG), 'chip'))
    buf_shapes.append((N_CHIPS, 2, R, LANES))
    red = _reduce_scatter(parts, dests, buf_shapes)
    grads = {n: r.reshape(w[n].shape) for n, r in zip(BIG, red[:len(BIG)])}
    small_all = _chip_allgather("gather_small_grads", [red[len(BIG)]])[0]
    small_all = jnp.transpose(small_all, (1, 0, 2, 3)).reshape(-1)[:n_small]
    off = 0
    p = 2 * lax.axis_index("x") + lax.axis_index("y")
    for n in SMALL:
        full_shape = (w[n].shape[0], E) if n in SMALL_SHARDED else w[n].shape
        size = math.prod(full_shape)
        gfull = small_all[off:off + size].reshape(full_shape)
        off += size
        if n in SMALL_SHARDED:
            gfull = lax.dynamic_slice_in_dim(gfull, p * (E // N_CHIPS), E // N_CHIPS, axis=1)
        grads[n] = gfull

    delta, new_m, new_v = {}, {}, {}
    for n in BIG:
        f2 = lambda a: a.reshape(-1, a.shape[-1])
        d_, m_, v_ = _adamw(f"adamw_{n}", f2(w[n]), f2(grads[n]), f2(mom_m[n]), f2(mom_v[n]))
        delta[n], new_m[n], new_v[n] = d_.reshape(w[n].shape), m_.reshape(w[n].shape), v_.reshape(w[n].shape)
    sizes = [math.prod(w[n].shape) for n in SMALL]
    tot = sum(sizes)
    tot_pad = -(-tot // (256 * LANES)) * (256 * LANES)
    pack = lambda d: jnp.pad(jnp.concatenate([d[n].reshape(-1) for n in SMALL]), (0, tot_pad - tot)).reshape(-1, LANES)
    d_, m_, v_ = _adamw("adamw_small", pack(w), pack(grads), pack(mom_m), pack(mom_v))
    off = 0
    for n, size in zip(SMALL, sizes):
        for src, dst in ((d_, delta), (m_, new_m), (v_, new_v)):
            dst[n] = src.reshape(-1)[off:off + size].reshape(w[n].shape)
        off += size
    return (loss, grad_x, *[grads[n] for n in ORDER], *[delta[n] for n in ORDER], *[new_m[n] for n in ORDER], *[new_v[n] for n in ORDER])
```

```python
import functools
import math

import jax
import jax.numpy as jnp
from jax import lax
from jax.experimental import pallas as pl
from jax.experimental.pallas import tpu as pltpu

F32 = jnp.float32
BF16 = jnp.bfloat16
MESH = pl.DeviceIdType.MESH

N_CHIPS = 4
VMEM_LIMIT = 56 * 1024 * 1024
LANES = 128
SUB = 8

EPS = 1e-6
S5_GROUP = 16
S5_STATE = 64
GROUPS_PER_CHUNK = 16
FOX_HEAD_DIM = 128
ATTN_SUB = 256
POOL_WINDOWS = (2, 4, 8, 16)
POOL_HALO = 16
ADAM_LR, ADAM_B1, ADAM_B2, ADAM_EPS, ADAM_WD, ADAM_STEP = 0.001, 0.9, 0.999, 1e-08, 0.01, 10
NEG = -1e30
K_STEP = 2048


ANY = pl.BlockSpec(memory_space=pl.ANY)


def _t(pref, dim):
    if dim <= pref:
        return dim
    t = pref - pref % 16
    while t > 16 and dim % t:
        t -= 16
    assert dim % t == 0, (pref, dim)
    return t


def _params(sem):
    return pltpu.CompilerParams(dimension_semantics=sem, vmem_limit_bytes=VMEM_LIMIT)


def _sigmoid(x):
    return 1.0 / (1.0 + jnp.exp(-x))


def _silu(z):
    return z * _sigmoid(z)


def _dsilu(z):
    s = _sigmoid(z)
    return s * (1.0 + z * (1.0 - s))


_GELU_C = math.sqrt(2.0 / math.pi)


def _gelu(x):
    return 0.5 * x * (1.0 + jnp.tanh(_GELU_C * (x + 0.044715 * (x * x * x))))


def _dgelu(x):
    t = jnp.tanh(_GELU_C * (x + 0.044715 * (x * x * x)))
    return 0.5 * (1.0 + t) + 0.5 * x * (1.0 - t * t) * (_GELU_C * (1.0 + 3.0 * 0.044715 * x * x))


def _log_sigmoid(x):
    return jnp.minimum(x, 0.0) - jnp.log(1.0 + jnp.exp(-jnp.abs(x)))


def _rms(x):
    return lax.rsqrt(jnp.mean(x * x, axis=-1, keepdims=True) + EPS)


def _rms_bwd(x, w, dy):
    r = _rms(x)
    xhat = x * r
    dxh = dy * w
    dx = r * (dxh - xhat * jnp.mean(dxh * xhat, axis=-1, keepdims=True))
    return dx, dy * xhat


def _rows(name, fn, ins, outs, tr, pre=None, into=None):
    rows = None
    for arr, kind, cols, cb in ins:
        if kind == 'r':
            rows = arr.shape[0]
        elif kind == 's' and rows is None:
            rows = arr.shape[1]
    tr = _t(tr, rows)
    n_in = len(ins)
    has_acc = any(o[0] == 'a' for o in outs)

    def spec(kind, cols, cb):
        if kind == 'r':
            return pl.BlockSpec((tr, cols), lambda r, *p: (r, cb))
        if kind == 'b':
            return pl.BlockSpec((1, cols), lambda r, *p: (0, cb))
        return pl.BlockSpec((None, tr, cols), lambda r, p: (p[cb], r, 0))

    in_specs = [spec(kind, cols, cb) for _, kind, cols, cb in ins]
    out_specs, out_shape = [], []
    for o in outs:
        if o[0] == 'r':
            out_specs.append(pl.BlockSpec((tr, o[1]), lambda r, *p: (r, 0)))
            out_shape.append(jax.ShapeDtypeStruct((rows, o[1]), o[2]))
        elif o[0] == 'a':
            out_specs.append(pl.BlockSpec((1, o[1]), lambda r, *p: (0, 0)))
            out_shape.append(jax.ShapeDtypeStruct((1, o[1]), o[2]))
        else:
            blk = tuple(tr if d == 'tr' else d for d in o[3])
            out_specs.append(pl.BlockSpec(blk, o[4]))
            out_shape.append(jax.ShapeDtypeStruct(o[1], o[2]))
    n_pre = 0 if pre is None else 1
    args = [a[0] for a in ins]
    aliases = {}
    if into is not None:
        in_specs.append(ANY)
        args.append(into)
        aliases = {n_pre + n_in: 0}
    n_all = len(args)

    def body(*refs):
        refs = refs[n_pre:]
        res = fn(*[r[...] for r in refs[:n_in]])
        for spec_o, o, v in zip(outs, refs[n_all:], res):
            if spec_o[0] == 'a':
                @pl.when(pl.program_id(0) == 0)
                def _():
                    o[...] = jnp.zeros_like(o)
                o[...] += v.astype(o.dtype)
            else:
                o[...] = v.astype(o.dtype)

    grid_spec = pltpu.PrefetchScalarGridSpec(num_scalar_prefetch=n_pre, grid=(rows // tr,), in_specs=in_specs, out_specs=out_specs)
    if pre is not None:
        args = [pre] + args
    return pl.pallas_call(body, name=name, grid_spec=grid_spec, out_shape=out_shape, input_output_aliases=aliases,
                          compiler_params=_params(("arbitrary" if has_acc else "parallel",)))(*args)


def _colsum(v):
    return jnp.sum(v, axis=0, keepdims=True)


def _mm(name, a, b, *, M, N, K, tm, tn, tk, a_spec, b_spec, outs, epi=None, extras=(), groups=1, ta=False, tb=False):
    nk = K // tk
    assert M % tm == 0 and N % tn == 0 and K % tk == 0, (name, M, N, K, tm, tn, tk)
    dims = (((0 if ta else 1,), (1 if tb else 0,)), ((), ()))
    n_ex = len(extras)

    def body(*refs):
        a_ref, b_ref = refs[0], refs[1]
        ex = refs[2:2 + n_ex]
        out_refs = refs[2 + n_ex:2 + n_ex + len(outs)]

        def finish(r):
            res = (r,) if epi is None else epi(r, *[e[...] for e in ex])
            for o, v in zip(out_refs, res):
                o[...] = v.astype(o.dtype)

        part = lax.dot_general(a_ref[...].astype(BF16), b_ref[...].astype(BF16), dims, preferred_element_type=F32)
        if nk == 1:
            finish(part)
            return
        acc = refs[-1]
        k = pl.program_id(3)

        @pl.when(k == 0)
        def _():
            acc[...] = part

        @pl.when(k > 0)
        def _():
            acc[...] += part

        @pl.when(k == nk - 1)
        def _():
            finish(acc[...])

    return pl.pallas_call(
        body, name=name, grid=(groups, M // tm, N // tn, nk),
        in_specs=[a_spec, b_spec] + [s for _, s in extras],
        out_specs=[s for _, _, s in outs],
        out_shape=[jax.ShapeDtypeStruct(sh, dt) for sh, dt, _ in outs],
        scratch_shapes=[] if nk == 1 else [pltpu.VMEM((tm, tn), F32)],
        compiler_params=_params(("parallel", "parallel", "parallel", "arbitrary")),
    )(a, b, *[e for e, _ in extras])


def _bs(shape, f):
    return pl.BlockSpec(shape, f)


def _tile(tm, tn, coff=0):
    return _bs((tm, tn), lambda g, m, n, k: (m, n + coff))


def _rowvec(tn, coff=0):
    return _bs((1, tn), lambda g, m, n, k: (0, n + coff))


def _mm_proj(name, xn, w, j, *, epi=None, extras=(), out_dtype=F32):
    T, D = xn.shape
    sw = w.shape[3]
    N = N_CHIPS * sw
    tm, tn, tk = _t(512, T), _t(1024, sw), _t(K_STEP, D)
    nb = sw // tn
    return _mm(name, xn, w, M=T, N=N, K=D, tm=tm, tn=tn, tk=tk,
               a_spec=_bs((tm, tk), lambda g, m, n, k: (m, k)),
               b_spec=_bs((None, None, tk, tn), lambda g, m, n, k: (n // nb, j, k, n % nb)),
               outs=[((T, N), out_dtype, _tile(tm, tn))], epi=epi, extras=extras)[0]


def _mm_plain(name, a, b, *, out_dtype=F32, epi=None, extras=(), outs=None, tn_pref=1024):
    M, K = a.shape
    N = b.shape[1]
    tm, tn, tk = _t(512, M), _t(tn_pref, N), _t(K_STEP, K)
    if outs is None:
        outs = [((M, N), out_dtype, _tile(tm, tn))]
    return _mm(name, a, b, M=M, N=N, K=K, tm=tm, tn=tn, tk=tk,
               a_spec=_bs((tm, tk), lambda g, m, n, k: (m, k)),
               b_spec=_bs((tk, tn), lambda g, m, n, k: (k, n)),
               outs=outs, epi=epi, extras=extras)


def _mm_rowsharded(name, a, w, i, *, epi, extras, outs_fn):
    T, E = a.shape
    tk = w.shape[2]
    N = w.shape[3]
    tm, tn = _t(512, T), _t(1024, N)
    return _mm(name, a, w, M=T, N=N, K=E, tm=tm, tn=tn, tk=tk,
               a_spec=_bs((tm, tk), lambda g, m, n, k: (m, k)),
               b_spec=_bs((None, None, tk, tn), lambda g, m, n, k: (k, i, 0, n)),
               outs=outs_fn(tm, tn), epi=epi, extras=extras(tm, tn))


def _mm_rowsharded_t(name, d, w, i, *, epi, extras, outs_fn):
    T, N = d.shape
    tn = w.shape[2]
    E = N_CHIPS * tn
    tm, tk = _t(512, T), _t(K_STEP, N)
    return _mm(name, d, w, M=T, N=E, K=N, tm=tm, tn=tn, tk=tk, tb=True,
               a_spec=_bs((tm, tk), lambda g, m, n, k: (m, k)),
               b_spec=_bs((None, None, tn, tk), lambda g, m, n, k: (n, i, 0, k)),
               outs=outs_fn(tm, tn), epi=epi, extras=extras(tm, tn))


def _mm_colsharded_t(name, d, w, j):
    T, N = d.shape
    D, sw = w.shape[2], w.shape[3]
    tm, tn, tk = _t(512, T), _t(1024, D), _t(1024, sw)
    kb = sw // tk
    return _mm(name, d, w, M=T, N=D, K=N, tm=tm, tn=tn, tk=tk, tb=True,
               a_spec=_bs((tm, tk), lambda g, m, n, k: (m, k)),
               b_spec=_bs((None, None, tn, tk), lambda g, m, n, k: (k // kb, j, n, k % kb)),
               outs=[((T, D), F32, _tile(tm, tn))])[0]


def _mm_dw_rows(name, a, d):
    T, E = a.shape
    N = d.shape[1]
    tm, tn, tk = E // (2 * N_CHIPS), _t(2048, N), _t(K_STEP, T)
    return _mm(name, a, d, M=E, N=N, K=T, tm=tm, tn=tn, tk=tk, ta=True,
               a_spec=_bs((tk, tm), lambda g, m, n, k: (k, m)),
               b_spec=_bs((tk, tn), lambda g, m, n, k: (k, n)),
               outs=[((2, N_CHIPS, tm, N), BF16, _bs((None, None, tm, tn), lambda g, m, n, k: (m % 2, m // 2, 0, n)))])[0]


def _mm_dw_cols(name, xn, d):
    T, D = xn.shape
    N = d.shape[1]
    sw = N // N_CHIPS
    tm, tn, tk = _t(512, D // 2), _t(1024, sw), _t(K_STEP, T)
    mh, nb = (D // 2) // tm, sw // tn
    return _mm(name, xn, d, M=D, N=N, K=T, tm=tm, tn=tn, tk=tk, ta=True,
               a_spec=_bs((tk, tm), lambda g, m, n, k: (k, m)),
               b_spec=_bs((tk, tn), lambda g, m, n, k: (k, n)),
               outs=[((2, N_CHIPS, D // 2, sw), BF16,
                      _bs((None, None, tm, tn), lambda g, m, n, k: (m // mh, n // nb, m % mh, n % nb)))])[0]


def _norm_fwd(name, h, w):
    D = h.shape[1]
    return _rows(name, lambda x, g: ((x * _rms(x)) * g,), [(h, 'r', D, 0), (w, 'b', D, 0)], [('r', D, BF16)], 256)[0]


def _norm_bwd(name, dxn, h, w, dh):
    D = h.shape[1]

    def fn(dy, x, g, up):
        dx, dwt = _rms_bwd(x, g, dy)
        r = up + dx
        return r, r, _colsum(dwt)

    return _rows(name, fn, [(dxn, 'r', D, 0), (h, 'r', D, 0), (w, 'b', D, 0), (dh, 'r', D, 0)],
                 [('r', D, F32), ('r', D, BF16), ('a', D, F32)], 256)


def _loss(h, target):
    D = h.shape[1]

    def fn(y, t):
        e = y - t
        d = e * (1.0 / D)
        return d, d, _colsum(e * e) * (0.5 / D)

    return _rows("loss", fn, [(h, 'r', D, 0), (target, 'r', D, 0)], [('r', D, F32), ('r', D, BF16), ('a', D, F32)], 256)


def _adamw(name, w, g, m, v):
    cols = w.shape[1]

    def fn(w, g, m, v):
        m = ADAM_B1 * m + (1.0 - ADAM_B1) * g
        v = ADAM_B2 * v + (1.0 - ADAM_B2) * (g * g)
        m_hat = m / (1.0 - ADAM_B1 ** ADAM_STEP)
        v_hat = v / (1.0 - ADAM_B2 ** ADAM_STEP)
        delta = -ADAM_LR * (m_hat / (jnp.sqrt(v_hat) + ADAM_EPS) + ADAM_WD * w)
        return delta, m, v

    return _rows(name, fn, [(x, 'r', cols, 0) for x in (w, g, m, v)], [('r', cols, F32)] * 3, 256)


def _s5_disc(a_re, a_im, log_dt):
    dt = jnp.exp(log_dt)
    mag = jnp.exp(a_re * dt)
    abar_r = mag * jnp.cos(a_im * dt)
    abar_i = mag * jnp.sin(a_im * dt)
    den = a_re * a_re + a_im * a_im
    xr = abar_r - 1.0
    fr = (xr * a_re + abar_i * a_im) / den
    fi = (abar_i * a_re - xr * a_im) / den
    return abar_r, abar_i, fr, fi


def _s5_disc_fwd(name, a_re, a_im, log_dt):
    G, P = a_re.shape

    def body(ar, ai, ld, o0, o1, o2, o3):
        for o, v in zip((o0, o1, o2, o3), _s5_disc(ar[...], ai[...], ld[...])):
            o[...] = v

    return pl.pallas_call(body, name=name, out_shape=[jax.ShapeDtypeStruct((G, P), F32)] * 4)(a_re, a_im, log_dt)


def _s5_disc_bwd(name, a_re, a_im, log_dt, cts):
    G, P = a_re.shape

    def body(ar, ai, ld, c0, c1, c2, c3, d0, d1, d2):
        _, vjp = jax.vjp(_s5_disc, ar[...], ai[...], ld[...])
        g0, g1, g2 = vjp((c0[...], c1[...], c2[...], c3[...]))
        d0[...] = g0
        d1[...] = g1
        d2[...] = g2

    return pl.pallas_call(body, name=name, out_shape=[jax.ShapeDtypeStruct((G, P), F32)] * 2 + [jax.ShapeDtypeStruct((G, 1), F32)])(
        a_re, a_im, log_dt, *cts)


def _s5_bbar(name, fr, fi, br, bi):
    return _rows(name, lambda fr, fi, br, bi: (fr * br - fi * bi, fr * bi + fi * br),
                 [(fr, 'r', 1, 0), (fi, 'r', 1, 0), (br, 'r', S5_GROUP, 0), (bi, 'r', S5_GROUP, 0)],
                 [('r', S5_GROUP, F32)] * 2, 2048)


def _s5_bbar_bwd(name, fr, fi, br, bi, dr, di):
    def fn(fr, fi, br, bi, dr, di):
        return (fr * dr + fi * di, fr * di - fi * dr,
                jnp.sum(br * dr + bi * di, axis=1, keepdims=True), jnp.sum(br * di - bi * dr, axis=1, keepdims=True))

    return _rows(name, fn, [(fr, 'r', 1, 0), (fi, 'r', 1, 0)] + [(x, 'r', S5_GROUP, 0) for x in (br, bi, dr, di)],
                 [('r', S5_GROUP, F32)] * 2 + [('r', 1, F32)] * 2, 2048)


def _scan_mults(m_ref, ar, ai, reverse):
    L = ar.shape[1]
    row = lax.broadcasted_iota(jnp.int32, (SUB, L), 0)
    if reverse:
        row = (SUB - 1) - row
    ar = jnp.broadcast_to(ar, (SUB, L))
    ai = jnp.broadcast_to(ai, (SUB, L))
    a2r, a2i = ar * ar - ai * ai, 2.0 * ar * ai
    a4r, a4i = a2r * a2r - a2i * a2i, 2.0 * a2r * a2i
    zero = jnp.zeros((SUB, L), F32)
    for s, (pr, pi, d) in enumerate(((ar, ai, 1), (a2r, a2i, 2), (a4r, a4i, 4))):
        m_ref[2 * s] = jnp.where(row >= d, pr, zero)
        m_ref[2 * s + 1] = jnp.where(row >= d, pi, zero)
    pr, pi = ar, ai
    for bit, (qr, qi) in ((1, (ar, ai)), (2, (a2r, a2i)), (4, (a4r, a4i))):
        on = (row & bit) != 0
        nr, ni = pr * qr - pi * qi, pr * qi + pi * qr
        pr, pi = jnp.where(on, nr, pr), jnp.where(on, ni, pi)
    m_ref[6] = pr
    m_ref[7] = pi


def _scan8(xr, xi, m_ref, cr, ci, reverse):
    for s, d in enumerate((1, 2, 4)):
        sh = (SUB - d) if reverse else d
        sr, si = pltpu.roll(xr, sh, 0), pltpu.roll(xi, sh, 0)
        mr, mi = m_ref[2 * s], m_ref[2 * s + 1]
        xr, xi = xr + mr * sr - mi * si, xi + mr * si + mi * sr
    pr, pi = m_ref[6], m_ref[7]
    return xr + pr * cr - pi * ci, xi + pr * ci + pi * cr


def _blockdiag_fill(bd_ref, c_ref, C, L):
    P = S5_STATE
    bd_ref[...] = jnp.zeros_like(bd_ref)
    for g in range(L // P):
        for half in (0, L):
            bd_ref[g * C:(g + 1) * C, half + g * P:half + (g + 1) * P] = c_ref[:, half + g * P:half + (g + 1) * P]


def _blockdiag_take(out_ref, dense_ref, C, L):
    P = S5_STATE
    for g in range(L // P):
        for half in (0, L):
            out_ref[:, half + g * P:half + (g + 1) * P] = dense_ref[g * C:(g + 1) * C, half + g * P:half + (g + 1) * P]


def _s5_fwd(name, proj, bbd, cbd, abar_r, abar_i, dskip, E):
    T = proj.shape[0]
    NC, C, L2 = bbd.shape
    L = L2 // 2
    CH = GROUPS_PER_CHUNK * C
    tT = _t(256, T)
    nt = (((1,), (1,)), ((), ()))

    def body(u_ref, bc_ref, cc_ref, ar_ref, ai_ref, d_ref, y_ref, g_ref, h_ref, bu, carry, mult, b_bd, c_bd):
        tb = pl.program_id(1)

        @pl.when(tb == 0)
        def _():
            carry[...] = jnp.zeros_like(carry)
            _blockdiag_fill(b_bd, bc_ref, C, L)
            _blockdiag_fill(c_bd, cc_ref, C, L)

        u = u_ref[...]
        bu[...] = jnp.dot(u.astype(BF16), b_bd[...], preferred_element_type=F32)
        _scan_mults(mult, ar_ref[...], ai_ref[...], False)

        def step(jb, c):
            cr, ci = c
            r0 = pl.multiple_of(jb * SUB, SUB)
            hr, hi = _scan8(bu[pl.ds(r0, SUB), 0:L], bu[pl.ds(r0, SUB), L:L2], mult, cr, ci, False)
            h_ref[pl.ds(r0, SUB), 0:L] = hr
            h_ref[pl.ds(r0, SUB), L:L2] = hi
            return (jnp.broadcast_to(hr[SUB - 1:SUB, :], (SUB, L)), jnp.broadcast_to(hi[SUB - 1:SUB, :], (SUB, L)))

        cr, ci = lax.fori_loop(0, tT // SUB, step, (carry[:, 0:L], carry[:, L:L2]))
        carry[:, 0:L] = cr
        carry[:, L:L2] = ci
        y1 = lax.dot_general(h_ref[...].astype(BF16), c_bd[...], nt, preferred_element_type=F32) + d_ref[...] * u
        y_ref[...] = y1
        g_ref[...] = _gelu(y1).astype(BF16)

    return pl.pallas_call(
        body, name=name, grid=(NC, T // tT),
        in_specs=[_bs((tT, CH), lambda c, t: (t, c)), _bs((None, C, L2), lambda c, t: (c, 0, 0)),
                  _bs((None, C, L2), lambda c, t: (c, 0, 0)), _bs((None, 1, L), lambda c, t: (c, 0, 0)),
                  _bs((None, 1, L), lambda c, t: (c, 0, 0)), _bs((1, CH), lambda c, t: (0, c))],
        out_specs=[_bs((tT, CH), lambda c, t: (t, c)), _bs((tT, CH), lambda c, t: (t, c)),
                   _bs((None, tT, L2), lambda c, t: (c, t, 0))],
        out_shape=[jax.ShapeDtypeStruct((T, E), F32), jax.ShapeDtypeStruct((T, E), BF16),
                   jax.ShapeDtypeStruct((NC, T, L2), F32)],
        scratch_shapes=[pltpu.VMEM((tT, L2), F32), pltpu.VMEM((SUB, L2), F32), pltpu.VMEM((8, SUB, L), F32),
                        pltpu.VMEM((CH, L2), BF16), pltpu.VMEM((CH, L2), BF16)],
        compiler_params=_params(("parallel", "arbitrary")),
    )(proj, bbd, cbd, abar_r, abar_i, dskip)


def _s5_bwd(name, dy1, proj, hs, bbd, cbd, abar_r, abar_i, dskip, E):
    T = proj.shape[0]
    NC, C, L2 = bbd.shape
    L = L2 // 2
    CH = GROUPS_PER_CHUNK * C
    tT = _t(256, T)
    nT = T // tT
    tn = (((0,), (0,)), ((), ()))
    nt = (((1,), (1,)), ((), ()))

    def body(dy_ref, u_ref, h_ref, bc_ref, cc_ref, ar_ref, ai_ref, d_ref, du_ref, db_ref, dc_ref, da_ref, dd_ref,
             gb, carry, mult, b_bd, c_bd, db_acc, dc_acc):
        tb = pl.program_id(1)

        @pl.when(tb == 0)
        def _():
            carry[...] = jnp.zeros_like(carry)
            db_acc[...] = jnp.zeros_like(db_acc)
            dc_acc[...] = jnp.zeros_like(dc_acc)
            da_ref[...] = jnp.zeros_like(da_ref)
            dd_ref[...] = jnp.zeros_like(dd_ref)
            _blockdiag_fill(b_bd, bc_ref, C, L)
            _blockdiag_fill(c_bd, cc_ref, C, L)

        dy = dy_ref[...]
        u = u_ref[...]
        dy16 = dy.astype(BF16)
        dc_acc[...] += lax.dot_general(dy16, h_ref[...].astype(BF16), tn, preferred_element_type=F32)
        gb[...] = jnp.dot(dy16, c_bd[...], preferred_element_type=F32)
        _scan_mults(mult, ar_ref[...], -ai_ref[...], True)
        row = lax.broadcasted_iota(jnp.int32, (SUB, L), 0)
        nblk = tT // SUB

        def step(jj, c):
            cr, ci, sr, si = c
            r0 = pl.multiple_of((nblk - 1 - jj) * SUB, SUB)
            gr, gi = _scan8(gb[pl.ds(r0, SUB), 0:L], gb[pl.ds(r0, SUB), L:L2], mult, cr, ci, True)
            gb[pl.ds(r0, SUB), 0:L] = gr
            gb[pl.ds(r0, SUB), L:L2] = gi
            nr = jnp.where(row == SUB - 1, cr, pltpu.roll(gr, SUB - 1, 0))
            ni = jnp.where(row == SUB - 1, ci, pltpu.roll(gi, SUB - 1, 0))
            hr, hi = h_ref[pl.ds(r0, SUB), 0:L], h_ref[pl.ds(r0, SUB), L:L2]
            sr = sr + nr * hr + ni * hi
            si = si + ni * hr - nr * hi
            return (jnp.broadcast_to(gr[0:1, :], (SUB, L)), jnp.broadcast_to(gi[0:1, :], (SUB, L)), sr, si)

        z = jnp.zeros((SUB, L), F32)
        cr, ci, sr, si = lax.fori_loop(0, nblk, step, (carry[:, 0:L], carry[:, L:L2], z, z))
        carry[:, 0:L] = cr
        carry[:, L:L2] = ci
        da_ref[:, 0:L] += sr
        da_ref[:, L:L2] += si
        g16 = gb[...].astype(BF16)
        du = lax.dot_general(g16, b_bd[...], nt, preferred_element_type=F32) + d_ref[...] * dy
        du_ref[...] = du.astype(BF16)
        db_acc[...] += lax.dot_general(u.astype(BF16), g16, tn, preferred_element_type=F32)
        dd_ref[...] += _colsum(dy * u)

        @pl.when(tb == nT - 1)
        def _():
            _blockdiag_take(db_ref, db_acc, C, L)
            _blockdiag_take(dc_ref, dc_acc, C, L)

    rev = lambda c, t: (nT - 1 - t, c)
    return pl.pallas_call(
        body, name=name, grid=(NC, nT),
        in_specs=[_bs((tT, CH), rev), _bs((tT, CH), rev), _bs((None, tT, L2), lambda c, t: (c, nT - 1 - t, 0)),
                  _bs((None, C, L2), lambda c, t: (c, 0, 0)), _bs((None, C, L2), lambda c, t: (c, 0, 0)),
                  _bs((None, 1, L), lambda c, t: (c, 0, 0)), _bs((None, 1, L), lambda c, t: (c, 0, 0)),
                  _bs((1, CH), lambda c, t: (0, c))],
        out_specs=[_bs((tT, CH), rev), _bs((None, C, L2), lambda c, t: (c, 0, 0)), _bs((None, C, L2), lambda c, t: (c, 0, 0)),
                   _bs((None, SUB, L2), lambda c, t: (c, 0, 0)), _bs((None, 1, CH), lambda c, t: (c, 0, 0))],
        out_shape=[jax.ShapeDtypeStruct((T, E), BF16), jax.ShapeDtypeStruct((NC, C, L2), F32),
                   jax.ShapeDtypeStruct((NC, C, L2), F32), jax.ShapeDtypeStruct((NC, SUB, L2), F32),
                   jax.ShapeDtypeStruct((NC, 1, CH), F32)],
        scratch_shapes=[pltpu.VMEM((tT, L2), F32), pltpu.VMEM((SUB, L2), F32), pltpu.VMEM((8, SUB, L), F32),
                        pltpu.VMEM((CH, L2), BF16), pltpu.VMEM((CH, L2), BF16), pltpu.VMEM((CH, L2), F32), pltpu.VMEM((CH, L2), F32)],
        compiler_params=_params(("parallel", "arbitrary")),
    )(dy1, proj, hs, bbd, cbd, abar_r, abar_i, dskip)


def _compact(v, NC):
    G, P, C = v.shape
    return jnp.transpose(v.reshape(NC, G // NC, P, C), (0, 3, 1, 2)).reshape(NC, C, (G // NC) * P)


def _uncompact(d, G):
    NC, C, L = d.shape
    gpc = G // NC
    return jnp.transpose(d.reshape(NC, C, gpc, L // gpc), (0, 2, 3, 1)).reshape(G, L // gpc, C)


def _cum_rows(name, x, bias, reverse, log_sig):
    T, L = x.shape

    def body(x_ref, b_ref, o_ref):
        row = lax.broadcasted_iota(jnp.int32, (SUB, L), 0)
        if reverse:
            row = (SUB - 1) - row
        nblk = T // SUB

        def step(jj, c):
            r0 = pl.multiple_of(((nblk - 1 - jj) if reverse else jj) * SUB, SUB)
            v = x_ref[pl.ds(r0, SUB), :] + b_ref[...]
            if log_sig:
                v = _log_sigmoid(v)
            for d in (1, 2, 4):
                v = v + jnp.where(row >= d, pltpu.roll(v, (SUB - d) if reverse else d, 0), 0.0)
            v = v + c
            o_ref[pl.ds(r0, SUB), :] = v
            e = 0 if reverse else SUB - 1
            return jnp.broadcast_to(v[e:e + 1, :], (SUB, L))

        lax.fori_loop(0, nblk, step, jnp.zeros((SUB, L), F32))

    return pl.pallas_call(body, name=name, out_shape=jax.ShapeDtypeStruct((T, L), F32),
                          compiler_params=pltpu.CompilerParams(vmem_limit_bytes=VMEM_LIMIT))(x, bias)


def _qk_norm(name, proj, wq, wk, H):
    T = proj.shape[0]
    Dh = FOX_HEAD_DIM
    tT = _t(512, T)

    def body(q_ref, k_ref, wq_ref, wk_ref, qn_ref, kn_ref):
        q, k = q_ref[...], k_ref[...]
        qn_ref[...] = ((q * _rms(q)) * wq_ref[...]).astype(BF16)
        kn_ref[...] = ((k * _rms(k)) * wk_ref[...]).astype(BF16)

    blk = lambda off: _bs((tT, Dh), lambda t, h: (t, h + off))
    return pl.pallas_call(
        body, name=name, grid=(T // tT, H),
        in_specs=[blk(0), blk(H), _bs((1, Dh), lambda t, h: (0, 0)), _bs((1, Dh), lambda t, h: (0, 0))],
        out_specs=[blk(0), blk(0)], out_shape=[jax.ShapeDtypeStruct((T, H * Dh), BF16)] * 2,
        compiler_params=_params(("parallel", "parallel")))(proj, proj, wq, wk)


def _qk_norm_bwd(name, proj, wq, wk, dqn, dkn, H):
    T = proj.shape[0]
    Dh = FOX_HEAD_DIM
    tT = _t(512, T)

    def body(q_ref, k_ref, wq_ref, wk_ref, dqn_ref, dkn_ref, dq_ref, dk_ref, dwq_ref, dwk_ref):
        @pl.when((pl.program_id(0) == 0) & (pl.program_id(1) == 0))
        def _():
            dwq_ref[...] = jnp.zeros_like(dwq_ref)
            dwk_ref[...] = jnp.zeros_like(dwk_ref)

        dq, tq = _rms_bwd(q_ref[...], wq_ref[...], dqn_ref[...])
        dk, tk = _rms_bwd(k_ref[...], wk_ref[...], dkn_ref[...])
        dq_ref[...] = dq.astype(BF16)
        dk_ref[...] = dk.astype(BF16)
        dwq_ref[...] += _colsum(tq)
        dwk_ref[...] += _colsum(tk)

    blk = lambda off: _bs((tT, Dh), lambda t, h: (t, h + off))
    one = _bs((1, Dh), lambda t, h: (0, 0))
    return pl.pallas_call(
        body, name=name, grid=(T // tT, H),
        in_specs=[blk(0), blk(H), one, one, blk(0), blk(0)],
        out_specs=[blk(0), blk(0), one, one],
        out_shape=[jax.ShapeDtypeStruct((T, H * Dh), BF16)] * 2 + [jax.ShapeDtypeStruct((1, Dh), F32)] * 2,
        compiler_params=_params(("arbitrary", "arbitrary")))(proj, proj, wq, wk, dqn, dkn)


def _attn_fwd(name, qn, kn, proj, cum_q, cum_k, H):
    T = qn.shape[0]
    Dh = FOX_HEAD_DIM
    tq = cum_k.shape[3]
    nq = T // tq
    scale = Dh ** -0.5
    nt = (((1,), (1,)), ((), ()))

    sq = _t(ATTN_SUB, tq)
    rep = tq // LANES

    def body(q_ref, k_ref, v_ref, cq_ref, ck_ref, o_ref, lse_ref, m_sc, l_sc, acc_sc):
        i = pl.program_id(1)
        m_sc[...] = jnp.full_like(m_sc, NEG)
        l_sc[...] = jnp.zeros_like(l_sc)
        acc_sc[...] = jnp.zeros_like(acc_sc)
        kloc = lax.broadcasted_iota(jnp.int32, (sq, tq), 1)
        qloc = lax.broadcasted_iota(jnp.int32, (sq, tq), 0)

        def chunk(kc, masked):
            ks = pl.multiple_of(kc * tq, tq)
            k = k_ref[pl.ds(ks, tq), :]
            v16 = v_ref[pl.ds(ks, tq), :].astype(BF16)
            ck = ck_ref[kc]
            for r in range(tq // sq):
                rows = pl.ds(r * sq, sq)
                s = lax.dot_general(q_ref[rows, :], k, nt, preferred_element_type=F32) * scale + (jnp.tile(cq_ref[rows, :], (1, rep)) - ck)
                if masked:
                    s = jnp.where(kloc <= qloc + r * sq, s, NEG)
                m_old = m_sc[rows, :]
                m_new = jnp.maximum(m_old, jnp.max(s, axis=1, keepdims=True))
                alpha = jnp.exp(m_old - m_new)
                p = jnp.exp(s - jnp.tile(m_new, (1, rep)))
                l_sc[rows, :] = alpha * l_sc[rows, :] + jnp.sum(p, axis=1, keepdims=True)
                acc_sc[rows, :] = alpha * acc_sc[rows, :] + jnp.dot(p.astype(BF16), v16, preferred_element_type=F32)
                m_sc[rows, :] = m_new

        def below(kc, c):
            chunk(kc, False)
            return c

        lax.fori_loop(0, i, below, 0)
        chunk(i, True)
        o_ref[...] = acc_sc[...] / l_sc[...]
        lse_ref[...] = m_sc[...] + jnp.log(l_sc[...])

    return pl.pallas_call(
        body, name=name, grid=(H, nq),
        in_specs=[_bs((tq, Dh), lambda h, i: (i, h)), _bs((T, Dh), lambda h, i: (0, h)), _bs((T, Dh), lambda h, i: (0, 2 * H + h)),
                  _bs((None, tq, LANES), lambda h, i: (h, i, 0)), _bs((None, nq, 1, tq), lambda h, i: (h, 0, 0, 0))],
        out_specs=[_bs((tq, Dh), lambda h, i: (i, h)), _bs((None, tq, LANES), lambda h, i: (h, i, 0))],
        out_shape=[jax.ShapeDtypeStruct((T, H * Dh), F32), jax.ShapeDtypeStruct((H, T, LANES), F32)],
        scratch_shapes=[pltpu.VMEM((tq, LANES), F32), pltpu.VMEM((tq, LANES), F32), pltpu.VMEM((tq, Dh), F32)],
        compiler_params=_params(("parallel", "parallel")))(qn, kn, proj, cum_q, cum_k)


def _attn_bwd(name, qn, kn, proj, do, o, lse, cum_q, cum_k, H):
    T = qn.shape[0]
    Dh = FOX_HEAD_DIM
    tq = cum_k.shape[3]
    nq = T // tq
    scale = Dh ** -0.5
    nt = (((1,), (1,)), ((), ()))
    tn = (((0,), (0,)), ((), ()))

    sq = _t(ATTN_SUB, tq)
    rep = tq // LANES

    def body(q_ref, k_ref, v_ref, do_ref, o_ref, lse_ref, cq_ref, ck_ref, dq_ref, dk_ref, dv_ref, dcq_ref, dck_ref,
             delta, cql, dk_sc, dv_sc, dck_sc):
        j = pl.program_id(1)

        @pl.when(j == 0)
        def _():
            dq_ref[...] = jnp.zeros_like(dq_ref)
            dcq_ref[...] = jnp.zeros_like(dcq_ref)
            delta[...] = jnp.broadcast_to(jnp.sum(do_ref[...] * o_ref[...], axis=1, keepdims=True), delta.shape)
            cql[...] = cq_ref[...] - lse_ref[...]

        dk_sc[...] = jnp.zeros_like(dk_sc)
        dv_sc[...] = jnp.zeros_like(dv_sc)
        dck_sc[...] = jnp.zeros_like(dck_sc)
        k = k_ref[...]
        v16 = v_ref[...].astype(BF16)
        ck = ck_ref[...]
        kloc = lax.broadcasted_iota(jnp.int32, (sq, tq), 1)
        qloc = lax.broadcasted_iota(jnp.int32, (sq, tq), 0)

        def qblk(i, masked):
            for r in range(tq // sq):
                rows = pl.ds(pl.multiple_of(i * tq + r * sq, sq), sq)
                q = q_ref[rows, :]
                do16 = do_ref[rows, :].astype(BF16)
                e = lax.dot_general(q, k, nt, preferred_element_type=F32) * scale + (jnp.tile(cql[rows, :], (1, rep)) - ck)
                p = jnp.exp(e)
                if masked:
                    p = jnp.where(kloc <= qloc + r * sq, p, 0.0)
                dv_sc[...] += lax.dot_general(p.astype(BF16), do16, tn, preferred_element_type=F32)
                dp = lax.dot_general(do16, v16, nt, preferred_element_type=F32)
                ds = p * (dp - jnp.tile(delta[rows, :], (1, rep)))
                ds16 = ds.astype(BF16)
                dk_sc[...] += lax.dot_general(ds16, q, tn, preferred_element_type=F32)
                dq_ref[rows, :] += jnp.dot(ds16, k, preferred_element_type=F32) * scale
                dcq_ref[rows, :] += jnp.broadcast_to(jnp.sum(ds, axis=1, keepdims=True), (sq, LANES))
                dck_sc[...] += jnp.sum(ds, axis=0, keepdims=True)

        def above(i, c):
            qblk(i, False)
            return c

        qblk(j, True)
        lax.fori_loop(j + 1, nq, above, 0)
        dk_ref[...] = dk_sc[...] * scale
        dv_ref[...] = dv_sc[...].astype(BF16)
        dck_ref[...] = -dck_sc[...]

    whole = lambda off: _bs((T, Dh), lambda h, j: (0, h + off))
    blk = lambda off: _bs((tq, Dh), lambda h, j: (j, h + off))
    return pl.pallas_call(
        body, name=name, grid=(H, nq),
        in_specs=[whole(0), blk(0), blk(2 * H), whole(0), whole(0), _bs((None, T, LANES), lambda h, j: (h, 0, 0)),
                  _bs((None, T, LANES), lambda h, j: (h, 0, 0)), _bs((None, None, 1, tq), lambda h, j: (h, j, 0, 0))],
        out_specs=[whole(0), blk(0), blk(0), _bs((None, T, LANES), lambda h, j: (h, 0, 0)),
                   _bs((None, None, 1, tq), lambda h, j: (h, j, 0, 0))],
        out_shape=[jax.ShapeDtypeStruct((T, H * Dh), F32), jax.ShapeDtypeStruct((T, H * Dh), F32), jax.ShapeDtypeStruct((T, H * Dh), BF16),
                   jax.ShapeDtypeStruct((H, T, LANES), F32), jax.ShapeDtypeStruct((H, nq, 1, tq), F32)],
        scratch_shapes=[pltpu.VMEM((T, LANES), F32), pltpu.VMEM((T, LANES), F32), pltpu.VMEM((tq, Dh), F32), pltpu.VMEM((tq, Dh), F32),
                        pltpu.VMEM((1, tq), F32)],
        compiler_params=_params(("parallel", "arbitrary")))(qn, kn, proj, do, o, lse, cum_q, cum_k)


def _pool_fwd(name, proj, E):
    T = proj.shape[0]
    PG = len(POOL_WINDOWS)
    PD = E // PG
    tT = _t(256, T)
    hb = tT // POOL_HALO

    def body(u_ref, halo_ref, o_ref, buf):
        g, tb = pl.program_id(0), pl.program_id(1)
        u = u_ref[...]
        buf[pl.ds(POOL_HALO, tT), :] = u
        buf[pl.ds(0, POOL_HALO), :] = jnp.where(tb == 0, 0.0, halo_ref[...])
        t = tb * tT + lax.broadcasted_iota(jnp.int32, (tT, 1), 0)
        for gi, w in enumerate(POOL_WINDOWS):
            @pl.when(g == gi)
            def _():
                acc = u
                for d in range(1, w):
                    acc = acc + buf[pl.ds(POOL_HALO - d, tT), :]
                cnt = jnp.minimum(t + 1, w).astype(F32)
                o_ref[...] = (acc / cnt - u).astype(BF16)

    return pl.pallas_call(
        body, name=name, grid=(PG, T // tT),
        in_specs=[_bs((tT, PD), lambda g, t: (t, g)), _bs((POOL_HALO, PD), lambda g, t: (jnp.maximum(t * hb - 1, 0), g))],
        out_specs=_bs((tT, PD), lambda g, t: (t, g)), out_shape=jax.ShapeDtypeStruct((T, E), BF16),
        scratch_shapes=[pltpu.VMEM((tT + POOL_HALO, PD), F32)],
        compiler_params=_params(("parallel", "parallel")))(proj, proj)


def _pool_bwd(name, dpm, E):
    T = dpm.shape[0]
    PG = len(POOL_WINDOWS)
    PD = E // PG
    tT = _t(256, T)
    hb = tT // POOL_HALO
    nT = T // tT

    def body(d_ref, halo_ref, o_ref, buf):
        g, tb = pl.program_id(0), pl.program_id(1)
        d = d_ref[...]
        t = tb * tT + lax.broadcasted_iota(jnp.int32, (tT, 1), 0)
        th = (tb + 1) * tT + lax.broadcasted_iota(jnp.int32, (POOL_HALO, 1), 0)
        for gi, w in enumerate(POOL_WINDOWS):
            @pl.when(g == gi)
            def _():
                dn = d / jnp.minimum(t + 1, w).astype(F32)
                buf[pl.ds(0, tT), :] = dn
                buf[pl.ds(tT, POOL_HALO), :] = jnp.where(tb == nT - 1, 0.0, halo_ref[...] / jnp.minimum(th + 1, w).astype(F32))
                acc = dn
                for s in range(1, w):
                    acc = acc + buf[pl.ds(s, tT), :]
                o_ref[...] = (acc - d).astype(BF16)

    return pl.pallas_call(
        body, name=name, grid=(PG, nT),
        in_specs=[_bs((tT, PD), lambda g, t: (t, g)), _bs((POOL_HALO, PD), lambda g, t: (jnp.minimum((t + 1) * hb, T // POOL_HALO - 1), g))],
        out_specs=_bs((tT, PD), lambda g, t: (t, g)), out_shape=jax.ShapeDtypeStruct((T, E), BF16),
        scratch_shapes=[pltpu.VMEM((tT + POOL_HALO, PD), F32)],
        compiler_params=_params(("parallel", "parallel")))(dpm, dpm)


def _coords():
    x, y, c = lax.axis_index("x"), lax.axis_index("y"), lax.axis_index("c")
    chips = [(1 - x, y), (x, 1 - y), (1 - x, 1 - y)]
    return x, y, c, 2 * x + y, (x, y, 1 - c), chips


def _chip_allgather(name, bufs):
    n = len(bufs)

    def body(*refs):
        outs = refs[n:2 * n]
        send, recv, fsend, frecv = refs[2 * n:]
        x, y, c, p, sib, chips = _coords()

        def direct(t, j, chip):
            return pltpu.make_async_remote_copy(src_ref=outs[t].at[p, c], dst_ref=outs[t].at[p, c], send_sem=send.at[t, j],
                                                recv_sem=recv.at[t, j], device_id=(*chip, c), device_id_type=MESH)

        def landed(t, j, chip):
            blk = outs[t].at[2 * chip[0] + chip[1], c]
            return pltpu.make_async_remote_copy(src_ref=blk, dst_ref=blk, send_sem=send.at[t, j],
                                                recv_sem=recv.at[t, j], device_id=(*chip, c), device_id_type=MESH)

        def passed(t, j, chip, half):
            blk = outs[t].at[2 * chip[0] + chip[1], half]
            return pltpu.make_async_remote_copy(src_ref=blk, dst_ref=blk, send_sem=fsend.at[t, j], recv_sem=frecv.at[t, j],
                                                device_id=sib, device_id_type=MESH)

        first = [direct(t, j, chip) for t in range(n) for j, chip in enumerate(chips)]
        for cp in first:
            cp.start()
        fwd = []
        for j, chip in enumerate(chips):
            for t in range(n):
                landed(t, j, chip).wait_recv()
                f = passed(t, j, chip, c)
                f.start()
                fwd.append(f)
        for j, chip in enumerate(chips):
            for t in range(n):
                passed(t, j, chip, 1 - c).wait_recv()
        for cp in first + fwd:
            cp.wait_send()

    return pl.pallas_call(
        body, name=name, in_specs=[ANY] * n, out_specs=[ANY] * n,
        out_shape=[jax.ShapeDtypeStruct(a.shape, a.dtype) for a in bufs],
        input_output_aliases={t: t for t in range(n)},
        scratch_shapes=[pltpu.SemaphoreType.DMA((n, 3))] * 4,
    )(*bufs)


def _pair_exchange(name, parts):
    n = len(parts)

    def body(*refs):
        ins, outs = refs[:n], refs[n:2 * n]
        send, recv = refs[2 * n:]
        x, y, c, p, sib, chips = _coords()
        cps = [pltpu.make_async_remote_copy(src_ref=ins[t].at[1 - c], dst_ref=outs[t], send_sem=send.at[t], recv_sem=recv.at[t],
                                            device_id=sib, device_id_type=MESH) for t in range(n)]
        for cp in cps:
            cp.start()
        for cp in cps:
            cp.wait()

    return pl.pallas_call(
        body, name=name, in_specs=[ANY] * n, out_specs=[ANY] * n,
        out_shape=[jax.ShapeDtypeStruct(a.shape[1:], a.dtype) for a in parts],
        scratch_shapes=[pltpu.SemaphoreType.DMA((n,))] * 2,
    )(*parts)


def _chip_exchange(name, sums):
    n = len(sums)

    def body(*refs):
        ins, outs = refs[:n], refs[n:2 * n]
        send, recv = refs[2 * n:]
        x, y, c, p, sib, chips = _coords()
        cps = [pltpu.make_async_remote_copy(src_ref=ins[t].at[2 * chip[0] + chip[1]], dst_ref=outs[t].at[j], send_sem=send.at[t, j],
                                            recv_sem=recv.at[t, j], device_id=(*chip, c), device_id_type=MESH)
               for t in range(n) for j, chip in enumerate(chips)]
        for cp in cps:
            cp.start()
        for cp in cps:
            cp.wait()

    return pl.pallas_call(
        body, name=name, in_specs=[ANY] * n, out_specs=[ANY] * n,
        out_shape=[jax.ShapeDtypeStruct((3,) + a.shape[1:], a.dtype) for a in sums],
        scratch_shapes=[pltpu.SemaphoreType.DMA((n, 3))] * 2,
    )(*sums)


def _pair_share(name, bufs, items):
    n = len(items)
    nb = len(bufs)

    def body(*refs):
        outs = refs[nb:2 * nb]
        send, recv = refs[2 * nb:]
        x, y, c, p, sib, chips = _coords()

        def blk(t, half):
            o, lead = items[t]
            return outs[o].at[p if lead == 'chip' else lead, half]

        def swap(t, half):
            return pltpu.make_async_remote_copy(src_ref=blk(t, half), dst_ref=blk(t, half), send_sem=send.at[t], recv_sem=recv.at[t],
                                                device_id=sib, device_id_type=MESH)

        cps = [swap(t, c) for t in range(n)]
        for cp in cps:
            cp.start()
        for t in range(n):
            swap(t, 1 - c).wait_recv()
        for cp in cps:
            cp.wait_send()

    return pl.pallas_call(
        body, name=name, in_specs=[ANY] * nb, out_specs=[ANY] * nb,
        out_shape=[jax.ShapeDtypeStruct(b.shape, b.dtype) for b in bufs],
        input_output_aliases={t: t for t in range(nb)},
        scratch_shapes=[pltpu.SemaphoreType.DMA((n,))] * 2,
    )(*bufs)


def _flat2(a, lead):
    return a.reshape(a.shape[:lead] + (-1, a.shape[-1]))


def _reduce_scatter(parts, dests, buf_shapes):
    c = lax.axis_index("c").astype(jnp.int32)
    p = (2 * lax.axis_index("x") + lax.axis_index("y")).astype(jnp.int32)
    got = _pair_exchange("rs_pair_exchange", parts)
    sums = []
    for t, (mine, theirs) in enumerate(zip(parts, got)):
        m3, t2 = _flat2(mine, 1), theirs.reshape(-1, theirs.shape[-1])
        m3 = m3.reshape(2, -1, m3.shape[-1])
        cols = t2.shape[1]
        s = _rows(f"rs_pair_sum_{t}", lambda a, b: (a.astype(F32) + b.astype(F32),),
                  [(m3, 's', cols, 0), (t2, 'r', cols, 0)], [('r', cols, BF16)], 512, pre=c.reshape(1))[0]
        sums.append(s.reshape(theirs.shape))
    got = _chip_exchange("rs_chip_exchange", sums)
    bufs = [None] * len(buf_shapes)
    for t, (mine, theirs) in enumerate(zip(sums, got)):
        o, lead = dests[t]
        shape = buf_shapes[o]
        rows, cols = shape[2], shape[3]
        m3, t3 = mine.reshape(N_CHIPS, rows, cols), theirs.reshape(3, rows, cols)
        pre = jnp.stack([p, jnp.int32(0), jnp.int32(1), jnp.int32(2), c, p if lead == 'chip' else jnp.int32(lead)])
        out = ('x', shape, F32, (None, None, 'tr', cols), lambda r, pr: (pr[5], pr[4], r, 0))
        bufs[o] = _rows(f"rs_chip_sum_{t}", lambda a, b0, b1, b2: (((a.astype(F32) + b0.astype(F32)) + b1.astype(F32)) + b2.astype(F32),),
                        [(m3, 's', cols, 0), (t3, 's', cols, 1), (t3, 's', cols, 2), (t3, 's', cols, 3)], [out], 512, pre=pre, into=bufs[o])[0]
    return _pair_share("rs_pair_share", bufs, dests)


def kernel(x, norm_w, out_proj, s5_in_proj, s5_a_re, s5_a_im, s5_log_dt, s5_b_re, s5_b_im, s5_c_re, s5_c_im, s5_d, s5_w_glu, s5_b_glu, fox_in_proj, fox_q_norm, fox_k_norm, fox_f_bias, pool_in_proj, pool_w_group, pool_scale, loss_target, m_norm_w, m_out_proj, m_s5_in_proj, m_s5_a_re, m_s5_a_im, m_s5_log_dt, m_s5_b_re, m_s5_b_im, m_s5_c_re, m_s5_c_im, m_s5_d, m_s5_w_glu, m_s5_b_glu, m_fox_in_proj, m_fox_q_norm, m_fox_k_norm, m_fox_f_bias, m_pool_in_proj, m_pool_w_group, m_pool_scale, v_norm_w, v_out_proj, v_s5_in_proj, v_s5_a_re, v_s5_a_im, v_s5_log_dt, v_s5_b_re, v_s5_b_im, v_s5_c_re, v_s5_c_im, v_s5_d, v_s5_w_glu, v_s5_b_glu, v_fox_in_proj, v_fox_q_norm, v_fox_k_norm, v_fox_f_bias, v_pool_in_proj, v_pool_w_group, v_pool_scale):
    weights = dict(norm_w=norm_w, out_proj=out_proj, s5_in_proj=s5_in_proj, s5_a_re=s5_a_re, s5_a_im=s5_a_im, s5_log_dt=s5_log_dt,
                   s5_b_re=s5_b_re, s5_b_im=s5_b_im, s5_c_re=s5_c_re, s5_c_im=s5_c_im, s5_d=s5_d, s5_w_glu=s5_w_glu, s5_b_glu=s5_b_glu,
                   fox_in_proj=fox_in_proj, fox_q_norm=fox_q_norm, fox_k_norm=fox_k_norm, fox_f_bias=fox_f_bias,
                   pool_in_proj=pool_in_proj, pool_w_group=pool_w_group, pool_scale=pool_scale)
    mom_m = dict(norm_w=m_norm_w, out_proj=m_out_proj, s5_in_proj=m_s5_in_proj, s5_a_re=m_s5_a_re, s5_a_im=m_s5_a_im, s5_log_dt=m_s5_log_dt,
                 s5_b_re=m_s5_b_re, s5_b_im=m_s5_b_im, s5_c_re=m_s5_c_re, s5_c_im=m_s5_c_im, s5_d=m_s5_d, s5_w_glu=m_s5_w_glu, s5_b_glu=m_s5_b_glu,
                 fox_in_proj=m_fox_in_proj, fox_q_norm=m_fox_q_norm, fox_k_norm=m_fox_k_norm, fox_f_bias=m_fox_f_bias,
                 pool_in_proj=m_pool_in_proj, pool_w_group=m_pool_w_group, pool_scale=m_pool_scale)
    mom_v = dict(norm_w=v_norm_w, out_proj=v_out_proj, s5_in_proj=v_s5_in_proj, s5_a_re=v_s5_a_re, s5_a_im=v_s5_a_im, s5_log_dt=v_s5_log_dt,
                 s5_b_re=v_s5_b_re, s5_b_im=v_s5_b_im, s5_c_re=v_s5_c_re, s5_c_im=v_s5_c_im, s5_d=v_s5_d, s5_w_glu=v_s5_w_glu, s5_b_glu=v_s5_b_glu,
                 fox_in_proj=v_fox_in_proj, fox_q_norm=v_fox_q_norm, fox_k_norm=v_fox_k_norm, fox_f_bias=v_fox_f_bias,
                 pool_in_proj=v_pool_in_proj, pool_w_group=v_pool_w_group, pool_scale=v_pool_scale)
    return _step(x, loss_target, weights, mom_m, mom_v)


BIG = ('out_proj', 's5_in_proj', 's5_w_glu', 'fox_in_proj', 'pool_in_proj', 'pool_w_group')
SMALL = ('norm_w', 's5_a_re', 's5_a_im', 's5_log_dt', 's5_b_re', 's5_b_im', 's5_c_re', 's5_c_im', 's5_d', 's5_b_glu',
         'fox_q_norm', 'fox_k_norm', 'fox_f_bias', 'pool_scale')
SMALL_SHARDED = ('s5_d', 's5_b_glu', 'pool_scale')
GROUP_AXIS_1 = ('s5_a_re', 's5_a_im', 's5_b_re', 's5_b_im', 's5_c_re', 's5_c_im')
ORDER = ('norm_w', 'out_proj', 's5_in_proj', 's5_a_re', 's5_a_im', 's5_log_dt', 's5_b_re', 's5_b_im', 's5_c_re', 's5_c_im', 's5_d',
         's5_w_glu', 's5_b_glu', 'fox_in_proj', 'fox_q_norm', 'fox_k_norm', 'fox_f_bias', 'pool_in_proj', 'pool_w_group', 'pool_scale')


def _split2(shape):
    if shape[0] % 2 == 0:
        return (2, shape[0] // 2) + tuple(shape[1:])
    assert shape[0] == 1 and shape[1] % 2 == 0
    return (2, shape[1] // 2) + tuple(shape[2:])


def _gather_weights(w):
    p = (2 * lax.axis_index("x") + lax.axis_index("y")).astype(jnp.int32).reshape(1)
    bufs = []
    for n in BIG:
        a = w[n]
        a2 = a.reshape(-1, a.shape[-1])
        rows, cols = a2.shape
        out = ('x', (N_CHIPS, rows, cols), BF16, (None, 'tr', cols), lambda r, pr: (pr[0], r, 0))
        b = _rows(f"cast_{n}", lambda v: (v,), [(a2, 'r', cols, 0)], [out], 256, pre=p)[0]
        bufs.append(b.reshape((N_CHIPS,) + _split2(a.shape)))
    got = _chip_allgather("gather_weights", bufs)
    full = {n: g.reshape((N_CHIPS,) + w[n].shape) for n, g in zip(BIG, got)}
    fox = full['fox_in_proj']
    D = fox.shape[2]
    fox = jnp.transpose(fox[:, 0], (1, 0, 2)).reshape(D, -1)
    return full, fox


def _step(x, loss_target, w, mom_m, mom_v):
    T, D = x.shape[1], x.shape[2]
    E = D
    G, P, C = w['s5_a_re'].shape[1], S5_STATE, S5_GROUP
    H = E // FOX_HEAD_DIM
    PG = len(POOL_WINDOWS)
    PD = E // PG
    NC = G // GROUPS_PER_CHUNK
    L = GROUPS_PER_CHUNK * P
    tq = _t(256, T)
    nq = T // tq

    full, fox_w = _gather_weights(w)
    w_out, w_s5in, w_glu, w_pin, w_pg = full['out_proj'], full['s5_in_proj'], full['s5_w_glu'], full['pool_in_proj'], full['pool_w_group']
    w_qkvz = fox_w[:, :4 * E]
    w_f = jnp.pad(fox_w[:, 4 * E:], ((0, 0), (0, LANES - H)))
    small_full = {}
    chip = 2 * lax.axis_index("x") + lax.axis_index("y")
    sv = [lax.dynamic_update_index_in_dim(jnp.zeros((N_CHIPS, 2) + w[n].shape, F32), jnp.stack([w[n], w[n]]), chip, 0)
          for n in SMALL_SHARDED]
    got = _chip_allgather("gather_vectors", sv)
    for n, g in zip(SMALL_SHARDED, got):
        small_full[n] = jnp.transpose(g[:, 0], (1, 0, 2)).reshape(w[n].shape[0], E)

    norm_w = w['norm_w']
    h = x.reshape(T, D)
    saved = []
    dparts = {}

    def s5_consts(j):
        ar, ai, fr, fi = _s5_disc_fwd(f"s5_disc_{j}", w['s5_a_re'][j], w['s5_a_im'][j], w['s5_log_dt'][j].reshape(G, 1))
        br, bi = w['s5_b_re'][j].reshape(G * P, C), w['s5_b_im'][j].reshape(G * P, C)
        bbr, bbi = _s5_bbar(f"s5_bbar_{j}", fr.reshape(G * P, 1), fi.reshape(G * P, 1), br, bi)
        bbd = jnp.concatenate([_compact(bbr.reshape(G, P, C), NC), _compact(bbi.reshape(G, P, C), NC)], axis=2).astype(BF16)
        ct = lambda v: jnp.transpose(v, (0, 2, 1))
        cbd = jnp.concatenate([_compact(ct(w['s5_c_re'][j]), NC), -_compact(ct(w['s5_c_im'][j]), NC)], axis=2).astype(BF16)
        return dict(ar=ar, ai=ai, fr=fr, fi=fi, br=br, bi=bi, bbd=bbd, cbd=cbd,
                    ar3=ar.reshape(NC, 1, L), ai3=ai.reshape(NC, 1, L))

    for i in range(4):
        kind, j = i % 3, i // 3
        nw = norm_w[i].reshape(1, D)
        xn = _norm_fwd(f"norm_{i}", h, nw)
        if kind == 0:
            k5 = s5_consts(j)
            proj = _mm_proj(f"s5_proj_{i}", xn, w_s5in, j)
            dsk = small_full['s5_d'][j].reshape(1, E)
            y1, g, hs = _s5_fwd(f"s5_scan_{i}", proj, k5['bbd'], k5['cbd'], k5['ar3'], k5['ai3'], dsk, E)
            bglu = small_full['s5_b_glu'][j].reshape(1, E)

            def glu_epi(acc, b, y1t, z):
                lin = acc + b
                return lin, (_gelu(y1t) * _sigmoid(lin)) * _silu(z)

            lin, a = _mm_rowsharded(
                f"s5_glu_{i}", g, w_glu, j, epi=glu_epi,
                extras=lambda tm, tn: [(bglu, _rowvec(tn)), (y1, _tile(tm, tn)), (proj, _tile(tm, tn, E // tn))],
                outs_fn=lambda tm, tn: [((T, E), F32, _tile(tm, tn)), ((T, E), BF16, _tile(tm, tn))])
            saved.append(dict(h=h, xn=xn, proj=proj, y1=y1, g=g, hs=hs, lin=lin, a=a, k5=k5, dsk=dsk))
        elif kind == 1:
            proj = _mm_plain(f"fox_proj_{i}", xn, w_qkvz)[0]
            flog = _mm_plain(f"fox_gate_proj_{i}", xn, w_f)[0]
            fb = jnp.pad(w['fox_f_bias'][j].reshape(1, H), ((0, 0), (0, LANES - H)))
            wq, wk = w['fox_q_norm'][j].reshape(1, FOX_HEAD_DIM), w['fox_k_norm'][j].reshape(1, FOX_HEAD_DIM)
            qn, kn = _qk_norm(f"fox_qk_norm_{i}", proj, wq, wk, H)
            cum = _cum_rows(f"fox_cum_{i}", flog, fb, False, True)
            cum_t = jnp.transpose(cum)[:H]
            cum_q = jnp.broadcast_to(cum_t[:, :, None], (H, T, LANES))
            cum_k = cum_t.reshape(H, nq, 1, tq)
            y, lse = _attn_fwd(f"fox_attn_{i}", qn, kn, proj, cum_q, cum_k, H)
            a = _rows(f"fox_gate_{i}", lambda yt, z: (yt * _silu(z),), [(y, 'r', E, 0), (proj, 'r', E, 3)], [('r', E, BF16)], 256)[0]
            saved.append(dict(h=h, xn=xn, proj=proj, flog=flog, fb=fb, wq=wq, wk=wk, qn=qn, kn=kn, cum_q=cum_q, cum_k=cum_k, y=y, lse=lse, a=a))
        else:
            proj = _mm_proj(f"pool_proj_{i}", xn, w_pin, j)
            pm = _pool_fwd(f"pool_win_{i}", proj, E)
            scale = small_full['pool_scale'][j].reshape(1, E)
            tm, tn, tk = _t(512, T), _t(512, PD), w_pg.shape[3]
            kb, nb = PD // tk, PD // tn
            mixed, a = _mm(
                f"pool_mix_{i}", pm, w_pg, M=T, N=PD, K=PD, tm=tm, tn=tn, tk=tk, groups=PG,
                a_spec=_bs((tm, tk), lambda g, m, n, k: (m, g * kb + k)),
                b_spec=_bs((None, None, None, tk, tn), lambda g, m, n, k: (k, j, g, 0, n)),
                extras=[(scale, _bs((1, tn), lambda g, m, n, k: (0, g * nb + n))),
                        (proj, _bs((tm, tn), lambda g, m, n, k: (m, E // tn + g * nb + n)))],
                epi=lambda acc, sc, z: (acc, (acc * sc) * _silu(z)),
                outs=[((T, E), F32, _bs((tm, tn), lambda g, m, n, k: (m, g * nb + n))),
                      ((T, E), BF16, _bs((tm, tn), lambda g, m, n, k: (m, g * nb + n)))])
            saved.append(dict(h=h, xn=xn, proj=proj, pm=pm, mixed=mixed, scale=scale, a=a))
        h = _mm_rowsharded(f"out_proj_{i}", saved[-1]['a'], w_out, i, epi=lambda acc, r: (r + acc,),
                           extras=lambda tm, tn: [(h, _tile(tm, tn))],
                           outs_fn=lambda tm, tn: [((T, D), F32, _tile(tm, tn))])[0]

    dh, dh16, loss_cols = _loss(h, loss_target.reshape(T, D))
    loss = lax.psum(jnp.sum(loss_cols), ("x", "y", "c"))

    gsmall = {n: [None] * w[n].shape[0] for n in SMALL}
    out_parts = [None] * 4
    for i in reversed(range(4)):
        kind, j = i % 3, i // 3
        sv_ = saved[i]
        nw = norm_w[i].reshape(1, D)
        out_parts[i] = _mm_dw_rows(f"d_out_proj_{i}", sv_['a'], dh16)
        if kind == 0:
            proj, y1, lin, k5 = sv_['proj'], sv_['y1'], sv_['lin'], sv_['k5']

            def da_epi(da, y1t, lint, z):
                gt, sg = _gelu(y1t), _sigmoid(lint)
                dy2 = da * _silu(z)
                dlin = (dy2 * gt) * (sg * (1.0 - sg))
                return da * (gt * sg) * _dsilu(z), dlin, dy2 * sg, _colsum(dlin)

            nm = T // _t(512, T)
            dz, dlin, dgd, dbg = _mm_rowsharded_t(
                f"d_s5_act_{i}", dh16, w_out, i, epi=da_epi,
                extras=lambda tm, tn: [(y1, _tile(tm, tn)), (lin, _tile(tm, tn)), (proj, _tile(tm, tn, E // tn))],
                outs_fn=lambda tm, tn: [((T, E), BF16, _tile(tm, tn)), ((T, E), BF16, _tile(tm, tn)), ((T, E), F32, _tile(tm, tn)),
                                        ((nm, 1, E), F32, _bs((None, 1, tn), lambda g, m, n, k: (m, 0, n)))])
            gsmall['s5_b_glu'][j] = jnp.sum(dbg, axis=(0, 1))
            dparts.setdefault('s5_w_glu', [None, None])[j] = _mm_dw_rows(f"d_s5_w_glu_{i}", sv_['g'], dlin)
            dy1 = _mm_rowsharded_t(
                f"d_s5_glu_{i}", dlin, w_glu, j, epi=lambda acc, d, y1t: ((acc + d) * _dgelu(y1t),),
                extras=lambda tm, tn: [(dgd, _tile(tm, tn)), (y1, _tile(tm, tn))],
                outs_fn=lambda tm, tn: [((T, E), F32, _tile(tm, tn))])[0]
            du, dbd, dcd, dab, ddk = _s5_bwd(f"d_s5_scan_{i}", dy1, proj, sv_['hs'], k5['bbd'], k5['cbd'], k5['ar3'], k5['ai3'], sv_['dsk'], E)
            gsmall['s5_d'][j] = ddk.reshape(E)
            gsmall['s5_c_re'][j] = jnp.transpose(_uncompact(dcd[:, :, :L], G), (0, 2, 1))
            gsmall['s5_c_im'][j] = -jnp.transpose(_uncompact(dcd[:, :, L:], G), (0, 2, 1))
            dbbr = _uncompact(dbd[:, :, :L], G).reshape(G * P, C)
            dbbi = _uncompact(dbd[:, :, L:], G).reshape(G * P, C)
            dbr, dbi, dfr, dfi = _s5_bbar_bwd(f"d_s5_bbar_{i}", k5['fr'].reshape(G * P, 1), k5['fi'].reshape(G * P, 1), k5['br'], k5['bi'], dbbr, dbbi)
            gsmall['s5_b_re'][j] = dbr.reshape(G, P, C)
            gsmall['s5_b_im'][j] = dbi.reshape(G, P, C)
            dab = jnp.sum(dab, axis=1)
            dare, daim, dldt = _s5_disc_bwd(f"d_s5_disc_{i}", w['s5_a_re'][j], w['s5_a_im'][j], w['s5_log_dt'][j].reshape(G, 1),
                                            (dab[:, :L].reshape(G, P), dab[:, L:].reshape(G, P), dfr.reshape(G, P), dfi.reshape(G, P)))
            gsmall['s5_a_re'][j], gsmall['s5_a_im'][j], gsmall['s5_log_dt'][j] = dare, daim, dldt.reshape(G)
            dproj = jnp.concatenate([du, dz], axis=1)
            dparts.setdefault('s5_in_proj', [None, None])[j] = _mm_dw_cols(f"d_s5_in_proj_{i}", sv_['xn'], dproj)
            dxn = _mm_colsharded_t(f"d_s5_xn_{i}", dproj, w_s5in, j)
        elif kind == 1:
            proj, y = sv_['proj'], sv_['y']
            do, dz = _mm_rowsharded_t(
                f"d_fox_act_{i}", dh16, w_out, i, epi=lambda da, yt, z: (da * _silu(z), (da * yt) * _dsilu(z)),
                extras=lambda tm, tn: [(y, _tile(tm, tn)), (proj, _tile(tm, tn, 3 * E // tn))],
                outs_fn=lambda tm, tn: [((T, E), F32, _tile(tm, tn)), ((T, E), BF16, _tile(tm, tn))])
            dqn, dkn, dv, dcq, dck = _attn_bwd(f"d_fox_attn_{i}", sv_['qn'], sv_['kn'], proj, do, y, sv_['lse'], sv_['cum_q'], sv_['cum_k'], H)
            dq, dk, dwq, dwk = _qk_norm_bwd(f"d_fox_qk_norm_{i}", proj, sv_['wq'], sv_['wk'], dqn, dkn, H)
            gsmall['fox_q_norm'][j], gsmall['fox_k_norm'][j] = dwq.reshape(-1), dwk.reshape(-1)
            dcum_t = dcq[:, :, 0] + dck.reshape(H, T)
            dcum = jnp.pad(jnp.transpose(dcum_t), ((0, 0), (0, LANES - H)))
            dls = _cum_rows(f"d_fox_cum_{i}", dcum, jnp.zeros((1, LANES), F32), True, False)
            dflog, dfb = _rows(f"d_fox_gate_{i}", lambda d, f, b: ((lambda r: (r, _colsum(r)))(d * _sigmoid(-(f + b)))),
                               [(dls, 'r', LANES, 0), (sv_['flog'], 'r', LANES, 0), (sv_['fb'], 'b', LANES, 0)],
                               [('r', LANES, BF16), ('a', LANES, F32)], 256)
            gsmall['fox_f_bias'][j] = dfb[0, :H]
            dproj = jnp.concatenate([dq, dk, dv, dz], axis=1)
            tkT = _t(K_STEP, T)
            dw_qkvz = _mm(f"d_fox_in_proj_{i}", sv_['xn'], dproj, M=D, N=4 * E, K=T, tm=_t(512, D), tn=_t(1024, 4 * E), tk=tkT, ta=True,
                          a_spec=_bs((tkT, _t(512, D)), lambda g, m, n, k: (k, m)),
                          b_spec=_bs((tkT, _t(1024, 4 * E)), lambda g, m, n, k: (k, n)),
                          outs=[((D, 4 * E), BF16, _tile(_t(512, D), _t(1024, 4 * E)))])[0]
            dw_f = _mm(f"d_fox_gate_proj_{i}", sv_['xn'], dflog, M=D, N=LANES, K=T, tm=_t(512, D), tn=LANES, tk=tkT, ta=True,
                       a_spec=_bs((tkT, _t(512, D)), lambda g, m, n, k: (k, m)),
                       b_spec=_bs((tkT, LANES), lambda g, m, n, k: (k, n)),
                       outs=[((D, LANES), BF16, _tile(_t(512, D), LANES))])[0]
            dw_fox = jnp.concatenate([dw_qkvz, dw_f[:, :H]], axis=1)
            sw = dw_fox.shape[1] // N_CHIPS
            dparts['fox_in_proj'] = [jnp.transpose(dw_fox.reshape(2, D // 2, N_CHIPS, sw), (0, 2, 1, 3))]
            dxn_f = _mm(f"d_fox_xn_gate_{i}", dflog, w_f, M=T, N=D, K=LANES, tm=_t(512, T), tn=_t(1024, D), tk=LANES, tb=True,
                        a_spec=_bs((_t(512, T), LANES), lambda g, m, n, k: (m, k)),
                        b_spec=_bs((_t(1024, D), LANES), lambda g, m, n, k: (n, k)),
                        outs=[((T, D), F32, _tile(_t(512, T), _t(1024, D)))])[0]
            tm, tn, tk = _t(512, T), _t(1024, D), _t(1024, 4 * E)
            dxn = _mm(f"d_fox_xn_{i}", dproj, w_qkvz, M=T, N=D, K=4 * E, tm=tm, tn=tn, tk=tk, tb=True,
                      a_spec=_bs((tm, tk), lambda g, m, n, k: (m, k)), b_spec=_bs((tn, tk), lambda g, m, n, k: (n, k)),
                      extras=[(dxn_f, _tile(tm, tn))], epi=lambda acc, e: (acc + e,),
                      outs=[((T, D), F32, _tile(tm, tn))])[0]
        else:
            proj, mixed, scale = sv_['proj'], sv_['mixed'], sv_['scale']
            nm = T // _t(512, T)

            def pool_epi(da, mx, sc, z):
                dy = da * _silu(z)
                return (da * (mx * sc)) * _dsilu(z), dy * sc, _colsum(dy * mx)

            dz, dmix, dsc = _mm_rowsharded_t(
                f"d_pool_act_{i}", dh16, w_out, i, epi=pool_epi,
                extras=lambda tm, tn: [(mixed, _tile(tm, tn)), (scale, _rowvec(tn)), (proj, _tile(tm, tn, E // tn))],
                outs_fn=lambda tm, tn: [((T, E), BF16, _tile(tm, tn)), ((T, E), BF16, _tile(tm, tn)),
                                        ((nm, 1, E), F32, _bs((None, 1, tn), lambda g, m, n, k: (m, 0, n)))])
            gsmall['pool_scale'][j] = jnp.sum(dsc, axis=(0, 1))
            tkw = w_pg.shape[3]
            tk = _t(K_STEP, T)
            dparts['pool_w_group'] = [_mm(
                f"d_pool_w_group_{i}", sv_['pm'], dmix, M=PD, N=PD, K=T, tm=tkw, tn=PD, tk=tk, groups=PG, ta=True,
                a_spec=_bs((tk, tkw), lambda g, m, n, k: (k, g * (PD // tkw) + m)),
                b_spec=_bs((tk, PD), lambda g, m, n, k: (k, g)),
                outs=[((2, N_CHIPS, PG // 2, tkw, PD), BF16, _bs((None, None, None, tkw, PD), lambda g, m, n, k: (g // (PG // 2), m, g % (PG // 2), 0, 0)))])[0]]
            tm, tk2 = _t(512, T), _t(512, PD)
            dpm = _mm(f"d_pool_mix_{i}", dmix, w_pg, M=T, N=PD, K=PD, tm=tm, tn=tkw, tk=tk2, groups=PG, tb=True,
                      a_spec=_bs((tm, tk2), lambda g, m, n, k: (m, g * (PD // tk2) + k)),
                      b_spec=_bs((None, None, None, tkw, tk2), lambda g, m, n, k: (n, j, g, 0, k)),
                      outs=[((T, E), F32, _bs((tm, tkw), lambda g, m, n, k: (m, g * (PD // tkw) + n)))])[0]
            du = _pool_bwd(f"d_pool_win_{i}", dpm, E)
            dproj = jnp.concatenate([du, dz], axis=1)
            dparts['pool_in_proj'] = [_mm_dw_cols(f"d_pool_in_proj_{i}", sv_['xn'], dproj)]
            dxn = _mm_colsharded_t(f"d_pool_xn_{i}", dproj, w_pin, j)
        dh, dh16, dnw = _norm_bwd(f"d_norm_{i}", dxn, sv_['h'], nw, dh)
        gsmall['norm_w'][i] = dnw.reshape(D)
    grad_x = dh.reshape(x.shape)
    dparts['out_proj'] = out_parts

    small_flat = jnp.concatenate([jnp.stack(gsmall[n]).reshape(-1) for n in SMALL])
    n_small = small_flat.shape[0]
    unit = 2 * N_CHIPS * 16 * LANES
    n_pad = -(-n_small // unit) * unit
    R = n_pad // (2 * N_CHIPS * LANES)
    small_part = jnp.pad(small_flat, (0, n_pad - n_small)).astype(BF16).reshape(2, N_CHIPS, R, LANES)
    parts, dests, buf_shapes = [], [], []
    for o, n in enumerate(BIG):
        ps = dparts[n]
        half = ps[0].shape[2:]
        buf_shapes.append((len(ps), 2, math.prod(half[:-1]), half[-1]))
        for li, pt in enumerate(ps):
            parts.append(pt)
            dests.append((o, li))
    parts.append(small_part)
    dests.append((len(BIG), 'chip'))
    buf_shapes.append((N_CHIPS, 2, R, LANES))
    red = _reduce_scatter(parts, dests, buf_shapes)
    grads = {n: r.reshape(w[n].shape) for n, r in zip(BIG, red[:len(BIG)])}
    small_all = _chip_allgather("gather_small_grads", [red[len(BIG)]])[0]
    small_all = jnp.transpose(small_all, (1, 0, 2, 3)).reshape(-1)[:n_small]
    off = 0
    p = 2 * lax.axis_index("x") + lax.axis_index("y")
    for n in SMALL:
        full_shape = (w[n].shape[0], E) if n in SMALL_SHARDED else w[n].shape
        size = math.prod(full_shape)
        gfull = small_all[off:off + size].reshape(full_shape)
        off += size
        if n in SMALL_SHARDED:
            gfull = lax.dynamic_slice_in_dim(gfull, p * (E // N_CHIPS), E // N_CHIPS, axis=1)
        grads[n] = gfull

    delta, new_m, new_v = {}, {}, {}
    for n in BIG:
        f2 = lambda a: a.reshape(-1, a.shape[-1])
        d_, m_, v_ = _adamw(f"adamw_{n}", f2(w[n]), f2(grads[n]), f2(mom_m[n]), f2(mom_v[n]))
        delta[n], new_m[n], new_v[n] = d_.reshape(w[n].shape), m_.reshape(w[n].shape), v_.reshape(w[n].shape)
    for n in SMALL:
        shape = w[n].shape
        if n in GROUP_AXIS_1:
            perm = (0,) + tuple(range(2, len(shape))) + (1,)
            inv = (0, len(shape) - 1) + tuple(range(1, len(shape) - 1))
            view = lambda a: jnp.transpose(a, perm).reshape(-1, shape[1])
            back = lambda a: jnp.transpose(a.reshape(tuple(shape[k] for k in perm)), inv)
        else:
            view = lambda a: a.reshape(-1, shape[-1])
            back = lambda a: a.reshape(shape)
        d_, m_, v_ = _adamw(f"adamw_{n}", view(w[n]), view(grads[n]), view(mom_m[n]), view(mom_v[n]))
        delta[n], new_m[n], new_v[n] = back(d_), back(m_), back(v_)
    return (loss, grad_x, *[grads[n] for n in ORDER], *[delta[n] for n in ORDER], *[new_m[n] for n in ORDER], *[new_v[n] for n in ORDER])
```

```python
import functools
import math

import jax
import jax.numpy as jnp
from jax import lax
from jax.experimental import pallas as pl
from jax.experimental.pallas import tpu as pltpu

F32 = jnp.float32
BF16 = jnp.bfloat16
MESH = pl.DeviceIdType.MESH

N_CHIPS = 4
VMEM_LIMIT = 56 * 1024 * 1024
LANES = 128
SUB = 8

EPS = 1e-6
S5_GROUP = 16
S5_STATE = 64
GROUPS_PER_CHUNK = 16
FOX_HEAD_DIM = 128
ATTN_SUB = 256
POOL_WINDOWS = (2, 4, 8, 16)
POOL_HALO = 16
ADAM_LR, ADAM_B1, ADAM_B2, ADAM_EPS, ADAM_WD, ADAM_STEP = 0.001, 0.9, 0.999, 1e-08, 0.01, 10
NEG = -1e30
K_STEP = 2048


ANY = pl.BlockSpec(memory_space=pl.ANY)


def _t(pref, dim):
    if dim <= pref:
        return dim
    t = pref - pref % 16
    while t > 16 and dim % t:
        t -= 16
    assert dim % t == 0, (pref, dim)
    return t


def _params(sem):
    return pltpu.CompilerParams(dimension_semantics=sem, vmem_limit_bytes=VMEM_LIMIT)


def _sigmoid(x):
    return 1.0 / (1.0 + jnp.exp(-x))


def _silu(z):
    return z * _sigmoid(z)


def _dsilu(z):
    s = _sigmoid(z)
    return s * (1.0 + z * (1.0 - s))


_GELU_C = math.sqrt(2.0 / math.pi)


def _gelu(x):
    return 0.5 * x * (1.0 + jnp.tanh(_GELU_C * (x + 0.044715 * (x * x * x))))


def _dgelu(x):
    t = jnp.tanh(_GELU_C * (x + 0.044715 * (x * x * x)))
    return 0.5 * (1.0 + t) + 0.5 * x * (1.0 - t * t) * (_GELU_C * (1.0 + 3.0 * 0.044715 * x * x))


def _log_sigmoid(x):
    return jnp.minimum(x, 0.0) - jnp.log(1.0 + jnp.exp(-jnp.abs(x)))


def _rms(x):
    return lax.rsqrt(jnp.mean(x * x, axis=-1, keepdims=True) + EPS)


def _rms_bwd(x, w, dy):
    r = _rms(x)
    xhat = x * r
    dxh = dy * w
    dx = r * (dxh - xhat * jnp.mean(dxh * xhat, axis=-1, keepdims=True))
    return dx, dy * xhat


def _rows(name, fn, ins, outs, tr, pre=None, into=None, deps=()):
    rows = None
    for arr, kind, cols, cb in ins:
        if kind == 'r':
            rows = arr.shape[0]
        elif kind == 's' and rows is None:
            rows = arr.shape[1]
    tr = _t(tr, rows)
    n_in = len(ins)
    has_acc = any(o[0] == 'a' for o in outs)

    def spec(kind, cols, cb):
        if kind == 'r':
            return pl.BlockSpec((tr, cols), lambda r, *p: (r, cb))
        if kind == 'b':
            return pl.BlockSpec((1, cols), lambda r, *p: (0, cb))
        return pl.BlockSpec((None, tr, cols), lambda r, p: (p[cb], r, 0))

    in_specs = [spec(kind, cols, cb) for _, kind, cols, cb in ins]
    out_specs, out_shape = [], []
    for o in outs:
        if o[0] == 'r':
            out_specs.append(pl.BlockSpec((tr, o[1]), lambda r, *p: (r, 0)))
            out_shape.append(jax.ShapeDtypeStruct((rows, o[1]), o[2]))
        elif o[0] == 'a':
            out_specs.append(pl.BlockSpec((1, o[1]), lambda r, *p: (0, 0)))
            out_shape.append(jax.ShapeDtypeStruct((1, o[1]), o[2]))
        else:
            blk = tuple(tr if d == 'tr' else d for d in o[3])
            out_specs.append(pl.BlockSpec(blk, o[4]))
            out_shape.append(jax.ShapeDtypeStruct(o[1], o[2]))
    n_pre = 0 if pre is None else 1
    args = [a[0] for a in ins]
    aliases = {}
    if into is not None:
        in_specs.append(ANY)
        args.append(into)
        aliases = {n_pre + n_in: 0}
    in_specs += [ANY] * len(deps)
    args += list(deps)
    n_all = len(args)

    def body(*refs):
        refs = refs[n_pre:]
        res = fn(*[r[...] for r in refs[:n_in]])
        for spec_o, o, v in zip(outs, refs[n_all:], res):
            if spec_o[0] == 'a':
                @pl.when(pl.program_id(0) == 0)
                def _():
                    o[...] = jnp.zeros_like(o)
                o[...] += v.astype(o.dtype)
            else:
                o[...] = v.astype(o.dtype)

    grid_spec = pltpu.PrefetchScalarGridSpec(num_scalar_prefetch=n_pre, grid=(rows // tr,), in_specs=in_specs, out_specs=out_specs)
    if pre is not None:
        args = [pre] + args
    return pl.pallas_call(body, name=name, grid_spec=grid_spec, out_shape=out_shape, input_output_aliases=aliases,
                          compiler_params=_params(("arbitrary" if has_acc else "parallel",)))(*args)


def _colsum(v):
    return jnp.sum(v, axis=0, keepdims=True)


def _mm(name, a, b, *, M, N, K, tm, tn, tk, a_spec, b_spec, outs, epi=None, extras=(), groups=1, ta=False, tb=False, deps=()):
    nk = K // tk
    assert M % tm == 0 and N % tn == 0 and K % tk == 0, (name, M, N, K, tm, tn, tk)
    dims = (((0 if ta else 1,), (1 if tb else 0,)), ((), ()))
    n_ex = len(extras)

    def body(*refs):
        a_ref, b_ref = refs[0], refs[1]
        ex = refs[2:2 + n_ex]
        out_refs = refs[2 + n_ex + len(deps):2 + n_ex + len(deps) + len(outs)]

        def finish(r):
            res = (r,) if epi is None else epi(r, *[e[...] for e in ex])
            for o, v in zip(out_refs, res):
                o[...] = v.astype(o.dtype)

        part = lax.dot_general(a_ref[...].astype(BF16), b_ref[...].astype(BF16), dims, preferred_element_type=F32)
        if nk == 1:
            finish(part)
            return
        acc = refs[-1]
        k = pl.program_id(3)

        @pl.when(k == 0)
        def _():
            acc[...] = part

        @pl.when(k > 0)
        def _():
            acc[...] += part

        @pl.when(k == nk - 1)
        def _():
            finish(acc[...])

    return pl.pallas_call(
        body, name=name, grid=(groups, M // tm, N // tn, nk),
        in_specs=[a_spec, b_spec] + [s for _, s in extras] + [ANY] * len(deps),
        out_specs=[s for _, _, s in outs],
        out_shape=[jax.ShapeDtypeStruct(sh, dt) for sh, dt, _ in outs],
        scratch_shapes=[] if nk == 1 else [pltpu.VMEM((tm, tn), F32)],
        compiler_params=_params(("parallel", "parallel", "parallel", "arbitrary")),
    )(a, b, *[e for e, _ in extras], *deps)


def _bs(shape, f):
    return pl.BlockSpec(shape, f)


def _tile(tm, tn, coff=0):
    return _bs((tm, tn), lambda g, m, n, k: (m, n + coff))


def _rowvec(tn, coff=0):
    return _bs((1, tn), lambda g, m, n, k: (0, n + coff))


def _mm_proj(name, xn, w, *, epi=None, extras=(), out_dtype=F32):
    T, D = xn.shape
    sw = w.shape[2]
    N = N_CHIPS * sw
    tm, tn, tk = _t(512, T), _t(1024, sw), _t(K_STEP, D)
    nb = sw // tn
    return _mm(name, xn, w, M=T, N=N, K=D, tm=tm, tn=tn, tk=tk,
               a_spec=_bs((tm, tk), lambda g, m, n, k: (m, k)),
               b_spec=_bs((None, tk, tn), lambda g, m, n, k: (n // nb, k, n % nb)),
               outs=[((T, N), out_dtype, _tile(tm, tn))], epi=epi, extras=extras)[0]


def _mm_plain(name, a, b, *, out_dtype=F32, epi=None, extras=(), outs=None, tn_pref=1024):
    M, K = a.shape
    N = b.shape[1]
    tm, tn, tk = _t(512, M), _t(tn_pref, N), _t(K_STEP, K)
    if outs is None:
        outs = [((M, N), out_dtype, _tile(tm, tn))]
    return _mm(name, a, b, M=M, N=N, K=K, tm=tm, tn=tn, tk=tk,
               a_spec=_bs((tm, tk), lambda g, m, n, k: (m, k)),
               b_spec=_bs((tk, tn), lambda g, m, n, k: (k, n)),
               outs=outs, epi=epi, extras=extras)


def _mm_rowsharded(name, a, w, *, epi, extras, outs_fn):
    T, E = a.shape
    tk = w.shape[1]
    N = w.shape[2]
    tm, tn = _t(512, T), _t(1024, N)
    return _mm(name, a, w, M=T, N=N, K=E, tm=tm, tn=tn, tk=tk,
               a_spec=_bs((tm, tk), lambda g, m, n, k: (m, k)),
               b_spec=_bs((None, tk, tn), lambda g, m, n, k: (k, 0, n)),
               outs=outs_fn(tm, tn), epi=epi, extras=extras(tm, tn))


def _mm_rowsharded_t(name, d, w, *, epi, extras, outs_fn):
    T, N = d.shape
    tn = w.shape[1]
    E = N_CHIPS * tn
    tm, tk = _t(512, T), _t(K_STEP, N)
    return _mm(name, d, w, M=T, N=E, K=N, tm=tm, tn=tn, tk=tk, tb=True,
               a_spec=_bs((tm, tk), lambda g, m, n, k: (m, k)),
               b_spec=_bs((None, tn, tk), lambda g, m, n, k: (n, 0, k)),
               outs=outs_fn(tm, tn), epi=epi, extras=extras(tm, tn))


def _mm_colsharded_t(name, d, w):
    T, N = d.shape
    D, sw = w.shape[1], w.shape[2]
    tm, tn, tk = _t(512, T), _t(1024, D), _t(1024, sw)
    kb = sw // tk
    return _mm(name, d, w, M=T, N=D, K=N, tm=tm, tn=tn, tk=tk, tb=True,
               a_spec=_bs((tm, tk), lambda g, m, n, k: (m, k)),
               b_spec=_bs((None, tn, tk), lambda g, m, n, k: (k // kb, n, k % kb)),
               outs=[((T, D), F32, _tile(tm, tn))])[0]


def _mm_dw_rows(name, a, d, deps=()):
    T, E = a.shape
    N = d.shape[1]
    tm, tn, tk = E // (2 * N_CHIPS), _t(2048, N), _t(K_STEP, T)
    return _mm(name, a, d, M=E, N=N, K=T, tm=tm, tn=tn, tk=tk, ta=True, deps=deps,
               a_spec=_bs((tk, tm), lambda g, m, n, k: (k, m)),
               b_spec=_bs((tk, tn), lambda g, m, n, k: (k, n)),
               outs=[((2, N_CHIPS, tm, N), BF16, _bs((None, None, tm, tn), lambda g, m, n, k: (m % 2, m // 2, 0, n)))])[0]


def _mm_dw_cols(name, xn, d):
    T, D = xn.shape
    N = d.shape[1]
    sw = N // N_CHIPS
    tm, tn, tk = _t(512, D // 2), _t(1024, sw), _t(K_STEP, T)
    mh, nb = (D // 2) // tm, sw // tn
    return _mm(name, xn, d, M=D, N=N, K=T, tm=tm, tn=tn, tk=tk, ta=True,
               a_spec=_bs((tk, tm), lambda g, m, n, k: (k, m)),
               b_spec=_bs((tk, tn), lambda g, m, n, k: (k, n)),
               outs=[((2, N_CHIPS, D // 2, sw), BF16,
                      _bs((None, None, tm, tn), lambda g, m, n, k: (m // mh, n // nb, m % mh, n % nb)))])[0]


def _norm_fwd(name, h, w, deps=()):
    D = h.shape[1]
    return _rows(name, lambda x, g: ((x * _rms(x)) * g,), [(h, 'r', D, 0), (w, 'b', D, 0)], [('r', D, BF16)], 256, deps=deps)[0]


def _norm_bwd(name, dxn, h, w, dh):
    D = h.shape[1]

    def fn(dy, x, g, up):
        dx, dwt = _rms_bwd(x, g, dy)
        r = up + dx
        return r, r, _colsum(dwt)

    return _rows(name, fn, [(dxn, 'r', D, 0), (h, 'r', D, 0), (w, 'b', D, 0), (dh, 'r', D, 0)],
                 [('r', D, F32), ('r', D, BF16), ('a', D, F32)], 256)


def _loss(h, target):
    D = h.shape[1]

    def fn(y, t):
        e = y - t
        d = e * (1.0 / D)
        return d, d, _colsum(e * e) * (0.5 / D)

    return _rows("loss", fn, [(h, 'r', D, 0), (target, 'r', D, 0)], [('r', D, F32), ('r', D, BF16), ('a', D, F32)], 256)


def _adamw(name, w, g, m, v):
    cols = w.shape[1]

    def fn(w, g, m, v):
        m = ADAM_B1 * m + (1.0 - ADAM_B1) * g
        v = ADAM_B2 * v + (1.0 - ADAM_B2) * (g * g)
        m_hat = m / (1.0 - ADAM_B1 ** ADAM_STEP)
        v_hat = v / (1.0 - ADAM_B2 ** ADAM_STEP)
        delta = -ADAM_LR * (m_hat / (jnp.sqrt(v_hat) + ADAM_EPS) + ADAM_WD * w)
        return delta, m, v

    return _rows(name, fn, [(x, 'r', cols, 0) for x in (w, g, m, v)], [('r', cols, F32)] * 3, 256)


def _s5_disc(a_re, a_im, log_dt):
    dt = jnp.exp(log_dt)
    mag = jnp.exp(a_re * dt)
    abar_r = mag * jnp.cos(a_im * dt)
    abar_i = mag * jnp.sin(a_im * dt)
    den = a_re * a_re + a_im * a_im
    xr = abar_r - 1.0
    fr = (xr * a_re + abar_i * a_im) / den
    fi = (abar_i * a_re - xr * a_im) / den
    return abar_r, abar_i, fr, fi


def _s5_disc_fwd(name, a_re, a_im, log_dt):
    G, P = a_re.shape

    def body(ar, ai, ld, o0, o1, o2, o3):
        for o, v in zip((o0, o1, o2, o3), _s5_disc(ar[...], ai[...], ld[...])):
            o[...] = v

    return pl.pallas_call(body, name=name, out_shape=[jax.ShapeDtypeStruct((G, P), F32)] * 4)(a_re, a_im, log_dt)


def _s5_disc_bwd(name, a_re, a_im, log_dt, cts):
    G, P = a_re.shape

    def body(ar, ai, ld, c0, c1, c2, c3, d0, d1, d2):
        _, vjp = jax.vjp(_s5_disc, ar[...], ai[...], ld[...])
        g0, g1, g2 = vjp((c0[...], c1[...], c2[...], c3[...]))
        d0[...] = g0
        d1[...] = g1
        d2[...] = g2

    return pl.pallas_call(body, name=name, out_shape=[jax.ShapeDtypeStruct((G, P), F32)] * 2 + [jax.ShapeDtypeStruct((G, 1), F32)])(
        a_re, a_im, log_dt, *cts)


def _s5_bbar(name, fr, fi, br, bi):
    return _rows(name, lambda fr, fi, br, bi: (fr * br - fi * bi, fr * bi + fi * br),
                 [(fr, 'r', 1, 0), (fi, 'r', 1, 0), (br, 'r', S5_GROUP, 0), (bi, 'r', S5_GROUP, 0)],
                 [('r', S5_GROUP, F32)] * 2, 2048)


def _s5_bbar_bwd(name, fr, fi, br, bi, dr, di):
    def fn(fr, fi, br, bi, dr, di):
        return (fr * dr + fi * di, fr * di - fi * dr,
                jnp.sum(br * dr + bi * di, axis=1, keepdims=True), jnp.sum(br * di - bi * dr, axis=1, keepdims=True))

    return _rows(name, fn, [(fr, 'r', 1, 0), (fi, 'r', 1, 0)] + [(x, 'r', S5_GROUP, 0) for x in (br, bi, dr, di)],
                 [('r', S5_GROUP, F32)] * 2 + [('r', 1, F32)] * 2, 2048)


def _scan_mults(m_ref, ar, ai, reverse):
    L = ar.shape[1]
    row = lax.broadcasted_iota(jnp.int32, (SUB, L), 0)
    if reverse:
        row = (SUB - 1) - row
    ar = jnp.broadcast_to(ar, (SUB, L))
    ai = jnp.broadcast_to(ai, (SUB, L))
    a2r, a2i = ar * ar - ai * ai, 2.0 * ar * ai
    a4r, a4i = a2r * a2r - a2i * a2i, 2.0 * a2r * a2i
    zero = jnp.zeros((SUB, L), F32)
    for s, (pr, pi, d) in enumerate(((ar, ai, 1), (a2r, a2i, 2), (a4r, a4i, 4))):
        m_ref[2 * s] = jnp.where(row >= d, pr, zero)
        m_ref[2 * s + 1] = jnp.where(row >= d, pi, zero)
    pr, pi = ar, ai
    for bit, (qr, qi) in ((1, (ar, ai)), (2, (a2r, a2i)), (4, (a4r, a4i))):
        on = (row & bit) != 0
        nr, ni = pr * qr - pi * qi, pr * qi + pi * qr
        pr, pi = jnp.where(on, nr, pr), jnp.where(on, ni, pi)
    m_ref[6] = pr
    m_ref[7] = pi


def _scan8(xr, xi, m_ref, cr, ci, reverse):
    for s, d in enumerate((1, 2, 4)):
        sh = (SUB - d) if reverse else d
        sr, si = pltpu.roll(xr, sh, 0), pltpu.roll(xi, sh, 0)
        mr, mi = m_ref[2 * s], m_ref[2 * s + 1]
        xr, xi = xr + mr * sr - mi * si, xi + mr * si + mi * sr
    pr, pi = m_ref[6], m_ref[7]
    return xr + pr * cr - pi * ci, xi + pr * ci + pi * cr


def _blockdiag_fill(bd_ref, c_ref, C, L):
    P = S5_STATE
    bd_ref[...] = jnp.zeros_like(bd_ref)
    for g in range(L // P):
        for half in (0, L):
            bd_ref[g * C:(g + 1) * C, half + g * P:half + (g + 1) * P] = c_ref[:, half + g * P:half + (g + 1) * P]


def _blockdiag_take(out_ref, dense_ref, C, L):
    P = S5_STATE
    for g in range(L // P):
        for half in (0, L):
            out_ref[:, half + g * P:half + (g + 1) * P] = dense_ref[g * C:(g + 1) * C, half + g * P:half + (g + 1) * P]


def _s5_fwd(name, proj, bbd, cbd, abar_r, abar_i, dskip, E):
    T = proj.shape[0]
    NC, C, L2 = bbd.shape
    L = L2 // 2
    CH = GROUPS_PER_CHUNK * C
    tT = _t(256, T)
    nt = (((1,), (1,)), ((), ()))

    def body(u_ref, bc_ref, cc_ref, ar_ref, ai_ref, d_ref, y_ref, g_ref, h_ref, bu, carry, mult, b_bd, c_bd):
        tb = pl.program_id(1)

        @pl.when(tb == 0)
        def _():
            carry[...] = jnp.zeros_like(carry)
            _blockdiag_fill(b_bd, bc_ref, C, L)
            _blockdiag_fill(c_bd, cc_ref, C, L)

        u = u_ref[...]
        bu[...] = jnp.dot(u.astype(BF16), b_bd[...], preferred_element_type=F32)
        _scan_mults(mult, ar_ref[...], ai_ref[...], False)

        def step(jb, c):
            cr, ci = c
            r0 = pl.multiple_of(jb * SUB, SUB)
            hr, hi = _scan8(bu[pl.ds(r0, SUB), 0:L], bu[pl.ds(r0, SUB), L:L2], mult, cr, ci, False)
            h_ref[pl.ds(r0, SUB), 0:L] = hr
            h_ref[pl.ds(r0, SUB), L:L2] = hi
            return (jnp.broadcast_to(hr[SUB - 1:SUB, :], (SUB, L)), jnp.broadcast_to(hi[SUB - 1:SUB, :], (SUB, L)))

        cr, ci = lax.fori_loop(0, tT // SUB, step, (carry[:, 0:L], carry[:, L:L2]))
        carry[:, 0:L] = cr
        carry[:, L:L2] = ci
        y1 = lax.dot_general(h_ref[...].astype(BF16), c_bd[...], nt, preferred_element_type=F32) + d_ref[...] * u
        y_ref[...] = y1
        g_ref[...] = _gelu(y1).astype(BF16)

    return pl.pallas_call(
        body, name=name, grid=(NC, T // tT),
        in_specs=[_bs((tT, CH), lambda c, t: (t, c)), _bs((None, C, L2), lambda c, t: (c, 0, 0)),
                  _bs((None, C, L2), lambda c, t: (c, 0, 0)), _bs((None, 1, L), lambda c, t: (c, 0, 0)),
                  _bs((None, 1, L), lambda c, t: (c, 0, 0)), _bs((1, CH), lambda c, t: (0, c))],
        out_specs=[_bs((tT, CH), lambda c, t: (t, c)), _bs((tT, CH), lambda c, t: (t, c)),
                   _bs((None, tT, L2), lambda c, t: (c, t, 0))],
        out_shape=[jax.ShapeDtypeStruct((T, E), F32), jax.ShapeDtypeStruct((T, E), BF16),
                   jax.ShapeDtypeStruct((NC, T, L2), F32)],
        scratch_shapes=[pltpu.VMEM((tT, L2), F32), pltpu.VMEM((SUB, L2), F32), pltpu.VMEM((8, SUB, L), F32),
                        pltpu.VMEM((CH, L2), BF16), pltpu.VMEM((CH, L2), BF16)],
        compiler_params=_params(("parallel", "arbitrary")),
    )(proj, bbd, cbd, abar_r, abar_i, dskip)


def _s5_bwd(name, dy1, proj, hs, bbd, cbd, abar_r, abar_i, dskip, E):
    T = proj.shape[0]
    NC, C, L2 = bbd.shape
    L = L2 // 2
    CH = GROUPS_PER_CHUNK * C
    tT = _t(256, T)
    nT = T // tT
    tn = (((0,), (0,)), ((), ()))
    nt = (((1,), (1,)), ((), ()))

    def body(dy_ref, u_ref, h_ref, bc_ref, cc_ref, ar_ref, ai_ref, d_ref, du_ref, db_ref, dc_ref, da_ref, dd_ref,
             gb, carry, mult, b_bd, c_bd, db_acc, dc_acc):
        tb = pl.program_id(1)

        @pl.when(tb == 0)
        def _():
            carry[...] = jnp.zeros_like(carry)
            db_acc[...] = jnp.zeros_like(db_acc)
            dc_acc[...] = jnp.zeros_like(dc_acc)
            da_ref[...] = jnp.zeros_like(da_ref)
            dd_ref[...] = jnp.zeros_like(dd_ref)
            _blockdiag_fill(b_bd, bc_ref, C, L)
            _blockdiag_fill(c_bd, cc_ref, C, L)

        dy = dy_ref[...]
        u = u_ref[...]
        dy16 = dy.astype(BF16)
        dc_acc[...] += lax.dot_general(dy16, h_ref[...].astype(BF16), tn, preferred_element_type=F32)
        gb[...] = jnp.dot(dy16, c_bd[...], preferred_element_type=F32)
        _scan_mults(mult, ar_ref[...], -ai_ref[...], True)
        row = lax.broadcasted_iota(jnp.int32, (SUB, L), 0)
        nblk = tT // SUB

        def step(jj, c):
            cr, ci, sr, si = c
            r0 = pl.multiple_of((nblk - 1 - jj) * SUB, SUB)
            gr, gi = _scan8(gb[pl.ds(r0, SUB), 0:L], gb[pl.ds(r0, SUB), L:L2], mult, cr, ci, True)
            gb[pl.ds(r0, SUB), 0:L] = gr
            gb[pl.ds(r0, SUB), L:L2] = gi
            nr = jnp.where(row == SUB - 1, cr, pltpu.roll(gr, SUB - 1, 0))
            ni = jnp.where(row == SUB - 1, ci, pltpu.roll(gi, SUB - 1, 0))
            hr, hi = h_ref[pl.ds(r0, SUB), 0:L], h_ref[pl.ds(r0, SUB), L:L2]
            sr = sr + nr * hr + ni * hi
            si = si + ni * hr - nr * hi
            return (jnp.broadcast_to(gr[0:1, :], (SUB, L)), jnp.broadcast_to(gi[0:1, :], (SUB, L)), sr, si)

        z = jnp.zeros((SUB, L), F32)
        cr, ci, sr, si = lax.fori_loop(0, nblk, step, (carry[:, 0:L], carry[:, L:L2], z, z))
        carry[:, 0:L] = cr
        carry[:, L:L2] = ci
        da_ref[:, 0:L] += sr
        da_ref[:, L:L2] += si
        g16 = gb[...].astype(BF16)
        du = lax.dot_general(g16, b_bd[...], nt, preferred_element_type=F32) + d_ref[...] * dy
        du_ref[...] = du.astype(BF16)
        db_acc[...] += lax.dot_general(u.astype(BF16), g16, tn, preferred_element_type=F32)
        dd_ref[...] += _colsum(dy * u)

        @pl.when(tb == nT - 1)
        def _():
            _blockdiag_take(db_ref, db_acc, C, L)
            _blockdiag_take(dc_ref, dc_acc, C, L)

    rev = lambda c, t: (nT - 1 - t, c)
    return pl.pallas_call(
        body, name=name, grid=(NC, nT),
        in_specs=[_bs((tT, CH), rev), _bs((tT, CH), rev), _bs((None, tT, L2), lambda c, t: (c, nT - 1 - t, 0)),
                  _bs((None, C, L2), lambda c, t: (c, 0, 0)), _bs((None, C, L2), lambda c, t: (c, 0, 0)),
                  _bs((None, 1, L), lambda c, t: (c, 0, 0)), _bs((None, 1, L), lambda c, t: (c, 0, 0)),
                  _bs((1, CH), lambda c, t: (0, c))],
        out_specs=[_bs((tT, CH), rev), _bs((None, C, L2), lambda c, t: (c, 0, 0)), _bs((None, C, L2), lambda c, t: (c, 0, 0)),
                   _bs((None, SUB, L2), lambda c, t: (c, 0, 0)), _bs((None, 1, CH), lambda c, t: (c, 0, 0))],
        out_shape=[jax.ShapeDtypeStruct((T, E), BF16), jax.ShapeDtypeStruct((NC, C, L2), F32),
                   jax.ShapeDtypeStruct((NC, C, L2), F32), jax.ShapeDtypeStruct((NC, SUB, L2), F32),
                   jax.ShapeDtypeStruct((NC, 1, CH), F32)],
        scratch_shapes=[pltpu.VMEM((tT, L2), F32), pltpu.VMEM((SUB, L2), F32), pltpu.VMEM((8, SUB, L), F32),
                        pltpu.VMEM((CH, L2), BF16), pltpu.VMEM((CH, L2), BF16), pltpu.VMEM((CH, L2), F32), pltpu.VMEM((CH, L2), F32)],
        compiler_params=_params(("parallel", "arbitrary")),
    )(dy1, proj, hs, bbd, cbd, abar_r, abar_i, dskip)


def _compact(v, NC):
    G, P, C = v.shape
    return jnp.transpose(v.reshape(NC, G // NC, P, C), (0, 3, 1, 2)).reshape(NC, C, (G // NC) * P)


def _uncompact(d, G):
    NC, C, L = d.shape
    gpc = G // NC
    return jnp.transpose(d.reshape(NC, C, gpc, L // gpc), (0, 2, 3, 1)).reshape(G, L // gpc, C)


def _cum_rows(name, x, bias, reverse, log_sig):
    T, L = x.shape

    def body(x_ref, b_ref, o_ref):
        row = lax.broadcasted_iota(jnp.int32, (SUB, L), 0)
        if reverse:
            row = (SUB - 1) - row
        nblk = T // SUB

        def step(jj, c):
            r0 = pl.multiple_of(((nblk - 1 - jj) if reverse else jj) * SUB, SUB)
            v = x_ref[pl.ds(r0, SUB), :] + b_ref[...]
            if log_sig:
                v = _log_sigmoid(v)
            for d in (1, 2, 4):
                v = v + jnp.where(row >= d, pltpu.roll(v, (SUB - d) if reverse else d, 0), 0.0)
            v = v + c
            o_ref[pl.ds(r0, SUB), :] = v
            e = 0 if reverse else SUB - 1
            return jnp.broadcast_to(v[e:e + 1, :], (SUB, L))

        lax.fori_loop(0, nblk, step, jnp.zeros((SUB, L), F32))

    return pl.pallas_call(body, name=name, out_shape=jax.ShapeDtypeStruct((T, L), F32),
                          compiler_params=pltpu.CompilerParams(vmem_limit_bytes=VMEM_LIMIT))(x, bias)


def _qk_norm(name, proj, wq, wk, H):
    T = proj.shape[0]
    Dh = FOX_HEAD_DIM
    tT = _t(512, T)

    def body(q_ref, k_ref, wq_ref, wk_ref, qn_ref, kn_ref):
        q, k = q_ref[...], k_ref[...]
        qn_ref[...] = ((q * _rms(q)) * wq_ref[...]).astype(BF16)
        kn_ref[...] = ((k * _rms(k)) * wk_ref[...]).astype(BF16)

    blk = lambda off: _bs((tT, Dh), lambda t, h: (t, h + off))
    return pl.pallas_call(
        body, name=name, grid=(T // tT, H),
        in_specs=[blk(0), blk(H), _bs((1, Dh), lambda t, h: (0, 0)), _bs((1, Dh), lambda t, h: (0, 0))],
        out_specs=[blk(0), blk(0)], out_shape=[jax.ShapeDtypeStruct((T, H * Dh), BF16)] * 2,
        compiler_params=_params(("parallel", "parallel")))(proj, proj, wq, wk)


def _qk_norm_bwd(name, proj, wq, wk, dqn, dkn, H):
    T = proj.shape[0]
    Dh = FOX_HEAD_DIM
    tT = _t(512, T)

    def body(q_ref, k_ref, wq_ref, wk_ref, dqn_ref, dkn_ref, dq_ref, dk_ref, dwq_ref, dwk_ref):
        @pl.when((pl.program_id(0) == 0) & (pl.program_id(1) == 0))
        def _():
            dwq_ref[...] = jnp.zeros_like(dwq_ref)
            dwk_ref[...] = jnp.zeros_like(dwk_ref)

        dq, tq = _rms_bwd(q_ref[...], wq_ref[...], dqn_ref[...])
        dk, tk = _rms_bwd(k_ref[...], wk_ref[...], dkn_ref[...])
        dq_ref[...] = dq.astype(BF16)
        dk_ref[...] = dk.astype(BF16)
        dwq_ref[...] += _colsum(tq)
        dwk_ref[...] += _colsum(tk)

    blk = lambda off: _bs((tT, Dh), lambda t, h: (t, h + off))
    one = _bs((1, Dh), lambda t, h: (0, 0))
    return pl.pallas_call(
        body, name=name, grid=(T // tT, H),
        in_specs=[blk(0), blk(H), one, one, blk(0), blk(0)],
        out_specs=[blk(0), blk(0), one, one],
        out_shape=[jax.ShapeDtypeStruct((T, H * Dh), BF16)] * 2 + [jax.ShapeDtypeStruct((1, Dh), F32)] * 2,
        compiler_params=_params(("arbitrary", "arbitrary")))(proj, proj, wq, wk, dqn, dkn)


def _attn_fwd(name, qn, kn, proj, cum_q, cum_k, H):
    T = qn.shape[0]
    Dh = FOX_HEAD_DIM
    tq = cum_k.shape[3]
    nq = T // tq
    scale = Dh ** -0.5
    nt = (((1,), (1,)), ((), ()))

    sq = _t(ATTN_SUB, tq)
    rep = tq // LANES

    def body(q_ref, k_ref, v_ref, cq_ref, ck_ref, o_ref, lse_ref, m_sc, l_sc, acc_sc):
        i = pl.program_id(1)
        m_sc[...] = jnp.full_like(m_sc, NEG)
        l_sc[...] = jnp.zeros_like(l_sc)
        acc_sc[...] = jnp.zeros_like(acc_sc)
        kloc = lax.broadcasted_iota(jnp.int32, (sq, tq), 1)
        qloc = lax.broadcasted_iota(jnp.int32, (sq, tq), 0)

        def chunk(kc, masked):
            ks = pl.multiple_of(kc * tq, tq)
            k = k_ref[pl.ds(ks, tq), :]
            v16 = v_ref[pl.ds(ks, tq), :].astype(BF16)
            ck = ck_ref[kc]
            for r in range(tq // sq):
                rows = pl.ds(r * sq, sq)
                s = lax.dot_general(q_ref[rows, :], k, nt, preferred_element_type=F32) * scale + (jnp.tile(cq_ref[rows, :], (1, rep)) - ck)
                if masked:
                    s = jnp.where(kloc <= qloc + r * sq, s, NEG)
                m_old = m_sc[rows, :]
                m_new = jnp.maximum(m_old, jnp.max(s, axis=1, keepdims=True))
                alpha = jnp.exp(m_old - m_new)
                p = jnp.exp(s - jnp.tile(m_new, (1, rep)))
                l_sc[rows, :] = alpha * l_sc[rows, :] + jnp.sum(p, axis=1, keepdims=True)
                acc_sc[rows, :] = alpha * acc_sc[rows, :] + jnp.dot(p.astype(BF16), v16, preferred_element_type=F32)
                m_sc[rows, :] = m_new

        def below(kc, c):
            chunk(kc, False)
            return c

        lax.fori_loop(0, i, below, 0)
        chunk(i, True)
        o_ref[...] = acc_sc[...] / l_sc[...]
        lse_ref[...] = m_sc[...] + jnp.log(l_sc[...])

    return pl.pallas_call(
        body, name=name, grid=(H, nq),
        in_specs=[_bs((tq, Dh), lambda h, i: (i, h)), _bs((T, Dh), lambda h, i: (0, h)), _bs((T, Dh), lambda h, i: (0, 2 * H + h)),
                  _bs((None, tq, LANES), lambda h, i: (h, i, 0)), _bs((None, nq, 1, tq), lambda h, i: (h, 0, 0, 0))],
        out_specs=[_bs((tq, Dh), lambda h, i: (i, h)), _bs((None, tq, LANES), lambda h, i: (h, i, 0))],
        out_shape=[jax.ShapeDtypeStruct((T, H * Dh), F32), jax.ShapeDtypeStruct((H, T, LANES), F32)],
        scratch_shapes=[pltpu.VMEM((tq, LANES), F32), pltpu.VMEM((tq, LANES), F32), pltpu.VMEM((tq, Dh), F32)],
        compiler_params=_params(("parallel", "parallel")))(qn, kn, proj, cum_q, cum_k)


def _attn_bwd(name, qn, kn, proj, do, o, lse, cum_q, cum_k, H):
    T = qn.shape[0]
    Dh = FOX_HEAD_DIM
    tq = cum_k.shape[3]
    nq = T // tq
    scale = Dh ** -0.5
    nt = (((1,), (1,)), ((), ()))
    tn = (((0,), (0,)), ((), ()))

    sq = _t(ATTN_SUB, tq)
    rep = tq // LANES

    def body(q_ref, k_ref, v_ref, do_ref, o_ref, lse_ref, cq_ref, ck_ref, dq_ref, dk_ref, dv_ref, dcq_ref, dck_ref,
             delta, cql, dk_sc, dv_sc, dck_sc):
        j = pl.program_id(1)

        @pl.when(j == 0)
        def _():
            dq_ref[...] = jnp.zeros_like(dq_ref)
            dcq_ref[...] = jnp.zeros_like(dcq_ref)
            delta[...] = jnp.broadcast_to(jnp.sum(do_ref[...] * o_ref[...], axis=1, keepdims=True), delta.shape)
            cql[...] = cq_ref[...] - lse_ref[...]

        dk_sc[...] = jnp.zeros_like(dk_sc)
        dv_sc[...] = jnp.zeros_like(dv_sc)
        dck_sc[...] = jnp.zeros_like(dck_sc)
        k = k_ref[...]
        v16 = v_ref[...].astype(BF16)
        ck = ck_ref[...]
        kloc = lax.broadcasted_iota(jnp.int32, (sq, tq), 1)
        qloc = lax.broadcasted_iota(jnp.int32, (sq, tq), 0)

        def qblk(i, masked):
            for r in range(tq // sq):
                rows = pl.ds(pl.multiple_of(i * tq + r * sq, sq), sq)
                q = q_ref[rows, :]
                do16 = do_ref[rows, :].astype(BF16)
                e = lax.dot_general(q, k, nt, preferred_element_type=F32) * scale + (jnp.tile(cql[rows, :], (1, rep)) - ck)
                p = jnp.exp(e)
                if masked:
                    p = jnp.where(kloc <= qloc + r * sq, p, 0.0)
                dv_sc[...] += lax.dot_general(p.astype(BF16), do16, tn, preferred_element_type=F32)
                dp = lax.dot_general(do16, v16, nt, preferred_element_type=F32)
                ds = p * (dp - jnp.tile(delta[rows, :], (1, rep)))
                ds16 = ds.astype(BF16)
                dk_sc[...] += lax.dot_general(ds16, q, tn, preferred_element_type=F32)
                dq_ref[rows, :] += jnp.dot(ds16, k, preferred_element_type=F32) * scale
                dcq_ref[rows, :] += jnp.broadcast_to(jnp.sum(ds, axis=1, keepdims=True), (sq, LANES))
                dck_sc[...] += jnp.sum(ds, axis=0, keepdims=True)

        def above(i, c):
            qblk(i, False)
            return c

        qblk(j, True)
        lax.fori_loop(j + 1, nq, above, 0)
        dk_ref[...] = dk_sc[...] * scale
        dv_ref[...] = dv_sc[...].astype(BF16)
        dck_ref[...] = -dck_sc[...]

    whole = lambda off: _bs((T, Dh), lambda h, j: (0, h + off))
    blk = lambda off: _bs((tq, Dh), lambda h, j: (j, h + off))
    return pl.pallas_call(
        body, name=name, grid=(H, nq),
        in_specs=[whole(0), blk(0), blk(2 * H), whole(0), whole(0), _bs((None, T, LANES), lambda h, j: (h, 0, 0)),
                  _bs((None, T, LANES), lambda h, j: (h, 0, 0)), _bs((None, None, 1, tq), lambda h, j: (h, j, 0, 0))],
        out_specs=[whole(0), blk(0), blk(0), _bs((None, T, LANES), lambda h, j: (h, 0, 0)),
                   _bs((None, None, 1, tq), lambda h, j: (h, j, 0, 0))],
        out_shape=[jax.ShapeDtypeStruct((T, H * Dh), F32), jax.ShapeDtypeStruct((T, H * Dh), F32), jax.ShapeDtypeStruct((T, H * Dh), BF16),
                   jax.ShapeDtypeStruct((H, T, LANES), F32), jax.ShapeDtypeStruct((H, nq, 1, tq), F32)],
        scratch_shapes=[pltpu.VMEM((T, LANES), F32), pltpu.VMEM((T, LANES), F32), pltpu.VMEM((tq, Dh), F32), pltpu.VMEM((tq, Dh), F32),
                        pltpu.VMEM((1, tq), F32)],
        compiler_params=_params(("parallel", "arbitrary")))(qn, kn, proj, do, o, lse, cum_q, cum_k)


def _pool_fwd(name, proj, E):
    T = proj.shape[0]
    PG = len(POOL_WINDOWS)
    PD = E // PG
    tT = _t(256, T)
    hb = tT // POOL_HALO

    def body(u_ref, halo_ref, o_ref, buf):
        g, tb = pl.program_id(0), pl.program_id(1)
        u = u_ref[...]
        buf[pl.ds(POOL_HALO, tT), :] = u
        buf[pl.ds(0, POOL_HALO), :] = jnp.where(tb == 0, 0.0, halo_ref[...])
        t = tb * tT + lax.broadcasted_iota(jnp.int32, (tT, 1), 0)
        for gi, w in enumerate(POOL_WINDOWS):
            @pl.when(g == gi)
            def _():
                acc = u
                for d in range(1, w):
                    acc = acc + buf[pl.ds(POOL_HALO - d, tT), :]
                cnt = jnp.minimum(t + 1, w).astype(F32)
                o_ref[...] = (acc / cnt - u).astype(BF16)

    return pl.pallas_call(
        body, name=name, grid=(PG, T // tT),
        in_specs=[_bs((tT, PD), lambda g, t: (t, g)), _bs((POOL_HALO, PD), lambda g, t: (jnp.maximum(t * hb - 1, 0), g))],
        out_specs=_bs((tT, PD), lambda g, t: (t, g)), out_shape=jax.ShapeDtypeStruct((T, E), BF16),
        scratch_shapes=[pltpu.VMEM((tT + POOL_HALO, PD), F32)],
        compiler_params=_params(("parallel", "parallel")))(proj, proj)


def _pool_bwd(name, dpm, E):
    T = dpm.shape[0]
    PG = len(POOL_WINDOWS)
    PD = E // PG
    tT = _t(256, T)
    hb = tT // POOL_HALO
    nT = T // tT

    def body(d_ref, halo_ref, o_ref, buf):
        g, tb = pl.program_id(0), pl.program_id(1)
        d = d_ref[...]
        t = tb * tT + lax.broadcasted_iota(jnp.int32, (tT, 1), 0)
        th = (tb + 1) * tT + lax.broadcasted_iota(jnp.int32, (POOL_HALO, 1), 0)
        for gi, w in enumerate(POOL_WINDOWS):
            @pl.when(g == gi)
            def _():
                dn = d / jnp.minimum(t + 1, w).astype(F32)
                buf[pl.ds(0, tT), :] = dn
                buf[pl.ds(tT, POOL_HALO), :] = jnp.where(tb == nT - 1, 0.0, halo_ref[...] / jnp.minimum(th + 1, w).astype(F32))
                acc = dn
                for s in range(1, w):
                    acc = acc + buf[pl.ds(s, tT), :]
                o_ref[...] = (acc - d).astype(BF16)

    return pl.pallas_call(
        body, name=name, grid=(PG, nT),
        in_specs=[_bs((tT, PD), lambda g, t: (t, g)), _bs((POOL_HALO, PD), lambda g, t: (jnp.minimum((t + 1) * hb, T // POOL_HALO - 1), g))],
        out_specs=_bs((tT, PD), lambda g, t: (t, g)), out_shape=jax.ShapeDtypeStruct((T, E), BF16),
        scratch_shapes=[pltpu.VMEM((tT + POOL_HALO, PD), F32)],
        compiler_params=_params(("parallel", "parallel")))(dpm, dpm)


def _coords():
    x, y, c = lax.axis_index("x"), lax.axis_index("y"), lax.axis_index("c")
    chips = [(1 - x, y), (x, 1 - y), (1 - x, 1 - y)]
    return x, y, c, 2 * x + y, (x, y, 1 - c), chips


def _chip_allgather(name, bufs):
    n = len(bufs)

    def body(*refs):
        outs = refs[n:2 * n]
        send, recv, fsend, frecv = refs[2 * n:]
        x, y, c, p, sib, chips = _coords()

        def direct(t, j, chip):
            return pltpu.make_async_remote_copy(src_ref=outs[t].at[p, c], dst_ref=outs[t].at[p, c], send_sem=send.at[t, j],
                                                recv_sem=recv.at[t, j], device_id=(*chip, c), device_id_type=MESH)

        def landed(t, j, chip):
            blk = outs[t].at[2 * chip[0] + chip[1], c]
            return pltpu.make_async_remote_copy(src_ref=blk, dst_ref=blk, send_sem=send.at[t, j],
                                                recv_sem=recv.at[t, j], device_id=(*chip, c), device_id_type=MESH)

        def passed(t, j, chip, half):
            blk = outs[t].at[2 * chip[0] + chip[1], half]
            return pltpu.make_async_remote_copy(src_ref=blk, dst_ref=blk, send_sem=fsend.at[t, j], recv_sem=frecv.at[t, j],
                                                device_id=sib, device_id_type=MESH)

        first = [direct(t, j, chip) for t in range(n) for j, chip in enumerate(chips)]
        for cp in first:
            cp.start()
        fwd = []
        for j, chip in enumerate(chips):
            for t in range(n):
                landed(t, j, chip).wait_recv()
                f = passed(t, j, chip, c)
                f.start()
                fwd.append(f)
        for j, chip in enumerate(chips):
            for t in range(n):
                passed(t, j, chip, 1 - c).wait_recv()
        for cp in first + fwd:
            cp.wait_send()

    return pl.pallas_call(
        body, name=name, in_specs=[ANY] * n, out_specs=[ANY] * n,
        out_shape=[jax.ShapeDtypeStruct(a.shape, a.dtype) for a in bufs],
        input_output_aliases={t: t for t in range(n)},
        scratch_shapes=[pltpu.SemaphoreType.DMA((n, 3))] * 4,
    )(*bufs)


SEM = pl.BlockSpec(memory_space=pltpu.SEMAPHORE)
TOKEN = jax.ShapeDtypeStruct((SUB, LANES), F32)


def _split_params():
    return pltpu.CompilerParams(has_side_effects=pltpu.SideEffectType.DATAFLOW_SIDE_EFFECTING)


def _struct(a):
    return jax.ShapeDtypeStruct(a.shape, a.dtype)


def _gather_start(name, bufs, deps):
    n, nd = len(bufs), len(deps)

    def body(*refs):
        outs = refs[n + nd:2 * n + nd]
        send, recv, token = refs[2 * n + nd:]
        x, y, c, p, sib, chips = _coords()
        for t in range(n):
            for j, chip in enumerate(chips):
                pltpu.make_async_remote_copy(src_ref=outs[t].at[p, c], dst_ref=outs[t].at[p, c], send_sem=send.at[3 * t + j],
                                             recv_sem=recv.at[3 * t + j], device_id=(*chip, c), device_id_type=MESH).start()
        token[...] = jnp.zeros_like(token)

    res = pl.pallas_call(
        body, name=name, in_specs=[ANY] * (n + nd), out_specs=[ANY] * n + [SEM, SEM, pl.BlockSpec(memory_space=pltpu.VMEM)],
        out_shape=[_struct(a) for a in bufs] + [pltpu.SemaphoreType.DMA((3 * n,)), pltpu.SemaphoreType.DMA((3 * n,)), TOKEN],
        input_output_aliases={t: t for t in range(n)}, compiler_params=_split_params(),
    )(*bufs, *deps)
    return list(res[:n]), res[n], res[n + 1], res[n + 2]


def _gather_wait(name, bufs, send, recv, after):
    n = len(bufs)

    def body(*refs):
        send_r, recv_r = refs[n], refs[n + 1]
        outs = refs[n + 3:2 * n + 3]
        x, y, c, p, sib, chips = _coords()
        for t in range(n):
            for j, chip in enumerate(chips):
                cp = pltpu.make_async_remote_copy(src_ref=outs[t].at[p, c], dst_ref=outs[t].at[2 * chip[0] + chip[1], c], send_sem=send_r.at[3 * t + j],
                                                  recv_sem=recv_r.at[3 * t + j], device_id=(*chip, c), device_id_type=MESH)
                cp.wait_send()
                cp.wait_recv()

    return list(pl.pallas_call(
        body, name=name, in_specs=[ANY] * n + [SEM, SEM, ANY], out_specs=[ANY] * n, out_shape=[_struct(a) for a in bufs],
        input_output_aliases={t: t for t in range(n)}, compiler_params=_split_params(),
    )(*bufs, send, recv, after))


def _gather_forward(name, bufs):
    n = len(bufs)

    def body(*refs):
        outs = refs[n:2 * n]
        fsend, frecv = refs[2 * n:]
        x, y, c, p, sib, chips = _coords()

        def passed(t, j, chip, half):
            blk = outs[t].at[2 * chip[0] + chip[1], half]
            return pltpu.make_async_remote_copy(src_ref=blk, dst_ref=blk, send_sem=fsend.at[t, j], recv_sem=frecv.at[t, j],
                                                device_id=sib, device_id_type=MESH)

        fwd = [passed(t, j, chip, c) for t in range(n) for j, chip in enumerate(chips)]
        for cp in fwd:
            cp.start()
        for t in range(n):
            for j, chip in enumerate(chips):
                passed(t, j, chip, 1 - c).wait_recv()
        for cp in fwd:
            cp.wait_send()

    return list(pl.pallas_call(
        body, name=name, in_specs=[ANY] * n, out_specs=[ANY] * n, out_shape=[_struct(a) for a in bufs],
        input_output_aliases={t: t for t in range(n)}, scratch_shapes=[pltpu.SemaphoreType.DMA((n, 3))] * 2,
    )(*bufs))


def _chip_exchange_start(name, sums):
    n = len(sums)
    lands = [lax.empty((3,) + a.shape[1:], a.dtype) for a in sums]

    def body(*refs):
        src, dst = refs[2 * n:3 * n], refs[3 * n:4 * n]
        send, recv, token = refs[4 * n:]
        x, y, c, p, sib, chips = _coords()
        for t in range(n):
            for j, chip in enumerate(chips):
                pltpu.make_async_remote_copy(src_ref=src[t].at[2 * chip[0] + chip[1]], dst_ref=dst[t].at[j], send_sem=send.at[3 * t + j],
                                             recv_sem=recv.at[3 * t + j], device_id=(*chip, c), device_id_type=MESH).start()
        token[...] = jnp.zeros_like(token)

    res = pl.pallas_call(
        body, name=name, in_specs=[ANY] * (2 * n), out_specs=[ANY] * (2 * n) + [SEM, SEM, pl.BlockSpec(memory_space=pltpu.VMEM)],
        out_shape=[_struct(a) for a in sums + lands] + [pltpu.SemaphoreType.DMA((3 * n,)), pltpu.SemaphoreType.DMA((3 * n,)), TOKEN],
        input_output_aliases={t: t for t in range(2 * n)}, compiler_params=_split_params(),
    )(*sums, *lands)
    return list(res[:n]), list(res[n:2 * n]), res[2 * n], res[2 * n + 1], res[2 * n + 2]


def _chip_exchange_wait(name, sums, lands, send, recv, after):
    n = len(sums)

    def body(*refs):
        send_r, recv_r = refs[2 * n], refs[2 * n + 1]
        src, dst = refs[2 * n + 3:3 * n + 3], refs[3 * n + 3:4 * n + 3]
        x, y, c, p, sib, chips = _coords()
        for t in range(n):
            for j, chip in enumerate(chips):
                cp = pltpu.make_async_remote_copy(src_ref=src[t].at[2 * chip[0] + chip[1]], dst_ref=dst[t].at[j], send_sem=send_r.at[3 * t + j],
                                                  recv_sem=recv_r.at[3 * t + j], device_id=(*chip, c), device_id_type=MESH)
                cp.wait_send()
                cp.wait_recv()

    res = pl.pallas_call(
        body, name=name, in_specs=[ANY] * (2 * n) + [SEM, SEM, ANY], out_specs=[ANY] * (2 * n),
        out_shape=[_struct(a) for a in sums + lands], input_output_aliases={t: t for t in range(2 * n)},
        compiler_params=_split_params(),
    )(*sums, *lands, send, recv, after)
    return list(res[:n]), list(res[n:])


def _pair_exchange(name, parts):
    n = len(parts)

    def body(*refs):
        ins, outs = refs[:n], refs[n:2 * n]
        send, recv = refs[2 * n:]
        x, y, c, p, sib, chips = _coords()
        cps = [pltpu.make_async_remote_copy(src_ref=ins[t].at[1 - c], dst_ref=outs[t], send_sem=send.at[t], recv_sem=recv.at[t],
                                            device_id=sib, device_id_type=MESH) for t in range(n)]
        for cp in cps:
            cp.start()
        for cp in cps:
            cp.wait()

    return pl.pallas_call(
        body, name=name, in_specs=[ANY] * n, out_specs=[ANY] * n,
        out_shape=[jax.ShapeDtypeStruct(a.shape[1:], a.dtype) for a in parts],
        scratch_shapes=[pltpu.SemaphoreType.DMA((n,))] * 2,
    )(*parts)


def _chip_exchange(name, sums):
    n = len(sums)

    def body(*refs):
        ins, outs = refs[:n], refs[n:2 * n]
        send, recv = refs[2 * n:]
        x, y, c, p, sib, chips = _coords()
        cps = [pltpu.make_async_remote_copy(src_ref=ins[t].at[2 * chip[0] + chip[1]], dst_ref=outs[t].at[j], send_sem=send.at[t, j],
                                            recv_sem=recv.at[t, j], device_id=(*chip, c), device_id_type=MESH)
               for t in range(n) for j, chip in enumerate(chips)]
        for cp in cps:
            cp.start()
        for cp in cps:
            cp.wait()

    return pl.pallas_call(
        body, name=name, in_specs=[ANY] * n, out_specs=[ANY] * n,
        out_shape=[jax.ShapeDtypeStruct((3,) + a.shape[1:], a.dtype) for a in sums],
        scratch_shapes=[pltpu.SemaphoreType.DMA((n, 3))] * 2,
    )(*sums)


def _pair_share(name, bufs, items):
    n = len(items)
    nb = len(bufs)

    def body(*refs):
        outs = refs[nb:2 * nb]
        send, recv = refs[2 * nb:]
        x, y, c, p, sib, chips = _coords()

        def blk(t, half):
            o, lead = items[t]
            return outs[o].at[p if lead == 'chip' else lead, half]

        def swap(t, half):
            return pltpu.make_async_remote_copy(src_ref=blk(t, half), dst_ref=blk(t, half), send_sem=send.at[t], recv_sem=recv.at[t],
                                                device_id=sib, device_id_type=MESH)

        cps = [swap(t, c) for t in range(n)]
        for cp in cps:
            cp.start()
        for t in range(n):
            swap(t, 1 - c).wait_recv()
        for cp in cps:
            cp.wait_send()

    return pl.pallas_call(
        body, name=name, in_specs=[ANY] * nb, out_specs=[ANY] * nb,
        out_shape=[jax.ShapeDtypeStruct(b.shape, b.dtype) for b in bufs],
        input_output_aliases={t: t for t in range(nb)},
        scratch_shapes=[pltpu.SemaphoreType.DMA((n,))] * 2,
    )(*bufs)


def _flat2(a, lead):
    return a.reshape(a.shape[:lead] + (-1, a.shape[-1]))


def _reduce_begin(tag, parts):
    c = lax.axis_index("c").astype(jnp.int32)
    got = _pair_exchange(f"rs_pair_exchange_{tag}", parts)
    sums = []
    for t, (mine, theirs) in enumerate(zip(parts, got)):
        m3, t2 = _flat2(mine, 1), theirs.reshape(-1, theirs.shape[-1])
        m3 = m3.reshape(2, -1, m3.shape[-1])
        cols = t2.shape[1]
        s = _rows(f"rs_pair_sum_{tag}_{t}", lambda a, b: (a.astype(F32) + b.astype(F32),),
                  [(m3, 's', cols, 0), (t2, 'r', cols, 0)], [('r', cols, BF16)], 512, pre=c.reshape(1))[0]
        sums.append(s.reshape(theirs.shape))
    sums, lands, send, recv, token = _chip_exchange_start(f"rs_chip_start_{tag}", sums)
    return (sums, lands, send, recv), token


def _reduce_end(tag, state, after, dests, bufs, buf_shapes):
    c = lax.axis_index("c").astype(jnp.int32)
    p = (2 * lax.axis_index("x") + lax.axis_index("y")).astype(jnp.int32)
    sums, lands = _chip_exchange_wait(f"rs_chip_wait_{tag}", *state, after)
    for t, (mine, theirs) in enumerate(zip(sums, lands)):
        o, lead = dests[t]
        shape = buf_shapes[o]
        rows, cols = shape[2], shape[3]
        m3, t3 = mine.reshape(N_CHIPS, rows, cols), theirs.reshape(3, rows, cols)
        pre = jnp.stack([p, jnp.int32(0), jnp.int32(1), jnp.int32(2), c, p if lead == 'chip' else jnp.int32(lead)])
        out = ('x', shape, F32, (None, None, 'tr', cols), lambda r, pr: (pr[5], pr[4], r, 0))
        bufs[o] = _rows(f"rs_chip_sum_{tag}_{t}", lambda a, b0, b1, b2: (((a.astype(F32) + b0.astype(F32)) + b1.astype(F32)) + b2.astype(F32),),
                        [(m3, 's', cols, 0), (t3, 's', cols, 1), (t3, 's', cols, 2), (t3, 's', cols, 3)], [out], 512, pre=pre, into=bufs[o])[0]


def kernel(x, norm_w, out_proj, s5_in_proj, s5_a_re, s5_a_im, s5_log_dt, s5_b_re, s5_b_im, s5_c_re, s5_c_im, s5_d, s5_w_glu, s5_b_glu, fox_in_proj, fox_q_norm, fox_k_norm, fox_f_bias, pool_in_proj, pool_w_group, pool_scale, loss_target, m_norm_w, m_out_proj, m_s5_in_proj, m_s5_a_re, m_s5_a_im, m_s5_log_dt, m_s5_b_re, m_s5_b_im, m_s5_c_re, m_s5_c_im, m_s5_d, m_s5_w_glu, m_s5_b_glu, m_fox_in_proj, m_fox_q_norm, m_fox_k_norm, m_fox_f_bias, m_pool_in_proj, m_pool_w_group, m_pool_scale, v_norm_w, v_out_proj, v_s5_in_proj, v_s5_a_re, v_s5_a_im, v_s5_log_dt, v_s5_b_re, v_s5_b_im, v_s5_c_re, v_s5_c_im, v_s5_d, v_s5_w_glu, v_s5_b_glu, v_fox_in_proj, v_fox_q_norm, v_fox_k_norm, v_fox_f_bias, v_pool_in_proj, v_pool_w_group, v_pool_scale):
    weights = dict(norm_w=norm_w, out_proj=out_proj, s5_in_proj=s5_in_proj, s5_a_re=s5_a_re, s5_a_im=s5_a_im, s5_log_dt=s5_log_dt,
                   s5_b_re=s5_b_re, s5_b_im=s5_b_im, s5_c_re=s5_c_re, s5_c_im=s5_c_im, s5_d=s5_d, s5_w_glu=s5_w_glu, s5_b_glu=s5_b_glu,
                   fox_in_proj=fox_in_proj, fox_q_norm=fox_q_norm, fox_k_norm=fox_k_norm, fox_f_bias=fox_f_bias,
                   pool_in_proj=pool_in_proj, pool_w_group=pool_w_group, pool_scale=pool_scale)
    mom_m = dict(norm_w=m_norm_w, out_proj=m_out_proj, s5_in_proj=m_s5_in_proj, s5_a_re=m_s5_a_re, s5_a_im=m_s5_a_im, s5_log_dt=m_s5_log_dt,
                 s5_b_re=m_s5_b_re, s5_b_im=m_s5_b_im, s5_c_re=m_s5_c_re, s5_c_im=m_s5_c_im, s5_d=m_s5_d, s5_w_glu=m_s5_w_glu, s5_b_glu=m_s5_b_glu,
                 fox_in_proj=m_fox_in_proj, fox_q_norm=m_fox_q_norm, fox_k_norm=m_fox_k_norm, fox_f_bias=m_fox_f_bias,
                 pool_in_proj=m_pool_in_proj, pool_w_group=m_pool_w_group, pool_scale=m_pool_scale)
    mom_v = dict(norm_w=v_norm_w, out_proj=v_out_proj, s5_in_proj=v_s5_in_proj, s5_a_re=v_s5_a_re, s5_a_im=v_s5_a_im, s5_log_dt=v_s5_log_dt,
                 s5_b_re=v_s5_b_re, s5_b_im=v_s5_b_im, s5_c_re=v_s5_c_re, s5_c_im=v_s5_c_im, s5_d=v_s5_d, s5_w_glu=v_s5_w_glu, s5_b_glu=v_s5_b_glu,
                 fox_in_proj=v_fox_in_proj, fox_q_norm=v_fox_q_norm, fox_k_norm=v_fox_k_norm, fox_f_bias=v_fox_f_bias,
                 pool_in_proj=v_pool_in_proj, pool_w_group=v_pool_w_group, pool_scale=v_pool_scale)
    return _step(x, loss_target, weights, mom_m, mom_v)


BIG = ('out_proj', 's5_in_proj', 's5_w_glu', 'fox_in_proj', 'pool_in_proj', 'pool_w_group')
SMALL = ('norm_w', 's5_a_re', 's5_a_im', 's5_log_dt', 's5_b_re', 's5_b_im', 's5_c_re', 's5_c_im', 's5_d', 's5_b_glu',
         'fox_q_norm', 'fox_k_norm', 'fox_f_bias', 'pool_scale')
SMALL_SHARDED = ('s5_d', 's5_b_glu', 'pool_scale')
GROUP_AXIS_1 = ('s5_a_re', 's5_a_im', 's5_b_re', 's5_b_im', 's5_c_re', 's5_c_im')
ORDER = ('norm_w', 'out_proj', 's5_in_proj', 's5_a_re', 's5_a_im', 's5_log_dt', 's5_b_re', 's5_b_im', 's5_c_re', 's5_c_im', 's5_d',
         's5_w_glu', 's5_b_glu', 'fox_in_proj', 'fox_q_norm', 'fox_k_norm', 'fox_f_bias', 'pool_in_proj', 'pool_w_group', 'pool_scale')


def _split2(shape):
    if shape[0] % 2 == 0:
        return (2, shape[0] // 2) + tuple(shape[1:])
    assert shape[0] == 1 and shape[1] % 2 == 0
    return (2, shape[1] // 2) + tuple(shape[2:])


def _cast_weights(w):
    p = (2 * lax.axis_index("x") + lax.axis_index("y")).astype(jnp.int32)
    bufs = {}
    for n in BIG:
        a3 = w[n].reshape(w[n].shape[0], -1, w[n].shape[-1])
        layers, rows, cols = a3.shape
        for l in range(layers):
            out = ('x', (N_CHIPS, rows, cols), BF16, (None, 'tr', cols), lambda r, pr: (pr[0], r, 0))
            b = _rows(f"cast_{n}_{l}", lambda v: (v,), [(a3, 's', cols, 1)], [out], 256, pre=jnp.stack([p, jnp.int32(l)]))[0]
            bufs[(n, l)] = b.reshape(N_CHIPS, 2, rows // 2, cols)
    return bufs


def _step(x, loss_target, w, mom_m, mom_v):
    T, D = x.shape[1], x.shape[2]
    E = D
    G, P, C = w['s5_a_re'].shape[1], S5_STATE, S5_GROUP
    H = E // FOX_HEAD_DIM
    PG = len(POOL_WINDOWS)
    PD = E // PG
    NC = G // GROUPS_PER_CHUNK
    L = GROUPS_PER_CHUNK * P
    tq = _t(256, T)
    nq = T // tq

    wb = _cast_weights(w)
    phases = [[('out_proj', 0), ('s5_in_proj', 0), ('s5_w_glu', 0)],
              [('out_proj', 1), ('fox_in_proj', 0)],
              [('out_proj', 2), ('pool_in_proj', 0), ('pool_w_group', 0), ('out_proj', 3), ('s5_in_proj', 1), ('s5_w_glu', 1)]]
    W = {}

    def landed(keys, bufs):
        for k, b in zip(keys, bufs):
            W[k] = b.reshape(N_CHIPS, 2 * b.shape[2], b.shape[3])

    landed(phases[0], _chip_allgather("gather_0", [wb[k] for k in phases[0]]))
    flight = _gather_start("gather_1_start", [wb[k] for k in phases[1]], [W[phases[0][0]]])
    small_full = {}
    chip = 2 * lax.axis_index("x") + lax.axis_index("y")
    sv = [lax.dynamic_update_index_in_dim(jnp.zeros((N_CHIPS, 2) + w[n].shape, F32), jnp.stack([w[n], w[n]]), chip, 0)
          for n in SMALL_SHARDED]
    got = _chip_allgather("gather_vectors", sv)
    for n, g in zip(SMALL_SHARDED, got):
        small_full[n] = jnp.transpose(g[:, 0], (1, 0, 2)).reshape(w[n].shape[0], E)

    norm_w = w['norm_w']
    h = x.reshape(T, D)
    saved = []
    dparts = {}

    def s5_consts(j):
        ar, ai, fr, fi = _s5_disc_fwd(f"s5_disc_{j}", w['s5_a_re'][j], w['s5_a_im'][j], w['s5_log_dt'][j].reshape(G, 1))
        br, bi = w['s5_b_re'][j].reshape(G * P, C), w['s5_b_im'][j].reshape(G * P, C)
        bbr, bbi = _s5_bbar(f"s5_bbar_{j}", fr.reshape(G * P, 1), fi.reshape(G * P, 1), br, bi)
        bbd = jnp.concatenate([_compact(bbr.reshape(G, P, C), NC), _compact(bbi.reshape(G, P, C), NC)], axis=2).astype(BF16)
        ct = lambda v: jnp.transpose(v, (0, 2, 1))
        cbd = jnp.concatenate([_compact(ct(w['s5_c_re'][j]), NC), -_compact(ct(w['s5_c_im'][j]), NC)], axis=2).astype(BF16)
        return dict(ar=ar, ai=ai, fr=fr, fi=fi, br=br, bi=bi, bbd=bbd, cbd=cbd,
                    ar3=ar.reshape(NC, 1, L), ai3=ai.reshape(NC, 1, L))

    for i in range(4):
        kind, j = i % 3, i // 3
        nw = norm_w[i].reshape(1, D)
        xn = _norm_fwd(f"norm_{i}", h, nw, deps=[flight[3]] if flight is not None else ())
        w_out = W[('out_proj', i)]
        if kind == 0:
            k5 = s5_consts(j)
            w_glu = W[('s5_w_glu', j)]
            proj = _mm_proj(f"s5_proj_{i}", xn, W[('s5_in_proj', j)])
            dsk = small_full['s5_d'][j].reshape(1, E)
            y1, g, hs = _s5_fwd(f"s5_scan_{i}", proj, k5['bbd'], k5['cbd'], k5['ar3'], k5['ai3'], dsk, E)
            bglu = small_full['s5_b_glu'][j].reshape(1, E)

            def glu_epi(acc, b, y1t, z):
                lin = acc + b
                return lin, (_gelu(y1t) * _sigmoid(lin)) * _silu(z)

            lin, a = _mm_rowsharded(
                f"s5_glu_{i}", g, w_glu, epi=glu_epi,
                extras=lambda tm, tn: [(bglu, _rowvec(tn)), (y1, _tile(tm, tn)), (proj, _tile(tm, tn, E // tn))],
                outs_fn=lambda tm, tn: [((T, E), F32, _tile(tm, tn)), ((T, E), BF16, _tile(tm, tn))])
            saved.append(dict(h=h, xn=xn, proj=proj, y1=y1, g=g, hs=hs, lin=lin, a=a, k5=k5, dsk=dsk))
        elif kind == 1:
            fox_w = jnp.transpose(W[('fox_in_proj', j)], (1, 0, 2)).reshape(D, -1)
            w_qkvz = fox_w[:, :4 * E]
            w_f = jnp.pad(fox_w[:, 4 * E:], ((0, 0), (0, LANES - H)))
            proj = _mm_plain(f"fox_proj_{i}", xn, w_qkvz)[0]
            flog = _mm_plain(f"fox_gate_proj_{i}", xn, w_f)[0]
            fb = jnp.pad(w['fox_f_bias'][j].reshape(1, H), ((0, 0), (0, LANES - H)))
            wq, wk = w['fox_q_norm'][j].reshape(1, FOX_HEAD_DIM), w['fox_k_norm'][j].reshape(1, FOX_HEAD_DIM)
            qn, kn = _qk_norm(f"fox_qk_norm_{i}", proj, wq, wk, H)
            cum = _cum_rows(f"fox_cum_{i}", flog, fb, False, True)
            cum_t = jnp.transpose(cum)[:H]
            cum_q = jnp.broadcast_to(cum_t[:, :, None], (H, T, LANES))
            cum_k = cum_t.reshape(H, nq, 1, tq)
            y, lse = _attn_fwd(f"fox_attn_{i}", qn, kn, proj, cum_q, cum_k, H)
            a = _rows(f"fox_gate_{i}", lambda yt, z: (yt * _silu(z),), [(y, 'r', E, 0), (proj, 'r', E, 3)], [('r', E, BF16)], 256)[0]
            saved.append(dict(h=h, xn=xn, proj=proj, flog=flog, fb=fb, wq=wq, wk=wk, qn=qn, kn=kn, cum_q=cum_q, cum_k=cum_k, y=y, lse=lse, a=a,
                              w_qkvz=w_qkvz, w_f=w_f))
        else:
            w_pg = W[('pool_w_group', j)].reshape(N_CHIPS, PG, PD // N_CHIPS, PD)
            proj = _mm_proj(f"pool_proj_{i}", xn, W[('pool_in_proj', j)])
            pm = _pool_fwd(f"pool_win_{i}", proj, E)
            scale = small_full['pool_scale'][j].reshape(1, E)
            tm, tn, tk = _t(512, T), _t(512, PD), w_pg.shape[2]
            kb, nb = PD // tk, PD // tn
            mixed, a = _mm(
                f"pool_mix_{i}", pm, w_pg, M=T, N=PD, K=PD, tm=tm, tn=tn, tk=tk, groups=PG,
                a_spec=_bs((tm, tk), lambda g, m, n, k: (m, g * kb + k)),
                b_spec=_bs((None, None, tk, tn), lambda g, m, n, k: (k, g, 0, n)),
                extras=[(scale, _bs((1, tn), lambda g, m, n, k: (0, g * nb + n))),
                        (proj, _bs((tm, tn), lambda g, m, n, k: (m, E // tn + g * nb + n)))],
                epi=lambda acc, sc, z: (acc, (acc * sc) * _silu(z)),
                outs=[((T, E), F32, _bs((tm, tn), lambda g, m, n, k: (m, g * nb + n))),
                      ((T, E), BF16, _bs((tm, tn), lambda g, m, n, k: (m, g * nb + n)))])
            saved.append(dict(h=h, xn=xn, proj=proj, pm=pm, mixed=mixed, scale=scale, a=a, w_pg=w_pg))
        h = _mm_rowsharded(f"out_proj_{i}", saved[-1]['a'], w_out, epi=lambda acc, r: (r + acc,),
                           extras=lambda tm, tn: [(h, _tile(tm, tn))],
                           outs_fn=lambda tm, tn: [((T, D), F32, _tile(tm, tn))])[0]
        if flight is not None:
            ph = 1 if i == 0 else 2
            got = _gather_wait(f"gather_{ph}_wait", flight[0], flight[1], flight[2], h)
            got = _gather_forward(f"gather_{ph}_pass", got)
            landed(phases[ph], got)
            flight = _gather_start("gather_2_start", [wb[k] for k in phases[2]], [got[0]]) if ph == 1 else None

    dh, dh16, loss_cols = _loss(h, loss_target.reshape(T, D))
    loss = lax.psum(jnp.sum(loss_cols), ("x", "y", "c"))

    gsmall = {n: [None] * w[n].shape[0] for n in SMALL}
    big_index = {n: o for o, n in enumerate(BIG)}
    rs_shapes = [None] * (len(BIG) + 1)
    rs_bufs = [None] * (len(BIG) + 1)
    rs_dests_all = []
    pending = None

    def reduce_layer(tag, named_parts):
        parts, dests = [], []
        for n, l, pt in named_parts:
            o = big_index[n] if n in big_index else len(BIG)
            half = pt.shape[2:]
            rs_shapes[o] = (N_CHIPS if l == 'chip' else w[n].shape[0], 2, math.prod(half[:-1]), half[-1])
            parts.append(pt)
            dests.append((o, l))
        rs_dests_all.extend(dests)
        state, token = _reduce_begin(tag, parts)
        return (tag, state, dests), token

    token = None
    for i in reversed(range(4)):
        kind, j = i % 3, i // 3
        sv_ = saved[i]
        nw = norm_w[i].reshape(1, D)
        w_out = W[('out_proj', i)]
        layer_parts = [('out_proj', i, _mm_dw_rows(f"d_out_proj_{i}", sv_['a'], dh16, deps=[token] if token is not None else ()))]
        if kind == 0:
            w_glu = W[('s5_w_glu', j)]
            proj, y1, lin, k5 = sv_['proj'], sv_['y1'], sv_['lin'], sv_['k5']

            def da_epi(da, y1t, lint, z):
                gt, sg = _gelu(y1t), _sigmoid(lint)
                dy2 = da * _silu(z)
                dlin = (dy2 * gt) * (sg * (1.0 - sg))
                return da * (gt * sg) * _dsilu(z), dlin, dy2 * sg, _colsum(dlin)

            nm = T // _t(512, T)
            dz, dlin, dgd, dbg = _mm_rowsharded_t(
                f"d_s5_act_{i}", dh16, w_out, epi=da_epi,
                extras=lambda tm, tn: [(y1, _tile(tm, tn)), (lin, _tile(tm, tn)), (proj, _tile(tm, tn, E // tn))],
                outs_fn=lambda tm, tn: [((T, E), BF16, _tile(tm, tn)), ((T, E), BF16, _tile(tm, tn)), ((T, E), F32, _tile(tm, tn)),
                                        ((nm, 1, E), F32, _bs((None, 1, tn), lambda g, m, n, k: (m, 0, n)))])
            gsmall['s5_b_glu'][j] = jnp.sum(dbg, axis=(0, 1))
            layer_parts.append(('s5_w_glu', j, _mm_dw_rows(f"d_s5_w_glu_{i}", sv_['g'], dlin)))
            dy1 = _mm_rowsharded_t(
                f"d_s5_glu_{i}", dlin, w_glu, epi=lambda acc, d, y1t: ((acc + d) * _dgelu(y1t),),
                extras=lambda tm, tn: [(dgd, _tile(tm, tn)), (y1, _tile(tm, tn))],
                outs_fn=lambda tm, tn: [((T, E), F32, _tile(tm, tn))])[0]
            du, dbd, dcd, dab, ddk = _s5_bwd(f"d_s5_scan_{i}", dy1, proj, sv_['hs'], k5['bbd'], k5['cbd'], k5['ar3'], k5['ai3'], sv_['dsk'], E)
            gsmall['s5_d'][j] = ddk.reshape(E)
            gsmall['s5_c_re'][j] = jnp.transpose(_uncompact(dcd[:, :, :L], G), (0, 2, 1))
            gsmall['s5_c_im'][j] = -jnp.transpose(_uncompact(dcd[:, :, L:], G), (0, 2, 1))
            dbbr = _uncompact(dbd[:, :, :L], G).reshape(G * P, C)
            dbbi = _uncompact(dbd[:, :, L:], G).reshape(G * P, C)
            dbr, dbi, dfr, dfi = _s5_bbar_bwd(f"d_s5_bbar_{i}", k5['fr'].reshape(G * P, 1), k5['fi'].reshape(G * P, 1), k5['br'], k5['bi'], dbbr, dbbi)
            gsmall['s5_b_re'][j] = dbr.reshape(G, P, C)
            gsmall['s5_b_im'][j] = dbi.reshape(G, P, C)
            dab = jnp.sum(dab, axis=1)
            dare, daim, dldt = _s5_disc_bwd(f"d_s5_disc_{i}", w['s5_a_re'][j], w['s5_a_im'][j], w['s5_log_dt'][j].reshape(G, 1),
                                            (dab[:, :L].reshape(G, P), dab[:, L:].reshape(G, P), dfr.reshape(G, P), dfi.reshape(G, P)))
            gsmall['s5_a_re'][j], gsmall['s5_a_im'][j], gsmall['s5_log_dt'][j] = dare, daim, dldt.reshape(G)
            dproj = jnp.concatenate([du, dz], axis=1)
            layer_parts.append(('s5_in_proj', j, _mm_dw_cols(f"d_s5_in_proj_{i}", sv_['xn'], dproj)))
            dxn = _mm_colsharded_t(f"d_s5_xn_{i}", dproj, W[('s5_in_proj', j)])
        elif kind == 1:
            proj, y = sv_['proj'], sv_['y']
            do, dz = _mm_rowsharded_t(
                f"d_fox_act_{i}", dh16, w_out, epi=lambda da, yt, z: (da * _silu(z), (da * yt) * _dsilu(z)),
                extras=lambda tm, tn: [(y, _tile(tm, tn)), (proj, _tile(tm, tn, 3 * E // tn))],
                outs_fn=lambda tm, tn: [((T, E), F32, _tile(tm, tn)), ((T, E), BF16, _tile(tm, tn))])
            dqn, dkn, dv, dcq, dck = _attn_bwd(f"d_fox_attn_{i}", sv_['qn'], sv_['kn'], proj, do, y, sv_['lse'], sv_['cum_q'], sv_['cum_k'], H)
            dq, dk, dwq, dwk = _qk_norm_bwd(f"d_fox_qk_norm_{i}", proj, sv_['wq'], sv_['wk'], dqn, dkn, H)
            gsmall['fox_q_norm'][j], gsmall['fox_k_norm'][j] = dwq.reshape(-1), dwk.reshape(-1)
            dcum_t = dcq[:, :, 0] + dck.reshape(H, T)
            dcum = jnp.pad(jnp.transpose(dcum_t), ((0, 0), (0, LANES - H)))
            dls = _cum_rows(f"d_fox_cum_{i}", dcum, jnp.zeros((1, LANES), F32), True, False)
            dflog, dfb = _rows(f"d_fox_gate_{i}", lambda d, f, b: ((lambda r: (r, _colsum(r)))(d * _sigmoid(-(f + b)))),
                               [(dls, 'r', LANES, 0), (sv_['flog'], 'r', LANES, 0), (sv_['fb'], 'b', LANES, 0)],
                               [('r', LANES, BF16), ('a', LANES, F32)], 256)
            gsmall['fox_f_bias'][j] = dfb[0, :H]
            dproj = jnp.concatenate([dq, dk, dv, dz], axis=1)
            tkT = _t(K_STEP, T)
            dw_qkvz = _mm(f"d_fox_in_proj_{i}", sv_['xn'], dproj, M=D, N=4 * E, K=T, tm=_t(512, D), tn=_t(1024, 4 * E), tk=tkT, ta=True,
                          a_spec=_bs((tkT, _t(512, D)), lambda g, m, n, k: (k, m)),
                          b_spec=_bs((tkT, _t(1024, 4 * E)), lambda g, m, n, k: (k, n)),
                          outs=[((D, 4 * E), BF16, _tile(_t(512, D), _t(1024, 4 * E)))])[0]
            dw_f = _mm(f"d_fox_gate_proj_{i}", sv_['xn'], dflog, M=D, N=LANES, K=T, tm=_t(512, D), tn=LANES, tk=tkT, ta=True,
                       a_spec=_bs((tkT, _t(512, D)), lambda g, m, n, k: (k, m)),
                       b_spec=_bs((tkT, LANES), lambda g, m, n, k: (k, n)),
                       outs=[((D, LANES), BF16, _tile(_t(512, D), LANES))])[0]
            dw_fox = jnp.concatenate([dw_qkvz, dw_f[:, :H]], axis=1)
            sw = dw_fox.shape[1] // N_CHIPS
            layer_parts.append(('fox_in_proj', j, jnp.transpose(dw_fox.reshape(2, D // 2, N_CHIPS, sw), (0, 2, 1, 3))))
            w_qkvz, w_f = sv_['w_qkvz'], sv_['w_f']
            dxn_f = _mm(f"d_fox_xn_gate_{i}", dflog, w_f, M=T, N=D, K=LANES, tm=_t(512, T), tn=_t(1024, D), tk=LANES, tb=True,
                        a_spec=_bs((_t(512, T), LANES), lambda g, m, n, k: (m, k)),
                        b_spec=_bs((_t(1024, D), LANES), lambda g, m, n, k: (n, k)),
                        outs=[((T, D), F32, _tile(_t(512, T), _t(1024, D)))])[0]
            tm, tn, tk = _t(512, T), _t(1024, D), _t(1024, 4 * E)
            dxn = _mm(f"d_fox_xn_{i}", dproj, w_qkvz, M=T, N=D, K=4 * E, tm=tm, tn=tn, tk=tk, tb=True,
                      a_spec=_bs((tm, tk), lambda g, m, n, k: (m, k)), b_spec=_bs((tn, tk), lambda g, m, n, k: (n, k)),
                      extras=[(dxn_f, _tile(tm, tn))], epi=lambda acc, e: (acc + e,),
                      outs=[((T, D), F32, _tile(tm, tn))])[0]
        else:
            proj, mixed, scale = sv_['proj'], sv_['mixed'], sv_['scale']
            nm = T // _t(512, T)

            def pool_epi(da, mx, sc, z):
                dy = da * _silu(z)
                return (da * (mx * sc)) * _dsilu(z), dy * sc, _colsum(dy * mx)

            dz, dmix, dsc = _mm_rowsharded_t(
                f"d_pool_act_{i}", dh16, w_out, epi=pool_epi,
                extras=lambda tm, tn: [(mixed, _tile(tm, tn)), (scale, _rowvec(tn)), (proj, _tile(tm, tn, E // tn))],
                outs_fn=lambda tm, tn: [((T, E), BF16, _tile(tm, tn)), ((T, E), BF16, _tile(tm, tn)),
                                        ((nm, 1, E), F32, _bs((None, 1, tn), lambda g, m, n, k: (m, 0, n)))])
            gsmall['pool_scale'][j] = jnp.sum(dsc, axis=(0, 1))
            w_pg = sv_['w_pg']
            tkw = w_pg.shape[2]
            tk = _t(K_STEP, T)
            layer_parts.append(('pool_w_group', j, _mm(
                f"d_pool_w_group_{i}", sv_['pm'], dmix, M=PD, N=PD, K=T, tm=tkw, tn=PD, tk=tk, groups=PG, ta=True,
                a_spec=_bs((tk, tkw), lambda g, m, n, k: (k, g * (PD // tkw) + m)),
                b_spec=_bs((tk, PD), lambda g, m, n, k: (k, g)),
                outs=[((2, N_CHIPS, PG // 2, tkw, PD), BF16, _bs((None, None, None, tkw, PD), lambda g, m, n, k: (g // (PG // 2), m, g % (PG // 2), 0, 0)))])[0]))
            tm, tk2 = _t(512, T), _t(512, PD)
            dpm = _mm(f"d_pool_mix_{i}", dmix, w_pg, M=T, N=PD, K=PD, tm=tm, tn=tkw, tk=tk2, groups=PG, tb=True,
                      a_spec=_bs((tm, tk2), lambda g, m, n, k: (m, g * (PD // tk2) + k)),
                      b_spec=_bs((None, None, tkw, tk2), lambda g, m, n, k: (n, g, 0, k)),
                      outs=[((T, E), F32, _bs((tm, tkw), lambda g, m, n, k: (m, g * (PD // tkw) + n)))])[0]
            du = _pool_bwd(f"d_pool_win_{i}", dpm, E)
            dproj = jnp.concatenate([du, dz], axis=1)
            layer_parts.append(('pool_in_proj', j, _mm_dw_cols(f"d_pool_in_proj_{i}", sv_['xn'], dproj)))
            dxn = _mm_colsharded_t(f"d_pool_xn_{i}", dproj, W[('pool_in_proj', j)])
        dh, dh16, dnw = _norm_bwd(f"d_norm_{i}", dxn, sv_['h'], nw, dh)
        gsmall['norm_w'][i] = dnw.reshape(D)
        if pending is not None:
            _reduce_end(pending[0], pending[1], dh16, pending[2], rs_bufs, rs_shapes)
        if i > 0:
            pending, token = reduce_layer(f"l{i}", layer_parts)
    grad_x = dh.reshape(x.shape)

    small_flat = jnp.concatenate([jnp.stack(gsmall[n]).reshape(-1) for n in SMALL])
    n_small = small_flat.shape[0]
    unit = 2 * N_CHIPS * 16 * LANES
    n_pad = -(-n_small // unit) * unit
    R = n_pad // (2 * N_CHIPS * LANES)
    small_part = jnp.pad(small_flat, (0, n_pad - n_small)).astype(BF16).reshape(2, N_CHIPS, R, LANES)
    pending, token = reduce_layer("l0", layer_parts + [('small', 'chip', small_part)])
    _reduce_end(pending[0], pending[1], token, pending[2], rs_bufs, rs_shapes)
    red = _pair_share("rs_pair_share", rs_bufs, rs_dests_all)
    grads = {n: r.reshape(w[n].shape) for n, r in zip(BIG, red[:len(BIG)])}
    small_all = _chip_allgather("gather_small_grads", [red[len(BIG)]])[0]
    small_all = jnp.transpose(small_all, (1, 0, 2, 3)).reshape(-1)[:n_small]
    off = 0
    p = 2 * lax.axis_index("x") + lax.axis_index("y")
    for n in SMALL:
        full_shape = (w[n].shape[0], E) if n in SMALL_SHARDED else w[n].shape
        size = math.prod(full_shape)
        gfull = small_all[off:off + size].reshape(full_shape)
        off += size
        if n in SMALL_SHARDED:
            gfull = lax.dynamic_slice_in_dim(gfull, p * (E // N_CHIPS), E // N_CHIPS, axis=1)
        grads[n] = gfull

    delta, new_m, new_v = {}, {}, {}
    for n in BIG:
        f2 = lambda a: a.reshape(-1, a.shape[-1])
        d_, m_, v_ = _adamw(f"adamw_{n}", f2(w[n]), f2(grads[n]), f2(mom_m[n]), f2(mom_v[n]))
        delta[n], new_m[n], new_v[n] = d_.reshape(w[n].shape), m_.reshape(w[n].shape), v_.reshape(w[n].shape)
    for n in SMALL:
        shape = w[n].shape
        if n in GROUP_AXIS_1:
            perm = (0,) + tuple(range(2, len(shape))) + (1,)
            inv = (0, len(shape) - 1) + tuple(range(1, len(shape) - 1))
            view = lambda a: jnp.transpose(a, perm).reshape(-1, shape[1])
            back = lambda a: jnp.transpose(a.reshape(tuple(shape[k] for k in perm)), inv)
        else:
            view = lambda a: a.reshape(-1, shape[-1])
            back = lambda a: a.reshape(shape)
        d_, m_, v_ = _adamw(f"adamw_{n}", view(w[n]), view(grads[n]), view(mom_m[n]), view(mom_v[n]))
        delta[n], new_m[n], new_v[n] = back(d_), back(m_), back(v_)
    return (loss, grad_x, *[grads[n] for n in ORDER], *[delta[n] for n in ORDER], *[new_m[n] for n in ORDER], *[new_v[n] for n in ORDER])
```

```python
import functools
import math

import jax
import jax.numpy as jnp
from jax import lax
from jax.experimental import pallas as pl
from jax.experimental.pallas import tpu as pltpu

F32 = jnp.float32
BF16 = jnp.bfloat16
MESH = pl.DeviceIdType.MESH

N_CHIPS = 4
VMEM_LIMIT = 56 * 1024 * 1024
LANES = 128
SUB = 8

EPS = 1e-6
S5_GROUP = 16
S5_STATE = 64
GROUPS_PER_CHUNK = 16
FOX_HEAD_DIM = 128
ATTN_SUB = 256
POOL_WINDOWS = (2, 4, 8, 16)
POOL_HALO = 16
ADAM_LR, ADAM_B1, ADAM_B2, ADAM_EPS, ADAM_WD, ADAM_STEP = 0.001, 0.9, 0.999, 1e-08, 0.01, 10
NEG = -1e30
K_STEP = 2048


ANY = pl.BlockSpec(memory_space=pl.ANY)


def _t(pref, dim):
    if dim <= pref:
        return dim
    t = pref - pref % 16
    while t > 16 and dim % t:
        t -= 16
    assert dim % t == 0, (pref, dim)
    return t


def _params(sem):
    return pltpu.CompilerParams(dimension_semantics=sem, vmem_limit_bytes=VMEM_LIMIT)


def _sigmoid(x):
    return 1.0 / (1.0 + jnp.exp(-x))


def _silu(z):
    return z * _sigmoid(z)


def _dsilu(z):
    s = _sigmoid(z)
    return s * (1.0 + z * (1.0 - s))


_GELU_C = math.sqrt(2.0 / math.pi)


def _gelu(x):
    return 0.5 * x * (1.0 + jnp.tanh(_GELU_C * (x + 0.044715 * (x * x * x))))


def _dgelu(x):
    t = jnp.tanh(_GELU_C * (x + 0.044715 * (x * x * x)))
    return 0.5 * (1.0 + t) + 0.5 * x * (1.0 - t * t) * (_GELU_C * (1.0 + 3.0 * 0.044715 * x * x))


def _log_sigmoid(x):
    return jnp.minimum(x, 0.0) - jnp.log(1.0 + jnp.exp(-jnp.abs(x)))


def _rms(x):
    return lax.rsqrt(jnp.mean(x * x, axis=-1, keepdims=True) + EPS)


def _rms_bwd(x, w, dy):
    r = _rms(x)
    xhat = x * r
    dxh = dy * w
    dx = r * (dxh - xhat * jnp.mean(dxh * xhat, axis=-1, keepdims=True))
    return dx, dy * xhat


def _rows(name, fn, ins, outs, tr, pre=None, into=None, deps=()):
    rows = None
    for arr, kind, cols, cb in ins:
        if kind == 'r':
            rows = arr.shape[0]
        elif kind == 's' and rows is None:
            rows = arr.shape[1]
    tr = _t(tr, rows)
    n_in = len(ins)
    has_acc = any(o[0] == 'a' for o in outs)

    def spec(kind, cols, cb):
        if kind == 'r':
            return pl.BlockSpec((tr, cols), lambda r, *p: (r, cb))
        if kind == 'b':
            return pl.BlockSpec((1, cols), lambda r, *p: (0, cb))
        return pl.BlockSpec((None, tr, cols), lambda r, p: (p[cb], r, 0))

    in_specs = [spec(kind, cols, cb) for _, kind, cols, cb in ins]
    out_specs, out_shape = [], []
    for o in outs:
        if o[0] == 'r':
            out_specs.append(pl.BlockSpec((tr, o[1]), lambda r, *p: (r, 0)))
            out_shape.append(jax.ShapeDtypeStruct((rows, o[1]), o[2]))
        elif o[0] == 'a':
            out_specs.append(pl.BlockSpec((1, o[1]), lambda r, *p: (0, 0)))
            out_shape.append(jax.ShapeDtypeStruct((1, o[1]), o[2]))
        else:
            blk = tuple(tr if d == 'tr' else d for d in o[3])
            out_specs.append(pl.BlockSpec(blk, o[4]))
            out_shape.append(jax.ShapeDtypeStruct(o[1], o[2]))
    n_pre = 0 if pre is None else 1
    args = [a[0] for a in ins]
    aliases = {}
    if into is not None:
        in_specs.append(ANY)
        args.append(into)
        aliases = {n_pre + n_in: 0}
    in_specs += [ANY] * len(deps)
    args += list(deps)
    n_all = len(args)

    def body(*refs):
        refs = refs[n_pre:]
        res = fn(*[r[...] for r in refs[:n_in]])
        for spec_o, o, v in zip(outs, refs[n_all:], res):
            if spec_o[0] == 'a':
                @pl.when(pl.program_id(0) == 0)
                def _():
                    o[...] = jnp.zeros_like(o)
                o[...] += v.astype(o.dtype)
            else:
                o[...] = v.astype(o.dtype)

    grid_spec = pltpu.PrefetchScalarGridSpec(num_scalar_prefetch=n_pre, grid=(rows // tr,), in_specs=in_specs, out_specs=out_specs)
    if pre is not None:
        args = [pre] + args
    return pl.pallas_call(body, name=name, grid_spec=grid_spec, out_shape=out_shape, input_output_aliases=aliases,
                          compiler_params=_params(("arbitrary" if has_acc else "parallel",)))(*args)


def _colsum(v):
    return jnp.sum(v, axis=0, keepdims=True)


def _mm(name, a, b, *, M, N, K, tm, tn, tk, a_spec, b_spec, outs, epi=None, extras=(), groups=1, ta=False, tb=False, deps=()):
    nk = K // tk
    assert M % tm == 0 and N % tn == 0 and K % tk == 0, (name, M, N, K, tm, tn, tk)
    dims = (((0 if ta else 1,), (1 if tb else 0,)), ((), ()))
    n_ex = len(extras)

    def body(*refs):
        a_ref, b_ref = refs[0], refs[1]
        ex = refs[2:2 + n_ex]
        out_refs = refs[2 + n_ex + len(deps):2 + n_ex + len(deps) + len(outs)]

        def finish(r):
            res = (r,) if epi is None else epi(r, *[e[...] for e in ex])
            for o, v in zip(out_refs, res):
                o[...] = v.astype(o.dtype)

        part = lax.dot_general(a_ref[...].astype(BF16), b_ref[...].astype(BF16), dims, preferred_element_type=F32)
        if nk == 1:
            finish(part)
            return
        acc = refs[-1]
        k = pl.program_id(3)

        @pl.when(k == 0)
        def _():
            acc[...] = part

        @pl.when(k > 0)
        def _():
            acc[...] += part

        @pl.when(k == nk - 1)
        def _():
            finish(acc[...])

    return pl.pallas_call(
        body, name=name, grid=(groups, M // tm, N // tn, nk),
        in_specs=[a_spec, b_spec] + [s for _, s in extras] + [ANY] * len(deps),
        out_specs=[s for _, _, s in outs],
        out_shape=[jax.ShapeDtypeStruct(sh, dt) for sh, dt, _ in outs],
        scratch_shapes=[] if nk == 1 else [pltpu.VMEM((tm, tn), F32)],
        compiler_params=_params(("parallel", "parallel", "parallel", "arbitrary")),
    )(a, b, *[e for e, _ in extras], *deps)


def _bs(shape, f):
    return pl.BlockSpec(shape, f)


def _tile(tm, tn, coff=0):
    return _bs((tm, tn), lambda g, m, n, k: (m, n + coff))


def _rowvec(tn, coff=0):
    return _bs((1, tn), lambda g, m, n, k: (0, n + coff))


def _mm_proj(name, xn, w, *, epi=None, extras=(), out_dtype=F32):
    T, D = xn.shape
    sw = w.shape[2]
    N = N_CHIPS * sw
    tm, tn, tk = _t(512, T), _t(1024, sw), _t(K_STEP, D)
    nb = sw // tn
    return _mm(name, xn, w, M=T, N=N, K=D, tm=tm, tn=tn, tk=tk,
               a_spec=_bs((tm, tk), lambda g, m, n, k: (m, k)),
               b_spec=_bs((None, tk, tn), lambda g, m, n, k: (n // nb, k, n % nb)),
               outs=[((T, N), out_dtype, _tile(tm, tn))], epi=epi, extras=extras)[0]


def _mm_plain(name, a, b, *, out_dtype=F32, epi=None, extras=(), outs=None, tn_pref=1024):
    M, K = a.shape
    N = b.shape[1]
    tm, tn, tk = _t(512, M), _t(tn_pref, N), _t(K_STEP, K)
    if outs is None:
        outs = [((M, N), out_dtype, _tile(tm, tn))]
    return _mm(name, a, b, M=M, N=N, K=K, tm=tm, tn=tn, tk=tk,
               a_spec=_bs((tm, tk), lambda g, m, n, k: (m, k)),
               b_spec=_bs((tk, tn), lambda g, m, n, k: (k, n)),
               outs=outs, epi=epi, extras=extras)


def _mm_rowsharded(name, a, w, *, epi, extras, outs_fn):
    T, E = a.shape
    tk = w.shape[1]
    N = w.shape[2]
    tm, tn = _t(512, T), _t(1024, N)
    return _mm(name, a, w, M=T, N=N, K=E, tm=tm, tn=tn, tk=tk,
               a_spec=_bs((tm, tk), lambda g, m, n, k: (m, k)),
               b_spec=_bs((None, tk, tn), lambda g, m, n, k: (k, 0, n)),
               outs=outs_fn(tm, tn), epi=epi, extras=extras(tm, tn))


def _mm_rowsharded_t(name, d, w, *, epi, extras, outs_fn, deps=()):
    T, N = d.shape
    tn = w.shape[1]
    E = N_CHIPS * tn
    tm, tk = _t(512, T), _t(K_STEP, N)
    return _mm(name, d, w, M=T, N=E, K=N, tm=tm, tn=tn, tk=tk, tb=True, deps=deps,
               a_spec=_bs((tm, tk), lambda g, m, n, k: (m, k)),
               b_spec=_bs((None, tn, tk), lambda g, m, n, k: (n, 0, k)),
               outs=outs_fn(tm, tn), epi=epi, extras=extras(tm, tn))


def _mm_colsharded_t(name, d, w):
    T, N = d.shape
    D, sw = w.shape[1], w.shape[2]
    tm, tn, tk = _t(512, T), _t(1024, D), _t(1024, sw)
    kb = sw // tk
    return _mm(name, d, w, M=T, N=D, K=N, tm=tm, tn=tn, tk=tk, tb=True,
               a_spec=_bs((tm, tk), lambda g, m, n, k: (m, k)),
               b_spec=_bs((None, tn, tk), lambda g, m, n, k: (k // kb, n, k % kb)),
               outs=[((T, D), F32, _tile(tm, tn))])[0]


def _mm_dw_rows(name, a, d, deps=()):
    T, E = a.shape
    N = d.shape[1]
    tm, tn, tk = E // (2 * N_CHIPS), _t(2048, N), _t(K_STEP, T)
    return _mm(name, a, d, M=E, N=N, K=T, tm=tm, tn=tn, tk=tk, ta=True, deps=deps,
               a_spec=_bs((tk, tm), lambda g, m, n, k: (k, m)),
               b_spec=_bs((tk, tn), lambda g, m, n, k: (k, n)),
               outs=[((2, N_CHIPS, tm, N), BF16, _bs((None, None, tm, tn), lambda g, m, n, k: (m % 2, m // 2, 0, n)))])[0]


def _mm_dw_cols(name, xn, d):
    T, D = xn.shape
    N = d.shape[1]
    sw = N // N_CHIPS
    tm, tn, tk = _t(512, D // 2), _t(1024, sw), _t(K_STEP, T)
    mh, nb = (D // 2) // tm, sw // tn
    return _mm(name, xn, d, M=D, N=N, K=T, tm=tm, tn=tn, tk=tk, ta=True,
               a_spec=_bs((tk, tm), lambda g, m, n, k: (k, m)),
               b_spec=_bs((tk, tn), lambda g, m, n, k: (k, n)),
               outs=[((2, N_CHIPS, D // 2, sw), BF16,
                      _bs((None, None, tm, tn), lambda g, m, n, k: (m // mh, n // nb, m % mh, n % nb)))])[0]


def _norm_fwd(name, h, w, deps=()):
    D = h.shape[1]
    return _rows(name, lambda x, g: ((x * _rms(x)) * g,), [(h, 'r', D, 0), (w, 'b', D, 0)], [('r', D, BF16)], 256, deps=deps)[0]


def _norm_bwd(name, dxn, h, w, dh):
    D = h.shape[1]

    def fn(dy, x, g, up):
        dx, dwt = _rms_bwd(x, g, dy)
        r = up + dx
        return r, r, _colsum(dwt)

    return _rows(name, fn, [(dxn, 'r', D, 0), (h, 'r', D, 0), (w, 'b', D, 0), (dh, 'r', D, 0)],
                 [('r', D, F32), ('r', D, BF16), ('a', D, F32)], 256)


def _loss(h, target):
    D = h.shape[1]

    def fn(y, t):
        e = y - t
        d = e * (1.0 / D)
        return d, d, _colsum(e * e) * (0.5 / D)

    return _rows("loss", fn, [(h, 'r', D, 0), (target, 'r', D, 0)], [('r', D, F32), ('r', D, BF16), ('a', D, F32)], 256)


def _adamw(name, w, g, m, v):
    cols = w.shape[1]

    def fn(w, g, m, v):
        m = ADAM_B1 * m + (1.0 - ADAM_B1) * g
        v = ADAM_B2 * v + (1.0 - ADAM_B2) * (g * g)
        m_hat = m / (1.0 - ADAM_B1 ** ADAM_STEP)
        v_hat = v / (1.0 - ADAM_B2 ** ADAM_STEP)
        delta = -ADAM_LR * (m_hat / (jnp.sqrt(v_hat) + ADAM_EPS) + ADAM_WD * w)
        return delta, m, v

    return _rows(name, fn, [(x, 'r', cols, 0) for x in (w, g, m, v)], [('r', cols, F32)] * 3, 256)


def _s5_disc(a_re, a_im, log_dt):
    dt = jnp.exp(log_dt)
    mag = jnp.exp(a_re * dt)
    abar_r = mag * jnp.cos(a_im * dt)
    abar_i = mag * jnp.sin(a_im * dt)
    den = a_re * a_re + a_im * a_im
    xr = abar_r - 1.0
    fr = (xr * a_re + abar_i * a_im) / den
    fi = (abar_i * a_re - xr * a_im) / den
    return abar_r, abar_i, fr, fi


def _s5_disc_fwd(name, a_re, a_im, log_dt):
    G, P = a_re.shape

    def body(ar, ai, ld, o0, o1, o2, o3):
        for o, v in zip((o0, o1, o2, o3), _s5_disc(ar[...], ai[...], ld[...])):
            o[...] = v

    return pl.pallas_call(body, name=name, out_shape=[jax.ShapeDtypeStruct((G, P), F32)] * 4)(a_re, a_im, log_dt)


def _s5_disc_bwd(name, a_re, a_im, log_dt, cts):
    G, P = a_re.shape

    def body(ar, ai, ld, c0, c1, c2, c3, d0, d1, d2):
        _, vjp = jax.vjp(_s5_disc, ar[...], ai[...], ld[...])
        g0, g1, g2 = vjp((c0[...], c1[...], c2[...], c3[...]))
        d0[...] = g0
        d1[...] = g1
        d2[...] = g2

    return pl.pallas_call(body, name=name, out_shape=[jax.ShapeDtypeStruct((G, P), F32)] * 2 + [jax.ShapeDtypeStruct((G, 1), F32)])(
        a_re, a_im, log_dt, *cts)


def _s5_bbar(name, fr, fi, br, bi):
    return _rows(name, lambda fr, fi, br, bi: (fr * br - fi * bi, fr * bi + fi * br),
                 [(fr, 'r', 1, 0), (fi, 'r', 1, 0), (br, 'r', S5_GROUP, 0), (bi, 'r', S5_GROUP, 0)],
                 [('r', S5_GROUP, F32)] * 2, 2048)


def _s5_bbar_bwd(name, fr, fi, br, bi, dr, di):
    def fn(fr, fi, br, bi, dr, di):
        return (fr * dr + fi * di, fr * di - fi * dr,
                jnp.sum(br * dr + bi * di, axis=1, keepdims=True), jnp.sum(br * di - bi * dr, axis=1, keepdims=True))

    return _rows(name, fn, [(fr, 'r', 1, 0), (fi, 'r', 1, 0)] + [(x, 'r', S5_GROUP, 0) for x in (br, bi, dr, di)],
                 [('r', S5_GROUP, F32)] * 2 + [('r', 1, F32)] * 2, 2048)


def _scan_mults(m_ref, ar, ai, reverse):
    L = ar.shape[1]
    row = lax.broadcasted_iota(jnp.int32, (SUB, L), 0)
    if reverse:
        row = (SUB - 1) - row
    ar = jnp.broadcast_to(ar, (SUB, L))
    ai = jnp.broadcast_to(ai, (SUB, L))
    a2r, a2i = ar * ar - ai * ai, 2.0 * ar * ai
    a4r, a4i = a2r * a2r - a2i * a2i, 2.0 * a2r * a2i
    zero = jnp.zeros((SUB, L), F32)
    for s, (pr, pi, d) in enumerate(((ar, ai, 1), (a2r, a2i, 2), (a4r, a4i, 4))):
        m_ref[2 * s] = jnp.where(row >= d, pr, zero)
        m_ref[2 * s + 1] = jnp.where(row >= d, pi, zero)
    pr, pi = ar, ai
    for bit, (qr, qi) in ((1, (ar, ai)), (2, (a2r, a2i)), (4, (a4r, a4i))):
        on = (row & bit) != 0
        nr, ni = pr * qr - pi * qi, pr * qi + pi * qr
        pr, pi = jnp.where(on, nr, pr), jnp.where(on, ni, pi)
    m_ref[6] = pr
    m_ref[7] = pi


def _scan8(xr, xi, m_ref, cr, ci, reverse):
    for s, d in enumerate((1, 2, 4)):
        sh = (SUB - d) if reverse else d
        sr, si = pltpu.roll(xr, sh, 0), pltpu.roll(xi, sh, 0)
        mr, mi = m_ref[2 * s], m_ref[2 * s + 1]
        xr, xi = xr + mr * sr - mi * si, xi + mr * si + mi * sr
    pr, pi = m_ref[6], m_ref[7]
    return xr + pr * cr - pi * ci, xi + pr * ci + pi * cr


def _blockdiag_fill(bd_ref, c_ref, C, L):
    P = S5_STATE
    bd_ref[...] = jnp.zeros_like(bd_ref)
    for g in range(L // P):
        for half in (0, L):
            bd_ref[g * C:(g + 1) * C, half + g * P:half + (g + 1) * P] = c_ref[:, half + g * P:half + (g + 1) * P]


def _blockdiag_take(out_ref, dense_ref, C, L):
    P = S5_STATE
    for g in range(L // P):
        for half in (0, L):
            out_ref[:, half + g * P:half + (g + 1) * P] = dense_ref[g * C:(g + 1) * C, half + g * P:half + (g + 1) * P]


def _s5_fwd(name, proj, bbd, cbd, abar_r, abar_i, dskip, E):
    T = proj.shape[0]
    NC, C, L2 = bbd.shape
    L = L2 // 2
    CH = GROUPS_PER_CHUNK * C
    tT = _t(256, T)
    nt = (((1,), (1,)), ((), ()))

    def body(u_ref, bc_ref, cc_ref, ar_ref, ai_ref, d_ref, y_ref, g_ref, h_ref, bu, carry, mult, b_bd, c_bd):
        tb = pl.program_id(1)

        @pl.when(tb == 0)
        def _():
            carry[...] = jnp.zeros_like(carry)
            _blockdiag_fill(b_bd, bc_ref, C, L)
            _blockdiag_fill(c_bd, cc_ref, C, L)

        u = u_ref[...]
        bu[...] = jnp.dot(u.astype(BF16), b_bd[...], preferred_element_type=F32)
        _scan_mults(mult, ar_ref[...], ai_ref[...], False)

        def step(jb, c):
            cr, ci = c
            r0 = pl.multiple_of(jb * SUB, SUB)
            hr, hi = _scan8(bu[pl.ds(r0, SUB), 0:L], bu[pl.ds(r0, SUB), L:L2], mult, cr, ci, False)
            h_ref[pl.ds(r0, SUB), 0:L] = hr
            h_ref[pl.ds(r0, SUB), L:L2] = hi
            return (jnp.broadcast_to(hr[SUB - 1:SUB, :], (SUB, L)), jnp.broadcast_to(hi[SUB - 1:SUB, :], (SUB, L)))

        cr, ci = lax.fori_loop(0, tT // SUB, step, (carry[:, 0:L], carry[:, L:L2]))
        carry[:, 0:L] = cr
        carry[:, L:L2] = ci
        y1 = lax.dot_general(h_ref[...].astype(BF16), c_bd[...], nt, preferred_element_type=F32) + d_ref[...] * u
        y_ref[...] = y1
        g_ref[...] = _gelu(y1).astype(BF16)

    return pl.pallas_call(
        body, name=name, grid=(NC, T // tT),
        in_specs=[_bs((tT, CH), lambda c, t: (t, c)), _bs((None, C, L2), lambda c, t: (c, 0, 0)),
                  _bs((None, C, L2), lambda c, t: (c, 0, 0)), _bs((None, 1, L), lambda c, t: (c, 0, 0)),
                  _bs((None, 1, L), lambda c, t: (c, 0, 0)), _bs((1, CH), lambda c, t: (0, c))],
        out_specs=[_bs((tT, CH), lambda c, t: (t, c)), _bs((tT, CH), lambda c, t: (t, c)),
                   _bs((None, tT, L2), lambda c, t: (c, t, 0))],
        out_shape=[jax.ShapeDtypeStruct((T, E), F32), jax.ShapeDtypeStruct((T, E), BF16),
                   jax.ShapeDtypeStruct((NC, T, L2), F32)],
        scratch_shapes=[pltpu.VMEM((tT, L2), F32), pltpu.VMEM((SUB, L2), F32), pltpu.VMEM((8, SUB, L), F32),
                        pltpu.VMEM((CH, L2), BF16), pltpu.VMEM((CH, L2), BF16)],
        compiler_params=_params(("parallel", "arbitrary")),
    )(proj, bbd, cbd, abar_r, abar_i, dskip)


def _s5_bwd(name, dy1, proj, hs, bbd, cbd, abar_r, abar_i, dskip, E):
    T = proj.shape[0]
    NC, C, L2 = bbd.shape
    L = L2 // 2
    CH = GROUPS_PER_CHUNK * C
    tT = _t(256, T)
    nT = T // tT
    tn = (((0,), (0,)), ((), ()))
    nt = (((1,), (1,)), ((), ()))

    def body(dy_ref, u_ref, h_ref, bc_ref, cc_ref, ar_ref, ai_ref, d_ref, du_ref, db_ref, dc_ref, da_ref, dd_ref,
             gb, carry, mult, b_bd, c_bd, db_acc, dc_acc):
        tb = pl.program_id(1)

        @pl.when(tb == 0)
        def _():
            carry[...] = jnp.zeros_like(carry)
            db_acc[...] = jnp.zeros_like(db_acc)
            dc_acc[...] = jnp.zeros_like(dc_acc)
            da_ref[...] = jnp.zeros_like(da_ref)
            dd_ref[...] = jnp.zeros_like(dd_ref)
            _blockdiag_fill(b_bd, bc_ref, C, L)
            _blockdiag_fill(c_bd, cc_ref, C, L)

        dy = dy_ref[...]
        u = u_ref[...]
        dy16 = dy.astype(BF16)
        dc_acc[...] += lax.dot_general(dy16, h_ref[...].astype(BF16), tn, preferred_element_type=F32)
        gb[...] = jnp.dot(dy16, c_bd[...], preferred_element_type=F32)
        _scan_mults(mult, ar_ref[...], -ai_ref[...], True)
        row = lax.broadcasted_iota(jnp.int32, (SUB, L), 0)
        nblk = tT // SUB

        def step(jj, c):
            cr, ci, sr, si = c
            r0 = pl.multiple_of((nblk - 1 - jj) * SUB, SUB)
            gr, gi = _scan8(gb[pl.ds(r0, SUB), 0:L], gb[pl.ds(r0, SUB), L:L2], mult, cr, ci, True)
            gb[pl.ds(r0, SUB), 0:L] = gr
            gb[pl.ds(r0, SUB), L:L2] = gi
            nr = jnp.where(row == SUB - 1, cr, pltpu.roll(gr, SUB - 1, 0))
            ni = jnp.where(row == SUB - 1, ci, pltpu.roll(gi, SUB - 1, 0))
            hr, hi = h_ref[pl.ds(r0, SUB), 0:L], h_ref[pl.ds(r0, SUB), L:L2]
            sr = sr + nr * hr + ni * hi
            si = si + ni * hr - nr * hi
            return (jnp.broadcast_to(gr[0:1, :], (SUB, L)), jnp.broadcast_to(gi[0:1, :], (SUB, L)), sr, si)

        z = jnp.zeros((SUB, L), F32)
        cr, ci, sr, si = lax.fori_loop(0, nblk, step, (carry[:, 0:L], carry[:, L:L2], z, z))
        carry[:, 0:L] = cr
        carry[:, L:L2] = ci
        da_ref[:, 0:L] += sr
        da_ref[:, L:L2] += si
        g16 = gb[...].astype(BF16)
        du = lax.dot_general(g16, b_bd[...], nt, preferred_element_type=F32) + d_ref[...] * dy
        du_ref[...] = du.astype(BF16)
        db_acc[...] += lax.dot_general(u.astype(BF16), g16, tn, preferred_element_type=F32)
        dd_ref[...] += _colsum(dy * u)

        @pl.when(tb == nT - 1)
        def _():
            _blockdiag_take(db_ref, db_acc, C, L)
            _blockdiag_take(dc_ref, dc_acc, C, L)

    rev = lambda c, t: (nT - 1 - t, c)
    return pl.pallas_call(
        body, name=name, grid=(NC, nT),
        in_specs=[_bs((tT, CH), rev), _bs((tT, CH), rev), _bs((None, tT, L2), lambda c, t: (c, nT - 1 - t, 0)),
                  _bs((None, C, L2), lambda c, t: (c, 0, 0)), _bs((None, C, L2), lambda c, t: (c, 0, 0)),
                  _bs((None, 1, L), lambda c, t: (c, 0, 0)), _bs((None, 1, L), lambda c, t: (c, 0, 0)),
                  _bs((1, CH), lambda c, t: (0, c))],
        out_specs=[_bs((tT, CH), rev), _bs((None, C, L2), lambda c, t: (c, 0, 0)), _bs((None, C, L2), lambda c, t: (c, 0, 0)),
                   _bs((None, SUB, L2), lambda c, t: (c, 0, 0)), _bs((None, 1, CH), lambda c, t: (c, 0, 0))],
        out_shape=[jax.ShapeDtypeStruct((T, E), BF16), jax.ShapeDtypeStruct((NC, C, L2), F32),
                   jax.ShapeDtypeStruct((NC, C, L2), F32), jax.ShapeDtypeStruct((NC, SUB, L2), F32),
                   jax.ShapeDtypeStruct((NC, 1, CH), F32)],
        scratch_shapes=[pltpu.VMEM((tT, L2), F32), pltpu.VMEM((SUB, L2), F32), pltpu.VMEM((8, SUB, L), F32),
                        pltpu.VMEM((CH, L2), BF16), pltpu.VMEM((CH, L2), BF16), pltpu.VMEM((CH, L2), F32), pltpu.VMEM((CH, L2), F32)],
        compiler_params=_params(("parallel", "arbitrary")),
    )(dy1, proj, hs, bbd, cbd, abar_r, abar_i, dskip)


def _compact(v, NC):
    G, P, C = v.shape
    return jnp.transpose(v.reshape(NC, G // NC, P, C), (0, 3, 1, 2)).reshape(NC, C, (G // NC) * P)


def _uncompact(d, G):
    NC, C, L = d.shape
    gpc = G // NC
    return jnp.transpose(d.reshape(NC, C, gpc, L // gpc), (0, 2, 3, 1)).reshape(G, L // gpc, C)


def _cum_rows(name, x, bias, reverse, log_sig):
    T, L = x.shape

    def body(x_ref, b_ref, o_ref):
        row = lax.broadcasted_iota(jnp.int32, (SUB, L), 0)
        if reverse:
            row = (SUB - 1) - row
        nblk = T // SUB

        def step(jj, c):
            r0 = pl.multiple_of(((nblk - 1 - jj) if reverse else jj) * SUB, SUB)
            v = x_ref[pl.ds(r0, SUB), :] + b_ref[...]
            if log_sig:
                v = _log_sigmoid(v)
            for d in (1, 2, 4):
                v = v + jnp.where(row >= d, pltpu.roll(v, (SUB - d) if reverse else d, 0), 0.0)
            v = v + c
            o_ref[pl.ds(r0, SUB), :] = v
            e = 0 if reverse else SUB - 1
            return jnp.broadcast_to(v[e:e + 1, :], (SUB, L))

        lax.fori_loop(0, nblk, step, jnp.zeros((SUB, L), F32))

    return pl.pallas_call(body, name=name, out_shape=jax.ShapeDtypeStruct((T, L), F32),
                          compiler_params=pltpu.CompilerParams(vmem_limit_bytes=VMEM_LIMIT))(x, bias)


def _qk_norm(name, proj, wq, wk, H):
    T = proj.shape[0]
    Dh = FOX_HEAD_DIM
    tT = _t(512, T)

    def body(q_ref, k_ref, wq_ref, wk_ref, qn_ref, kn_ref):
        q, k = q_ref[...], k_ref[...]
        qn_ref[...] = ((q * _rms(q)) * wq_ref[...]).astype(BF16)
        kn_ref[...] = ((k * _rms(k)) * wk_ref[...]).astype(BF16)

    blk = lambda off: _bs((tT, Dh), lambda t, h: (t, h + off))
    return pl.pallas_call(
        body, name=name, grid=(T // tT, H),
        in_specs=[blk(0), blk(H), _bs((1, Dh), lambda t, h: (0, 0)), _bs((1, Dh), lambda t, h: (0, 0))],
        out_specs=[blk(0), blk(0)], out_shape=[jax.ShapeDtypeStruct((T, H * Dh), BF16)] * 2,
        compiler_params=_params(("parallel", "parallel")))(proj, proj, wq, wk)


def _qk_norm_bwd(name, proj, wq, wk, dqn, dkn, H):
    T = proj.shape[0]
    Dh = FOX_HEAD_DIM
    tT = _t(512, T)

    def body(q_ref, k_ref, wq_ref, wk_ref, dqn_ref, dkn_ref, dq_ref, dk_ref, dwq_ref, dwk_ref):
        @pl.when((pl.program_id(0) == 0) & (pl.program_id(1) == 0))
        def _():
            dwq_ref[...] = jnp.zeros_like(dwq_ref)
            dwk_ref[...] = jnp.zeros_like(dwk_ref)

        dq, tq = _rms_bwd(q_ref[...], wq_ref[...], dqn_ref[...])
        dk, tk = _rms_bwd(k_ref[...], wk_ref[...], dkn_ref[...])
        dq_ref[...] = dq.astype(BF16)
        dk_ref[...] = dk.astype(BF16)
        dwq_ref[...] += _colsum(tq)
        dwk_ref[...] += _colsum(tk)

    blk = lambda off: _bs((tT, Dh), lambda t, h: (t, h + off))
    one = _bs((1, Dh), lambda t, h: (0, 0))
    return pl.pallas_call(
        body, name=name, grid=(T // tT, H),
        in_specs=[blk(0), blk(H), one, one, blk(0), blk(0)],
        out_specs=[blk(0), blk(0), one, one],
        out_shape=[jax.ShapeDtypeStruct((T, H * Dh), BF16)] * 2 + [jax.ShapeDtypeStruct((1, Dh), F32)] * 2,
        compiler_params=_params(("arbitrary", "arbitrary")))(proj, proj, wq, wk, dqn, dkn)


def _attn_fwd(name, qn, kn, proj, cum_q, cum_k, H):
    T = qn.shape[0]
    Dh = FOX_HEAD_DIM
    tq = cum_k.shape[3]
    nq = T // tq
    scale = Dh ** -0.5
    nt = (((1,), (1,)), ((), ()))

    sq = _t(ATTN_SUB, tq)
    rep = tq // LANES

    def body(q_ref, k_ref, v_ref, cq_ref, ck_ref, o_ref, lse_ref, m_sc, l_sc, acc_sc):
        i = pl.program_id(1)
        m_sc[...] = jnp.full_like(m_sc, NEG)
        l_sc[...] = jnp.zeros_like(l_sc)
        acc_sc[...] = jnp.zeros_like(acc_sc)
        kloc = lax.broadcasted_iota(jnp.int32, (sq, tq), 1)
        qloc = lax.broadcasted_iota(jnp.int32, (sq, tq), 0)

        def chunk(kc, masked):
            ks = pl.multiple_of(kc * tq, tq)
            k = k_ref[pl.ds(ks, tq), :]
            v16 = v_ref[pl.ds(ks, tq), :].astype(BF16)
            ck = ck_ref[kc]
            for r in range(tq // sq):
                rows = pl.ds(r * sq, sq)
                s = lax.dot_general(q_ref[rows, :], k, nt, preferred_element_type=F32) * scale + (jnp.tile(cq_ref[rows, :], (1, rep)) - ck)
                if masked:
                    s = jnp.where(kloc <= qloc + r * sq, s, NEG)
                m_old = m_sc[rows, :]
                m_new = jnp.maximum(m_old, jnp.max(s, axis=1, keepdims=True))
                alpha = jnp.exp(m_old - m_new)
                p = jnp.exp(s - jnp.tile(m_new, (1, rep)))
                l_sc[rows, :] = alpha * l_sc[rows, :] + jnp.sum(p, axis=1, keepdims=True)
                acc_sc[rows, :] = alpha * acc_sc[rows, :] + jnp.dot(p.astype(BF16), v16, preferred_element_type=F32)
                m_sc[rows, :] = m_new

        def below(kc, c):
            chunk(kc, False)
            return c

        lax.fori_loop(0, i, below, 0)
        chunk(i, True)
        o_ref[...] = acc_sc[...] / l_sc[...]
        lse_ref[...] = m_sc[...] + jnp.log(l_sc[...])

    return pl.pallas_call(
        body, name=name, grid=(H, nq),
        in_specs=[_bs((tq, Dh), lambda h, i: (i, h)), _bs((T, Dh), lambda h, i: (0, h)), _bs((T, Dh), lambda h, i: (0, 2 * H + h)),
                  _bs((None, tq, LANES), lambda h, i: (h, i, 0)), _bs((None, nq, 1, tq), lambda h, i: (h, 0, 0, 0))],
        out_specs=[_bs((tq, Dh), lambda h, i: (i, h)), _bs((None, tq, LANES), lambda h, i: (h, i, 0))],
        out_shape=[jax.ShapeDtypeStruct((T, H * Dh), F32), jax.ShapeDtypeStruct((H, T, LANES), F32)],
        scratch_shapes=[pltpu.VMEM((tq, LANES), F32), pltpu.VMEM((tq, LANES), F32), pltpu.VMEM((tq, Dh), F32)],
        compiler_params=_params(("parallel", "parallel")))(qn, kn, proj, cum_q, cum_k)


def _attn_bwd(name, qn, kn, proj, do, o, lse, cum_q, cum_k, H):
    T = qn.shape[0]
    Dh = FOX_HEAD_DIM
    tq = cum_k.shape[3]
    nq = T // tq
    scale = Dh ** -0.5
    nt = (((1,), (1,)), ((), ()))
    tn = (((0,), (0,)), ((), ()))

    sq = _t(ATTN_SUB, tq)
    rep = tq // LANES

    def body(q_ref, k_ref, v_ref, do_ref, o_ref, lse_ref, cq_ref, ck_ref, dq_ref, dk_ref, dv_ref, dcq_ref, dck_ref,
             delta, cql, dk_sc, dv_sc, dck_sc):
        j = pl.program_id(1)

        @pl.when(j == 0)
        def _():
            dq_ref[...] = jnp.zeros_like(dq_ref)
            dcq_ref[...] = jnp.zeros_like(dcq_ref)
            delta[...] = jnp.broadcast_to(jnp.sum(do_ref[...] * o_ref[...], axis=1, keepdims=True), delta.shape)
            cql[...] = cq_ref[...] - lse_ref[...]

        dk_sc[...] = jnp.zeros_like(dk_sc)
        dv_sc[...] = jnp.zeros_like(dv_sc)
        dck_sc[...] = jnp.zeros_like(dck_sc)
        k = k_ref[...]
        v16 = v_ref[...].astype(BF16)
        ck = ck_ref[...]
        kloc = lax.broadcasted_iota(jnp.int32, (sq, tq), 1)
        qloc = lax.broadcasted_iota(jnp.int32, (sq, tq), 0)

        def qblk(i, masked):
            for r in range(tq // sq):
                rows = pl.ds(pl.multiple_of(i * tq + r * sq, sq), sq)
                q = q_ref[rows, :]
                do16 = do_ref[rows, :].astype(BF16)
                e = lax.dot_general(q, k, nt, preferred_element_type=F32) * scale + (jnp.tile(cql[rows, :], (1, rep)) - ck)
                p = jnp.exp(e)
                if masked:
                    p = jnp.where(kloc <= qloc + r * sq, p, 0.0)
                dv_sc[...] += lax.dot_general(p.astype(BF16), do16, tn, preferred_element_type=F32)
                dp = lax.dot_general(do16, v16, nt, preferred_element_type=F32)
                ds = p * (dp - jnp.tile(delta[rows, :], (1, rep)))
                ds16 = ds.astype(BF16)
                dk_sc[...] += lax.dot_general(ds16, q, tn, preferred_element_type=F32)
                dq_ref[rows, :] += jnp.dot(ds16, k, preferred_element_type=F32) * scale
                dcq_ref[rows, :] += jnp.broadcast_to(jnp.sum(ds, axis=1, keepdims=True), (sq, LANES))
                dck_sc[...] += jnp.sum(ds, axis=0, keepdims=True)

        def above(i, c):
            qblk(i, False)
            return c

        qblk(j, True)
        lax.fori_loop(j + 1, nq, above, 0)
        dk_ref[...] = dk_sc[...] * scale
        dv_ref[...] = dv_sc[...].astype(BF16)
        dck_ref[...] = -dck_sc[...]

    whole = lambda off: _bs((T, Dh), lambda h, j: (0, h + off))
    blk = lambda off: _bs((tq, Dh), lambda h, j: (j, h + off))
    return pl.pallas_call(
        body, name=name, grid=(H, nq),
        in_specs=[whole(0), blk(0), blk(2 * H), whole(0), whole(0), _bs((None, T, LANES), lambda h, j: (h, 0, 0)),
                  _bs((None, T, LANES), lambda h, j: (h, 0, 0)), _bs((None, None, 1, tq), lambda h, j: (h, j, 0, 0))],
        out_specs=[whole(0), blk(0), blk(0), _bs((None, T, LANES), lambda h, j: (h, 0, 0)),
                   _bs((None, None, 1, tq), lambda h, j: (h, j, 0, 0))],
        out_shape=[jax.ShapeDtypeStruct((T, H * Dh), F32), jax.ShapeDtypeStruct((T, H * Dh), F32), jax.ShapeDtypeStruct((T, H * Dh), BF16),
                   jax.ShapeDtypeStruct((H, T, LANES), F32), jax.ShapeDtypeStruct((H, nq, 1, tq), F32)],
        scratch_shapes=[pltpu.VMEM((T, LANES), F32), pltpu.VMEM((T, LANES), F32), pltpu.VMEM((tq, Dh), F32), pltpu.VMEM((tq, Dh), F32),
                        pltpu.VMEM((1, tq), F32)],
        compiler_params=_params(("parallel", "arbitrary")))(qn, kn, proj, do, o, lse, cum_q, cum_k)


def _pool_fwd(name, proj, E):
    T = proj.shape[0]
    PG = len(POOL_WINDOWS)
    PD = E // PG
    tT = _t(256, T)
    hb = tT // POOL_HALO

    def body(u_ref, halo_ref, o_ref, buf):
        g, tb = pl.program_id(0), pl.program_id(1)
        u = u_ref[...]
        buf[pl.ds(POOL_HALO, tT), :] = u
        buf[pl.ds(0, POOL_HALO), :] = jnp.where(tb == 0, 0.0, halo_ref[...])
        t = tb * tT + lax.broadcasted_iota(jnp.int32, (tT, 1), 0)
        for gi, w in enumerate(POOL_WINDOWS):
            @pl.when(g == gi)
            def _():
                acc = u
                for d in range(1, w):
                    acc = acc + buf[pl.ds(POOL_HALO - d, tT), :]
                cnt = jnp.minimum(t + 1, w).astype(F32)
                o_ref[...] = (acc / cnt - u).astype(BF16)

    return pl.pallas_call(
        body, name=name, grid=(PG, T // tT),
        in_specs=[_bs((tT, PD), lambda g, t: (t, g)), _bs((POOL_HALO, PD), lambda g, t: (jnp.maximum(t * hb - 1, 0), g))],
        out_specs=_bs((tT, PD), lambda g, t: (t, g)), out_shape=jax.ShapeDtypeStruct((T, E), BF16),
        scratch_shapes=[pltpu.VMEM((tT + POOL_HALO, PD), F32)],
        compiler_params=_params(("parallel", "parallel")))(proj, proj)


def _pool_bwd(name, dpm, E):
    T = dpm.shape[0]
    PG = len(POOL_WINDOWS)
    PD = E // PG
    tT = _t(256, T)
    hb = tT // POOL_HALO
    nT = T // tT

    def body(d_ref, halo_ref, o_ref, buf):
        g, tb = pl.program_id(0), pl.program_id(1)
        d = d_ref[...]
        t = tb * tT + lax.broadcasted_iota(jnp.int32, (tT, 1), 0)
        th = (tb + 1) * tT + lax.broadcasted_iota(jnp.int32, (POOL_HALO, 1), 0)
        for gi, w in enumerate(POOL_WINDOWS):
            @pl.when(g == gi)
            def _():
                dn = d / jnp.minimum(t + 1, w).astype(F32)
                buf[pl.ds(0, tT), :] = dn
                buf[pl.ds(tT, POOL_HALO), :] = jnp.where(tb == nT - 1, 0.0, halo_ref[...] / jnp.minimum(th + 1, w).astype(F32))
                acc = dn
                for s in range(1, w):
                    acc = acc + buf[pl.ds(s, tT), :]
                o_ref[...] = (acc - d).astype(BF16)

    return pl.pallas_call(
        body, name=name, grid=(PG, nT),
        in_specs=[_bs((tT, PD), lambda g, t: (t, g)), _bs((POOL_HALO, PD), lambda g, t: (jnp.minimum((t + 1) * hb, T // POOL_HALO - 1), g))],
        out_specs=_bs((tT, PD), lambda g, t: (t, g)), out_shape=jax.ShapeDtypeStruct((T, E), BF16),
        scratch_shapes=[pltpu.VMEM((tT + POOL_HALO, PD), F32)],
        compiler_params=_params(("parallel", "parallel")))(dpm, dpm)


def _coords():
    x, y, c = lax.axis_index("x"), lax.axis_index("y"), lax.axis_index("c")
    chips = [(1 - x, y), (x, 1 - y), (1 - x, 1 - y)]
    return x, y, c, 2 * x + y, (x, y, 1 - c), chips


def _chip_allgather(name, bufs):
    n = len(bufs)

    def body(*refs):
        outs = refs[n:2 * n]
        send, recv, fsend, frecv = refs[2 * n:]
        x, y, c, p, sib, chips = _coords()

        def direct(t, j, chip):
            return pltpu.make_async_remote_copy(src_ref=outs[t].at[p, c], dst_ref=outs[t].at[p, c], send_sem=send.at[t, j],
                                                recv_sem=recv.at[t, j], device_id=(*chip, c), device_id_type=MESH)

        def landed(t, j, chip):
            blk = outs[t].at[2 * chip[0] + chip[1], c]
            return pltpu.make_async_remote_copy(src_ref=blk, dst_ref=blk, send_sem=send.at[t, j],
                                                recv_sem=recv.at[t, j], device_id=(*chip, c), device_id_type=MESH)

        def passed(t, j, chip, half):
            blk = outs[t].at[2 * chip[0] + chip[1], half]
            return pltpu.make_async_remote_copy(src_ref=blk, dst_ref=blk, send_sem=fsend.at[t, j], recv_sem=frecv.at[t, j],
                                                device_id=sib, device_id_type=MESH)

        first = [direct(t, j, chip) for t in range(n) for j, chip in enumerate(chips)]
        for cp in first:
            cp.start()
        fwd = []
        for j, chip in enumerate(chips):
            for t in range(n):
                landed(t, j, chip).wait_recv()
                f = passed(t, j, chip, c)
                f.start()
                fwd.append(f)
        for j, chip in enumerate(chips):
            for t in range(n):
                passed(t, j, chip, 1 - c).wait_recv()
        for cp in first + fwd:
            cp.wait_send()

    return pl.pallas_call(
        body, name=name, in_specs=[ANY] * n, out_specs=[ANY] * n,
        out_shape=[jax.ShapeDtypeStruct(a.shape, a.dtype) for a in bufs],
        input_output_aliases={t: t for t in range(n)},
        scratch_shapes=[pltpu.SemaphoreType.DMA((n, 3))] * 4,
    )(*bufs)


SEM = pl.BlockSpec(memory_space=pltpu.SEMAPHORE)
TOKEN = jax.ShapeDtypeStruct((SUB, LANES), F32)


def _split_params():
    return pltpu.CompilerParams(has_side_effects=pltpu.SideEffectType.DATAFLOW_SIDE_EFFECTING)


def _struct(a):
    return jax.ShapeDtypeStruct(a.shape, a.dtype)


def _gather_start(name, bufs, deps):
    n, nd = len(bufs), len(deps)

    def body(*refs):
        outs = refs[n + nd:2 * n + nd]
        send, recv, token = refs[2 * n + nd:]
        x, y, c, p, sib, chips = _coords()
        for t in range(n):
            for j, chip in enumerate(chips):
                pltpu.make_async_remote_copy(src_ref=outs[t].at[p, c], dst_ref=outs[t].at[p, c], send_sem=send.at[3 * t + j],
                                             recv_sem=recv.at[3 * t + j], device_id=(*chip, c), device_id_type=MESH).start()
        token[...] = jnp.zeros_like(token)

    res = pl.pallas_call(
        body, name=name, in_specs=[ANY] * (n + nd), out_specs=[ANY] * n + [SEM, SEM, pl.BlockSpec(memory_space=pltpu.VMEM)],
        out_shape=[_struct(a) for a in bufs] + [pltpu.SemaphoreType.DMA((3 * n,)), pltpu.SemaphoreType.DMA((3 * n,)), TOKEN],
        input_output_aliases={t: t for t in range(n)}, compiler_params=_split_params(),
    )(*bufs, *deps)
    return list(res[:n]), res[n], res[n + 1], res[n + 2]


def _gather_wait(name, bufs, send, recv, after):
    n = len(bufs)

    def body(*refs):
        send_r, recv_r = refs[n], refs[n + 1]
        outs = refs[n + 3:2 * n + 3]
        x, y, c, p, sib, chips = _coords()
        for t in range(n):
            for j, chip in enumerate(chips):
                cp = pltpu.make_async_remote_copy(src_ref=outs[t].at[p, c], dst_ref=outs[t].at[2 * chip[0] + chip[1], c], send_sem=send_r.at[3 * t + j],
                                                  recv_sem=recv_r.at[3 * t + j], device_id=(*chip, c), device_id_type=MESH)
                cp.wait_send()
                cp.wait_recv()

    return list(pl.pallas_call(
        body, name=name, in_specs=[ANY] * n + [SEM, SEM, ANY], out_specs=[ANY] * n, out_shape=[_struct(a) for a in bufs],
        input_output_aliases={t: t for t in range(n)}, compiler_params=_split_params(),
    )(*bufs, send, recv, after))


def _gather_forward(name, bufs):
    n = len(bufs)

    def body(*refs):
        outs = refs[n:2 * n]
        fsend, frecv = refs[2 * n:]
        x, y, c, p, sib, chips = _coords()

        def passed(t, j, chip, half):
            blk = outs[t].at[2 * chip[0] + chip[1], half]
            return pltpu.make_async_remote_copy(src_ref=blk, dst_ref=blk, send_sem=fsend.at[t, j], recv_sem=frecv.at[t, j],
                                                device_id=sib, device_id_type=MESH)

        fwd = [passed(t, j, chip, c) for t in range(n) for j, chip in enumerate(chips)]
        for cp in fwd:
            cp.start()
        for t in range(n):
            for j, chip in enumerate(chips):
                passed(t, j, chip, 1 - c).wait_recv()
        for cp in fwd:
            cp.wait_send()

    return list(pl.pallas_call(
        body, name=name, in_specs=[ANY] * n, out_specs=[ANY] * n, out_shape=[_struct(a) for a in bufs],
        input_output_aliases={t: t for t in range(n)}, scratch_shapes=[pltpu.SemaphoreType.DMA((n, 3))] * 2,
    )(*bufs))


def _chip_exchange_start(name, sums):
    n = len(sums)
    lands = [lax.empty((3,) + a.shape[1:], a.dtype) for a in sums]

    def body(*refs):
        src, dst = refs[2 * n:3 * n], refs[3 * n:4 * n]
        send, recv, token = refs[4 * n:]
        x, y, c, p, sib, chips = _coords()
        for t in range(n):
            for j, chip in enumerate(chips):
                pltpu.make_async_remote_copy(src_ref=src[t].at[2 * chip[0] + chip[1]], dst_ref=dst[t].at[j], send_sem=send.at[3 * t + j],
                                             recv_sem=recv.at[3 * t + j], device_id=(*chip, c), device_id_type=MESH).start()
        token[...] = jnp.zeros_like(token)

    res = pl.pallas_call(
        body, name=name, in_specs=[ANY] * (2 * n), out_specs=[ANY] * (2 * n) + [SEM, SEM, pl.BlockSpec(memory_space=pltpu.VMEM)],
        out_shape=[_struct(a) for a in sums + lands] + [pltpu.SemaphoreType.DMA((3 * n,)), pltpu.SemaphoreType.DMA((3 * n,)), TOKEN],
        input_output_aliases={t: t for t in range(2 * n)}, compiler_params=_split_params(),
    )(*sums, *lands)
    return list(res[:n]), list(res[n:2 * n]), res[2 * n], res[2 * n + 1], res[2 * n + 2]


def _chip_exchange_wait(name, sums, lands, send, recv, after):
    n = len(sums)

    def body(*refs):
        send_r, recv_r = refs[2 * n], refs[2 * n + 1]
        src, dst = refs[2 * n + 3:3 * n + 3], refs[3 * n + 3:4 * n + 3]
        x, y, c, p, sib, chips = _coords()
        for t in range(n):
            for j, chip in enumerate(chips):
                cp = pltpu.make_async_remote_copy(src_ref=src[t].at[2 * chip[0] + chip[1]], dst_ref=dst[t].at[j], send_sem=send_r.at[3 * t + j],
                                                  recv_sem=recv_r.at[3 * t + j], device_id=(*chip, c), device_id_type=MESH)
                cp.wait_send()
                cp.wait_recv()

    res = pl.pallas_call(
        body, name=name, in_specs=[ANY] * (2 * n) + [SEM, SEM, ANY], out_specs=[ANY] * (2 * n),
        out_shape=[_struct(a) for a in sums + lands], input_output_aliases={t: t for t in range(2 * n)},
        compiler_params=_split_params(),
    )(*sums, *lands, send, recv, after)
    return list(res[:n]), list(res[n:])


def _pair_exchange(name, parts):
    n = len(parts)

    def body(*refs):
        ins, outs = refs[:n], refs[n:2 * n]
        send, recv = refs[2 * n:]
        x, y, c, p, sib, chips = _coords()
        cps = [pltpu.make_async_remote_copy(src_ref=ins[t].at[1 - c], dst_ref=outs[t], send_sem=send.at[t], recv_sem=recv.at[t],
                                            device_id=sib, device_id_type=MESH) for t in range(n)]
        for cp in cps:
            cp.start()
        for cp in cps:
            cp.wait()

    return pl.pallas_call(
        body, name=name, in_specs=[ANY] * n, out_specs=[ANY] * n,
        out_shape=[jax.ShapeDtypeStruct(a.shape[1:], a.dtype) for a in parts],
        scratch_shapes=[pltpu.SemaphoreType.DMA((n,))] * 2,
    )(*parts)


def _chip_exchange(name, sums):
    n = len(sums)

    def body(*refs):
        ins, outs = refs[:n], refs[n:2 * n]
        send, recv = refs[2 * n:]
        x, y, c, p, sib, chips = _coords()
        cps = [pltpu.make_async_remote_copy(src_ref=ins[t].at[2 * chip[0] + chip[1]], dst_ref=outs[t].at[j], send_sem=send.at[t, j],
                                            recv_sem=recv.at[t, j], device_id=(*chip, c), device_id_type=MESH)
               for t in range(n) for j, chip in enumerate(chips)]
        for cp in cps:
            cp.start()
        for cp in cps:
            cp.wait()

    return pl.pallas_call(
        body, name=name, in_specs=[ANY] * n, out_specs=[ANY] * n,
        out_shape=[jax.ShapeDtypeStruct((3,) + a.shape[1:], a.dtype) for a in sums],
        scratch_shapes=[pltpu.SemaphoreType.DMA((n, 3))] * 2,
    )(*sums)


def _pair_share(name, bufs, items):
    n = len(items)
    nb = len(bufs)

    def body(*refs):
        outs = refs[nb:2 * nb]
        send, recv = refs[2 * nb:]
        x, y, c, p, sib, chips = _coords()

        def blk(t, half):
            o, lead = items[t]
            return outs[o].at[p if lead == 'chip' else lead, half]

        def swap(t, half):
            return pltpu.make_async_remote_copy(src_ref=blk(t, half), dst_ref=blk(t, half), send_sem=send.at[t], recv_sem=recv.at[t],
                                                device_id=sib, device_id_type=MESH)

        cps = [swap(t, c) for t in range(n)]
        for cp in cps:
            cp.start()
        for t in range(n):
            swap(t, 1 - c).wait_recv()
        for cp in cps:
            cp.wait_send()

    return pl.pallas_call(
        body, name=name, in_specs=[ANY] * nb, out_specs=[ANY] * nb,
        out_shape=[jax.ShapeDtypeStruct(b.shape, b.dtype) for b in bufs],
        input_output_aliases={t: t for t in range(nb)},
        scratch_shapes=[pltpu.SemaphoreType.DMA((n,))] * 2,
    )(*bufs)


def _flat2(a, lead):
    return a.reshape(a.shape[:lead] + (-1, a.shape[-1]))


def _reduce_begin(tag, parts):
    c = lax.axis_index("c").astype(jnp.int32)
    got = _pair_exchange(f"rs_pair_exchange_{tag}", parts)
    sums = []
    for t, (mine, theirs) in enumerate(zip(parts, got)):
        m3, t2 = _flat2(mine, 1), theirs.reshape(-1, theirs.shape[-1])
        m3 = m3.reshape(2, -1, m3.shape[-1])
        cols = t2.shape[1]
        s = _rows(f"rs_pair_sum_{tag}_{t}", lambda a, b: (a.astype(F32) + b.astype(F32),),
                  [(m3, 's', cols, 0), (t2, 'r', cols, 0)], [('r', cols, BF16)], 512, pre=c.reshape(1))[0]
        sums.append(s.reshape(theirs.shape))
    sums, lands, send, recv, token = _chip_exchange_start(f"rs_chip_start_{tag}", sums)
    return (sums, lands, send, recv), token


def _reduce_end(tag, state, after, dests, bufs, buf_shapes):
    c = lax.axis_index("c").astype(jnp.int32)
    p = (2 * lax.axis_index("x") + lax.axis_index("y")).astype(jnp.int32)
    sums, lands = _chip_exchange_wait(f"rs_chip_wait_{tag}", *state, after)
    for t, (mine, theirs) in enumerate(zip(sums, lands)):
        o, lead = dests[t]
        shape = buf_shapes[o]
        rows, cols = shape[2], shape[3]
        m3, t3 = mine.reshape(N_CHIPS, rows, cols), theirs.reshape(3, rows, cols)
        pre = jnp.stack([p, jnp.int32(0), jnp.int32(1), jnp.int32(2), c, p if lead == 'chip' else jnp.int32(lead)])
        out = ('x', shape, F32, (None, None, 'tr', cols), lambda r, pr: (pr[5], pr[4], r, 0))
        bufs[o] = _rows(f"rs_chip_sum_{tag}_{t}", lambda a, b0, b1, b2: (((a.astype(F32) + b0.astype(F32)) + b1.astype(F32)) + b2.astype(F32),),
                        [(m3, 's', cols, 0), (t3, 's', cols, 1), (t3, 's', cols, 2), (t3, 's', cols, 3)], [out], 512, pre=pre, into=bufs[o])[0]


def kernel(x, norm_w, out_proj, s5_in_proj, s5_a_re, s5_a_im, s5_log_dt, s5_b_re, s5_b_im, s5_c_re, s5_c_im, s5_d, s5_w_glu, s5_b_glu, fox_in_proj, fox_q_norm, fox_k_norm, fox_f_bias, pool_in_proj, pool_w_group, pool_scale, loss_target, m_norm_w, m_out_proj, m_s5_in_proj, m_s5_a_re, m_s5_a_im, m_s5_log_dt, m_s5_b_re, m_s5_b_im, m_s5_c_re, m_s5_c_im, m_s5_d, m_s5_w_glu, m_s5_b_glu, m_fox_in_proj, m_fox_q_norm, m_fox_k_norm, m_fox_f_bias, m_pool_in_proj, m_pool_w_group, m_pool_scale, v_norm_w, v_out_proj, v_s5_in_proj, v_s5_a_re, v_s5_a_im, v_s5_log_dt, v_s5_b_re, v_s5_b_im, v_s5_c_re, v_s5_c_im, v_s5_d, v_s5_w_glu, v_s5_b_glu, v_fox_in_proj, v_fox_q_norm, v_fox_k_norm, v_fox_f_bias, v_pool_in_proj, v_pool_w_group, v_pool_scale):
    weights = dict(norm_w=norm_w, out_proj=out_proj, s5_in_proj=s5_in_proj, s5_a_re=s5_a_re, s5_a_im=s5_a_im, s5_log_dt=s5_log_dt,
                   s5_b_re=s5_b_re, s5_b_im=s5_b_im, s5_c_re=s5_c_re, s5_c_im=s5_c_im, s5_d=s5_d, s5_w_glu=s5_w_glu, s5_b_glu=s5_b_glu,
                   fox_in_proj=fox_in_proj, fox_q_norm=fox_q_norm, fox_k_norm=fox_k_norm, fox_f_bias=fox_f_bias,
                   pool_in_proj=pool_in_proj, pool_w_group=pool_w_group, pool_scale=pool_scale)
    mom_m = dict(norm_w=m_norm_w, out_proj=m_out_proj, s5_in_proj=m_s5_in_proj, s5_a_re=m_s5_a_re, s5_a_im=m_s5_a_im, s5_log_dt=m_s5_log_dt,
                 s5_b_re=m_s5_b_re, s5_b_im=m_s5_b_im, s5_c_re=m_s5_c_re, s5_c_im=m_s5_c_im, s5_d=m_s5_d, s5_w_glu=m_s5_w_glu, s5_b_glu=m_s5_b_glu,
                 fox_in_proj=m_fox_in_proj, fox_q_norm=m_fox_q_norm, fox_k_norm=m_fox_k_norm, fox_f_bias=m_fox_f_bias,
                 pool_in_proj=m_pool_in_proj, pool_w_group=m_pool_w_group, pool_scale=m_pool_scale)
    mom_v = dict(norm_w=v_norm_w, out_proj=v_out_proj, s5_in_proj=v_s5_in_proj, s5_a_re=v_s5_a_re, s5_a_im=v_s5_a_im, s5_log_dt=v_s5_log_dt,
                 s5_b_re=v_s5_b_re, s5_b_im=v_s5_b_im, s5_c_re=v_s5_c_re, s5_c_im=v_s5_c_im, s5_d=v_s5_d, s5_w_glu=v_s5_w_glu, s5_b_glu=v_s5_b_glu,
                 fox_in_proj=v_fox_in_proj, fox_q_norm=v_fox_q_norm, fox_k_norm=v_fox_k_norm, fox_f_bias=v_fox_f_bias,
                 pool_in_proj=v_pool_in_proj, pool_w_group=v_pool_w_group, pool_scale=v_pool_scale)
    return _step(x, loss_target, weights, mom_m, mom_v)


BIG = ('out_proj', 's5_in_proj', 's5_w_glu', 'fox_in_proj', 'pool_in_proj', 'pool_w_group')
SMALL = ('norm_w', 's5_a_re', 's5_a_im', 's5_log_dt', 's5_b_re', 's5_b_im', 's5_c_re', 's5_c_im', 's5_d', 's5_b_glu',
         'fox_q_norm', 'fox_k_norm', 'fox_f_bias', 'pool_scale')
SMALL_SHARDED = ('s5_d', 's5_b_glu', 'pool_scale')
GROUP_AXIS_1 = ('s5_a_re', 's5_a_im', 's5_b_re', 's5_b_im', 's5_c_re', 's5_c_im')
ORDER = ('norm_w', 'out_proj', 's5_in_proj', 's5_a_re', 's5_a_im', 's5_log_dt', 's5_b_re', 's5_b_im', 's5_c_re', 's5_c_im', 's5_d',
         's5_w_glu', 's5_b_glu', 'fox_in_proj', 'fox_q_norm', 'fox_k_norm', 'fox_f_bias', 'pool_in_proj', 'pool_w_group', 'pool_scale')


def _split2(shape):
    if shape[0] % 2 == 0:
        return (2, shape[0] // 2) + tuple(shape[1:])
    assert shape[0] == 1 and shape[1] % 2 == 0
    return (2, shape[1] // 2) + tuple(shape[2:])


def _cast_weights(w):
    p = (2 * lax.axis_index("x") + lax.axis_index("y")).astype(jnp.int32)
    bufs = {}
    for n in BIG:
        a3 = w[n].reshape(w[n].shape[0], -1, w[n].shape[-1])
        layers, rows, cols = a3.shape
        for l in range(layers):
            out = ('x', (N_CHIPS, rows, cols), BF16, (None, 'tr', cols), lambda r, pr: (pr[0], r, 0))
            b = _rows(f"cast_{n}_{l}", lambda v: (v,), [(a3, 's', cols, 1)], [out], 256, pre=jnp.stack([p, jnp.int32(l)]))[0]
            bufs[(n, l)] = b.reshape(N_CHIPS, 2, rows // 2, cols)
    return bufs


def _step(x, loss_target, w, mom_m, mom_v):
    T, D = x.shape[1], x.shape[2]
    E = D
    G, P, C = w['s5_a_re'].shape[1], S5_STATE, S5_GROUP
    H = E // FOX_HEAD_DIM
    PG = len(POOL_WINDOWS)
    PD = E // PG
    NC = G // GROUPS_PER_CHUNK
    L = GROUPS_PER_CHUNK * P
    tq = _t(256, T)
    nq = T // tq

    wb = _cast_weights(w)
    phases = [[('out_proj', 0), ('s5_in_proj', 0), ('s5_w_glu', 0)],
              [('out_proj', 1), ('fox_in_proj', 0)],
              [('out_proj', 2), ('pool_in_proj', 0), ('pool_w_group', 0), ('out_proj', 3), ('s5_in_proj', 1), ('s5_w_glu', 1)]]
    W = {}

    def landed(keys, bufs):
        for k, b in zip(keys, bufs):
            W[k] = b.reshape(N_CHIPS, 2 * b.shape[2], b.shape[3])

    landed(phases[0], _chip_allgather("gather_0", [wb[k] for k in phases[0]]))
    flight = _gather_start("gather_1_start", [wb[k] for k in phases[1]], [W[phases[0][0]]])
    small_full = {}
    chip = 2 * lax.axis_index("x") + lax.axis_index("y")
    sv = [lax.dynamic_update_index_in_dim(jnp.zeros((N_CHIPS, 2) + w[n].shape, F32), jnp.stack([w[n], w[n]]), chip, 0)
          for n in SMALL_SHARDED]
    got = _chip_allgather("gather_vectors", sv)
    for n, g in zip(SMALL_SHARDED, got):
        small_full[n] = jnp.transpose(g[:, 0], (1, 0, 2)).reshape(w[n].shape[0], E)

    norm_w = w['norm_w']
    h = x.reshape(T, D)
    saved = []
    dparts = {}

    def s5_consts(j):
        ar, ai, fr, fi = _s5_disc_fwd(f"s5_disc_{j}", w['s5_a_re'][j], w['s5_a_im'][j], w['s5_log_dt'][j].reshape(G, 1))
        br, bi = w['s5_b_re'][j].reshape(G * P, C), w['s5_b_im'][j].reshape(G * P, C)
        bbr, bbi = _s5_bbar(f"s5_bbar_{j}", fr.reshape(G * P, 1), fi.reshape(G * P, 1), br, bi)
        bbd = jnp.concatenate([_compact(bbr.reshape(G, P, C), NC), _compact(bbi.reshape(G, P, C), NC)], axis=2).astype(BF16)
        ct = lambda v: jnp.transpose(v, (0, 2, 1))
        cbd = jnp.concatenate([_compact(ct(w['s5_c_re'][j]), NC), -_compact(ct(w['s5_c_im'][j]), NC)], axis=2).astype(BF16)
        return dict(ar=ar, ai=ai, fr=fr, fi=fi, br=br, bi=bi, bbd=bbd, cbd=cbd,
                    ar3=ar.reshape(NC, 1, L), ai3=ai.reshape(NC, 1, L))

    for i in range(4):
        kind, j = i % 3, i // 3
        nw = norm_w[i].reshape(1, D)
        xn = _norm_fwd(f"norm_{i}", h, nw, deps=[flight[3]] if flight is not None else ())
        w_out = W[('out_proj', i)]
        if kind == 0:
            k5 = s5_consts(j)
            w_glu = W[('s5_w_glu', j)]
            proj = _mm_proj(f"s5_proj_{i}", xn, W[('s5_in_proj', j)])
            dsk = small_full['s5_d'][j].reshape(1, E)
            y1, g, hs = _s5_fwd(f"s5_scan_{i}", proj, k5['bbd'], k5['cbd'], k5['ar3'], k5['ai3'], dsk, E)
            bglu = small_full['s5_b_glu'][j].reshape(1, E)

            def glu_epi(acc, b, y1t, z):
                lin = acc + b
                return lin, (_gelu(y1t) * _sigmoid(lin)) * _silu(z)

            lin, a = _mm_rowsharded(
                f"s5_glu_{i}", g, w_glu, epi=glu_epi,
                extras=lambda tm, tn: [(bglu, _rowvec(tn)), (y1, _tile(tm, tn)), (proj, _tile(tm, tn, E // tn))],
                outs_fn=lambda tm, tn: [((T, E), F32, _tile(tm, tn)), ((T, E), BF16, _tile(tm, tn))])
            saved.append(dict(h=h, xn=xn, proj=proj, y1=y1, g=g, hs=hs, lin=lin, a=a, k5=k5, dsk=dsk))
        elif kind == 1:
            fox_w = jnp.transpose(W[('fox_in_proj', j)], (1, 0, 2)).reshape(D, -1)
            w_qkvz = fox_w[:, :4 * E]
            w_f = jnp.pad(fox_w[:, 4 * E:], ((0, 0), (0, LANES - H)))
            proj = _mm_plain(f"fox_proj_{i}", xn, w_qkvz)[0]
            flog = _mm_plain(f"fox_gate_proj_{i}", xn, w_f)[0]
            fb = jnp.pad(w['fox_f_bias'][j].reshape(1, H), ((0, 0), (0, LANES - H)))
            wq, wk = w['fox_q_norm'][j].reshape(1, FOX_HEAD_DIM), w['fox_k_norm'][j].reshape(1, FOX_HEAD_DIM)
            qn, kn = _qk_norm(f"fox_qk_norm_{i}", proj, wq, wk, H)
            cum = _cum_rows(f"fox_cum_{i}", flog, fb, False, True)
            cum_t = jnp.transpose(cum)[:H]
            cum_q = jnp.broadcast_to(cum_t[:, :, None], (H, T, LANES))
            cum_k = cum_t.reshape(H, nq, 1, tq)
            y, lse = _attn_fwd(f"fox_attn_{i}", qn, kn, proj, cum_q, cum_k, H)
            a = _rows(f"fox_gate_{i}", lambda yt, z: (yt * _silu(z),), [(y, 'r', E, 0), (proj, 'r', E, 3)], [('r', E, BF16)], 256)[0]
            saved.append(dict(h=h, xn=xn, proj=proj, flog=flog, fb=fb, wq=wq, wk=wk, qn=qn, kn=kn, cum_q=cum_q, cum_k=cum_k, y=y, lse=lse, a=a,
                              w_qkvz=w_qkvz, w_f=w_f))
        else:
            w_pg = W[('pool_w_group', j)].reshape(N_CHIPS, PG, PD // N_CHIPS, PD)
            proj = _mm_proj(f"pool_proj_{i}", xn, W[('pool_in_proj', j)])
            pm = _pool_fwd(f"pool_win_{i}", proj, E)
            scale = small_full['pool_scale'][j].reshape(1, E)
            tm, tn, tk = _t(512, T), _t(512, PD), w_pg.shape[2]
            kb, nb = PD // tk, PD // tn
            mixed, a = _mm(
                f"pool_mix_{i}", pm, w_pg, M=T, N=PD, K=PD, tm=tm, tn=tn, tk=tk, groups=PG,
                a_spec=_bs((tm, tk), lambda g, m, n, k: (m, g * kb + k)),
                b_spec=_bs((None, None, tk, tn), lambda g, m, n, k: (k, g, 0, n)),
                extras=[(scale, _bs((1, tn), lambda g, m, n, k: (0, g * nb + n))),
                        (proj, _bs((tm, tn), lambda g, m, n, k: (m, E // tn + g * nb + n)))],
                epi=lambda acc, sc, z: (acc, (acc * sc) * _silu(z)),
                outs=[((T, E), F32, _bs((tm, tn), lambda g, m, n, k: (m, g * nb + n))),
                      ((T, E), BF16, _bs((tm, tn), lambda g, m, n, k: (m, g * nb + n)))])
            saved.append(dict(h=h, xn=xn, proj=proj, pm=pm, mixed=mixed, scale=scale, a=a, w_pg=w_pg))
        h = _mm_rowsharded(f"out_proj_{i}", saved[-1]['a'], w_out, epi=lambda acc, r: (r + acc,),
                           extras=lambda tm, tn: [(h, _tile(tm, tn))],
                           outs_fn=lambda tm, tn: [((T, D), F32, _tile(tm, tn))])[0]
        if flight is not None:
            ph = 1 if i == 0 else 2
            got = _gather_wait(f"gather_{ph}_wait", flight[0], flight[1], flight[2], h)
            got = _gather_forward(f"gather_{ph}_pass", got)
            landed(phases[ph], got)
            flight = _gather_start("gather_2_start", [wb[k] for k in phases[2]], [got[0]]) if ph == 1 else None

    dh, dh16, loss_cols = _loss(h, loss_target.reshape(T, D))
    loss = lax.psum(jnp.sum(loss_cols), ("x", "y", "c"))

    gsmall = {n: [None] * w[n].shape[0] for n in SMALL}
    big_index = {n: o for o, n in enumerate(BIG)}
    rs_shapes = [None] * (len(BIG) + 1)
    rs_bufs = [None] * (len(BIG) + 1)
    rs_dests_all = []
    pending = None

    def reduce_layer(tag, named_parts):
        parts, dests = [], []
        for n, l, pt in named_parts:
            o = big_index[n] if n in big_index else len(BIG)
            half = pt.shape[2:]
            rs_shapes[o] = (N_CHIPS if l == 'chip' else w[n].shape[0], 2, math.prod(half[:-1]), half[-1])
            parts.append(pt)
            dests.append((o, l))
        rs_dests_all.extend(dests)
        state, token = _reduce_begin(tag, parts)
        return (tag, state, dests), token

    token = None
    for i in reversed(range(4)):
        kind, j = i % 3, i // 3
        sv_ = saved[i]
        nw = norm_w[i].reshape(1, D)
        w_out = W[('out_proj', i)]
        after_start = [token] if token is not None else ()
        layer_parts = [('out_proj', i, _mm_dw_rows(f"d_out_proj_{i}", sv_['a'], dh16, deps=after_start))]
        if kind == 0:
            w_glu = W[('s5_w_glu', j)]
            proj, y1, lin, k5 = sv_['proj'], sv_['y1'], sv_['lin'], sv_['k5']

            def da_epi(da, y1t, lint, z):
                gt, sg = _gelu(y1t), _sigmoid(lint)
                dy2 = da * _silu(z)
                dlin = (dy2 * gt) * (sg * (1.0 - sg))
                return da * (gt * sg) * _dsilu(z), dlin, dy2 * sg, _colsum(dlin)

            nm = T // _t(512, T)
            dz, dlin, dgd, dbg = _mm_rowsharded_t(
                f"d_s5_act_{i}", dh16, w_out, epi=da_epi, deps=after_start,
                extras=lambda tm, tn: [(y1, _tile(tm, tn)), (lin, _tile(tm, tn)), (proj, _tile(tm, tn, E // tn))],
                outs_fn=lambda tm, tn: [((T, E), BF16, _tile(tm, tn)), ((T, E), BF16, _tile(tm, tn)), ((T, E), F32, _tile(tm, tn)),
                                        ((nm, 1, E), F32, _bs((None, 1, tn), lambda g, m, n, k: (m, 0, n)))])
            gsmall['s5_b_glu'][j] = jnp.sum(dbg, axis=(0, 1))
            layer_parts.append(('s5_w_glu', j, _mm_dw_rows(f"d_s5_w_glu_{i}", sv_['g'], dlin)))
            dy1 = _mm_rowsharded_t(
                f"d_s5_glu_{i}", dlin, w_glu, epi=lambda acc, d, y1t: ((acc + d) * _dgelu(y1t),),
                extras=lambda tm, tn: [(dgd, _tile(tm, tn)), (y1, _tile(tm, tn))],
                outs_fn=lambda tm, tn: [((T, E), F32, _tile(tm, tn))])[0]
            du, dbd, dcd, dab, ddk = _s5_bwd(f"d_s5_scan_{i}", dy1, proj, sv_['hs'], k5['bbd'], k5['cbd'], k5['ar3'], k5['ai3'], sv_['dsk'], E)
            gsmall['s5_d'][j] = ddk.reshape(E)
            gsmall['s5_c_re'][j] = jnp.transpose(_uncompact(dcd[:, :, :L], G), (0, 2, 1))
            gsmall['s5_c_im'][j] = -jnp.transpose(_uncompact(dcd[:, :, L:], G), (0, 2, 1))
            dbbr = _uncompact(dbd[:, :, :L], G).reshape(G * P, C)
            dbbi = _uncompact(dbd[:, :, L:], G).reshape(G * P, C)
            dbr, dbi, dfr, dfi = _s5_bbar_bwd(f"d_s5_bbar_{i}", k5['fr'].reshape(G * P, 1), k5['fi'].reshape(G * P, 1), k5['br'], k5['bi'], dbbr, dbbi)
            gsmall['s5_b_re'][j] = dbr.reshape(G, P, C)
            gsmall['s5_b_im'][j] = dbi.reshape(G, P, C)
            dab = jnp.sum(dab, axis=1)
            dare, daim, dldt = _s5_disc_bwd(f"d_s5_disc_{i}", w['s5_a_re'][j], w['s5_a_im'][j], w['s5_log_dt'][j].reshape(G, 1),
                                            (dab[:, :L].reshape(G, P), dab[:, L:].reshape(G, P), dfr.reshape(G, P), dfi.reshape(G, P)))
            gsmall['s5_a_re'][j], gsmall['s5_a_im'][j], gsmall['s5_log_dt'][j] = dare, daim, dldt.reshape(G)
            dproj = jnp.concatenate([du, dz], axis=1)
            layer_parts.append(('s5_in_proj', j, _mm_dw_cols(f"d_s5_in_proj_{i}", sv_['xn'], dproj)))
            dxn = _mm_colsharded_t(f"d_s5_xn_{i}", dproj, W[('s5_in_proj', j)])
        elif kind == 1:
            proj, y = sv_['proj'], sv_['y']
            do, dz = _mm_rowsharded_t(
                f"d_fox_act_{i}", dh16, w_out, epi=lambda da, yt, z: (da * _silu(z), (da * yt) * _dsilu(z)), deps=after_start,
                extras=lambda tm, tn: [(y, _tile(tm, tn)), (proj, _tile(tm, tn, 3 * E // tn))],
                outs_fn=lambda tm, tn: [((T, E), F32, _tile(tm, tn)), ((T, E), BF16, _tile(tm, tn))])
            dqn, dkn, dv, dcq, dck = _attn_bwd(f"d_fox_attn_{i}", sv_['qn'], sv_['kn'], proj, do, y, sv_['lse'], sv_['cum_q'], sv_['cum_k'], H)
            dq, dk, dwq, dwk = _qk_norm_bwd(f"d_fox_qk_norm_{i}", proj, sv_['wq'], sv_['wk'], dqn, dkn, H)
            gsmall['fox_q_norm'][j], gsmall['fox_k_norm'][j] = dwq.reshape(-1), dwk.reshape(-1)
            dcum_t = dcq[:, :, 0] + dck.reshape(H, T)
            dcum = jnp.pad(jnp.transpose(dcum_t), ((0, 0), (0, LANES - H)))
            dls = _cum_rows(f"d_fox_cum_{i}", dcum, jnp.zeros((1, LANES), F32), True, False)
            dflog, dfb = _rows(f"d_fox_gate_{i}", lambda d, f, b: ((lambda r: (r, _colsum(r)))(d * _sigmoid(-(f + b)))),
                               [(dls, 'r', LANES, 0), (sv_['flog'], 'r', LANES, 0), (sv_['fb'], 'b', LANES, 0)],
                               [('r', LANES, BF16), ('a', LANES, F32)], 256)
            gsmall['fox_f_bias'][j] = dfb[0, :H]
            dproj = jnp.concatenate([dq, dk, dv, dz], axis=1)
            tkT = _t(K_STEP, T)
            dw_qkvz = _mm(f"d_fox_in_proj_{i}", sv_['xn'], dproj, M=D, N=4 * E, K=T, tm=_t(512, D), tn=_t(1024, 4 * E), tk=tkT, ta=True,
                          a_spec=_bs((tkT, _t(512, D)), lambda g, m, n, k: (k, m)),
                          b_spec=_bs((tkT, _t(1024, 4 * E)), lambda g, m, n, k: (k, n)),
                          outs=[((D, 4 * E), BF16, _tile(_t(512, D), _t(1024, 4 * E)))])[0]
            dw_f = _mm(f"d_fox_gate_proj_{i}", sv_['xn'], dflog, M=D, N=LANES, K=T, tm=_t(512, D), tn=LANES, tk=tkT, ta=True,
                       a_spec=_bs((tkT, _t(512, D)), lambda g, m, n, k: (k, m)),
                       b_spec=_bs((tkT, LANES), lambda g, m, n, k: (k, n)),
                       outs=[((D, LANES), BF16, _tile(_t(512, D), LANES))])[0]
            dw_fox = jnp.concatenate([dw_qkvz, dw_f[:, :H]], axis=1)
            sw = dw_fox.shape[1] // N_CHIPS
            layer_parts.append(('fox_in_proj', j, jnp.transpose(dw_fox.reshape(2, D // 2, N_CHIPS, sw), (0, 2, 1, 3))))
            w_qkvz, w_f = sv_['w_qkvz'], sv_['w_f']
            dxn_f = _mm(f"d_fox_xn_gate_{i}", dflog, w_f, M=T, N=D, K=LANES, tm=_t(512, T), tn=_t(1024, D), tk=LANES, tb=True,
                        a_spec=_bs((_t(512, T), LANES), lambda g, m, n, k: (m, k)),
                        b_spec=_bs((_t(1024, D), LANES), lambda g, m, n, k: (n, k)),
                        outs=[((T, D), F32, _tile(_t(512, T), _t(1024, D)))])[0]
            tm, tn, tk = _t(512, T), _t(1024, D), _t(1024, 4 * E)
            dxn = _mm(f"d_fox_xn_{i}", dproj, w_qkvz, M=T, N=D, K=4 * E, tm=tm, tn=tn, tk=tk, tb=True,
                      a_spec=_bs((tm, tk), lambda g, m, n, k: (m, k)), b_spec=_bs((tn, tk), lambda g, m, n, k: (n, k)),
                      extras=[(dxn_f, _tile(tm, tn))], epi=lambda acc, e: (acc + e,),
                      outs=[((T, D), F32, _tile(tm, tn))])[0]
        else:
            proj, mixed, scale = sv_['proj'], sv_['mixed'], sv_['scale']
            nm = T // _t(512, T)

            def pool_epi(da, mx, sc, z):
                dy = da * _silu(z)
                return (da * (mx * sc)) * _dsilu(z), dy * sc, _colsum(dy * mx)

            dz, dmix, dsc = _mm_rowsharded_t(
                f"d_pool_act_{i}", dh16, w_out, epi=pool_epi, deps=after_start,
                extras=lambda tm, tn: [(mixed, _tile(tm, tn)), (scale, _rowvec(tn)), (proj, _tile(tm, tn, E // tn))],
                outs_fn=lambda tm, tn: [((T, E), BF16, _tile(tm, tn)), ((T, E), BF16, _tile(tm, tn)),
                                        ((nm, 1, E), F32, _bs((None, 1, tn), lambda g, m, n, k: (m, 0, n)))])
            gsmall['pool_scale'][j] = jnp.sum(dsc, axis=(0, 1))
            w_pg = sv_['w_pg']
            tkw = w_pg.shape[2]
            tk = _t(K_STEP, T)
            layer_parts.append(('pool_w_group', j, _mm(
                f"d_pool_w_group_{i}", sv_['pm'], dmix, M=PD, N=PD, K=T, tm=tkw, tn=PD, tk=tk, groups=PG, ta=True,
                a_spec=_bs((tk, tkw), lambda g, m, n, k: (k, g * (PD // tkw) + m)),
                b_spec=_bs((tk, PD), lambda g, m, n, k: (k, g)),
                outs=[((2, N_CHIPS, PG // 2, tkw, PD), BF16, _bs((None, None, None, tkw, PD), lambda g, m, n, k: (g // (PG // 2), m, g % (PG // 2), 0, 0)))])[0]))
            tm, tk2 = _t(512, T), _t(512, PD)
            dpm = _mm(f"d_pool_mix_{i}", dmix, w_pg, M=T, N=PD, K=PD, tm=tm, tn=tkw, tk=tk2, groups=PG, tb=True,
                      a_spec=_bs((tm, tk2), lambda g, m, n, k: (m, g * (PD // tk2) + k)),
                      b_spec=_bs((None, None, tkw, tk2), lambda g, m, n, k: (n, g, 0, k)),
                      outs=[((T, E), F32, _bs((tm, tkw), lambda g, m, n, k: (m, g * (PD // tkw) + n)))])[0]
            du = _pool_bwd(f"d_pool_win_{i}", dpm, E)
            dproj = jnp.concatenate([du, dz], axis=1)
            layer_parts.append(('pool_in_proj', j, _mm_dw_cols(f"d_pool_in_proj_{i}", sv_['xn'], dproj)))
            dxn = _mm_colsharded_t(f"d_pool_xn_{i}", dproj, W[('pool_in_proj', j)])
        dh, dh16, dnw = _norm_bwd(f"d_norm_{i}", dxn, sv_['h'], nw, dh)
        gsmall['norm_w'][i] = dnw.reshape(D)
        if pending is not None:
            _reduce_end(pending[0], pending[1], dh16, pending[2], rs_bufs, rs_shapes)
        if i > 0:
            pending, token = reduce_layer(f"l{i}", layer_parts)
    grad_x = dh.reshape(x.shape)

    small_flat = jnp.concatenate([jnp.stack(gsmall[n]).reshape(-1) for n in SMALL])
    n_small = small_flat.shape[0]
    unit = 2 * N_CHIPS * 16 * LANES
    n_pad = -(-n_small // unit) * unit
    R = n_pad // (2 * N_CHIPS * LANES)
    small_part = jnp.pad(small_flat, (0, n_pad - n_small)).astype(BF16).reshape(2, N_CHIPS, R, LANES)
    pending, token = reduce_layer("l0", layer_parts + [('small', 'chip', small_part)])
    _reduce_end(pending[0], pending[1], token, pending[2], rs_bufs, rs_shapes)
    red = _pair_share("rs_pair_share", rs_bufs, rs_dests_all)
    grads = {n: r.reshape(w[n].shape) for n, r in zip(BIG, red[:len(BIG)])}
    small_all = _chip_allgather("gather_small_grads", [red[len(BIG)]])[0]
    small_all = jnp.transpose(small_all, (1, 0, 2, 3)).reshape(-1)[:n_small]
    off = 0
    p = 2 * lax.axis_index("x") + lax.axis_index("y")
    for n in SMALL:
        full_shape = (w[n].shape[0], E) if n in SMALL_SHARDED else w[n].shape
        size = math.prod(full_shape)
        gfull = small_all[off:off + size].reshape(full_shape)
        off += size
        if n in SMALL_SHARDED:
            gfull = lax.dynamic_slice_in_dim(gfull, p * (E // N_CHIPS), E // N_CHIPS, axis=1)
        grads[n] = gfull

    delta, new_m, new_v = {}, {}, {}
    for n in BIG:
        f2 = lambda a: a.reshape(-1, a.shape[-1])
        d_, m_, v_ = _adamw(f"adamw_{n}", f2(w[n]), f2(grads[n]), f2(mom_m[n]), f2(mom_v[n]))
        delta[n], new_m[n], new_v[n] = d_.reshape(w[n].shape), m_.reshape(w[n].shape), v_.reshape(w[n].shape)
    for n in SMALL:
        shape = w[n].shape
        if n in GROUP_AXIS_1:
            perm = (0,) + tuple(range(2, len(shape))) + (1,)
            inv = (0, len(shape) - 1) + tuple(range(1, len(shape) - 1))
            view = lambda a: jnp.transpose(a, perm).reshape(-1, shape[1])
            back = lambda a: jnp.transpose(a.reshape(tuple(shape[k] for k in perm)), inv)
        else:
            view = lambda a: a.reshape(-1, shape[-1])
            back = lambda a: a.reshape(shape)
        d_, m_, v_ = _adamw(f"adamw_{n}", view(w[n]), view(grads[n]), view(mom_m[n]), view(mom_v[n]))
        delta[n], new_m[n], new_v[n] = back(d_), back(m_), back(v_)
    return (loss, grad_x, *[grads[n] for n in ORDER], *[delta[n] for n in ORDER], *[new_m[n] for n in ORDER], *[new_v[n] for n in ORDER])
```

```python
import functools
import math

import jax
import jax.numpy as jnp
from jax import lax
from jax.experimental import pallas as pl
from jax.experimental.pallas import tpu as pltpu

F32 = jnp.float32
BF16 = jnp.bfloat16
MESH = pl.DeviceIdType.MESH

N_CHIPS = 4
VMEM_LIMIT = 56 * 1024 * 1024
LANES = 128
SUB = 8

EPS = 1e-6
S5_GROUP = 16
S5_STATE = 64
GROUPS_PER_CHUNK = 16
FOX_HEAD_DIM = 128
ATTN_SUB = 256
POOL_WINDOWS = (2, 4, 8, 16)
POOL_HALO = 16
ADAM_LR, ADAM_B1, ADAM_B2, ADAM_EPS, ADAM_WD, ADAM_STEP = 0.001, 0.9, 0.999, 1e-08, 0.01, 10
NEG = -1e30
K_STEP = 2048


ANY = pl.BlockSpec(memory_space=pl.ANY)


def _t(pref, dim):
    if dim <= pref:
        return dim
    t = pref - pref % 16
    while t > 16 and dim % t:
        t -= 16
    assert dim % t == 0, (pref, dim)
    return t


def _params(sem):
    return pltpu.CompilerParams(dimension_semantics=sem, vmem_limit_bytes=VMEM_LIMIT)


def _sigmoid(x):
    return 1.0 / (1.0 + jnp.exp(-x))


def _silu(z):
    return z * _sigmoid(z)


def _dsilu(z):
    s = _sigmoid(z)
    return s * (1.0 + z * (1.0 - s))


_GELU_C = math.sqrt(2.0 / math.pi)


def _gelu(x):
    return 0.5 * x * (1.0 + jnp.tanh(_GELU_C * (x + 0.044715 * (x * x * x))))


def _dgelu(x):
    t = jnp.tanh(_GELU_C * (x + 0.044715 * (x * x * x)))
    return 0.5 * (1.0 + t) + 0.5 * x * (1.0 - t * t) * (_GELU_C * (1.0 + 3.0 * 0.044715 * x * x))


def _log_sigmoid(x):
    return jnp.minimum(x, 0.0) - jnp.log(1.0 + jnp.exp(-jnp.abs(x)))


def _rms(x):
    return lax.rsqrt(jnp.mean(x * x, axis=-1, keepdims=True) + EPS)


def _rms_bwd(x, w, dy):
    r = _rms(x)
    xhat = x * r
    dxh = dy * w
    dx = r * (dxh - xhat * jnp.mean(dxh * xhat, axis=-1, keepdims=True))
    return dx, dy * xhat


def _rows(name, fn, ins, outs, tr, pre=None, into=None, deps=()):
    rows = None
    for arr, kind, cols, cb in ins:
        if kind == 'r':
            rows = arr.shape[0]
        elif kind == 's' and rows is None:
            rows = arr.shape[1]
    tr = _t(tr, rows)
    n_in = len(ins)
    has_acc = any(o[0] == 'a' for o in outs)

    def spec(kind, cols, cb):
        if kind == 'r':
            return pl.BlockSpec((tr, cols), lambda r, *p: (r, cb))
        if kind == 'b':
            return pl.BlockSpec((1, cols), lambda r, *p: (0, cb))
        return pl.BlockSpec((None, tr, cols), lambda r, p: (p[cb], r, 0))

    in_specs = [spec(kind, cols, cb) for _, kind, cols, cb in ins]
    out_specs, out_shape = [], []
    for o in outs:
        if o[0] == 'r':
            out_specs.append(pl.BlockSpec((tr, o[1]), lambda r, *p: (r, 0)))
            out_shape.append(jax.ShapeDtypeStruct((rows, o[1]), o[2]))
        elif o[0] == 'a':
            out_specs.append(pl.BlockSpec((1, o[1]), lambda r, *p: (0, 0)))
            out_shape.append(jax.ShapeDtypeStruct((1, o[1]), o[2]))
        else:
            blk = tuple(tr if d == 'tr' else d for d in o[3])
            out_specs.append(pl.BlockSpec(blk, o[4]))
            out_shape.append(jax.ShapeDtypeStruct(o[1], o[2]))
    n_pre = 0 if pre is None else 1
    args = [a[0] for a in ins]
    aliases = {}
    if into is not None:
        in_specs.append(ANY)
        args.append(into)
        aliases = {n_pre + n_in: 0}
    in_specs += [ANY] * len(deps)
    args += list(deps)
    n_all = len(args)

    def body(*refs):
        refs = refs[n_pre:]
        res = fn(*[r[...] for r in refs[:n_in]])
        for spec_o, o, v in zip(outs, refs[n_all:], res):
            if spec_o[0] == 'a':
                @pl.when(pl.program_id(0) == 0)
                def _():
                    o[...] = jnp.zeros_like(o)
                o[...] += v.astype(o.dtype)
            else:
                o[...] = v.astype(o.dtype)

    grid_spec = pltpu.PrefetchScalarGridSpec(num_scalar_prefetch=n_pre, grid=(rows // tr,), in_specs=in_specs, out_specs=out_specs)
    if pre is not None:
        args = [pre] + args
    return pl.pallas_call(body, name=name, grid_spec=grid_spec, out_shape=out_shape, input_output_aliases=aliases,
                          compiler_params=_params(("arbitrary" if has_acc else "parallel",)))(*args)


def _colsum(v):
    return jnp.sum(v, axis=0, keepdims=True)


def _mm(name, a, b, *, M, N, K, tm, tn, tk, a_spec, b_spec, outs, epi=None, extras=(), groups=1, ta=False, tb=False, deps=()):
    nk = K // tk
    assert M % tm == 0 and N % tn == 0 and K % tk == 0, (name, M, N, K, tm, tn, tk)
    dims = (((0 if ta else 1,), (1 if tb else 0,)), ((), ()))
    n_ex = len(extras)

    def body(*refs):
        a_ref, b_ref = refs[0], refs[1]
        ex = refs[2:2 + n_ex]
        out_refs = refs[2 + n_ex + len(deps):2 + n_ex + len(deps) + len(outs)]

        def finish(r):
            res = (r,) if epi is None else epi(r, *[e[...] for e in ex])
            for o, v in zip(out_refs, res):
                o[...] = v.astype(o.dtype)

        part = lax.dot_general(a_ref[...].astype(BF16), b_ref[...].astype(BF16), dims, preferred_element_type=F32)
        if nk == 1:
            finish(part)
            return
        acc = refs[-1]
        k = pl.program_id(3)

        @pl.when(k == 0)
        def _():
            acc[...] = part

        @pl.when(k > 0)
        def _():
            acc[...] += part

        @pl.when(k == nk - 1)
        def _():
            finish(acc[...])

    return pl.pallas_call(
        body, name=name, grid=(groups, M // tm, N // tn, nk),
        in_specs=[a_spec, b_spec] + [s for _, s in extras] + [ANY] * len(deps),
        out_specs=[s for _, _, s in outs],
        out_shape=[jax.ShapeDtypeStruct(sh, dt) for sh, dt, _ in outs],
        scratch_shapes=[] if nk == 1 else [pltpu.VMEM((tm, tn), F32)],
        compiler_params=_params(("parallel", "parallel", "parallel", "arbitrary")),
    )(a, b, *[e for e, _ in extras], *deps)


def _bs(shape, f):
    return pl.BlockSpec(shape, f)


def _tile(tm, tn, coff=0):
    return _bs((tm, tn), lambda g, m, n, k: (m, n + coff))


def _rowvec(tn, coff=0):
    return _bs((1, tn), lambda g, m, n, k: (0, n + coff))


def _mm_proj(name, xn, w, *, epi=None, extras=(), out_dtype=F32):
    T, D = xn.shape
    sw = w.shape[2]
    N = N_CHIPS * sw
    tm, tn, tk = _t(512, T), _t(1024, sw), _t(K_STEP, D)
    nb = sw // tn
    return _mm(name, xn, w, M=T, N=N, K=D, tm=tm, tn=tn, tk=tk,
               a_spec=_bs((tm, tk), lambda g, m, n, k: (m, k)),
               b_spec=_bs((None, tk, tn), lambda g, m, n, k: (n // nb, k, n % nb)),
               outs=[((T, N), out_dtype, _tile(tm, tn))], epi=epi, extras=extras)[0]


def _mm_plain(name, a, b, *, out_dtype=F32, epi=None, extras=(), outs=None, tn_pref=1024):
    M, K = a.shape
    N = b.shape[1]
    tm, tn, tk = _t(512, M), _t(tn_pref, N), _t(K_STEP, K)
    if outs is None:
        outs = [((M, N), out_dtype, _tile(tm, tn))]
    return _mm(name, a, b, M=M, N=N, K=K, tm=tm, tn=tn, tk=tk,
               a_spec=_bs((tm, tk), lambda g, m, n, k: (m, k)),
               b_spec=_bs((tk, tn), lambda g, m, n, k: (k, n)),
               outs=outs, epi=epi, extras=extras)


def _mm_rowsharded(name, a, w, *, epi, extras, outs_fn, deps=()):
    T, E = a.shape
    tk = w.shape[1]
    N = w.shape[2]
    tm, tn = _t(512, T), _t(1024, N)
    return _mm(name, a, w, M=T, N=N, K=E, tm=tm, tn=tn, tk=tk, deps=deps,
               a_spec=_bs((tm, tk), lambda g, m, n, k: (m, k)),
               b_spec=_bs((None, tk, tn), lambda g, m, n, k: (k, 0, n)),
               outs=outs_fn(tm, tn), epi=epi, extras=extras(tm, tn))


def _mm_rowsharded_t(name, d, w, *, epi, extras, outs_fn, deps=()):
    T, N = d.shape
    tn = w.shape[1]
    E = N_CHIPS * tn
    tm, tk = _t(512, T), _t(K_STEP, N)
    return _mm(name, d, w, M=T, N=E, K=N, tm=tm, tn=tn, tk=tk, tb=True, deps=deps,
               a_spec=_bs((tm, tk), lambda g, m, n, k: (m, k)),
               b_spec=_bs((None, tn, tk), lambda g, m, n, k: (n, 0, k)),
               outs=outs_fn(tm, tn), epi=epi, extras=extras(tm, tn))


def _mm_colsharded_t(name, d, w):
    T, N = d.shape
    D, sw = w.shape[1], w.shape[2]
    tm, tn, tk = _t(512, T), _t(1024, D), _t(1024, sw)
    kb = sw // tk
    return _mm(name, d, w, M=T, N=D, K=N, tm=tm, tn=tn, tk=tk, tb=True,
               a_spec=_bs((tm, tk), lambda g, m, n, k: (m, k)),
               b_spec=_bs((None, tn, tk), lambda g, m, n, k: (k // kb, n, k % kb)),
               outs=[((T, D), F32, _tile(tm, tn))])[0]


def _mm_dw_rows(name, a, d, deps=()):
    T, E = a.shape
    N = d.shape[1]
    tm, tn, tk = E // (2 * N_CHIPS), _t(2048, N), _t(K_STEP, T)
    return _mm(name, a, d, M=E, N=N, K=T, tm=tm, tn=tn, tk=tk, ta=True, deps=deps,
               a_spec=_bs((tk, tm), lambda g, m, n, k: (k, m)),
               b_spec=_bs((tk, tn), lambda g, m, n, k: (k, n)),
               outs=[((2, N_CHIPS, tm, N), BF16, _bs((None, None, tm, tn), lambda g, m, n, k: (m % 2, m // 2, 0, n)))])[0]


def _mm_dw_cols(name, xn, d):
    T, D = xn.shape
    N = d.shape[1]
    sw = N // N_CHIPS
    tm, tn, tk = _t(512, D // 2), _t(1024, sw), _t(K_STEP, T)
    mh, nb = (D // 2) // tm, sw // tn
    return _mm(name, xn, d, M=D, N=N, K=T, tm=tm, tn=tn, tk=tk, ta=True,
               a_spec=_bs((tk, tm), lambda g, m, n, k: (k, m)),
               b_spec=_bs((tk, tn), lambda g, m, n, k: (k, n)),
               outs=[((2, N_CHIPS, D // 2, sw), BF16,
                      _bs((None, None, tm, tn), lambda g, m, n, k: (m // mh, n // nb, m % mh, n % nb)))])[0]


def _norm_fwd(name, h, w, deps=()):
    D = h.shape[1]
    return _rows(name, lambda x, g: ((x * _rms(x)) * g,), [(h, 'r', D, 0), (w, 'b', D, 0)], [('r', D, BF16)], 256, deps=deps)[0]


def _norm_bwd(name, dxn, h, w, dh):
    D = h.shape[1]

    def fn(dy, x, g, up):
        dx, dwt = _rms_bwd(x, g, dy)
        r = up + dx
        return r, r, _colsum(dwt)

    return _rows(name, fn, [(dxn, 'r', D, 0), (h, 'r', D, 0), (w, 'b', D, 0), (dh, 'r', D, 0)],
                 [('r', D, F32), ('r', D, BF16), ('a', D, F32)], 256)


def _loss(h, target):
    D = h.shape[1]

    def fn(y, t):
        e = y - t
        d = e * (1.0 / D)
        return d, d, _colsum(e * e) * (0.5 / D)

    return _rows("loss", fn, [(h, 'r', D, 0), (target, 'r', D, 0)], [('r', D, F32), ('r', D, BF16), ('a', D, F32)], 256)


def _adamw(name, w, g, m, v):
    cols = w.shape[1]

    def fn(w, g, m, v):
        m = ADAM_B1 * m + (1.0 - ADAM_B1) * g
        v = ADAM_B2 * v + (1.0 - ADAM_B2) * (g * g)
        m_hat = m / (1.0 - ADAM_B1 ** ADAM_STEP)
        v_hat = v / (1.0 - ADAM_B2 ** ADAM_STEP)
        delta = -ADAM_LR * (m_hat / (jnp.sqrt(v_hat) + ADAM_EPS) + ADAM_WD * w)
        return delta, m, v

    rows = w.shape[0]
    if rows % SUB == 0 or rows <= 256:
        return _rows(name, fn, [(x, 'r', cols, 0) for x in (w, g, m, v)], [('r', cols, F32)] * 3, 256)
    tc = _t(256, cols)
    assert tc % LANES == 0, (rows, cols)

    def body(w_ref, g_ref, m_ref, v_ref, d_out, m_out, v_out):
        for o, r in zip((d_out, m_out, v_out), fn(w_ref[...], g_ref[...], m_ref[...], v_ref[...])):
            o[...] = r

    blk = pl.BlockSpec((rows, tc), lambda j: (0, j))
    return pl.pallas_call(body, name=name, grid=(cols // tc,), in_specs=[blk] * 4, out_specs=[blk] * 3,
                          out_shape=[jax.ShapeDtypeStruct((rows, cols), F32)] * 3, compiler_params=_params(("parallel",)))(w, g, m, v)


def _s5_disc(a_re, a_im, log_dt):
    dt = jnp.exp(log_dt)
    mag = jnp.exp(a_re * dt)
    abar_r = mag * jnp.cos(a_im * dt)
    abar_i = mag * jnp.sin(a_im * dt)
    den = a_re * a_re + a_im * a_im
    xr = abar_r - 1.0
    fr = (xr * a_re + abar_i * a_im) / den
    fi = (abar_i * a_re - xr * a_im) / den
    return abar_r, abar_i, fr, fi


def _s5_disc_fwd(name, a_re, a_im, log_dt):
    G, P = a_re.shape

    def body(ar, ai, ld, o0, o1, o2, o3):
        for o, v in zip((o0, o1, o2, o3), _s5_disc(ar[...], ai[...], ld[...])):
            o[...] = v

    return pl.pallas_call(body, name=name, out_shape=[jax.ShapeDtypeStruct((G, P), F32)] * 4)(a_re, a_im, log_dt)


def _s5_disc_bwd(name, a_re, a_im, log_dt, cts):
    G, P = a_re.shape

    def body(ar, ai, ld, c0, c1, c2, c3, d0, d1, d2):
        _, vjp = jax.vjp(_s5_disc, ar[...], ai[...], ld[...])
        g0, g1, g2 = vjp((c0[...], c1[...], c2[...], c3[...]))
        d0[...] = g0
        d1[...] = g1
        d2[...] = g2

    return pl.pallas_call(body, name=name, out_shape=[jax.ShapeDtypeStruct((G, P), F32)] * 2 + [jax.ShapeDtypeStruct((G, 1), F32)])(
        a_re, a_im, log_dt, *cts)


def _s5_bbar(name, fr, fi, br, bi):
    return _rows(name, lambda fr, fi, br, bi: (fr * br - fi * bi, fr * bi + fi * br),
                 [(fr, 'r', 1, 0), (fi, 'r', 1, 0), (br, 'r', S5_GROUP, 0), (bi, 'r', S5_GROUP, 0)],
                 [('r', S5_GROUP, F32)] * 2, 2048)


def _s5_bbar_bwd(name, fr, fi, br, bi, dr, di):
    def fn(fr, fi, br, bi, dr, di):
        return (fr * dr + fi * di, fr * di - fi * dr,
                jnp.sum(br * dr + bi * di, axis=1, keepdims=True), jnp.sum(br * di - bi * dr, axis=1, keepdims=True))

    return _rows(name, fn, [(fr, 'r', 1, 0), (fi, 'r', 1, 0)] + [(x, 'r', S5_GROUP, 0) for x in (br, bi, dr, di)],
                 [('r', S5_GROUP, F32)] * 2 + [('r', 1, F32)] * 2, 2048)


def _scan_mults(m_ref, ar, ai, reverse):
    L = ar.shape[1]
    row = lax.broadcasted_iota(jnp.int32, (SUB, L), 0)
    if reverse:
        row = (SUB - 1) - row
    ar = jnp.broadcast_to(ar, (SUB, L))
    ai = jnp.broadcast_to(ai, (SUB, L))
    a2r, a2i = ar * ar - ai * ai, 2.0 * ar * ai
    a4r, a4i = a2r * a2r - a2i * a2i, 2.0 * a2r * a2i
    zero = jnp.zeros((SUB, L), F32)
    for s, (pr, pi, d) in enumerate(((ar, ai, 1), (a2r, a2i, 2), (a4r, a4i, 4))):
        m_ref[2 * s] = jnp.where(row >= d, pr, zero)
        m_ref[2 * s + 1] = jnp.where(row >= d, pi, zero)
    pr, pi = ar, ai
    for bit, (qr, qi) in ((1, (ar, ai)), (2, (a2r, a2i)), (4, (a4r, a4i))):
        on = (row & bit) != 0
        nr, ni = pr * qr - pi * qi, pr * qi + pi * qr
        pr, pi = jnp.where(on, nr, pr), jnp.where(on, ni, pi)
    m_ref[6] = pr
    m_ref[7] = pi


def _scan8(xr, xi, m_ref, cr, ci, reverse):
    for s, d in enumerate((1, 2, 4)):
        sh = (SUB - d) if reverse else d
        sr, si = pltpu.roll(xr, sh, 0), pltpu.roll(xi, sh, 0)
        mr, mi = m_ref[2 * s], m_ref[2 * s + 1]
        xr, xi = xr + mr * sr - mi * si, xi + mr * si + mi * sr
    pr, pi = m_ref[6], m_ref[7]
    return xr + pr * cr - pi * ci, xi + pr * ci + pi * cr


def _blockdiag_fill(bd_ref, c_ref, C, L):
    P = S5_STATE
    bd_ref[...] = jnp.zeros_like(bd_ref)
    for g in range(L // P):
        for half in (0, L):
            bd_ref[g * C:(g + 1) * C, half + g * P:half + (g + 1) * P] = c_ref[:, half + g * P:half + (g + 1) * P]


def _blockdiag_take(out_ref, dense_ref, C, L):
    P = S5_STATE
    for g in range(L // P):
        for half in (0, L):
            out_ref[:, half + g * P:half + (g + 1) * P] = dense_ref[g * C:(g + 1) * C, half + g * P:half + (g + 1) * P]


def _s5_fwd(name, proj, bbd, cbd, abar_r, abar_i, dskip, E):
    T = proj.shape[0]
    NC, C, L2 = bbd.shape
    L = L2 // 2
    CH = GROUPS_PER_CHUNK * C
    tT = _t(256, T)
    nt = (((1,), (1,)), ((), ()))

    def body(u_ref, bc_ref, cc_ref, ar_ref, ai_ref, d_ref, y_ref, g_ref, h_ref, bu, carry, mult, b_bd, c_bd):
        tb = pl.program_id(1)

        @pl.when(tb == 0)
        def _():
            carry[...] = jnp.zeros_like(carry)
            _blockdiag_fill(b_bd, bc_ref, C, L)
            _blockdiag_fill(c_bd, cc_ref, C, L)

        u = u_ref[...]
        bu[...] = jnp.dot(u.astype(BF16), b_bd[...], preferred_element_type=F32)
        _scan_mults(mult, ar_ref[...], ai_ref[...], False)

        def step(jb, c):
            cr, ci = c
            r0 = pl.multiple_of(jb * SUB, SUB)
            hr, hi = _scan8(bu[pl.ds(r0, SUB), 0:L], bu[pl.ds(r0, SUB), L:L2], mult, cr, ci, False)
            h_ref[pl.ds(r0, SUB), 0:L] = hr
            h_ref[pl.ds(r0, SUB), L:L2] = hi
            return (jnp.broadcast_to(hr[SUB - 1:SUB, :], (SUB, L)), jnp.broadcast_to(hi[SUB - 1:SUB, :], (SUB, L)))

        cr, ci = lax.fori_loop(0, tT // SUB, step, (carry[:, 0:L], carry[:, L:L2]))
        carry[:, 0:L] = cr
        carry[:, L:L2] = ci
        y1 = lax.dot_general(h_ref[...].astype(BF16), c_bd[...], nt, preferred_element_type=F32) + d_ref[...] * u
        y_ref[...] = y1
        g_ref[...] = _gelu(y1).astype(BF16)

    return pl.pallas_call(
        body, name=name, grid=(NC, T // tT),
        in_specs=[_bs((tT, CH), lambda c, t: (t, c)), _bs((None, C, L2), lambda c, t: (c, 0, 0)),
                  _bs((None, C, L2), lambda c, t: (c, 0, 0)), _bs((None, 1, L), lambda c, t: (c, 0, 0)),
                  _bs((None, 1, L), lambda c, t: (c, 0, 0)), _bs((1, CH), lambda c, t: (0, c))],
        out_specs=[_bs((tT, CH), lambda c, t: (t, c)), _bs((tT, CH), lambda c, t: (t, c)),
                   _bs((None, tT, L2), lambda c, t: (c, t, 0))],
        out_shape=[jax.ShapeDtypeStruct((T, E), F32), jax.ShapeDtypeStruct((T, E), BF16),
                   jax.ShapeDtypeStruct((NC, T, L2), F32)],
        scratch_shapes=[pltpu.VMEM((tT, L2), F32), pltpu.VMEM((SUB, L2), F32), pltpu.VMEM((8, SUB, L), F32),
                        pltpu.VMEM((CH, L2), BF16), pltpu.VMEM((CH, L2), BF16)],
        compiler_params=_params(("parallel", "arbitrary")),
    )(proj, bbd, cbd, abar_r, abar_i, dskip)


def _s5_bwd(name, dy1, proj, hs, bbd, cbd, abar_r, abar_i, dskip, E):
    T = proj.shape[0]
    NC, C, L2 = bbd.shape
    L = L2 // 2
    CH = GROUPS_PER_CHUNK * C
    tT = _t(256, T)
    nT = T // tT
    tn = (((0,), (0,)), ((), ()))
    nt = (((1,), (1,)), ((), ()))

    def body(dy_ref, u_ref, h_ref, bc_ref, cc_ref, ar_ref, ai_ref, d_ref, du_ref, db_ref, dc_ref, da_ref, dd_ref,
             gb, carry, mult, b_bd, c_bd, db_acc, dc_acc):
        tb = pl.program_id(1)

        @pl.when(tb == 0)
        def _():
            carry[...] = jnp.zeros_like(carry)
            db_acc[...] = jnp.zeros_like(db_acc)
            dc_acc[...] = jnp.zeros_like(dc_acc)
            da_ref[...] = jnp.zeros_like(da_ref)
            dd_ref[...] = jnp.zeros_like(dd_ref)
            _blockdiag_fill(b_bd, bc_ref, C, L)
            _blockdiag_fill(c_bd, cc_ref, C, L)

        dy = dy_ref[...]
        u = u_ref[...]
        dy16 = dy.astype(BF16)
        dc_acc[...] += lax.dot_general(dy16, h_ref[...].astype(BF16), tn, preferred_element_type=F32)
        gb[...] = jnp.dot(dy16, c_bd[...], preferred_element_type=F32)
        _scan_mults(mult, ar_ref[...], -ai_ref[...], True)
        row = lax.broadcasted_iota(jnp.int32, (SUB, L), 0)
        nblk = tT // SUB

        def step(jj, c):
            cr, ci, sr, si = c
            r0 = pl.multiple_of((nblk - 1 - jj) * SUB, SUB)
            gr, gi = _scan8(gb[pl.ds(r0, SUB), 0:L], gb[pl.ds(r0, SUB), L:L2], mult, cr, ci, True)
            gb[pl.ds(r0, SUB), 0:L] = gr
            gb[pl.ds(r0, SUB), L:L2] = gi
            nr = jnp.where(row == SUB - 1, cr, pltpu.roll(gr, SUB - 1, 0))
            ni = jnp.where(row == SUB - 1, ci, pltpu.roll(gi, SUB - 1, 0))
            hr, hi = h_ref[pl.ds(r0, SUB), 0:L], h_ref[pl.ds(r0, SUB), L:L2]
            sr = sr + nr * hr + ni * hi
            si = si + ni * hr - nr * hi
            return (jnp.broadcast_to(gr[0:1, :], (SUB, L)), jnp.broadcast_to(gi[0:1, :], (SUB, L)), sr, si)

        z = jnp.zeros((SUB, L), F32)
        cr, ci, sr, si = lax.fori_loop(0, nblk, step, (carry[:, 0:L], carry[:, L:L2], z, z))
        carry[:, 0:L] = cr
        carry[:, L:L2] = ci
        da_ref[:, 0:L] += sr
        da_ref[:, L:L2] += si
        g16 = gb[...].astype(BF16)
        du = lax.dot_general(g16, b_bd[...], nt, preferred_element_type=F32) + d_ref[...] * dy
        du_ref[...] = du.astype(BF16)
        db_acc[...] += lax.dot_general(u.astype(BF16), g16, tn, preferred_element_type=F32)
        dd_ref[...] += _colsum(dy * u)

        @pl.when(tb == nT - 1)
        def _():
            _blockdiag_take(db_ref, db_acc, C, L)
            _blockdiag_take(dc_ref, dc_acc, C, L)

    rev = lambda c, t: (nT - 1 - t, c)
    return pl.pallas_call(
        body, name=name, grid=(NC, nT),
        in_specs=[_bs((tT, CH), rev), _bs((tT, CH), rev), _bs((None, tT, L2), lambda c, t: (c, nT - 1 - t, 0)),
                  _bs((None, C, L2), lambda c, t: (c, 0, 0)), _bs((None, C, L2), lambda c, t: (c, 0, 0)),
                  _bs((None, 1, L), lambda c, t: (c, 0, 0)), _bs((None, 1, L), lambda c, t: (c, 0, 0)),
                  _bs((1, CH), lambda c, t: (0, c))],
        out_specs=[_bs((tT, CH), rev), _bs((None, C, L2), lambda c, t: (c, 0, 0)), _bs((None, C, L2), lambda c, t: (c, 0, 0)),
                   _bs((None, SUB, L2), lambda c, t: (c, 0, 0)), _bs((None, 1, CH), lambda c, t: (c, 0, 0))],
        out_shape=[jax.ShapeDtypeStruct((T, E), BF16), jax.ShapeDtypeStruct((NC, C, L2), F32),
                   jax.ShapeDtypeStruct((NC, C, L2), F32), jax.ShapeDtypeStruct((NC, SUB, L2), F32),
                   jax.ShapeDtypeStruct((NC, 1, CH), F32)],
        scratch_shapes=[pltpu.VMEM((tT, L2), F32), pltpu.VMEM((SUB, L2), F32), pltpu.VMEM((8, SUB, L), F32),
                        pltpu.VMEM((CH, L2), BF16), pltpu.VMEM((CH, L2), BF16), pltpu.VMEM((CH, L2), F32), pltpu.VMEM((CH, L2), F32)],
        compiler_params=_params(("parallel", "arbitrary")),
    )(dy1, proj, hs, bbd, cbd, abar_r, abar_i, dskip)


def _compact(v, NC):
    G, P, C = v.shape
    return jnp.transpose(v.reshape(NC, G // NC, P, C), (0, 3, 1, 2)).reshape(NC, C, (G // NC) * P)


def _uncompact(d, G):
    NC, C, L = d.shape
    gpc = G // NC
    return jnp.transpose(d.reshape(NC, C, gpc, L // gpc), (0, 2, 3, 1)).reshape(G, L // gpc, C)


def _cum_rows(name, x, bias, reverse, log_sig):
    T, L = x.shape

    def body(x_ref, b_ref, o_ref):
        row = lax.broadcasted_iota(jnp.int32, (SUB, L), 0)
        if reverse:
            row = (SUB - 1) - row
        nblk = T // SUB

        def step(jj, c):
            r0 = pl.multiple_of(((nblk - 1 - jj) if reverse else jj) * SUB, SUB)
            v = x_ref[pl.ds(r0, SUB), :] + b_ref[...]
            if log_sig:
                v = _log_sigmoid(v)
            for d in (1, 2, 4):
                v = v + jnp.where(row >= d, pltpu.roll(v, (SUB - d) if reverse else d, 0), 0.0)
            v = v + c
            o_ref[pl.ds(r0, SUB), :] = v
            e = 0 if reverse else SUB - 1
            return jnp.broadcast_to(v[e:e + 1, :], (SUB, L))

        lax.fori_loop(0, nblk, step, jnp.zeros((SUB, L), F32))

    return pl.pallas_call(body, name=name, out_shape=jax.ShapeDtypeStruct((T, L), F32),
                          compiler_params=pltpu.CompilerParams(vmem_limit_bytes=VMEM_LIMIT))(x, bias)


def _qk_norm(name, proj, wq, wk, H):
    T = proj.shape[0]
    Dh = FOX_HEAD_DIM
    tT = _t(512, T)

    def body(q_ref, k_ref, wq_ref, wk_ref, qn_ref, kn_ref):
        q, k = q_ref[...], k_ref[...]
        qn_ref[...] = ((q * _rms(q)) * wq_ref[...]).astype(BF16)
        kn_ref[...] = ((k * _rms(k)) * wk_ref[...]).astype(BF16)

    blk = lambda off: _bs((tT, Dh), lambda t, h: (t, h + off))
    return pl.pallas_call(
        body, name=name, grid=(T // tT, H),
        in_specs=[blk(0), blk(H), _bs((1, Dh), lambda t, h: (0, 0)), _bs((1, Dh), lambda t, h: (0, 0))],
        out_specs=[blk(0), blk(0)], out_shape=[jax.ShapeDtypeStruct((T, H * Dh), BF16)] * 2,
        compiler_params=_params(("parallel", "parallel")))(proj, proj, wq, wk)


def _qk_norm_bwd(name, proj, wq, wk, dqn, dkn, H):
    T = proj.shape[0]
    Dh = FOX_HEAD_DIM
    tT = _t(512, T)

    def body(q_ref, k_ref, wq_ref, wk_ref, dqn_ref, dkn_ref, dq_ref, dk_ref, dwq_ref, dwk_ref):
        @pl.when((pl.program_id(0) == 0) & (pl.program_id(1) == 0))
        def _():
            dwq_ref[...] = jnp.zeros_like(dwq_ref)
            dwk_ref[...] = jnp.zeros_like(dwk_ref)

        dq, tq = _rms_bwd(q_ref[...], wq_ref[...], dqn_ref[...])
        dk, tk = _rms_bwd(k_ref[...], wk_ref[...], dkn_ref[...])
        dq_ref[...] = dq.astype(BF16)
        dk_ref[...] = dk.astype(BF16)
        dwq_ref[...] += _colsum(tq)
        dwk_ref[...] += _colsum(tk)

    blk = lambda off: _bs((tT, Dh), lambda t, h: (t, h + off))
    one = _bs((1, Dh), lambda t, h: (0, 0))
    return pl.pallas_call(
        body, name=name, grid=(T // tT, H),
        in_specs=[blk(0), blk(H), one, one, blk(0), blk(0)],
        out_specs=[blk(0), blk(0), one, one],
        out_shape=[jax.ShapeDtypeStruct((T, H * Dh), BF16)] * 2 + [jax.ShapeDtypeStruct((1, Dh), F32)] * 2,
        compiler_params=_params(("arbitrary", "arbitrary")))(proj, proj, wq, wk, dqn, dkn)


def _attn_fwd(name, qn, kn, proj, cum_q, cum_k, H):
    T = qn.shape[0]
    Dh = FOX_HEAD_DIM
    tq = cum_k.shape[3]
    nq = T // tq
    scale = Dh ** -0.5
    nt = (((1,), (1,)), ((), ()))

    sq = _t(ATTN_SUB, tq)
    rep = tq // LANES

    def body(q_ref, k_ref, v_ref, cq_ref, ck_ref, o_ref, lse_ref, m_sc, l_sc, acc_sc):
        i = pl.program_id(1)
        m_sc[...] = jnp.full_like(m_sc, NEG)
        l_sc[...] = jnp.zeros_like(l_sc)
        acc_sc[...] = jnp.zeros_like(acc_sc)
        kloc = lax.broadcasted_iota(jnp.int32, (sq, tq), 1)
        qloc = lax.broadcasted_iota(jnp.int32, (sq, tq), 0)

        def chunk(kc, masked):
            ks = pl.multiple_of(kc * tq, tq)
            k = k_ref[pl.ds(ks, tq), :]
            v16 = v_ref[pl.ds(ks, tq), :].astype(BF16)
            ck = ck_ref[kc]
            for r in range(tq // sq):
                rows = pl.ds(r * sq, sq)
                s = lax.dot_general(q_ref[rows, :], k, nt, preferred_element_type=F32) * scale + (jnp.tile(cq_ref[rows, :], (1, rep)) - ck)
                if masked:
                    s = jnp.where(kloc <= qloc + r * sq, s, NEG)
                m_old = m_sc[rows, :]
                m_new = jnp.maximum(m_old, jnp.max(s, axis=1, keepdims=True))
                alpha = jnp.exp(m_old - m_new)
                p = jnp.exp(s - jnp.tile(m_new, (1, rep)))
                l_sc[rows, :] = alpha * l_sc[rows, :] + jnp.sum(p, axis=1, keepdims=True)
                acc_sc[rows, :] = alpha * acc_sc[rows, :] + jnp.dot(p.astype(BF16), v16, preferred_element_type=F32)
                m_sc[rows, :] = m_new

        def below(kc, c):
            chunk(kc, False)
            return c

        lax.fori_loop(0, i, below, 0)
        chunk(i, True)
        o_ref[...] = acc_sc[...] / l_sc[...]
        lse_ref[...] = m_sc[...] + jnp.log(l_sc[...])

    return pl.pallas_call(
        body, name=name, grid=(H, nq),
        in_specs=[_bs((tq, Dh), lambda h, i: (i, h)), _bs((T, Dh), lambda h, i: (0, h)), _bs((T, Dh), lambda h, i: (0, 2 * H + h)),
                  _bs((None, tq, LANES), lambda h, i: (h, i, 0)), _bs((None, nq, 1, tq), lambda h, i: (h, 0, 0, 0))],
        out_specs=[_bs((tq, Dh), lambda h, i: (i, h)), _bs((None, tq, LANES), lambda h, i: (h, i, 0))],
        out_shape=[jax.ShapeDtypeStruct((T, H * Dh), F32), jax.ShapeDtypeStruct((H, T, LANES), F32)],
        scratch_shapes=[pltpu.VMEM((tq, LANES), F32), pltpu.VMEM((tq, LANES), F32), pltpu.VMEM((tq, Dh), F32)],
        compiler_params=_params(("parallel", "parallel")))(qn, kn, proj, cum_q, cum_k)


def _attn_bwd(name, qn, kn, proj, do, o, lse, cum_q, cum_k, H):
    T = qn.shape[0]
    Dh = FOX_HEAD_DIM
    tq = cum_k.shape[3]
    nq = T // tq
    scale = Dh ** -0.5
    nt = (((1,), (1,)), ((), ()))
    tn = (((0,), (0,)), ((), ()))
    assert H <= LANES

    sq = _t(ATTN_SUB, tq)
    rep = tq // LANES

    def body(q_ref, k_ref, v_ref, do_ref, o_ref, lse_ref, cq_ref, ck_ref, dq_ref, dk_ref, dv_ref, dcq_ref, dck_ref,
             delta, cql, dk_sc, dv_sc, dck_sc):
        h, j = pl.program_id(0), pl.program_id(1)

        @pl.when((h == 0) & (j == 0))
        def _():
            dcq_ref[...] = jnp.zeros_like(dcq_ref)

        @pl.when(j == 0)
        def _():
            dq_ref[...] = jnp.zeros_like(dq_ref)
            delta[...] = jnp.broadcast_to(jnp.sum(do_ref[...] * o_ref[...], axis=1, keepdims=True), delta.shape)
            cql[...] = cq_ref[...] - lse_ref[...]

        head_lane = lax.broadcasted_iota(jnp.int32, (sq, LANES), 1) == h

        dk_sc[...] = jnp.zeros_like(dk_sc)
        dv_sc[...] = jnp.zeros_like(dv_sc)
        dck_sc[...] = jnp.zeros_like(dck_sc)
        k = k_ref[...]
        v16 = v_ref[...].astype(BF16)
        ck = ck_ref[...]
        kloc = lax.broadcasted_iota(jnp.int32, (sq, tq), 1)
        qloc = lax.broadcasted_iota(jnp.int32, (sq, tq), 0)

        def qblk(i, masked):
            for r in range(tq // sq):
                rows = pl.ds(pl.multiple_of(i * tq + r * sq, sq), sq)
                q = q_ref[rows, :]
                do16 = do_ref[rows, :].astype(BF16)
                e = lax.dot_general(q, k, nt, preferred_element_type=F32) * scale + (jnp.tile(cql[rows, :], (1, rep)) - ck)
                p = jnp.exp(e)
                if masked:
                    p = jnp.where(kloc <= qloc + r * sq, p, 0.0)
                dv_sc[...] += lax.dot_general(p.astype(BF16), do16, tn, preferred_element_type=F32)
                dp = lax.dot_general(do16, v16, nt, preferred_element_type=F32)
                ds = p * (dp - jnp.tile(delta[rows, :], (1, rep)))
                ds16 = ds.astype(BF16)
                dk_sc[...] += lax.dot_general(ds16, q, tn, preferred_element_type=F32)
                dq_ref[rows, :] += jnp.dot(ds16, k, preferred_element_type=F32) * scale
                dcq_ref[rows, :] += jnp.where(head_lane, jnp.sum(ds, axis=1, keepdims=True), 0.0)
                dck_sc[...] += jnp.sum(ds, axis=0, keepdims=True)

        def above(i, c):
            qblk(i, False)
            return c

        qblk(j, True)
        lax.fori_loop(j + 1, nq, above, 0)
        dk_ref[...] = dk_sc[...] * scale
        dv_ref[...] = dv_sc[...].astype(BF16)
        dck_ref[...] = -dck_sc[...]

    whole = lambda off: _bs((T, Dh), lambda h, j: (0, h + off))
    blk = lambda off: _bs((tq, Dh), lambda h, j: (j, h + off))
    return pl.pallas_call(
        body, name=name, grid=(H, nq),
        in_specs=[whole(0), blk(0), blk(2 * H), whole(0), whole(0), _bs((None, T, LANES), lambda h, j: (h, 0, 0)),
                  _bs((None, T, LANES), lambda h, j: (h, 0, 0)), _bs((None, None, 1, tq), lambda h, j: (h, j, 0, 0))],
        out_specs=[whole(0), blk(0), blk(0), _bs((T, LANES), lambda h, j: (0, 0)),
                   _bs((None, None, 1, tq), lambda h, j: (h, j, 0, 0))],
        out_shape=[jax.ShapeDtypeStruct((T, H * Dh), F32), jax.ShapeDtypeStruct((T, H * Dh), F32), jax.ShapeDtypeStruct((T, H * Dh), BF16),
                   jax.ShapeDtypeStruct((T, LANES), F32), jax.ShapeDtypeStruct((H, nq, 1, tq), F32)],
        scratch_shapes=[pltpu.VMEM((T, LANES), F32), pltpu.VMEM((T, LANES), F32), pltpu.VMEM((tq, Dh), F32), pltpu.VMEM((tq, Dh), F32),
                        pltpu.VMEM((1, tq), F32)],
        compiler_params=_params(("arbitrary", "arbitrary")))(qn, kn, proj, do, o, lse, cum_q, cum_k)


def _pool_fwd(name, proj, E):
    T = proj.shape[0]
    PG = len(POOL_WINDOWS)
    PD = E // PG
    tT = _t(256, T)
    hb = tT // POOL_HALO

    def body(u_ref, halo_ref, o_ref, buf):
        g, tb = pl.program_id(0), pl.program_id(1)
        u = u_ref[...]
        buf[pl.ds(POOL_HALO, tT), :] = u
        buf[pl.ds(0, POOL_HALO), :] = jnp.where(tb == 0, 0.0, halo_ref[...])
        t = tb * tT + lax.broadcasted_iota(jnp.int32, (tT, 1), 0)
        for gi, w in enumerate(POOL_WINDOWS):
            @pl.when(g == gi)
            def _():
                acc = u
                for d in range(1, w):
                    acc = acc + buf[pl.ds(POOL_HALO - d, tT), :]
                cnt = jnp.minimum(t + 1, w).astype(F32)
                o_ref[...] = (acc / cnt - u).astype(BF16)

    return pl.pallas_call(
        body, name=name, grid=(PG, T // tT),
        in_specs=[_bs((tT, PD), lambda g, t: (t, g)), _bs((POOL_HALO, PD), lambda g, t: (jnp.maximum(t * hb - 1, 0), g))],
        out_specs=_bs((tT, PD), lambda g, t: (t, g)), out_shape=jax.ShapeDtypeStruct((T, E), BF16),
        scratch_shapes=[pltpu.VMEM((tT + POOL_HALO, PD), F32)],
        compiler_params=_params(("parallel", "parallel")))(proj, proj)


def _pool_bwd(name, dpm, E):
    T = dpm.shape[0]
    PG = len(POOL_WINDOWS)
    PD = E // PG
    tT = _t(256, T)
    hb = tT // POOL_HALO
    nT = T // tT

    def body(d_ref, halo_ref, o_ref, buf):
        g, tb = pl.program_id(0), pl.program_id(1)
        d = d_ref[...]
        t = tb * tT + lax.broadcasted_iota(jnp.int32, (tT, 1), 0)
        th = (tb + 1) * tT + lax.broadcasted_iota(jnp.int32, (POOL_HALO, 1), 0)
        for gi, w in enumerate(POOL_WINDOWS):
            @pl.when(g == gi)
            def _():
                dn = d / jnp.minimum(t + 1, w).astype(F32)
                buf[pl.ds(0, tT), :] = dn
                buf[pl.ds(tT, POOL_HALO), :] = jnp.where(tb == nT - 1, 0.0, halo_ref[...] / jnp.minimum(th + 1, w).astype(F32))
                acc = dn
                for s in range(1, w):
                    acc = acc + buf[pl.ds(s, tT), :]
                o_ref[...] = (acc - d).astype(BF16)

    return pl.pallas_call(
        body, name=name, grid=(PG, nT),
        in_specs=[_bs((tT, PD), lambda g, t: (t, g)), _bs((POOL_HALO, PD), lambda g, t: (jnp.minimum((t + 1) * hb, T // POOL_HALO - 1), g))],
        out_specs=_bs((tT, PD), lambda g, t: (t, g)), out_shape=jax.ShapeDtypeStruct((T, E), BF16),
        scratch_shapes=[pltpu.VMEM((tT + POOL_HALO, PD), F32)],
        compiler_params=_params(("parallel", "parallel")))(dpm, dpm)


def _coords():
    x, y, c = lax.axis_index("x"), lax.axis_index("y"), lax.axis_index("c")
    chips = [(1 - x, y), (x, 1 - y), (1 - x, 1 - y)]
    return x, y, c, 2 * x + y, (x, y, 1 - c), chips


def _chip_allgather(name, bufs):
    n = len(bufs)

    def body(*refs):
        outs = refs[n:2 * n]
        send, recv, fsend, frecv = refs[2 * n:]
        x, y, c, p, sib, chips = _coords()

        def direct(t, j, chip):
            return pltpu.make_async_remote_copy(src_ref=outs[t].at[p, c], dst_ref=outs[t].at[p, c], send_sem=send.at[t, j],
                                                recv_sem=recv.at[t, j], device_id=(*chip, c), device_id_type=MESH)

        def landed(t, j, chip):
            blk = outs[t].at[2 * chip[0] + chip[1], c]
            return pltpu.make_async_remote_copy(src_ref=blk, dst_ref=blk, send_sem=send.at[t, j],
                                                recv_sem=recv.at[t, j], device_id=(*chip, c), device_id_type=MESH)

        def passed(t, j, chip, half):
            blk = outs[t].at[2 * chip[0] + chip[1], half]
            return pltpu.make_async_remote_copy(src_ref=blk, dst_ref=blk, send_sem=fsend.at[t, j], recv_sem=frecv.at[t, j],
                                                device_id=sib, device_id_type=MESH)

        first = [direct(t, j, chip) for t in range(n) for j, chip in enumerate(chips)]
        for cp in first:
            cp.start()
        fwd = []
        for j, chip in enumerate(chips):
            for t in range(n):
                landed(t, j, chip).wait_recv()
                f = passed(t, j, chip, c)
                f.start()
                fwd.append(f)
        for j, chip in enumerate(chips):
            for t in range(n):
                passed(t, j, chip, 1 - c).wait_recv()
        for cp in first + fwd:
            cp.wait_send()

    return pl.pallas_call(
        body, name=name, in_specs=[ANY] * n, out_specs=[ANY] * n,
        out_shape=[jax.ShapeDtypeStruct(a.shape, a.dtype) for a in bufs],
        input_output_aliases={t: t for t in range(n)},
        scratch_shapes=[pltpu.SemaphoreType.DMA((n, 3))] * 4,
    )(*bufs)


SEM = pl.BlockSpec(memory_space=pltpu.SEMAPHORE)
TOKEN = jax.ShapeDtypeStruct((SUB, LANES), F32)


def _split_params():
    return pltpu.CompilerParams(has_side_effects=pltpu.SideEffectType.DATAFLOW_SIDE_EFFECTING)


def _struct(a):
    return jax.ShapeDtypeStruct(a.shape, a.dtype)


def _gather_start(name, bufs, deps):
    n, nd = len(bufs), len(deps)

    def body(*refs):
        outs = refs[n + nd:2 * n + nd]
        send, recv, token = refs[2 * n + nd:]
        x, y, c, p, sib, chips = _coords()
        for t in range(n):
            for j, chip in enumerate(chips):
                pltpu.make_async_remote_copy(src_ref=outs[t].at[p, c], dst_ref=outs[t].at[p, c], send_sem=send.at[3 * t + j],
                                             recv_sem=recv.at[3 * t + j], device_id=(*chip, c), device_id_type=MESH).start()
        token[...] = jnp.zeros_like(token)

    res = pl.pallas_call(
        body, name=name, in_specs=[ANY] * (n + nd), out_specs=[ANY] * n + [SEM, SEM, pl.BlockSpec(memory_space=pltpu.VMEM)],
        out_shape=[_struct(a) for a in bufs] + [pltpu.SemaphoreType.DMA((3 * n,)), pltpu.SemaphoreType.DMA((3 * n,)), TOKEN],
        input_output_aliases={t: t for t in range(n)}, compiler_params=_split_params(),
    )(*bufs, *deps)
    return list(res[:n]), res[n], res[n + 1], res[n + 2]


def _gather_wait(name, bufs, send, recv, after):
    n = len(bufs)

    def body(*refs):
        send_r, recv_r = refs[n], refs[n + 1]
        outs = refs[n + 3:2 * n + 3]
        x, y, c, p, sib, chips = _coords()
        for t in range(n):
            for j, chip in enumerate(chips):
                cp = pltpu.make_async_remote_copy(src_ref=outs[t].at[p, c], dst_ref=outs[t].at[2 * chip[0] + chip[1], c], send_sem=send_r.at[3 * t + j],
                                                  recv_sem=recv_r.at[3 * t + j], device_id=(*chip, c), device_id_type=MESH)
                cp.wait_send()
                cp.wait_recv()

    return list(pl.pallas_call(
        body, name=name, in_specs=[ANY] * n + [SEM, SEM, ANY], out_specs=[ANY] * n, out_shape=[_struct(a) for a in bufs],
        input_output_aliases={t: t for t in range(n)}, compiler_params=_split_params(),
    )(*bufs, send, recv, after))


def _gather_forward(name, bufs):
    n = len(bufs)

    def body(*refs):
        outs = refs[n:2 * n]
        fsend, frecv = refs[2 * n:]
        x, y, c, p, sib, chips = _coords()

        def passed(t, j, chip, half):
            blk = outs[t].at[2 * chip[0] + chip[1], half]
            return pltpu.make_async_remote_copy(src_ref=blk, dst_ref=blk, send_sem=fsend.at[t, j], recv_sem=frecv.at[t, j],
                                                device_id=sib, device_id_type=MESH)

        fwd = [passed(t, j, chip, c) for t in range(n) for j, chip in enumerate(chips)]
        for cp in fwd:
            cp.start()
        for t in range(n):
            for j, chip in enumerate(chips):
                passed(t, j, chip, 1 - c).wait_recv()
        for cp in fwd:
            cp.wait_send()

    return list(pl.pallas_call(
        body, name=name, in_specs=[ANY] * n, out_specs=[ANY] * n, out_shape=[_struct(a) for a in bufs],
        input_output_aliases={t: t for t in range(n)}, scratch_shapes=[pltpu.SemaphoreType.DMA((n, 3))] * 2,
    )(*bufs))


def _chip_exchange_start(name, sums):
    n = len(sums)
    lands = [lax.empty((3,) + a.shape[1:], a.dtype) for a in sums]

    def body(*refs):
        src, dst = refs[2 * n:3 * n], refs[3 * n:4 * n]
        send, recv, token = refs[4 * n:]
        x, y, c, p, sib, chips = _coords()
        for t in range(n):
            for j, chip in enumerate(chips):
                pltpu.make_async_remote_copy(src_ref=src[t].at[2 * chip[0] + chip[1]], dst_ref=dst[t].at[j], send_sem=send.at[3 * t + j],
                                             recv_sem=recv.at[3 * t + j], device_id=(*chip, c), device_id_type=MESH).start()
        token[...] = jnp.zeros_like(token)

    res = pl.pallas_call(
        body, name=name, in_specs=[ANY] * (2 * n), out_specs=[ANY] * (2 * n) + [SEM, SEM, pl.BlockSpec(memory_space=pltpu.VMEM)],
        out_shape=[_struct(a) for a in sums + lands] + [pltpu.SemaphoreType.DMA((3 * n,)), pltpu.SemaphoreType.DMA((3 * n,)), TOKEN],
        input_output_aliases={t: t for t in range(2 * n)}, compiler_params=_split_params(),
    )(*sums, *lands)
    return list(res[:n]), list(res[n:2 * n]), res[2 * n], res[2 * n + 1], res[2 * n + 2]


def _chip_exchange_wait(name, sums, lands, send, recv, after):
    n = len(sums)

    def body(*refs):
        send_r, recv_r = refs[2 * n], refs[2 * n + 1]
        src, dst = refs[2 * n + 3:3 * n + 3], refs[3 * n + 3:4 * n + 3]
        x, y, c, p, sib, chips = _coords()
        for t in range(n):
            for j, chip in enumerate(chips):
                cp = pltpu.make_async_remote_copy(src_ref=src[t].at[2 * chip[0] + chip[1]], dst_ref=dst[t].at[j], send_sem=send_r.at[3 * t + j],
                                                  recv_sem=recv_r.at[3 * t + j], device_id=(*chip, c), device_id_type=MESH)
                cp.wait_send()
                cp.wait_recv()

    res = pl.pallas_call(
        body, name=name, in_specs=[ANY] * (2 * n) + [SEM, SEM, ANY], out_specs=[ANY] * (2 * n),
        out_shape=[_struct(a) for a in sums + lands], input_output_aliases={t: t for t in range(2 * n)},
        compiler_params=_split_params(),
    )(*sums, *lands, send, recv, after)
    return list(res[:n]), list(res[n:])


def _pair_exchange(name, parts):
    n = len(parts)

    def body(*refs):
        ins, outs = refs[:n], refs[n:2 * n]
        send, recv = refs[2 * n:]
        x, y, c, p, sib, chips = _coords()
        cps = [pltpu.make_async_remote_copy(src_ref=ins[t].at[1 - c], dst_ref=outs[t], send_sem=send.at[t], recv_sem=recv.at[t],
                                            device_id=sib, device_id_type=MESH) for t in range(n)]
        for cp in cps:
            cp.start()
        for cp in cps:
            cp.wait()

    return pl.pallas_call(
        body, name=name, in_specs=[ANY] * n, out_specs=[ANY] * n,
        out_shape=[jax.ShapeDtypeStruct(a.shape[1:], a.dtype) for a in parts],
        scratch_shapes=[pltpu.SemaphoreType.DMA((n,))] * 2,
    )(*parts)


def _chip_exchange(name, sums):
    n = len(sums)

    def body(*refs):
        ins, outs = refs[:n], refs[n:2 * n]
        send, recv = refs[2 * n:]
        x, y, c, p, sib, chips = _coords()
        cps = [pltpu.make_async_remote_copy(src_ref=ins[t].at[2 * chip[0] + chip[1]], dst_ref=outs[t].at[j], send_sem=send.at[t, j],
                                            recv_sem=recv.at[t, j], device_id=(*chip, c), device_id_type=MESH)
               for t in range(n) for j, chip in enumerate(chips)]
        for cp in cps:
            cp.start()
        for cp in cps:
            cp.wait()

    return pl.pallas_call(
        body, name=name, in_specs=[ANY] * n, out_specs=[ANY] * n,
        out_shape=[jax.ShapeDtypeStruct((3,) + a.shape[1:], a.dtype) for a in sums],
        scratch_shapes=[pltpu.SemaphoreType.DMA((n, 3))] * 2,
    )(*sums)


def _pair_share(name, bufs, items):
    n = len(items)
    nb = len(bufs)

    def body(*refs):
        outs = refs[nb:2 * nb]
        send, recv = refs[2 * nb:]
        x, y, c, p, sib, chips = _coords()

        def blk(t, half):
            o, lead = items[t]
            return outs[o].at[p if lead == 'chip' else lead, half]

        def swap(t, half):
            return pltpu.make_async_remote_copy(src_ref=blk(t, half), dst_ref=blk(t, half), send_sem=send.at[t], recv_sem=recv.at[t],
                                                device_id=sib, device_id_type=MESH)

        cps = [swap(t, c) for t in range(n)]
        for cp in cps:
            cp.start()
        for t in range(n):
            swap(t, 1 - c).wait_recv()
        for cp in cps:
            cp.wait_send()

    return pl.pallas_call(
        body, name=name, in_specs=[ANY] * nb, out_specs=[ANY] * nb,
        out_shape=[jax.ShapeDtypeStruct(b.shape, b.dtype) for b in bufs],
        input_output_aliases={t: t for t in range(nb)},
        scratch_shapes=[pltpu.SemaphoreType.DMA((n,))] * 2,
    )(*bufs)


def _flat2(a, lead):
    return a.reshape(a.shape[:lead] + (-1, a.shape[-1]))


def _reduce_begin(tag, parts):
    c = lax.axis_index("c").astype(jnp.int32)
    got = _pair_exchange(f"rs_pair_exchange_{tag}", parts)
    sums = []
    for t, (mine, theirs) in enumerate(zip(parts, got)):
        m3, t2 = _flat2(mine, 1), theirs.reshape(-1, theirs.shape[-1])
        m3 = m3.reshape(2, -1, m3.shape[-1])
        cols = t2.shape[1]
        s = _rows(f"rs_pair_sum_{tag}_{t}", lambda a, b: (a.astype(F32) + b.astype(F32),),
                  [(m3, 's', cols, 0), (t2, 'r', cols, 0)], [('r', cols, BF16)], 512, pre=c.reshape(1))[0]
        sums.append(s.reshape(theirs.shape))
    sums, lands, send, recv, token = _chip_exchange_start(f"rs_chip_start_{tag}", sums)
    return (sums, lands, send, recv), token


def _reduce_end(tag, state, after, dests, bufs, buf_shapes):
    c = lax.axis_index("c").astype(jnp.int32)
    p = (2 * lax.axis_index("x") + lax.axis_index("y")).astype(jnp.int32)
    sums, lands = _chip_exchange_wait(f"rs_chip_wait_{tag}", *state, after)
    for t, (mine, theirs) in enumerate(zip(sums, lands)):
        o, lead = dests[t]
        shape = buf_shapes[o]
        rows, cols = shape[2], shape[3]
        m3, t3 = mine.reshape(N_CHIPS, rows, cols), theirs.reshape(3, rows, cols)
        pre = jnp.stack([p, jnp.int32(0), jnp.int32(1), jnp.int32(2), c, p if lead == 'chip' else jnp.int32(lead)])
        out = ('x', shape, F32, (None, None, 'tr', cols), lambda r, pr: (pr[5], pr[4], r, 0))
        bufs[o] = _rows(f"rs_chip_sum_{tag}_{t}", lambda a, b0, b1, b2: (((a.astype(F32) + b0.astype(F32)) + b1.astype(F32)) + b2.astype(F32),),
                        [(m3, 's', cols, 0), (t3, 's', cols, 1), (t3, 's', cols, 2), (t3, 's', cols, 3)], [out], 512, pre=pre, into=bufs[o])[0]


def kernel(x, norm_w, out_proj, s5_in_proj, s5_a_re, s5_a_im, s5_log_dt, s5_b_re, s5_b_im, s5_c_re, s5_c_im, s5_d, s5_w_glu, s5_b_glu, fox_in_proj, fox_q_norm, fox_k_norm, fox_f_bias, pool_in_proj, pool_w_group, pool_scale, loss_target, m_norm_w, m_out_proj, m_s5_in_proj, m_s5_a_re, m_s5_a_im, m_s5_log_dt, m_s5_b_re, m_s5_b_im, m_s5_c_re, m_s5_c_im, m_s5_d, m_s5_w_glu, m_s5_b_glu, m_fox_in_proj, m_fox_q_norm, m_fox_k_norm, m_fox_f_bias, m_pool_in_proj, m_pool_w_group, m_pool_scale, v_norm_w, v_out_proj, v_s5_in_proj, v_s5_a_re, v_s5_a_im, v_s5_log_dt, v_s5_b_re, v_s5_b_im, v_s5_c_re, v_s5_c_im, v_s5_d, v_s5_w_glu, v_s5_b_glu, v_fox_in_proj, v_fox_q_norm, v_fox_k_norm, v_fox_f_bias, v_pool_in_proj, v_pool_w_group, v_pool_scale):
    weights = dict(norm_w=norm_w, out_proj=out_proj, s5_in_proj=s5_in_proj, s5_a_re=s5_a_re, s5_a_im=s5_a_im, s5_log_dt=s5_log_dt,
                   s5_b_re=s5_b_re, s5_b_im=s5_b_im, s5_c_re=s5_c_re, s5_c_im=s5_c_im, s5_d=s5_d, s5_w_glu=s5_w_glu, s5_b_glu=s5_b_glu,
                   fox_in_proj=fox_in_proj, fox_q_norm=fox_q_norm, fox_k_norm=fox_k_norm, fox_f_bias=fox_f_bias,
                   pool_in_proj=pool_in_proj, pool_w_group=pool_w_group, pool_scale=pool_scale)
    mom_m = dict(norm_w=m_norm_w, out_proj=m_out_proj, s5_in_proj=m_s5_in_proj, s5_a_re=m_s5_a_re, s5_a_im=m_s5_a_im, s5_log_dt=m_s5_log_dt,
                 s5_b_re=m_s5_b_re, s5_b_im=m_s5_b_im, s5_c_re=m_s5_c_re, s5_c_im=m_s5_c_im, s5_d=m_s5_d, s5_w_glu=m_s5_w_glu, s5_b_glu=m_s5_b_glu,
                 fox_in_proj=m_fox_in_proj, fox_q_norm=m_fox_q_norm, fox_k_norm=m_fox_k_norm, fox_f_bias=m_fox_f_bias,
                 pool_in_proj=m_pool_in_proj, pool_w_group=m_pool_w_group, pool_scale=m_pool_scale)
    mom_v = dict(norm_w=v_norm_w, out_proj=v_out_proj, s5_in_proj=v_s5_in_proj, s5_a_re=v_s5_a_re, s5_a_im=v_s5_a_im, s5_log_dt=v_s5_log_dt,
                 s5_b_re=v_s5_b_re, s5_b_im=v_s5_b_im, s5_c_re=v_s5_c_re, s5_c_im=v_s5_c_im, s5_d=v_s5_d, s5_w_glu=v_s5_w_glu, s5_b_glu=v_s5_b_glu,
                 fox_in_proj=v_fox_in_proj, fox_q_norm=v_fox_q_norm, fox_k_norm=v_fox_k_norm, fox_f_bias=v_fox_f_bias,
                 pool_in_proj=v_pool_in_proj, pool_w_group=v_pool_w_group, pool_scale=v_pool_scale)
    return _step(x, loss_target, weights, mom_m, mom_v)


BIG = ('out_proj', 's5_in_proj', 's5_w_glu', 'fox_in_proj', 'pool_in_proj', 'pool_w_group')
SMALL = ('norm_w', 's5_a_re', 's5_a_im', 's5_log_dt', 's5_b_re', 's5_b_im', 's5_c_re', 's5_c_im', 's5_d', 's5_b_glu',
         'fox_q_norm', 'fox_k_norm', 'fox_f_bias', 'pool_scale')
SMALL_SHARDED = ('s5_d', 's5_b_glu', 'pool_scale')
GROUP_AXIS_1 = ('s5_a_re', 's5_a_im', 's5_b_re', 's5_b_im', 's5_c_re', 's5_c_im')
ORDER = ('norm_w', 'out_proj', 's5_in_proj', 's5_a_re', 's5_a_im', 's5_log_dt', 's5_b_re', 's5_b_im', 's5_c_re', 's5_c_im', 's5_d',
         's5_w_glu', 's5_b_glu', 'fox_in_proj', 'fox_q_norm', 'fox_k_norm', 'fox_f_bias', 'pool_in_proj', 'pool_w_group', 'pool_scale')


def _split2(shape):
    if shape[0] % 2 == 0:
        return (2, shape[0] // 2) + tuple(shape[1:])
    assert shape[0] == 1 and shape[1] % 2 == 0
    return (2, shape[1] // 2) + tuple(shape[2:])


def _cast_weights(w):
    p = (2 * lax.axis_index("x") + lax.axis_index("y")).astype(jnp.int32)
    bufs = {}
    for n in BIG:
        a3 = w[n].reshape(w[n].shape[0], -1, w[n].shape[-1])
        layers, rows, cols = a3.shape
        for l in range(layers):
            out = ('x', (N_CHIPS, rows, cols), BF16, (None, 'tr', cols), lambda r, pr: (pr[0], r, 0))
            b = _rows(f"cast_{n}_{l}", lambda v: (v,), [(a3, 's', cols, 1)], [out], 256, pre=jnp.stack([p, jnp.int32(l)]))[0]
            bufs[(n, l)] = b.reshape(N_CHIPS, 2, rows // 2, cols)
    return bufs


def _step(x, loss_target, w, mom_m, mom_v):
    T, D = x.shape[1], x.shape[2]
    E = D
    G, P, C = w['s5_a_re'].shape[1], S5_STATE, S5_GROUP
    H = E // FOX_HEAD_DIM
    PG = len(POOL_WINDOWS)
    PD = E // PG
    NC = G // GROUPS_PER_CHUNK
    L = GROUPS_PER_CHUNK * P
    tq = _t(256, T)
    nq = T // tq

    wb = _cast_weights(w)
    phases = [[('s5_in_proj', 0)],
              [('s5_w_glu', 0), ('out_proj', 0)],
              [('out_proj', 1), ('fox_in_proj', 0)],
              [('out_proj', 2), ('pool_in_proj', 0), ('pool_w_group', 0), ('out_proj', 3), ('s5_in_proj', 1), ('s5_w_glu', 1)]]
    W = {}
    flight = [None]

    def landed(keys, bufs):
        for k, b in zip(keys, bufs):
            W[k] = b.reshape(N_CHIPS, 2 * b.shape[2], b.shape[3])

    def start_phase(ph, after):
        flight[0] = _gather_start(f"gather_{ph}_start", [wb[k] for k in phases[ph]], [after]) if ph < len(phases) else None

    def take_phase(ph, after):
        bufs, send, recv, _ = flight[0]
        got = _gather_forward(f"gather_{ph}_pass", _gather_wait(f"gather_{ph}_wait", bufs, send, recv, after))
        landed(phases[ph], got)
        start_phase(ph + 1, got[0])

    def gather_token():
        return [flight[0][3]] if flight[0] is not None else ()

    landed(phases[0], _chip_allgather("gather_0", [wb[k] for k in phases[0]]))
    start_phase(1, W[phases[0][0]])
    small_full = {}
    chip = 2 * lax.axis_index("x") + lax.axis_index("y")
    sv = [lax.dynamic_update_index_in_dim(jnp.zeros((N_CHIPS, 2) + w[n].shape, F32), jnp.stack([w[n], w[n]]), chip, 0)
          for n in SMALL_SHARDED]
    got = _chip_allgather("gather_vectors", sv)
    for n, g in zip(SMALL_SHARDED, got):
        small_full[n] = jnp.transpose(g[:, 0], (1, 0, 2)).reshape(w[n].shape[0], E)

    norm_w = w['norm_w']
    h = x.reshape(T, D)
    saved = []
    dparts = {}

    def s5_consts(j):
        ar, ai, fr, fi = _s5_disc_fwd(f"s5_disc_{j}", w['s5_a_re'][j], w['s5_a_im'][j], w['s5_log_dt'][j].reshape(G, 1))
        br, bi = w['s5_b_re'][j].reshape(G * P, C), w['s5_b_im'][j].reshape(G * P, C)
        bbr, bbi = _s5_bbar(f"s5_bbar_{j}", fr.reshape(G * P, 1), fi.reshape(G * P, 1), br, bi)
        bbd = jnp.concatenate([_compact(bbr.reshape(G, P, C), NC), _compact(bbi.reshape(G, P, C), NC)], axis=2).astype(BF16)
        ct = lambda v: jnp.transpose(v, (0, 2, 1))
        cbd = jnp.concatenate([_compact(ct(w['s5_c_re'][j]), NC), -_compact(ct(w['s5_c_im'][j]), NC)], axis=2).astype(BF16)
        return dict(ar=ar, ai=ai, fr=fr, fi=fi, br=br, bi=bi, bbd=bbd, cbd=cbd,
                    ar3=ar.reshape(NC, 1, L), ai3=ai.reshape(NC, 1, L))

    for i in range(4):
        kind, j = i % 3, i // 3
        nw = norm_w[i].reshape(1, D)
        xn = _norm_fwd(f"norm_{i}", h, nw, deps=gather_token())
        if kind == 0:
            k5 = s5_consts(j)
            proj = _mm_proj(f"s5_proj_{i}", xn, W[('s5_in_proj', j)])
            dsk = small_full['s5_d'][j].reshape(1, E)
            y1, g, hs = _s5_fwd(f"s5_scan_{i}", proj, k5['bbd'], k5['cbd'], k5['ar3'], k5['ai3'], dsk, E)
            bglu = small_full['s5_b_glu'][j].reshape(1, E)
            if i == 0:
                take_phase(1, y1)

            def glu_epi(acc, b, y1t, z):
                lin = acc + b
                return lin, (_gelu(y1t) * _sigmoid(lin)) * _silu(z)

            lin, a = _mm_rowsharded(
                f"s5_glu_{i}", g, W[('s5_w_glu', j)], epi=glu_epi, deps=gather_token() if i == 0 else (),
                extras=lambda tm, tn: [(bglu, _rowvec(tn)), (y1, _tile(tm, tn)), (proj, _tile(tm, tn, E // tn))],
                outs_fn=lambda tm, tn: [((T, E), F32, _tile(tm, tn)), ((T, E), BF16, _tile(tm, tn))])
            saved.append(dict(h=h, xn=xn, proj=proj, y1=y1, g=g, hs=hs, lin=lin, a=a, k5=k5, dsk=dsk))
        elif kind == 1:
            fox_w = jnp.transpose(W[('fox_in_proj', j)], (1, 0, 2)).reshape(D, -1)
            w_qkvz = fox_w[:, :4 * E]
            w_f = jnp.pad(fox_w[:, 4 * E:], ((0, 0), (0, LANES - H)))
            proj = _mm_plain(f"fox_proj_{i}", xn, w_qkvz)[0]
            flog = _mm_plain(f"fox_gate_proj_{i}", xn, w_f)[0]
            fb = jnp.pad(w['fox_f_bias'][j].reshape(1, H), ((0, 0), (0, LANES - H)))
            wq, wk = w['fox_q_norm'][j].reshape(1, FOX_HEAD_DIM), w['fox_k_norm'][j].reshape(1, FOX_HEAD_DIM)
            qn, kn = _qk_norm(f"fox_qk_norm_{i}", proj, wq, wk, H)
            cum = _cum_rows(f"fox_cum_{i}", flog, fb, False, True)
            cum_t = jnp.transpose(cum)[:H]
            cum_q = jnp.broadcast_to(cum_t[:, :, None], (H, T, LANES))
            cum_k = cum_t.reshape(H, nq, 1, tq)
            y, lse = _attn_fwd(f"fox_attn_{i}", qn, kn, proj, cum_q, cum_k, H)
            a = _rows(f"fox_gate_{i}", lambda yt, z: (yt * _silu(z),), [(y, 'r', E, 0), (proj, 'r', E, 3)], [('r', E, BF16)], 256)[0]
            saved.append(dict(h=h, xn=xn, proj=proj, flog=flog, fb=fb, wq=wq, wk=wk, qn=qn, kn=kn, cum_q=cum_q, cum_k=cum_k, y=y, lse=lse, a=a,
                              w_qkvz=w_qkvz, w_f=w_f))
        else:
            w_pg = W[('pool_w_group', j)].reshape(N_CHIPS, PG, PD // N_CHIPS, PD)
            proj = _mm_proj(f"pool_proj_{i}", xn, W[('pool_in_proj', j)])
            pm = _pool_fwd(f"pool_win_{i}", proj, E)
            scale = small_full['pool_scale'][j].reshape(1, E)
            tm, tn, tk = _t(512, T), _t(512, PD), w_pg.shape[2]
            kb, nb = PD // tk, PD // tn
            mixed, a = _mm(
                f"pool_mix_{i}", pm, w_pg, M=T, N=PD, K=PD, tm=tm, tn=tn, tk=tk, groups=PG,
                a_spec=_bs((tm, tk), lambda g, m, n, k: (m, g * kb + k)),
                b_spec=_bs((None, None, tk, tn), lambda g, m, n, k: (k, g, 0, n)),
                extras=[(scale, _bs((1, tn), lambda g, m, n, k: (0, g * nb + n))),
                        (proj, _bs((tm, tn), lambda g, m, n, k: (m, E // tn + g * nb + n)))],
                epi=lambda acc, sc, z: (acc, (acc * sc) * _silu(z)),
                outs=[((T, E), F32, _bs((tm, tn), lambda g, m, n, k: (m, g * nb + n))),
                      ((T, E), BF16, _bs((tm, tn), lambda g, m, n, k: (m, g * nb + n)))])
            saved.append(dict(h=h, xn=xn, proj=proj, pm=pm, mixed=mixed, scale=scale, a=a, w_pg=w_pg))
        h = _mm_rowsharded(f"out_proj_{i}", saved[-1]['a'], W[('out_proj', i)], epi=lambda acc, r: (r + acc,),
                           extras=lambda tm, tn: [(h, _tile(tm, tn))],
                           outs_fn=lambda tm, tn: [((T, D), F32, _tile(tm, tn))])[0]
        if i < 2:
            take_phase(i + 2, h)

    dh, dh16, loss_cols = _loss(h, loss_target.reshape(T, D))
    loss = lax.psum(jnp.sum(loss_cols), ("x", "y", "c"))

    gsmall = {n: [None] * w[n].shape[0] for n in SMALL}
    big_index = {n: o for o, n in enumerate(BIG)}
    rs_shapes = [None] * (len(BIG) + 1)
    rs_bufs = [None] * (len(BIG) + 1)
    rs_dests_all = []
    pending = None

    def reduce_layer(tag, named_parts):
        parts, dests = [], []
        for n, l, pt in named_parts:
            o = big_index[n] if n in big_index else len(BIG)
            half = pt.shape[2:]
            rs_shapes[o] = (N_CHIPS if l == 'chip' else w[n].shape[0], 2, math.prod(half[:-1]), half[-1])
            parts.append(pt)
            dests.append((o, l))
        rs_dests_all.extend(dests)
        state, token = _reduce_begin(tag, parts)
        return (tag, state, dests), token

    token = None
    for i in reversed(range(4)):
        kind, j = i % 3, i // 3
        sv_ = saved[i]
        nw = norm_w[i].reshape(1, D)
        w_out = W[('out_proj', i)]
        after_start = [token] if token is not None else ()
        layer_parts = [('out_proj', i, _mm_dw_rows(f"d_out_proj_{i}", sv_['a'], dh16, deps=after_start))]
        if kind == 0:
            w_glu = W[('s5_w_glu', j)]
            proj, y1, lin, k5 = sv_['proj'], sv_['y1'], sv_['lin'], sv_['k5']

            def da_epi(da, y1t, lint, z):
                gt, sg = _gelu(y1t), _sigmoid(lint)
                dy2 = da * _silu(z)
                dlin = (dy2 * gt) * (sg * (1.0 - sg))
                return da * (gt * sg) * _dsilu(z), dlin, dy2 * sg, _colsum(dlin)

            nm = T // _t(512, T)
            dz, dlin, dgd, dbg = _mm_rowsharded_t(
                f"d_s5_act_{i}", dh16, w_out, epi=da_epi, deps=after_start,
                extras=lambda tm, tn: [(y1, _tile(tm, tn)), (lin, _tile(tm, tn)), (proj, _tile(tm, tn, E // tn))],
                outs_fn=lambda tm, tn: [((T, E), BF16, _tile(tm, tn)), ((T, E), BF16, _tile(tm, tn)), ((T, E), F32, _tile(tm, tn)),
                                        ((nm, 1, E), F32, _bs((None, 1, tn), lambda g, m, n, k: (m, 0, n)))])
            gsmall['s5_b_glu'][j] = jnp.sum(dbg, axis=(0, 1))
            layer_parts.append(('s5_w_glu', j, _mm_dw_rows(f"d_s5_w_glu_{i}", sv_['g'], dlin)))
            glu_deps = ()
            if i == 0:
                early, early_token = reduce_layer("l0a", layer_parts)
                layer_parts, glu_deps = [], [early_token]
            dy1 = _mm_rowsharded_t(
                f"d_s5_glu_{i}", dlin, w_glu, epi=lambda acc, d, y1t: ((acc + d) * _dgelu(y1t),), deps=glu_deps,
                extras=lambda tm, tn: [(dgd, _tile(tm, tn)), (y1, _tile(tm, tn))],
                outs_fn=lambda tm, tn: [((T, E), F32, _tile(tm, tn))])[0]
            du, dbd, dcd, dab, ddk = _s5_bwd(f"d_s5_scan_{i}", dy1, proj, sv_['hs'], k5['bbd'], k5['cbd'], k5['ar3'], k5['ai3'], sv_['dsk'], E)
            gsmall['s5_d'][j] = ddk.reshape(E)
            gsmall['s5_c_re'][j] = jnp.transpose(_uncompact(dcd[:, :, :L], G), (0, 2, 1))
            gsmall['s5_c_im'][j] = -jnp.transpose(_uncompact(dcd[:, :, L:], G), (0, 2, 1))
            dbbr = _uncompact(dbd[:, :, :L], G).reshape(G * P, C)
            dbbi = _uncompact(dbd[:, :, L:], G).reshape(G * P, C)
            dbr, dbi, dfr, dfi = _s5_bbar_bwd(f"d_s5_bbar_{i}", k5['fr'].reshape(G * P, 1), k5['fi'].reshape(G * P, 1), k5['br'], k5['bi'], dbbr, dbbi)
            gsmall['s5_b_re'][j] = dbr.reshape(G, P, C)
            gsmall['s5_b_im'][j] = dbi.reshape(G, P, C)
            dab = jnp.sum(dab, axis=1)
            dare, daim, dldt = _s5_disc_bwd(f"d_s5_disc_{i}", w['s5_a_re'][j], w['s5_a_im'][j], w['s5_log_dt'][j].reshape(G, 1),
                                            (dab[:, :L].reshape(G, P), dab[:, L:].reshape(G, P), dfr.reshape(G, P), dfi.reshape(G, P)))
            gsmall['s5_a_re'][j], gsmall['s5_a_im'][j], gsmall['s5_log_dt'][j] = dare, daim, dldt.reshape(G)
            dproj = jnp.concatenate([du, dz], axis=1)
            layer_parts.append(('s5_in_proj', j, _mm_dw_cols(f"d_s5_in_proj_{i}", sv_['xn'], dproj)))
            dxn = _mm_colsharded_t(f"d_s5_xn_{i}", dproj, W[('s5_in_proj', j)])
        elif kind == 1:
            proj, y = sv_['proj'], sv_['y']
            do, dz = _mm_rowsharded_t(
                f"d_fox_act_{i}", dh16, w_out, epi=lambda da, yt, z: (da * _silu(z), (da * yt) * _dsilu(z)), deps=after_start,
                extras=lambda tm, tn: [(y, _tile(tm, tn)), (proj, _tile(tm, tn, 3 * E // tn))],
                outs_fn=lambda tm, tn: [((T, E), F32, _tile(tm, tn)), ((T, E), BF16, _tile(tm, tn))])
            dqn, dkn, dv, dcq, dck = _attn_bwd(f"d_fox_attn_{i}", sv_['qn'], sv_['kn'], proj, do, y, sv_['lse'], sv_['cum_q'], sv_['cum_k'], H)
            dq, dk, dwq, dwk = _qk_norm_bwd(f"d_fox_qk_norm_{i}", proj, sv_['wq'], sv_['wk'], dqn, dkn, H)
            gsmall['fox_q_norm'][j], gsmall['fox_k_norm'][j] = dwq.reshape(-1), dwk.reshape(-1)
            dcum = dcq + jnp.pad(jnp.transpose(dck.reshape(H, T)), ((0, 0), (0, LANES - H)))
            dls = _cum_rows(f"d_fox_cum_{i}", dcum, jnp.zeros((1, LANES), F32), True, False)
            dflog, dfb = _rows(f"d_fox_gate_{i}", lambda d, f, b: ((lambda r: (r, _colsum(r)))(d * _sigmoid(-(f + b)))),
                               [(dls, 'r', LANES, 0), (sv_['flog'], 'r', LANES, 0), (sv_['fb'], 'b', LANES, 0)],
                               [('r', LANES, BF16), ('a', LANES, F32)], 256)
            gsmall['fox_f_bias'][j] = dfb[0, :H]
            dproj = jnp.concatenate([dq, dk, dv, dz], axis=1)
            tkT = _t(K_STEP, T)
            dw_qkvz = _mm(f"d_fox_in_proj_{i}", sv_['xn'], dproj, M=D, N=4 * E, K=T, tm=_t(512, D), tn=_t(1024, 4 * E), tk=tkT, ta=True,
                          a_spec=_bs((tkT, _t(512, D)), lambda g, m, n, k: (k, m)),
                          b_spec=_bs((tkT, _t(1024, 4 * E)), lambda g, m, n, k: (k, n)),
                          outs=[((D, 4 * E), BF16, _tile(_t(512, D), _t(1024, 4 * E)))])[0]
            dw_f = _mm(f"d_fox_gate_proj_{i}", sv_['xn'], dflog, M=D, N=LANES, K=T, tm=_t(512, D), tn=LANES, tk=tkT, ta=True,
                       a_spec=_bs((tkT, _t(512, D)), lambda g, m, n, k: (k, m)),
                       b_spec=_bs((tkT, LANES), lambda g, m, n, k: (k, n)),
                       outs=[((D, LANES), BF16, _tile(_t(512, D), LANES))])[0]
            dw_fox = jnp.concatenate([dw_qkvz, dw_f[:, :H]], axis=1)
            sw = dw_fox.shape[1] // N_CHIPS
            layer_parts.append(('fox_in_proj', j, jnp.transpose(dw_fox.reshape(2, D // 2, N_CHIPS, sw), (0, 2, 1, 3))))
            w_qkvz, w_f = sv_['w_qkvz'], sv_['w_f']
            dxn_f = _mm(f"d_fox_xn_gate_{i}", dflog, w_f, M=T, N=D, K=LANES, tm=_t(512, T), tn=_t(1024, D), tk=LANES, tb=True,
                        a_spec=_bs((_t(512, T), LANES), lambda g, m, n, k: (m, k)),
                        b_spec=_bs((_t(1024, D), LANES), lambda g, m, n, k: (n, k)),
                        outs=[((T, D), F32, _tile(_t(512, T), _t(1024, D)))])[0]
            tm, tn, tk = _t(512, T), _t(1024, D), _t(1024, 4 * E)
            dxn = _mm(f"d_fox_xn_{i}", dproj, w_qkvz, M=T, N=D, K=4 * E, tm=tm, tn=tn, tk=tk, tb=True,
                      a_spec=_bs((tm, tk), lambda g, m, n, k: (m, k)), b_spec=_bs((tn, tk), lambda g, m, n, k: (n, k)),
                      extras=[(dxn_f, _tile(tm, tn))], epi=lambda acc, e: (acc + e,),
                      outs=[((T, D), F32, _tile(tm, tn))])[0]
        else:
            proj, mixed, scale = sv_['proj'], sv_['mixed'], sv_['scale']
            nm = T // _t(512, T)

            def pool_epi(da, mx, sc, z):
                dy = da * _silu(z)
                return (da * (mx * sc)) * _dsilu(z), dy * sc, _colsum(dy * mx)

            dz, dmix, dsc = _mm_rowsharded_t(
                f"d_pool_act_{i}", dh16, w_out, epi=pool_epi, deps=after_start,
                extras=lambda tm, tn: [(mixed, _tile(tm, tn)), (scale, _rowvec(tn)), (proj, _tile(tm, tn, E // tn))],
                outs_fn=lambda tm, tn: [((T, E), BF16, _tile(tm, tn)), ((T, E), BF16, _tile(tm, tn)),
                                        ((nm, 1, E), F32, _bs((None, 1, tn), lambda g, m, n, k: (m, 0, n)))])
            gsmall['pool_scale'][j] = jnp.sum(dsc, axis=(0, 1))
            w_pg = sv_['w_pg']
            tkw = w_pg.shape[2]
            tk = _t(K_STEP, T)
            layer_parts.append(('pool_w_group', j, _mm(
                f"d_pool_w_group_{i}", sv_['pm'], dmix, M=PD, N=PD, K=T, tm=tkw, tn=PD, tk=tk, groups=PG, ta=True,
                a_spec=_bs((tk, tkw), lambda g, m, n, k: (k, g * (PD // tkw) + m)),
                b_spec=_bs((tk, PD), lambda g, m, n, k: (k, g)),
                outs=[((2, N_CHIPS, PG // 2, tkw, PD), BF16, _bs((None, None, None, tkw, PD), lambda g, m, n, k: (g // (PG // 2), m, g % (PG // 2), 0, 0)))])[0]))
            tm, tk2 = _t(512, T), _t(512, PD)
            dpm = _mm(f"d_pool_mix_{i}", dmix, w_pg, M=T, N=PD, K=PD, tm=tm, tn=tkw, tk=tk2, groups=PG, tb=True,
                      a_spec=_bs((tm, tk2), lambda g, m, n, k: (m, g * (PD // tk2) + k)),
                      b_spec=_bs((None, None, tkw, tk2), lambda g, m, n, k: (n, g, 0, k)),
                      outs=[((T, E), F32, _bs((tm, tkw), lambda g, m, n, k: (m, g * (PD // tkw) + n)))])[0]
            du = _pool_bwd(f"d_pool_win_{i}", dpm, E)
            dproj = jnp.concatenate([du, dz], axis=1)
            layer_parts.append(('pool_in_proj', j, _mm_dw_cols(f"d_pool_in_proj_{i}", sv_['xn'], dproj)))
            dxn = _mm_colsharded_t(f"d_pool_xn_{i}", dproj, W[('pool_in_proj', j)])
        dh, dh16, dnw = _norm_bwd(f"d_norm_{i}", dxn, sv_['h'], nw, dh)
        gsmall['norm_w'][i] = dnw.reshape(D)
        if pending is not None:
            _reduce_end(pending[0], pending[1], dh16, pending[2], rs_bufs, rs_shapes)
        if i > 0:
            pending, token = reduce_layer(f"l{i}", layer_parts)
    grad_x = dh.reshape(x.shape)

    small_flat = jnp.concatenate([jnp.stack(gsmall[n]).reshape(-1) for n in SMALL])
    n_small = small_flat.shape[0]
    unit = 2 * N_CHIPS * 16 * LANES
    n_pad = -(-n_small // unit) * unit
    R = n_pad // (2 * N_CHIPS * LANES)
    small_part = jnp.pad(small_flat, (0, n_pad - n_small)).astype(BF16).reshape(2, N_CHIPS, R, LANES)
    pending, token = reduce_layer("l0", layer_parts + [('small', 'chip', small_part)])
    _reduce_end(early[0], early[1], token, early[2], rs_bufs, rs_shapes)
    _reduce_end(pending[0], pending[1], token, pending[2], rs_bufs, rs_shapes)
    red = _pair_share("rs_pair_share", rs_bufs, rs_dests_all)
    grads = {n: r.reshape(w[n].shape) for n, r in zip(BIG, red[:len(BIG)])}
    small_all = _chip_allgather("gather_small_grads", [red[len(BIG)]])[0]
    small_all = jnp.transpose(small_all, (1, 0, 2, 3)).reshape(-1)[:n_small]
    off = 0
    p = 2 * lax.axis_index("x") + lax.axis_index("y")
    for n in SMALL:
        full_shape = (w[n].shape[0], E) if n in SMALL_SHARDED else w[n].shape
        size = math.prod(full_shape)
        gfull = small_all[off:off + size].reshape(full_shape)
        off += size
        if n in SMALL_SHARDED:
            gfull = lax.dynamic_slice_in_dim(gfull, p * (E // N_CHIPS), E // N_CHIPS, axis=1)
        grads[n] = gfull

    delta, new_m, new_v = {}, {}, {}
    for n in BIG:
        if w[n].shape[-1] % LANES:
            f2 = lambda a: jnp.transpose(a.reshape(-1, a.shape[-1]))
            b2 = lambda a: jnp.transpose(a).reshape(w[n].shape)
        else:
            f2 = lambda a: a.reshape(-1, a.shape[-1])
            b2 = lambda a: a.reshape(w[n].shape)
        d_, m_, v_ = _adamw(f"adamw_{n}", f2(w[n]), f2(grads[n]), f2(mom_m[n]), f2(mom_v[n]))
        delta[n], new_m[n], new_v[n] = b2(d_), b2(m_), b2(v_)
    for n in SMALL:
        shape = w[n].shape
        if n in GROUP_AXIS_1:
            perm = (0,) + tuple(range(2, len(shape))) + (1,)
            inv = (0, len(shape) - 1) + tuple(range(1, len(shape) - 1))
            view = lambda a: jnp.transpose(a, perm).reshape(-1, shape[1])
            back = lambda a: jnp.transpose(a.reshape(tuple(shape[k] for k in perm)), inv)
        else:
            view = lambda a: a.reshape(-1, shape[-1])
            back = lambda a: a.reshape(shape)
        d_, m_, v_ = _adamw(f"adamw_{n}", view(w[n]), view(grads[n]), view(mom_m[n]), view(mom_v[n]))
        delta[n], new_m[n], new_v[n] = back(d_), back(m_), back(v_)
    return (loss, grad_x, *[grads[n] for n in ORDER], *[delta[n] for n in ORDER], *[new_m[n] for n in ORDER], *[new_v[n] for n in ORDER])
```

```python
import functools
import math

import jax
import jax.numpy as jnp
from jax import lax
from jax.experimental import pallas as pl
from jax.experimental.pallas import tpu as pltpu

F32 = jnp.float32
BF16 = jnp.bfloat16
MESH = pl.DeviceIdType.MESH

N_CHIPS = 4
VMEM_LIMIT = 56 * 1024 * 1024
LANES = 128
SUB = 8

EPS = 1e-6
S5_GROUP = 16
S5_STATE = 64
GROUPS_PER_CHUNK = 16
FOX_HEAD_DIM = 128
ATTN_SUB = 256
POOL_WINDOWS = (2, 4, 8, 16)
POOL_HALO = 16
ADAM_LR, ADAM_B1, ADAM_B2, ADAM_EPS, ADAM_WD, ADAM_STEP = 0.001, 0.9, 0.999, 1e-08, 0.01, 10
NEG = -1e30
K_STEP = 2048


ANY = pl.BlockSpec(memory_space=pl.ANY)


def _t(pref, dim):
    if dim <= pref:
        return dim
    t = pref - pref % 16
    while t > 16 and dim % t:
        t -= 16
    assert dim % t == 0, (pref, dim)
    return t


def _params(sem):
    return pltpu.CompilerParams(dimension_semantics=sem, vmem_limit_bytes=VMEM_LIMIT)


def _sigmoid(x):
    return 1.0 / (1.0 + jnp.exp(-x))


def _silu(z):
    return z * _sigmoid(z)


def _dsilu(z):
    s = _sigmoid(z)
    return s * (1.0 + z * (1.0 - s))


_GELU_C = math.sqrt(2.0 / math.pi)


def _gelu(x):
    return 0.5 * x * (1.0 + jnp.tanh(_GELU_C * (x + 0.044715 * (x * x * x))))


def _dgelu(x):
    t = jnp.tanh(_GELU_C * (x + 0.044715 * (x * x * x)))
    return 0.5 * (1.0 + t) + 0.5 * x * (1.0 - t * t) * (_GELU_C * (1.0 + 3.0 * 0.044715 * x * x))


def _log_sigmoid(x):
    return jnp.minimum(x, 0.0) - jnp.log(1.0 + jnp.exp(-jnp.abs(x)))


def _rms(x):
    return lax.rsqrt(jnp.mean(x * x, axis=-1, keepdims=True) + EPS)


def _rms_bwd(x, w, dy):
    r = _rms(x)
    xhat = x * r
    dxh = dy * w
    dx = r * (dxh - xhat * jnp.mean(dxh * xhat, axis=-1, keepdims=True))
    return dx, dy * xhat


def _rows(name, fn, ins, outs, tr, pre=None, into=None, deps=()):
    rows = None
    for arr, kind, cols, cb in ins:
        if kind == 'r':
            rows = arr.shape[0]
        elif kind == 's' and rows is None:
            rows = arr.shape[1]
    tr = _t(tr, rows)
    n_in = len(ins)
    has_acc = any(o[0] == 'a' for o in outs)

    def spec(kind, cols, cb):
        if kind == 'r':
            return pl.BlockSpec((tr, cols), lambda r, *p: (r, cb))
        if kind == 'b':
            return pl.BlockSpec((1, cols), lambda r, *p: (0, cb))
        return pl.BlockSpec((None, tr, cols), lambda r, p: (p[cb], r, 0))

    in_specs = [spec(kind, cols, cb) for _, kind, cols, cb in ins]
    out_specs, out_shape = [], []
    for o in outs:
        if o[0] == 'r':
            out_specs.append(pl.BlockSpec((tr, o[1]), lambda r, *p: (r, 0)))
            out_shape.append(jax.ShapeDtypeStruct((rows, o[1]), o[2]))
        elif o[0] == 'a':
            out_specs.append(pl.BlockSpec((1, o[1]), lambda r, *p: (0, 0)))
            out_shape.append(jax.ShapeDtypeStruct((1, o[1]), o[2]))
        else:
            blk = tuple(tr if d == 'tr' else d for d in o[3])
            out_specs.append(pl.BlockSpec(blk, o[4]))
            out_shape.append(jax.ShapeDtypeStruct(o[1], o[2]))
    n_pre = 0 if pre is None else 1
    args = [a[0] for a in ins]
    aliases = {}
    if into is not None:
        in_specs.append(ANY)
        args.append(into)
        aliases = {n_pre + n_in: 0}
    in_specs += [ANY] * len(deps)
    args += list(deps)
    n_all = len(args)

    def body(*refs):
        refs = refs[n_pre:]
        res = fn(*[r[...] for r in refs[:n_in]])
        for spec_o, o, v in zip(outs, refs[n_all:], res):
            if spec_o[0] == 'a':
                @pl.when(pl.program_id(0) == 0)
                def _():
                    o[...] = jnp.zeros_like(o)
                o[...] += v.astype(o.dtype)
            else:
                o[...] = v.astype(o.dtype)

    grid_spec = pltpu.PrefetchScalarGridSpec(num_scalar_prefetch=n_pre, grid=(rows // tr,), in_specs=in_specs, out_specs=out_specs)
    if pre is not None:
        args = [pre] + args
    return pl.pallas_call(body, name=name, grid_spec=grid_spec, out_shape=out_shape, input_output_aliases=aliases,
                          compiler_params=_params(("arbitrary" if has_acc else "parallel",)))(*args)


def _colsum(v):
    return jnp.sum(v, axis=0, keepdims=True)


def _mm(name, a, b, *, M, N, K, tm, tn, tk, a_spec, b_spec, outs, epi=None, extras=(), groups=1, ta=False, tb=False, deps=()):
    nk = K // tk
    assert M % tm == 0 and N % tn == 0 and K % tk == 0, (name, M, N, K, tm, tn, tk)
    dims = (((0 if ta else 1,), (1 if tb else 0,)), ((), ()))
    n_ex = len(extras)

    def body(*refs):
        a_ref, b_ref = refs[0], refs[1]
        ex = refs[2:2 + n_ex]
        out_refs = refs[2 + n_ex + len(deps):2 + n_ex + len(deps) + len(outs)]

        def finish(r):
            res = (r,) if epi is None else epi(r, *[e[...] for e in ex])
            for o, v in zip(out_refs, res):
                o[...] = v.astype(o.dtype)

        part = lax.dot_general(a_ref[...].astype(BF16), b_ref[...].astype(BF16), dims, preferred_element_type=F32)
        if nk == 1:
            finish(part)
            return
        acc = refs[-1]
        k = pl.program_id(3)

        @pl.when(k == 0)
        def _():
            acc[...] = part

        @pl.when(k > 0)
        def _():
            acc[...] += part

        @pl.when(k == nk - 1)
        def _():
            finish(acc[...])

    return pl.pallas_call(
        body, name=name, grid=(groups, M // tm, N // tn, nk),
        in_specs=[a_spec, b_spec] + [s for _, s in extras] + [ANY] * len(deps),
        out_specs=[s for _, _, s in outs],
        out_shape=[jax.ShapeDtypeStruct(sh, dt) for sh, dt, _ in outs],
        scratch_shapes=[] if nk == 1 else [pltpu.VMEM((tm, tn), F32)],
        compiler_params=_params(("parallel", "parallel", "parallel", "arbitrary")),
    )(a, b, *[e for e, _ in extras], *deps)


def _bs(shape, f):
    return pl.BlockSpec(shape, f)


def _tile(tm, tn, coff=0):
    return _bs((tm, tn), lambda g, m, n, k: (m, n + coff))


def _rowvec(tn, coff=0):
    return _bs((1, tn), lambda g, m, n, k: (0, n + coff))


def _mm_proj(name, xn, w, *, epi=None, extras=(), out_dtype=F32):
    T, D = xn.shape
    sw = w.shape[2]
    N = N_CHIPS * sw
    tm, tn, tk = _t(512, T), _t(1024, sw), _t(K_STEP, D)
    nb = sw // tn
    return _mm(name, xn, w, M=T, N=N, K=D, tm=tm, tn=tn, tk=tk,
               a_spec=_bs((tm, tk), lambda g, m, n, k: (m, k)),
               b_spec=_bs((None, tk, tn), lambda g, m, n, k: (n // nb, k, n % nb)),
               outs=[((T, N), out_dtype, _tile(tm, tn))], epi=epi, extras=extras)[0]


def _mm_plain(name, a, b, *, out_dtype=F32, epi=None, extras=(), outs=None, tn_pref=1024):
    M, K = a.shape
    N = b.shape[1]
    tm, tn, tk = _t(512, M), _t(tn_pref, N), _t(K_STEP, K)
    if outs is None:
        outs = [((M, N), out_dtype, _tile(tm, tn))]
    return _mm(name, a, b, M=M, N=N, K=K, tm=tm, tn=tn, tk=tk,
               a_spec=_bs((tm, tk), lambda g, m, n, k: (m, k)),
               b_spec=_bs((tk, tn), lambda g, m, n, k: (k, n)),
               outs=outs, epi=epi, extras=extras)


def _mm_rowsharded(name, a, w, *, epi, extras, outs_fn, deps=()):
    T, E = a.shape
    tk = w.shape[1]
    N = w.shape[2]
    tm, tn = _t(512, T), _t(1024, N)
    return _mm(name, a, w, M=T, N=N, K=E, tm=tm, tn=tn, tk=tk, deps=deps,
               a_spec=_bs((tm, tk), lambda g, m, n, k: (m, k)),
               b_spec=_bs((None, tk, tn), lambda g, m, n, k: (k, 0, n)),
               outs=outs_fn(tm, tn), epi=epi, extras=extras(tm, tn))


def _mm_rowsharded_t(name, d, w, *, epi, extras, outs_fn, deps=()):
    T, N = d.shape
    tn = w.shape[1]
    E = N_CHIPS * tn
    tm, tk = _t(512, T), _t(K_STEP, N)
    return _mm(name, d, w, M=T, N=E, K=N, tm=tm, tn=tn, tk=tk, tb=True, deps=deps,
               a_spec=_bs((tm, tk), lambda g, m, n, k: (m, k)),
               b_spec=_bs((None, tn, tk), lambda g, m, n, k: (n, 0, k)),
               outs=outs_fn(tm, tn), epi=epi, extras=extras(tm, tn))


def _mm_colsharded_t(name, d, w):
    T, N = d.shape
    D, sw = w.shape[1], w.shape[2]
    tm, tn, tk = _t(512, T), _t(1024, D), _t(1024, sw)
    kb = sw // tk
    return _mm(name, d, w, M=T, N=D, K=N, tm=tm, tn=tn, tk=tk, tb=True,
               a_spec=_bs((tm, tk), lambda g, m, n, k: (m, k)),
               b_spec=_bs((None, tn, tk), lambda g, m, n, k: (k // kb, n, k % kb)),
               outs=[((T, D), F32, _tile(tm, tn))])[0]


def _mm_dw_rows(name, a, d, deps=()):
    T, E = a.shape
    N = d.shape[1]
    tm, tn, tk = E // (2 * N_CHIPS), _t(2048, N), _t(K_STEP, T)
    return _mm(name, a, d, M=E, N=N, K=T, tm=tm, tn=tn, tk=tk, ta=True, deps=deps,
               a_spec=_bs((tk, tm), lambda g, m, n, k: (k, m)),
               b_spec=_bs((tk, tn), lambda g, m, n, k: (k, n)),
               outs=[((2, N_CHIPS, tm, N), BF16, _bs((None, None, tm, tn), lambda g, m, n, k: (m % 2, m // 2, 0, n)))])[0]


def _mm_dw_cols(name, xn, d):
    T, D = xn.shape
    N = d.shape[1]
    sw = N // N_CHIPS
    tm, tn, tk = _t(512, D // 2), _t(1024, sw), _t(K_STEP, T)
    mh, nb = (D // 2) // tm, sw // tn
    return _mm(name, xn, d, M=D, N=N, K=T, tm=tm, tn=tn, tk=tk, ta=True,
               a_spec=_bs((tk, tm), lambda g, m, n, k: (k, m)),
               b_spec=_bs((tk, tn), lambda g, m, n, k: (k, n)),
               outs=[((2, N_CHIPS, D // 2, sw), BF16,
                      _bs((None, None, tm, tn), lambda g, m, n, k: (m // mh, n // nb, m % mh, n % nb)))])[0]


def _norm_fwd(name, h, w, deps=()):
    D = h.shape[1]
    return _rows(name, lambda x, g: ((x * _rms(x)) * g,), [(h, 'r', D, 0), (w, 'b', D, 0)], [('r', D, BF16)], 256, deps=deps)[0]


def _norm_bwd(name, dxn, h, w, dh):
    D = h.shape[1]

    def fn(dy, x, g, up):
        dx, dwt = _rms_bwd(x, g, dy)
        r = up + dx
        return r, r, _colsum(dwt)

    return _rows(name, fn, [(dxn, 'r', D, 0), (h, 'r', D, 0), (w, 'b', D, 0), (dh, 'r', D, 0)],
                 [('r', D, F32), ('r', D, BF16), ('a', D, F32)], 256)


def _loss(h, target):
    D = h.shape[1]

    def fn(y, t):
        e = y - t
        d = e * (1.0 / D)
        return d, d, _colsum(e * e) * (0.5 / D)

    return _rows("loss", fn, [(h, 'r', D, 0), (target, 'r', D, 0)], [('r', D, F32), ('r', D, BF16), ('a', D, F32)], 256)


def _adamw(name, w, g, m, v):
    cols = w.shape[1]

    def fn(w, g, m, v):
        m = ADAM_B1 * m + (1.0 - ADAM_B1) * g
        v = ADAM_B2 * v + (1.0 - ADAM_B2) * (g * g)
        m_hat = m / (1.0 - ADAM_B1 ** ADAM_STEP)
        v_hat = v / (1.0 - ADAM_B2 ** ADAM_STEP)
        delta = -ADAM_LR * (m_hat / (jnp.sqrt(v_hat) + ADAM_EPS) + ADAM_WD * w)
        return delta, m, v

    rows = w.shape[0]
    if rows % SUB == 0 or rows <= 256:
        return _rows(name, fn, [(x, 'r', cols, 0) for x in (w, g, m, v)], [('r', cols, F32)] * 3, 256)
    tc = _t(256, cols)
    assert tc % LANES == 0, (rows, cols)

    def body(w_ref, g_ref, m_ref, v_ref, d_out, m_out, v_out):
        for o, r in zip((d_out, m_out, v_out), fn(w_ref[...], g_ref[...], m_ref[...], v_ref[...])):
            o[...] = r

    blk = pl.BlockSpec((rows, tc), lambda j: (0, j))
    return pl.pallas_call(body, name=name, grid=(cols // tc,), in_specs=[blk] * 4, out_specs=[blk] * 3,
                          out_shape=[jax.ShapeDtypeStruct((rows, cols), F32)] * 3, compiler_params=_params(("parallel",)))(w, g, m, v)


def _s5_disc(a_re, a_im, log_dt):
    dt = jnp.exp(log_dt)
    mag = jnp.exp(a_re * dt)
    abar_r = mag * jnp.cos(a_im * dt)
    abar_i = mag * jnp.sin(a_im * dt)
    den = a_re * a_re + a_im * a_im
    xr = abar_r - 1.0
    fr = (xr * a_re + abar_i * a_im) / den
    fi = (abar_i * a_re - xr * a_im) / den
    return abar_r, abar_i, fr, fi


def _s5_disc_fwd(name, a_re, a_im, log_dt):
    G, P = a_re.shape

    def body(ar, ai, ld, o0, o1, o2, o3):
        for o, v in zip((o0, o1, o2, o3), _s5_disc(ar[...], ai[...], ld[...])):
            o[...] = v

    return pl.pallas_call(body, name=name, out_shape=[jax.ShapeDtypeStruct((G, P), F32)] * 4)(a_re, a_im, log_dt)


def _s5_disc_bwd(name, a_re, a_im, log_dt, cts):
    G, P = a_re.shape

    def body(ar, ai, ld, c0, c1, c2, c3, d0, d1, d2):
        _, vjp = jax.vjp(_s5_disc, ar[...], ai[...], ld[...])
        g0, g1, g2 = vjp((c0[...], c1[...], c2[...], c3[...]))
        d0[...] = g0
        d1[...] = g1
        d2[...] = g2

    return pl.pallas_call(body, name=name, out_shape=[jax.ShapeDtypeStruct((G, P), F32)] * 2 + [jax.ShapeDtypeStruct((G, 1), F32)])(
        a_re, a_im, log_dt, *cts)


def _s5_bbar(name, fr, fi, br, bi):
    return _rows(name, lambda fr, fi, br, bi: (fr * br - fi * bi, fr * bi + fi * br),
                 [(fr, 'r', 1, 0), (fi, 'r', 1, 0), (br, 'r', S5_GROUP, 0), (bi, 'r', S5_GROUP, 0)],
                 [('r', S5_GROUP, F32)] * 2, 2048)


def _s5_bbar_bwd(name, fr, fi, br, bi, dr, di):
    def fn(fr, fi, br, bi, dr, di):
        return (fr * dr + fi * di, fr * di - fi * dr,
                jnp.sum(br * dr + bi * di, axis=1, keepdims=True), jnp.sum(br * di - bi * dr, axis=1, keepdims=True))

    return _rows(name, fn, [(fr, 'r', 1, 0), (fi, 'r', 1, 0)] + [(x, 'r', S5_GROUP, 0) for x in (br, bi, dr, di)],
                 [('r', S5_GROUP, F32)] * 2 + [('r', 1, F32)] * 2, 2048)


def _scan_mults(m_ref, ar, ai, reverse):
    L = ar.shape[1]
    row = lax.broadcasted_iota(jnp.int32, (SUB, L), 0)
    if reverse:
        row = (SUB - 1) - row
    ar = jnp.broadcast_to(ar, (SUB, L))
    ai = jnp.broadcast_to(ai, (SUB, L))
    a2r, a2i = ar * ar - ai * ai, 2.0 * ar * ai
    a4r, a4i = a2r * a2r - a2i * a2i, 2.0 * a2r * a2i
    zero = jnp.zeros((SUB, L), F32)
    for s, (pr, pi, d) in enumerate(((ar, ai, 1), (a2r, a2i, 2), (a4r, a4i, 4))):
        m_ref[2 * s] = jnp.where(row >= d, pr, zero)
        m_ref[2 * s + 1] = jnp.where(row >= d, pi, zero)
    pr, pi = ar, ai
    for bit, (qr, qi) in ((1, (ar, ai)), (2, (a2r, a2i)), (4, (a4r, a4i))):
        on = (row & bit) != 0
        nr, ni = pr * qr - pi * qi, pr * qi + pi * qr
        pr, pi = jnp.where(on, nr, pr), jnp.where(on, ni, pi)
    m_ref[6] = pr
    m_ref[7] = pi


def _scan8(xr, xi, m_ref, cr, ci, reverse):
    for s, d in enumerate((1, 2, 4)):
        sh = (SUB - d) if reverse else d
        sr, si = pltpu.roll(xr, sh, 0), pltpu.roll(xi, sh, 0)
        mr, mi = m_ref[2 * s], m_ref[2 * s + 1]
        xr, xi = xr + mr * sr - mi * si, xi + mr * si + mi * sr
    pr, pi = m_ref[6], m_ref[7]
    return xr + pr * cr - pi * ci, xi + pr * ci + pi * cr


def _blockdiag_fill(bd_ref, c_ref, C, L):
    P = S5_STATE
    bd_ref[...] = jnp.zeros_like(bd_ref)
    for g in range(L // P):
        for half in (0, L):
            bd_ref[g * C:(g + 1) * C, half + g * P:half + (g + 1) * P] = c_ref[:, half + g * P:half + (g + 1) * P]


def _blockdiag_take(out_ref, dense_ref, C, L):
    P = S5_STATE
    for g in range(L // P):
        for half in (0, L):
            out_ref[:, half + g * P:half + (g + 1) * P] = dense_ref[g * C:(g + 1) * C, half + g * P:half + (g + 1) * P]


def _s5_fwd(name, proj, bbd, cbd, abar_r, abar_i, dskip, E):
    T = proj.shape[0]
    NC, C, L2 = bbd.shape
    L = L2 // 2
    CH = GROUPS_PER_CHUNK * C
    tT = _t(256, T)
    nt = (((1,), (1,)), ((), ()))

    def body(u_ref, bc_ref, cc_ref, ar_ref, ai_ref, d_ref, y_ref, g_ref, h_ref, bu, carry, mult, b_bd, c_bd):
        tb = pl.program_id(1)

        @pl.when(tb == 0)
        def _():
            carry[...] = jnp.zeros_like(carry)
            _blockdiag_fill(b_bd, bc_ref, C, L)
            _blockdiag_fill(c_bd, cc_ref, C, L)

        u = u_ref[...]
        bu[...] = jnp.dot(u.astype(BF16), b_bd[...], preferred_element_type=F32)
        _scan_mults(mult, ar_ref[...], ai_ref[...], False)

        def step(jb, c):
            cr, ci = c
            r0 = pl.multiple_of(jb * SUB, SUB)
            hr, hi = _scan8(bu[pl.ds(r0, SUB), 0:L], bu[pl.ds(r0, SUB), L:L2], mult, cr, ci, False)
            h_ref[pl.ds(r0, SUB), 0:L] = hr
            h_ref[pl.ds(r0, SUB), L:L2] = hi
            return (jnp.broadcast_to(hr[SUB - 1:SUB, :], (SUB, L)), jnp.broadcast_to(hi[SUB - 1:SUB, :], (SUB, L)))

        cr, ci = lax.fori_loop(0, tT // SUB, step, (carry[:, 0:L], carry[:, L:L2]))
        carry[:, 0:L] = cr
        carry[:, L:L2] = ci
        y1 = lax.dot_general(h_ref[...].astype(BF16), c_bd[...], nt, preferred_element_type=F32) + d_ref[...] * u
        y_ref[...] = y1
        g_ref[...] = _gelu(y1).astype(BF16)

    return pl.pallas_call(
        body, name=name, grid=(NC, T // tT),
        in_specs=[_bs((tT, CH), lambda c, t: (t, c)), _bs((None, C, L2), lambda c, t: (c, 0, 0)),
                  _bs((None, C, L2), lambda c, t: (c, 0, 0)), _bs((None, 1, L), lambda c, t: (c, 0, 0)),
                  _bs((None, 1, L), lambda c, t: (c, 0, 0)), _bs((1, CH), lambda c, t: (0, c))],
        out_specs=[_bs((tT, CH), lambda c, t: (t, c)), _bs((tT, CH), lambda c, t: (t, c)),
                   _bs((None, tT, L2), lambda c, t: (c, t, 0))],
        out_shape=[jax.ShapeDtypeStruct((T, E), F32), jax.ShapeDtypeStruct((T, E), BF16),
                   jax.ShapeDtypeStruct((NC, T, L2), F32)],
        scratch_shapes=[pltpu.VMEM((tT, L2), F32), pltpu.VMEM((SUB, L2), F32), pltpu.VMEM((8, SUB, L), F32),
                        pltpu.VMEM((CH, L2), BF16), pltpu.VMEM((CH, L2), BF16)],
        compiler_params=_params(("parallel", "arbitrary")),
    )(proj, bbd, cbd, abar_r, abar_i, dskip)


def _s5_bwd(name, dy1, proj, hs, bbd, cbd, abar_r, abar_i, dskip, E):
    T = proj.shape[0]
    NC, C, L2 = bbd.shape
    L = L2 // 2
    CH = GROUPS_PER_CHUNK * C
    tT = _t(256, T)
    nT = T // tT
    tn = (((0,), (0,)), ((), ()))
    nt = (((1,), (1,)), ((), ()))

    def body(dy_ref, u_ref, h_ref, bc_ref, cc_ref, ar_ref, ai_ref, d_ref, du_ref, db_ref, dc_ref, da_ref, dd_ref,
             gb, carry, mult, b_bd, c_bd, db_acc, dc_acc):
        tb = pl.program_id(1)

        @pl.when(tb == 0)
        def _():
            carry[...] = jnp.zeros_like(carry)
            db_acc[...] = jnp.zeros_like(db_acc)
            dc_acc[...] = jnp.zeros_like(dc_acc)
            da_ref[...] = jnp.zeros_like(da_ref)
            dd_ref[...] = jnp.zeros_like(dd_ref)
            _blockdiag_fill(b_bd, bc_ref, C, L)
            _blockdiag_fill(c_bd, cc_ref, C, L)

        dy = dy_ref[...]
        u = u_ref[...]
        dy16 = dy.astype(BF16)
        dc_acc[...] += lax.dot_general(dy16, h_ref[...].astype(BF16), tn, preferred_element_type=F32)
        gb[...] = jnp.dot(dy16, c_bd[...], preferred_element_type=F32)
        _scan_mults(mult, ar_ref[...], -ai_ref[...], True)
        row = lax.broadcasted_iota(jnp.int32, (SUB, L), 0)
        nblk = tT // SUB

        def step(jj, c):
            cr, ci, sr, si = c
            r0 = pl.multiple_of((nblk - 1 - jj) * SUB, SUB)
            gr, gi = _scan8(gb[pl.ds(r0, SUB), 0:L], gb[pl.ds(r0, SUB), L:L2], mult, cr, ci, True)
            gb[pl.ds(r0, SUB), 0:L] = gr
            gb[pl.ds(r0, SUB), L:L2] = gi
            nr = jnp.where(row == SUB - 1, cr, pltpu.roll(gr, SUB - 1, 0))
            ni = jnp.where(row == SUB - 1, ci, pltpu.roll(gi, SUB - 1, 0))
            hr, hi = h_ref[pl.ds(r0, SUB), 0:L], h_ref[pl.ds(r0, SUB), L:L2]
            sr = sr + nr * hr + ni * hi
            si = si + ni * hr - nr * hi
            return (jnp.broadcast_to(gr[0:1, :], (SUB, L)), jnp.broadcast_to(gi[0:1, :], (SUB, L)), sr, si)

        z = jnp.zeros((SUB, L), F32)
        cr, ci, sr, si = lax.fori_loop(0, nblk, step, (carry[:, 0:L], carry[:, L:L2], z, z))
        carry[:, 0:L] = cr
        carry[:, L:L2] = ci
        da_ref[:, 0:L] += sr
        da_ref[:, L:L2] += si
        g16 = gb[...].astype(BF16)
        du = lax.dot_general(g16, b_bd[...], nt, preferred_element_type=F32) + d_ref[...] * dy
        du_ref[...] = du.astype(BF16)
        db_acc[...] += lax.dot_general(u.astype(BF16), g16, tn, preferred_element_type=F32)
        dd_ref[...] += _colsum(dy * u)

        @pl.when(tb == nT - 1)
        def _():
            _blockdiag_take(db_ref, db_acc, C, L)
            _blockdiag_take(dc_ref, dc_acc, C, L)

    rev = lambda c, t: (nT - 1 - t, c)
    return pl.pallas_call(
        body, name=name, grid=(NC, nT),
        in_specs=[_bs((tT, CH), rev), _bs((tT, CH), rev), _bs((None, tT, L2), lambda c, t: (c, nT - 1 - t, 0)),
                  _bs((None, C, L2), lambda c, t: (c, 0, 0)), _bs((None, C, L2), lambda c, t: (c, 0, 0)),
                  _bs((None, 1, L), lambda c, t: (c, 0, 0)), _bs((None, 1, L), lambda c, t: (c, 0, 0)),
                  _bs((1, CH), lambda c, t: (0, c))],
        out_specs=[_bs((tT, CH), rev), _bs((None, C, L2), lambda c, t: (c, 0, 0)), _bs((None, C, L2), lambda c, t: (c, 0, 0)),
                   _bs((None, SUB, L2), lambda c, t: (c, 0, 0)), _bs((None, 1, CH), lambda c, t: (c, 0, 0))],
        out_shape=[jax.ShapeDtypeStruct((T, E), BF16), jax.ShapeDtypeStruct((NC, C, L2), F32),
                   jax.ShapeDtypeStruct((NC, C, L2), F32), jax.ShapeDtypeStruct((NC, SUB, L2), F32),
                   jax.ShapeDtypeStruct((NC, 1, CH), F32)],
        scratch_shapes=[pltpu.VMEM((tT, L2), F32), pltpu.VMEM((SUB, L2), F32), pltpu.VMEM((8, SUB, L), F32),
                        pltpu.VMEM((CH, L2), BF16), pltpu.VMEM((CH, L2), BF16), pltpu.VMEM((CH, L2), F32), pltpu.VMEM((CH, L2), F32)],
        compiler_params=_params(("parallel", "arbitrary")),
    )(dy1, proj, hs, bbd, cbd, abar_r, abar_i, dskip)


def _compact(v, NC):
    G, P, C = v.shape
    return jnp.transpose(v.reshape(NC, G // NC, P, C), (0, 3, 1, 2)).reshape(NC, C, (G // NC) * P)


def _uncompact(d, G):
    NC, C, L = d.shape
    gpc = G // NC
    return jnp.transpose(d.reshape(NC, C, gpc, L // gpc), (0, 2, 3, 1)).reshape(G, L // gpc, C)


def _cum_rows(name, x, bias, reverse, log_sig):
    T, L = x.shape

    def body(x_ref, b_ref, o_ref):
        row = lax.broadcasted_iota(jnp.int32, (SUB, L), 0)
        if reverse:
            row = (SUB - 1) - row
        nblk = T // SUB

        def step(jj, c):
            r0 = pl.multiple_of(((nblk - 1 - jj) if reverse else jj) * SUB, SUB)
            v = x_ref[pl.ds(r0, SUB), :] + b_ref[...]
            if log_sig:
                v = _log_sigmoid(v)
            for d in (1, 2, 4):
                v = v + jnp.where(row >= d, pltpu.roll(v, (SUB - d) if reverse else d, 0), 0.0)
            v = v + c
            o_ref[pl.ds(r0, SUB), :] = v
            e = 0 if reverse else SUB - 1
            return jnp.broadcast_to(v[e:e + 1, :], (SUB, L))

        lax.fori_loop(0, nblk, step, jnp.zeros((SUB, L), F32))

    return pl.pallas_call(body, name=name, out_shape=jax.ShapeDtypeStruct((T, L), F32),
                          compiler_params=pltpu.CompilerParams(vmem_limit_bytes=VMEM_LIMIT))(x, bias)


def _qk_norm(name, proj, wq, wk, H):
    T = proj.shape[0]
    Dh = FOX_HEAD_DIM
    tT = _t(512, T)

    def body(q_ref, k_ref, wq_ref, wk_ref, qn_ref, kn_ref):
        q, k = q_ref[...], k_ref[...]
        qn_ref[...] = ((q * _rms(q)) * wq_ref[...]).astype(BF16)
        kn_ref[...] = ((k * _rms(k)) * wk_ref[...]).astype(BF16)

    blk = lambda off: _bs((tT, Dh), lambda t, h: (t, h + off))
    return pl.pallas_call(
        body, name=name, grid=(T // tT, H),
        in_specs=[blk(0), blk(H), _bs((1, Dh), lambda t, h: (0, 0)), _bs((1, Dh), lambda t, h: (0, 0))],
        out_specs=[blk(0), blk(0)], out_shape=[jax.ShapeDtypeStruct((T, H * Dh), BF16)] * 2,
        compiler_params=_params(("parallel", "parallel")))(proj, proj, wq, wk)


def _qk_norm_bwd(name, proj, wq, wk, dqn, dkn, H):
    T = proj.shape[0]
    Dh = FOX_HEAD_DIM
    tT = _t(512, T)

    def body(q_ref, k_ref, wq_ref, wk_ref, dqn_ref, dkn_ref, dq_ref, dk_ref, dwq_ref, dwk_ref):
        @pl.when((pl.program_id(0) == 0) & (pl.program_id(1) == 0))
        def _():
            dwq_ref[...] = jnp.zeros_like(dwq_ref)
            dwk_ref[...] = jnp.zeros_like(dwk_ref)

        dq, tq = _rms_bwd(q_ref[...], wq_ref[...], dqn_ref[...])
        dk, tk = _rms_bwd(k_ref[...], wk_ref[...], dkn_ref[...])
        dq_ref[...] = dq.astype(BF16)
        dk_ref[...] = dk.astype(BF16)
        dwq_ref[...] += _colsum(tq)
        dwk_ref[...] += _colsum(tk)

    blk = lambda off: _bs((tT, Dh), lambda t, h: (t, h + off))
    one = _bs((1, Dh), lambda t, h: (0, 0))
    return pl.pallas_call(
        body, name=name, grid=(T // tT, H),
        in_specs=[blk(0), blk(H), one, one, blk(0), blk(0)],
        out_specs=[blk(0), blk(0), one, one],
        out_shape=[jax.ShapeDtypeStruct((T, H * Dh), BF16)] * 2 + [jax.ShapeDtypeStruct((1, Dh), F32)] * 2,
        compiler_params=_params(("arbitrary", "arbitrary")))(proj, proj, wq, wk, dqn, dkn)


def _attn_fwd(name, qn, kn, proj, cum_q, cum_k, H):
    T = qn.shape[0]
    Dh = FOX_HEAD_DIM
    tq = cum_k.shape[3]
    nq = T // tq
    scale = Dh ** -0.5
    nt = (((1,), (1,)), ((), ()))

    sq = _t(ATTN_SUB, tq)
    rep = tq // LANES

    def body(q_ref, k_ref, v_ref, cq_ref, ck_ref, o_ref, lse_ref, m_sc, l_sc, acc_sc):
        i = pl.program_id(1)
        m_sc[...] = jnp.full_like(m_sc, NEG)
        l_sc[...] = jnp.zeros_like(l_sc)
        acc_sc[...] = jnp.zeros_like(acc_sc)
        kloc = lax.broadcasted_iota(jnp.int32, (sq, tq), 1)
        qloc = lax.broadcasted_iota(jnp.int32, (sq, tq), 0)

        def chunk(kc, masked):
            ks = pl.multiple_of(kc * tq, tq)
            k = k_ref[pl.ds(ks, tq), :]
            v16 = v_ref[pl.ds(ks, tq), :].astype(BF16)
            ck = ck_ref[kc]
            for r in range(tq // sq):
                rows = pl.ds(r * sq, sq)
                s = lax.dot_general(q_ref[rows, :], k, nt, preferred_element_type=F32) * scale + (jnp.tile(cq_ref[rows, :], (1, rep)) - ck)
                if masked:
                    s = jnp.where(kloc <= qloc + r * sq, s, NEG)
                m_old = m_sc[rows, :]
                m_new = jnp.maximum(m_old, jnp.max(s, axis=1, keepdims=True))
                alpha = jnp.exp(m_old - m_new)
                p = jnp.exp(s - jnp.tile(m_new, (1, rep)))
                l_sc[rows, :] = alpha * l_sc[rows, :] + jnp.sum(p, axis=1, keepdims=True)
                acc_sc[rows, :] = alpha * acc_sc[rows, :] + jnp.dot(p.astype(BF16), v16, preferred_element_type=F32)
                m_sc[rows, :] = m_new

        def below(kc, c):
            chunk(kc, False)
            return c

        lax.fori_loop(0, i, below, 0)
        chunk(i, True)
        o_ref[...] = acc_sc[...] / l_sc[...]
        lse_ref[...] = m_sc[...] + jnp.log(l_sc[...])

    return pl.pallas_call(
        body, name=name, grid=(H, nq),
        in_specs=[_bs((tq, Dh), lambda h, i: (i, h)), _bs((T, Dh), lambda h, i: (0, h)), _bs((T, Dh), lambda h, i: (0, 2 * H + h)),
                  _bs((None, tq, LANES), lambda h, i: (h, i, 0)), _bs((None, nq, 1, tq), lambda h, i: (h, 0, 0, 0))],
        out_specs=[_bs((tq, Dh), lambda h, i: (i, h)), _bs((None, tq, LANES), lambda h, i: (h, i, 0))],
        out_shape=[jax.ShapeDtypeStruct((T, H * Dh), F32), jax.ShapeDtypeStruct((H, T, LANES), F32)],
        scratch_shapes=[pltpu.VMEM((tq, LANES), F32), pltpu.VMEM((tq, LANES), F32), pltpu.VMEM((tq, Dh), F32)],
        compiler_params=_params(("parallel", "parallel")))(qn, kn, proj, cum_q, cum_k)


def _attn_bwd(name, qn, kn, proj, do, o, lse, cum_q, cum_k, H):
    T = qn.shape[0]
    Dh = FOX_HEAD_DIM
    tq = cum_k.shape[3]
    nq = T // tq
    scale = Dh ** -0.5
    nt = (((1,), (1,)), ((), ()))
    tn = (((0,), (0,)), ((), ()))
    assert H <= LANES

    sq = _t(ATTN_SUB, tq)
    rep = tq // LANES

    def body(q_ref, k_ref, v_ref, do_ref, o_ref, lse_ref, cq_ref, ck_ref, dq_ref, dk_ref, dv_ref, dcq_ref, dck_ref,
             delta, cql, dk_sc, dv_sc, dck_sc):
        h, j = pl.program_id(0), pl.program_id(1)

        @pl.when((h == 0) & (j == 0))
        def _():
            dcq_ref[...] = jnp.zeros_like(dcq_ref)

        @pl.when(j == 0)
        def _():
            dq_ref[...] = jnp.zeros_like(dq_ref)
            delta[...] = jnp.broadcast_to(jnp.sum(do_ref[...] * o_ref[...], axis=1, keepdims=True), delta.shape)
            cql[...] = cq_ref[...] - lse_ref[...]

        head_lane = lax.broadcasted_iota(jnp.int32, (sq, LANES), 1) == h

        dk_sc[...] = jnp.zeros_like(dk_sc)
        dv_sc[...] = jnp.zeros_like(dv_sc)
        dck_sc[...] = jnp.zeros_like(dck_sc)
        k = k_ref[...]
        v16 = v_ref[...].astype(BF16)
        ck = ck_ref[...]
        kloc = lax.broadcasted_iota(jnp.int32, (sq, tq), 1)
        qloc = lax.broadcasted_iota(jnp.int32, (sq, tq), 0)

        def qblk(i, masked):
            for r in range(tq // sq):
                rows = pl.ds(pl.multiple_of(i * tq + r * sq, sq), sq)
                q = q_ref[rows, :]
                do16 = do_ref[rows, :].astype(BF16)
                e = lax.dot_general(q, k, nt, preferred_element_type=F32) * scale + (jnp.tile(cql[rows, :], (1, rep)) - ck)
                p = jnp.exp(e)
                if masked:
                    p = jnp.where(kloc <= qloc + r * sq, p, 0.0)
                dv_sc[...] += lax.dot_general(p.astype(BF16), do16, tn, preferred_element_type=F32)
                dp = lax.dot_general(do16, v16, nt, preferred_element_type=F32)
                ds = p * (dp - jnp.tile(delta[rows, :], (1, rep)))
                ds16 = ds.astype(BF16)
                dk_sc[...] += lax.dot_general(ds16, q, tn, preferred_element_type=F32)
                dq_ref[rows, :] += jnp.dot(ds16, k, preferred_element_type=F32) * scale
                dcq_ref[rows, :] += jnp.where(head_lane, jnp.sum(ds, axis=1, keepdims=True), 0.0)
                dck_sc[...] += jnp.sum(ds, axis=0, keepdims=True)

        def above(i, c):
            qblk(i, False)
            return c

        qblk(j, True)
        lax.fori_loop(j + 1, nq, above, 0)
        dk_ref[...] = dk_sc[...] * scale
        dv_ref[...] = dv_sc[...].astype(BF16)
        dck_ref[...] = -dck_sc[...]

    whole = lambda off: _bs((T, Dh), lambda h, j: (0, h + off))
    blk = lambda off: _bs((tq, Dh), lambda h, j: (j, h + off))
    return pl.pallas_call(
        body, name=name, grid=(H, nq),
        in_specs=[whole(0), blk(0), blk(2 * H), whole(0), whole(0), _bs((None, T, LANES), lambda h, j: (h, 0, 0)),
                  _bs((None, T, LANES), lambda h, j: (h, 0, 0)), _bs((None, None, 1, tq), lambda h, j: (h, j, 0, 0))],
        out_specs=[whole(0), blk(0), blk(0), _bs((T, LANES), lambda h, j: (0, 0)),
                   _bs((None, None, 1, tq), lambda h, j: (h, j, 0, 0))],
        out_shape=[jax.ShapeDtypeStruct((T, H * Dh), F32), jax.ShapeDtypeStruct((T, H * Dh), F32), jax.ShapeDtypeStruct((T, H * Dh), BF16),
                   jax.ShapeDtypeStruct((T, LANES), F32), jax.ShapeDtypeStruct((H, nq, 1, tq), F32)],
        scratch_shapes=[pltpu.VMEM((T, LANES), F32), pltpu.VMEM((T, LANES), F32), pltpu.VMEM((tq, Dh), F32), pltpu.VMEM((tq, Dh), F32),
                        pltpu.VMEM((1, tq), F32)],
        compiler_params=_params(("arbitrary", "arbitrary")))(qn, kn, proj, do, o, lse, cum_q, cum_k)


def _pool_fwd(name, proj, E):
    T = proj.shape[0]
    PG = len(POOL_WINDOWS)
    PD = E // PG
    tT = _t(256, T)
    hb = tT // POOL_HALO

    def body(u_ref, halo_ref, o_ref, buf):
        g, tb = pl.program_id(0), pl.program_id(1)
        u = u_ref[...]
        buf[pl.ds(POOL_HALO, tT), :] = u
        buf[pl.ds(0, POOL_HALO), :] = jnp.where(tb == 0, 0.0, halo_ref[...])
        t = tb * tT + lax.broadcasted_iota(jnp.int32, (tT, 1), 0)
        for gi, w in enumerate(POOL_WINDOWS):
            @pl.when(g == gi)
            def _():
                acc = u
                for d in range(1, w):
                    acc = acc + buf[pl.ds(POOL_HALO - d, tT), :]
                cnt = jnp.minimum(t + 1, w).astype(F32)
                o_ref[...] = (acc / cnt - u).astype(BF16)

    return pl.pallas_call(
        body, name=name, grid=(PG, T // tT),
        in_specs=[_bs((tT, PD), lambda g, t: (t, g)), _bs((POOL_HALO, PD), lambda g, t: (jnp.maximum(t * hb - 1, 0), g))],
        out_specs=_bs((tT, PD), lambda g, t: (t, g)), out_shape=jax.ShapeDtypeStruct((T, E), BF16),
        scratch_shapes=[pltpu.VMEM((tT + POOL_HALO, PD), F32)],
        compiler_params=_params(("parallel", "parallel")))(proj, proj)


def _pool_bwd(name, dpm, E):
    T = dpm.shape[0]
    PG = len(POOL_WINDOWS)
    PD = E // PG
    tT = _t(256, T)
    hb = tT // POOL_HALO
    nT = T // tT

    def body(d_ref, halo_ref, o_ref, buf):
        g, tb = pl.program_id(0), pl.program_id(1)
        d = d_ref[...]
        t = tb * tT + lax.broadcasted_iota(jnp.int32, (tT, 1), 0)
        th = (tb + 1) * tT + lax.broadcasted_iota(jnp.int32, (POOL_HALO, 1), 0)
        for gi, w in enumerate(POOL_WINDOWS):
            @pl.when(g == gi)
            def _():
                dn = d / jnp.minimum(t + 1, w).astype(F32)
                buf[pl.ds(0, tT), :] = dn
                buf[pl.ds(tT, POOL_HALO), :] = jnp.where(tb == nT - 1, 0.0, halo_ref[...] / jnp.minimum(th + 1, w).astype(F32))
                acc = dn
                for s in range(1, w):
                    acc = acc + buf[pl.ds(s, tT), :]
                o_ref[...] = (acc - d).astype(BF16)

    return pl.pallas_call(
        body, name=name, grid=(PG, nT),
        in_specs=[_bs((tT, PD), lambda g, t: (t, g)), _bs((POOL_HALO, PD), lambda g, t: (jnp.minimum((t + 1) * hb, T // POOL_HALO - 1), g))],
        out_specs=_bs((tT, PD), lambda g, t: (t, g)), out_shape=jax.ShapeDtypeStruct((T, E), BF16),
        scratch_shapes=[pltpu.VMEM((tT + POOL_HALO, PD), F32)],
        compiler_params=_params(("parallel", "parallel")))(dpm, dpm)


def _coords():
    x, y, c = lax.axis_index("x"), lax.axis_index("y"), lax.axis_index("c")
    chips = [(1 - x, y), (x, 1 - y), (1 - x, 1 - y)]
    return x, y, c, 2 * x + y, (x, y, 1 - c), chips


def _chip_allgather(name, bufs):
    n = len(bufs)

    def body(*refs):
        outs = refs[n:2 * n]
        send, recv, fsend, frecv = refs[2 * n:]
        x, y, c, p, sib, chips = _coords()

        def direct(t, j, chip):
            return pltpu.make_async_remote_copy(src_ref=outs[t].at[p, c], dst_ref=outs[t].at[p, c], send_sem=send.at[t, j],
                                                recv_sem=recv.at[t, j], device_id=(*chip, c), device_id_type=MESH)

        def landed(t, j, chip):
            blk = outs[t].at[2 * chip[0] + chip[1], c]
            return pltpu.make_async_remote_copy(src_ref=blk, dst_ref=blk, send_sem=send.at[t, j],
                                                recv_sem=recv.at[t, j], device_id=(*chip, c), device_id_type=MESH)

        def passed(t, j, chip, half):
            blk = outs[t].at[2 * chip[0] + chip[1], half]
            return pltpu.make_async_remote_copy(src_ref=blk, dst_ref=blk, send_sem=fsend.at[t, j], recv_sem=frecv.at[t, j],
                                                device_id=sib, device_id_type=MESH)

        first = [direct(t, j, chip) for t in range(n) for j, chip in enumerate(chips)]
        for cp in first:
            cp.start()
        fwd = []
        for j, chip in enumerate(chips):
            for t in range(n):
                landed(t, j, chip).wait_recv()
                f = passed(t, j, chip, c)
                f.start()
                fwd.append(f)
        for j, chip in enumerate(chips):
            for t in range(n):
                passed(t, j, chip, 1 - c).wait_recv()
        for cp in first + fwd:
            cp.wait_send()

    return pl.pallas_call(
        body, name=name, in_specs=[ANY] * n, out_specs=[ANY] * n,
        out_shape=[jax.ShapeDtypeStruct(a.shape, a.dtype) for a in bufs],
        input_output_aliases={t: t for t in range(n)},
        scratch_shapes=[pltpu.SemaphoreType.DMA((n, 3))] * 4,
    )(*bufs)


SEM = pl.BlockSpec(memory_space=pltpu.SEMAPHORE)
TOKEN = jax.ShapeDtypeStruct((SUB, LANES), F32)


def _split_params():
    return pltpu.CompilerParams(has_side_effects=pltpu.SideEffectType.DATAFLOW_SIDE_EFFECTING)


def _struct(a):
    return jax.ShapeDtypeStruct(a.shape, a.dtype)


def _gather_start(name, bufs, deps):
    n, nd = len(bufs), len(deps)

    def body(*refs):
        outs = refs[n + nd:2 * n + nd]
        send, recv, token = refs[2 * n + nd:]
        x, y, c, p, sib, chips = _coords()
        for t in range(n):
            for j, chip in enumerate(chips):
                pltpu.make_async_remote_copy(src_ref=outs[t].at[p, c], dst_ref=outs[t].at[p, c], send_sem=send.at[3 * t + j],
                                             recv_sem=recv.at[3 * t + j], device_id=(*chip, c), device_id_type=MESH).start()
        token[...] = jnp.zeros_like(token)

    res = pl.pallas_call(
        body, name=name, in_specs=[ANY] * (n + nd), out_specs=[ANY] * n + [SEM, SEM, pl.BlockSpec(memory_space=pltpu.VMEM)],
        out_shape=[_struct(a) for a in bufs] + [pltpu.SemaphoreType.DMA((3 * n,)), pltpu.SemaphoreType.DMA((3 * n,)), TOKEN],
        input_output_aliases={t: t for t in range(n)}, compiler_params=_split_params(),
    )(*bufs, *deps)
    return list(res[:n]), res[n], res[n + 1], res[n + 2]


def _gather_wait(name, bufs, send, recv, after):
    n = len(bufs)

    def body(*refs):
        send_r, recv_r = refs[n], refs[n + 1]
        outs = refs[n + 3:2 * n + 3]
        x, y, c, p, sib, chips = _coords()
        for t in range(n):
            for j, chip in enumerate(chips):
                cp = pltpu.make_async_remote_copy(src_ref=outs[t].at[p, c], dst_ref=outs[t].at[2 * chip[0] + chip[1], c], send_sem=send_r.at[3 * t + j],
                                                  recv_sem=recv_r.at[3 * t + j], device_id=(*chip, c), device_id_type=MESH)
                cp.wait_send()
                cp.wait_recv()

    return list(pl.pallas_call(
        body, name=name, in_specs=[ANY] * n + [SEM, SEM, ANY], out_specs=[ANY] * n, out_shape=[_struct(a) for a in bufs],
        input_output_aliases={t: t for t in range(n)}, compiler_params=_split_params(),
    )(*bufs, send, recv, after))


def _gather_forward(name, bufs):
    n = len(bufs)

    def body(*refs):
        outs = refs[n:2 * n]
        fsend, frecv = refs[2 * n:]
        x, y, c, p, sib, chips = _coords()

        def passed(t, j, chip, half):
            blk = outs[t].at[2 * chip[0] + chip[1], half]
            return pltpu.make_async_remote_copy(src_ref=blk, dst_ref=blk, send_sem=fsend.at[t, j], recv_sem=frecv.at[t, j],
                                                device_id=sib, device_id_type=MESH)

        fwd = [passed(t, j, chip, c) for t in range(n) for j, chip in enumerate(chips)]
        for cp in fwd:
            cp.start()
        for t in range(n):
            for j, chip in enumerate(chips):
                passed(t, j, chip, 1 - c).wait_recv()
        for cp in fwd:
            cp.wait_send()

    return list(pl.pallas_call(
        body, name=name, in_specs=[ANY] * n, out_specs=[ANY] * n, out_shape=[_struct(a) for a in bufs],
        input_output_aliases={t: t for t in range(n)}, scratch_shapes=[pltpu.SemaphoreType.DMA((n, 3))] * 2,
    )(*bufs))


def _chip_exchange_start(name, sums):
    n = len(sums)
    lands = [lax.empty((3,) + a.shape[1:], a.dtype) for a in sums]

    def body(*refs):
        src, dst = refs[2 * n:3 * n], refs[3 * n:4 * n]
        send, recv, token = refs[4 * n:]
        x, y, c, p, sib, chips = _coords()
        for t in range(n):
            for j, chip in enumerate(chips):
                pltpu.make_async_remote_copy(src_ref=src[t].at[2 * chip[0] + chip[1]], dst_ref=dst[t].at[j], send_sem=send.at[3 * t + j],
                                             recv_sem=recv.at[3 * t + j], device_id=(*chip, c), device_id_type=MESH).start()
        token[...] = jnp.zeros_like(token)

    res = pl.pallas_call(
        body, name=name, in_specs=[ANY] * (2 * n), out_specs=[ANY] * (2 * n) + [SEM, SEM, pl.BlockSpec(memory_space=pltpu.VMEM)],
        out_shape=[_struct(a) for a in sums + lands] + [pltpu.SemaphoreType.DMA((3 * n,)), pltpu.SemaphoreType.DMA((3 * n,)), TOKEN],
        input_output_aliases={t: t for t in range(2 * n)}, compiler_params=_split_params(),
    )(*sums, *lands)
    return list(res[:n]), list(res[n:2 * n]), res[2 * n], res[2 * n + 1], res[2 * n + 2]


def _chip_exchange_wait(name, sums, lands, send, recv, after):
    n = len(sums)

    def body(*refs):
        send_r, recv_r = refs[2 * n], refs[2 * n + 1]
        src, dst = refs[2 * n + 3:3 * n + 3], refs[3 * n + 3:4 * n + 3]
        x, y, c, p, sib, chips = _coords()
        for t in range(n):
            for j, chip in enumerate(chips):
                cp = pltpu.make_async_remote_copy(src_ref=src[t].at[2 * chip[0] + chip[1]], dst_ref=dst[t].at[j], send_sem=send_r.at[3 * t + j],
                                                  recv_sem=recv_r.at[3 * t + j], device_id=(*chip, c), device_id_type=MESH)
                cp.wait_send()
                cp.wait_recv()

    res = pl.pallas_call(
        body, name=name, in_specs=[ANY] * (2 * n) + [SEM, SEM, ANY], out_specs=[ANY] * (2 * n),
        out_shape=[_struct(a) for a in sums + lands], input_output_aliases={t: t for t in range(2 * n)},
        compiler_params=_split_params(),
    )(*sums, *lands, send, recv, after)
    return list(res[:n]), list(res[n:])


def _pair_exchange(name, parts):
    n = len(parts)

    def body(*refs):
        ins, outs = refs[:n], refs[n:2 * n]
        send, recv = refs[2 * n:]
        x, y, c, p, sib, chips = _coords()
        cps = [pltpu.make_async_remote_copy(src_ref=ins[t].at[1 - c], dst_ref=outs[t], send_sem=send.at[t], recv_sem=recv.at[t],
                                            device_id=sib, device_id_type=MESH) for t in range(n)]
        for cp in cps:
            cp.start()
        for cp in cps:
            cp.wait()

    return pl.pallas_call(
        body, name=name, in_specs=[ANY] * n, out_specs=[ANY] * n,
        out_shape=[jax.ShapeDtypeStruct(a.shape[1:], a.dtype) for a in parts],
        scratch_shapes=[pltpu.SemaphoreType.DMA((n,))] * 2,
    )(*parts)


def _chip_exchange(name, sums):
    n = len(sums)

    def body(*refs):
        ins, outs = refs[:n], refs[n:2 * n]
        send, recv = refs[2 * n:]
        x, y, c, p, sib, chips = _coords()
        cps = [pltpu.make_async_remote_copy(src_ref=ins[t].at[2 * chip[0] + chip[1]], dst_ref=outs[t].at[j], send_sem=send.at[t, j],
                                            recv_sem=recv.at[t, j], device_id=(*chip, c), device_id_type=MESH)
               for t in range(n) for j, chip in enumerate(chips)]
        for cp in cps:
            cp.start()
        for cp in cps:
            cp.wait()

    return pl.pallas_call(
        body, name=name, in_specs=[ANY] * n, out_specs=[ANY] * n,
        out_shape=[jax.ShapeDtypeStruct((3,) + a.shape[1:], a.dtype) for a in sums],
        scratch_shapes=[pltpu.SemaphoreType.DMA((n, 3))] * 2,
    )(*sums)


def _pair_share(name, bufs, items, deps=()):
    n = len(items)
    nb = len(bufs)
    nd = len(deps)

    def body(*refs):
        outs = refs[nb + nd:2 * nb + nd]
        send, recv = refs[2 * nb + nd:]
        x, y, c, p, sib, chips = _coords()

        def blk(t, half):
            o, lead = items[t]
            return outs[o].at[p if lead == 'chip' else lead, half]

        def swap(t, half):
            return pltpu.make_async_remote_copy(src_ref=blk(t, half), dst_ref=blk(t, half), send_sem=send.at[t], recv_sem=recv.at[t],
                                                device_id=sib, device_id_type=MESH)

        cps = [swap(t, c) for t in range(n)]
        for cp in cps:
            cp.start()
        for t in range(n):
            swap(t, 1 - c).wait_recv()
        for cp in cps:
            cp.wait_send()

    return list(pl.pallas_call(
        body, name=name, in_specs=[ANY] * (nb + nd), out_specs=[ANY] * nb,
        out_shape=[jax.ShapeDtypeStruct(b.shape, b.dtype) for b in bufs],
        input_output_aliases={t: t for t in range(nb)},
        scratch_shapes=[pltpu.SemaphoreType.DMA((n,))] * 2,
    )(*bufs, *deps))


def _flat2(a, lead):
    return a.reshape(a.shape[:lead] + (-1, a.shape[-1]))


def _reduce_begin(tag, parts):
    c = lax.axis_index("c").astype(jnp.int32)
    got = _pair_exchange(f"rs_pair_exchange_{tag}", parts)
    sums = []
    for t, (mine, theirs) in enumerate(zip(parts, got)):
        m3, t2 = _flat2(mine, 1), theirs.reshape(-1, theirs.shape[-1])
        m3 = m3.reshape(2, -1, m3.shape[-1])
        cols = t2.shape[1]
        s = _rows(f"rs_pair_sum_{tag}_{t}", lambda a, b: (a.astype(F32) + b.astype(F32),),
                  [(m3, 's', cols, 0), (t2, 'r', cols, 0)], [('r', cols, BF16)], 512, pre=c.reshape(1))[0]
        sums.append(s.reshape(theirs.shape))
    sums, lands, send, recv, token = _chip_exchange_start(f"rs_chip_start_{tag}", sums)
    return (sums, lands, send, recv), token


def _reduce_end(tag, state, after, dests, bufs, buf_shapes):
    c = lax.axis_index("c").astype(jnp.int32)
    p = (2 * lax.axis_index("x") + lax.axis_index("y")).astype(jnp.int32)
    sums, lands = _chip_exchange_wait(f"rs_chip_wait_{tag}", *state, after)
    for t, (mine, theirs) in enumerate(zip(sums, lands)):
        o, lead = dests[t]
        shape = buf_shapes[o]
        rows, cols = shape[2], shape[3]
        m3, t3 = mine.reshape(N_CHIPS, rows, cols), theirs.reshape(3, rows, cols)
        pre = jnp.stack([p, jnp.int32(0), jnp.int32(1), jnp.int32(2), c, p if lead == 'chip' else jnp.int32(lead)])
        out = ('x', shape, F32, (None, None, 'tr', cols), lambda r, pr: (pr[5], pr[4], r, 0))
        bufs[o] = _rows(f"rs_chip_sum_{tag}_{t}", lambda a, b0, b1, b2: (((a.astype(F32) + b0.astype(F32)) + b1.astype(F32)) + b2.astype(F32),),
                        [(m3, 's', cols, 0), (t3, 's', cols, 1), (t3, 's', cols, 2), (t3, 's', cols, 3)], [out], 512, pre=pre, into=bufs[o])[0]


def kernel(x, norm_w, out_proj, s5_in_proj, s5_a_re, s5_a_im, s5_log_dt, s5_b_re, s5_b_im, s5_c_re, s5_c_im, s5_d, s5_w_glu, s5_b_glu, fox_in_proj, fox_q_norm, fox_k_norm, fox_f_bias, pool_in_proj, pool_w_group, pool_scale, loss_target, m_norm_w, m_out_proj, m_s5_in_proj, m_s5_a_re, m_s5_a_im, m_s5_log_dt, m_s5_b_re, m_s5_b_im, m_s5_c_re, m_s5_c_im, m_s5_d, m_s5_w_glu, m_s5_b_glu, m_fox_in_proj, m_fox_q_norm, m_fox_k_norm, m_fox_f_bias, m_pool_in_proj, m_pool_w_group, m_pool_scale, v_norm_w, v_out_proj, v_s5_in_proj, v_s5_a_re, v_s5_a_im, v_s5_log_dt, v_s5_b_re, v_s5_b_im, v_s5_c_re, v_s5_c_im, v_s5_d, v_s5_w_glu, v_s5_b_glu, v_fox_in_proj, v_fox_q_norm, v_fox_k_norm, v_fox_f_bias, v_pool_in_proj, v_pool_w_group, v_pool_scale):
    weights = dict(norm_w=norm_w, out_proj=out_proj, s5_in_proj=s5_in_proj, s5_a_re=s5_a_re, s5_a_im=s5_a_im, s5_log_dt=s5_log_dt,
                   s5_b_re=s5_b_re, s5_b_im=s5_b_im, s5_c_re=s5_c_re, s5_c_im=s5_c_im, s5_d=s5_d, s5_w_glu=s5_w_glu, s5_b_glu=s5_b_glu,
                   fox_in_proj=fox_in_proj, fox_q_norm=fox_q_norm, fox_k_norm=fox_k_norm, fox_f_bias=fox_f_bias,
                   pool_in_proj=pool_in_proj, pool_w_group=pool_w_group, pool_scale=pool_scale)
    mom_m = dict(norm_w=m_norm_w, out_proj=m_out_proj, s5_in_proj=m_s5_in_proj, s5_a_re=m_s5_a_re, s5_a_im=m_s5_a_im, s5_log_dt=m_s5_log_dt,
                 s5_b_re=m_s5_b_re, s5_b_im=m_s5_b_im, s5_c_re=m_s5_c_re, s5_c_im=m_s5_c_im, s5_d=m_s5_d, s5_w_glu=m_s5_w_glu, s5_b_glu=m_s5_b_glu,
                 fox_in_proj=m_fox_in_proj, fox_q_norm=m_fox_q_norm, fox_k_norm=m_fox_k_norm, fox_f_bias=m_fox_f_bias,
                 pool_in_proj=m_pool_in_proj, pool_w_group=m_pool_w_group, pool_scale=m_pool_scale)
    mom_v = dict(norm_w=v_norm_w, out_proj=v_out_proj, s5_in_proj=v_s5_in_proj, s5_a_re=v_s5_a_re, s5_a_im=v_s5_a_im, s5_log_dt=v_s5_log_dt,
                 s5_b_re=v_s5_b_re, s5_b_im=v_s5_b_im, s5_c_re=v_s5_c_re, s5_c_im=v_s5_c_im, s5_d=v_s5_d, s5_w_glu=v_s5_w_glu, s5_b_glu=v_s5_b_glu,
                 fox_in_proj=v_fox_in_proj, fox_q_norm=v_fox_q_norm, fox_k_norm=v_fox_k_norm, fox_f_bias=v_fox_f_bias,
                 pool_in_proj=v_pool_in_proj, pool_w_group=v_pool_w_group, pool_scale=v_pool_scale)
    return _step(x, loss_target, weights, mom_m, mom_v)


BIG = ('out_proj', 's5_in_proj', 's5_w_glu', 'fox_in_proj', 'pool_in_proj', 'pool_w_group')
SMALL = ('norm_w', 's5_a_re', 's5_a_im', 's5_log_dt', 's5_b_re', 's5_b_im', 's5_c_re', 's5_c_im', 's5_d', 's5_b_glu',
         'fox_q_norm', 'fox_k_norm', 'fox_f_bias', 'pool_scale')
SMALL_SHARDED = ('s5_d', 's5_b_glu', 'pool_scale')
GROUP_AXIS_1 = ('s5_a_re', 's5_a_im', 's5_b_re', 's5_b_im', 's5_c_re', 's5_c_im')
ORDER = ('norm_w', 'out_proj', 's5_in_proj', 's5_a_re', 's5_a_im', 's5_log_dt', 's5_b_re', 's5_b_im', 's5_c_re', 's5_c_im', 's5_d',
         's5_w_glu', 's5_b_glu', 'fox_in_proj', 'fox_q_norm', 'fox_k_norm', 'fox_f_bias', 'pool_in_proj', 'pool_w_group', 'pool_scale')


def _split2(shape):
    if shape[0] % 2 == 0:
        return (2, shape[0] // 2) + tuple(shape[1:])
    assert shape[0] == 1 and shape[1] % 2 == 0
    return (2, shape[1] // 2) + tuple(shape[2:])


def _adamw_big(n, w, grads, mom_m, mom_v, delta, new_m, new_v):
    shape = w[n].shape
    if shape[-1] % LANES:
        f2 = lambda a: jnp.transpose(a.reshape(-1, shape[-1]))
        b2 = lambda a: jnp.transpose(a).reshape(shape)
    else:
        f2 = lambda a: a.reshape(-1, shape[-1])
        b2 = lambda a: a.reshape(shape)
    d_, m_, v_ = _adamw(f"adamw_{n}", f2(w[n]), f2(grads[n]), f2(mom_m[n]), f2(mom_v[n]))
    delta[n], new_m[n], new_v[n] = b2(d_), b2(m_), b2(v_)
    return d_


def _cast_weights(w):
    p = (2 * lax.axis_index("x") + lax.axis_index("y")).astype(jnp.int32)
    bufs = {}
    for n in BIG:
        a3 = w[n].reshape(w[n].shape[0], -1, w[n].shape[-1])
        layers, rows, cols = a3.shape
        for l in range(layers):
            out = ('x', (N_CHIPS, rows, cols), BF16, (None, 'tr', cols), lambda r, pr: (pr[0], r, 0))
            b = _rows(f"cast_{n}_{l}", lambda v: (v,), [(a3, 's', cols, 1)], [out], 256, pre=jnp.stack([p, jnp.int32(l)]))[0]
            bufs[(n, l)] = b.reshape(N_CHIPS, 2, rows // 2, cols)
    return bufs


def _step(x, loss_target, w, mom_m, mom_v):
    T, D = x.shape[1], x.shape[2]
    E = D
    G, P, C = w['s5_a_re'].shape[1], S5_STATE, S5_GROUP
    H = E // FOX_HEAD_DIM
    PG = len(POOL_WINDOWS)
    PD = E // PG
    NC = G // GROUPS_PER_CHUNK
    L = GROUPS_PER_CHUNK * P
    tq = _t(256, T)
    nq = T // tq

    wb = _cast_weights(w)
    phases = [[('s5_in_proj', 0)],
              [('s5_w_glu', 0), ('out_proj', 0)],
              [('out_proj', 1), ('fox_in_proj', 0)],
              [('out_proj', 2), ('pool_in_proj', 0), ('pool_w_group', 0), ('out_proj', 3), ('s5_in_proj', 1), ('s5_w_glu', 1)]]
    W = {}
    flight = {}

    def landed(keys, bufs):
        for k, b in zip(keys, bufs):
            W[k] = b.reshape(N_CHIPS, 2 * b.shape[2], b.shape[3])

    def take_phase(ph, after):
        bufs, send, recv, _ = flight.pop(ph)
        landed(phases[ph], _gather_forward(f"gather_{ph}_pass", _gather_wait(f"gather_{ph}_wait", bufs, send, recv, after)))

    landed(phases[0], _chip_allgather("gather_0", [wb[k] for k in phases[0]]))
    after = W[phases[0][0]]
    for ph in range(1, len(phases)):
        flight[ph] = _gather_start(f"gather_{ph}_start", [wb[k] for k in phases[ph]], [after])
        after = flight[ph][3]
    gather_tokens = [after]
    small_full = {}
    chip = 2 * lax.axis_index("x") + lax.axis_index("y")
    sv = [lax.dynamic_update_index_in_dim(jnp.zeros((N_CHIPS, 2) + w[n].shape, F32), jnp.stack([w[n], w[n]]), chip, 0)
          for n in SMALL_SHARDED]
    got = _chip_allgather("gather_vectors", sv)
    for n, g in zip(SMALL_SHARDED, got):
        small_full[n] = jnp.transpose(g[:, 0], (1, 0, 2)).reshape(w[n].shape[0], E)

    norm_w = w['norm_w']
    h = x.reshape(T, D)
    saved = []
    dparts = {}

    def s5_consts(j):
        ar, ai, fr, fi = _s5_disc_fwd(f"s5_disc_{j}", w['s5_a_re'][j], w['s5_a_im'][j], w['s5_log_dt'][j].reshape(G, 1))
        br, bi = w['s5_b_re'][j].reshape(G * P, C), w['s5_b_im'][j].reshape(G * P, C)
        bbr, bbi = _s5_bbar(f"s5_bbar_{j}", fr.reshape(G * P, 1), fi.reshape(G * P, 1), br, bi)
        bbd = jnp.concatenate([_compact(bbr.reshape(G, P, C), NC), _compact(bbi.reshape(G, P, C), NC)], axis=2).astype(BF16)
        ct = lambda v: jnp.transpose(v, (0, 2, 1))
        cbd = jnp.concatenate([_compact(ct(w['s5_c_re'][j]), NC), -_compact(ct(w['s5_c_im'][j]), NC)], axis=2).astype(BF16)
        return dict(ar=ar, ai=ai, fr=fr, fi=fi, br=br, bi=bi, bbd=bbd, cbd=cbd,
                    ar3=ar.reshape(NC, 1, L), ai3=ai.reshape(NC, 1, L))

    for i in range(4):
        kind, j = i % 3, i // 3
        nw = norm_w[i].reshape(1, D)
        xn = _norm_fwd(f"norm_{i}", h, nw, deps=gather_tokens if i == 0 else ())
        if kind == 0:
            k5 = s5_consts(j)
            proj = _mm_proj(f"s5_proj_{i}", xn, W[('s5_in_proj', j)])
            dsk = small_full['s5_d'][j].reshape(1, E)
            y1, g, hs = _s5_fwd(f"s5_scan_{i}", proj, k5['bbd'], k5['cbd'], k5['ar3'], k5['ai3'], dsk, E)
            bglu = small_full['s5_b_glu'][j].reshape(1, E)
            if i == 0:
                take_phase(1, y1)

            def glu_epi(acc, b, y1t, z):
                lin = acc + b
                return lin, (_gelu(y1t) * _sigmoid(lin)) * _silu(z)

            lin, a = _mm_rowsharded(
                f"s5_glu_{i}", g, W[('s5_w_glu', j)], epi=glu_epi,
                extras=lambda tm, tn: [(bglu, _rowvec(tn)), (y1, _tile(tm, tn)), (proj, _tile(tm, tn, E // tn))],
                outs_fn=lambda tm, tn: [((T, E), F32, _tile(tm, tn)), ((T, E), BF16, _tile(tm, tn))])
            saved.append(dict(h=h, xn=xn, proj=proj, y1=y1, g=g, hs=hs, lin=lin, a=a, k5=k5, dsk=dsk))
        elif kind == 1:
            fox_w = jnp.transpose(W[('fox_in_proj', j)], (1, 0, 2)).reshape(D, -1)
            w_qkvz = fox_w[:, :4 * E]
            w_f = jnp.pad(fox_w[:, 4 * E:], ((0, 0), (0, LANES - H)))
            proj = _mm_plain(f"fox_proj_{i}", xn, w_qkvz)[0]
            flog = _mm_plain(f"fox_gate_proj_{i}", xn, w_f)[0]
            fb = jnp.pad(w['fox_f_bias'][j].reshape(1, H), ((0, 0), (0, LANES - H)))
            wq, wk = w['fox_q_norm'][j].reshape(1, FOX_HEAD_DIM), w['fox_k_norm'][j].reshape(1, FOX_HEAD_DIM)
            qn, kn = _qk_norm(f"fox_qk_norm_{i}", proj, wq, wk, H)
            cum = _cum_rows(f"fox_cum_{i}", flog, fb, False, True)
            cum_t = jnp.transpose(cum)[:H]
            cum_q = jnp.broadcast_to(cum_t[:, :, None], (H, T, LANES))
            cum_k = cum_t.reshape(H, nq, 1, tq)
            y, lse = _attn_fwd(f"fox_attn_{i}", qn, kn, proj, cum_q, cum_k, H)
            a = _rows(f"fox_gate_{i}", lambda yt, z: (yt * _silu(z),), [(y, 'r', E, 0), (proj, 'r', E, 3)], [('r', E, BF16)], 256)[0]
            saved.append(dict(h=h, xn=xn, proj=proj, flog=flog, fb=fb, wq=wq, wk=wk, qn=qn, kn=kn, cum_q=cum_q, cum_k=cum_k, y=y, lse=lse, a=a,
                              w_qkvz=w_qkvz, w_f=w_f))
        else:
            w_pg = W[('pool_w_group', j)].reshape(N_CHIPS, PG, PD // N_CHIPS, PD)
            proj = _mm_proj(f"pool_proj_{i}", xn, W[('pool_in_proj', j)])
            pm = _pool_fwd(f"pool_win_{i}", proj, E)
            scale = small_full['pool_scale'][j].reshape(1, E)
            tm, tn, tk = _t(512, T), _t(512, PD), w_pg.shape[2]
            kb, nb = PD // tk, PD // tn
            mixed, a = _mm(
                f"pool_mix_{i}", pm, w_pg, M=T, N=PD, K=PD, tm=tm, tn=tn, tk=tk, groups=PG,
                a_spec=_bs((tm, tk), lambda g, m, n, k: (m, g * kb + k)),
                b_spec=_bs((None, None, tk, tn), lambda g, m, n, k: (k, g, 0, n)),
                extras=[(scale, _bs((1, tn), lambda g, m, n, k: (0, g * nb + n))),
                        (proj, _bs((tm, tn), lambda g, m, n, k: (m, E // tn + g * nb + n)))],
                epi=lambda acc, sc, z: (acc, (acc * sc) * _silu(z)),
                outs=[((T, E), F32, _bs((tm, tn), lambda g, m, n, k: (m, g * nb + n))),
                      ((T, E), BF16, _bs((tm, tn), lambda g, m, n, k: (m, g * nb + n)))])
            saved.append(dict(h=h, xn=xn, proj=proj, pm=pm, mixed=mixed, scale=scale, a=a, w_pg=w_pg))
        h = _mm_rowsharded(f"out_proj_{i}", saved[-1]['a'], W[('out_proj', i)], epi=lambda acc, r: (r + acc,),
                           extras=lambda tm, tn: [(h, _tile(tm, tn))],
                           outs_fn=lambda tm, tn: [((T, D), F32, _tile(tm, tn))])[0]
        if i < 2:
            take_phase(i + 2, h)

    dh, dh16, loss_cols = _loss(h, loss_target.reshape(T, D))
    loss = lax.psum(jnp.sum(loss_cols), ("x", "y", "c"))

    gsmall = {n: [None] * w[n].shape[0] for n in SMALL}
    big_index = {n: o for o, n in enumerate(BIG)}
    rs_shapes = [None] * (len(BIG) + 1)
    rs_bufs = [None] * (len(BIG) + 1)
    rs_dests_all = []
    pending = None

    def reduce_layer(tag, named_parts):
        parts, dests = [], []
        for n, l, pt in named_parts:
            o = big_index[n] if n in big_index else len(BIG)
            half = pt.shape[2:]
            rs_shapes[o] = (N_CHIPS if l == 'chip' else w[n].shape[0], 2, math.prod(half[:-1]), half[-1])
            parts.append(pt)
            dests.append((o, l))
        rs_dests_all.extend(dests)
        state, token = _reduce_begin(tag, parts)
        return (tag, state, dests), token

    token = None
    for i in reversed(range(4)):
        kind, j = i % 3, i // 3
        sv_ = saved[i]
        nw = norm_w[i].reshape(1, D)
        w_out = W[('out_proj', i)]
        after_start = [token] if token is not None else ()
        layer_parts = [('out_proj', i, _mm_dw_rows(f"d_out_proj_{i}", sv_['a'], dh16, deps=after_start))]
        if kind == 0:
            w_glu = W[('s5_w_glu', j)]
            proj, y1, lin, k5 = sv_['proj'], sv_['y1'], sv_['lin'], sv_['k5']

            def da_epi(da, y1t, lint, z):
                gt, sg = _gelu(y1t), _sigmoid(lint)
                dy2 = da * _silu(z)
                dlin = (dy2 * gt) * (sg * (1.0 - sg))
                return da * (gt * sg) * _dsilu(z), dlin, dy2 * sg, _colsum(dlin)

            nm = T // _t(512, T)
            dz, dlin, dgd, dbg = _mm_rowsharded_t(
                f"d_s5_act_{i}", dh16, w_out, epi=da_epi, deps=after_start,
                extras=lambda tm, tn: [(y1, _tile(tm, tn)), (lin, _tile(tm, tn)), (proj, _tile(tm, tn, E // tn))],
                outs_fn=lambda tm, tn: [((T, E), BF16, _tile(tm, tn)), ((T, E), BF16, _tile(tm, tn)), ((T, E), F32, _tile(tm, tn)),
                                        ((nm, 1, E), F32, _bs((None, 1, tn), lambda g, m, n, k: (m, 0, n)))])
            gsmall['s5_b_glu'][j] = jnp.sum(dbg, axis=(0, 1))
            layer_parts.append(('s5_w_glu', j, _mm_dw_rows(f"d_s5_w_glu_{i}", sv_['g'], dlin)))
            glu_deps = ()
            if i == 0:
                early, early_token = reduce_layer("l0a", layer_parts)
                layer_parts, glu_deps = [], [early_token]
            dy1 = _mm_rowsharded_t(
                f"d_s5_glu_{i}", dlin, w_glu, epi=lambda acc, d, y1t: ((acc + d) * _dgelu(y1t),), deps=glu_deps,
                extras=lambda tm, tn: [(dgd, _tile(tm, tn)), (y1, _tile(tm, tn))],
                outs_fn=lambda tm, tn: [((T, E), F32, _tile(tm, tn))])[0]
            du, dbd, dcd, dab, ddk = _s5_bwd(f"d_s5_scan_{i}", dy1, proj, sv_['hs'], k5['bbd'], k5['cbd'], k5['ar3'], k5['ai3'], sv_['dsk'], E)
            gsmall['s5_d'][j] = ddk.reshape(E)
            gsmall['s5_c_re'][j] = jnp.transpose(_uncompact(dcd[:, :, :L], G), (0, 2, 1))
            gsmall['s5_c_im'][j] = -jnp.transpose(_uncompact(dcd[:, :, L:], G), (0, 2, 1))
            dbbr = _uncompact(dbd[:, :, :L], G).reshape(G * P, C)
            dbbi = _uncompact(dbd[:, :, L:], G).reshape(G * P, C)
            dbr, dbi, dfr, dfi = _s5_bbar_bwd(f"d_s5_bbar_{i}", k5['fr'].reshape(G * P, 1), k5['fi'].reshape(G * P, 1), k5['br'], k5['bi'], dbbr, dbbi)
            gsmall['s5_b_re'][j] = dbr.reshape(G, P, C)
            gsmall['s5_b_im'][j] = dbi.reshape(G, P, C)
            dab = jnp.sum(dab, axis=1)
            dare, daim, dldt = _s5_disc_bwd(f"d_s5_disc_{i}", w['s5_a_re'][j], w['s5_a_im'][j], w['s5_log_dt'][j].reshape(G, 1),
                                            (dab[:, :L].reshape(G, P), dab[:, L:].reshape(G, P), dfr.reshape(G, P), dfi.reshape(G, P)))
            gsmall['s5_a_re'][j], gsmall['s5_a_im'][j], gsmall['s5_log_dt'][j] = dare, daim, dldt.reshape(G)
            dproj = jnp.concatenate([du, dz], axis=1)
            layer_parts.append(('s5_in_proj', j, _mm_dw_cols(f"d_s5_in_proj_{i}", sv_['xn'], dproj)))
            dxn = _mm_colsharded_t(f"d_s5_xn_{i}", dproj, W[('s5_in_proj', j)])
        elif kind == 1:
            proj, y = sv_['proj'], sv_['y']
            do, dz = _mm_rowsharded_t(
                f"d_fox_act_{i}", dh16, w_out, epi=lambda da, yt, z: (da * _silu(z), (da * yt) * _dsilu(z)), deps=after_start,
                extras=lambda tm, tn: [(y, _tile(tm, tn)), (proj, _tile(tm, tn, 3 * E // tn))],
                outs_fn=lambda tm, tn: [((T, E), F32, _tile(tm, tn)), ((T, E), BF16, _tile(tm, tn))])
            dqn, dkn, dv, dcq, dck = _attn_bwd(f"d_fox_attn_{i}", sv_['qn'], sv_['kn'], proj, do, y, sv_['lse'], sv_['cum_q'], sv_['cum_k'], H)
            dq, dk, dwq, dwk = _qk_norm_bwd(f"d_fox_qk_norm_{i}", proj, sv_['wq'], sv_['wk'], dqn, dkn, H)
            gsmall['fox_q_norm'][j], gsmall['fox_k_norm'][j] = dwq.reshape(-1), dwk.reshape(-1)
            dcum = dcq + jnp.pad(jnp.transpose(dck.reshape(H, T)), ((0, 0), (0, LANES - H)))
            dls = _cum_rows(f"d_fox_cum_{i}", dcum, jnp.zeros((1, LANES), F32), True, False)
            dflog, dfb = _rows(f"d_fox_gate_{i}", lambda d, f, b: ((lambda r: (r, _colsum(r)))(d * _sigmoid(-(f + b)))),
                               [(dls, 'r', LANES, 0), (sv_['flog'], 'r', LANES, 0), (sv_['fb'], 'b', LANES, 0)],
                               [('r', LANES, BF16), ('a', LANES, F32)], 256)
            gsmall['fox_f_bias'][j] = dfb[0, :H]
            dproj = jnp.concatenate([dq, dk, dv, dz], axis=1)
            tkT = _t(K_STEP, T)
            dw_qkvz = _mm(f"d_fox_in_proj_{i}", sv_['xn'], dproj, M=D, N=4 * E, K=T, tm=_t(512, D), tn=_t(1024, 4 * E), tk=tkT, ta=True,
                          a_spec=_bs((tkT, _t(512, D)), lambda g, m, n, k: (k, m)),
                          b_spec=_bs((tkT, _t(1024, 4 * E)), lambda g, m, n, k: (k, n)),
                          outs=[((D, 4 * E), BF16, _tile(_t(512, D), _t(1024, 4 * E)))])[0]
            dw_f = _mm(f"d_fox_gate_proj_{i}", sv_['xn'], dflog, M=D, N=LANES, K=T, tm=_t(512, D), tn=LANES, tk=tkT, ta=True,
                       a_spec=_bs((tkT, _t(512, D)), lambda g, m, n, k: (k, m)),
                       b_spec=_bs((tkT, LANES), lambda g, m, n, k: (k, n)),
                       outs=[((D, LANES), BF16, _tile(_t(512, D), LANES))])[0]
            dw_fox = jnp.concatenate([dw_qkvz, dw_f[:, :H]], axis=1)
            sw = dw_fox.shape[1] // N_CHIPS
            layer_parts.append(('fox_in_proj', j, jnp.transpose(dw_fox.reshape(2, D // 2, N_CHIPS, sw), (0, 2, 1, 3))))
            w_qkvz, w_f = sv_['w_qkvz'], sv_['w_f']
            dxn_f = _mm(f"d_fox_xn_gate_{i}", dflog, w_f, M=T, N=D, K=LANES, tm=_t(512, T), tn=_t(1024, D), tk=LANES, tb=True,
                        a_spec=_bs((_t(512, T), LANES), lambda g, m, n, k: (m, k)),
                        b_spec=_bs((_t(1024, D), LANES), lambda g, m, n, k: (n, k)),
                        outs=[((T, D), F32, _tile(_t(512, T), _t(1024, D)))])[0]
            tm, tn, tk = _t(512, T), _t(1024, D), _t(1024, 4 * E)
            dxn = _mm(f"d_fox_xn_{i}", dproj, w_qkvz, M=T, N=D, K=4 * E, tm=tm, tn=tn, tk=tk, tb=True,
                      a_spec=_bs((tm, tk), lambda g, m, n, k: (m, k)), b_spec=_bs((tn, tk), lambda g, m, n, k: (n, k)),
                      extras=[(dxn_f, _tile(tm, tn))], epi=lambda acc, e: (acc + e,),
                      outs=[((T, D), F32, _tile(tm, tn))])[0]
        else:
            proj, mixed, scale = sv_['proj'], sv_['mixed'], sv_['scale']
            nm = T // _t(512, T)

            def pool_epi(da, mx, sc, z):
                dy = da * _silu(z)
                return (da * (mx * sc)) * _dsilu(z), dy * sc, _colsum(dy * mx)

            dz, dmix, dsc = _mm_rowsharded_t(
                f"d_pool_act_{i}", dh16, w_out, epi=pool_epi, deps=after_start,
                extras=lambda tm, tn: [(mixed, _tile(tm, tn)), (scale, _rowvec(tn)), (proj, _tile(tm, tn, E // tn))],
                outs_fn=lambda tm, tn: [((T, E), BF16, _tile(tm, tn)), ((T, E), BF16, _tile(tm, tn)),
                                        ((nm, 1, E), F32, _bs((None, 1, tn), lambda g, m, n, k: (m, 0, n)))])
            gsmall['pool_scale'][j] = jnp.sum(dsc, axis=(0, 1))
            w_pg = sv_['w_pg']
            tkw = w_pg.shape[2]
            tk = _t(K_STEP, T)
            layer_parts.append(('pool_w_group', j, _mm(
                f"d_pool_w_group_{i}", sv_['pm'], dmix, M=PD, N=PD, K=T, tm=tkw, tn=PD, tk=tk, groups=PG, ta=True,
                a_spec=_bs((tk, tkw), lambda g, m, n, k: (k, g * (PD // tkw) + m)),
                b_spec=_bs((tk, PD), lambda g, m, n, k: (k, g)),
                outs=[((2, N_CHIPS, PG // 2, tkw, PD), BF16, _bs((None, None, None, tkw, PD), lambda g, m, n, k: (g // (PG // 2), m, g % (PG // 2), 0, 0)))])[0]))
            tm, tk2 = _t(512, T), _t(512, PD)
            dpm = _mm(f"d_pool_mix_{i}", dmix, w_pg, M=T, N=PD, K=PD, tm=tm, tn=tkw, tk=tk2, groups=PG, tb=True,
                      a_spec=_bs((tm, tk2), lambda g, m, n, k: (m, g * (PD // tk2) + k)),
                      b_spec=_bs((None, None, tkw, tk2), lambda g, m, n, k: (n, g, 0, k)),
                      outs=[((T, E), F32, _bs((tm, tkw), lambda g, m, n, k: (m, g * (PD // tkw) + n)))])[0]
            du = _pool_bwd(f"d_pool_win_{i}", dpm, E)
            dproj = jnp.concatenate([du, dz], axis=1)
            layer_parts.append(('pool_in_proj', j, _mm_dw_cols(f"d_pool_in_proj_{i}", sv_['xn'], dproj)))
            dxn = _mm_colsharded_t(f"d_pool_xn_{i}", dproj, W[('pool_in_proj', j)])
        dh, dh16, dnw = _norm_bwd(f"d_norm_{i}", dxn, sv_['h'], nw, dh)
        gsmall['norm_w'][i] = dnw.reshape(D)
        if pending is not None:
            _reduce_end(pending[0], pending[1], dh16, pending[2], rs_bufs, rs_shapes)
        if i > 0:
            pending, token = reduce_layer(f"l{i}", layer_parts)
    grad_x = dh.reshape(x.shape)

    small_flat = jnp.concatenate([jnp.stack(gsmall[n]).reshape(-1) for n in SMALL])
    n_small = small_flat.shape[0]
    unit = 2 * N_CHIPS * 16 * LANES
    n_pad = -(-n_small // unit) * unit
    R = n_pad // (2 * N_CHIPS * LANES)
    small_part = jnp.pad(small_flat, (0, n_pad - n_small)).astype(BF16).reshape(2, N_CHIPS, R, LANES)
    pending, token = reduce_layer("l0", layer_parts + [('small', 'chip', small_part)])
    _reduce_end(early[0], early[1], token, early[2], rs_bufs, rs_shapes)
    nb = len(BIG)
    done_items = [d for d in rs_dests_all if d not in pending[2]]
    rs_bufs[:nb] = _pair_share("rs_pair_share_a", rs_bufs[:nb], done_items, deps=[token])
    late = [o for o, _ in pending[2]]
    delta, new_m, new_v = {}, {}, {}
    grads = {}
    last = token
    for o, n in enumerate(BIG):
        if o not in late:
            grads[n] = rs_bufs[o].reshape(w[n].shape)
            last = _adamw_big(n, w, grads, mom_m, mom_v, delta, new_m, new_v)
    _reduce_end(pending[0], pending[1], last, pending[2], rs_bufs, rs_shapes)
    shared = _pair_share("rs_pair_share_b", [rs_bufs[o] for o in late], [(k, l) for k, (_, l) in enumerate(pending[2])])
    for k, o in enumerate(late):
        rs_bufs[o] = shared[k]
        if o < nb:
            grads[BIG[o]] = shared[k].reshape(w[BIG[o]].shape)
            _adamw_big(BIG[o], w, grads, mom_m, mom_v, delta, new_m, new_v)
    small_all = _chip_allgather("gather_small_grads", [rs_bufs[nb]])[0]
    small_all = jnp.transpose(small_all, (1, 0, 2, 3)).reshape(-1)[:n_small]
    off = 0
    p = 2 * lax.axis_index("x") + lax.axis_index("y")
    for n in SMALL:
        full_shape = (w[n].shape[0], E) if n in SMALL_SHARDED else w[n].shape
        size = math.prod(full_shape)
        gfull = small_all[off:off + size].reshape(full_shape)
        off += size
        if n in SMALL_SHARDED:
            gfull = lax.dynamic_slice_in_dim(gfull, p * (E // N_CHIPS), E // N_CHIPS, axis=1)
        grads[n] = gfull

    for n in SMALL:
        shape = w[n].shape
        if n in GROUP_AXIS_1:
            perm = (0,) + tuple(range(2, len(shape))) + (1,)
            inv = (0, len(shape) - 1) + tuple(range(1, len(shape) - 1))
            view = lambda a: jnp.transpose(a, perm).reshape(-1, shape[1])
            back = lambda a: jnp.transpose(a.reshape(tuple(shape[k] for k in perm)), inv)
        else:
            view = lambda a: a.reshape(-1, shape[-1])
            back = lambda a: a.reshape(shape)
        d_, m_, v_ = _adamw(f"adamw_{n}", view(w[n]), view(grads[n]), view(mom_m[n]), view(mom_v[n]))
        delta[n], new_m[n], new_v[n] = back(d_), back(m_), back(v_)
    return (loss, grad_x, *[grads[n] for n in ORDER], *[delta[n] for n in ORDER], *[new_m[n] for n in ORDER], *[new_v[n] for n in ORDER])
```

```python
import functools
import math

import jax
import jax.numpy as jnp
from jax import lax
from jax.experimental import pallas as pl
from jax.experimental.pallas import tpu as pltpu

F32 = jnp.float32
BF16 = jnp.bfloat16
MESH = pl.DeviceIdType.MESH

N_CHIPS = 4
VMEM_LIMIT = 56 * 1024 * 1024
LANES = 128
SUB = 8

EPS = 1e-6
S5_GROUP = 16
S5_STATE = 64
GROUPS_PER_CHUNK = 16
FOX_HEAD_DIM = 128
ATTN_SUB = 256
ATTN_HEADS = 2
POOL_WINDOWS = (2, 4, 8, 16)
POOL_HALO = 16
ADAM_LR, ADAM_B1, ADAM_B2, ADAM_EPS, ADAM_WD, ADAM_STEP = 0.001, 0.9, 0.999, 1e-08, 0.01, 10
NEG = -1e30
K_STEP = 2048


ANY = pl.BlockSpec(memory_space=pl.ANY)


def _t(pref, dim):
    if dim <= pref:
        return dim
    t = pref - pref % 16
    while t > 16 and dim % t:
        t -= 16
    assert dim % t == 0, (pref, dim)
    return t


def _params(sem):
    return pltpu.CompilerParams(dimension_semantics=sem, vmem_limit_bytes=VMEM_LIMIT)


def _sigmoid(x):
    return 1.0 / (1.0 + jnp.exp(-x))


def _silu(z):
    return z * _sigmoid(z)


def _dsilu(z):
    s = _sigmoid(z)
    return s * (1.0 + z * (1.0 - s))


_GELU_C = math.sqrt(2.0 / math.pi)


def _gelu(x):
    return 0.5 * x * (1.0 + jnp.tanh(_GELU_C * (x + 0.044715 * (x * x * x))))


def _dgelu(x):
    t = jnp.tanh(_GELU_C * (x + 0.044715 * (x * x * x)))
    return 0.5 * (1.0 + t) + 0.5 * x * (1.0 - t * t) * (_GELU_C * (1.0 + 3.0 * 0.044715 * x * x))


def _log_sigmoid(x):
    return jnp.minimum(x, 0.0) - jnp.log(1.0 + jnp.exp(-jnp.abs(x)))


def _rms(x):
    return lax.rsqrt(jnp.mean(x * x, axis=-1, keepdims=True) + EPS)


def _rms_bwd(x, w, dy):
    r = _rms(x)
    xhat = x * r
    dxh = dy * w
    dx = r * (dxh - xhat * jnp.mean(dxh * xhat, axis=-1, keepdims=True))
    return dx, dy * xhat


def _rows(name, fn, ins, outs, tr, pre=None, into=None, deps=()):
    rows = None
    for arr, kind, cols, cb in ins:
        if kind == 'r':
            rows = arr.shape[0]
        elif kind == 's' and rows is None:
            rows = arr.shape[1]
    tr = _t(tr, rows)
    n_in = len(ins)
    has_acc = any(o[0] == 'a' for o in outs)

    def spec(kind, cols, cb):
        if kind == 'r':
            return pl.BlockSpec((tr, cols), lambda r, *p: (r, cb))
        if kind == 'b':
            return pl.BlockSpec((1, cols), lambda r, *p: (0, cb))
        return pl.BlockSpec((None, tr, cols), lambda r, p: (p[cb], r, 0))

    in_specs = [spec(kind, cols, cb) for _, kind, cols, cb in ins]
    out_specs, out_shape = [], []
    for o in outs:
        if o[0] == 'r':
            out_specs.append(pl.BlockSpec((tr, o[1]), lambda r, *p: (r, 0)))
            out_shape.append(jax.ShapeDtypeStruct((rows, o[1]), o[2]))
        elif o[0] == 'a':
            out_specs.append(pl.BlockSpec((1, o[1]), lambda r, *p: (0, 0)))
            out_shape.append(jax.ShapeDtypeStruct((1, o[1]), o[2]))
        else:
            blk = tuple(tr if d == 'tr' else d for d in o[3])
            out_specs.append(pl.BlockSpec(blk, o[4]))
            out_shape.append(jax.ShapeDtypeStruct(o[1], o[2]))
    n_pre = 0 if pre is None else 1
    args = [a[0] for a in ins]
    aliases = {}
    if into is not None:
        in_specs.append(ANY)
        args.append(into)
        aliases = {n_pre + n_in: 0}
    in_specs += [ANY] * len(deps)
    args += list(deps)
    n_all = len(args)

    def body(*refs):
        refs = refs[n_pre:]
        res = fn(*[r[...] for r in refs[:n_in]])
        for spec_o, o, v in zip(outs, refs[n_all:], res):
            if spec_o[0] == 'a':
                @pl.when(pl.program_id(0) == 0)
                def _():
                    o[...] = jnp.zeros_like(o)
                o[...] += v.astype(o.dtype)
            else:
                o[...] = v.astype(o.dtype)

    grid_spec = pltpu.PrefetchScalarGridSpec(num_scalar_prefetch=n_pre, grid=(rows // tr,), in_specs=in_specs, out_specs=out_specs)
    if pre is not None:
        args = [pre] + args
    return pl.pallas_call(body, name=name, grid_spec=grid_spec, out_shape=out_shape, input_output_aliases=aliases,
                          compiler_params=_params(("arbitrary" if has_acc else "parallel",)))(*args)


def _colsum(v):
    return jnp.sum(v, axis=0, keepdims=True)


def _mm(name, a, b, *, M, N, K, tm, tn, tk, a_spec, b_spec, outs, epi=None, extras=(), groups=1, ta=False, tb=False, deps=()):
    nk = K // tk
    assert M % tm == 0 and N % tn == 0 and K % tk == 0, (name, M, N, K, tm, tn, tk)
    dims = (((0 if ta else 1,), (1 if tb else 0,)), ((), ()))
    n_ex = len(extras)

    def body(*refs):
        a_ref, b_ref = refs[0], refs[1]
        ex = refs[2:2 + n_ex]
        out_refs = refs[2 + n_ex + len(deps):2 + n_ex + len(deps) + len(outs)]

        def finish(r):
            res = (r,) if epi is None else epi(r, *[e[...] for e in ex])
            for o, v in zip(out_refs, res):
                o[...] = v.astype(o.dtype)

        part = lax.dot_general(a_ref[...].astype(BF16), b_ref[...].astype(BF16), dims, preferred_element_type=F32)
        if nk == 1:
            finish(part)
            return
        acc = refs[-1]
        k = pl.program_id(3)

        @pl.when(k == 0)
        def _():
            acc[...] = part

        @pl.when(k > 0)
        def _():
            acc[...] += part

        @pl.when(k == nk - 1)
        def _():
            finish(acc[...])

    return pl.pallas_call(
        body, name=name, grid=(groups, M // tm, N // tn, nk),
        in_specs=[a_spec, b_spec] + [s for _, s in extras] + [ANY] * len(deps),
        out_specs=[s for _, _, s in outs],
        out_shape=[jax.ShapeDtypeStruct(sh, dt) for sh, dt, _ in outs],
        scratch_shapes=[] if nk == 1 else [pltpu.VMEM((tm, tn), F32)],
        compiler_params=_params(("parallel", "parallel", "parallel", "arbitrary")),
    )(a, b, *[e for e, _ in extras], *deps)


def _bs(shape, f):
    return pl.BlockSpec(shape, f)


def _tile(tm, tn, coff=0):
    return _bs((tm, tn), lambda g, m, n, k: (m, n + coff))


def _rowvec(tn, coff=0):
    return _bs((1, tn), lambda g, m, n, k: (0, n + coff))


def _mm_proj(name, xn, w, *, epi=None, extras=(), out_dtype=F32):
    T, D = xn.shape
    sw = w.shape[2]
    N = N_CHIPS * sw
    tm, tn, tk = _t(512, T), _t(1024, sw), _t(K_STEP, D)
    nb = sw // tn
    return _mm(name, xn, w, M=T, N=N, K=D, tm=tm, tn=tn, tk=tk,
               a_spec=_bs((tm, tk), lambda g, m, n, k: (m, k)),
               b_spec=_bs((None, tk, tn), lambda g, m, n, k: (n // nb, k, n % nb)),
               outs=[((T, N), out_dtype, _tile(tm, tn))], epi=epi, extras=extras)[0]


def _mm_plain(name, a, b, *, out_dtype=F32, epi=None, extras=(), outs=None, tn_pref=1024):
    M, K = a.shape
    N = b.shape[1]
    tm, tn, tk = _t(512, M), _t(tn_pref, N), _t(K_STEP, K)
    if outs is None:
        outs = [((M, N), out_dtype, _tile(tm, tn))]
    return _mm(name, a, b, M=M, N=N, K=K, tm=tm, tn=tn, tk=tk,
               a_spec=_bs((tm, tk), lambda g, m, n, k: (m, k)),
               b_spec=_bs((tk, tn), lambda g, m, n, k: (k, n)),
               outs=outs, epi=epi, extras=extras)


def _mm_rowsharded(name, a, w, *, epi, extras, outs_fn, deps=()):
    T, E = a.shape
    tk = w.shape[1]
    N = w.shape[2]
    tm, tn = _t(512, T), _t(1024, N)
    return _mm(name, a, w, M=T, N=N, K=E, tm=tm, tn=tn, tk=tk, deps=deps,
               a_spec=_bs((tm, tk), lambda g, m, n, k: (m, k)),
               b_spec=_bs((None, tk, tn), lambda g, m, n, k: (k, 0, n)),
               outs=outs_fn(tm, tn), epi=epi, extras=extras(tm, tn))


def _mm_rowsharded_t(name, d, w, *, epi, extras, outs_fn, deps=()):
    T, N = d.shape
    tn = w.shape[1]
    E = N_CHIPS * tn
    tm, tk = _t(512, T), _t(K_STEP, N)
    return _mm(name, d, w, M=T, N=E, K=N, tm=tm, tn=tn, tk=tk, tb=True, deps=deps,
               a_spec=_bs((tm, tk), lambda g, m, n, k: (m, k)),
               b_spec=_bs((None, tn, tk), lambda g, m, n, k: (n, 0, k)),
               outs=outs_fn(tm, tn), epi=epi, extras=extras(tm, tn))


def _mm_colsharded_t(name, d, w):
    T, N = d.shape
    D, sw = w.shape[1], w.shape[2]
    tm, tn, tk = _t(512, T), _t(1024, D), _t(1024, sw)
    kb = sw // tk
    return _mm(name, d, w, M=T, N=D, K=N, tm=tm, tn=tn, tk=tk, tb=True,
               a_spec=_bs((tm, tk), lambda g, m, n, k: (m, k)),
               b_spec=_bs((None, tn, tk), lambda g, m, n, k: (k // kb, n, k % kb)),
               outs=[((T, D), F32, _tile(tm, tn))])[0]


def _mm_dw_rows(name, a, d, deps=()):
    T, E = a.shape
    N = d.shape[1]
    tm, tn, tk = E // (2 * N_CHIPS), _t(2048, N), _t(K_STEP, T)
    return _mm(name, a, d, M=E, N=N, K=T, tm=tm, tn=tn, tk=tk, ta=True, deps=deps,
               a_spec=_bs((tk, tm), lambda g, m, n, k: (k, m)),
               b_spec=_bs((tk, tn), lambda g, m, n, k: (k, n)),
               outs=[((2, N_CHIPS, tm, N), BF16, _bs((None, None, tm, tn), lambda g, m, n, k: (m % 2, m // 2, 0, n)))])[0]


def _mm_dw_cols(name, xn, d):
    T, D = xn.shape
    N = d.shape[1]
    sw = N // N_CHIPS
    tm, tn, tk = _t(512, D // 2), _t(1024, sw), _t(K_STEP, T)
    mh, nb = (D // 2) // tm, sw // tn
    return _mm(name, xn, d, M=D, N=N, K=T, tm=tm, tn=tn, tk=tk, ta=True,
               a_spec=_bs((tk, tm), lambda g, m, n, k: (k, m)),
               b_spec=_bs((tk, tn), lambda g, m, n, k: (k, n)),
               outs=[((2, N_CHIPS, D // 2, sw), BF16,
                      _bs((None, None, tm, tn), lambda g, m, n, k: (m // mh, n // nb, m % mh, n % nb)))])[0]


def _norm_fwd(name, h, w, deps=()):
    D = h.shape[1]
    return _rows(name, lambda x, g: ((x * _rms(x)) * g,), [(h, 'r', D, 0), (w, 'b', D, 0)], [('r', D, BF16)], 256, deps=deps)[0]


def _norm_bwd(name, dxn, h, w, dh):
    D = h.shape[1]

    def fn(dy, x, g, up):
        dx, dwt = _rms_bwd(x, g, dy)
        r = up + dx
        return r, r, _colsum(dwt)

    return _rows(name, fn, [(dxn, 'r', D, 0), (h, 'r', D, 0), (w, 'b', D, 0), (dh, 'r', D, 0)],
                 [('r', D, F32), ('r', D, BF16), ('a', D, F32)], 256)


def _loss(h, target):
    D = h.shape[1]

    def fn(y, t):
        e = y - t
        d = e * (1.0 / D)
        return d, d, _colsum(e * e) * (0.5 / D)

    return _rows("loss", fn, [(h, 'r', D, 0), (target, 'r', D, 0)], [('r', D, F32), ('r', D, BF16), ('a', D, F32)], 256)


def _adamw(name, w, g, m, v):
    cols = w.shape[1]

    def fn(w, g, m, v):
        m = ADAM_B1 * m + (1.0 - ADAM_B1) * g
        v = ADAM_B2 * v + (1.0 - ADAM_B2) * (g * g)
        m_hat = m / (1.0 - ADAM_B1 ** ADAM_STEP)
        v_hat = v / (1.0 - ADAM_B2 ** ADAM_STEP)
        delta = -ADAM_LR * (m_hat / (jnp.sqrt(v_hat) + ADAM_EPS) + ADAM_WD * w)
        return delta, m, v

    rows = w.shape[0]
    if rows % SUB == 0 or rows <= 256:
        return _rows(name, fn, [(x, 'r', cols, 0) for x in (w, g, m, v)], [('r', cols, F32)] * 3, 256)
    tc = _t(256, cols)
    assert tc % LANES == 0, (rows, cols)

    def body(w_ref, g_ref, m_ref, v_ref, d_out, m_out, v_out):
        for o, r in zip((d_out, m_out, v_out), fn(w_ref[...], g_ref[...], m_ref[...], v_ref[...])):
            o[...] = r

    blk = pl.BlockSpec((rows, tc), lambda j: (0, j))
    return pl.pallas_call(body, name=name, grid=(cols // tc,), in_specs=[blk] * 4, out_specs=[blk] * 3,
                          out_shape=[jax.ShapeDtypeStruct((rows, cols), F32)] * 3, compiler_params=_params(("parallel",)))(w, g, m, v)


def _s5_disc(a_re, a_im, log_dt):
    dt = jnp.exp(log_dt)
    mag = jnp.exp(a_re * dt)
    abar_r = mag * jnp.cos(a_im * dt)
    abar_i = mag * jnp.sin(a_im * dt)
    den = a_re * a_re + a_im * a_im
    xr = abar_r - 1.0
    fr = (xr * a_re + abar_i * a_im) / den
    fi = (abar_i * a_re - xr * a_im) / den
    return abar_r, abar_i, fr, fi


def _s5_disc_fwd(name, a_re, a_im, log_dt):
    G, P = a_re.shape

    def body(ar, ai, ld, o0, o1, o2, o3):
        for o, v in zip((o0, o1, o2, o3), _s5_disc(ar[...], ai[...], ld[...])):
            o[...] = v

    return pl.pallas_call(body, name=name, out_shape=[jax.ShapeDtypeStruct((G, P), F32)] * 4)(a_re, a_im, log_dt)


def _s5_disc_bwd(name, a_re, a_im, log_dt, cts):
    G, P = a_re.shape

    def body(ar, ai, ld, c0, c1, c2, c3, d0, d1, d2):
        _, vjp = jax.vjp(_s5_disc, ar[...], ai[...], ld[...])
        g0, g1, g2 = vjp((c0[...], c1[...], c2[...], c3[...]))
        d0[...] = g0
        d1[...] = g1
        d2[...] = g2

    return pl.pallas_call(body, name=name, out_shape=[jax.ShapeDtypeStruct((G, P), F32)] * 2 + [jax.ShapeDtypeStruct((G, 1), F32)])(
        a_re, a_im, log_dt, *cts)


def _s5_bbar(name, fr, fi, br, bi):
    return _rows(name, lambda fr, fi, br, bi: (fr * br - fi * bi, fr * bi + fi * br),
                 [(fr, 'r', 1, 0), (fi, 'r', 1, 0), (br, 'r', S5_GROUP, 0), (bi, 'r', S5_GROUP, 0)],
                 [('r', S5_GROUP, F32)] * 2, 2048)


def _s5_bbar_bwd(name, fr, fi, br, bi, dr, di):
    def fn(fr, fi, br, bi, dr, di):
        return (fr * dr + fi * di, fr * di - fi * dr,
                jnp.sum(br * dr + bi * di, axis=1, keepdims=True), jnp.sum(br * di - bi * dr, axis=1, keepdims=True))

    return _rows(name, fn, [(fr, 'r', 1, 0), (fi, 'r', 1, 0)] + [(x, 'r', S5_GROUP, 0) for x in (br, bi, dr, di)],
                 [('r', S5_GROUP, F32)] * 2 + [('r', 1, F32)] * 2, 2048)


def _scan_mults(m_ref, ar, ai, reverse):
    L = ar.shape[1]
    row = lax.broadcasted_iota(jnp.int32, (SUB, L), 0)
    if reverse:
        row = (SUB - 1) - row
    ar = jnp.broadcast_to(ar, (SUB, L))
    ai = jnp.broadcast_to(ai, (SUB, L))
    a2r, a2i = ar * ar - ai * ai, 2.0 * ar * ai
    a4r, a4i = a2r * a2r - a2i * a2i, 2.0 * a2r * a2i
    zero = jnp.zeros((SUB, L), F32)
    for s, (pr, pi, d) in enumerate(((ar, ai, 1), (a2r, a2i, 2), (a4r, a4i, 4))):
        m_ref[2 * s] = jnp.where(row >= d, pr, zero)
        m_ref[2 * s + 1] = jnp.where(row >= d, pi, zero)
    pr, pi = ar, ai
    for bit, (qr, qi) in ((1, (ar, ai)), (2, (a2r, a2i)), (4, (a4r, a4i))):
        on = (row & bit) != 0
        nr, ni = pr * qr - pi * qi, pr * qi + pi * qr
        pr, pi = jnp.where(on, nr, pr), jnp.where(on, ni, pi)
    m_ref[6] = pr
    m_ref[7] = pi


def _scan8(xr, xi, m_ref, cr, ci, reverse):
    for s, d in enumerate((1, 2, 4)):
        sh = (SUB - d) if reverse else d
        sr, si = pltpu.roll(xr, sh, 0), pltpu.roll(xi, sh, 0)
        mr, mi = m_ref[2 * s], m_ref[2 * s + 1]
        xr, xi = xr + mr * sr - mi * si, xi + mr * si + mi * sr
    pr, pi = m_ref[6], m_ref[7]
    return xr + pr * cr - pi * ci, xi + pr * ci + pi * cr


def _blockdiag_fill(bd_ref, c_ref, C, L):
    P = S5_STATE
    bd_ref[...] = jnp.zeros_like(bd_ref)
    for g in range(L // P):
        for half in (0, L):
            bd_ref[g * C:(g + 1) * C, half + g * P:half + (g + 1) * P] = c_ref[:, half + g * P:half + (g + 1) * P]


def _blockdiag_take(out_ref, dense_ref, C, L):
    P = S5_STATE
    for g in range(L // P):
        for half in (0, L):
            out_ref[:, half + g * P:half + (g + 1) * P] = dense_ref[g * C:(g + 1) * C, half + g * P:half + (g + 1) * P]


def _s5_fwd(name, proj, bbd, cbd, abar_r, abar_i, dskip, E):
    T = proj.shape[0]
    NC, C, L2 = bbd.shape
    L = L2 // 2
    CH = GROUPS_PER_CHUNK * C
    tT = _t(256, T)
    nt = (((1,), (1,)), ((), ()))

    def body(u_ref, bc_ref, cc_ref, ar_ref, ai_ref, d_ref, y_ref, g_ref, h_ref, bu, carry, mult, b_bd, c_bd):
        tb = pl.program_id(1)

        @pl.when(tb == 0)
        def _():
            carry[...] = jnp.zeros_like(carry)
            _blockdiag_fill(b_bd, bc_ref, C, L)
            _blockdiag_fill(c_bd, cc_ref, C, L)

        u = u_ref[...]
        bu[...] = jnp.dot(u.astype(BF16), b_bd[...], preferred_element_type=F32)
        _scan_mults(mult, ar_ref[...], ai_ref[...], False)

        def step(jb, c):
            cr, ci = c
            r0 = pl.multiple_of(jb * SUB, SUB)
            hr, hi = _scan8(bu[pl.ds(r0, SUB), 0:L], bu[pl.ds(r0, SUB), L:L2], mult, cr, ci, False)
            h_ref[pl.ds(r0, SUB), 0:L] = hr
            h_ref[pl.ds(r0, SUB), L:L2] = hi
            return (jnp.broadcast_to(hr[SUB - 1:SUB, :], (SUB, L)), jnp.broadcast_to(hi[SUB - 1:SUB, :], (SUB, L)))

        cr, ci = lax.fori_loop(0, tT // SUB, step, (carry[:, 0:L], carry[:, L:L2]))
        carry[:, 0:L] = cr
        carry[:, L:L2] = ci
        y1 = lax.dot_general(h_ref[...].astype(BF16), c_bd[...], nt, preferred_element_type=F32) + d_ref[...] * u
        y_ref[...] = y1
        g_ref[...] = _gelu(y1).astype(BF16)

    return pl.pallas_call(
        body, name=name, grid=(NC, T // tT),
        in_specs=[_bs((tT, CH), lambda c, t: (t, c)), _bs((None, C, L2), lambda c, t: (c, 0, 0)),
                  _bs((None, C, L2), lambda c, t: (c, 0, 0)), _bs((None, 1, L), lambda c, t: (c, 0, 0)),
                  _bs((None, 1, L), lambda c, t: (c, 0, 0)), _bs((1, CH), lambda c, t: (0, c))],
        out_specs=[_bs((tT, CH), lambda c, t: (t, c)), _bs((tT, CH), lambda c, t: (t, c)),
                   _bs((None, tT, L2), lambda c, t: (c, t, 0))],
        out_shape=[jax.ShapeDtypeStruct((T, E), F32), jax.ShapeDtypeStruct((T, E), BF16),
                   jax.ShapeDtypeStruct((NC, T, L2), F32)],
        scratch_shapes=[pltpu.VMEM((tT, L2), F32), pltpu.VMEM((SUB, L2), F32), pltpu.VMEM((8, SUB, L), F32),
                        pltpu.VMEM((CH, L2), BF16), pltpu.VMEM((CH, L2), BF16)],
        compiler_params=_params(("parallel", "arbitrary")),
    )(proj, bbd, cbd, abar_r, abar_i, dskip)


def _s5_bwd(name, dy1, proj, hs, bbd, cbd, abar_r, abar_i, dskip, E):
    T = proj.shape[0]
    NC, C, L2 = bbd.shape
    L = L2 // 2
    CH = GROUPS_PER_CHUNK * C
    tT = _t(256, T)
    nT = T // tT
    tn = (((0,), (0,)), ((), ()))
    nt = (((1,), (1,)), ((), ()))

    def body(dy_ref, u_ref, h_ref, bc_ref, cc_ref, ar_ref, ai_ref, d_ref, du_ref, db_ref, dc_ref, da_ref, dd_ref,
             gb, carry, mult, b_bd, c_bd, db_acc, dc_acc):
        tb = pl.program_id(1)

        @pl.when(tb == 0)
        def _():
            carry[...] = jnp.zeros_like(carry)
            db_acc[...] = jnp.zeros_like(db_acc)
            dc_acc[...] = jnp.zeros_like(dc_acc)
            da_ref[...] = jnp.zeros_like(da_ref)
            dd_ref[...] = jnp.zeros_like(dd_ref)
            _blockdiag_fill(b_bd, bc_ref, C, L)
            _blockdiag_fill(c_bd, cc_ref, C, L)

        dy = dy_ref[...]
        u = u_ref[...]
        dy16 = dy.astype(BF16)
        dc_acc[...] += lax.dot_general(dy16, h_ref[...].astype(BF16), tn, preferred_element_type=F32)
        gb[...] = jnp.dot(dy16, c_bd[...], preferred_element_type=F32)
        _scan_mults(mult, ar_ref[...], -ai_ref[...], True)
        row = lax.broadcasted_iota(jnp.int32, (SUB, L), 0)
        nblk = tT // SUB

        def step(jj, c):
            cr, ci, sr, si = c
            r0 = pl.multiple_of((nblk - 1 - jj) * SUB, SUB)
            gr, gi = _scan8(gb[pl.ds(r0, SUB), 0:L], gb[pl.ds(r0, SUB), L:L2], mult, cr, ci, True)
            gb[pl.ds(r0, SUB), 0:L] = gr
            gb[pl.ds(r0, SUB), L:L2] = gi
            nr = jnp.where(row == SUB - 1, cr, pltpu.roll(gr, SUB - 1, 0))
            ni = jnp.where(row == SUB - 1, ci, pltpu.roll(gi, SUB - 1, 0))
            hr, hi = h_ref[pl.ds(r0, SUB), 0:L], h_ref[pl.ds(r0, SUB), L:L2]
            sr = sr + nr * hr + ni * hi
            si = si + ni * hr - nr * hi
            return (jnp.broadcast_to(gr[0:1, :], (SUB, L)), jnp.broadcast_to(gi[0:1, :], (SUB, L)), sr, si)

        z = jnp.zeros((SUB, L), F32)
        cr, ci, sr, si = lax.fori_loop(0, nblk, step, (carry[:, 0:L], carry[:, L:L2], z, z))
        carry[:, 0:L] = cr
        carry[:, L:L2] = ci
        da_ref[:, 0:L] += sr
        da_ref[:, L:L2] += si
        g16 = gb[...].astype(BF16)
        du = lax.dot_general(g16, b_bd[...], nt, preferred_element_type=F32) + d_ref[...] * dy
        du_ref[...] = du.astype(BF16)
        db_acc[...] += lax.dot_general(u.astype(BF16), g16, tn, preferred_element_type=F32)
        dd_ref[...] += _colsum(dy * u)

        @pl.when(tb == nT - 1)
        def _():
            _blockdiag_take(db_ref, db_acc, C, L)
            _blockdiag_take(dc_ref, dc_acc, C, L)

    rev = lambda c, t: (nT - 1 - t, c)
    return pl.pallas_call(
        body, name=name, grid=(NC, nT),
        in_specs=[_bs((tT, CH), rev), _bs((tT, CH), rev), _bs((None, tT, L2), lambda c, t: (c, nT - 1 - t, 0)),
                  _bs((None, C, L2), lambda c, t: (c, 0, 0)), _bs((None, C, L2), lambda c, t: (c, 0, 0)),
                  _bs((None, 1, L), lambda c, t: (c, 0, 0)), _bs((None, 1, L), lambda c, t: (c, 0, 0)),
                  _bs((1, CH), lambda c, t: (0, c))],
        out_specs=[_bs((tT, CH), rev), _bs((None, C, L2), lambda c, t: (c, 0, 0)), _bs((None, C, L2), lambda c, t: (c, 0, 0)),
                   _bs((None, SUB, L2), lambda c, t: (c, 0, 0)), _bs((None, 1, CH), lambda c, t: (c, 0, 0))],
        out_shape=[jax.ShapeDtypeStruct((T, E), BF16), jax.ShapeDtypeStruct((NC, C, L2), F32),
                   jax.ShapeDtypeStruct((NC, C, L2), F32), jax.ShapeDtypeStruct((NC, SUB, L2), F32),
                   jax.ShapeDtypeStruct((NC, 1, CH), F32)],
        scratch_shapes=[pltpu.VMEM((tT, L2), F32), pltpu.VMEM((SUB, L2), F32), pltpu.VMEM((8, SUB, L), F32),
                        pltpu.VMEM((CH, L2), BF16), pltpu.VMEM((CH, L2), BF16), pltpu.VMEM((CH, L2), F32), pltpu.VMEM((CH, L2), F32)],
        compiler_params=_params(("parallel", "arbitrary")),
    )(dy1, proj, hs, bbd, cbd, abar_r, abar_i, dskip)


def _compact(v, NC):
    G, P, C = v.shape
    return jnp.transpose(v.reshape(NC, G // NC, P, C), (0, 3, 1, 2)).reshape(NC, C, (G // NC) * P)


def _uncompact(d, G):
    NC, C, L = d.shape
    gpc = G // NC
    return jnp.transpose(d.reshape(NC, C, gpc, L // gpc), (0, 2, 3, 1)).reshape(G, L // gpc, C)


def _cum_rows(name, x, bias, reverse, log_sig):
    T, L = x.shape

    def body(x_ref, b_ref, o_ref):
        row = lax.broadcasted_iota(jnp.int32, (SUB, L), 0)
        if reverse:
            row = (SUB - 1) - row
        nblk = T // SUB

        def step(jj, c):
            r0 = pl.multiple_of(((nblk - 1 - jj) if reverse else jj) * SUB, SUB)
            v = x_ref[pl.ds(r0, SUB), :] + b_ref[...]
            if log_sig:
                v = _log_sigmoid(v)
            for d in (1, 2, 4):
                v = v + jnp.where(row >= d, pltpu.roll(v, (SUB - d) if reverse else d, 0), 0.0)
            v = v + c
            o_ref[pl.ds(r0, SUB), :] = v
            e = 0 if reverse else SUB - 1
            return jnp.broadcast_to(v[e:e + 1, :], (SUB, L))

        lax.fori_loop(0, nblk, step, jnp.zeros((SUB, L), F32))

    return pl.pallas_call(body, name=name, out_shape=jax.ShapeDtypeStruct((T, L), F32),
                          compiler_params=pltpu.CompilerParams(vmem_limit_bytes=VMEM_LIMIT))(x, bias)


def _qk_norm(name, proj, wq, wk, H):
    T = proj.shape[0]
    Dh = FOX_HEAD_DIM
    tT = _t(512, T)

    def body(q_ref, k_ref, wq_ref, wk_ref, qn_ref, kn_ref):
        q, k = q_ref[...], k_ref[...]
        qn_ref[...] = ((q * _rms(q)) * wq_ref[...]).astype(BF16)
        kn_ref[...] = ((k * _rms(k)) * wk_ref[...]).astype(BF16)

    blk = lambda off: _bs((tT, Dh), lambda t, h: (t, h + off))
    return pl.pallas_call(
        body, name=name, grid=(T // tT, H),
        in_specs=[blk(0), blk(H), _bs((1, Dh), lambda t, h: (0, 0)), _bs((1, Dh), lambda t, h: (0, 0))],
        out_specs=[blk(0), blk(0)], out_shape=[jax.ShapeDtypeStruct((T, H * Dh), BF16)] * 2,
        compiler_params=_params(("parallel", "parallel")))(proj, proj, wq, wk)


def _qk_norm_bwd(name, proj, wq, wk, dqn, dkn, H):
    T = proj.shape[0]
    Dh = FOX_HEAD_DIM
    tT = _t(512, T)

    def body(q_ref, k_ref, wq_ref, wk_ref, dqn_ref, dkn_ref, dq_ref, dk_ref, dwq_ref, dwk_ref):
        @pl.when((pl.program_id(0) == 0) & (pl.program_id(1) == 0))
        def _():
            dwq_ref[...] = jnp.zeros_like(dwq_ref)
            dwk_ref[...] = jnp.zeros_like(dwk_ref)

        dq, tq = _rms_bwd(q_ref[...], wq_ref[...], dqn_ref[...])
        dk, tk = _rms_bwd(k_ref[...], wk_ref[...], dkn_ref[...])
        dq_ref[...] = dq.astype(BF16)
        dk_ref[...] = dk.astype(BF16)
        dwq_ref[...] += _colsum(tq)
        dwk_ref[...] += _colsum(tk)

    blk = lambda off: _bs((tT, Dh), lambda t, h: (t, h + off))
    one = _bs((1, Dh), lambda t, h: (0, 0))
    return pl.pallas_call(
        body, name=name, grid=(T // tT, H),
        in_specs=[blk(0), blk(H), one, one, blk(0), blk(0)],
        out_specs=[blk(0), blk(0), one, one],
        out_shape=[jax.ShapeDtypeStruct((T, H * Dh), BF16)] * 2 + [jax.ShapeDtypeStruct((1, Dh), F32)] * 2,
        compiler_params=_params(("arbitrary", "arbitrary")))(proj, proj, wq, wk, dqn, dkn)


def _attn_fwd(name, qn, kn, proj, cum_q, cum_k, H):
    T = qn.shape[0]
    Dh = FOX_HEAD_DIM
    tq = cum_k.shape[3]
    nq = T // tq
    scale = Dh ** -0.5
    nt = (((1,), (1,)), ((), ()))

    sq = _t(ATTN_SUB, tq)
    rep = tq // LANES
    HP = ATTN_HEADS
    assert H % HP == 0 and Dh == LANES

    def body(q_ref, k_ref, v_ref, cq_ref, ck_ref, o_ref, lse_ref, m_sc, l_sc, acc_sc):
        i = pl.program_id(1)
        m_sc[...] = jnp.full_like(m_sc, NEG)
        l_sc[...] = jnp.zeros_like(l_sc)
        acc_sc[...] = jnp.zeros_like(acc_sc)
        kloc = lax.broadcasted_iota(jnp.int32, (sq, tq), 1)
        qloc = lax.broadcasted_iota(jnp.int32, (sq, tq), 0)

        def chunk(kc, masked):
            ks = pl.multiple_of(kc * tq, tq)
            for hh in range(HP):
                lanes = slice(hh * Dh, (hh + 1) * Dh)
                k = k_ref[pl.ds(ks, tq), lanes]
                v16 = v_ref[pl.ds(ks, tq), lanes].astype(BF16)
                ck = ck_ref[hh, kc]
                for r in range(tq // sq):
                    rows = pl.ds(r * sq, sq)
                    s = lax.dot_general(q_ref[rows, lanes], k, nt, preferred_element_type=F32) * scale + (jnp.tile(cq_ref[hh, rows, :], (1, rep)) - ck)
                    if masked:
                        s = jnp.where(kloc <= qloc + r * sq, s, NEG)
                    m_old = m_sc[rows, lanes]
                    m_new = jnp.maximum(m_old, jnp.max(s, axis=1, keepdims=True))
                    alpha = jnp.exp(m_old - m_new)
                    p = jnp.exp(s - jnp.tile(m_new, (1, rep)))
                    l_sc[rows, lanes] = alpha * l_sc[rows, lanes] + jnp.sum(p, axis=1, keepdims=True)
                    acc_sc[rows, lanes] = alpha * acc_sc[rows, lanes] + jnp.dot(p.astype(BF16), v16, preferred_element_type=F32)
                    m_sc[rows, lanes] = m_new

        def below(kc, c):
            chunk(kc, False)
            return c

        lax.fori_loop(0, i, below, 0)
        chunk(i, True)
        o_ref[...] = acc_sc[...] / l_sc[...]
        for hh in range(HP):
            lanes = slice(hh * Dh, (hh + 1) * Dh)
            lse_ref[hh] = m_sc[:, lanes] + jnp.log(l_sc[:, lanes])

    W2 = HP * Dh
    return pl.pallas_call(
        body, name=name, grid=(H // HP, nq),
        in_specs=[_bs((tq, W2), lambda h, i: (i, h)), _bs((T, W2), lambda h, i: (0, h)), _bs((T, W2), lambda h, i: (0, 2 * (H // HP) + h)),
                  _bs((HP, tq, LANES), lambda h, i: (h, i, 0)), _bs((HP, nq, 1, tq), lambda h, i: (h, 0, 0, 0))],
        out_specs=[_bs((tq, W2), lambda h, i: (i, h)), _bs((HP, tq, LANES), lambda h, i: (h, i, 0))],
        out_shape=[jax.ShapeDtypeStruct((T, H * Dh), F32), jax.ShapeDtypeStruct((H, T, LANES), F32)],
        scratch_shapes=[pltpu.VMEM((tq, W2), F32), pltpu.VMEM((tq, W2), F32), pltpu.VMEM((tq, W2), F32)],
        compiler_params=_params(("parallel", "parallel")))(qn, kn, proj, cum_q, cum_k)


def _attn_bwd(name, qn, kn, proj, do, o, lse, cum_q, cum_k, H):
    T = qn.shape[0]
    Dh = FOX_HEAD_DIM
    tq = cum_k.shape[3]
    nq = T // tq
    scale = Dh ** -0.5
    nt = (((1,), (1,)), ((), ()))
    tn = (((0,), (0,)), ((), ()))
    assert H <= LANES

    sq = _t(ATTN_SUB, tq)
    rep = tq // LANES

    def body(q_ref, k_ref, v_ref, do_ref, o_ref, lse_ref, cq_ref, ck_ref, dq_ref, dk_ref, dv_ref, dcq_ref, dck_ref,
             delta, cql, dk_sc, dv_sc, dck_sc):
        h, j = pl.program_id(0), pl.program_id(1)

        @pl.when((h == 0) & (j == 0))
        def _():
            dcq_ref[...] = jnp.zeros_like(dcq_ref)

        @pl.when(j == 0)
        def _():
            dq_ref[...] = jnp.zeros_like(dq_ref)
            delta[...] = jnp.broadcast_to(jnp.sum(do_ref[...] * o_ref[...], axis=1, keepdims=True), delta.shape)
            cql[...] = cq_ref[...] - lse_ref[...]

        head_lane = lax.broadcasted_iota(jnp.int32, (sq, LANES), 1) == h

        dk_sc[...] = jnp.zeros_like(dk_sc)
        dv_sc[...] = jnp.zeros_like(dv_sc)
        dck_sc[...] = jnp.zeros_like(dck_sc)
        k = k_ref[...]
        v16 = v_ref[...].astype(BF16)
        ck = ck_ref[...]
        kloc = lax.broadcasted_iota(jnp.int32, (sq, tq), 1)
        qloc = lax.broadcasted_iota(jnp.int32, (sq, tq), 0)

        def qblk(i, masked):
            for r in range(tq // sq):
                rows = pl.ds(pl.multiple_of(i * tq + r * sq, sq), sq)
                q = q_ref[rows, :]
                do16 = do_ref[rows, :].astype(BF16)
                e = lax.dot_general(q, k, nt, preferred_element_type=F32) * scale + (jnp.tile(cql[rows, :], (1, rep)) - ck)
                p = jnp.exp(e)
                if masked:
                    p = jnp.where(kloc <= qloc + r * sq, p, 0.0)
                dv_sc[...] += lax.dot_general(p.astype(BF16), do16, tn, preferred_element_type=F32)
                dp = lax.dot_general(do16, v16, nt, preferred_element_type=F32)
                ds = p * (dp - jnp.tile(delta[rows, :], (1, rep)))
                ds16 = ds.astype(BF16)
                dk_sc[...] += lax.dot_general(ds16, q, tn, preferred_element_type=F32)
                dq_ref[rows, :] += jnp.dot(ds16, k, preferred_element_type=F32) * scale
                dcq_ref[rows, :] += jnp.where(head_lane, jnp.sum(ds, axis=1, keepdims=True), 0.0)
                dck_sc[...] += jnp.sum(ds, axis=0, keepdims=True)

        def above(i, c):
            qblk(i, False)
            return c

        qblk(j, True)
        lax.fori_loop(j + 1, nq, above, 0)
        dk_ref[...] = dk_sc[...] * scale
        dv_ref[...] = dv_sc[...].astype(BF16)
        dck_ref[...] = -dck_sc[...]

    whole = lambda off: _bs((T, Dh), lambda h, j: (0, h + off))
    blk = lambda off: _bs((tq, Dh), lambda h, j: (j, h + off))
    return pl.pallas_call(
        body, name=name, grid=(H, nq),
        in_specs=[whole(0), blk(0), blk(2 * H), whole(0), whole(0), _bs((None, T, LANES), lambda h, j: (h, 0, 0)),
                  _bs((None, T, LANES), lambda h, j: (h, 0, 0)), _bs((None, None, 1, tq), lambda h, j: (h, j, 0, 0))],
        out_specs=[whole(0), blk(0), blk(0), _bs((T, LANES), lambda h, j: (0, 0)),
                   _bs((None, None, 1, tq), lambda h, j: (h, j, 0, 0))],
        out_shape=[jax.ShapeDtypeStruct((T, H * Dh), F32), jax.ShapeDtypeStruct((T, H * Dh), F32), jax.ShapeDtypeStruct((T, H * Dh), BF16),
                   jax.ShapeDtypeStruct((T, LANES), F32), jax.ShapeDtypeStruct((H, nq, 1, tq), F32)],
        scratch_shapes=[pltpu.VMEM((T, LANES), F32), pltpu.VMEM((T, LANES), F32), pltpu.VMEM((tq, Dh), F32), pltpu.VMEM((tq, Dh), F32),
                        pltpu.VMEM((1, tq), F32)],
        compiler_params=_params(("arbitrary", "arbitrary")))(qn, kn, proj, do, o, lse, cum_q, cum_k)


def _pool_fwd(name, proj, E):
    T = proj.shape[0]
    PG = len(POOL_WINDOWS)
    PD = E // PG
    tT = _t(256, T)
    hb = tT // POOL_HALO

    def body(u_ref, halo_ref, o_ref, buf):
        g, tb = pl.program_id(0), pl.program_id(1)
        u = u_ref[...]
        buf[pl.ds(POOL_HALO, tT), :] = u
        buf[pl.ds(0, POOL_HALO), :] = jnp.where(tb == 0, 0.0, halo_ref[...])
        t = tb * tT + lax.broadcasted_iota(jnp.int32, (tT, 1), 0)
        for gi, w in enumerate(POOL_WINDOWS):
            @pl.when(g == gi)
            def _():
                acc = u
                for d in range(1, w):
                    acc = acc + buf[pl.ds(POOL_HALO - d, tT), :]
                cnt = jnp.minimum(t + 1, w).astype(F32)
                o_ref[...] = (acc / cnt - u).astype(BF16)

    return pl.pallas_call(
        body, name=name, grid=(PG, T // tT),
        in_specs=[_bs((tT, PD), lambda g, t: (t, g)), _bs((POOL_HALO, PD), lambda g, t: (jnp.maximum(t * hb - 1, 0), g))],
        out_specs=_bs((tT, PD), lambda g, t: (t, g)), out_shape=jax.ShapeDtypeStruct((T, E), BF16),
        scratch_shapes=[pltpu.VMEM((tT + POOL_HALO, PD), F32)],
        compiler_params=_params(("parallel", "parallel")))(proj, proj)


def _pool_bwd(name, dpm, E):
    T = dpm.shape[0]
    PG = len(POOL_WINDOWS)
    PD = E // PG
    tT = _t(256, T)
    hb = tT // POOL_HALO
    nT = T // tT

    def body(d_ref, halo_ref, o_ref, buf):
        g, tb = pl.program_id(0), pl.program_id(1)
        d = d_ref[...]
        t = tb * tT + lax.broadcasted_iota(jnp.int32, (tT, 1), 0)
        th = (tb + 1) * tT + lax.broadcasted_iota(jnp.int32, (POOL_HALO, 1), 0)
        for gi, w in enumerate(POOL_WINDOWS):
            @pl.when(g == gi)
            def _():
                dn = d / jnp.minimum(t + 1, w).astype(F32)
                buf[pl.ds(0, tT), :] = dn
                buf[pl.ds(tT, POOL_HALO), :] = jnp.where(tb == nT - 1, 0.0, halo_ref[...] / jnp.minimum(th + 1, w).astype(F32))
                acc = dn
                for s in range(1, w):
                    acc = acc + buf[pl.ds(s, tT), :]
                o_ref[...] = (acc - d).astype(BF16)

    return pl.pallas_call(
        body, name=name, grid=(PG, nT),
        in_specs=[_bs((tT, PD), lambda g, t: (t, g)), _bs((POOL_HALO, PD), lambda g, t: (jnp.minimum((t + 1) * hb, T // POOL_HALO - 1), g))],
        out_specs=_bs((tT, PD), lambda g, t: (t, g)), out_shape=jax.ShapeDtypeStruct((T, E), BF16),
        scratch_shapes=[pltpu.VMEM((tT + POOL_HALO, PD), F32)],
        compiler_params=_params(("parallel", "parallel")))(dpm, dpm)


def _coords():
    x, y, c = lax.axis_index("x"), lax.axis_index("y"), lax.axis_index("c")
    chips = [(1 - x, y), (x, 1 - y), (1 - x, 1 - y)]
    return x, y, c, 2 * x + y, (x, y, 1 - c), chips


def _chip_allgather(name, bufs):
    n = len(bufs)

    def body(*refs):
        outs = refs[n:2 * n]
        send, recv, fsend, frecv = refs[2 * n:]
        x, y, c, p, sib, chips = _coords()

        def direct(t, j, chip):
            return pltpu.make_async_remote_copy(src_ref=outs[t].at[p, c], dst_ref=outs[t].at[p, c], send_sem=send.at[t, j],
                                                recv_sem=recv.at[t, j], device_id=(*chip, c), device_id_type=MESH)

        def landed(t, j, chip):
            blk = outs[t].at[2 * chip[0] + chip[1], c]
            return pltpu.make_async_remote_copy(src_ref=blk, dst_ref=blk, send_sem=send.at[t, j],
                                                recv_sem=recv.at[t, j], device_id=(*chip, c), device_id_type=MESH)

        def passed(t, j, chip, half):
            blk = outs[t].at[2 * chip[0] + chip[1], half]
            return pltpu.make_async_remote_copy(src_ref=blk, dst_ref=blk, send_sem=fsend.at[t, j], recv_sem=frecv.at[t, j],
                                                device_id=sib, device_id_type=MESH)

        first = [direct(t, j, chip) for t in range(n) for j, chip in enumerate(chips)]
        for cp in first:
            cp.start()
        fwd = []
        for j, chip in enumerate(chips):
            for t in range(n):
                landed(t, j, chip).wait_recv()
                f = passed(t, j, chip, c)
                f.start()
                fwd.append(f)
        for j, chip in enumerate(chips):
            for t in range(n):
                passed(t, j, chip, 1 - c).wait_recv()
        for cp in first + fwd:
            cp.wait_send()

    return pl.pallas_call(
        body, name=name, in_specs=[ANY] * n, out_specs=[ANY] * n,
        out_shape=[jax.ShapeDtypeStruct(a.shape, a.dtype) for a in bufs],
        input_output_aliases={t: t for t in range(n)},
        scratch_shapes=[pltpu.SemaphoreType.DMA((n, 3))] * 4,
    )(*bufs)


SEM = pl.BlockSpec(memory_space=pltpu.SEMAPHORE)
TOKEN = jax.ShapeDtypeStruct((SUB, LANES), F32)


def _split_params():
    return pltpu.CompilerParams(has_side_effects=pltpu.SideEffectType.DATAFLOW_SIDE_EFFECTING)


def _struct(a):
    return jax.ShapeDtypeStruct(a.shape, a.dtype)


def _gather_start(name, bufs, deps):
    n, nd = len(bufs), len(deps)

    def body(*refs):
        outs = refs[n + nd:2 * n + nd]
        send, recv, token = refs[2 * n + nd:]
        x, y, c, p, sib, chips = _coords()
        for t in range(n):
            for j, chip in enumerate(chips):
                pltpu.make_async_remote_copy(src_ref=outs[t].at[p, c], dst_ref=outs[t].at[p, c], send_sem=send.at[3 * t + j],
                                             recv_sem=recv.at[3 * t + j], device_id=(*chip, c), device_id_type=MESH).start()
        token[...] = jnp.zeros_like(token)

    res = pl.pallas_call(
        body, name=name, in_specs=[ANY] * (n + nd), out_specs=[ANY] * n + [SEM, SEM, pl.BlockSpec(memory_space=pltpu.VMEM)],
        out_shape=[_struct(a) for a in bufs] + [pltpu.SemaphoreType.DMA((3 * n,)), pltpu.SemaphoreType.DMA((3 * n,)), TOKEN],
        input_output_aliases={t: t for t in range(n)}, compiler_params=_split_params(),
    )(*bufs, *deps)
    return list(res[:n]), res[n], res[n + 1], res[n + 2]


def _gather_wait(name, bufs, send, recv, after):
    n = len(bufs)

    def body(*refs):
        send_r, recv_r = refs[n], refs[n + 1]
        outs = refs[n + 3:2 * n + 3]
        x, y, c, p, sib, chips = _coords()
        for t in range(n):
            for j, chip in enumerate(chips):
                cp = pltpu.make_async_remote_copy(src_ref=outs[t].at[p, c], dst_ref=outs[t].at[2 * chip[0] + chip[1], c], send_sem=send_r.at[3 * t + j],
                                                  recv_sem=recv_r.at[3 * t + j], device_id=(*chip, c), device_id_type=MESH)
                cp.wait_send()
                cp.wait_recv()

    return list(pl.pallas_call(
        body, name=name, in_specs=[ANY] * n + [SEM, SEM, ANY], out_specs=[ANY] * n, out_shape=[_struct(a) for a in bufs],
        input_output_aliases={t: t for t in range(n)}, compiler_params=_split_params(),
    )(*bufs, send, recv, after))


def _gather_forward(name, bufs):
    n = len(bufs)

    def body(*refs):
        outs = refs[n:2 * n]
        fsend, frecv = refs[2 * n:]
        x, y, c, p, sib, chips = _coords()

        def passed(t, j, chip, half):
            blk = outs[t].at[2 * chip[0] + chip[1], half]
            return pltpu.make_async_remote_copy(src_ref=blk, dst_ref=blk, send_sem=fsend.at[t, j], recv_sem=frecv.at[t, j],
                                                device_id=sib, device_id_type=MESH)

        fwd = [passed(t, j, chip, c) for t in range(n) for j, chip in enumerate(chips)]
        for cp in fwd:
            cp.start()
        for t in range(n):
            for j, chip in enumerate(chips):
                passed(t, j, chip, 1 - c).wait_recv()
        for cp in fwd:
            cp.wait_send()

    return list(pl.pallas_call(
        body, name=name, in_specs=[ANY] * n, out_specs=[ANY] * n, out_shape=[_struct(a) for a in bufs],
        input_output_aliases={t: t for t in range(n)}, scratch_shapes=[pltpu.SemaphoreType.DMA((n, 3))] * 2,
    )(*bufs))


def _chip_exchange_start(name, sums):
    n = len(sums)
    lands = [lax.empty((3,) + a.shape[1:], a.dtype) for a in sums]

    def body(*refs):
        src, dst = refs[2 * n:3 * n], refs[3 * n:4 * n]
        send, recv, token = refs[4 * n:]
        x, y, c, p, sib, chips = _coords()
        for t in range(n):
            for j, chip in enumerate(chips):
                pltpu.make_async_remote_copy(src_ref=src[t].at[2 * chip[0] + chip[1]], dst_ref=dst[t].at[j], send_sem=send.at[3 * t + j],
                                             recv_sem=recv.at[3 * t + j], device_id=(*chip, c), device_id_type=MESH).start()
        token[...] = jnp.zeros_like(token)

    res = pl.pallas_call(
        body, name=name, in_specs=[ANY] * (2 * n), out_specs=[ANY] * (2 * n) + [SEM, SEM, pl.BlockSpec(memory_space=pltpu.VMEM)],
        out_shape=[_struct(a) for a in sums + lands] + [pltpu.SemaphoreType.DMA((3 * n,)), pltpu.SemaphoreType.DMA((3 * n,)), TOKEN],
        input_output_aliases={t: t for t in range(2 * n)}, compiler_params=_split_params(),
    )(*sums, *lands)
    return list(res[:n]), list(res[n:2 * n]), res[2 * n], res[2 * n + 1], res[2 * n + 2]


def _chip_exchange_wait(name, sums, lands, send, recv, after):
    n = len(sums)

    def body(*refs):
        send_r, recv_r = refs[2 * n], refs[2 * n + 1]
        src, dst = refs[2 * n + 3:3 * n + 3], refs[3 * n + 3:4 * n + 3]
        x, y, c, p, sib, chips = _coords()
        for t in range(n):
            for j, chip in enumerate(chips):
                cp = pltpu.make_async_remote_copy(src_ref=src[t].at[2 * chip[0] + chip[1]], dst_ref=dst[t].at[j], send_sem=send_r.at[3 * t + j],
                                                  recv_sem=recv_r.at[3 * t + j], device_id=(*chip, c), device_id_type=MESH)
                cp.wait_send()
                cp.wait_recv()

    res = pl.pallas_call(
        body, name=name, in_specs=[ANY] * (2 * n) + [SEM, SEM, ANY], out_specs=[ANY] * (2 * n),
        out_shape=[_struct(a) for a in sums + lands], input_output_aliases={t: t for t in range(2 * n)},
        compiler_params=_split_params(),
    )(*sums, *lands, send, recv, after)
    return list(res[:n]), list(res[n:])


def _pair_exchange(name, parts):
    n = len(parts)

    def body(*refs):
        ins, outs = refs[:n], refs[n:2 * n]
        send, recv = refs[2 * n:]
        x, y, c, p, sib, chips = _coords()
        cps = [pltpu.make_async_remote_copy(src_ref=ins[t].at[1 - c], dst_ref=outs[t], send_sem=send.at[t], recv_sem=recv.at[t],
                                            device_id=sib, device_id_type=MESH) for t in range(n)]
        for cp in cps:
            cp.start()
        for cp in cps:
            cp.wait()

    return pl.pallas_call(
        body, name=name, in_specs=[ANY] * n, out_specs=[ANY] * n,
        out_shape=[jax.ShapeDtypeStruct(a.shape[1:], a.dtype) for a in parts],
        scratch_shapes=[pltpu.SemaphoreType.DMA((n,))] * 2,
    )(*parts)


def _chip_exchange(name, sums):
    n = len(sums)

    def body(*refs):
        ins, outs = refs[:n], refs[n:2 * n]
        send, recv = refs[2 * n:]
        x, y, c, p, sib, chips = _coords()
        cps = [pltpu.make_async_remote_copy(src_ref=ins[t].at[2 * chip[0] + chip[1]], dst_ref=outs[t].at[j], send_sem=send.at[t, j],
                                            recv_sem=recv.at[t, j], device_id=(*chip, c), device_id_type=MESH)
               for t in range(n) for j, chip in enumerate(chips)]
        for cp in cps:
            cp.start()
        for cp in cps:
            cp.wait()

    return pl.pallas_call(
        body, name=name, in_specs=[ANY] * n, out_specs=[ANY] * n,
        out_shape=[jax.ShapeDtypeStruct((3,) + a.shape[1:], a.dtype) for a in sums],
        scratch_shapes=[pltpu.SemaphoreType.DMA((n, 3))] * 2,
    )(*sums)


def _pair_share(name, bufs, items, deps=()):
    n = len(items)
    nb = len(bufs)
    nd = len(deps)

    def body(*refs):
        outs = refs[nb + nd:2 * nb + nd]
        send, recv = refs[2 * nb + nd:]
        x, y, c, p, sib, chips = _coords()

        def blk(t, half):
            o, lead = items[t]
            return outs[o].at[p if lead == 'chip' else lead, half]

        def swap(t, half):
            return pltpu.make_async_remote_copy(src_ref=blk(t, half), dst_ref=blk(t, half), send_sem=send.at[t], recv_sem=recv.at[t],
                                                device_id=sib, device_id_type=MESH)

        cps = [swap(t, c) for t in range(n)]
        for cp in cps:
            cp.start()
        for t in range(n):
            swap(t, 1 - c).wait_recv()
        for cp in cps:
            cp.wait_send()

    return list(pl.pallas_call(
        body, name=name, in_specs=[ANY] * (nb + nd), out_specs=[ANY] * nb,
        out_shape=[jax.ShapeDtypeStruct(b.shape, b.dtype) for b in bufs],
        input_output_aliases={t: t for t in range(nb)},
        scratch_shapes=[pltpu.SemaphoreType.DMA((n,))] * 2,
    )(*bufs, *deps))


def _flat2(a, lead):
    return a.reshape(a.shape[:lead] + (-1, a.shape[-1]))


def _reduce_begin(tag, parts):
    c = lax.axis_index("c").astype(jnp.int32)
    got = _pair_exchange(f"rs_pair_exchange_{tag}", parts)
    sums = []
    for t, (mine, theirs) in enumerate(zip(parts, got)):
        m3, t2 = _flat2(mine, 1), theirs.reshape(-1, theirs.shape[-1])
        m3 = m3.reshape(2, -1, m3.shape[-1])
        cols = t2.shape[1]
        s = _rows(f"rs_pair_sum_{tag}_{t}", lambda a, b: (a.astype(F32) + b.astype(F32),),
                  [(m3, 's', cols, 0), (t2, 'r', cols, 0)], [('r', cols, BF16)], 512, pre=c.reshape(1))[0]
        sums.append(s.reshape(theirs.shape))
    sums, lands, send, recv, token = _chip_exchange_start(f"rs_chip_start_{tag}", sums)
    return (sums, lands, send, recv), token


def _reduce_end(tag, state, after, dests, bufs, buf_shapes):
    c = lax.axis_index("c").astype(jnp.int32)
    p = (2 * lax.axis_index("x") + lax.axis_index("y")).astype(jnp.int32)
    sums, lands = _chip_exchange_wait(f"rs_chip_wait_{tag}", *state, after)
    for t, (mine, theirs) in enumerate(zip(sums, lands)):
        o, lead = dests[t]
        shape = buf_shapes[o]
        rows, cols = shape[2], shape[3]
        m3, t3 = mine.reshape(N_CHIPS, rows, cols), theirs.reshape(3, rows, cols)
        pre = jnp.stack([p, jnp.int32(0), jnp.int32(1), jnp.int32(2), c, p if lead == 'chip' else jnp.int32(lead)])
        out = ('x', shape, F32, (None, None, 'tr', cols), lambda r, pr: (pr[5], pr[4], r, 0))
        bufs[o] = _rows(f"rs_chip_sum_{tag}_{t}", lambda a, b0, b1, b2: (((a.astype(F32) + b0.astype(F32)) + b1.astype(F32)) + b2.astype(F32),),
                        [(m3, 's', cols, 0), (t3, 's', cols, 1), (t3, 's', cols, 2), (t3, 's', cols, 3)], [out], 512, pre=pre, into=bufs[o])[0]


def kernel(x, norm_w, out_proj, s5_in_proj, s5_a_re, s5_a_im, s5_log_dt, s5_b_re, s5_b_im, s5_c_re, s5_c_im, s5_d, s5_w_glu, s5_b_glu, fox_in_proj, fox_q_norm, fox_k_norm, fox_f_bias, pool_in_proj, pool_w_group, pool_scale, loss_target, m_norm_w, m_out_proj, m_s5_in_proj, m_s5_a_re, m_s5_a_im, m_s5_log_dt, m_s5_b_re, m_s5_b_im, m_s5_c_re, m_s5_c_im, m_s5_d, m_s5_w_glu, m_s5_b_glu, m_fox_in_proj, m_fox_q_norm, m_fox_k_norm, m_fox_f_bias, m_pool_in_proj, m_pool_w_group, m_pool_scale, v_norm_w, v_out_proj, v_s5_in_proj, v_s5_a_re, v_s5_a_im, v_s5_log_dt, v_s5_b_re, v_s5_b_im, v_s5_c_re, v_s5_c_im, v_s5_d, v_s5_w_glu, v_s5_b_glu, v_fox_in_proj, v_fox_q_norm, v_fox_k_norm, v_fox_f_bias, v_pool_in_proj, v_pool_w_group, v_pool_scale):
    weights = dict(norm_w=norm_w, out_proj=out_proj, s5_in_proj=s5_in_proj, s5_a_re=s5_a_re, s5_a_im=s5_a_im, s5_log_dt=s5_log_dt,
                   s5_b_re=s5_b_re, s5_b_im=s5_b_im, s5_c_re=s5_c_re, s5_c_im=s5_c_im, s5_d=s5_d, s5_w_glu=s5_w_glu, s5_b_glu=s5_b_glu,
                   fox_in_proj=fox_in_proj, fox_q_norm=fox_q_norm, fox_k_norm=fox_k_norm, fox_f_bias=fox_f_bias,
                   pool_in_proj=pool_in_proj, pool_w_group=pool_w_group, pool_scale=pool_scale)
    mom_m = dict(norm_w=m_norm_w, out_proj=m_out_proj, s5_in_proj=m_s5_in_proj, s5_a_re=m_s5_a_re, s5_a_im=m_s5_a_im, s5_log_dt=m_s5_log_dt,
                 s5_b_re=m_s5_b_re, s5_b_im=m_s5_b_im, s5_c_re=m_s5_c_re, s5_c_im=m_s5_c_im, s5_d=m_s5_d, s5_w_glu=m_s5_w_glu, s5_b_glu=m_s5_b_glu,
                 fox_in_proj=m_fox_in_proj, fox_q_norm=m_fox_q_norm, fox_k_norm=m_fox_k_norm, fox_f_bias=m_fox_f_bias,
                 pool_in_proj=m_pool_in_proj, pool_w_group=m_pool_w_group, pool_scale=m_pool_scale)
    mom_v = dict(norm_w=v_norm_w, out_proj=v_out_proj, s5_in_proj=v_s5_in_proj, s5_a_re=v_s5_a_re, s5_a_im=v_s5_a_im, s5_log_dt=v_s5_log_dt,
                 s5_b_re=v_s5_b_re, s5_b_im=v_s5_b_im, s5_c_re=v_s5_c_re, s5_c_im=v_s5_c_im, s5_d=v_s5_d, s5_w_glu=v_s5_w_glu, s5_b_glu=v_s5_b_glu,
                 fox_in_proj=v_fox_in_proj, fox_q_norm=v_fox_q_norm, fox_k_norm=v_fox_k_norm, fox_f_bias=v_fox_f_bias,
                 pool_in_proj=v_pool_in_proj, pool_w_group=v_pool_w_group, pool_scale=v_pool_scale)
    return _step(x, loss_target, weights, mom_m, mom_v)


BIG = ('out_proj', 's5_in_proj', 's5_w_glu', 'fox_in_proj', 'pool_in_proj', 'pool_w_group')
SMALL = ('norm_w', 's5_a_re', 's5_a_im', 's5_log_dt', 's5_b_re', 's5_b_im', 's5_c_re', 's5_c_im', 's5_d', 's5_b_glu',
         'fox_q_norm', 'fox_k_norm', 'fox_f_bias', 'pool_scale')
SMALL_SHARDED = ('s5_d', 's5_b_glu', 'pool_scale')
GROUP_AXIS_1 = ('s5_a_re', 's5_a_im', 's5_b_re', 's5_b_im', 's5_c_re', 's5_c_im')
ORDER = ('norm_w', 'out_proj', 's5_in_proj', 's5_a_re', 's5_a_im', 's5_log_dt', 's5_b_re', 's5_b_im', 's5_c_re', 's5_c_im', 's5_d',
         's5_w_glu', 's5_b_glu', 'fox_in_proj', 'fox_q_norm', 'fox_k_norm', 'fox_f_bias', 'pool_in_proj', 'pool_w_group', 'pool_scale')


def _split2(shape):
    if shape[0] % 2 == 0:
        return (2, shape[0] // 2) + tuple(shape[1:])
    assert shape[0] == 1 and shape[1] % 2 == 0
    return (2, shape[1] // 2) + tuple(shape[2:])


def _adamw_big(n, w, grads, mom_m, mom_v, delta, new_m, new_v):
    shape = w[n].shape
    if shape[-1] % LANES:
        f2 = lambda a: jnp.transpose(a.reshape(-1, shape[-1]))
        b2 = lambda a: jnp.transpose(a).reshape(shape)
    else:
        f2 = lambda a: a.reshape(-1, shape[-1])
        b2 = lambda a: a.reshape(shape)
    d_, m_, v_ = _adamw(f"adamw_{n}", f2(w[n]), f2(grads[n]), f2(mom_m[n]), f2(mom_v[n]))
    delta[n], new_m[n], new_v[n] = b2(d_), b2(m_), b2(v_)
    return d_


def _cast_weights(w):
    p = (2 * lax.axis_index("x") + lax.axis_index("y")).astype(jnp.int32)
    bufs = {}
    for n in BIG:
        a3 = w[n].reshape(w[n].shape[0], -1, w[n].shape[-1])
        layers, rows, cols = a3.shape
        for l in range(layers):
            out = ('x', (N_CHIPS, rows, cols), BF16, (None, 'tr', cols), lambda r, pr: (pr[0], r, 0))
            b = _rows(f"cast_{n}_{l}", lambda v: (v,), [(a3, 's', cols, 1)], [out], 256, pre=jnp.stack([p, jnp.int32(l)]))[0]
            bufs[(n, l)] = b.reshape(N_CHIPS, 2, rows // 2, cols)
    return bufs


def _step(x, loss_target, w, mom_m, mom_v):
    T, D = x.shape[1], x.shape[2]
    E = D
    G, P, C = w['s5_a_re'].shape[1], S5_STATE, S5_GROUP
    H = E // FOX_HEAD_DIM
    PG = len(POOL_WINDOWS)
    PD = E // PG
    NC = G // GROUPS_PER_CHUNK
    L = GROUPS_PER_CHUNK * P
    tq = _t(256, T)
    nq = T // tq

    wb = _cast_weights(w)
    phases = [[('s5_in_proj', 0)],
              [('s5_w_glu', 0), ('out_proj', 0)],
              [('out_proj', 1), ('fox_in_proj', 0)],
              [('out_proj', 2), ('pool_in_proj', 0), ('pool_w_group', 0), ('out_proj', 3), ('s5_in_proj', 1), ('s5_w_glu', 1)]]
    W = {}
    flight = {}

    def landed(keys, bufs):
        for k, b in zip(keys, bufs):
            W[k] = b.reshape(N_CHIPS, 2 * b.shape[2], b.shape[3])

    def take_phase(ph, after):
        bufs, send, recv, _ = flight.pop(ph)
        landed(phases[ph], _gather_forward(f"gather_{ph}_pass", _gather_wait(f"gather_{ph}_wait", bufs, send, recv, after)))

    small_full = {}
    chip = 2 * lax.axis_index("x") + lax.axis_index("y")
    sv = [lax.dynamic_update_index_in_dim(jnp.zeros((N_CHIPS, 2) + w[n].shape, F32), jnp.stack([w[n], w[n]]), chip, 0)
          for n in SMALL_SHARDED]
    got = _chip_allgather("gather_vectors", sv)
    for n, g in zip(SMALL_SHARDED, got):
        small_full[n] = jnp.transpose(g[:, 0], (1, 0, 2)).reshape(w[n].shape[0], E)
    landed(phases[0], _chip_allgather("gather_0", [wb[k] for k in phases[0]]))
    after = [W[phases[0][0]], got[0]]
    for ph in range(1, len(phases)):
        flight[ph] = _gather_start(f"gather_{ph}_start", [wb[k] for k in phases[ph]], after)
        after = [flight[ph][3]]
    gather_tokens = after

    norm_w = w['norm_w']
    h = x.reshape(T, D)
    saved = []
    dparts = {}

    def s5_consts(j):
        ar, ai, fr, fi = _s5_disc_fwd(f"s5_disc_{j}", w['s5_a_re'][j], w['s5_a_im'][j], w['s5_log_dt'][j].reshape(G, 1))
        br, bi = w['s5_b_re'][j].reshape(G * P, C), w['s5_b_im'][j].reshape(G * P, C)
        bbr, bbi = _s5_bbar(f"s5_bbar_{j}", fr.reshape(G * P, 1), fi.reshape(G * P, 1), br, bi)
        bbd = jnp.concatenate([_compact(bbr.reshape(G, P, C), NC), _compact(bbi.reshape(G, P, C), NC)], axis=2).astype(BF16)
        ct = lambda v: jnp.transpose(v, (0, 2, 1))
        cbd = jnp.concatenate([_compact(ct(w['s5_c_re'][j]), NC), -_compact(ct(w['s5_c_im'][j]), NC)], axis=2).astype(BF16)
        return dict(ar=ar, ai=ai, fr=fr, fi=fi, br=br, bi=bi, bbd=bbd, cbd=cbd,
                    ar3=ar.reshape(NC, 1, L), ai3=ai.reshape(NC, 1, L))

    for i in range(4):
        kind, j = i % 3, i // 3
        nw = norm_w[i].reshape(1, D)
        xn = _norm_fwd(f"norm_{i}", h, nw, deps=gather_tokens if i == 0 else ())
        if kind == 0:
            k5 = s5_consts(j)
            proj = _mm_proj(f"s5_proj_{i}", xn, W[('s5_in_proj', j)])
            dsk = small_full['s5_d'][j].reshape(1, E)
            y1, g, hs = _s5_fwd(f"s5_scan_{i}", proj, k5['bbd'], k5['cbd'], k5['ar3'], k5['ai3'], dsk, E)
            bglu = small_full['s5_b_glu'][j].reshape(1, E)
            if i == 0:
                take_phase(1, y1)

            def glu_epi(acc, b, y1t, z):
                lin = acc + b
                return lin, (_gelu(y1t) * _sigmoid(lin)) * _silu(z)

            lin, a = _mm_rowsharded(
                f"s5_glu_{i}", g, W[('s5_w_glu', j)], epi=glu_epi,
                extras=lambda tm, tn: [(bglu, _rowvec(tn)), (y1, _tile(tm, tn)), (proj, _tile(tm, tn, E // tn))],
                outs_fn=lambda tm, tn: [((T, E), F32, _tile(tm, tn)), ((T, E), BF16, _tile(tm, tn))])
            saved.append(dict(h=h, xn=xn, proj=proj, y1=y1, g=g, hs=hs, lin=lin, a=a, k5=k5, dsk=dsk))
        elif kind == 1:
            fox_w = jnp.transpose(W[('fox_in_proj', j)], (1, 0, 2)).reshape(D, -1)
            w_qkvz = fox_w[:, :4 * E]
            w_f = jnp.pad(fox_w[:, 4 * E:], ((0, 0), (0, LANES - H)))
            proj = _mm_plain(f"fox_proj_{i}", xn, w_qkvz)[0]
            flog = _mm_plain(f"fox_gate_proj_{i}", xn, w_f)[0]
            fb = jnp.pad(w['fox_f_bias'][j].reshape(1, H), ((0, 0), (0, LANES - H)))
            wq, wk = w['fox_q_norm'][j].reshape(1, FOX_HEAD_DIM), w['fox_k_norm'][j].reshape(1, FOX_HEAD_DIM)
            qn, kn = _qk_norm(f"fox_qk_norm_{i}", proj, wq, wk, H)
            cum = _cum_rows(f"fox_cum_{i}", flog, fb, False, True)
            cum_t = jnp.transpose(cum)[:H]
            cum_q = jnp.broadcast_to(cum_t[:, :, None], (H, T, LANES))
            cum_k = cum_t.reshape(H, nq, 1, tq)
            y, lse = _attn_fwd(f"fox_attn_{i}", qn, kn, proj, cum_q, cum_k, H)
            a = _rows(f"fox_gate_{i}", lambda yt, z: (yt * _silu(z),), [(y, 'r', E, 0), (proj, 'r', E, 3)], [('r', E, BF16)], 256)[0]
            saved.append(dict(h=h, xn=xn, proj=proj, flog=flog, fb=fb, wq=wq, wk=wk, qn=qn, kn=kn, cum_q=cum_q, cum_k=cum_k, y=y, lse=lse, a=a,
                              w_qkvz=w_qkvz, w_f=w_f))
        else:
            w_pg = W[('pool_w_group', j)].reshape(N_CHIPS, PG, PD // N_CHIPS, PD)
            proj = _mm_proj(f"pool_proj_{i}", xn, W[('pool_in_proj', j)])
            pm = _pool_fwd(f"pool_win_{i}", proj, E)
            scale = small_full['pool_scale'][j].reshape(1, E)
            tm, tn, tk = _t(512, T), _t(512, PD), w_pg.shape[2]
            kb, nb = PD // tk, PD // tn
            mixed, a = _mm(
                f"pool_mix_{i}", pm, w_pg, M=T, N=PD, K=PD, tm=tm, tn=tn, tk=tk, groups=PG,
                a_spec=_bs((tm, tk), lambda g, m, n, k: (m, g * kb + k)),
                b_spec=_bs((None, None, tk, tn), lambda g, m, n, k: (k, g, 0, n)),
                extras=[(scale, _bs((1, tn), lambda g, m, n, k: (0, g * nb + n))),
                        (proj, _bs((tm, tn), lambda g, m, n, k: (m, E // tn + g * nb + n)))],
                epi=lambda acc, sc, z: (acc, (acc * sc) * _silu(z)),
                outs=[((T, E), F32, _bs((tm, tn), lambda g, m, n, k: (m, g * nb + n))),
                      ((T, E), BF16, _bs((tm, tn), lambda g, m, n, k: (m, g * nb + n)))])
            saved.append(dict(h=h, xn=xn, proj=proj, pm=pm, mixed=mixed, scale=scale, a=a, w_pg=w_pg))
        h = _mm_rowsharded(f"out_proj_{i}", saved[-1]['a'], W[('out_proj', i)], epi=lambda acc, r: (r + acc,),
                           extras=lambda tm, tn: [(h, _tile(tm, tn))],
                           outs_fn=lambda tm, tn: [((T, D), F32, _tile(tm, tn))])[0]
        if i < 2:
            take_phase(i + 2, h)

    dh, dh16, loss_cols = _loss(h, loss_target.reshape(T, D))
    loss = lax.psum(jnp.sum(loss_cols), ("x", "y", "c"))

    gsmall = {n: [None] * w[n].shape[0] for n in SMALL}
    big_index = {n: o for o, n in enumerate(BIG)}
    rs_shapes = [None] * (len(BIG) + 1)
    rs_bufs = [None] * (len(BIG) + 1)
    rs_dests_all = []
    pending = None

    def reduce_layer(tag, named_parts):
        parts, dests = [], []
        for n, l, pt in named_parts:
            o = big_index[n] if n in big_index else len(BIG)
            half = pt.shape[2:]
            rs_shapes[o] = (N_CHIPS if l == 'chip' else w[n].shape[0], 2, math.prod(half[:-1]), half[-1])
            parts.append(pt)
            dests.append((o, l))
        rs_dests_all.extend(dests)
        state, token = _reduce_begin(tag, parts)
        return (tag, state, dests), token

    token = None
    for i in reversed(range(4)):
        kind, j = i % 3, i // 3
        sv_ = saved[i]
        nw = norm_w[i].reshape(1, D)
        w_out = W[('out_proj', i)]
        after_start = [token] if token is not None else ()
        layer_parts = [('out_proj', i, _mm_dw_rows(f"d_out_proj_{i}", sv_['a'], dh16, deps=after_start))]
        if kind == 0:
            w_glu = W[('s5_w_glu', j)]
            proj, y1, lin, k5 = sv_['proj'], sv_['y1'], sv_['lin'], sv_['k5']

            def da_epi(da, y1t, lint, z):
                gt, sg = _gelu(y1t), _sigmoid(lint)
                dy2 = da * _silu(z)
                dlin = (dy2 * gt) * (sg * (1.0 - sg))
                return da * (gt * sg) * _dsilu(z), dlin, dy2 * sg, _colsum(dlin)

            nm = T // _t(512, T)
            dz, dlin, dgd, dbg = _mm_rowsharded_t(
                f"d_s5_act_{i}", dh16, w_out, epi=da_epi, deps=after_start,
                extras=lambda tm, tn: [(y1, _tile(tm, tn)), (lin, _tile(tm, tn)), (proj, _tile(tm, tn, E // tn))],
                outs_fn=lambda tm, tn: [((T, E), BF16, _tile(tm, tn)), ((T, E), BF16, _tile(tm, tn)), ((T, E), F32, _tile(tm, tn)),
                                        ((nm, 1, E), F32, _bs((None, 1, tn), lambda g, m, n, k: (m, 0, n)))])
            gsmall['s5_b_glu'][j] = jnp.sum(dbg, axis=(0, 1))
            layer_parts.append(('s5_w_glu', j, _mm_dw_rows(f"d_s5_w_glu_{i}", sv_['g'], dlin)))
            glu_deps = ()
            if i == 0:
                early, early_token = reduce_layer("l0a", layer_parts)
                layer_parts, glu_deps = [], [early_token]
            dy1 = _mm_rowsharded_t(
                f"d_s5_glu_{i}", dlin, w_glu, epi=lambda acc, d, y1t: ((acc + d) * _dgelu(y1t),), deps=glu_deps,
                extras=lambda tm, tn: [(dgd, _tile(tm, tn)), (y1, _tile(tm, tn))],
                outs_fn=lambda tm, tn: [((T, E), F32, _tile(tm, tn))])[0]
            du, dbd, dcd, dab, ddk = _s5_bwd(f"d_s5_scan_{i}", dy1, proj, sv_['hs'], k5['bbd'], k5['cbd'], k5['ar3'], k5['ai3'], sv_['dsk'], E)
            gsmall['s5_d'][j] = ddk.reshape(E)
            gsmall['s5_c_re'][j] = jnp.transpose(_uncompact(dcd[:, :, :L], G), (0, 2, 1))
            gsmall['s5_c_im'][j] = -jnp.transpose(_uncompact(dcd[:, :, L:], G), (0, 2, 1))
            dbbr = _uncompact(dbd[:, :, :L], G).reshape(G * P, C)
            dbbi = _uncompact(dbd[:, :, L:], G).reshape(G * P, C)
            dbr, dbi, dfr, dfi = _s5_bbar_bwd(f"d_s5_bbar_{i}", k5['fr'].reshape(G * P, 1), k5['fi'].reshape(G * P, 1), k5['br'], k5['bi'], dbbr, dbbi)
            gsmall['s5_b_re'][j] = dbr.reshape(G, P, C)
            gsmall['s5_b_im'][j] = dbi.reshape(G, P, C)
            dab = jnp.sum(dab, axis=1)
            dare, daim, dldt = _s5_disc_bwd(f"d_s5_disc_{i}", w['s5_a_re'][j], w['s5_a_im'][j], w['s5_log_dt'][j].reshape(G, 1),
                                            (dab[:, :L].reshape(G, P), dab[:, L:].reshape(G, P), dfr.reshape(G, P), dfi.reshape(G, P)))
            gsmall['s5_a_re'][j], gsmall['s5_a_im'][j], gsmall['s5_log_dt'][j] = dare, daim, dldt.reshape(G)
            dproj = jnp.concatenate([du, dz], axis=1)
            layer_parts.append(('s5_in_proj', j, _mm_dw_cols(f"d_s5_in_proj_{i}", sv_['xn'], dproj)))
            dxn = _mm_colsharded_t(f"d_s5_xn_{i}", dproj, W[('s5_in_proj', j)])
        elif kind == 1:
            proj, y = sv_['proj'], sv_['y']
            do, dz = _mm_rowsharded_t(
                f"d_fox_act_{i}", dh16, w_out, epi=lambda da, yt, z: (da * _silu(z), (da * yt) * _dsilu(z)), deps=after_start,
                extras=lambda tm, tn: [(y, _tile(tm, tn)), (proj, _tile(tm, tn, 3 * E // tn))],
                outs_fn=lambda tm, tn: [((T, E), F32, _tile(tm, tn)), ((T, E), BF16, _tile(tm, tn))])
            dqn, dkn, dv, dcq, dck = _attn_bwd(f"d_fox_attn_{i}", sv_['qn'], sv_['kn'], proj, do, y, sv_['lse'], sv_['cum_q'], sv_['cum_k'], H)
            dq, dk, dwq, dwk = _qk_norm_bwd(f"d_fox_qk_norm_{i}", proj, sv_['wq'], sv_['wk'], dqn, dkn, H)
            gsmall['fox_q_norm'][j], gsmall['fox_k_norm'][j] = dwq.reshape(-1), dwk.reshape(-1)
            dcum = dcq + jnp.pad(jnp.transpose(dck.reshape(H, T)), ((0, 0), (0, LANES - H)))
            dls = _cum_rows(f"d_fox_cum_{i}", dcum, jnp.zeros((1, LANES), F32), True, False)
            dflog, dfb = _rows(f"d_fox_gate_{i}", lambda d, f, b: ((lambda r: (r, _colsum(r)))(d * _sigmoid(-(f + b)))),
                               [(dls, 'r', LANES, 0), (sv_['flog'], 'r', LANES, 0), (sv_['fb'], 'b', LANES, 0)],
                               [('r', LANES, BF16), ('a', LANES, F32)], 256)
            gsmall['fox_f_bias'][j] = dfb[0, :H]
            dproj = jnp.concatenate([dq, dk, dv, dz], axis=1)
            tkT = _t(K_STEP, T)
            dw_qkvz = _mm(f"d_fox_in_proj_{i}", sv_['xn'], dproj, M=D, N=4 * E, K=T, tm=_t(512, D), tn=_t(1024, 4 * E), tk=tkT, ta=True,
                          a_spec=_bs((tkT, _t(512, D)), lambda g, m, n, k: (k, m)),
                          b_spec=_bs((tkT, _t(1024, 4 * E)), lambda g, m, n, k: (k, n)),
                          outs=[((D, 4 * E), BF16, _tile(_t(512, D), _t(1024, 4 * E)))])[0]
            dw_f = _mm(f"d_fox_gate_proj_{i}", sv_['xn'], dflog, M=D, N=LANES, K=T, tm=_t(512, D), tn=LANES, tk=tkT, ta=True,
                       a_spec=_bs((tkT, _t(512, D)), lambda g, m, n, k: (k, m)),
                       b_spec=_bs((tkT, LANES), lambda g, m, n, k: (k, n)),
                       outs=[((D, LANES), BF16, _tile(_t(512, D), LANES))])[0]
            dw_fox = jnp.concatenate([dw_qkvz, dw_f[:, :H]], axis=1)
            sw = dw_fox.shape[1] // N_CHIPS
            layer_parts.append(('fox_in_proj', j, jnp.transpose(dw_fox.reshape(2, D // 2, N_CHIPS, sw), (0, 2, 1, 3))))
            w_qkvz, w_f = sv_['w_qkvz'], sv_['w_f']
            dxn_f = _mm(f"d_fox_xn_gate_{i}", dflog, w_f, M=T, N=D, K=LANES, tm=_t(512, T), tn=_t(1024, D), tk=LANES, tb=True,
                        a_spec=_bs((_t(512, T), LANES), lambda g, m, n, k: (m, k)),
                        b_spec=_bs((_t(1024, D), LANES), lambda g, m, n, k: (n, k)),
                        outs=[((T, D), F32, _tile(_t(512, T), _t(1024, D)))])[0]
            tm, tn, tk = _t(512, T), _t(1024, D), _t(1024, 4 * E)
            dxn = _mm(f"d_fox_xn_{i}", dproj, w_qkvz, M=T, N=D, K=4 * E, tm=tm, tn=tn, tk=tk, tb=True,
                      a_spec=_bs((tm, tk), lambda g, m, n, k: (m, k)), b_spec=_bs((tn, tk), lambda g, m, n, k: (n, k)),
                      extras=[(dxn_f, _tile(tm, tn))], epi=lambda acc, e: (acc + e,),
                      outs=[((T, D), F32, _tile(tm, tn))])[0]
        else:
            proj, mixed, scale = sv_['proj'], sv_['mixed'], sv_['scale']
            nm = T // _t(512, T)

            def pool_epi(da, mx, sc, z):
                dy = da * _silu(z)
                return (da * (mx * sc)) * _dsilu(z), dy * sc, _colsum(dy * mx)

            dz, dmix, dsc = _mm_rowsharded_t(
                f"d_pool_act_{i}", dh16, w_out, epi=pool_epi, deps=after_start,
                extras=lambda tm, tn: [(mixed, _tile(tm, tn)), (scale, _rowvec(tn)), (proj, _tile(tm, tn, E // tn))],
                outs_fn=lambda tm, tn: [((T, E), BF16, _tile(tm, tn)), ((T, E), BF16, _tile(tm, tn)),
                                        ((nm, 1, E), F32, _bs((None, 1, tn), lambda g, m, n, k: (m, 0, n)))])
            gsmall['pool_scale'][j] = jnp.sum(dsc, axis=(0, 1))
            w_pg = sv_['w_pg']
            tkw = w_pg.shape[2]
            tk = _t(K_STEP, T)
            layer_parts.append(('pool_w_group', j, _mm(
                f"d_pool_w_group_{i}", sv_['pm'], dmix, M=PD, N=PD, K=T, tm=tkw, tn=PD, tk=tk, groups=PG, ta=True,
                a_spec=_bs((tk, tkw), lambda g, m, n, k: (k, g * (PD // tkw) + m)),
                b_spec=_bs((tk, PD), lambda g, m, n, k: (k, g)),
                outs=[((2, N_CHIPS, PG // 2, tkw, PD), BF16, _bs((None, None, None, tkw, PD), lambda g, m, n, k: (g // (PG // 2), m, g % (PG // 2), 0, 0)))])[0]))
            tm, tk2 = _t(512, T), _t(512, PD)
            dpm = _mm(f"d_pool_mix_{i}", dmix, w_pg, M=T, N=PD, K=PD, tm=tm, tn=tkw, tk=tk2, groups=PG, tb=True,
                      a_spec=_bs((tm, tk2), lambda g, m, n, k: (m, g * (PD // tk2) + k)),
                      b_spec=_bs((None, None, tkw, tk2), lambda g, m, n, k: (n, g, 0, k)),
                      outs=[((T, E), F32, _bs((tm, tkw), lambda g, m, n, k: (m, g * (PD // tkw) + n)))])[0]
            du = _pool_bwd(f"d_pool_win_{i}", dpm, E)
            dproj = jnp.concatenate([du, dz], axis=1)
            layer_parts.append(('pool_in_proj', j, _mm_dw_cols(f"d_pool_in_proj_{i}", sv_['xn'], dproj)))
            dxn = _mm_colsharded_t(f"d_pool_xn_{i}", dproj, W[('pool_in_proj', j)])
        dh, dh16, dnw = _norm_bwd(f"d_norm_{i}", dxn, sv_['h'], nw, dh)
        gsmall['norm_w'][i] = dnw.reshape(D)
        if pending is not None:
            _reduce_end(pending[0], pending[1], dh16, pending[2], rs_bufs, rs_shapes)
        if i > 0:
            pending, token = reduce_layer(f"l{i}", layer_parts)
    grad_x = dh.reshape(x.shape)

    small_flat = jnp.concatenate([jnp.stack(gsmall[n]).reshape(-1) for n in SMALL])
    n_small = small_flat.shape[0]
    unit = 2 * N_CHIPS * 16 * LANES
    n_pad = -(-n_small // unit) * unit
    R = n_pad // (2 * N_CHIPS * LANES)
    small_part = jnp.pad(small_flat, (0, n_pad - n_small)).astype(BF16).reshape(2, N_CHIPS, R, LANES)
    pending, token = reduce_layer("l0", layer_parts + [('small', 'chip', small_part)])
    _reduce_end(early[0], early[1], token, early[2], rs_bufs, rs_shapes)
    nb = len(BIG)
    done_items = [d for d in rs_dests_all if d not in pending[2]]
    rs_bufs[:nb] = _pair_share("rs_pair_share_a", rs_bufs[:nb], done_items, deps=[token])
    late = [o for o, _ in pending[2]]
    delta, new_m, new_v = {}, {}, {}
    grads = {}
    last = token
    for o, n in enumerate(BIG):
        if o not in late:
            grads[n] = rs_bufs[o].reshape(w[n].shape)
            last = _adamw_big(n, w, grads, mom_m, mom_v, delta, new_m, new_v)
    _reduce_end(pending[0], pending[1], last, pending[2], rs_bufs, rs_shapes)
    shared = _pair_share("rs_pair_share_b", [rs_bufs[o] for o in late], [(k, l) for k, (_, l) in enumerate(pending[2])])
    for k, o in enumerate(late):
        rs_bufs[o] = shared[k]
        if o < nb:
            grads[BIG[o]] = shared[k].reshape(w[BIG[o]].shape)
            _adamw_big(BIG[o], w, grads, mom_m, mom_v, delta, new_m, new_v)
    small_all = _chip_allgather("gather_small_grads", [rs_bufs[nb]])[0]
    small_all = jnp.transpose(small_all, (1, 0, 2, 3)).reshape(-1)[:n_small]
    off = 0
    p = 2 * lax.axis_index("x") + lax.axis_index("y")
    for n in SMALL:
        full_shape = (w[n].shape[0], E) if n in SMALL_SHARDED else w[n].shape
        size = math.prod(full_shape)
        gfull = small_all[off:off + size].reshape(full_shape)
        off += size
        if n in SMALL_SHARDED:
            gfull = lax.dynamic_slice_in_dim(gfull, p * (E // N_CHIPS), E // N_CHIPS, axis=1)
        grads[n] = gfull

    for n in SMALL:
        shape = w[n].shape
        if n in GROUP_AXIS_1:
            perm = (0,) + tuple(range(2, len(shape))) + (1,)
            inv = (0, len(shape) - 1) + tuple(range(1, len(shape) - 1))
            view = lambda a: jnp.transpose(a, perm).reshape(-1, shape[1])
            back = lambda a: jnp.transpose(a.reshape(tuple(shape[k] for k in perm)), inv)
        else:
            view = lambda a: a.reshape(-1, shape[-1])
            back = lambda a: a.reshape(shape)
        d_, m_, v_ = _adamw(f"adamw_{n}", view(w[n]), view(grads[n]), view(mom_m[n]), view(mom_v[n]))
        delta[n], new_m[n], new_v[n] = back(d_), back(m_), back(v_)
    return (loss, grad_x, *[grads[n] for n in ORDER], *[delta[n] for n in ORDER], *[new_m[n] for n in ORDER], *[new_v[n] for n in ORDER])
```

```python
import functools
import math

import jax
import jax.numpy as jnp
from jax import lax
from jax.experimental import pallas as pl
from jax.experimental.pallas import tpu as pltpu

F32 = jnp.float32
BF16 = jnp.bfloat16
MESH = pl.DeviceIdType.MESH

N_CHIPS = 4
VMEM_LIMIT = 56 * 1024 * 1024
LANES = 128
SUB = 8

EPS = 1e-6
S5_GROUP = 16
S5_STATE = 64
GROUPS_PER_CHUNK = 16
FOX_HEAD_DIM = 128
ATTN_SUB = 256
ATTN_HEADS = 2
POOL_WINDOWS = (2, 4, 8, 16)
POOL_HALO = 16
ADAM_LR, ADAM_B1, ADAM_B2, ADAM_EPS, ADAM_WD, ADAM_STEP = 0.001, 0.9, 0.999, 1e-08, 0.01, 10
NEG = -1e30
K_STEP = 2048


ANY = pl.BlockSpec(memory_space=pl.ANY)


def _t(pref, dim):
    if dim <= pref:
        return dim
    t = pref - pref % 16
    while t > 16 and dim % t:
        t -= 16
    assert dim % t == 0, (pref, dim)
    return t


def _params(sem):
    return pltpu.CompilerParams(dimension_semantics=sem, vmem_limit_bytes=VMEM_LIMIT)


def _sigmoid(x):
    return 1.0 / (1.0 + jnp.exp(-x))


def _silu(z):
    return z * _sigmoid(z)


def _dsilu(z):
    s = _sigmoid(z)
    return s * (1.0 + z * (1.0 - s))


_GELU_C = math.sqrt(2.0 / math.pi)


def _gelu(x):
    return 0.5 * x * (1.0 + jnp.tanh(_GELU_C * (x + 0.044715 * (x * x * x))))


def _dgelu(x):
    t = jnp.tanh(_GELU_C * (x + 0.044715 * (x * x * x)))
    return 0.5 * (1.0 + t) + 0.5 * x * (1.0 - t * t) * (_GELU_C * (1.0 + 3.0 * 0.044715 * x * x))


def _log_sigmoid(x):
    return jnp.minimum(x, 0.0) - jnp.log(1.0 + jnp.exp(-jnp.abs(x)))


def _rms(x):
    return lax.rsqrt(jnp.mean(x * x, axis=-1, keepdims=True) + EPS)


def _rms_bwd(x, w, dy):
    r = _rms(x)
    xhat = x * r
    dxh = dy * w
    dx = r * (dxh - xhat * jnp.mean(dxh * xhat, axis=-1, keepdims=True))
    return dx, dy * xhat


def _rows(name, fn, ins, outs, tr, pre=None, into=None, deps=()):
    rows = None
    for arr, kind, cols, cb in ins:
        if kind == 'r':
            rows = arr.shape[0]
        elif kind == 's' and rows is None:
            rows = arr.shape[1]
    tr = _t(tr, rows)
    n_in = len(ins)
    has_acc = any(o[0] == 'a' for o in outs)

    def spec(kind, cols, cb):
        if kind == 'r':
            return pl.BlockSpec((tr, cols), lambda r, *p: (r, cb))
        if kind == 'b':
            return pl.BlockSpec((1, cols), lambda r, *p: (0, cb))
        return pl.BlockSpec((None, tr, cols), lambda r, p: (p[cb], r, 0))

    in_specs = [spec(kind, cols, cb) for _, kind, cols, cb in ins]
    out_specs, out_shape = [], []
    for o in outs:
        if o[0] == 'r':
            out_specs.append(pl.BlockSpec((tr, o[1]), lambda r, *p: (r, 0)))
            out_shape.append(jax.ShapeDtypeStruct((rows, o[1]), o[2]))
        elif o[0] == 'a':
            out_specs.append(pl.BlockSpec((1, o[1]), lambda r, *p: (0, 0)))
            out_shape.append(jax.ShapeDtypeStruct((1, o[1]), o[2]))
        else:
            blk = tuple(tr if d == 'tr' else d for d in o[3])
            out_specs.append(pl.BlockSpec(blk, o[4]))
            out_shape.append(jax.ShapeDtypeStruct(o[1], o[2]))
    n_pre = 0 if pre is None else 1
    args = [a[0] for a in ins]
    aliases = {}
    if into is not None:
        in_specs.append(ANY)
        args.append(into)
        aliases = {n_pre + n_in: 0}
    in_specs += [ANY] * len(deps)
    args += list(deps)
    n_all = len(args)

    def body(*refs):
        refs = refs[n_pre:]
        res = fn(*[r[...] for r in refs[:n_in]])
        for spec_o, o, v in zip(outs, refs[n_all:], res):
            if spec_o[0] == 'a':
                @pl.when(pl.program_id(0) == 0)
                def _():
                    o[...] = jnp.zeros_like(o)
                o[...] += v.astype(o.dtype)
            else:
                o[...] = v.astype(o.dtype)

    grid_spec = pltpu.PrefetchScalarGridSpec(num_scalar_prefetch=n_pre, grid=(rows // tr,), in_specs=in_specs, out_specs=out_specs)
    if pre is not None:
        args = [pre] + args
    return pl.pallas_call(body, name=name, grid_spec=grid_spec, out_shape=out_shape, input_output_aliases=aliases,
                          compiler_params=_params(("arbitrary" if has_acc else "parallel",)))(*args)


def _colsum(v):
    return jnp.sum(v, axis=0, keepdims=True)


def _mm(name, a, b, *, M, N, K, tm, tn, tk, a_spec, b_spec, outs, epi=None, extras=(), groups=1, ta=False, tb=False, deps=()):
    nk = K // tk
    assert M % tm == 0 and N % tn == 0 and K % tk == 0, (name, M, N, K, tm, tn, tk)
    dims = (((0 if ta else 1,), (1 if tb else 0,)), ((), ()))
    n_ex = len(extras)

    def body(*refs):
        a_ref, b_ref = refs[0], refs[1]
        ex = refs[2:2 + n_ex]
        out_refs = refs[2 + n_ex + len(deps):2 + n_ex + len(deps) + len(outs)]

        def finish(r):
            res = (r,) if epi is None else epi(r, *[e[...] for e in ex])
            for o, v in zip(out_refs, res):
                o[...] = v.astype(o.dtype)

        part = lax.dot_general(a_ref[...].astype(BF16), b_ref[...].astype(BF16), dims, preferred_element_type=F32)
        if nk == 1:
            finish(part)
            return
        acc = refs[-1]
        k = pl.program_id(3)

        @pl.when(k == 0)
        def _():
            acc[...] = part

        @pl.when(k > 0)
        def _():
            acc[...] += part

        @pl.when(k == nk - 1)
        def _():
            finish(acc[...])

    return pl.pallas_call(
        body, name=name, grid=(groups, M // tm, N // tn, nk),
        in_specs=[a_spec, b_spec] + [s for _, s in extras] + [ANY] * len(deps),
        out_specs=[s for _, _, s in outs],
        out_shape=[jax.ShapeDtypeStruct(sh, dt) for sh, dt, _ in outs],
        scratch_shapes=[] if nk == 1 else [pltpu.VMEM((tm, tn), F32)],
        compiler_params=_params(("parallel", "parallel", "parallel", "arbitrary")),
    )(a, b, *[e for e, _ in extras], *deps)


def _bs(shape, f):
    return pl.BlockSpec(shape, f)


def _tile(tm, tn, coff=0):
    return _bs((tm, tn), lambda g, m, n, k: (m, n + coff))


def _rowvec(tn, coff=0):
    return _bs((1, tn), lambda g, m, n, k: (0, n + coff))


def _mm_proj(name, xn, w, *, epi=None, extras=(), out_dtype=F32):
    T, D = xn.shape
    sw = w.shape[2]
    N = N_CHIPS * sw
    tm, tn, tk = _t(512, T), _t(1024, sw), _t(K_STEP, D)
    nb = sw // tn
    return _mm(name, xn, w, M=T, N=N, K=D, tm=tm, tn=tn, tk=tk,
               a_spec=_bs((tm, tk), lambda g, m, n, k: (m, k)),
               b_spec=_bs((None, tk, tn), lambda g, m, n, k: (n // nb, k, n % nb)),
               outs=[((T, N), out_dtype, _tile(tm, tn))], epi=epi, extras=extras)[0]


def _mm_plain(name, a, b, *, out_dtype=F32, epi=None, extras=(), outs=None, tn_pref=1024):
    M, K = a.shape
    N = b.shape[1]
    tm, tn, tk = _t(512, M), _t(tn_pref, N), _t(K_STEP, K)
    if outs is None:
        outs = [((M, N), out_dtype, _tile(tm, tn))]
    return _mm(name, a, b, M=M, N=N, K=K, tm=tm, tn=tn, tk=tk,
               a_spec=_bs((tm, tk), lambda g, m, n, k: (m, k)),
               b_spec=_bs((tk, tn), lambda g, m, n, k: (k, n)),
               outs=outs, epi=epi, extras=extras)


def _mm_rowsharded(name, a, w, *, epi, extras, outs_fn, deps=()):
    T, E = a.shape
    tk = w.shape[1]
    N = w.shape[2]
    tm, tn = _t(512, T), _t(1024, N)
    return _mm(name, a, w, M=T, N=N, K=E, tm=tm, tn=tn, tk=tk, deps=deps,
               a_spec=_bs((tm, tk), lambda g, m, n, k: (m, k)),
               b_spec=_bs((None, tk, tn), lambda g, m, n, k: (k, 0, n)),
               outs=outs_fn(tm, tn), epi=epi, extras=extras(tm, tn))


def _mm_rowsharded_t(name, d, w, *, epi, extras, outs_fn, deps=()):
    T, N = d.shape
    tn = w.shape[1]
    E = N_CHIPS * tn
    tm, tk = _t(512, T), _t(K_STEP, N)
    return _mm(name, d, w, M=T, N=E, K=N, tm=tm, tn=tn, tk=tk, tb=True, deps=deps,
               a_spec=_bs((tm, tk), lambda g, m, n, k: (m, k)),
               b_spec=_bs((None, tn, tk), lambda g, m, n, k: (n, 0, k)),
               outs=outs_fn(tm, tn), epi=epi, extras=extras(tm, tn))


def _mm_colsharded_t(name, d, w):
    T, N = d.shape
    D, sw = w.shape[1], w.shape[2]
    tm, tn, tk = _t(512, T), _t(1024, D), _t(1024, sw)
    kb = sw // tk
    return _mm(name, d, w, M=T, N=D, K=N, tm=tm, tn=tn, tk=tk, tb=True,
               a_spec=_bs((tm, tk), lambda g, m, n, k: (m, k)),
               b_spec=_bs((None, tn, tk), lambda g, m, n, k: (k // kb, n, k % kb)),
               outs=[((T, D), F32, _tile(tm, tn))])[0]


def _mm_dw_rows(name, a, d, deps=()):
    T, E = a.shape
    N = d.shape[1]
    tm, tn, tk = E // (2 * N_CHIPS), _t(2048, N), _t(K_STEP, T)
    return _mm(name, a, d, M=E, N=N, K=T, tm=tm, tn=tn, tk=tk, ta=True, deps=deps,
               a_spec=_bs((tk, tm), lambda g, m, n, k: (k, m)),
               b_spec=_bs((tk, tn), lambda g, m, n, k: (k, n)),
               outs=[((2, N_CHIPS, tm, N), BF16, _bs((None, None, tm, tn), lambda g, m, n, k: (m % 2, m // 2, 0, n)))])[0]


def _mm_dw_cols(name, xn, d):
    T, D = xn.shape
    N = d.shape[1]
    sw = N // N_CHIPS
    tm, tn, tk = _t(512, D // 2), _t(1024, sw), _t(K_STEP, T)
    mh, nb = (D // 2) // tm, sw // tn
    return _mm(name, xn, d, M=D, N=N, K=T, tm=tm, tn=tn, tk=tk, ta=True,
               a_spec=_bs((tk, tm), lambda g, m, n, k: (k, m)),
               b_spec=_bs((tk, tn), lambda g, m, n, k: (k, n)),
               outs=[((2, N_CHIPS, D // 2, sw), BF16,
                      _bs((None, None, tm, tn), lambda g, m, n, k: (m // mh, n // nb, m % mh, n % nb)))])[0]


def _norm_fwd(name, h, w, deps=()):
    D = h.shape[1]
    return _rows(name, lambda x, g: ((x * _rms(x)) * g,), [(h, 'r', D, 0), (w, 'b', D, 0)], [('r', D, BF16)], 256, deps=deps)[0]


def _norm_bwd(name, dxn, h, w, dh):
    D = h.shape[1]

    def fn(dy, x, g, up):
        dx, dwt = _rms_bwd(x, g, dy)
        r = up + dx
        return r, r, _colsum(dwt)

    return _rows(name, fn, [(dxn, 'r', D, 0), (h, 'r', D, 0), (w, 'b', D, 0), (dh, 'r', D, 0)],
                 [('r', D, F32), ('r', D, BF16), ('a', D, F32)], 256)


def _loss(h, target):
    D = h.shape[1]

    def fn(y, t):
        e = y - t
        d = e * (1.0 / D)
        return d, d, _colsum(e * e) * (0.5 / D)

    return _rows("loss", fn, [(h, 'r', D, 0), (target, 'r', D, 0)], [('r', D, F32), ('r', D, BF16), ('a', D, F32)], 256)


def _adamw(name, w, g, m, v):
    cols = w.shape[1]

    def fn(w, g, m, v):
        m = ADAM_B1 * m + (1.0 - ADAM_B1) * g
        v = ADAM_B2 * v + (1.0 - ADAM_B2) * (g * g)
        m_hat = m / (1.0 - ADAM_B1 ** ADAM_STEP)
        v_hat = v / (1.0 - ADAM_B2 ** ADAM_STEP)
        delta = -ADAM_LR * (m_hat / (jnp.sqrt(v_hat) + ADAM_EPS) + ADAM_WD * w)
        return delta, m, v

    rows = w.shape[0]
    if rows % SUB == 0 or rows <= 256:
        return _rows(name, fn, [(x, 'r', cols, 0) for x in (w, g, m, v)], [('r', cols, F32)] * 3, 256)
    tc = _t(256, cols)
    assert tc % LANES == 0, (rows, cols)

    def body(w_ref, g_ref, m_ref, v_ref, d_out, m_out, v_out):
        for o, r in zip((d_out, m_out, v_out), fn(w_ref[...], g_ref[...], m_ref[...], v_ref[...])):
            o[...] = r

    blk = pl.BlockSpec((rows, tc), lambda j: (0, j))
    return pl.pallas_call(body, name=name, grid=(cols // tc,), in_specs=[blk] * 4, out_specs=[blk] * 3,
                          out_shape=[jax.ShapeDtypeStruct((rows, cols), F32)] * 3, compiler_params=_params(("parallel",)))(w, g, m, v)


def _s5_disc(a_re, a_im, log_dt):
    dt = jnp.exp(log_dt)
    mag = jnp.exp(a_re * dt)
    abar_r = mag * jnp.cos(a_im * dt)
    abar_i = mag * jnp.sin(a_im * dt)
    den = a_re * a_re + a_im * a_im
    xr = abar_r - 1.0
    fr = (xr * a_re + abar_i * a_im) / den
    fi = (abar_i * a_re - xr * a_im) / den
    return abar_r, abar_i, fr, fi


def _s5_disc_fwd(name, a_re, a_im, log_dt):
    G, P = a_re.shape

    def body(ar, ai, ld, o0, o1, o2, o3):
        for o, v in zip((o0, o1, o2, o3), _s5_disc(ar[...], ai[...], ld[...])):
            o[...] = v

    return pl.pallas_call(body, name=name, out_shape=[jax.ShapeDtypeStruct((G, P), F32)] * 4)(a_re, a_im, log_dt)


def _s5_disc_bwd(name, a_re, a_im, log_dt, cts):
    G, P = a_re.shape

    def body(ar, ai, ld, c0, c1, c2, c3, d0, d1, d2):
        _, vjp = jax.vjp(_s5_disc, ar[...], ai[...], ld[...])
        g0, g1, g2 = vjp((c0[...], c1[...], c2[...], c3[...]))
        d0[...] = g0
        d1[...] = g1
        d2[...] = g2

    return pl.pallas_call(body, name=name, out_shape=[jax.ShapeDtypeStruct((G, P), F32)] * 2 + [jax.ShapeDtypeStruct((G, 1), F32)])(
        a_re, a_im, log_dt, *cts)


def _s5_bbar(name, fr, fi, br, bi):
    return _rows(name, lambda fr, fi, br, bi: (fr * br - fi * bi, fr * bi + fi * br),
                 [(fr, 'r', 1, 0), (fi, 'r', 1, 0), (br, 'r', S5_GROUP, 0), (bi, 'r', S5_GROUP, 0)],
                 [('r', S5_GROUP, F32)] * 2, 2048)


def _s5_bbar_bwd(name, fr, fi, br, bi, dr, di):
    def fn(fr, fi, br, bi, dr, di):
        return (fr * dr + fi * di, fr * di - fi * dr,
                jnp.sum(br * dr + bi * di, axis=1, keepdims=True), jnp.sum(br * di - bi * dr, axis=1, keepdims=True))

    return _rows(name, fn, [(fr, 'r', 1, 0), (fi, 'r', 1, 0)] + [(x, 'r', S5_GROUP, 0) for x in (br, bi, dr, di)],
                 [('r', S5_GROUP, F32)] * 2 + [('r', 1, F32)] * 2, 2048)


def _scan_mults(m_ref, ar, ai, reverse):
    L = ar.shape[1]
    row = lax.broadcasted_iota(jnp.int32, (SUB, L), 0)
    if reverse:
        row = (SUB - 1) - row
    ar = jnp.broadcast_to(ar, (SUB, L))
    ai = jnp.broadcast_to(ai, (SUB, L))
    a2r, a2i = ar * ar - ai * ai, 2.0 * ar * ai
    a4r, a4i = a2r * a2r - a2i * a2i, 2.0 * a2r * a2i
    zero = jnp.zeros((SUB, L), F32)
    for s, (pr, pi, d) in enumerate(((ar, ai, 1), (a2r, a2i, 2), (a4r, a4i, 4))):
        m_ref[2 * s] = jnp.where(row >= d, pr, zero)
        m_ref[2 * s + 1] = jnp.where(row >= d, pi, zero)
    pr, pi = ar, ai
    for bit, (qr, qi) in ((1, (ar, ai)), (2, (a2r, a2i)), (4, (a4r, a4i))):
        on = (row & bit) != 0
        nr, ni = pr * qr - pi * qi, pr * qi + pi * qr
        pr, pi = jnp.where(on, nr, pr), jnp.where(on, ni, pi)
    m_ref[6] = pr
    m_ref[7] = pi


def _scan8(xr, xi, m_ref, cr, ci, reverse):
    for s, d in enumerate((1, 2, 4)):
        sh = (SUB - d) if reverse else d
        sr, si = pltpu.roll(xr, sh, 0), pltpu.roll(xi, sh, 0)
        mr, mi = m_ref[2 * s], m_ref[2 * s + 1]
        xr, xi = xr + mr * sr - mi * si, xi + mr * si + mi * sr
    pr, pi = m_ref[6], m_ref[7]
    return xr + pr * cr - pi * ci, xi + pr * ci + pi * cr


def _blockdiag_fill(bd_ref, c_ref, C, L):
    P = S5_STATE
    bd_ref[...] = jnp.zeros_like(bd_ref)
    for g in range(L // P):
        for half in (0, L):
            bd_ref[g * C:(g + 1) * C, half + g * P:half + (g + 1) * P] = c_ref[:, half + g * P:half + (g + 1) * P]


def _blockdiag_take(out_ref, dense_ref, C, L):
    P = S5_STATE
    for g in range(L // P):
        for half in (0, L):
            out_ref[:, half + g * P:half + (g + 1) * P] = dense_ref[g * C:(g + 1) * C, half + g * P:half + (g + 1) * P]


def _s5_fwd(name, proj, bbd, cbd, abar_r, abar_i, dskip, E):
    T = proj.shape[0]
    NC, C, L2 = bbd.shape
    L = L2 // 2
    CH = GROUPS_PER_CHUNK * C
    tT = _t(256, T)
    nt = (((1,), (1,)), ((), ()))

    def body(u_ref, bc_ref, cc_ref, ar_ref, ai_ref, d_ref, y_ref, g_ref, h_ref, bu, carry, mult, b_bd, c_bd):
        tb = pl.program_id(1)

        @pl.when(tb == 0)
        def _():
            carry[...] = jnp.zeros_like(carry)
            _blockdiag_fill(b_bd, bc_ref, C, L)
            _blockdiag_fill(c_bd, cc_ref, C, L)

        u = u_ref[...]
        bu[...] = jnp.dot(u.astype(BF16), b_bd[...], preferred_element_type=F32)
        _scan_mults(mult, ar_ref[...], ai_ref[...], False)

        def step(jb, c):
            cr, ci = c
            r0 = pl.multiple_of(jb * SUB, SUB)
            hr, hi = _scan8(bu[pl.ds(r0, SUB), 0:L], bu[pl.ds(r0, SUB), L:L2], mult, cr, ci, False)
            h_ref[pl.ds(r0, SUB), 0:L] = hr
            h_ref[pl.ds(r0, SUB), L:L2] = hi
            return (jnp.broadcast_to(hr[SUB - 1:SUB, :], (SUB, L)), jnp.broadcast_to(hi[SUB - 1:SUB, :], (SUB, L)))

        cr, ci = lax.fori_loop(0, tT // SUB, step, (carry[:, 0:L], carry[:, L:L2]))
        carry[:, 0:L] = cr
        carry[:, L:L2] = ci
        y1 = lax.dot_general(h_ref[...].astype(BF16), c_bd[...], nt, preferred_element_type=F32) + d_ref[...] * u
        y_ref[...] = y1
        g_ref[...] = _gelu(y1).astype(BF16)

    return pl.pallas_call(
        body, name=name, grid=(NC, T // tT),
        in_specs=[_bs((tT, CH), lambda c, t: (t, c)), _bs((None, C, L2), lambda c, t: (c, 0, 0)),
                  _bs((None, C, L2), lambda c, t: (c, 0, 0)), _bs((None, 1, L), lambda c, t: (c, 0, 0)),
                  _bs((None, 1, L), lambda c, t: (c, 0, 0)), _bs((1, CH), lambda c, t: (0, c))],
        out_specs=[_bs((tT, CH), lambda c, t: (t, c)), _bs((tT, CH), lambda c, t: (t, c)),
                   _bs((None, tT, L2), lambda c, t: (c, t, 0))],
        out_shape=[jax.ShapeDtypeStruct((T, E), F32), jax.ShapeDtypeStruct((T, E), BF16),
                   jax.ShapeDtypeStruct((NC, T, L2), F32)],
        scratch_shapes=[pltpu.VMEM((tT, L2), F32), pltpu.VMEM((SUB, L2), F32), pltpu.VMEM((8, SUB, L), F32),
                        pltpu.VMEM((CH, L2), BF16), pltpu.VMEM((CH, L2), BF16)],
        compiler_params=_params(("parallel", "arbitrary")),
    )(proj, bbd, cbd, abar_r, abar_i, dskip)


def _s5_bwd(name, dy1, proj, hs, bbd, cbd, abar_r, abar_i, dskip, E):
    T = proj.shape[0]
    NC, C, L2 = bbd.shape
    L = L2 // 2
    CH = GROUPS_PER_CHUNK * C
    tT = _t(256, T)
    nT = T // tT
    tn = (((0,), (0,)), ((), ()))
    nt = (((1,), (1,)), ((), ()))

    def body(dy_ref, u_ref, h_ref, bc_ref, cc_ref, ar_ref, ai_ref, d_ref, du_ref, db_ref, dc_ref, da_ref, dd_ref,
             gb, carry, mult, b_bd, c_bd, db_acc, dc_acc):
        tb = pl.program_id(1)

        @pl.when(tb == 0)
        def _():
            carry[...] = jnp.zeros_like(carry)
            db_acc[...] = jnp.zeros_like(db_acc)
            dc_acc[...] = jnp.zeros_like(dc_acc)
            da_ref[...] = jnp.zeros_like(da_ref)
            dd_ref[...] = jnp.zeros_like(dd_ref)
            _blockdiag_fill(b_bd, bc_ref, C, L)
            _blockdiag_fill(c_bd, cc_ref, C, L)

        dy = dy_ref[...]
        u = u_ref[...]
        dy16 = dy.astype(BF16)
        dc_acc[...] += lax.dot_general(dy16, h_ref[...].astype(BF16), tn, preferred_element_type=F32)
        gb[...] = jnp.dot(dy16, c_bd[...], preferred_element_type=F32)
        _scan_mults(mult, ar_ref[...], -ai_ref[...], True)
        row = lax.broadcasted_iota(jnp.int32, (SUB, L), 0)
        nblk = tT // SUB

        def step(jj, c):
            cr, ci, sr, si = c
            r0 = pl.multiple_of((nblk - 1 - jj) * SUB, SUB)
            gr, gi = _scan8(gb[pl.ds(r0, SUB), 0:L], gb[pl.ds(r0, SUB), L:L2], mult, cr, ci, True)
            gb[pl.ds(r0, SUB), 0:L] = gr
            gb[pl.ds(r0, SUB), L:L2] = gi
            nr = jnp.where(row == SUB - 1, cr, pltpu.roll(gr, SUB - 1, 0))
            ni = jnp.where(row == SUB - 1, ci, pltpu.roll(gi, SUB - 1, 0))
            hr, hi = h_ref[pl.ds(r0, SUB), 0:L], h_ref[pl.ds(r0, SUB), L:L2]
            sr = sr + nr * hr + ni * hi
            si = si + ni * hr - nr * hi
            return (jnp.broadcast_to(gr[0:1, :], (SUB, L)), jnp.broadcast_to(gi[0:1, :], (SUB, L)), sr, si)

        z = jnp.zeros((SUB, L), F32)
        cr, ci, sr, si = lax.fori_loop(0, nblk, step, (carry[:, 0:L], carry[:, L:L2], z, z))
        carry[:, 0:L] = cr
        carry[:, L:L2] = ci
        da_ref[:, 0:L] += sr
        da_ref[:, L:L2] += si
        g16 = gb[...].astype(BF16)
        du = lax.dot_general(g16, b_bd[...], nt, preferred_element_type=F32) + d_ref[...] * dy
        du_ref[...] = du.astype(BF16)
        db_acc[...] += lax.dot_general(u.astype(BF16), g16, tn, preferred_element_type=F32)
        dd_ref[...] += _colsum(dy * u)

        @pl.when(tb == nT - 1)
        def _():
            _blockdiag_take(db_ref, db_acc, C, L)
            _blockdiag_take(dc_ref, dc_acc, C, L)

    rev = lambda c, t: (nT - 1 - t, c)
    return pl.pallas_call(
        body, name=name, grid=(NC, nT),
        in_specs=[_bs((tT, CH), rev), _bs((tT, CH), rev), _bs((None, tT, L2), lambda c, t: (c, nT - 1 - t, 0)),
                  _bs((None, C, L2), lambda c, t: (c, 0, 0)), _bs((None, C, L2), lambda c, t: (c, 0, 0)),
                  _bs((None, 1, L), lambda c, t: (c, 0, 0)), _bs((None, 1, L), lambda c, t: (c, 0, 0)),
                  _bs((1, CH), lambda c, t: (0, c))],
        out_specs=[_bs((tT, CH), rev), _bs((None, C, L2), lambda c, t: (c, 0, 0)), _bs((None, C, L2), lambda c, t: (c, 0, 0)),
                   _bs((None, SUB, L2), lambda c, t: (c, 0, 0)), _bs((None, 1, CH), lambda c, t: (c, 0, 0))],
        out_shape=[jax.ShapeDtypeStruct((T, E), BF16), jax.ShapeDtypeStruct((NC, C, L2), F32),
                   jax.ShapeDtypeStruct((NC, C, L2), F32), jax.ShapeDtypeStruct((NC, SUB, L2), F32),
                   jax.ShapeDtypeStruct((NC, 1, CH), F32)],
        scratch_shapes=[pltpu.VMEM((tT, L2), F32), pltpu.VMEM((SUB, L2), F32), pltpu.VMEM((8, SUB, L), F32),
                        pltpu.VMEM((CH, L2), BF16), pltpu.VMEM((CH, L2), BF16), pltpu.VMEM((CH, L2), F32), pltpu.VMEM((CH, L2), F32)],
        compiler_params=_params(("parallel", "arbitrary")),
    )(dy1, proj, hs, bbd, cbd, abar_r, abar_i, dskip)


def _compact(v, NC):
    G, P, C = v.shape
    return jnp.transpose(v.reshape(NC, G // NC, P, C), (0, 3, 1, 2)).reshape(NC, C, (G // NC) * P)


def _uncompact(d, G):
    NC, C, L = d.shape
    gpc = G // NC
    return jnp.transpose(d.reshape(NC, C, gpc, L // gpc), (0, 2, 3, 1)).reshape(G, L // gpc, C)


def _cum_rows(name, x, bias, reverse, log_sig):
    T, L = x.shape

    def body(x_ref, b_ref, o_ref):
        row = lax.broadcasted_iota(jnp.int32, (SUB, L), 0)
        if reverse:
            row = (SUB - 1) - row
        nblk = T // SUB

        def step(jj, c):
            r0 = pl.multiple_of(((nblk - 1 - jj) if reverse else jj) * SUB, SUB)
            v = x_ref[pl.ds(r0, SUB), :] + b_ref[...]
            if log_sig:
                v = _log_sigmoid(v)
            for d in (1, 2, 4):
                v = v + jnp.where(row >= d, pltpu.roll(v, (SUB - d) if reverse else d, 0), 0.0)
            v = v + c
            o_ref[pl.ds(r0, SUB), :] = v
            e = 0 if reverse else SUB - 1
            return jnp.broadcast_to(v[e:e + 1, :], (SUB, L))

        lax.fori_loop(0, nblk, step, jnp.zeros((SUB, L), F32))

    return pl.pallas_call(body, name=name, out_shape=jax.ShapeDtypeStruct((T, L), F32),
                          compiler_params=pltpu.CompilerParams(vmem_limit_bytes=VMEM_LIMIT))(x, bias)


def _qk_norm(name, proj, wq, wk, H):
    T = proj.shape[0]
    Dh = FOX_HEAD_DIM
    tT = _t(512, T)

    def body(q_ref, k_ref, wq_ref, wk_ref, qn_ref, kn_ref):
        q, k = q_ref[...], k_ref[...]
        qn_ref[...] = ((q * _rms(q)) * wq_ref[...]).astype(BF16)
        kn_ref[...] = ((k * _rms(k)) * wk_ref[...]).astype(BF16)

    blk = lambda off: _bs((tT, Dh), lambda t, h: (t, h + off))
    return pl.pallas_call(
        body, name=name, grid=(T // tT, H),
        in_specs=[blk(0), blk(H), _bs((1, Dh), lambda t, h: (0, 0)), _bs((1, Dh), lambda t, h: (0, 0))],
        out_specs=[blk(0), blk(0)], out_shape=[jax.ShapeDtypeStruct((T, H * Dh), BF16)] * 2,
        compiler_params=_params(("parallel", "parallel")))(proj, proj, wq, wk)


def _qk_norm_bwd(name, proj, wq, wk, dqn, dkn, H):
    T = proj.shape[0]
    Dh = FOX_HEAD_DIM
    tT = _t(512, T)

    def body(q_ref, k_ref, wq_ref, wk_ref, dqn_ref, dkn_ref, dq_ref, dk_ref, dwq_ref, dwk_ref):
        @pl.when((pl.program_id(0) == 0) & (pl.program_id(1) == 0))
        def _():
            dwq_ref[...] = jnp.zeros_like(dwq_ref)
            dwk_ref[...] = jnp.zeros_like(dwk_ref)

        dq, tq = _rms_bwd(q_ref[...], wq_ref[...], dqn_ref[...])
        dk, tk = _rms_bwd(k_ref[...], wk_ref[...], dkn_ref[...])
        dq_ref[...] = dq.astype(BF16)
        dk_ref[...] = dk.astype(BF16)
        dwq_ref[...] += _colsum(tq)
        dwk_ref[...] += _colsum(tk)

    blk = lambda off: _bs((tT, Dh), lambda t, h: (t, h + off))
    one = _bs((1, Dh), lambda t, h: (0, 0))
    return pl.pallas_call(
        body, name=name, grid=(T // tT, H),
        in_specs=[blk(0), blk(H), one, one, blk(0), blk(0)],
        out_specs=[blk(0), blk(0), one, one],
        out_shape=[jax.ShapeDtypeStruct((T, H * Dh), BF16)] * 2 + [jax.ShapeDtypeStruct((1, Dh), F32)] * 2,
        compiler_params=_params(("arbitrary", "arbitrary")))(proj, proj, wq, wk, dqn, dkn)


def _attn_fwd(name, qn, kn, proj, cum_q, cum_k, H):
    T = qn.shape[0]
    Dh = FOX_HEAD_DIM
    tq = cum_k.shape[3]
    nq = T // tq
    scale = Dh ** -0.5
    nt = (((1,), (1,)), ((), ()))

    sq = _t(ATTN_SUB, tq)
    rep = tq // LANES
    HP = ATTN_HEADS
    assert H % HP == 0 and Dh == LANES

    def body(q_ref, k_ref, v_ref, cq_ref, ck_ref, o_ref, lse_ref, m_sc, l_sc, acc_sc):
        i = pl.program_id(1)
        m_sc[...] = jnp.full_like(m_sc, NEG)
        l_sc[...] = jnp.zeros_like(l_sc)
        acc_sc[...] = jnp.zeros_like(acc_sc)
        kloc = lax.broadcasted_iota(jnp.int32, (sq, tq), 1)
        qloc = lax.broadcasted_iota(jnp.int32, (sq, tq), 0)

        def chunk(kc, masked):
            ks = pl.multiple_of(kc * tq, tq)
            for hh in range(HP):
                lanes = slice(hh * Dh, (hh + 1) * Dh)
                k = k_ref[pl.ds(ks, tq), lanes]
                v16 = v_ref[pl.ds(ks, tq), lanes].astype(BF16)
                ck = ck_ref[hh, kc]
                for r in range(tq // sq):
                    rows = pl.ds(r * sq, sq)
                    s = lax.dot_general(q_ref[rows, lanes], k, nt, preferred_element_type=F32) * scale + (jnp.tile(cq_ref[hh, rows, :], (1, rep)) - ck)
                    if masked:
                        s = jnp.where(kloc <= qloc + r * sq, s, NEG)
                    m_old = m_sc[rows, lanes]
                    m_new = jnp.maximum(m_old, jnp.max(s, axis=1, keepdims=True))
                    alpha = jnp.exp(m_old - m_new)
                    p = jnp.exp(s - jnp.tile(m_new, (1, rep)))
                    l_sc[rows, lanes] = alpha * l_sc[rows, lanes] + jnp.sum(p, axis=1, keepdims=True)
                    acc_sc[rows, lanes] = alpha * acc_sc[rows, lanes] + jnp.dot(p.astype(BF16), v16, preferred_element_type=F32)
                    m_sc[rows, lanes] = m_new

        def below(kc, c):
            chunk(kc, False)
            return c

        lax.fori_loop(0, i, below, 0)
        chunk(i, True)
        o_ref[...] = acc_sc[...] / l_sc[...]
        for hh in range(HP):
            lanes = slice(hh * Dh, (hh + 1) * Dh)
            lse_ref[hh] = m_sc[:, lanes] + jnp.log(l_sc[:, lanes])

    W2 = HP * Dh
    return pl.pallas_call(
        body, name=name, grid=(H // HP, nq),
        in_specs=[_bs((tq, W2), lambda h, i: (i, h)), _bs((T, W2), lambda h, i: (0, h)), _bs((T, W2), lambda h, i: (0, 2 * (H // HP) + h)),
                  _bs((HP, tq, LANES), lambda h, i: (h, i, 0)), _bs((HP, nq, 1, tq), lambda h, i: (h, 0, 0, 0))],
        out_specs=[_bs((tq, W2), lambda h, i: (i, h)), _bs((HP, tq, LANES), lambda h, i: (h, i, 0))],
        out_shape=[jax.ShapeDtypeStruct((T, H * Dh), F32), jax.ShapeDtypeStruct((H, T, LANES), F32)],
        scratch_shapes=[pltpu.VMEM((tq, W2), F32), pltpu.VMEM((tq, W2), F32), pltpu.VMEM((tq, W2), F32)],
        compiler_params=_params(("parallel", "parallel")))(qn, kn, proj, cum_q, cum_k)


def _attn_bwd(name, qn, kn, proj, do, o, lse, cum_q, cum_k, H):
    T = qn.shape[0]
    Dh = FOX_HEAD_DIM
    tq = cum_k.shape[3]
    nq = T // tq
    scale = Dh ** -0.5
    nt = (((1,), (1,)), ((), ()))
    tn = (((0,), (0,)), ((), ()))
    assert H <= LANES

    sq = _t(ATTN_SUB, tq)
    rep = tq // LANES
    HP = ATTN_HEADS
    W2 = HP * Dh
    assert H % HP == 0 and Dh == LANES

    def body(q_ref, k_ref, v_ref, do_ref, o_ref, lse_ref, cq_ref, ck_ref, dq_ref, dk_ref, dv_ref, dcq_ref, dck_ref,
             delta, cql, dk_sc, dv_sc, dck_sc):
        h, j = pl.program_id(0), pl.program_id(1)

        @pl.when((h == 0) & (j == 0))
        def _():
            dcq_ref[...] = jnp.zeros_like(dcq_ref)

        @pl.when(j == 0)
        def _():
            dq_ref[...] = jnp.zeros_like(dq_ref)
            for hh in range(HP):
                lanes = slice(hh * Dh, (hh + 1) * Dh)
                delta[hh] = jnp.broadcast_to(jnp.sum(do_ref[:, lanes] * o_ref[:, lanes], axis=1, keepdims=True), (T, LANES))
            cql[...] = cq_ref[...] - lse_ref[...]

        lane_id = lax.broadcasted_iota(jnp.int32, (sq, LANES), 1)
        dk_sc[...] = jnp.zeros_like(dk_sc)
        dv_sc[...] = jnp.zeros_like(dv_sc)
        dck_sc[...] = jnp.zeros_like(dck_sc)
        kloc = lax.broadcasted_iota(jnp.int32, (sq, tq), 1)
        qloc = lax.broadcasted_iota(jnp.int32, (sq, tq), 0)

        def qblk(i, masked):
            for hh in range(HP):
                lanes = slice(hh * Dh, (hh + 1) * Dh)
                k = k_ref[:, lanes]
                v16 = v_ref[:, lanes].astype(BF16)
                ck = ck_ref[hh]
                for r in range(tq // sq):
                    rows = pl.ds(pl.multiple_of(i * tq + r * sq, sq), sq)
                    q = q_ref[rows, lanes]
                    do16 = do_ref[rows, lanes].astype(BF16)
                    e = lax.dot_general(q, k, nt, preferred_element_type=F32) * scale + (jnp.tile(cql[hh, rows, :], (1, rep)) - ck)
                    p = jnp.exp(e)
                    if masked:
                        p = jnp.where(kloc <= qloc + r * sq, p, 0.0)
                    dv_sc[:, lanes] += lax.dot_general(p.astype(BF16), do16, tn, preferred_element_type=F32)
                    dp = lax.dot_general(do16, v16, nt, preferred_element_type=F32)
                    ds = p * (dp - jnp.tile(delta[hh, rows, :], (1, rep)))
                    ds16 = ds.astype(BF16)
                    dk_sc[:, lanes] += lax.dot_general(ds16, q, tn, preferred_element_type=F32)
                    dq_ref[rows, lanes] += jnp.dot(ds16, k, preferred_element_type=F32) * scale
                    dcq_ref[rows, :] += jnp.where(lane_id == h * HP + hh, jnp.sum(ds, axis=1, keepdims=True), 0.0)
                    dck_sc[hh] += jnp.sum(ds, axis=0, keepdims=True)

        def above(i, c):
            qblk(i, False)
            return c

        qblk(j, True)
        lax.fori_loop(j + 1, nq, above, 0)
        dk_ref[...] = dk_sc[...] * scale
        dv_ref[...] = dv_sc[...].astype(BF16)
        for hh in range(HP):
            dck_ref[hh] = -dck_sc[hh]

    whole = lambda off: _bs((T, W2), lambda h, j: (0, h + off))
    blk = lambda off: _bs((tq, W2), lambda h, j: (j, h + off))
    return pl.pallas_call(
        body, name=name, grid=(H // HP, nq),
        in_specs=[whole(0), blk(0), blk(2 * (H // HP)), whole(0), whole(0), _bs((HP, T, LANES), lambda h, j: (h, 0, 0)),
                  _bs((HP, T, LANES), lambda h, j: (h, 0, 0)), _bs((HP, None, 1, tq), lambda h, j: (h, j, 0, 0))],
        out_specs=[whole(0), blk(0), blk(0), _bs((T, LANES), lambda h, j: (0, 0)),
                   _bs((HP, None, 1, tq), lambda h, j: (h, j, 0, 0))],
        out_shape=[jax.ShapeDtypeStruct((T, H * Dh), F32), jax.ShapeDtypeStruct((T, H * Dh), F32), jax.ShapeDtypeStruct((T, H * Dh), BF16),
                   jax.ShapeDtypeStruct((T, LANES), F32), jax.ShapeDtypeStruct((H, nq, 1, tq), F32)],
        scratch_shapes=[pltpu.VMEM((HP, T, LANES), F32), pltpu.VMEM((HP, T, LANES), F32), pltpu.VMEM((tq, W2), F32), pltpu.VMEM((tq, W2), F32),
                        pltpu.VMEM((HP, 1, tq), F32)],
        compiler_params=_params(("arbitrary", "arbitrary")))(qn, kn, proj, do, o, lse, cum_q, cum_k)


def _pool_fwd(name, proj, E):
    T = proj.shape[0]
    PG = len(POOL_WINDOWS)
    PD = E // PG
    tT = _t(256, T)
    hb = tT // POOL_HALO

    def body(u_ref, halo_ref, o_ref, buf):
        g, tb = pl.program_id(0), pl.program_id(1)
        u = u_ref[...]
        buf[pl.ds(POOL_HALO, tT), :] = u
        buf[pl.ds(0, POOL_HALO), :] = jnp.where(tb == 0, 0.0, halo_ref[...])
        t = tb * tT + lax.broadcasted_iota(jnp.int32, (tT, 1), 0)
        for gi, w in enumerate(POOL_WINDOWS):
            @pl.when(g == gi)
            def _():
                acc = u
                for d in range(1, w):
                    acc = acc + buf[pl.ds(POOL_HALO - d, tT), :]
                cnt = jnp.minimum(t + 1, w).astype(F32)
                o_ref[...] = (acc / cnt - u).astype(BF16)

    return pl.pallas_call(
        body, name=name, grid=(PG, T // tT),
        in_specs=[_bs((tT, PD), lambda g, t: (t, g)), _bs((POOL_HALO, PD), lambda g, t: (jnp.maximum(t * hb - 1, 0), g))],
        out_specs=_bs((tT, PD), lambda g, t: (t, g)), out_shape=jax.ShapeDtypeStruct((T, E), BF16),
        scratch_shapes=[pltpu.VMEM((tT + POOL_HALO, PD), F32)],
        compiler_params=_params(("parallel", "parallel")))(proj, proj)


def _pool_bwd(name, dpm, E):
    T = dpm.shape[0]
    PG = len(POOL_WINDOWS)
    PD = E // PG
    tT = _t(256, T)
    hb = tT // POOL_HALO
    nT = T // tT

    def body(d_ref, halo_ref, o_ref, buf):
        g, tb = pl.program_id(0), pl.program_id(1)
        d = d_ref[...]
        t = tb * tT + lax.broadcasted_iota(jnp.int32, (tT, 1), 0)
        th = (tb + 1) * tT + lax.broadcasted_iota(jnp.int32, (POOL_HALO, 1), 0)
        for gi, w in enumerate(POOL_WINDOWS):
            @pl.when(g == gi)
            def _():
                dn = d / jnp.minimum(t + 1, w).astype(F32)
                buf[pl.ds(0, tT), :] = dn
                buf[pl.ds(tT, POOL_HALO), :] = jnp.where(tb == nT - 1, 0.0, halo_ref[...] / jnp.minimum(th + 1, w).astype(F32))
                acc = dn
                for s in range(1, w):
                    acc = acc + buf[pl.ds(s, tT), :]
                o_ref[...] = (acc - d).astype(BF16)

    return pl.pallas_call(
        body, name=name, grid=(PG, nT),
        in_specs=[_bs((tT, PD), lambda g, t: (t, g)), _bs((POOL_HALO, PD), lambda g, t: (jnp.minimum((t + 1) * hb, T // POOL_HALO - 1), g))],
        out_specs=_bs((tT, PD), lambda g, t: (t, g)), out_shape=jax.ShapeDtypeStruct((T, E), BF16),
        scratch_shapes=[pltpu.VMEM((tT + POOL_HALO, PD), F32)],
        compiler_params=_params(("parallel", "parallel")))(dpm, dpm)


def _coords():
    x, y, c = lax.axis_index("x"), lax.axis_index("y"), lax.axis_index("c")
    chips = [(1 - x, y), (x, 1 - y), (1 - x, 1 - y)]
    return x, y, c, 2 * x + y, (x, y, 1 - c), chips


def _chip_allgather(name, bufs):
    n = len(bufs)

    def body(*refs):
        outs = refs[n:2 * n]
        send, recv, fsend, frecv = refs[2 * n:]
        x, y, c, p, sib, chips = _coords()

        def direct(t, j, chip):
            return pltpu.make_async_remote_copy(src_ref=outs[t].at[p, c], dst_ref=outs[t].at[p, c], send_sem=send.at[t, j],
                                                recv_sem=recv.at[t, j], device_id=(*chip, c), device_id_type=MESH)

        def landed(t, j, chip):
            blk = outs[t].at[2 * chip[0] + chip[1], c]
            return pltpu.make_async_remote_copy(src_ref=blk, dst_ref=blk, send_sem=send.at[t, j],
                                                recv_sem=recv.at[t, j], device_id=(*chip, c), device_id_type=MESH)

        def passed(t, j, chip, half):
            blk = outs[t].at[2 * chip[0] + chip[1], half]
            return pltpu.make_async_remote_copy(src_ref=blk, dst_ref=blk, send_sem=fsend.at[t, j], recv_sem=frecv.at[t, j],
                                                device_id=sib, device_id_type=MESH)

        first = [direct(t, j, chip) for t in range(n) for j, chip in enumerate(chips)]
        for cp in first:
            cp.start()
        fwd = []
        for j, chip in enumerate(chips):
            for t in range(n):
                landed(t, j, chip).wait_recv()
                f = passed(t, j, chip, c)
                f.start()
                fwd.append(f)
        for j, chip in enumerate(chips):
            for t in range(n):
                passed(t, j, chip, 1 - c).wait_recv()
        for cp in first + fwd:
            cp.wait_send()

    return pl.pallas_call(
        body, name=name, in_specs=[ANY] * n, out_specs=[ANY] * n,
        out_shape=[jax.ShapeDtypeStruct(a.shape, a.dtype) for a in bufs],
        input_output_aliases={t: t for t in range(n)},
        scratch_shapes=[pltpu.SemaphoreType.DMA((n, 3))] * 4,
    )(*bufs)


SEM = pl.BlockSpec(memory_space=pltpu.SEMAPHORE)
TOKEN = jax.ShapeDtypeStruct((SUB, LANES), F32)


def _split_params():
    return pltpu.CompilerParams(has_side_effects=pltpu.SideEffectType.DATAFLOW_SIDE_EFFECTING)


def _struct(a):
    return jax.ShapeDtypeStruct(a.shape, a.dtype)


def _gather_start(name, bufs, deps):
    n, nd = len(bufs), len(deps)

    def body(*refs):
        outs = refs[n + nd:2 * n + nd]
        send, recv, token = refs[2 * n + nd:]
        x, y, c, p, sib, chips = _coords()
        for t in range(n):
            for j, chip in enumerate(chips):
                pltpu.make_async_remote_copy(src_ref=outs[t].at[p, c], dst_ref=outs[t].at[p, c], send_sem=send.at[3 * t + j],
                                             recv_sem=recv.at[3 * t + j], device_id=(*chip, c), device_id_type=MESH).start()
        token[...] = jnp.zeros_like(token)

    res = pl.pallas_call(
        body, name=name, in_specs=[ANY] * (n + nd), out_specs=[ANY] * n + [SEM, SEM, pl.BlockSpec(memory_space=pltpu.VMEM)],
        out_shape=[_struct(a) for a in bufs] + [pltpu.SemaphoreType.DMA((3 * n,)), pltpu.SemaphoreType.DMA((3 * n,)), TOKEN],
        input_output_aliases={t: t for t in range(n)}, compiler_params=_split_params(),
    )(*bufs, *deps)
    return list(res[:n]), res[n], res[n + 1], res[n + 2]


def _gather_wait(name, bufs, send, recv, after):
    n = len(bufs)

    def body(*refs):
        send_r, recv_r = refs[n], refs[n + 1]
        outs = refs[n + 3:2 * n + 3]
        x, y, c, p, sib, chips = _coords()
        for t in range(n):
            for j, chip in enumerate(chips):
                cp = pltpu.make_async_remote_copy(src_ref=outs[t].at[p, c], dst_ref=outs[t].at[2 * chip[0] + chip[1], c], send_sem=send_r.at[3 * t + j],
                                                  recv_sem=recv_r.at[3 * t + j], device_id=(*chip, c), device_id_type=MESH)
                cp.wait_send()
                cp.wait_recv()

    return list(pl.pallas_call(
        body, name=name, in_specs=[ANY] * n + [SEM, SEM, ANY], out_specs=[ANY] * n, out_shape=[_struct(a) for a in bufs],
        input_output_aliases={t: t for t in range(n)}, compiler_params=_split_params(),
    )(*bufs, send, recv, after))


def _gather_forward(name, bufs):
    n = len(bufs)

    def body(*refs):
        outs = refs[n:2 * n]
        fsend, frecv = refs[2 * n:]
        x, y, c, p, sib, chips = _coords()

        def passed(t, j, chip, half):
            blk = outs[t].at[2 * chip[0] + chip[1], half]
            return pltpu.make_async_remote_copy(src_ref=blk, dst_ref=blk, send_sem=fsend.at[t, j], recv_sem=frecv.at[t, j],
                                                device_id=sib, device_id_type=MESH)

        fwd = [passed(t, j, chip, c) for t in range(n) for j, chip in enumerate(chips)]
        for cp in fwd:
            cp.start()
        for t in range(n):
            for j, chip in enumerate(chips):
                passed(t, j, chip, 1 - c).wait_recv()
        for cp in fwd:
            cp.wait_send()

    return list(pl.pallas_call(
        body, name=name, in_specs=[ANY] * n, out_specs=[ANY] * n, out_shape=[_struct(a) for a in bufs],
        input_output_aliases={t: t for t in range(n)}, scratch_shapes=[pltpu.SemaphoreType.DMA((n, 3))] * 2,
    )(*bufs))


def _relations():
    x, y, c = lax.axis_index("x"), lax.axis_index("y"), lax.axis_index("c")
    out = []
    for code in range(1, 8):
        tx = 1 - x if code & 4 else x
        ty = 1 - y if code & 2 else y
        tc = 1 - c if code & 1 else c
        out.append((code - 1, (tx, ty, tc), 2 * tx + ty, tc))
    return out


def _full_exchange_start(name, parts):
    n = len(parts)
    lands = [lax.empty((7,) + a.shape[2:], a.dtype) for a in parts]

    def body(*refs):
        src, dst = refs[2 * n:3 * n], refs[3 * n:4 * n]
        send, recv, token = refs[4 * n:]
        for t in range(n):
            for k, dev, q, half in _relations():
                pltpu.make_async_remote_copy(src_ref=src[t].at[half, q], dst_ref=dst[t].at[k], send_sem=send.at[7 * t + k],
                                             recv_sem=recv.at[7 * t + k], device_id=dev, device_id_type=MESH).start()
        token[...] = jnp.zeros_like(token)

    res = pl.pallas_call(
        body, name=name, in_specs=[ANY] * (2 * n), out_specs=[ANY] * (2 * n) + [SEM, SEM, pl.BlockSpec(memory_space=pltpu.VMEM)],
        out_shape=[_struct(a) for a in parts + lands] + [pltpu.SemaphoreType.DMA((7 * n,)), pltpu.SemaphoreType.DMA((7 * n,)), TOKEN],
        input_output_aliases={t: t for t in range(2 * n)}, compiler_params=_split_params(),
    )(*parts, *lands)
    return list(res[:n]), list(res[n:2 * n]), res[2 * n], res[2 * n + 1], res[2 * n + 2]


def _full_exchange_wait(name, parts, lands, send, recv, after):
    n = len(parts)

    def body(*refs):
        send_r, recv_r = refs[2 * n], refs[2 * n + 1]
        src, dst = refs[2 * n + 3:3 * n + 3], refs[3 * n + 3:4 * n + 3]
        for t in range(n):
            for k, dev, q, half in _relations():
                cp = pltpu.make_async_remote_copy(src_ref=src[t].at[half, q], dst_ref=dst[t].at[k], send_sem=send_r.at[7 * t + k],
                                                  recv_sem=recv_r.at[7 * t + k], device_id=dev, device_id_type=MESH)
                cp.wait_send()
                cp.wait_recv()

    res = pl.pallas_call(
        body, name=name, in_specs=[ANY] * (2 * n) + [SEM, SEM, ANY], out_specs=[ANY] * (2 * n),
        out_shape=[_struct(a) for a in parts + lands], input_output_aliases={t: t for t in range(2 * n)},
        compiler_params=_split_params(),
    )(*parts, *lands, send, recv, after)
    return list(res[:n]), list(res[n:])


def _chip_exchange_start(name, sums):
    n = len(sums)
    lands = [lax.empty((3,) + a.shape[1:], a.dtype) for a in sums]

    def body(*refs):
        src, dst = refs[2 * n:3 * n], refs[3 * n:4 * n]
        send, recv, token = refs[4 * n:]
        x, y, c, p, sib, chips = _coords()
        for t in range(n):
            for j, chip in enumerate(chips):
                pltpu.make_async_remote_copy(src_ref=src[t].at[2 * chip[0] + chip[1]], dst_ref=dst[t].at[j], send_sem=send.at[3 * t + j],
                                             recv_sem=recv.at[3 * t + j], device_id=(*chip, c), device_id_type=MESH).start()
        token[...] = jnp.zeros_like(token)

    res = pl.pallas_call(
        body, name=name, in_specs=[ANY] * (2 * n), out_specs=[ANY] * (2 * n) + [SEM, SEM, pl.BlockSpec(memory_space=pltpu.VMEM)],
        out_shape=[_struct(a) for a in sums + lands] + [pltpu.SemaphoreType.DMA((3 * n,)), pltpu.SemaphoreType.DMA((3 * n,)), TOKEN],
        input_output_aliases={t: t for t in range(2 * n)}, compiler_params=_split_params(),
    )(*sums, *lands)
    return list(res[:n]), list(res[n:2 * n]), res[2 * n], res[2 * n + 1], res[2 * n + 2]


def _chip_exchange_wait(name, sums, lands, send, recv, after):
    n = len(sums)

    def body(*refs):
        send_r, recv_r = refs[2 * n], refs[2 * n + 1]
        src, dst = refs[2 * n + 3:3 * n + 3], refs[3 * n + 3:4 * n + 3]
        x, y, c, p, sib, chips = _coords()
        for t in range(n):
            for j, chip in enumerate(chips):
                cp = pltpu.make_async_remote_copy(src_ref=src[t].at[2 * chip[0] + chip[1]], dst_ref=dst[t].at[j], send_sem=send_r.at[3 * t + j],
                                                  recv_sem=recv_r.at[3 * t + j], device_id=(*chip, c), device_id_type=MESH)
                cp.wait_send()
                cp.wait_recv()

    res = pl.pallas_call(
        body, name=name, in_specs=[ANY] * (2 * n) + [SEM, SEM, ANY], out_specs=[ANY] * (2 * n),
        out_shape=[_struct(a) for a in sums + lands], input_output_aliases={t: t for t in range(2 * n)},
        compiler_params=_split_params(),
    )(*sums, *lands, send, recv, after)
    return list(res[:n]), list(res[n:])


def _pair_exchange(name, parts):
    n = len(parts)

    def body(*refs):
        ins, outs = refs[:n], refs[n:2 * n]
        send, recv = refs[2 * n:]
        x, y, c, p, sib, chips = _coords()
        cps = [pltpu.make_async_remote_copy(src_ref=ins[t].at[1 - c], dst_ref=outs[t], send_sem=send.at[t], recv_sem=recv.at[t],
                                            device_id=sib, device_id_type=MESH) for t in range(n)]
        for cp in cps:
            cp.start()
        for cp in cps:
            cp.wait()

    return pl.pallas_call(
        body, name=name, in_specs=[ANY] * n, out_specs=[ANY] * n,
        out_shape=[jax.ShapeDtypeStruct(a.shape[1:], a.dtype) for a in parts],
        scratch_shapes=[pltpu.SemaphoreType.DMA((n,))] * 2,
    )(*parts)


def _chip_exchange(name, sums):
    n = len(sums)

    def body(*refs):
        ins, outs = refs[:n], refs[n:2 * n]
        send, recv = refs[2 * n:]
        x, y, c, p, sib, chips = _coords()
        cps = [pltpu.make_async_remote_copy(src_ref=ins[t].at[2 * chip[0] + chip[1]], dst_ref=outs[t].at[j], send_sem=send.at[t, j],
                                            recv_sem=recv.at[t, j], device_id=(*chip, c), device_id_type=MESH)
               for t in range(n) for j, chip in enumerate(chips)]
        for cp in cps:
            cp.start()
        for cp in cps:
            cp.wait()

    return pl.pallas_call(
        body, name=name, in_specs=[ANY] * n, out_specs=[ANY] * n,
        out_shape=[jax.ShapeDtypeStruct((3,) + a.shape[1:], a.dtype) for a in sums],
        scratch_shapes=[pltpu.SemaphoreType.DMA((n, 3))] * 2,
    )(*sums)


def _pair_share(name, bufs, items, deps=()):
    n = len(items)
    nb = len(bufs)
    nd = len(deps)

    def body(*refs):
        outs = refs[nb + nd:2 * nb + nd]
        send, recv = refs[2 * nb + nd:]
        x, y, c, p, sib, chips = _coords()

        def blk(t, half):
            o, lead = items[t]
            return outs[o].at[p if lead == 'chip' else lead, half]

        def swap(t, half):
            return pltpu.make_async_remote_copy(src_ref=blk(t, half), dst_ref=blk(t, half), send_sem=send.at[t], recv_sem=recv.at[t],
                                                device_id=sib, device_id_type=MESH)

        cps = [swap(t, c) for t in range(n)]
        for cp in cps:
            cp.start()
        for t in range(n):
            swap(t, 1 - c).wait_recv()
        for cp in cps:
            cp.wait_send()

    return list(pl.pallas_call(
        body, name=name, in_specs=[ANY] * (nb + nd), out_specs=[ANY] * nb,
        out_shape=[jax.ShapeDtypeStruct(b.shape, b.dtype) for b in bufs],
        input_output_aliases={t: t for t in range(nb)},
        scratch_shapes=[pltpu.SemaphoreType.DMA((n,))] * 2,
    )(*bufs, *deps))


def _flat2(a, lead):
    return a.reshape(a.shape[:lead] + (-1, a.shape[-1]))


def _reduce_begin(tag, parts):
    parts, lands, send, recv, token = _full_exchange_start(f"rs_start_{tag}", parts)
    return (parts, lands, send, recv), token


def _reduce_end(tag, state, after, dests, bufs, buf_shapes):
    c = lax.axis_index("c").astype(jnp.int32)
    p = (2 * lax.axis_index("x") + lax.axis_index("y")).astype(jnp.int32)
    parts, lands = _full_exchange_wait(f"rs_wait_{tag}", *state, after)

    def total(a, *others):
        s = a.astype(F32)
        for b in others:
            s = s + b.astype(F32)
        return (s,)

    for t, (mine, theirs) in enumerate(zip(parts, lands)):
        o, lead = dests[t]
        shape = buf_shapes[o]
        rows, cols = shape[2], shape[3]
        m3, t3 = mine.reshape(2 * N_CHIPS, rows, cols), theirs.reshape(7, rows, cols)
        pre = jnp.stack([c * N_CHIPS + p] + [jnp.int32(k) for k in range(7)] + [c, p if lead == 'chip' else jnp.int32(lead)])
        out = ('x', shape, F32, (None, None, 'tr', cols), lambda r, pr: (pr[9], pr[8], r, 0))
        bufs[o] = _rows(f"rs_sum_{tag}_{t}", total, [(m3, 's', cols, 0)] + [(t3, 's', cols, 1 + k) for k in range(7)], [out], 256,
                        pre=pre, into=bufs[o])[0]


def kernel(x, norm_w, out_proj, s5_in_proj, s5_a_re, s5_a_im, s5_log_dt, s5_b_re, s5_b_im, s5_c_re, s5_c_im, s5_d, s5_w_glu, s5_b_glu, fox_in_proj, fox_q_norm, fox_k_norm, fox_f_bias, pool_in_proj, pool_w_group, pool_scale, loss_target, m_norm_w, m_out_proj, m_s5_in_proj, m_s5_a_re, m_s5_a_im, m_s5_log_dt, m_s5_b_re, m_s5_b_im, m_s5_c_re, m_s5_c_im, m_s5_d, m_s5_w_glu, m_s5_b_glu, m_fox_in_proj, m_fox_q_norm, m_fox_k_norm, m_fox_f_bias, m_pool_in_proj, m_pool_w_group, m_pool_scale, v_norm_w, v_out_proj, v_s5_in_proj, v_s5_a_re, v_s5_a_im, v_s5_log_dt, v_s5_b_re, v_s5_b_im, v_s5_c_re, v_s5_c_im, v_s5_d, v_s5_w_glu, v_s5_b_glu, v_fox_in_proj, v_fox_q_norm, v_fox_k_norm, v_fox_f_bias, v_pool_in_proj, v_pool_w_group, v_pool_scale):
    weights = dict(norm_w=norm_w, out_proj=out_proj, s5_in_proj=s5_in_proj, s5_a_re=s5_a_re, s5_a_im=s5_a_im, s5_log_dt=s5_log_dt,
                   s5_b_re=s5_b_re, s5_b_im=s5_b_im, s5_c_re=s5_c_re, s5_c_im=s5_c_im, s5_d=s5_d, s5_w_glu=s5_w_glu, s5_b_glu=s5_b_glu,
                   fox_in_proj=fox_in_proj, fox_q_norm=fox_q_norm, fox_k_norm=fox_k_norm, fox_f_bias=fox_f_bias,
                   pool_in_proj=pool_in_proj, pool_w_group=pool_w_group, pool_scale=pool_scale)
    mom_m = dict(norm_w=m_norm_w, out_proj=m_out_proj, s5_in_proj=m_s5_in_proj, s5_a_re=m_s5_a_re, s5_a_im=m_s5_a_im, s5_log_dt=m_s5_log_dt,
                 s5_b_re=m_s5_b_re, s5_b_im=m_s5_b_im, s5_c_re=m_s5_c_re, s5_c_im=m_s5_c_im, s5_d=m_s5_d, s5_w_glu=m_s5_w_glu, s5_b_glu=m_s5_b_glu,
                 fox_in_proj=m_fox_in_proj, fox_q_norm=m_fox_q_norm, fox_k_norm=m_fox_k_norm, fox_f_bias=m_fox_f_bias,
                 pool_in_proj=m_pool_in_proj, pool_w_group=m_pool_w_group, pool_scale=m_pool_scale)
    mom_v = dict(norm_w=v_norm_w, out_proj=v_out_proj, s5_in_proj=v_s5_in_proj, s5_a_re=v_s5_a_re, s5_a_im=v_s5_a_im, s5_log_dt=v_s5_log_dt,
                 s5_b_re=v_s5_b_re, s5_b_im=v_s5_b_im, s5_c_re=v_s5_c_re, s5_c_im=v_s5_c_im, s5_d=v_s5_d, s5_w_glu=v_s5_w_glu, s5_b_glu=v_s5_b_glu,
                 fox_in_proj=v_fox_in_proj, fox_q_norm=v_fox_q_norm, fox_k_norm=v_fox_k_norm, fox_f_bias=v_fox_f_bias,
                 pool_in_proj=v_pool_in_proj, pool_w_group=v_pool_w_group, pool_scale=v_pool_scale)
    return _step(x, loss_target, weights, mom_m, mom_v)


BIG = ('out_proj', 's5_in_proj', 's5_w_glu', 'fox_in_proj', 'pool_in_proj', 'pool_w_group')
SMALL = ('norm_w', 's5_a_re', 's5_a_im', 's5_log_dt', 's5_b_re', 's5_b_im', 's5_c_re', 's5_c_im', 's5_d', 's5_b_glu',
         'fox_q_norm', 'fox_k_norm', 'fox_f_bias', 'pool_scale')
SMALL_SHARDED = ('s5_d', 's5_b_glu', 'pool_scale')
GROUP_AXIS_1 = ('s5_a_re', 's5_a_im', 's5_b_re', 's5_b_im', 's5_c_re', 's5_c_im')
ORDER = ('norm_w', 'out_proj', 's5_in_proj', 's5_a_re', 's5_a_im', 's5_log_dt', 's5_b_re', 's5_b_im', 's5_c_re', 's5_c_im', 's5_d',
         's5_w_glu', 's5_b_glu', 'fox_in_proj', 'fox_q_norm', 'fox_k_norm', 'fox_f_bias', 'pool_in_proj', 'pool_w_group', 'pool_scale')


def _split2(shape):
    if shape[0] % 2 == 0:
        return (2, shape[0] // 2) + tuple(shape[1:])
    assert shape[0] == 1 and shape[1] % 2 == 0
    return (2, shape[1] // 2) + tuple(shape[2:])


def _adamw_big(n, w, grads, mom_m, mom_v, delta, new_m, new_v):
    shape = w[n].shape
    if shape[-1] % LANES:
        f2 = lambda a: jnp.transpose(a.reshape(-1, shape[-1]))
        b2 = lambda a: jnp.transpose(a).reshape(shape)
    else:
        f2 = lambda a: a.reshape(-1, shape[-1])
        b2 = lambda a: a.reshape(shape)
    d_, m_, v_ = _adamw(f"adamw_{n}", f2(w[n]), f2(grads[n]), f2(mom_m[n]), f2(mom_v[n]))
    delta[n], new_m[n], new_v[n] = b2(d_), b2(m_), b2(v_)
    return d_


def _cast_weights(w):
    p = (2 * lax.axis_index("x") + lax.axis_index("y")).astype(jnp.int32)
    bufs = {}
    for n in BIG:
        a3 = w[n].reshape(w[n].shape[0], -1, w[n].shape[-1])
        layers, rows, cols = a3.shape
        for l in range(layers):
            out = ('x', (N_CHIPS, rows, cols), BF16, (None, 'tr', cols), lambda r, pr: (pr[0], r, 0))
            b = _rows(f"cast_{n}_{l}", lambda v: (v,), [(a3, 's', cols, 1)], [out], 256, pre=jnp.stack([p, jnp.int32(l)]))[0]
            bufs[(n, l)] = b.reshape(N_CHIPS, 2, rows // 2, cols)
    return bufs


def _step(x, loss_target, w, mom_m, mom_v):
    T, D = x.shape[1], x.shape[2]
    E = D
    G, P, C = w['s5_a_re'].shape[1], S5_STATE, S5_GROUP
    H = E // FOX_HEAD_DIM
    PG = len(POOL_WINDOWS)
    PD = E // PG
    NC = G // GROUPS_PER_CHUNK
    L = GROUPS_PER_CHUNK * P
    tq = _t(256, T)
    nq = T // tq

    wb = _cast_weights(w)
    phases = [[('s5_in_proj', 0)],
              [('s5_w_glu', 0), ('out_proj', 0)],
              [('out_proj', 1), ('fox_in_proj', 0)],
              [('out_proj', 2), ('pool_in_proj', 0), ('pool_w_group', 0), ('out_proj', 3), ('s5_in_proj', 1), ('s5_w_glu', 1)]]
    W = {}
    flight = {}

    def landed(keys, bufs):
        for k, b in zip(keys, bufs):
            W[k] = b.reshape(N_CHIPS, 2 * b.shape[2], b.shape[3])

    def take_phase(ph, after):
        bufs, send, recv, _ = flight.pop(ph)
        landed(phases[ph], _gather_forward(f"gather_{ph}_pass", _gather_wait(f"gather_{ph}_wait", bufs, send, recv, after)))

    small_full = {}
    chip = 2 * lax.axis_index("x") + lax.axis_index("y")
    sv = [lax.dynamic_update_index_in_dim(jnp.zeros((N_CHIPS, 2) + w[n].shape, F32), jnp.stack([w[n], w[n]]), chip, 0)
          for n in SMALL_SHARDED]
    got = _chip_allgather("gather_vectors", sv)
    for n, g in zip(SMALL_SHARDED, got):
        small_full[n] = jnp.transpose(g[:, 0], (1, 0, 2)).reshape(w[n].shape[0], E)
    landed(phases[0], _chip_allgather("gather_0", [wb[k] for k in phases[0]]))
    after = [W[phases[0][0]], got[0]]
    for ph in range(1, len(phases)):
        flight[ph] = _gather_start(f"gather_{ph}_start", [wb[k] for k in phases[ph]], after)
        after = [flight[ph][3]]
    gather_tokens = after

    norm_w = w['norm_w']
    h = x.reshape(T, D)
    saved = []
    dparts = {}

    def s5_consts(j):
        ar, ai, fr, fi = _s5_disc_fwd(f"s5_disc_{j}", w['s5_a_re'][j], w['s5_a_im'][j], w['s5_log_dt'][j].reshape(G, 1))
        br, bi = w['s5_b_re'][j].reshape(G * P, C), w['s5_b_im'][j].reshape(G * P, C)
        bbr, bbi = _s5_bbar(f"s5_bbar_{j}", fr.reshape(G * P, 1), fi.reshape(G * P, 1), br, bi)
        bbd = jnp.concatenate([_compact(bbr.reshape(G, P, C), NC), _compact(bbi.reshape(G, P, C), NC)], axis=2).astype(BF16)
        ct = lambda v: jnp.transpose(v, (0, 2, 1))
        cbd = jnp.concatenate([_compact(ct(w['s5_c_re'][j]), NC), -_compact(ct(w['s5_c_im'][j]), NC)], axis=2).astype(BF16)
        return dict(ar=ar, ai=ai, fr=fr, fi=fi, br=br, bi=bi, bbd=bbd, cbd=cbd,
                    ar3=ar.reshape(NC, 1, L), ai3=ai.reshape(NC, 1, L))

    for i in range(4):
        kind, j = i % 3, i // 3
        nw = norm_w[i].reshape(1, D)
        xn = _norm_fwd(f"norm_{i}", h, nw, deps=gather_tokens if i == 0 else ())
        if kind == 0:
            k5 = s5_consts(j)
            proj = _mm_proj(f"s5_proj_{i}", xn, W[('s5_in_proj', j)])
            dsk = small_full['s5_d'][j].reshape(1, E)
            y1, g, hs = _s5_fwd(f"s5_scan_{i}", proj, k5['bbd'], k5['cbd'], k5['ar3'], k5['ai3'], dsk, E)
            bglu = small_full['s5_b_glu'][j].reshape(1, E)
            if i == 0:
                take_phase(1, y1)

            def glu_epi(acc, b, y1t, z):
                lin = acc + b
                return lin, (_gelu(y1t) * _sigmoid(lin)) * _silu(z)

            lin, a = _mm_rowsharded(
                f"s5_glu_{i}", g, W[('s5_w_glu', j)], epi=glu_epi,
                extras=lambda tm, tn: [(bglu, _rowvec(tn)), (y1, _tile(tm, tn)), (proj, _tile(tm, tn, E // tn))],
                outs_fn=lambda tm, tn: [((T, E), F32, _tile(tm, tn)), ((T, E), BF16, _tile(tm, tn))])
            saved.append(dict(h=h, xn=xn, proj=proj, y1=y1, g=g, hs=hs, lin=lin, a=a, k5=k5, dsk=dsk))
        elif kind == 1:
            fox_w = jnp.transpose(W[('fox_in_proj', j)], (1, 0, 2)).reshape(D, -1)
            w_qkvz = fox_w[:, :4 * E]
            w_f = jnp.pad(fox_w[:, 4 * E:], ((0, 0), (0, LANES - H)))
            proj = _mm_plain(f"fox_proj_{i}", xn, w_qkvz)[0]
            flog = _mm_plain(f"fox_gate_proj_{i}", xn, w_f)[0]
            fb = jnp.pad(w['fox_f_bias'][j].reshape(1, H), ((0, 0), (0, LANES - H)))
            wq, wk = w['fox_q_norm'][j].reshape(1, FOX_HEAD_DIM), w['fox_k_norm'][j].reshape(1, FOX_HEAD_DIM)
            qn, kn = _qk_norm(f"fox_qk_norm_{i}", proj, wq, wk, H)
            cum = _cum_rows(f"fox_cum_{i}", flog, fb, False, True)
            cum_t = jnp.transpose(cum)[:H]
            cum_q = jnp.broadcast_to(cum_t[:, :, None], (H, T, LANES))
            cum_k = cum_t.reshape(H, nq, 1, tq)
            y, lse = _attn_fwd(f"fox_attn_{i}", qn, kn, proj, cum_q, cum_k, H)
            a = _rows(f"fox_gate_{i}", lambda yt, z: (yt * _silu(z),), [(y, 'r', E, 0), (proj, 'r', E, 3)], [('r', E, BF16)], 256)[0]
            saved.append(dict(h=h, xn=xn, proj=proj, flog=flog, fb=fb, wq=wq, wk=wk, qn=qn, kn=kn, cum_q=cum_q, cum_k=cum_k, y=y, lse=lse, a=a,
                              w_qkvz=w_qkvz, w_f=w_f))
        else:
            w_pg = W[('pool_w_group', j)].reshape(N_CHIPS, PG, PD // N_CHIPS, PD)
            proj = _mm_proj(f"pool_proj_{i}", xn, W[('pool_in_proj', j)])
            pm = _pool_fwd(f"pool_win_{i}", proj, E)
            scale = small_full['pool_scale'][j].reshape(1, E)
            tm, tn, tk = _t(512, T), _t(512, PD), w_pg.shape[2]
            kb, nb = PD // tk, PD // tn
            mixed, a = _mm(
                f"pool_mix_{i}", pm, w_pg, M=T, N=PD, K=PD, tm=tm, tn=tn, tk=tk, groups=PG,
                a_spec=_bs((tm, tk), lambda g, m, n, k: (m, g * kb + k)),
                b_spec=_bs((None, None, tk, tn), lambda g, m, n, k: (k, g, 0, n)),
                extras=[(scale, _bs((1, tn), lambda g, m, n, k: (0, g * nb + n))),
                        (proj, _bs((tm, tn), lambda g, m, n, k: (m, E // tn + g * nb + n)))],
                epi=lambda acc, sc, z: (acc, (acc * sc) * _silu(z)),
                outs=[((T, E), F32, _bs((tm, tn), lambda g, m, n, k: (m, g * nb + n))),
                      ((T, E), BF16, _bs((tm, tn), lambda g, m, n, k: (m, g * nb + n)))])
            saved.append(dict(h=h, xn=xn, proj=proj, pm=pm, mixed=mixed, scale=scale, a=a, w_pg=w_pg))
        h = _mm_rowsharded(f"out_proj_{i}", saved[-1]['a'], W[('out_proj', i)], epi=lambda acc, r: (r + acc,),
                           extras=lambda tm, tn: [(h, _tile(tm, tn))],
                           outs_fn=lambda tm, tn: [((T, D), F32, _tile(tm, tn))])[0]
        if i < 2:
            take_phase(i + 2, h)

    dh, dh16, loss_cols = _loss(h, loss_target.reshape(T, D))
    loss = lax.psum(jnp.sum(loss_cols), ("x", "y", "c"))

    gsmall = {n: [None] * w[n].shape[0] for n in SMALL}
    big_index = {n: o for o, n in enumerate(BIG)}
    rs_shapes = [None] * (len(BIG) + 1)
    rs_bufs = [None] * (len(BIG) + 1)
    rs_dests_all = []
    pending = None

    def reduce_layer(tag, named_parts):
        parts, dests = [], []
        for n, l, pt in named_parts:
            o = big_index[n] if n in big_index else len(BIG)
            half = pt.shape[2:]
            rs_shapes[o] = (N_CHIPS if l == 'chip' else w[n].shape[0], 2, math.prod(half[:-1]), half[-1])
            parts.append(pt)
            dests.append((o, l))
        rs_dests_all.extend(dests)
        state, token = _reduce_begin(tag, parts)
        return (tag, state, dests), token

    token = None
    for i in reversed(range(4)):
        kind, j = i % 3, i // 3
        sv_ = saved[i]
        nw = norm_w[i].reshape(1, D)
        w_out = W[('out_proj', i)]
        after_start = [token] if token is not None else ()
        layer_parts = [('out_proj', i, _mm_dw_rows(f"d_out_proj_{i}", sv_['a'], dh16, deps=after_start))]
        if kind == 0:
            w_glu = W[('s5_w_glu', j)]
            proj, y1, lin, k5 = sv_['proj'], sv_['y1'], sv_['lin'], sv_['k5']

            def da_epi(da, y1t, lint, z):
                gt, sg = _gelu(y1t), _sigmoid(lint)
                dy2 = da * _silu(z)
                dlin = (dy2 * gt) * (sg * (1.0 - sg))
                return da * (gt * sg) * _dsilu(z), dlin, dy2 * sg, _colsum(dlin)

            nm = T // _t(512, T)
            dz, dlin, dgd, dbg = _mm_rowsharded_t(
                f"d_s5_act_{i}", dh16, w_out, epi=da_epi, deps=after_start,
                extras=lambda tm, tn: [(y1, _tile(tm, tn)), (lin, _tile(tm, tn)), (proj, _tile(tm, tn, E // tn))],
                outs_fn=lambda tm, tn: [((T, E), BF16, _tile(tm, tn)), ((T, E), BF16, _tile(tm, tn)), ((T, E), F32, _tile(tm, tn)),
                                        ((nm, 1, E), F32, _bs((None, 1, tn), lambda g, m, n, k: (m, 0, n)))])
            gsmall['s5_b_glu'][j] = jnp.sum(dbg, axis=(0, 1))
            layer_parts.append(('s5_w_glu', j, _mm_dw_rows(f"d_s5_w_glu_{i}", sv_['g'], dlin)))
            glu_deps = ()
            if i == 0:
                early, early_token = reduce_layer("l0a", layer_parts)
                layer_parts, glu_deps = [], [early_token]
            dy1 = _mm_rowsharded_t(
                f"d_s5_glu_{i}", dlin, w_glu, epi=lambda acc, d, y1t: ((acc + d) * _dgelu(y1t),), deps=glu_deps,
                extras=lambda tm, tn: [(dgd, _tile(tm, tn)), (y1, _tile(tm, tn))],
                outs_fn=lambda tm, tn: [((T, E), F32, _tile(tm, tn))])[0]
            du, dbd, dcd, dab, ddk = _s5_bwd(f"d_s5_scan_{i}", dy1, proj, sv_['hs'], k5['bbd'], k5['cbd'], k5['ar3'], k5['ai3'], sv_['dsk'], E)
            gsmall['s5_d'][j] = ddk.reshape(E)
            gsmall['s5_c_re'][j] = jnp.transpose(_uncompact(dcd[:, :, :L], G), (0, 2, 1))
            gsmall['s5_c_im'][j] = -jnp.transpose(_uncompact(dcd[:, :, L:], G), (0, 2, 1))
            dbbr = _uncompact(dbd[:, :, :L], G).reshape(G * P, C)
            dbbi = _uncompact(dbd[:, :, L:], G).reshape(G * P, C)
            dbr, dbi, dfr, dfi = _s5_bbar_bwd(f"d_s5_bbar_{i}", k5['fr'].reshape(G * P, 1), k5['fi'].reshape(G * P, 1), k5['br'], k5['bi'], dbbr, dbbi)
            gsmall['s5_b_re'][j] = dbr.reshape(G, P, C)
            gsmall['s5_b_im'][j] = dbi.reshape(G, P, C)
            dab = jnp.sum(dab, axis=1)
            dare, daim, dldt = _s5_disc_bwd(f"d_s5_disc_{i}", w['s5_a_re'][j], w['s5_a_im'][j], w['s5_log_dt'][j].reshape(G, 1),
                                            (dab[:, :L].reshape(G, P), dab[:, L:].reshape(G, P), dfr.reshape(G, P), dfi.reshape(G, P)))
            gsmall['s5_a_re'][j], gsmall['s5_a_im'][j], gsmall['s5_log_dt'][j] = dare, daim, dldt.reshape(G)
            dproj = jnp.concatenate([du, dz], axis=1)
            layer_parts.append(('s5_in_proj', j, _mm_dw_cols(f"d_s5_in_proj_{i}", sv_['xn'], dproj)))
            dxn = _mm_colsharded_t(f"d_s5_xn_{i}", dproj, W[('s5_in_proj', j)])
        elif kind == 1:
            proj, y = sv_['proj'], sv_['y']
            do, dz = _mm_rowsharded_t(
                f"d_fox_act_{i}", dh16, w_out, epi=lambda da, yt, z: (da * _silu(z), (da * yt) * _dsilu(z)), deps=after_start,
                extras=lambda tm, tn: [(y, _tile(tm, tn)), (proj, _tile(tm, tn, 3 * E // tn))],
                outs_fn=lambda tm, tn: [((T, E), F32, _tile(tm, tn)), ((T, E), BF16, _tile(tm, tn))])
            dqn, dkn, dv, dcq, dck = _attn_bwd(f"d_fox_attn_{i}", sv_['qn'], sv_['kn'], proj, do, y, sv_['lse'], sv_['cum_q'], sv_['cum_k'], H)
            dq, dk, dwq, dwk = _qk_norm_bwd(f"d_fox_qk_norm_{i}", proj, sv_['wq'], sv_['wk'], dqn, dkn, H)
            gsmall['fox_q_norm'][j], gsmall['fox_k_norm'][j] = dwq.reshape(-1), dwk.reshape(-1)
            dcum = dcq + jnp.pad(jnp.transpose(dck.reshape(H, T)), ((0, 0), (0, LANES - H)))
            dls = _cum_rows(f"d_fox_cum_{i}", dcum, jnp.zeros((1, LANES), F32), True, False)
            dflog, dfb = _rows(f"d_fox_gate_{i}", lambda d, f, b: ((lambda r: (r, _colsum(r)))(d * _sigmoid(-(f + b)))),
                               [(dls, 'r', LANES, 0), (sv_['flog'], 'r', LANES, 0), (sv_['fb'], 'b', LANES, 0)],
                               [('r', LANES, BF16), ('a', LANES, F32)], 256)
            gsmall['fox_f_bias'][j] = dfb[0, :H]
            dproj = jnp.concatenate([dq, dk, dv, dz], axis=1)
            tkT = _t(K_STEP, T)
            dw_qkvz = _mm(f"d_fox_in_proj_{i}", sv_['xn'], dproj, M=D, N=4 * E, K=T, tm=_t(512, D), tn=_t(1024, 4 * E), tk=tkT, ta=True,
                          a_spec=_bs((tkT, _t(512, D)), lambda g, m, n, k: (k, m)),
                          b_spec=_bs((tkT, _t(1024, 4 * E)), lambda g, m, n, k: (k, n)),
                          outs=[((D, 4 * E), BF16, _tile(_t(512, D), _t(1024, 4 * E)))])[0]
            dw_f = _mm(f"d_fox_gate_proj_{i}", sv_['xn'], dflog, M=D, N=LANES, K=T, tm=_t(512, D), tn=LANES, tk=tkT, ta=True,
                       a_spec=_bs((tkT, _t(512, D)), lambda g, m, n, k: (k, m)),
                       b_spec=_bs((tkT, LANES), lambda g, m, n, k: (k, n)),
                       outs=[((D, LANES), BF16, _tile(_t(512, D), LANES))])[0]
            dw_fox = jnp.concatenate([dw_qkvz, dw_f[:, :H]], axis=1)
            sw = dw_fox.shape[1] // N_CHIPS
            layer_parts.append(('fox_in_proj', j, jnp.transpose(dw_fox.reshape(2, D // 2, N_CHIPS, sw), (0, 2, 1, 3))))
            w_qkvz, w_f = sv_['w_qkvz'], sv_['w_f']
            dxn_f = _mm(f"d_fox_xn_gate_{i}", dflog, w_f, M=T, N=D, K=LANES, tm=_t(512, T), tn=_t(1024, D), tk=LANES, tb=True,
                        a_spec=_bs((_t(512, T), LANES), lambda g, m, n, k: (m, k)),
                        b_spec=_bs((_t(1024, D), LANES), lambda g, m, n, k: (n, k)),
                        outs=[((T, D), F32, _tile(_t(512, T), _t(1024, D)))])[0]
            tm, tn, tk = _t(512, T), _t(1024, D), _t(1024, 4 * E)
            dxn = _mm(f"d_fox_xn_{i}", dproj, w_qkvz, M=T, N=D, K=4 * E, tm=tm, tn=tn, tk=tk, tb=True,
                      a_spec=_bs((tm, tk), lambda g, m, n, k: (m, k)), b_spec=_bs((tn, tk), lambda g, m, n, k: (n, k)),
                      extras=[(dxn_f, _tile(tm, tn))], epi=lambda acc, e: (acc + e,),
                      outs=[((T, D), F32, _tile(tm, tn))])[0]
        else:
            proj, mixed, scale = sv_['proj'], sv_['mixed'], sv_['scale']
            nm = T // _t(512, T)

            def pool_epi(da, mx, sc, z):
                dy = da * _silu(z)
                return (da * (mx * sc)) * _dsilu(z), dy * sc, _colsum(dy * mx)

            dz, dmix, dsc = _mm_rowsharded_t(
                f"d_pool_act_{i}", dh16, w_out, epi=pool_epi, deps=after_start,
                extras=lambda tm, tn: [(mixed, _tile(tm, tn)), (scale, _rowvec(tn)), (proj, _tile(tm, tn, E // tn))],
                outs_fn=lambda tm, tn: [((T, E), BF16, _tile(tm, tn)), ((T, E), BF16, _tile(tm, tn)),
                                        ((nm, 1, E), F32, _bs((None, 1, tn), lambda g, m, n, k: (m, 0, n)))])
            gsmall['pool_scale'][j] = jnp.sum(dsc, axis=(0, 1))
            w_pg = sv_['w_pg']
            tkw = w_pg.shape[2]
            tk = _t(K_STEP, T)
            layer_parts.append(('pool_w_group', j, _mm(
                f"d_pool_w_group_{i}", sv_['pm'], dmix, M=PD, N=PD, K=T, tm=tkw, tn=PD, tk=tk, groups=PG, ta=True,
                a_spec=_bs((tk, tkw), lambda g, m, n, k: (k, g * (PD // tkw) + m)),
                b_spec=_bs((tk, PD), lambda g, m, n, k: (k, g)),
                outs=[((2, N_CHIPS, PG // 2, tkw, PD), BF16, _bs((None, None, None, tkw, PD), lambda g, m, n, k: (g // (PG // 2), m, g % (PG // 2), 0, 0)))])[0]))
            tm, tk2 = _t(512, T), _t(512, PD)
            dpm = _mm(f"d_pool_mix_{i}", dmix, w_pg, M=T, N=PD, K=PD, tm=tm, tn=tkw, tk=tk2, groups=PG, tb=True,
                      a_spec=_bs((tm, tk2), lambda g, m, n, k: (m, g * (PD // tk2) + k)),
                      b_spec=_bs((None, None, tkw, tk2), lambda g, m, n, k: (n, g, 0, k)),
                      outs=[((T, E), F32, _bs((tm, tkw), lambda g, m, n, k: (m, g * (PD // tkw) + n)))])[0]
            du = _pool_bwd(f"d_pool_win_{i}", dpm, E)
            dproj = jnp.concatenate([du, dz], axis=1)
            layer_parts.append(('pool_in_proj', j, _mm_dw_cols(f"d_pool_in_proj_{i}", sv_['xn'], dproj)))
            dxn = _mm_colsharded_t(f"d_pool_xn_{i}", dproj, W[('pool_in_proj', j)])
        dh, dh16, dnw = _norm_bwd(f"d_norm_{i}", dxn, sv_['h'], nw, dh)
        gsmall['norm_w'][i] = dnw.reshape(D)
        if pending is not None:
            _reduce_end(pending[0], pending[1], dh16, pending[2], rs_bufs, rs_shapes)
        if i > 0:
            pending, token = reduce_layer(f"l{i}", layer_parts)
    grad_x = dh.reshape(x.shape)

    small_flat = jnp.concatenate([jnp.stack(gsmall[n]).reshape(-1) for n in SMALL])
    n_small = small_flat.shape[0]
    unit = 2 * N_CHIPS * 16 * LANES
    n_pad = -(-n_small // unit) * unit
    R = n_pad // (2 * N_CHIPS * LANES)
    small_part = jnp.pad(small_flat, (0, n_pad - n_small)).astype(BF16).reshape(2, N_CHIPS, R, LANES)
    pending, token = reduce_layer("l0", layer_parts + [('small', 'chip', small_part)])
    _reduce_end(early[0], early[1], token, early[2], rs_bufs, rs_shapes)
    nb = len(BIG)
    done_items = [d for d in rs_dests_all if d not in pending[2]]
    rs_bufs[:nb] = _pair_share("rs_pair_share_a", rs_bufs[:nb], done_items, deps=[token])
    late = [o for o, _ in pending[2]]
    delta, new_m, new_v = {}, {}, {}
    grads = {}
    last = token
    for o, n in enumerate(BIG):
        if o not in late:
            grads[n] = rs_bufs[o].reshape(w[n].shape)
            last = _adamw_big(n, w, grads, mom_m, mom_v, delta, new_m, new_v)
    _reduce_end(pending[0], pending[1], last, pending[2], rs_bufs, rs_shapes)
    shared = _pair_share("rs_pair_share_b", [rs_bufs[o] for o in late], [(k, l) for k, (_, l) in enumerate(pending[2])])
    for k, o in enumerate(late):
        rs_bufs[o] = shared[k]
        if o < nb:
            grads[BIG[o]] = shared[k].reshape(w[BIG[o]].shape)
            _adamw_big(BIG[o], w, grads, mom_m, mom_v, delta, new_m, new_v)
    small_all = _chip_allgather("gather_small_grads", [rs_bufs[nb]])[0]
    small_all = jnp.transpose(small_all, (1, 0, 2, 3)).reshape(-1)[:n_small]
    off = 0
    p = 2 * lax.axis_index("x") + lax.axis_index("y")
    for n in SMALL:
        full_shape = (w[n].shape[0], E) if n in SMALL_SHARDED else w[n].shape
        size = math.prod(full_shape)
        gfull = small_all[off:off + size].reshape(full_shape)
        off += size
        if n in SMALL_SHARDED:
            gfull = lax.dynamic_slice_in_dim(gfull, p * (E // N_CHIPS), E // N_CHIPS, axis=1)
        grads[n] = gfull

    for n in SMALL:
        shape = w[n].shape
        if n in GROUP_AXIS_1:
            perm = (0,) + tuple(range(2, len(shape))) + (1,)
            inv = (0, len(shape) - 1) + tuple(range(1, len(shape) - 1))
            view = lambda a: jnp.transpose(a, perm).reshape(-1, shape[1])
            back = lambda a: jnp.transpose(a.reshape(tuple(shape[k] for k in perm)), inv)
        else:
            view = lambda a: a.reshape(-1, shape[-1])
            back = lambda a: a.reshape(shape)
        d_, m_, v_ = _adamw(f"adamw_{n}", view(w[n]), view(grads[n]), view(mom_m[n]), view(mom_v[n]))
        delta[n], new_m[n], new_v[n] = back(d_), back(m_), back(v_)
    return (loss, grad_x, *[grads[n] for n in ORDER], *[delta[n] for n in ORDER], *[new_m[n] for n in ORDER], *[new_v[n] for n in ORDER])
```

```python
import functools
import math

import jax
import jax.numpy as jnp
from jax import lax
from jax.experimental import pallas as pl
from jax.experimental.pallas import tpu as pltpu

F32 = jnp.float32
BF16 = jnp.bfloat16
MESH = pl.DeviceIdType.MESH

N_CHIPS = 4
VMEM_LIMIT = 56 * 1024 * 1024
LANES = 128
SUB = 8

EPS = 1e-6
S5_GROUP = 16
S5_STATE = 64
GROUPS_PER_CHUNK = 16
FOX_HEAD_DIM = 128
ATTN_SUB = 256
ATTN_HEADS = 2
POOL_WINDOWS = (2, 4, 8, 16)
POOL_HALO = 16
ADAM_LR, ADAM_B1, ADAM_B2, ADAM_EPS, ADAM_WD, ADAM_STEP = 0.001, 0.9, 0.999, 1e-08, 0.01, 10
NEG = -1e30
K_STEP = 2048


ANY = pl.BlockSpec(memory_space=pl.ANY)


def _t(pref, dim):
    if dim <= pref:
        return dim
    t = pref - pref % 16
    while t > 16 and dim % t:
        t -= 16
    assert dim % t == 0, (pref, dim)
    return t


def _params(sem):
    return pltpu.CompilerParams(dimension_semantics=sem, vmem_limit_bytes=VMEM_LIMIT)


def _sigmoid(x):
    return 1.0 / (1.0 + jnp.exp(-x))


def _silu(z):
    return z * _sigmoid(z)


def _dsilu(z):
    s = _sigmoid(z)
    return s * (1.0 + z * (1.0 - s))


_GELU_C = math.sqrt(2.0 / math.pi)


def _gelu(x):
    return 0.5 * x * (1.0 + jnp.tanh(_GELU_C * (x + 0.044715 * (x * x * x))))


def _dgelu(x):
    t = jnp.tanh(_GELU_C * (x + 0.044715 * (x * x * x)))
    return 0.5 * (1.0 + t) + 0.5 * x * (1.0 - t * t) * (_GELU_C * (1.0 + 3.0 * 0.044715 * x * x))


def _log_sigmoid(x):
    return jnp.minimum(x, 0.0) - jnp.log(1.0 + jnp.exp(-jnp.abs(x)))


def _rms(x):
    return lax.rsqrt(jnp.mean(x * x, axis=-1, keepdims=True) + EPS)


def _rms_bwd(x, w, dy):
    r = _rms(x)
    xhat = x * r
    dxh = dy * w
    dx = r * (dxh - xhat * jnp.mean(dxh * xhat, axis=-1, keepdims=True))
    return dx, dy * xhat


def _rows(name, fn, ins, outs, tr, pre=None, into=None, deps=()):
    rows = None
    for arr, kind, cols, cb in ins:
        if kind == 'r':
            rows = arr.shape[0]
        elif kind == 's' and rows is None:
            rows = arr.shape[1]
    tr = _t(tr, rows)
    n_in = len(ins)
    has_acc = any(o[0] == 'a' for o in outs)

    def spec(kind, cols, cb):
        if kind == 'r':
            return pl.BlockSpec((tr, cols), lambda r, *p: (r, cb))
        if kind == 'b':
            return pl.BlockSpec((1, cols), lambda r, *p: (0, cb))
        return pl.BlockSpec((None, tr, cols), lambda r, p: (p[cb], r, 0))

    in_specs = [spec(kind, cols, cb) for _, kind, cols, cb in ins]
    out_specs, out_shape = [], []
    for o in outs:
        if o[0] == 'r':
            out_specs.append(pl.BlockSpec((tr, o[1]), lambda r, *p: (r, 0)))
            out_shape.append(jax.ShapeDtypeStruct((rows, o[1]), o[2]))
        elif o[0] == 'a':
            out_specs.append(pl.BlockSpec((1, o[1]), lambda r, *p: (0, 0)))
            out_shape.append(jax.ShapeDtypeStruct((1, o[1]), o[2]))
        else:
            blk = tuple(tr if d == 'tr' else d for d in o[3])
            out_specs.append(pl.BlockSpec(blk, o[4]))
            out_shape.append(jax.ShapeDtypeStruct(o[1], o[2]))
    n_pre = 0 if pre is None else 1
    args = [a[0] for a in ins]
    aliases = {}
    if into is not None:
        in_specs.append(ANY)
        args.append(into)
        aliases = {n_pre + n_in: 0}
    in_specs += [ANY] * len(deps)
    args += list(deps)
    n_all = len(args)

    def body(*refs):
        refs = refs[n_pre:]
        res = fn(*[r[...] for r in refs[:n_in]])
        for spec_o, o, v in zip(outs, refs[n_all:], res):
            if spec_o[0] == 'a':
                @pl.when(pl.program_id(0) == 0)
                def _():
                    o[...] = jnp.zeros_like(o)
                o[...] += v.astype(o.dtype)
            else:
                o[...] = v.astype(o.dtype)

    grid_spec = pltpu.PrefetchScalarGridSpec(num_scalar_prefetch=n_pre, grid=(rows // tr,), in_specs=in_specs, out_specs=out_specs)
    if pre is not None:
        args = [pre] + args
    return pl.pallas_call(body, name=name, grid_spec=grid_spec, out_shape=out_shape, input_output_aliases=aliases,
                          compiler_params=_params(("arbitrary" if has_acc else "parallel",)))(*args)


def _colsum(v):
    return jnp.sum(v, axis=0, keepdims=True)


def _mm(name, a, b, *, M, N, K, tm, tn, tk, a_spec, b_spec, outs, epi=None, extras=(), groups=1, ta=False, tb=False, deps=()):
    nk = K // tk
    assert M % tm == 0 and N % tn == 0 and K % tk == 0, (name, M, N, K, tm, tn, tk)
    dims = (((0 if ta else 1,), (1 if tb else 0,)), ((), ()))
    n_ex = len(extras)

    def body(*refs):
        a_ref, b_ref = refs[0], refs[1]
        ex = refs[2:2 + n_ex]
        out_refs = refs[2 + n_ex + len(deps):2 + n_ex + len(deps) + len(outs)]

        def finish(r):
            res = (r,) if epi is None else epi(r, *[e[...] for e in ex])
            for o, v in zip(out_refs, res):
                o[...] = v.astype(o.dtype)

        part = lax.dot_general(a_ref[...].astype(BF16), b_ref[...].astype(BF16), dims, preferred_element_type=F32)
        if nk == 1:
            finish(part)
            return
        acc = refs[-1]
        k = pl.program_id(3)

        @pl.when(k == 0)
        def _():
            acc[...] = part

        @pl.when(k > 0)
        def _():
            acc[...] += part

        @pl.when(k == nk - 1)
        def _():
            finish(acc[...])

    return pl.pallas_call(
        body, name=name, grid=(groups, M // tm, N // tn, nk),
        in_specs=[a_spec, b_spec] + [s for _, s in extras] + [ANY] * len(deps),
        out_specs=[s for _, _, s in outs],
        out_shape=[jax.ShapeDtypeStruct(sh, dt) for sh, dt, _ in outs],
        scratch_shapes=[] if nk == 1 else [pltpu.VMEM((tm, tn), F32)],
        compiler_params=_params(("parallel", "parallel", "parallel", "arbitrary")),
    )(a, b, *[e for e, _ in extras], *deps)


def _bs(shape, f):
    return pl.BlockSpec(shape, f)


def _tile(tm, tn, coff=0):
    return _bs((tm, tn), lambda g, m, n, k: (m, n + coff))


def _rowvec(tn, coff=0):
    return _bs((1, tn), lambda g, m, n, k: (0, n + coff))


def _mm_proj(name, xn, w, *, epi=None, extras=(), out_dtype=F32):
    T, D = xn.shape
    sw = w.shape[2]
    N = N_CHIPS * sw
    tm, tn, tk = _t(512, T), _t(1024, sw), _t(K_STEP, D)
    nb = sw // tn
    return _mm(name, xn, w, M=T, N=N, K=D, tm=tm, tn=tn, tk=tk,
               a_spec=_bs((tm, tk), lambda g, m, n, k: (m, k)),
               b_spec=_bs((None, tk, tn), lambda g, m, n, k: (n // nb, k, n % nb)),
               outs=[((T, N), out_dtype, _tile(tm, tn))], epi=epi, extras=extras)[0]


def _mm_plain(name, a, b, *, out_dtype=F32, epi=None, extras=(), outs=None, tn_pref=1024):
    M, K = a.shape
    N = b.shape[1]
    tm, tn, tk = _t(512, M), _t(tn_pref, N), _t(K_STEP, K)
    if outs is None:
        outs = [((M, N), out_dtype, _tile(tm, tn))]
    return _mm(name, a, b, M=M, N=N, K=K, tm=tm, tn=tn, tk=tk,
               a_spec=_bs((tm, tk), lambda g, m, n, k: (m, k)),
               b_spec=_bs((tk, tn), lambda g, m, n, k: (k, n)),
               outs=outs, epi=epi, extras=extras)


def _mm_rowsharded(name, a, w, *, epi, extras, outs_fn, deps=()):
    T, E = a.shape
    N = w.shape[2]
    tm, tn, tk = _t(512, T), _t(1024, N), _t(K_STEP, E)
    return _mm(name, a, w.reshape(E, N), M=T, N=N, K=E, tm=tm, tn=tn, tk=tk, deps=deps,
               a_spec=_bs((tm, tk), lambda g, m, n, k: (m, k)),
               b_spec=_bs((tk, tn), lambda g, m, n, k: (k, n)),
               outs=outs_fn(tm, tn), epi=epi, extras=extras(tm, tn))


def _mm_rowsharded_t(name, d, w, *, epi, extras, outs_fn, deps=()):
    T, N = d.shape
    tn = w.shape[1]
    E = N_CHIPS * tn
    tm, tk = _t(512, T), _t(K_STEP, N)
    return _mm(name, d, w, M=T, N=E, K=N, tm=tm, tn=tn, tk=tk, tb=True, deps=deps,
               a_spec=_bs((tm, tk), lambda g, m, n, k: (m, k)),
               b_spec=_bs((None, tn, tk), lambda g, m, n, k: (n, 0, k)),
               outs=outs_fn(tm, tn), epi=epi, extras=extras(tm, tn))


def _mm_colsharded_t(name, d, w):
    T, N = d.shape
    D, sw = w.shape[1], w.shape[2]
    tm, tn, tk = _t(512, T), _t(1024, D), _t(1024, sw)
    kb = sw // tk
    return _mm(name, d, w, M=T, N=D, K=N, tm=tm, tn=tn, tk=tk, tb=True,
               a_spec=_bs((tm, tk), lambda g, m, n, k: (m, k)),
               b_spec=_bs((None, tn, tk), lambda g, m, n, k: (k // kb, n, k % kb)),
               outs=[((T, D), F32, _tile(tm, tn))])[0]


def _mm_dw_rows(name, a, d, deps=()):
    T, E = a.shape
    N = d.shape[1]
    tm, tn, tk = E // (2 * N_CHIPS), _t(2048, N), _t(K_STEP, T)
    return _mm(name, a, d, M=E, N=N, K=T, tm=tm, tn=tn, tk=tk, ta=True, deps=deps,
               a_spec=_bs((tk, tm), lambda g, m, n, k: (k, m)),
               b_spec=_bs((tk, tn), lambda g, m, n, k: (k, n)),
               outs=[((2, N_CHIPS, tm, N), BF16, _bs((None, None, tm, tn), lambda g, m, n, k: (m % 2, m // 2, 0, n)))])[0]


def _mm_dw_cols(name, xn, d):
    T, D = xn.shape
    N = d.shape[1]
    sw = N // N_CHIPS
    tm, tn, tk = _t(512, D // 2), _t(1024, sw), _t(K_STEP, T)
    mh, nb = (D // 2) // tm, sw // tn
    return _mm(name, xn, d, M=D, N=N, K=T, tm=tm, tn=tn, tk=tk, ta=True,
               a_spec=_bs((tk, tm), lambda g, m, n, k: (k, m)),
               b_spec=_bs((tk, tn), lambda g, m, n, k: (k, n)),
               outs=[((2, N_CHIPS, D // 2, sw), BF16,
                      _bs((None, None, tm, tn), lambda g, m, n, k: (m // mh, n // nb, m % mh, n % nb)))])[0]


def _norm_fwd(name, h, w, deps=()):
    D = h.shape[1]
    return _rows(name, lambda x, g: ((x * _rms(x)) * g,), [(h, 'r', D, 0), (w, 'b', D, 0)], [('r', D, BF16)], 256, deps=deps)[0]


def _norm_bwd(name, dxn, h, w, dh):
    D = h.shape[1]

    def fn(dy, x, g, up):
        dx, dwt = _rms_bwd(x, g, dy)
        r = up + dx
        return r, r, _colsum(dwt)

    return _rows(name, fn, [(dxn, 'r', D, 0), (h, 'r', D, 0), (w, 'b', D, 0), (dh, 'r', D, 0)],
                 [('r', D, F32), ('r', D, BF16), ('a', D, F32)], 256)


def _loss(h, target):
    D = h.shape[1]

    def fn(y, t):
        e = y - t
        d = e * (1.0 / D)
        return d, d, _colsum(e * e) * (0.5 / D)

    return _rows("loss", fn, [(h, 'r', D, 0), (target, 'r', D, 0)], [('r', D, F32), ('r', D, BF16), ('a', D, F32)], 256)


def _adamw(name, w, g, m, v):
    cols = w.shape[1]

    def fn(w, g, m, v):
        m = ADAM_B1 * m + (1.0 - ADAM_B1) * g
        v = ADAM_B2 * v + (1.0 - ADAM_B2) * (g * g)
        m_hat = m / (1.0 - ADAM_B1 ** ADAM_STEP)
        v_hat = v / (1.0 - ADAM_B2 ** ADAM_STEP)
        delta = -ADAM_LR * (m_hat / (jnp.sqrt(v_hat) + ADAM_EPS) + ADAM_WD * w)
        return delta, m, v

    rows = w.shape[0]
    if rows % SUB == 0 or rows <= 256:
        return _rows(name, fn, [(x, 'r', cols, 0) for x in (w, g, m, v)], [('r', cols, F32)] * 3, 256)
    tc = _t(256, cols)
    assert tc % LANES == 0, (rows, cols)

    def body(w_ref, g_ref, m_ref, v_ref, d_out, m_out, v_out):
        for o, r in zip((d_out, m_out, v_out), fn(w_ref[...], g_ref[...], m_ref[...], v_ref[...])):
            o[...] = r

    blk = pl.BlockSpec((rows, tc), lambda j: (0, j))
    return pl.pallas_call(body, name=name, grid=(cols // tc,), in_specs=[blk] * 4, out_specs=[blk] * 3,
                          out_shape=[jax.ShapeDtypeStruct((rows, cols), F32)] * 3, compiler_params=_params(("parallel",)))(w, g, m, v)


def _s5_disc(a_re, a_im, log_dt):
    dt = jnp.exp(log_dt)
    mag = jnp.exp(a_re * dt)
    abar_r = mag * jnp.cos(a_im * dt)
    abar_i = mag * jnp.sin(a_im * dt)
    den = a_re * a_re + a_im * a_im
    xr = abar_r - 1.0
    fr = (xr * a_re + abar_i * a_im) / den
    fi = (abar_i * a_re - xr * a_im) / den
    return abar_r, abar_i, fr, fi


def _s5_disc_fwd(name, a_re, a_im, log_dt):
    G, P = a_re.shape

    def body(ar, ai, ld, o0, o1, o2, o3):
        for o, v in zip((o0, o1, o2, o3), _s5_disc(ar[...], ai[...], ld[...])):
            o[...] = v

    return pl.pallas_call(body, name=name, out_shape=[jax.ShapeDtypeStruct((G, P), F32)] * 4)(a_re, a_im, log_dt)


def _s5_disc_bwd(name, a_re, a_im, log_dt, cts):
    G, P = a_re.shape

    def body(ar, ai, ld, c0, c1, c2, c3, d0, d1, d2):
        _, vjp = jax.vjp(_s5_disc, ar[...], ai[...], ld[...])
        g0, g1, g2 = vjp((c0[...], c1[...], c2[...], c3[...]))
        d0[...] = g0
        d1[...] = g1
        d2[...] = g2

    return pl.pallas_call(body, name=name, out_shape=[jax.ShapeDtypeStruct((G, P), F32)] * 2 + [jax.ShapeDtypeStruct((G, 1), F32)])(
        a_re, a_im, log_dt, *cts)


def _s5_bbar(name, fr, fi, br, bi):
    return _rows(name, lambda fr, fi, br, bi: (fr * br - fi * bi, fr * bi + fi * br),
                 [(fr, 'r', 1, 0), (fi, 'r', 1, 0), (br, 'r', S5_GROUP, 0), (bi, 'r', S5_GROUP, 0)],
                 [('r', S5_GROUP, F32)] * 2, 2048)


def _s5_bbar_bwd(name, fr, fi, br, bi, dr, di):
    def fn(fr, fi, br, bi, dr, di):
        return (fr * dr + fi * di, fr * di - fi * dr,
                jnp.sum(br * dr + bi * di, axis=1, keepdims=True), jnp.sum(br * di - bi * dr, axis=1, keepdims=True))

    return _rows(name, fn, [(fr, 'r', 1, 0), (fi, 'r', 1, 0)] + [(x, 'r', S5_GROUP, 0) for x in (br, bi, dr, di)],
                 [('r', S5_GROUP, F32)] * 2 + [('r', 1, F32)] * 2, 2048)


def _scan_mults(m_ref, ar, ai, reverse):
    L = ar.shape[1]
    row = lax.broadcasted_iota(jnp.int32, (SUB, L), 0)
    if reverse:
        row = (SUB - 1) - row
    ar = jnp.broadcast_to(ar, (SUB, L))
    ai = jnp.broadcast_to(ai, (SUB, L))
    a2r, a2i = ar * ar - ai * ai, 2.0 * ar * ai
    a4r, a4i = a2r * a2r - a2i * a2i, 2.0 * a2r * a2i
    zero = jnp.zeros((SUB, L), F32)
    for s, (pr, pi, d) in enumerate(((ar, ai, 1), (a2r, a2i, 2), (a4r, a4i, 4))):
        m_ref[2 * s] = jnp.where(row >= d, pr, zero)
        m_ref[2 * s + 1] = jnp.where(row >= d, pi, zero)
    pr, pi = ar, ai
    for bit, (qr, qi) in ((1, (ar, ai)), (2, (a2r, a2i)), (4, (a4r, a4i))):
        on = (row & bit) != 0
        nr, ni = pr * qr - pi * qi, pr * qi + pi * qr
        pr, pi = jnp.where(on, nr, pr), jnp.where(on, ni, pi)
    m_ref[6] = pr
    m_ref[7] = pi


def _scan8(xr, xi, m_ref, cr, ci, reverse):
    for s, d in enumerate((1, 2, 4)):
        sh = (SUB - d) if reverse else d
        sr, si = pltpu.roll(xr, sh, 0), pltpu.roll(xi, sh, 0)
        mr, mi = m_ref[2 * s], m_ref[2 * s + 1]
        xr, xi = xr + mr * sr - mi * si, xi + mr * si + mi * sr
    pr, pi = m_ref[6], m_ref[7]
    return xr + pr * cr - pi * ci, xi + pr * ci + pi * cr


def _blockdiag_fill(bd_ref, c_ref, C, L):
    P = S5_STATE
    bd_ref[...] = jnp.zeros_like(bd_ref)
    for g in range(L // P):
        for half in (0, L):
            bd_ref[g * C:(g + 1) * C, half + g * P:half + (g + 1) * P] = c_ref[:, half + g * P:half + (g + 1) * P]


def _blockdiag_take(out_ref, dense_ref, C, L):
    P = S5_STATE
    for g in range(L // P):
        for half in (0, L):
            out_ref[:, half + g * P:half + (g + 1) * P] = dense_ref[g * C:(g + 1) * C, half + g * P:half + (g + 1) * P]


def _s5_fwd(name, proj, bbd, cbd, abar_r, abar_i, dskip, E):
    T = proj.shape[0]
    NC, C, L2 = bbd.shape
    L = L2 // 2
    CH = GROUPS_PER_CHUNK * C
    tT = _t(256, T)
    nt = (((1,), (1,)), ((), ()))

    def body(u_ref, bc_ref, cc_ref, ar_ref, ai_ref, d_ref, y_ref, g_ref, h_ref, bu, carry, mult, b_bd, c_bd):
        tb = pl.program_id(1)

        @pl.when(tb == 0)
        def _():
            carry[...] = jnp.zeros_like(carry)
            _blockdiag_fill(b_bd, bc_ref, C, L)
            _blockdiag_fill(c_bd, cc_ref, C, L)

        u = u_ref[...]
        bu[...] = jnp.dot(u.astype(BF16), b_bd[...], preferred_element_type=F32)
        _scan_mults(mult, ar_ref[...], ai_ref[...], False)

        def step(jb, c):
            cr, ci = c
            r0 = pl.multiple_of(jb * SUB, SUB)
            hr, hi = _scan8(bu[pl.ds(r0, SUB), 0:L], bu[pl.ds(r0, SUB), L:L2], mult, cr, ci, False)
            h_ref[pl.ds(r0, SUB), 0:L] = hr
            h_ref[pl.ds(r0, SUB), L:L2] = hi
            return (jnp.broadcast_to(hr[SUB - 1:SUB, :], (SUB, L)), jnp.broadcast_to(hi[SUB - 1:SUB, :], (SUB, L)))

        cr, ci = lax.fori_loop(0, tT // SUB, step, (carry[:, 0:L], carry[:, L:L2]))
        carry[:, 0:L] = cr
        carry[:, L:L2] = ci
        y1 = lax.dot_general(h_ref[...].astype(BF16), c_bd[...], nt, preferred_element_type=F32) + d_ref[...] * u
        y_ref[...] = y1
        g_ref[...] = _gelu(y1).astype(BF16)

    return pl.pallas_call(
        body, name=name, grid=(NC, T // tT),
        in_specs=[_bs((tT, CH), lambda c, t: (t, c)), _bs((None, C, L2), lambda c, t: (c, 0, 0)),
                  _bs((None, C, L2), lambda c, t: (c, 0, 0)), _bs((None, 1, L), lambda c, t: (c, 0, 0)),
                  _bs((None, 1, L), lambda c, t: (c, 0, 0)), _bs((1, CH), lambda c, t: (0, c))],
        out_specs=[_bs((tT, CH), lambda c, t: (t, c)), _bs((tT, CH), lambda c, t: (t, c)),
                   _bs((None, tT, L2), lambda c, t: (c, t, 0))],
        out_shape=[jax.ShapeDtypeStruct((T, E), F32), jax.ShapeDtypeStruct((T, E), BF16),
                   jax.ShapeDtypeStruct((NC, T, L2), F32)],
        scratch_shapes=[pltpu.VMEM((tT, L2), F32), pltpu.VMEM((SUB, L2), F32), pltpu.VMEM((8, SUB, L), F32),
                        pltpu.VMEM((CH, L2), BF16), pltpu.VMEM((CH, L2), BF16)],
        compiler_params=_params(("parallel", "arbitrary")),
    )(proj, bbd, cbd, abar_r, abar_i, dskip)


def _s5_bwd(name, dy1, proj, hs, bbd, cbd, abar_r, abar_i, dskip, E):
    T = proj.shape[0]
    NC, C, L2 = bbd.shape
    L = L2 // 2
    CH = GROUPS_PER_CHUNK * C
    tT = _t(256, T)
    nT = T // tT
    tn = (((0,), (0,)), ((), ()))
    nt = (((1,), (1,)), ((), ()))

    def body(dy_ref, u_ref, h_ref, bc_ref, cc_ref, ar_ref, ai_ref, d_ref, du_ref, db_ref, dc_ref, da_ref, dd_ref,
             gb, carry, mult, b_bd, c_bd, db_acc, dc_acc):
        tb = pl.program_id(1)

        @pl.when(tb == 0)
        def _():
            carry[...] = jnp.zeros_like(carry)
            db_acc[...] = jnp.zeros_like(db_acc)
            dc_acc[...] = jnp.zeros_like(dc_acc)
            da_ref[...] = jnp.zeros_like(da_ref)
            dd_ref[...] = jnp.zeros_like(dd_ref)
            _blockdiag_fill(b_bd, bc_ref, C, L)
            _blockdiag_fill(c_bd, cc_ref, C, L)

        dy = dy_ref[...]
        u = u_ref[...]
        dy16 = dy.astype(BF16)
        dc_acc[...] += lax.dot_general(dy16, h_ref[...].astype(BF16), tn, preferred_element_type=F32)
        gb[...] = jnp.dot(dy16, c_bd[...], preferred_element_type=F32)
        _scan_mults(mult, ar_ref[...], -ai_ref[...], True)
        row = lax.broadcasted_iota(jnp.int32, (SUB, L), 0)
        nblk = tT // SUB

        def step(jj, c):
            cr, ci, sr, si = c
            r0 = pl.multiple_of((nblk - 1 - jj) * SUB, SUB)
            gr, gi = _scan8(gb[pl.ds(r0, SUB), 0:L], gb[pl.ds(r0, SUB), L:L2], mult, cr, ci, True)
            gb[pl.ds(r0, SUB), 0:L] = gr
            gb[pl.ds(r0, SUB), L:L2] = gi
            nr = jnp.where(row == SUB - 1, cr, pltpu.roll(gr, SUB - 1, 0))
            ni = jnp.where(row == SUB - 1, ci, pltpu.roll(gi, SUB - 1, 0))
            hr, hi = h_ref[pl.ds(r0, SUB), 0:L], h_ref[pl.ds(r0, SUB), L:L2]
            sr = sr + nr * hr + ni * hi
            si = si + ni * hr - nr * hi
            return (jnp.broadcast_to(gr[0:1, :], (SUB, L)), jnp.broadcast_to(gi[0:1, :], (SUB, L)), sr, si)

        z = jnp.zeros((SUB, L), F32)
        cr, ci, sr, si = lax.fori_loop(0, nblk, step, (carry[:, 0:L], carry[:, L:L2], z, z))
        carry[:, 0:L] = cr
        carry[:, L:L2] = ci
        da_ref[:, 0:L] += sr
        da_ref[:, L:L2] += si
        g16 = gb[...].astype(BF16)
        du = lax.dot_general(g16, b_bd[...], nt, preferred_element_type=F32) + d_ref[...] * dy
        du_ref[...] = du.astype(BF16)
        db_acc[...] += lax.dot_general(u.astype(BF16), g16, tn, preferred_element_type=F32)
        dd_ref[...] += _colsum(dy * u)

        @pl.when(tb == nT - 1)
        def _():
            _blockdiag_take(db_ref, db_acc, C, L)
            _blockdiag_take(dc_ref, dc_acc, C, L)

    rev = lambda c, t: (nT - 1 - t, c)
    return pl.pallas_call(
        body, name=name, grid=(NC, nT),
        in_specs=[_bs((tT, CH), rev), _bs((tT, CH), rev), _bs((None, tT, L2), lambda c, t: (c, nT - 1 - t, 0)),
                  _bs((None, C, L2), lambda c, t: (c, 0, 0)), _bs((None, C, L2), lambda c, t: (c, 0, 0)),
                  _bs((None, 1, L), lambda c, t: (c, 0, 0)), _bs((None, 1, L), lambda c, t: (c, 0, 0)),
                  _bs((1, CH), lambda c, t: (0, c))],
        out_specs=[_bs((tT, CH), rev), _bs((None, C, L2), lambda c, t: (c, 0, 0)), _bs((None, C, L2), lambda c, t: (c, 0, 0)),
                   _bs((None, SUB, L2), lambda c, t: (c, 0, 0)), _bs((None, 1, CH), lambda c, t: (c, 0, 0))],
        out_shape=[jax.ShapeDtypeStruct((T, E), BF16), jax.ShapeDtypeStruct((NC, C, L2), F32),
                   jax.ShapeDtypeStruct((NC, C, L2), F32), jax.ShapeDtypeStruct((NC, SUB, L2), F32),
                   jax.ShapeDtypeStruct((NC, 1, CH), F32)],
        scratch_shapes=[pltpu.VMEM((tT, L2), F32), pltpu.VMEM((SUB, L2), F32), pltpu.VMEM((8, SUB, L), F32),
                        pltpu.VMEM((CH, L2), BF16), pltpu.VMEM((CH, L2), BF16), pltpu.VMEM((CH, L2), F32), pltpu.VMEM((CH, L2), F32)],
        compiler_params=_params(("parallel", "arbitrary")),
    )(dy1, proj, hs, bbd, cbd, abar_r, abar_i, dskip)


def _compact(v, NC):
    G, P, C = v.shape
    return jnp.transpose(v.reshape(NC, G // NC, P, C), (0, 3, 1, 2)).reshape(NC, C, (G // NC) * P)


def _uncompact(d, G):
    NC, C, L = d.shape
    gpc = G // NC
    return jnp.transpose(d.reshape(NC, C, gpc, L // gpc), (0, 2, 3, 1)).reshape(G, L // gpc, C)


def _cum_rows(name, x, bias, reverse, log_sig):
    T, L = x.shape

    def body(x_ref, b_ref, o_ref):
        row = lax.broadcasted_iota(jnp.int32, (SUB, L), 0)
        if reverse:
            row = (SUB - 1) - row
        nblk = T // SUB

        def step(jj, c):
            r0 = pl.multiple_of(((nblk - 1 - jj) if reverse else jj) * SUB, SUB)
            v = x_ref[pl.ds(r0, SUB), :] + b_ref[...]
            if log_sig:
                v = _log_sigmoid(v)
            for d in (1, 2, 4):
                v = v + jnp.where(row >= d, pltpu.roll(v, (SUB - d) if reverse else d, 0), 0.0)
            v = v + c
            o_ref[pl.ds(r0, SUB), :] = v
            e = 0 if reverse else SUB - 1
            return jnp.broadcast_to(v[e:e + 1, :], (SUB, L))

        lax.fori_loop(0, nblk, step, jnp.zeros((SUB, L), F32))

    return pl.pallas_call(body, name=name, out_shape=jax.ShapeDtypeStruct((T, L), F32),
                          compiler_params=pltpu.CompilerParams(vmem_limit_bytes=VMEM_LIMIT))(x, bias)


def _qk_norm(name, proj, wq, wk, H):
    T = proj.shape[0]
    Dh = FOX_HEAD_DIM
    tT = _t(512, T)

    def body(q_ref, k_ref, wq_ref, wk_ref, qn_ref, kn_ref):
        q, k = q_ref[...], k_ref[...]
        qn_ref[...] = ((q * _rms(q)) * wq_ref[...]).astype(BF16)
        kn_ref[...] = ((k * _rms(k)) * wk_ref[...]).astype(BF16)

    blk = lambda off: _bs((tT, Dh), lambda t, h: (t, h + off))
    return pl.pallas_call(
        body, name=name, grid=(T // tT, H),
        in_specs=[blk(0), blk(H), _bs((1, Dh), lambda t, h: (0, 0)), _bs((1, Dh), lambda t, h: (0, 0))],
        out_specs=[blk(0), blk(0)], out_shape=[jax.ShapeDtypeStruct((T, H * Dh), BF16)] * 2,
        compiler_params=_params(("parallel", "parallel")))(proj, proj, wq, wk)


def _qk_norm_bwd(name, proj, wq, wk, dqn, dkn, H):
    T = proj.shape[0]
    Dh = FOX_HEAD_DIM
    tT = _t(512, T)

    def body(q_ref, k_ref, wq_ref, wk_ref, dqn_ref, dkn_ref, dq_ref, dk_ref, dwq_ref, dwk_ref):
        @pl.when((pl.program_id(0) == 0) & (pl.program_id(1) == 0))
        def _():
            dwq_ref[...] = jnp.zeros_like(dwq_ref)
            dwk_ref[...] = jnp.zeros_like(dwk_ref)

        dq, tq = _rms_bwd(q_ref[...], wq_ref[...], dqn_ref[...])
        dk, tk = _rms_bwd(k_ref[...], wk_ref[...], dkn_ref[...])
        dq_ref[...] = dq.astype(BF16)
        dk_ref[...] = dk.astype(BF16)
        dwq_ref[...] += _colsum(tq)
        dwk_ref[...] += _colsum(tk)

    blk = lambda off: _bs((tT, Dh), lambda t, h: (t, h + off))
    one = _bs((1, Dh), lambda t, h: (0, 0))
    return pl.pallas_call(
        body, name=name, grid=(T // tT, H),
        in_specs=[blk(0), blk(H), one, one, blk(0), blk(0)],
        out_specs=[blk(0), blk(0), one, one],
        out_shape=[jax.ShapeDtypeStruct((T, H * Dh), BF16)] * 2 + [jax.ShapeDtypeStruct((1, Dh), F32)] * 2,
        compiler_params=_params(("arbitrary", "arbitrary")))(proj, proj, wq, wk, dqn, dkn)


def _attn_fwd(name, qn, kn, proj, cum_q, cum_k, H):
    T = qn.shape[0]
    Dh = FOX_HEAD_DIM
    tq = cum_k.shape[3]
    nq = T // tq
    scale = Dh ** -0.5
    nt = (((1,), (1,)), ((), ()))

    sq = _t(ATTN_SUB, tq)
    rep = tq // LANES
    HP = ATTN_HEADS
    assert H % HP == 0 and Dh == LANES

    def body(q_ref, k_ref, v_ref, cq_ref, ck_ref, o_ref, lse_ref, m_sc, l_sc, acc_sc):
        i = pl.program_id(1)
        m_sc[...] = jnp.full_like(m_sc, NEG)
        l_sc[...] = jnp.zeros_like(l_sc)
        acc_sc[...] = jnp.zeros_like(acc_sc)
        kloc = lax.broadcasted_iota(jnp.int32, (sq, tq), 1)
        qloc = lax.broadcasted_iota(jnp.int32, (sq, tq), 0)

        def chunk(kc, masked):
            ks = pl.multiple_of(kc * tq, tq)
            for hh in range(HP):
                lanes = slice(hh * Dh, (hh + 1) * Dh)
                k = k_ref[pl.ds(ks, tq), lanes]
                v16 = v_ref[pl.ds(ks, tq), lanes].astype(BF16)
                ck = ck_ref[hh, kc]
                for r in range(tq // sq):
                    rows = pl.ds(r * sq, sq)
                    s = lax.dot_general(q_ref[rows, lanes], k, nt, preferred_element_type=F32) * scale + (jnp.tile(cq_ref[hh, rows, :], (1, rep)) - ck)
                    if masked:
                        s = jnp.where(kloc <= qloc + r * sq, s, NEG)
                    m_old = m_sc[rows, lanes]
                    m_new = jnp.maximum(m_old, jnp.max(s, axis=1, keepdims=True))
                    alpha = jnp.exp(m_old - m_new)
                    p = jnp.exp(s - jnp.tile(m_new, (1, rep)))
                    l_sc[rows, lanes] = alpha * l_sc[rows, lanes] + jnp.sum(p, axis=1, keepdims=True)
                    acc_sc[rows, lanes] = alpha * acc_sc[rows, lanes] + jnp.dot(p.astype(BF16), v16, preferred_element_type=F32)
                    m_sc[rows, lanes] = m_new

        def below(kc, c):
            chunk(kc, False)
            return c

        lax.fori_loop(0, i, below, 0)
        chunk(i, True)
        o_ref[...] = acc_sc[...] / l_sc[...]
        for hh in range(HP):
            lanes = slice(hh * Dh, (hh + 1) * Dh)
            lse_ref[hh] = m_sc[:, lanes] + jnp.log(l_sc[:, lanes])

    W2 = HP * Dh
    return pl.pallas_call(
        body, name=name, grid=(H // HP, nq),
        in_specs=[_bs((tq, W2), lambda h, i: (i, h)), _bs((T, W2), lambda h, i: (0, h)), _bs((T, W2), lambda h, i: (0, 2 * (H // HP) + h)),
                  _bs((HP, tq, LANES), lambda h, i: (h, i, 0)), _bs((HP, nq, 1, tq), lambda h, i: (h, 0, 0, 0))],
        out_specs=[_bs((tq, W2), lambda h, i: (i, h)), _bs((HP, tq, LANES), lambda h, i: (h, i, 0))],
        out_shape=[jax.ShapeDtypeStruct((T, H * Dh), F32), jax.ShapeDtypeStruct((H, T, LANES), F32)],
        scratch_shapes=[pltpu.VMEM((tq, W2), F32), pltpu.VMEM((tq, W2), F32), pltpu.VMEM((tq, W2), F32)],
        compiler_params=_params(("parallel", "parallel")))(qn, kn, proj, cum_q, cum_k)


def _attn_bwd(name, qn, kn, proj, do, o, lse, cum_q, cum_k, H):
    T = qn.shape[0]
    Dh = FOX_HEAD_DIM
    tq = cum_k.shape[3]
    nq = T // tq
    scale = Dh ** -0.5
    nt = (((1,), (1,)), ((), ()))
    tn = (((0,), (0,)), ((), ()))
    assert H <= LANES

    sq = _t(ATTN_SUB, tq)
    rep = tq // LANES
    HP = ATTN_HEADS
    W2 = HP * Dh
    assert H % HP == 0 and Dh == LANES

    def body(q_ref, k_ref, v_ref, do_ref, o_ref, lse_ref, cq_ref, ck_ref, dq_ref, dk_ref, dv_ref, dcq_ref, dck_ref,
             delta, cql, dk_sc, dv_sc, dck_sc):
        h, j = pl.program_id(0), pl.program_id(1)

        @pl.when((h == 0) & (j == 0))
        def _():
            dcq_ref[...] = jnp.zeros_like(dcq_ref)

        @pl.when(j == 0)
        def _():
            dq_ref[...] = jnp.zeros_like(dq_ref)
            for hh in range(HP):
                lanes = slice(hh * Dh, (hh + 1) * Dh)
                delta[hh] = jnp.broadcast_to(jnp.sum(do_ref[:, lanes] * o_ref[:, lanes], axis=1, keepdims=True), (T, LANES))
            cql[...] = cq_ref[...] - lse_ref[...]

        lane_id = lax.broadcasted_iota(jnp.int32, (sq, LANES), 1)
        dk_sc[...] = jnp.zeros_like(dk_sc)
        dv_sc[...] = jnp.zeros_like(dv_sc)
        dck_sc[...] = jnp.zeros_like(dck_sc)
        kloc = lax.broadcasted_iota(jnp.int32, (sq, tq), 1)
        qloc = lax.broadcasted_iota(jnp.int32, (sq, tq), 0)

        def qblk(i, masked):
            for hh in range(HP):
                lanes = slice(hh * Dh, (hh + 1) * Dh)
                k = k_ref[:, lanes]
                v16 = v_ref[:, lanes].astype(BF16)
                ck = ck_ref[hh]
                for r in range(tq // sq):
                    rows = pl.ds(pl.multiple_of(i * tq + r * sq, sq), sq)
                    q = q_ref[rows, lanes]
                    do16 = do_ref[rows, lanes].astype(BF16)
                    e = lax.dot_general(q, k, nt, preferred_element_type=F32) * scale + (jnp.tile(cql[hh, rows, :], (1, rep)) - ck)
                    p = jnp.exp(e)
                    if masked:
                        p = jnp.where(kloc <= qloc + r * sq, p, 0.0)
                    dv_sc[:, lanes] += lax.dot_general(p.astype(BF16), do16, tn, preferred_element_type=F32)
                    dp = lax.dot_general(do16, v16, nt, preferred_element_type=F32)
                    ds = p * (dp - jnp.tile(delta[hh, rows, :], (1, rep)))
                    ds16 = ds.astype(BF16)
                    dk_sc[:, lanes] += lax.dot_general(ds16, q, tn, preferred_element_type=F32)
                    dq_ref[rows, lanes] += jnp.dot(ds16, k, preferred_element_type=F32) * scale
                    dcq_ref[rows, :] += jnp.where(lane_id == h * HP + hh, jnp.sum(ds, axis=1, keepdims=True), 0.0)
                    dck_sc[hh] += jnp.sum(ds, axis=0, keepdims=True)

        def above(i, c):
            qblk(i, False)
            return c

        qblk(j, True)
        lax.fori_loop(j + 1, nq, above, 0)
        dk_ref[...] = dk_sc[...] * scale
        dv_ref[...] = dv_sc[...].astype(BF16)
        for hh in range(HP):
            dck_ref[hh] = -dck_sc[hh]

    whole = lambda off: _bs((T, W2), lambda h, j: (0, h + off))
    blk = lambda off: _bs((tq, W2), lambda h, j: (j, h + off))
    return pl.pallas_call(
        body, name=name, grid=(H // HP, nq),
        in_specs=[whole(0), blk(0), blk(2 * (H // HP)), whole(0), whole(0), _bs((HP, T, LANES), lambda h, j: (h, 0, 0)),
                  _bs((HP, T, LANES), lambda h, j: (h, 0, 0)), _bs((HP, None, 1, tq), lambda h, j: (h, j, 0, 0))],
        out_specs=[whole(0), blk(0), blk(0), _bs((T, LANES), lambda h, j: (0, 0)),
                   _bs((HP, None, 1, tq), lambda h, j: (h, j, 0, 0))],
        out_shape=[jax.ShapeDtypeStruct((T, H * Dh), F32), jax.ShapeDtypeStruct((T, H * Dh), F32), jax.ShapeDtypeStruct((T, H * Dh), BF16),
                   jax.ShapeDtypeStruct((T, LANES), F32), jax.ShapeDtypeStruct((H, nq, 1, tq), F32)],
        scratch_shapes=[pltpu.VMEM((HP, T, LANES), F32), pltpu.VMEM((HP, T, LANES), F32), pltpu.VMEM((tq, W2), F32), pltpu.VMEM((tq, W2), F32),
                        pltpu.VMEM((HP, 1, tq), F32)],
        compiler_params=_params(("arbitrary", "arbitrary")))(qn, kn, proj, do, o, lse, cum_q, cum_k)


def _pool_fwd(name, proj, E):
    T = proj.shape[0]
    PG = len(POOL_WINDOWS)
    PD = E // PG
    tT = _t(256, T)
    hb = tT // POOL_HALO

    def body(u_ref, halo_ref, o_ref, buf):
        g, tb = pl.program_id(0), pl.program_id(1)
        u = u_ref[...]
        buf[pl.ds(POOL_HALO, tT), :] = u
        buf[pl.ds(0, POOL_HALO), :] = jnp.where(tb == 0, 0.0, halo_ref[...])
        t = tb * tT + lax.broadcasted_iota(jnp.int32, (tT, 1), 0)
        for gi, w in enumerate(POOL_WINDOWS):
            @pl.when(g == gi)
            def _():
                acc = u
                for d in range(1, w):
                    acc = acc + buf[pl.ds(POOL_HALO - d, tT), :]
                cnt = jnp.minimum(t + 1, w).astype(F32)
                o_ref[...] = (acc / cnt - u).astype(BF16)

    return pl.pallas_call(
        body, name=name, grid=(PG, T // tT),
        in_specs=[_bs((tT, PD), lambda g, t: (t, g)), _bs((POOL_HALO, PD), lambda g, t: (jnp.maximum(t * hb - 1, 0), g))],
        out_specs=_bs((tT, PD), lambda g, t: (t, g)), out_shape=jax.ShapeDtypeStruct((T, E), BF16),
        scratch_shapes=[pltpu.VMEM((tT + POOL_HALO, PD), F32)],
        compiler_params=_params(("parallel", "parallel")))(proj, proj)


def _pool_bwd(name, dpm, E):
    T = dpm.shape[0]
    PG = len(POOL_WINDOWS)
    PD = E // PG
    tT = _t(256, T)
    hb = tT // POOL_HALO
    nT = T // tT

    def body(d_ref, halo_ref, o_ref, buf):
        g, tb = pl.program_id(0), pl.program_id(1)
        d = d_ref[...]
        t = tb * tT + lax.broadcasted_iota(jnp.int32, (tT, 1), 0)
        th = (tb + 1) * tT + lax.broadcasted_iota(jnp.int32, (POOL_HALO, 1), 0)
        for gi, w in enumerate(POOL_WINDOWS):
            @pl.when(g == gi)
            def _():
                dn = d / jnp.minimum(t + 1, w).astype(F32)
                buf[pl.ds(0, tT), :] = dn
                buf[pl.ds(tT, POOL_HALO), :] = jnp.where(tb == nT - 1, 0.0, halo_ref[...] / jnp.minimum(th + 1, w).astype(F32))
                acc = dn
                for s in range(1, w):
                    acc = acc + buf[pl.ds(s, tT), :]
                o_ref[...] = (acc - d).astype(BF16)

    return pl.pallas_call(
        body, name=name, grid=(PG, nT),
        in_specs=[_bs((tT, PD), lambda g, t: (t, g)), _bs((POOL_HALO, PD), lambda g, t: (jnp.minimum((t + 1) * hb, T // POOL_HALO - 1), g))],
        out_specs=_bs((tT, PD), lambda g, t: (t, g)), out_shape=jax.ShapeDtypeStruct((T, E), BF16),
        scratch_shapes=[pltpu.VMEM((tT + POOL_HALO, PD), F32)],
        compiler_params=_params(("parallel", "parallel")))(dpm, dpm)


def _coords():
    x, y, c = lax.axis_index("x"), lax.axis_index("y"), lax.axis_index("c")
    chips = [(1 - x, y), (x, 1 - y), (1 - x, 1 - y)]
    return x, y, c, 2 * x + y, (x, y, 1 - c), chips


def _chip_allgather(name, bufs):
    n = len(bufs)

    def body(*refs):
        outs = refs[n:2 * n]
        send, recv, fsend, frecv = refs[2 * n:]
        x, y, c, p, sib, chips = _coords()

        def direct(t, j, chip):
            return pltpu.make_async_remote_copy(src_ref=outs[t].at[p, c], dst_ref=outs[t].at[p, c], send_sem=send.at[t, j],
                                                recv_sem=recv.at[t, j], device_id=(*chip, c), device_id_type=MESH)

        def landed(t, j, chip):
            blk = outs[t].at[2 * chip[0] + chip[1], c]
            return pltpu.make_async_remote_copy(src_ref=blk, dst_ref=blk, send_sem=send.at[t, j],
                                                recv_sem=recv.at[t, j], device_id=(*chip, c), device_id_type=MESH)

        def passed(t, j, chip, half):
            blk = outs[t].at[2 * chip[0] + chip[1], half]
            return pltpu.make_async_remote_copy(src_ref=blk, dst_ref=blk, send_sem=fsend.at[t, j], recv_sem=frecv.at[t, j],
                                                device_id=sib, device_id_type=MESH)

        first = [direct(t, j, chip) for t in range(n) for j, chip in enumerate(chips)]
        for cp in first:
            cp.start()
        fwd = []
        for j, chip in enumerate(chips):
            for t in range(n):
                landed(t, j, chip).wait_recv()
                f = passed(t, j, chip, c)
                f.start()
                fwd.append(f)
        for j, chip in enumerate(chips):
            for t in range(n):
                passed(t, j, chip, 1 - c).wait_recv()
        for cp in first + fwd:
            cp.wait_send()

    return pl.pallas_call(
        body, name=name, in_specs=[ANY] * n, out_specs=[ANY] * n,
        out_shape=[jax.ShapeDtypeStruct(a.shape, a.dtype) for a in bufs],
        input_output_aliases={t: t for t in range(n)},
        scratch_shapes=[pltpu.SemaphoreType.DMA((n, 3))] * 4,
    )(*bufs)


SEM = pl.BlockSpec(memory_space=pltpu.SEMAPHORE)
TOKEN = jax.ShapeDtypeStruct((SUB, LANES), F32)


def _split_params():
    return pltpu.CompilerParams(has_side_effects=pltpu.SideEffectType.DATAFLOW_SIDE_EFFECTING)


def _struct(a):
    return jax.ShapeDtypeStruct(a.shape, a.dtype)


def _gather_start(name, bufs, deps):
    n, nd = len(bufs), len(deps)

    def body(*refs):
        outs = refs[n + nd:2 * n + nd]
        send, recv, token = refs[2 * n + nd:]
        x, y, c, p, sib, chips = _coords()
        for t in range(n):
            for j, chip in enumerate(chips):
                pltpu.make_async_remote_copy(src_ref=outs[t].at[p, c], dst_ref=outs[t].at[p, c], send_sem=send.at[3 * t + j],
                                             recv_sem=recv.at[3 * t + j], device_id=(*chip, c), device_id_type=MESH).start()
        token[...] = jnp.zeros_like(token)

    res = pl.pallas_call(
        body, name=name, in_specs=[ANY] * (n + nd), out_specs=[ANY] * n + [SEM, SEM, pl.BlockSpec(memory_space=pltpu.VMEM)],
        out_shape=[_struct(a) for a in bufs] + [pltpu.SemaphoreType.DMA((3 * n,)), pltpu.SemaphoreType.DMA((3 * n,)), TOKEN],
        input_output_aliases={t: t for t in range(n)}, compiler_params=_split_params(),
    )(*bufs, *deps)
    return list(res[:n]), res[n], res[n + 1], res[n + 2]


def _gather_wait(name, bufs, send, recv, after):
    n = len(bufs)

    def body(*refs):
        send_r, recv_r = refs[n], refs[n + 1]
        outs = refs[n + 3:2 * n + 3]
        x, y, c, p, sib, chips = _coords()
        for t in range(n):
            for j, chip in enumerate(chips):
                cp = pltpu.make_async_remote_copy(src_ref=outs[t].at[p, c], dst_ref=outs[t].at[2 * chip[0] + chip[1], c], send_sem=send_r.at[3 * t + j],
                                                  recv_sem=recv_r.at[3 * t + j], device_id=(*chip, c), device_id_type=MESH)
                cp.wait_send()
                cp.wait_recv()

    return list(pl.pallas_call(
        body, name=name, in_specs=[ANY] * n + [SEM, SEM, ANY], out_specs=[ANY] * n, out_shape=[_struct(a) for a in bufs],
        input_output_aliases={t: t for t in range(n)}, compiler_params=_split_params(),
    )(*bufs, send, recv, after))


def _gather_forward(name, bufs):
    n = len(bufs)

    def body(*refs):
        outs = refs[n:2 * n]
        fsend, frecv = refs[2 * n:]
        x, y, c, p, sib, chips = _coords()

        def passed(t, j, chip, half):
            blk = outs[t].at[2 * chip[0] + chip[1], half]
            return pltpu.make_async_remote_copy(src_ref=blk, dst_ref=blk, send_sem=fsend.at[t, j], recv_sem=frecv.at[t, j],
                                                device_id=sib, device_id_type=MESH)

        fwd = [passed(t, j, chip, c) for t in range(n) for j, chip in enumerate(chips)]
        for cp in fwd:
            cp.start()
        for t in range(n):
            for j, chip in enumerate(chips):
                passed(t, j, chip, 1 - c).wait_recv()
        for cp in fwd:
            cp.wait_send()

    return list(pl.pallas_call(
        body, name=name, in_specs=[ANY] * n, out_specs=[ANY] * n, out_shape=[_struct(a) for a in bufs],
        input_output_aliases={t: t for t in range(n)}, scratch_shapes=[pltpu.SemaphoreType.DMA((n, 3))] * 2,
    )(*bufs))


def _relations():
    x, y, c = lax.axis_index("x"), lax.axis_index("y"), lax.axis_index("c")
    out = []
    for code in range(1, 8):
        tx = 1 - x if code & 4 else x
        ty = 1 - y if code & 2 else y
        tc = 1 - c if code & 1 else c
        out.append((code - 1, (tx, ty, tc), 2 * tx + ty, tc))
    return out


def _full_exchange_start(name, parts):
    n = len(parts)
    lands = [lax.empty((7,) + a.shape[2:], a.dtype) for a in parts]

    def body(*refs):
        src, dst = refs[2 * n:3 * n], refs[3 * n:4 * n]
        send, recv, token = refs[4 * n:]
        for t in range(n):
            for k, dev, q, half in _relations():
                pltpu.make_async_remote_copy(src_ref=src[t].at[half, q], dst_ref=dst[t].at[k], send_sem=send.at[7 * t + k],
                                             recv_sem=recv.at[7 * t + k], device_id=dev, device_id_type=MESH).start()
        token[...] = jnp.zeros_like(token)

    res = pl.pallas_call(
        body, name=name, in_specs=[ANY] * (2 * n), out_specs=[ANY] * (2 * n) + [SEM, SEM, pl.BlockSpec(memory_space=pltpu.VMEM)],
        out_shape=[_struct(a) for a in parts + lands] + [pltpu.SemaphoreType.DMA((7 * n,)), pltpu.SemaphoreType.DMA((7 * n,)), TOKEN],
        input_output_aliases={t: t for t in range(2 * n)}, compiler_params=_split_params(),
    )(*parts, *lands)
    return list(res[:n]), list(res[n:2 * n]), res[2 * n], res[2 * n + 1], res[2 * n + 2]


def _full_exchange_wait(name, parts, lands, send, recv, after):
    n = len(parts)

    def body(*refs):
        send_r, recv_r = refs[2 * n], refs[2 * n + 1]
        src, dst = refs[2 * n + 3:3 * n + 3], refs[3 * n + 3:4 * n + 3]
        for t in range(n):
            for k, dev, q, half in _relations():
                cp = pltpu.make_async_remote_copy(src_ref=src[t].at[half, q], dst_ref=dst[t].at[k], send_sem=send_r.at[7 * t + k],
                                                  recv_sem=recv_r.at[7 * t + k], device_id=dev, device_id_type=MESH)
                cp.wait_send()
                cp.wait_recv()

    res = pl.pallas_call(
        body, name=name, in_specs=[ANY] * (2 * n) + [SEM, SEM, ANY], out_specs=[ANY] * (2 * n),
        out_shape=[_struct(a) for a in parts + lands], input_output_aliases={t: t for t in range(2 * n)},
        compiler_params=_split_params(),
    )(*parts, *lands, send, recv, after)
    return list(res[:n]), list(res[n:])


def _chip_exchange_start(name, sums):
    n = len(sums)
    lands = [lax.empty((3,) + a.shape[1:], a.dtype) for a in sums]

    def body(*refs):
        src, dst = refs[2 * n:3 * n], refs[3 * n:4 * n]
        send, recv, token = refs[4 * n:]
        x, y, c, p, sib, chips = _coords()
        for t in range(n):
            for j, chip in enumerate(chips):
                pltpu.make_async_remote_copy(src_ref=src[t].at[2 * chip[0] + chip[1]], dst_ref=dst[t].at[j], send_sem=send.at[3 * t + j],
                                             recv_sem=recv.at[3 * t + j], device_id=(*chip, c), device_id_type=MESH).start()
        token[...] = jnp.zeros_like(token)

    res = pl.pallas_call(
        body, name=name, in_specs=[ANY] * (2 * n), out_specs=[ANY] * (2 * n) + [SEM, SEM, pl.BlockSpec(memory_space=pltpu.VMEM)],
        out_shape=[_struct(a) for a in sums + lands] + [pltpu.SemaphoreType.DMA((3 * n,)), pltpu.SemaphoreType.DMA((3 * n,)), TOKEN],
        input_output_aliases={t: t for t in range(2 * n)}, compiler_params=_split_params(),
    )(*sums, *lands)
    return list(res[:n]), list(res[n:2 * n]), res[2 * n], res[2 * n + 1], res[2 * n + 2]


def _chip_exchange_wait(name, sums, lands, send, recv, after):
    n = len(sums)

    def body(*refs):
        send_r, recv_r = refs[2 * n], refs[2 * n + 1]
        src, dst = refs[2 * n + 3:3 * n + 3], refs[3 * n + 3:4 * n + 3]
        x, y, c, p, sib, chips = _coords()
        for t in range(n):
            for j, chip in enumerate(chips):
                cp = pltpu.make_async_remote_copy(src_ref=src[t].at[2 * chip[0] + chip[1]], dst_ref=dst[t].at[j], send_sem=send_r.at[3 * t + j],
                                                  recv_sem=recv_r.at[3 * t + j], device_id=(*chip, c), device_id_type=MESH)
                cp.wait_send()
                cp.wait_recv()

    res = pl.pallas_call(
        body, name=name, in_specs=[ANY] * (2 * n) + [SEM, SEM, ANY], out_specs=[ANY] * (2 * n),
        out_shape=[_struct(a) for a in sums + lands], input_output_aliases={t: t for t in range(2 * n)},
        compiler_params=_split_params(),
    )(*sums, *lands, send, recv, after)
    return list(res[:n]), list(res[n:])


def _pair_exchange(name, parts):
    n = len(parts)

    def body(*refs):
        ins, outs = refs[:n], refs[n:2 * n]
        send, recv = refs[2 * n:]
        x, y, c, p, sib, chips = _coords()
        cps = [pltpu.make_async_remote_copy(src_ref=ins[t].at[1 - c], dst_ref=outs[t], send_sem=send.at[t], recv_sem=recv.at[t],
                                            device_id=sib, device_id_type=MESH) for t in range(n)]
        for cp in cps:
            cp.start()
        for cp in cps:
            cp.wait()

    return pl.pallas_call(
        body, name=name, in_specs=[ANY] * n, out_specs=[ANY] * n,
        out_shape=[jax.ShapeDtypeStruct(a.shape[1:], a.dtype) for a in parts],
        scratch_shapes=[pltpu.SemaphoreType.DMA((n,))] * 2,
    )(*parts)


def _chip_exchange(name, sums):
    n = len(sums)

    def body(*refs):
        ins, outs = refs[:n], refs[n:2 * n]
        send, recv = refs[2 * n:]
        x, y, c, p, sib, chips = _coords()
        cps = [pltpu.make_async_remote_copy(src_ref=ins[t].at[2 * chip[0] + chip[1]], dst_ref=outs[t].at[j], send_sem=send.at[t, j],
                                            recv_sem=recv.at[t, j], device_id=(*chip, c), device_id_type=MESH)
               for t in range(n) for j, chip in enumerate(chips)]
        for cp in cps:
            cp.start()
        for cp in cps:
            cp.wait()

    return pl.pallas_call(
        body, name=name, in_specs=[ANY] * n, out_specs=[ANY] * n,
        out_shape=[jax.ShapeDtypeStruct((3,) + a.shape[1:], a.dtype) for a in sums],
        scratch_shapes=[pltpu.SemaphoreType.DMA((n, 3))] * 2,
    )(*sums)


def _pair_share(name, bufs, items, deps=()):
    n = len(items)
    nb = len(bufs)
    nd = len(deps)

    def body(*refs):
        outs = refs[nb + nd:2 * nb + nd]
        send, recv = refs[2 * nb + nd:]
        x, y, c, p, sib, chips = _coords()

        def blk(t, half):
            o, lead = items[t]
            return outs[o].at[p if lead == 'chip' else lead, half]

        def swap(t, half):
            return pltpu.make_async_remote_copy(src_ref=blk(t, half), dst_ref=blk(t, half), send_sem=send.at[t], recv_sem=recv.at[t],
                                                device_id=sib, device_id_type=MESH)

        cps = [swap(t, c) for t in range(n)]
        for cp in cps:
            cp.start()
        for t in range(n):
            swap(t, 1 - c).wait_recv()
        for cp in cps:
            cp.wait_send()

    return list(pl.pallas_call(
        body, name=name, in_specs=[ANY] * (nb + nd), out_specs=[ANY] * nb,
        out_shape=[jax.ShapeDtypeStruct(b.shape, b.dtype) for b in bufs],
        input_output_aliases={t: t for t in range(nb)},
        scratch_shapes=[pltpu.SemaphoreType.DMA((n,))] * 2,
    )(*bufs, *deps))


def _flat2(a, lead):
    return a.reshape(a.shape[:lead] + (-1, a.shape[-1]))


def _reduce_begin(tag, parts):
    parts, lands, send, recv, token = _full_exchange_start(f"rs_start_{tag}", parts)
    return (parts, lands, send, recv), token


def _reduce_end(tag, state, after, dests, bufs, buf_shapes):
    c = lax.axis_index("c").astype(jnp.int32)
    p = (2 * lax.axis_index("x") + lax.axis_index("y")).astype(jnp.int32)
    parts, lands = _full_exchange_wait(f"rs_wait_{tag}", *state, after)

    def total(a, *others):
        s = a.astype(F32)
        for b in others:
            s = s + b.astype(F32)
        return (s,)

    for t, (mine, theirs) in enumerate(zip(parts, lands)):
        o, lead = dests[t]
        shape = buf_shapes[o]
        rows, cols = shape[2], shape[3]
        m3, t3 = mine.reshape(2 * N_CHIPS, rows, cols), theirs.reshape(7, rows, cols)
        pre = jnp.stack([c * N_CHIPS + p] + [jnp.int32(k) for k in range(7)] + [c, p if lead == 'chip' else jnp.int32(lead)])
        out = ('x', shape, F32, (None, None, 'tr', cols), lambda r, pr: (pr[9], pr[8], r, 0))
        bufs[o] = _rows(f"rs_sum_{tag}_{t}", total, [(m3, 's', cols, 0)] + [(t3, 's', cols, 1 + k) for k in range(7)], [out], 256,
                        pre=pre, into=bufs[o])[0]


def kernel(x, norm_w, out_proj, s5_in_proj, s5_a_re, s5_a_im, s5_log_dt, s5_b_re, s5_b_im, s5_c_re, s5_c_im, s5_d, s5_w_glu, s5_b_glu, fox_in_proj, fox_q_norm, fox_k_norm, fox_f_bias, pool_in_proj, pool_w_group, pool_scale, loss_target, m_norm_w, m_out_proj, m_s5_in_proj, m_s5_a_re, m_s5_a_im, m_s5_log_dt, m_s5_b_re, m_s5_b_im, m_s5_c_re, m_s5_c_im, m_s5_d, m_s5_w_glu, m_s5_b_glu, m_fox_in_proj, m_fox_q_norm, m_fox_k_norm, m_fox_f_bias, m_pool_in_proj, m_pool_w_group, m_pool_scale, v_norm_w, v_out_proj, v_s5_in_proj, v_s5_a_re, v_s5_a_im, v_s5_log_dt, v_s5_b_re, v_s5_b_im, v_s5_c_re, v_s5_c_im, v_s5_d, v_s5_w_glu, v_s5_b_glu, v_fox_in_proj, v_fox_q_norm, v_fox_k_norm, v_fox_f_bias, v_pool_in_proj, v_pool_w_group, v_pool_scale):
    weights = dict(norm_w=norm_w, out_proj=out_proj, s5_in_proj=s5_in_proj, s5_a_re=s5_a_re, s5_a_im=s5_a_im, s5_log_dt=s5_log_dt,
                   s5_b_re=s5_b_re, s5_b_im=s5_b_im, s5_c_re=s5_c_re, s5_c_im=s5_c_im, s5_d=s5_d, s5_w_glu=s5_w_glu, s5_b_glu=s5_b_glu,
                   fox_in_proj=fox_in_proj, fox_q_norm=fox_q_norm, fox_k_norm=fox_k_norm, fox_f_bias=fox_f_bias,
                   pool_in_proj=pool_in_proj, pool_w_group=pool_w_group, pool_scale=pool_scale)
    mom_m = dict(norm_w=m_norm_w, out_proj=m_out_proj, s5_in_proj=m_s5_in_proj, s5_a_re=m_s5_a_re, s5_a_im=m_s5_a_im, s5_log_dt=m_s5_log_dt,
                 s5_b_re=m_s5_b_re, s5_b_im=m_s5_b_im, s5_c_re=m_s5_c_re, s5_c_im=m_s5_c_im, s5_d=m_s5_d, s5_w_glu=m_s5_w_glu, s5_b_glu=m_s5_b_glu,
                 fox_in_proj=m_fox_in_proj, fox_q_norm=m_fox_q_norm, fox_k_norm=m_fox_k_norm, fox_f_bias=m_fox_f_bias,
                 pool_in_proj=m_pool_in_proj, pool_w_group=m_pool_w_group, pool_scale=m_pool_scale)
    mom_v = dict(norm_w=v_norm_w, out_proj=v_out_proj, s5_in_proj=v_s5_in_proj, s5_a_re=v_s5_a_re, s5_a_im=v_s5_a_im, s5_log_dt=v_s5_log_dt,
                 s5_b_re=v_s5_b_re, s5_b_im=v_s5_b_im, s5_c_re=v_s5_c_re, s5_c_im=v_s5_c_im, s5_d=v_s5_d, s5_w_glu=v_s5_w_glu, s5_b_glu=v_s5_b_glu,
                 fox_in_proj=v_fox_in_proj, fox_q_norm=v_fox_q_norm, fox_k_norm=v_fox_k_norm, fox_f_bias=v_fox_f_bias,
                 pool_in_proj=v_pool_in_proj, pool_w_group=v_pool_w_group, pool_scale=v_pool_scale)
    return _step(x, loss_target, weights, mom_m, mom_v)


BIG = ('out_proj', 's5_in_proj', 's5_w_glu', 'fox_in_proj', 'pool_in_proj', 'pool_w_group')
SMALL = ('norm_w', 's5_a_re', 's5_a_im', 's5_log_dt', 's5_b_re', 's5_b_im', 's5_c_re', 's5_c_im', 's5_d', 's5_b_glu',
         'fox_q_norm', 'fox_k_norm', 'fox_f_bias', 'pool_scale')
SMALL_SHARDED = ('s5_d', 's5_b_glu', 'pool_scale')
GROUP_AXIS_1 = ('s5_a_re', 's5_a_im', 's5_b_re', 's5_b_im', 's5_c_re', 's5_c_im')
ORDER = ('norm_w', 'out_proj', 's5_in_proj', 's5_a_re', 's5_a_im', 's5_log_dt', 's5_b_re', 's5_b_im', 's5_c_re', 's5_c_im', 's5_d',
         's5_w_glu', 's5_b_glu', 'fox_in_proj', 'fox_q_norm', 'fox_k_norm', 'fox_f_bias', 'pool_in_proj', 'pool_w_group', 'pool_scale')


def _split2(shape):
    if shape[0] % 2 == 0:
        return (2, shape[0] // 2) + tuple(shape[1:])
    assert shape[0] == 1 and shape[1] % 2 == 0
    return (2, shape[1] // 2) + tuple(shape[2:])


def _adamw_big(n, w, grads, mom_m, mom_v, delta, new_m, new_v):
    shape = w[n].shape
    if shape[-1] % LANES:
        f2 = lambda a: jnp.transpose(a.reshape(-1, shape[-1]))
        b2 = lambda a: jnp.transpose(a).reshape(shape)
    else:
        f2 = lambda a: a.reshape(-1, shape[-1])
        b2 = lambda a: a.reshape(shape)
    d_, m_, v_ = _adamw(f"adamw_{n}", f2(w[n]), f2(grads[n]), f2(mom_m[n]), f2(mom_v[n]))
    delta[n], new_m[n], new_v[n] = b2(d_), b2(m_), b2(v_)
    return d_


def _cast_weights(w):
    p = (2 * lax.axis_index("x") + lax.axis_index("y")).astype(jnp.int32)
    bufs = {}
    for n in BIG:
        a3 = w[n].reshape(w[n].shape[0], -1, w[n].shape[-1])
        layers, rows, cols = a3.shape
        for l in range(layers):
            out = ('x', (N_CHIPS, rows, cols), BF16, (None, 'tr', cols), lambda r, pr: (pr[0], r, 0))
            b = _rows(f"cast_{n}_{l}", lambda v: (v,), [(a3, 's', cols, 1)], [out], 256, pre=jnp.stack([p, jnp.int32(l)]))[0]
            bufs[(n, l)] = b.reshape(N_CHIPS, 2, rows // 2, cols)
    return bufs


def _step(x, loss_target, w, mom_m, mom_v):
    T, D = x.shape[1], x.shape[2]
    E = D
    G, P, C = w['s5_a_re'].shape[1], S5_STATE, S5_GROUP
    H = E // FOX_HEAD_DIM
    PG = len(POOL_WINDOWS)
    PD = E // PG
    NC = G // GROUPS_PER_CHUNK
    L = GROUPS_PER_CHUNK * P
    tq = _t(256, T)
    nq = T // tq

    wb = _cast_weights(w)
    phases = [[('s5_in_proj', 0)],
              [('s5_w_glu', 0), ('out_proj', 0)],
              [('out_proj', 1), ('fox_in_proj', 0)],
              [('out_proj', 2), ('pool_in_proj', 0), ('pool_w_group', 0), ('out_proj', 3), ('s5_in_proj', 1), ('s5_w_glu', 1)]]
    W = {}
    flight = {}

    def landed(keys, bufs):
        for k, b in zip(keys, bufs):
            W[k] = b.reshape(N_CHIPS, 2 * b.shape[2], b.shape[3])

    def take_phase(ph, after):
        bufs, send, recv, _ = flight.pop(ph)
        landed(phases[ph], _gather_forward(f"gather_{ph}_pass", _gather_wait(f"gather_{ph}_wait", bufs, send, recv, after)))

    small_full = {}
    chip = 2 * lax.axis_index("x") + lax.axis_index("y")
    sv = [lax.dynamic_update_index_in_dim(jnp.zeros((N_CHIPS, 2) + w[n].shape, F32), jnp.stack([w[n], w[n]]), chip, 0)
          for n in SMALL_SHARDED]
    got = _chip_allgather("gather_vectors", sv)
    for n, g in zip(SMALL_SHARDED, got):
        small_full[n] = jnp.transpose(g[:, 0], (1, 0, 2)).reshape(w[n].shape[0], E)
    landed(phases[0], _chip_allgather("gather_0", [wb[k] for k in phases[0]]))
    after = [W[phases[0][0]], got[0]]
    for ph in range(1, len(phases)):
        flight[ph] = _gather_start(f"gather_{ph}_start", [wb[k] for k in phases[ph]], after)
        after = [flight[ph][3]]
    gather_tokens = after

    norm_w = w['norm_w']
    h = x.reshape(T, D)
    saved = []
    dparts = {}

    def s5_consts(j):
        ar, ai, fr, fi = _s5_disc_fwd(f"s5_disc_{j}", w['s5_a_re'][j], w['s5_a_im'][j], w['s5_log_dt'][j].reshape(G, 1))
        br, bi = w['s5_b_re'][j].reshape(G * P, C), w['s5_b_im'][j].reshape(G * P, C)
        bbr, bbi = _s5_bbar(f"s5_bbar_{j}", fr.reshape(G * P, 1), fi.reshape(G * P, 1), br, bi)
        bbd = jnp.concatenate([_compact(bbr.reshape(G, P, C), NC), _compact(bbi.reshape(G, P, C), NC)], axis=2).astype(BF16)
        ct = lambda v: jnp.transpose(v, (0, 2, 1))
        cbd = jnp.concatenate([_compact(ct(w['s5_c_re'][j]), NC), -_compact(ct(w['s5_c_im'][j]), NC)], axis=2).astype(BF16)
        return dict(ar=ar, ai=ai, fr=fr, fi=fi, br=br, bi=bi, bbd=bbd, cbd=cbd,
                    ar3=ar.reshape(NC, 1, L), ai3=ai.reshape(NC, 1, L))

    for i in range(4):
        kind, j = i % 3, i // 3
        nw = norm_w[i].reshape(1, D)
        xn = _norm_fwd(f"norm_{i}", h, nw, deps=gather_tokens if i == 0 else ())
        if kind == 0:
            k5 = s5_consts(j)
            proj = _mm_proj(f"s5_proj_{i}", xn, W[('s5_in_proj', j)])
            dsk = small_full['s5_d'][j].reshape(1, E)
            y1, g, hs = _s5_fwd(f"s5_scan_{i}", proj, k5['bbd'], k5['cbd'], k5['ar3'], k5['ai3'], dsk, E)
            bglu = small_full['s5_b_glu'][j].reshape(1, E)
            if i == 0:
                take_phase(1, y1)

            def glu_epi(acc, b, y1t, z):
                lin = acc + b
                return lin, (_gelu(y1t) * _sigmoid(lin)) * _silu(z)

            lin, a = _mm_rowsharded(
                f"s5_glu_{i}", g, W[('s5_w_glu', j)], epi=glu_epi,
                extras=lambda tm, tn: [(bglu, _rowvec(tn)), (y1, _tile(tm, tn)), (proj, _tile(tm, tn, E // tn))],
                outs_fn=lambda tm, tn: [((T, E), F32, _tile(tm, tn)), ((T, E), BF16, _tile(tm, tn))])
            saved.append(dict(h=h, xn=xn, proj=proj, y1=y1, g=g, hs=hs, lin=lin, a=a, k5=k5, dsk=dsk))
        elif kind == 1:
            fox_w = jnp.transpose(W[('fox_in_proj', j)], (1, 0, 2)).reshape(D, -1)
            w_qkvz = fox_w[:, :4 * E]
            w_f = jnp.pad(fox_w[:, 4 * E:], ((0, 0), (0, LANES - H)))
            proj = _mm_plain(f"fox_proj_{i}", xn, w_qkvz)[0]
            flog = _mm_plain(f"fox_gate_proj_{i}", xn, w_f)[0]
            fb = jnp.pad(w['fox_f_bias'][j].reshape(1, H), ((0, 0), (0, LANES - H)))
            wq, wk = w['fox_q_norm'][j].reshape(1, FOX_HEAD_DIM), w['fox_k_norm'][j].reshape(1, FOX_HEAD_DIM)
            qn, kn = _qk_norm(f"fox_qk_norm_{i}", proj, wq, wk, H)
            cum = _cum_rows(f"fox_cum_{i}", flog, fb, False, True)
            cum_t = jnp.transpose(cum)[:H]
            cum_q = jnp.broadcast_to(cum_t[:, :, None], (H, T, LANES))
            cum_k = cum_t.reshape(H, nq, 1, tq)
            y, lse = _attn_fwd(f"fox_attn_{i}", qn, kn, proj, cum_q, cum_k, H)
            a = _rows(f"fox_gate_{i}", lambda yt, z: (yt * _silu(z),), [(y, 'r', E, 0), (proj, 'r', E, 3)], [('r', E, BF16)], 256)[0]
            saved.append(dict(h=h, xn=xn, proj=proj, flog=flog, fb=fb, wq=wq, wk=wk, qn=qn, kn=kn, cum_q=cum_q, cum_k=cum_k, y=y, lse=lse, a=a,
                              w_qkvz=w_qkvz, w_f=w_f))
        else:
            w_pg = W[('pool_w_group', j)].reshape(N_CHIPS, PG, PD // N_CHIPS, PD)
            proj = _mm_proj(f"pool_proj_{i}", xn, W[('pool_in_proj', j)])
            pm = _pool_fwd(f"pool_win_{i}", proj, E)
            scale = small_full['pool_scale'][j].reshape(1, E)
            tm, tn, tk = _t(512, T), _t(512, PD), w_pg.shape[2]
            kb, nb = PD // tk, PD // tn
            mixed, a = _mm(
                f"pool_mix_{i}", pm, w_pg, M=T, N=PD, K=PD, tm=tm, tn=tn, tk=tk, groups=PG,
                a_spec=_bs((tm, tk), lambda g, m, n, k: (m, g * kb + k)),
                b_spec=_bs((None, None, tk, tn), lambda g, m, n, k: (k, g, 0, n)),
                extras=[(scale, _bs((1, tn), lambda g, m, n, k: (0, g * nb + n))),
                        (proj, _bs((tm, tn), lambda g, m, n, k: (m, E // tn + g * nb + n)))],
                epi=lambda acc, sc, z: (acc, (acc * sc) * _silu(z)),
                outs=[((T, E), F32, _bs((tm, tn), lambda g, m, n, k: (m, g * nb + n))),
                      ((T, E), BF16, _bs((tm, tn), lambda g, m, n, k: (m, g * nb + n)))])
            saved.append(dict(h=h, xn=xn, proj=proj, pm=pm, mixed=mixed, scale=scale, a=a, w_pg=w_pg))
        h = _mm_rowsharded(f"out_proj_{i}", saved[-1]['a'], W[('out_proj', i)], epi=lambda acc, r: (r + acc,),
                           extras=lambda tm, tn: [(h, _tile(tm, tn))],
                           outs_fn=lambda tm, tn: [((T, D), F32, _tile(tm, tn))])[0]
        if i < 2:
            take_phase(i + 2, h)

    dh, dh16, loss_cols = _loss(h, loss_target.reshape(T, D))
    loss = lax.psum(jnp.sum(loss_cols), ("x", "y", "c"))

    gsmall = {n: [None] * w[n].shape[0] for n in SMALL}
    big_index = {n: o for o, n in enumerate(BIG)}
    rs_shapes = [None] * (len(BIG) + 1)
    rs_bufs = [None] * (len(BIG) + 1)
    rs_dests_all = []
    pending = None

    def reduce_layer(tag, named_parts):
        parts, dests = [], []
        for n, l, pt in named_parts:
            o = big_index[n] if n in big_index else len(BIG)
            half = pt.shape[2:]
            rs_shapes[o] = (N_CHIPS if l == 'chip' else w[n].shape[0], 2, math.prod(half[:-1]), half[-1])
            parts.append(pt)
            dests.append((o, l))
        rs_dests_all.extend(dests)
        state, token = _reduce_begin(tag, parts)
        return (tag, state, dests), token

    token = loss.reshape(1, 1)
    for i in reversed(range(4)):
        kind, j = i % 3, i // 3
        sv_ = saved[i]
        nw = norm_w[i].reshape(1, D)
        w_out = W[('out_proj', i)]
        after_start = [token] if token is not None else ()
        layer_parts = [('out_proj', i, _mm_dw_rows(f"d_out_proj_{i}", sv_['a'], dh16, deps=after_start))]
        if kind == 0:
            w_glu = W[('s5_w_glu', j)]
            proj, y1, lin, k5 = sv_['proj'], sv_['y1'], sv_['lin'], sv_['k5']

            def da_epi(da, y1t, lint, z):
                gt, sg = _gelu(y1t), _sigmoid(lint)
                dy2 = da * _silu(z)
                dlin = (dy2 * gt) * (sg * (1.0 - sg))
                return da * (gt * sg) * _dsilu(z), dlin, dy2 * sg, _colsum(dlin)

            nm = T // _t(512, T)
            dz, dlin, dgd, dbg = _mm_rowsharded_t(
                f"d_s5_act_{i}", dh16, w_out, epi=da_epi, deps=after_start,
                extras=lambda tm, tn: [(y1, _tile(tm, tn)), (lin, _tile(tm, tn)), (proj, _tile(tm, tn, E // tn))],
                outs_fn=lambda tm, tn: [((T, E), BF16, _tile(tm, tn)), ((T, E), BF16, _tile(tm, tn)), ((T, E), F32, _tile(tm, tn)),
                                        ((nm, 1, E), F32, _bs((None, 1, tn), lambda g, m, n, k: (m, 0, n)))])
            gsmall['s5_b_glu'][j] = jnp.sum(dbg, axis=(0, 1))
            layer_parts.append(('s5_w_glu', j, _mm_dw_rows(f"d_s5_w_glu_{i}", sv_['g'], dlin)))
            glu_deps = ()
            if i == 0:
                early, early_token = reduce_layer("l0a", layer_parts)
                layer_parts, glu_deps = [], [early_token]
            dy1 = _mm_rowsharded_t(
                f"d_s5_glu_{i}", dlin, w_glu, epi=lambda acc, d, y1t: ((acc + d) * _dgelu(y1t),), deps=glu_deps,
                extras=lambda tm, tn: [(dgd, _tile(tm, tn)), (y1, _tile(tm, tn))],
                outs_fn=lambda tm, tn: [((T, E), F32, _tile(tm, tn))])[0]
            du, dbd, dcd, dab, ddk = _s5_bwd(f"d_s5_scan_{i}", dy1, proj, sv_['hs'], k5['bbd'], k5['cbd'], k5['ar3'], k5['ai3'], sv_['dsk'], E)
            gsmall['s5_d'][j] = ddk.reshape(E)
            gsmall['s5_c_re'][j] = jnp.transpose(_uncompact(dcd[:, :, :L], G), (0, 2, 1))
            gsmall['s5_c_im'][j] = -jnp.transpose(_uncompact(dcd[:, :, L:], G), (0, 2, 1))
            dbbr = _uncompact(dbd[:, :, :L], G).reshape(G * P, C)
            dbbi = _uncompact(dbd[:, :, L:], G).reshape(G * P, C)
            dbr, dbi, dfr, dfi = _s5_bbar_bwd(f"d_s5_bbar_{i}", k5['fr'].reshape(G * P, 1), k5['fi'].reshape(G * P, 1), k5['br'], k5['bi'], dbbr, dbbi)
            gsmall['s5_b_re'][j] = dbr.reshape(G, P, C)
            gsmall['s5_b_im'][j] = dbi.reshape(G, P, C)
            dab = jnp.sum(dab, axis=1)
            dare, daim, dldt = _s5_disc_bwd(f"d_s5_disc_{i}", w['s5_a_re'][j], w['s5_a_im'][j], w['s5_log_dt'][j].reshape(G, 1),
                                            (dab[:, :L].reshape(G, P), dab[:, L:].reshape(G, P), dfr.reshape(G, P), dfi.reshape(G, P)))
            gsmall['s5_a_re'][j], gsmall['s5_a_im'][j], gsmall['s5_log_dt'][j] = dare, daim, dldt.reshape(G)
            dproj = jnp.concatenate([du, dz], axis=1)
            layer_parts.append(('s5_in_proj', j, _mm_dw_cols(f"d_s5_in_proj_{i}", sv_['xn'], dproj)))
            dxn = _mm_colsharded_t(f"d_s5_xn_{i}", dproj, W[('s5_in_proj', j)])
        elif kind == 1:
            proj, y = sv_['proj'], sv_['y']
            do, dz = _mm_rowsharded_t(
                f"d_fox_act_{i}", dh16, w_out, epi=lambda da, yt, z: (da * _silu(z), (da * yt) * _dsilu(z)), deps=after_start,
                extras=lambda tm, tn: [(y, _tile(tm, tn)), (proj, _tile(tm, tn, 3 * E // tn))],
                outs_fn=lambda tm, tn: [((T, E), F32, _tile(tm, tn)), ((T, E), BF16, _tile(tm, tn))])
            dqn, dkn, dv, dcq, dck = _attn_bwd(f"d_fox_attn_{i}", sv_['qn'], sv_['kn'], proj, do, y, sv_['lse'], sv_['cum_q'], sv_['cum_k'], H)
            dq, dk, dwq, dwk = _qk_norm_bwd(f"d_fox_qk_norm_{i}", proj, sv_['wq'], sv_['wk'], dqn, dkn, H)
            gsmall['fox_q_norm'][j], gsmall['fox_k_norm'][j] = dwq.reshape(-1), dwk.reshape(-1)
            dcum = dcq + jnp.pad(jnp.transpose(dck.reshape(H, T)), ((0, 0), (0, LANES - H)))
            dls = _cum_rows(f"d_fox_cum_{i}", dcum, jnp.zeros((1, LANES), F32), True, False)
            dflog, dfb = _rows(f"d_fox_gate_{i}", lambda d, f, b: ((lambda r: (r, _colsum(r)))(d * _sigmoid(-(f + b)))),
                               [(dls, 'r', LANES, 0), (sv_['flog'], 'r', LANES, 0), (sv_['fb'], 'b', LANES, 0)],
                               [('r', LANES, BF16), ('a', LANES, F32)], 256)
            gsmall['fox_f_bias'][j] = dfb[0, :H]
            dproj = jnp.concatenate([dq, dk, dv, dz], axis=1)
            tkT = _t(K_STEP, T)
            dw_qkvz = _mm(f"d_fox_in_proj_{i}", sv_['xn'], dproj, M=D, N=4 * E, K=T, tm=_t(512, D), tn=_t(1024, 4 * E), tk=tkT, ta=True,
                          a_spec=_bs((tkT, _t(512, D)), lambda g, m, n, k: (k, m)),
                          b_spec=_bs((tkT, _t(1024, 4 * E)), lambda g, m, n, k: (k, n)),
                          outs=[((D, 4 * E), BF16, _tile(_t(512, D), _t(1024, 4 * E)))])[0]
            dw_f = _mm(f"d_fox_gate_proj_{i}", sv_['xn'], dflog, M=D, N=LANES, K=T, tm=_t(512, D), tn=LANES, tk=tkT, ta=True,
                       a_spec=_bs((tkT, _t(512, D)), lambda g, m, n, k: (k, m)),
                       b_spec=_bs((tkT, LANES), lambda g, m, n, k: (k, n)),
                       outs=[((D, LANES), BF16, _tile(_t(512, D), LANES))])[0]
            dw_fox = jnp.concatenate([dw_qkvz, dw_f[:, :H]], axis=1)
            sw = dw_fox.shape[1] // N_CHIPS
            layer_parts.append(('fox_in_proj', j, jnp.transpose(dw_fox.reshape(2, D // 2, N_CHIPS, sw), (0, 2, 1, 3))))
            w_qkvz, w_f = sv_['w_qkvz'], sv_['w_f']
            dxn_f = _mm(f"d_fox_xn_gate_{i}", dflog, w_f, M=T, N=D, K=LANES, tm=_t(512, T), tn=_t(1024, D), tk=LANES, tb=True,
                        a_spec=_bs((_t(512, T), LANES), lambda g, m, n, k: (m, k)),
                        b_spec=_bs((_t(1024, D), LANES), lambda g, m, n, k: (n, k)),
                        outs=[((T, D), F32, _tile(_t(512, T), _t(1024, D)))])[0]
            tm, tn, tk = _t(512, T), _t(1024, D), _t(K_STEP, 4 * E)
            dxn = _mm(f"d_fox_xn_{i}", dproj, w_qkvz, M=T, N=D, K=4 * E, tm=tm, tn=tn, tk=tk, tb=True,
                      a_spec=_bs((tm, tk), lambda g, m, n, k: (m, k)), b_spec=_bs((tn, tk), lambda g, m, n, k: (n, k)),
                      extras=[(dxn_f, _tile(tm, tn))], epi=lambda acc, e: (acc + e,),
                      outs=[((T, D), F32, _tile(tm, tn))])[0]
        else:
            proj, mixed, scale = sv_['proj'], sv_['mixed'], sv_['scale']
            nm = T // _t(512, T)

            def pool_epi(da, mx, sc, z):
                dy = da * _silu(z)
                return (da * (mx * sc)) * _dsilu(z), dy * sc, _colsum(dy * mx)

            dz, dmix, dsc = _mm_rowsharded_t(
                f"d_pool_act_{i}", dh16, w_out, epi=pool_epi, deps=after_start,
                extras=lambda tm, tn: [(mixed, _tile(tm, tn)), (scale, _rowvec(tn)), (proj, _tile(tm, tn, E // tn))],
                outs_fn=lambda tm, tn: [((T, E), BF16, _tile(tm, tn)), ((T, E), BF16, _tile(tm, tn)),
                                        ((nm, 1, E), F32, _bs((None, 1, tn), lambda g, m, n, k: (m, 0, n)))])
            gsmall['pool_scale'][j] = jnp.sum(dsc, axis=(0, 1))
            w_pg = sv_['w_pg']
            tkw = w_pg.shape[2]
            tk = _t(K_STEP, T)
            layer_parts.append(('pool_w_group', j, _mm(
                f"d_pool_w_group_{i}", sv_['pm'], dmix, M=PD, N=PD, K=T, tm=tkw, tn=PD, tk=tk, groups=PG, ta=True,
                a_spec=_bs((tk, tkw), lambda g, m, n, k: (k, g * (PD // tkw) + m)),
                b_spec=_bs((tk, PD), lambda g, m, n, k: (k, g)),
                outs=[((2, N_CHIPS, PG // 2, tkw, PD), BF16, _bs((None, None, None, tkw, PD), lambda g, m, n, k: (g // (PG // 2), m, g % (PG // 2), 0, 0)))])[0]))
            tm, tk2 = _t(512, T), _t(512, PD)
            dpm = _mm(f"d_pool_mix_{i}", dmix, w_pg, M=T, N=PD, K=PD, tm=tm, tn=tkw, tk=tk2, groups=PG, tb=True,
                      a_spec=_bs((tm, tk2), lambda g, m, n, k: (m, g * (PD // tk2) + k)),
                      b_spec=_bs((None, None, tkw, tk2), lambda g, m, n, k: (n, g, 0, k)),
                      outs=[((T, E), F32, _bs((tm, tkw), lambda g, m, n, k: (m, g * (PD // tkw) + n)))])[0]
            du = _pool_bwd(f"d_pool_win_{i}", dpm, E)
            dproj = jnp.concatenate([du, dz], axis=1)
            layer_parts.append(('pool_in_proj', j, _mm_dw_cols(f"d_pool_in_proj_{i}", sv_['xn'], dproj)))
            dxn = _mm_colsharded_t(f"d_pool_xn_{i}", dproj, W[('pool_in_proj', j)])
        dh, dh16, dnw = _norm_bwd(f"d_norm_{i}", dxn, sv_['h'], nw, dh)
        gsmall['norm_w'][i] = dnw.reshape(D)
        if pending is not None:
            _reduce_end(pending[0], pending[1], dh16, pending[2], rs_bufs, rs_shapes)
        if i > 0:
            pending, token = reduce_layer(f"l{i}", layer_parts)
    grad_x = dh.reshape(x.shape)

    small_flat = jnp.concatenate([jnp.stack(gsmall[n]).reshape(-1) for n in SMALL])
    n_small = small_flat.shape[0]
    unit = 2 * N_CHIPS * 16 * LANES
    n_pad = -(-n_small // unit) * unit
    R = n_pad // (2 * N_CHIPS * LANES)
    small_part = jnp.pad(small_flat, (0, n_pad - n_small)).astype(BF16).reshape(2, N_CHIPS, R, LANES)
    pending, token = reduce_layer("l0", layer_parts + [('small', 'chip', small_part)])
    _reduce_end(early[0], early[1], token, early[2], rs_bufs, rs_shapes)
    nb = len(BIG)
    done_items = [d for d in rs_dests_all if d not in pending[2]]
    rs_bufs[:nb] = _pair_share("rs_pair_share_a", rs_bufs[:nb], done_items, deps=[token])
    late = [o for o, _ in pending[2]]
    delta, new_m, new_v = {}, {}, {}
    grads = {}
    last = token
    for o, n in enumerate(BIG):
        if o not in late:
            grads[n] = rs_bufs[o].reshape(w[n].shape)
            last = _adamw_big(n, w, grads, mom_m, mom_v, delta, new_m, new_v)
    _reduce_end(pending[0], pending[1], last, pending[2], rs_bufs, rs_shapes)
    shared = _pair_share("rs_pair_share_b", [rs_bufs[o] for o in late], [(k, l) for k, (_, l) in enumerate(pending[2])])
    for k, o in enumerate(late):
        rs_bufs[o] = shared[k]
        if o < nb:
            grads[BIG[o]] = shared[k].reshape(w[BIG[o]].shape)
            _adamw_big(BIG[o], w, grads, mom_m, mom_v, delta, new_m, new_v)
    small_all = _chip_allgather("gather_small_grads", [rs_bufs[nb]])[0]
    small_all = jnp.transpose(small_all, (1, 0, 2, 3)).reshape(-1)[:n_small]
    off = 0
    p = 2 * lax.axis_index("x") + lax.axis_index("y")
    for n in SMALL:
        full_shape = (w[n].shape[0], E) if n in SMALL_SHARDED else w[n].shape
        size = math.prod(full_shape)
        gfull = small_all[off:off + size].reshape(full_shape)
        off += size
        if n in SMALL_SHARDED:
            gfull = lax.dynamic_slice_in_dim(gfull, p * (E // N_CHIPS), E // N_CHIPS, axis=1)
        grads[n] = gfull

    for n in SMALL:
        shape = w[n].shape
        if n in GROUP_AXIS_1:
            perm = (0,) + tuple(range(2, len(shape))) + (1,)
            inv = (0, len(shape) - 1) + tuple(range(1, len(shape) - 1))
            view = lambda a: jnp.transpose(a, perm).reshape(-1, shape[1])
            back = lambda a: jnp.transpose(a.reshape(tuple(shape[k] for k in perm)), inv)
        else:
            view = lambda a: a.reshape(-1, shape[-1])
            back = lambda a: a.reshape(shape)
        d_, m_, v_ = _adamw(f"adamw_{n}", view(w[n]), view(grads[n]), view(mom_m[n]), view(mom_v[n]))
        delta[n], new_m[n], new_v[n] = back(d_), back(m_), back(v_)
    return (loss, grad_x, *[grads[n] for n in ORDER], *[delta[n] for n in ORDER], *[new_m[n] for n in ORDER], *[new_v[n] for n in ORDER])
```

```python
import functools
import math

import jax
import jax.numpy as jnp
from jax import lax
from jax.experimental import pallas as pl
from jax.experimental.pallas import tpu as pltpu

F32 = jnp.float32
BF16 = jnp.bfloat16
MESH = pl.DeviceIdType.MESH

N_CHIPS = 4
VMEM_LIMIT = 56 * 1024 * 1024
LANES = 128
SUB = 8

EPS = 1e-6
S5_GROUP = 16
S5_STATE = 64
GROUPS_PER_CHUNK = 16
FOX_HEAD_DIM = 128
ATTN_SUB = 256
ATTN_HEADS = 2
POOL_WINDOWS = (2, 4, 8, 16)
POOL_HALO = 16
ADAM_LR, ADAM_B1, ADAM_B2, ADAM_EPS, ADAM_WD, ADAM_STEP = 0.001, 0.9, 0.999, 1e-08, 0.01, 10
NEG = -1e30
K_STEP = 2048


ANY = pl.BlockSpec(memory_space=pl.ANY)


def _t(pref, dim):
    if dim <= pref:
        return dim
    t = pref - pref % 16
    while t > 16 and dim % t:
        t -= 16
    assert dim % t == 0, (pref, dim)
    return t


def _params(sem):
    return pltpu.CompilerParams(dimension_semantics=sem, vmem_limit_bytes=VMEM_LIMIT)


def _sigmoid(x):
    return 1.0 / (1.0 + jnp.exp(-x))


def _silu(z):
    return z * _sigmoid(z)


def _dsilu(z):
    s = _sigmoid(z)
    return s * (1.0 + z * (1.0 - s))


_GELU_C = math.sqrt(2.0 / math.pi)


def _gelu(x):
    return 0.5 * x * (1.0 + jnp.tanh(_GELU_C * (x + 0.044715 * (x * x * x))))


def _dgelu(x):
    t = jnp.tanh(_GELU_C * (x + 0.044715 * (x * x * x)))
    return 0.5 * (1.0 + t) + 0.5 * x * (1.0 - t * t) * (_GELU_C * (1.0 + 3.0 * 0.044715 * x * x))


def _log_sigmoid(x):
    return jnp.minimum(x, 0.0) - jnp.log(1.0 + jnp.exp(-jnp.abs(x)))


def _rms(x):
    return lax.rsqrt(jnp.mean(x * x, axis=-1, keepdims=True) + EPS)


def _rms_bwd(x, w, dy):
    r = _rms(x)
    xhat = x * r
    dxh = dy * w
    dx = r * (dxh - xhat * jnp.mean(dxh * xhat, axis=-1, keepdims=True))
    return dx, dy * xhat


def _rows(name, fn, ins, outs, tr, pre=None, into=None, deps=()):
    rows = None
    for arr, kind, cols, cb in ins:
        if kind == 'r':
            rows = arr.shape[0]
        elif kind == 's' and rows is None:
            rows = arr.shape[1]
    tr = _t(tr, rows)
    n_in = len(ins)
    has_acc = any(o[0] == 'a' for o in outs)

    def spec(kind, cols, cb):
        if kind == 'r':
            return pl.BlockSpec((tr, cols), lambda r, *p: (r, cb))
        if kind == 'b':
            return pl.BlockSpec((1, cols), lambda r, *p: (0, cb))
        return pl.BlockSpec((None, tr, cols), lambda r, p: (p[cb], r, 0))

    in_specs = [spec(kind, cols, cb) for _, kind, cols, cb in ins]
    out_specs, out_shape = [], []
    for o in outs:
        if o[0] == 'r':
            out_specs.append(pl.BlockSpec((tr, o[1]), lambda r, *p: (r, 0)))
            out_shape.append(jax.ShapeDtypeStruct((rows, o[1]), o[2]))
        elif o[0] == 'a':
            out_specs.append(pl.BlockSpec((1, o[1]), lambda r, *p: (0, 0)))
            out_shape.append(jax.ShapeDtypeStruct((1, o[1]), o[2]))
        else:
            blk = tuple(tr if d == 'tr' else d for d in o[3])
            out_specs.append(pl.BlockSpec(blk, o[4]))
            out_shape.append(jax.ShapeDtypeStruct(o[1], o[2]))
    n_pre = 0 if pre is None else 1
    args = [a[0] for a in ins]
    aliases = {}
    if into is not None:
        in_specs.append(ANY)
        args.append(into)
        aliases = {n_pre + n_in: 0}
    in_specs += [ANY] * len(deps)
    args += list(deps)
    n_all = len(args)

    def body(*refs):
        refs = refs[n_pre:]
        res = fn(*[r[...] for r in refs[:n_in]])
        for spec_o, o, v in zip(outs, refs[n_all:], res):
            if spec_o[0] == 'a':
                @pl.when(pl.program_id(0) == 0)
                def _():
                    o[...] = jnp.zeros_like(o)
                o[...] += v.astype(o.dtype)
            else:
                o[...] = v.astype(o.dtype)

    grid_spec = pltpu.PrefetchScalarGridSpec(num_scalar_prefetch=n_pre, grid=(rows // tr,), in_specs=in_specs, out_specs=out_specs)
    if pre is not None:
        args = [pre] + args
    return pl.pallas_call(body, name=name, grid_spec=grid_spec, out_shape=out_shape, input_output_aliases=aliases,
                          compiler_params=_params(("arbitrary" if has_acc else "parallel",)))(*args)


def _colsum(v):
    return jnp.sum(v, axis=0, keepdims=True)


def _mm(name, a, b, *, M, N, K, tm, tn, tk, a_spec, b_spec, outs, epi=None, extras=(), groups=1, ta=False, tb=False, deps=()):
    nk = K // tk
    assert M % tm == 0 and N % tn == 0 and K % tk == 0, (name, M, N, K, tm, tn, tk)
    dims = (((0 if ta else 1,), (1 if tb else 0,)), ((), ()))
    n_ex = len(extras)

    def body(*refs):
        a_ref, b_ref = refs[0], refs[1]
        ex = refs[2:2 + n_ex]
        out_refs = refs[2 + n_ex + len(deps):2 + n_ex + len(deps) + len(outs)]

        def finish(r):
            res = (r,) if epi is None else epi(r, *[e[...] for e in ex])
            for o, v in zip(out_refs, res):
                o[...] = v.astype(o.dtype)

        part = lax.dot_general(a_ref[...].astype(BF16), b_ref[...].astype(BF16), dims, preferred_element_type=F32)
        if nk == 1:
            finish(part)
            return
        acc = refs[-1]
        k = pl.program_id(3)

        @pl.when(k == 0)
        def _():
            acc[...] = part

        @pl.when(k > 0)
        def _():
            acc[...] += part

        @pl.when(k == nk - 1)
        def _():
            finish(acc[...])

    return pl.pallas_call(
        body, name=name, grid=(groups, M // tm, N // tn, nk),
        in_specs=[a_spec, b_spec] + [s for _, s in extras] + [ANY] * len(deps),
        out_specs=[s for _, _, s in outs],
        out_shape=[jax.ShapeDtypeStruct(sh, dt) for sh, dt, _ in outs],
        scratch_shapes=[] if nk == 1 else [pltpu.VMEM((tm, tn), F32)],
        compiler_params=_params(("parallel", "parallel", "parallel", "arbitrary")),
    )(a, b, *[e for e, _ in extras], *deps)


def _bs(shape, f):
    return pl.BlockSpec(shape, f)


def _tile(tm, tn, coff=0):
    return _bs((tm, tn), lambda g, m, n, k: (m, n + coff))


def _rowvec(tn, coff=0):
    return _bs((1, tn), lambda g, m, n, k: (0, n + coff))


def _mm_proj(name, xn, w, *, epi=None, extras=(), out_dtype=F32):
    T, D = xn.shape
    sw = w.shape[2]
    N = N_CHIPS * sw
    tm, tn, tk = _t(512, T), _t(1024, sw), _t(K_STEP, D)
    nb = sw // tn
    return _mm(name, xn, w, M=T, N=N, K=D, tm=tm, tn=tn, tk=tk,
               a_spec=_bs((tm, tk), lambda g, m, n, k: (m, k)),
               b_spec=_bs((None, tk, tn), lambda g, m, n, k: (n // nb, k, n % nb)),
               outs=[((T, N), out_dtype, _tile(tm, tn))], epi=epi, extras=extras)[0]


def _mm_plain(name, a, b, *, out_dtype=F32, epi=None, extras=(), outs=None, tn_pref=1024):
    M, K = a.shape
    N = b.shape[1]
    tm, tn, tk = _t(512, M), _t(tn_pref, N), _t(K_STEP, K)
    if outs is None:
        outs = [((M, N), out_dtype, _tile(tm, tn))]
    return _mm(name, a, b, M=M, N=N, K=K, tm=tm, tn=tn, tk=tk,
               a_spec=_bs((tm, tk), lambda g, m, n, k: (m, k)),
               b_spec=_bs((tk, tn), lambda g, m, n, k: (k, n)),
               outs=outs, epi=epi, extras=extras)


def _mm_rowsharded(name, a, w, *, epi, extras, outs_fn, deps=()):
    T, E = a.shape
    N = w.shape[2]
    tm, tn, tk = _t(512, T), _t(1024, N), _t(K_STEP, E)
    return _mm(name, a, w.reshape(E, N), M=T, N=N, K=E, tm=tm, tn=tn, tk=tk, deps=deps,
               a_spec=_bs((tm, tk), lambda g, m, n, k: (m, k)),
               b_spec=_bs((tk, tn), lambda g, m, n, k: (k, n)),
               outs=outs_fn(tm, tn), epi=epi, extras=extras(tm, tn))


def _mm_rowsharded_t(name, d, w, *, epi, extras, outs_fn, deps=()):
    T, N = d.shape
    tn = w.shape[1]
    E = N_CHIPS * tn
    tm, tk = _t(512, T), _t(K_STEP, N)
    return _mm(name, d, w, M=T, N=E, K=N, tm=tm, tn=tn, tk=tk, tb=True, deps=deps,
               a_spec=_bs((tm, tk), lambda g, m, n, k: (m, k)),
               b_spec=_bs((None, tn, tk), lambda g, m, n, k: (n, 0, k)),
               outs=outs_fn(tm, tn), epi=epi, extras=extras(tm, tn))


def _mm_colsharded_t(name, d, w):
    T, N = d.shape
    D, sw = w.shape[1], w.shape[2]
    tm, tn, tk = _t(512, T), _t(1024, D), _t(1024, sw)
    kb = sw // tk
    return _mm(name, d, w, M=T, N=D, K=N, tm=tm, tn=tn, tk=tk, tb=True,
               a_spec=_bs((tm, tk), lambda g, m, n, k: (m, k)),
               b_spec=_bs((None, tn, tk), lambda g, m, n, k: (k // kb, n, k % kb)),
               outs=[((T, D), F32, _tile(tm, tn))])[0]


def _mm_dw_rows(name, a, d, deps=()):
    T, E = a.shape
    N = d.shape[1]
    tm, tn, tk = E // (2 * N_CHIPS), _t(2048, N), _t(K_STEP, T)
    return _mm(name, a, d, M=E, N=N, K=T, tm=tm, tn=tn, tk=tk, ta=True, deps=deps,
               a_spec=_bs((tk, tm), lambda g, m, n, k: (k, m)),
               b_spec=_bs((tk, tn), lambda g, m, n, k: (k, n)),
               outs=[((2, N_CHIPS, tm, N), BF16, _bs((None, None, tm, tn), lambda g, m, n, k: (m % 2, m // 2, 0, n)))])[0]


def _mm_dw_cols(name, xn, d):
    T, D = xn.shape
    N = d.shape[1]
    sw = N // N_CHIPS
    tm, tn, tk = _t(512, D // 2), _t(1024, sw), _t(K_STEP, T)
    mh, nb = (D // 2) // tm, sw // tn
    return _mm(name, xn, d, M=D, N=N, K=T, tm=tm, tn=tn, tk=tk, ta=True,
               a_spec=_bs((tk, tm), lambda g, m, n, k: (k, m)),
               b_spec=_bs((tk, tn), lambda g, m, n, k: (k, n)),
               outs=[((2, N_CHIPS, D // 2, sw), BF16,
                      _bs((None, None, tm, tn), lambda g, m, n, k: (m // mh, n // nb, m % mh, n % nb)))])[0]


def _norm_fwd(name, h, w, deps=()):
    D = h.shape[1]
    return _rows(name, lambda x, g: ((x * _rms(x)) * g,), [(h, 'r', D, 0), (w, 'b', D, 0)], [('r', D, BF16)], 256, deps=deps)[0]


def _norm_bwd(name, dxn, h, w, dh):
    D = h.shape[1]

    def fn(dy, x, g, up):
        dx, dwt = _rms_bwd(x, g, dy)
        r = up + dx
        return r, r, _colsum(dwt)

    return _rows(name, fn, [(dxn, 'r', D, 0), (h, 'r', D, 0), (w, 'b', D, 0), (dh, 'r', D, 0)],
                 [('r', D, F32), ('r', D, BF16), ('a', D, F32)], 256)


def _loss(h, target):
    D = h.shape[1]

    def fn(y, t):
        e = y - t
        d = e * (1.0 / D)
        return d, d, _colsum(e * e) * (0.5 / D)

    return _rows("loss", fn, [(h, 'r', D, 0), (target, 'r', D, 0)], [('r', D, F32), ('r', D, BF16), ('a', D, F32)], 256)


def _adamw(name, w, g, m, v):
    cols = w.shape[1]

    def fn(w, g, m, v):
        m = ADAM_B1 * m + (1.0 - ADAM_B1) * g
        v = ADAM_B2 * v + (1.0 - ADAM_B2) * (g * g)
        m_hat = m / (1.0 - ADAM_B1 ** ADAM_STEP)
        v_hat = v / (1.0 - ADAM_B2 ** ADAM_STEP)
        delta = -ADAM_LR * (m_hat / (jnp.sqrt(v_hat) + ADAM_EPS) + ADAM_WD * w)
        return delta, m, v

    rows = w.shape[0]
    if rows % SUB == 0 or rows <= 256:
        return _rows(name, fn, [(x, 'r', cols, 0) for x in (w, g, m, v)], [('r', cols, F32)] * 3, 256)
    tc = _t(256, cols)
    assert tc % LANES == 0, (rows, cols)

    def body(w_ref, g_ref, m_ref, v_ref, d_out, m_out, v_out):
        for o, r in zip((d_out, m_out, v_out), fn(w_ref[...], g_ref[...], m_ref[...], v_ref[...])):
            o[...] = r

    blk = pl.BlockSpec((rows, tc), lambda j: (0, j))
    return pl.pallas_call(body, name=name, grid=(cols // tc,), in_specs=[blk] * 4, out_specs=[blk] * 3,
                          out_shape=[jax.ShapeDtypeStruct((rows, cols), F32)] * 3, compiler_params=_params(("parallel",)))(w, g, m, v)


def _s5_disc(a_re, a_im, log_dt):
    dt = jnp.exp(log_dt)
    mag = jnp.exp(a_re * dt)
    abar_r = mag * jnp.cos(a_im * dt)
    abar_i = mag * jnp.sin(a_im * dt)
    den = a_re * a_re + a_im * a_im
    xr = abar_r - 1.0
    fr = (xr * a_re + abar_i * a_im) / den
    fi = (abar_i * a_re - xr * a_im) / den
    return abar_r, abar_i, fr, fi


def _s5_disc_fwd(name, a_re, a_im, log_dt):
    G, P = a_re.shape

    def body(ar, ai, ld, o0, o1, o2, o3):
        for o, v in zip((o0, o1, o2, o3), _s5_disc(ar[...], ai[...], ld[...])):
            o[...] = v

    return pl.pallas_call(body, name=name, out_shape=[jax.ShapeDtypeStruct((G, P), F32)] * 4)(a_re, a_im, log_dt)


def _s5_disc_bwd(name, a_re, a_im, log_dt, cts):
    G, P = a_re.shape

    def body(ar, ai, ld, c0, c1, c2, c3, d0, d1, d2):
        _, vjp = jax.vjp(_s5_disc, ar[...], ai[...], ld[...])
        g0, g1, g2 = vjp((c0[...], c1[...], c2[...], c3[...]))
        d0[...] = g0
        d1[...] = g1
        d2[...] = g2

    return pl.pallas_call(body, name=name, out_shape=[jax.ShapeDtypeStruct((G, P), F32)] * 2 + [jax.ShapeDtypeStruct((G, 1), F32)])(
        a_re, a_im, log_dt, *cts)


def _s5_bbar(name, fr, fi, br, bi):
    return _rows(name, lambda fr, fi, br, bi: (fr * br - fi * bi, fr * bi + fi * br),
                 [(fr, 'r', 1, 0), (fi, 'r', 1, 0), (br, 'r', S5_GROUP, 0), (bi, 'r', S5_GROUP, 0)],
                 [('r', S5_GROUP, F32)] * 2, 2048)


def _s5_bbar_bwd(name, fr, fi, br, bi, dr, di):
    def fn(fr, fi, br, bi, dr, di):
        return (fr * dr + fi * di, fr * di - fi * dr,
                jnp.sum(br * dr + bi * di, axis=1, keepdims=True), jnp.sum(br * di - bi * dr, axis=1, keepdims=True))

    return _rows(name, fn, [(fr, 'r', 1, 0), (fi, 'r', 1, 0)] + [(x, 'r', S5_GROUP, 0) for x in (br, bi, dr, di)],
                 [('r', S5_GROUP, F32)] * 2 + [('r', 1, F32)] * 2, 2048)


def _scan_mults(m_ref, ar, ai, reverse):
    L = ar.shape[1]
    row = lax.broadcasted_iota(jnp.int32, (SUB, L), 0)
    if reverse:
        row = (SUB - 1) - row
    ar = jnp.broadcast_to(ar, (SUB, L))
    ai = jnp.broadcast_to(ai, (SUB, L))
    a2r, a2i = ar * ar - ai * ai, 2.0 * ar * ai
    a4r, a4i = a2r * a2r - a2i * a2i, 2.0 * a2r * a2i
    zero = jnp.zeros((SUB, L), F32)
    for s, (pr, pi, d) in enumerate(((ar, ai, 1), (a2r, a2i, 2), (a4r, a4i, 4))):
        m_ref[2 * s] = jnp.where(row >= d, pr, zero)
        m_ref[2 * s + 1] = jnp.where(row >= d, pi, zero)
    pr, pi = ar, ai
    for bit, (qr, qi) in ((1, (ar, ai)), (2, (a2r, a2i)), (4, (a4r, a4i))):
        on = (row & bit) != 0
        nr, ni = pr * qr - pi * qi, pr * qi + pi * qr
        pr, pi = jnp.where(on, nr, pr), jnp.where(on, ni, pi)
    m_ref[6] = pr
    m_ref[7] = pi


def _scan8(xr, xi, m_ref, cr, ci, reverse):
    for s, d in enumerate((1, 2, 4)):
        sh = (SUB - d) if reverse else d
        sr, si = pltpu.roll(xr, sh, 0), pltpu.roll(xi, sh, 0)
        mr, mi = m_ref[2 * s], m_ref[2 * s + 1]
        xr, xi = xr + mr * sr - mi * si, xi + mr * si + mi * sr
    pr, pi = m_ref[6], m_ref[7]
    return xr + pr * cr - pi * ci, xi + pr * ci + pi * cr


def _blockdiag_fill(bd_ref, c_ref, C, L):
    P = S5_STATE
    bd_ref[...] = jnp.zeros_like(bd_ref)
    for g in range(L // P):
        for half in (0, L):
            bd_ref[g * C:(g + 1) * C, half + g * P:half + (g + 1) * P] = c_ref[:, half + g * P:half + (g + 1) * P]


def _blockdiag_take(out_ref, dense_ref, C, L):
    P = S5_STATE
    for g in range(L // P):
        for half in (0, L):
            out_ref[:, half + g * P:half + (g + 1) * P] = dense_ref[g * C:(g + 1) * C, half + g * P:half + (g + 1) * P]


def _s5_fwd(name, proj, bbd, cbd, abar_r, abar_i, dskip, E):
    T = proj.shape[0]
    NC, C, L2 = bbd.shape
    L = L2 // 2
    CH = GROUPS_PER_CHUNK * C
    tT = _t(256, T)
    nt = (((1,), (1,)), ((), ()))

    def body(u_ref, bc_ref, cc_ref, ar_ref, ai_ref, d_ref, y_ref, g_ref, h_ref, bu, carry, mult, b_bd, c_bd):
        tb = pl.program_id(1)

        @pl.when(tb == 0)
        def _():
            carry[...] = jnp.zeros_like(carry)
            _blockdiag_fill(b_bd, bc_ref, C, L)
            _blockdiag_fill(c_bd, cc_ref, C, L)

        u = u_ref[...]
        bu[...] = jnp.dot(u.astype(BF16), b_bd[...], preferred_element_type=F32)
        _scan_mults(mult, ar_ref[...], ai_ref[...], False)

        def step(jb, c):
            cr, ci = c
            r0 = pl.multiple_of(jb * SUB, SUB)
            hr, hi = _scan8(bu[pl.ds(r0, SUB), 0:L], bu[pl.ds(r0, SUB), L:L2], mult, cr, ci, False)
            h_ref[pl.ds(r0, SUB), 0:L] = hr
            h_ref[pl.ds(r0, SUB), L:L2] = hi
            return (jnp.broadcast_to(hr[SUB - 1:SUB, :], (SUB, L)), jnp.broadcast_to(hi[SUB - 1:SUB, :], (SUB, L)))

        cr, ci = lax.fori_loop(0, tT // SUB, step, (carry[:, 0:L], carry[:, L:L2]))
        carry[:, 0:L] = cr
        carry[:, L:L2] = ci
        y1 = lax.dot_general(h_ref[...].astype(BF16), c_bd[...], nt, preferred_element_type=F32) + d_ref[...] * u
        y_ref[...] = y1
        g_ref[...] = _gelu(y1).astype(BF16)

    return pl.pallas_call(
        body, name=name, grid=(NC, T // tT),
        in_specs=[_bs((tT, CH), lambda c, t: (t, c)), _bs((None, C, L2), lambda c, t: (c, 0, 0)),
                  _bs((None, C, L2), lambda c, t: (c, 0, 0)), _bs((None, 1, L), lambda c, t: (c, 0, 0)),
                  _bs((None, 1, L), lambda c, t: (c, 0, 0)), _bs((1, CH), lambda c, t: (0, c))],
        out_specs=[_bs((tT, CH), lambda c, t: (t, c)), _bs((tT, CH), lambda c, t: (t, c)),
                   _bs((None, tT, L2), lambda c, t: (c, t, 0))],
        out_shape=[jax.ShapeDtypeStruct((T, E), F32), jax.ShapeDtypeStruct((T, E), BF16),
                   jax.ShapeDtypeStruct((NC, T, L2), F32)],
        scratch_shapes=[pltpu.VMEM((tT, L2), F32), pltpu.VMEM((SUB, L2), F32), pltpu.VMEM((8, SUB, L), F32),
                        pltpu.VMEM((CH, L2), BF16), pltpu.VMEM((CH, L2), BF16)],
        compiler_params=_params(("parallel", "arbitrary")),
    )(proj, bbd, cbd, abar_r, abar_i, dskip)


def _s5_bwd(name, dy1, proj, hs, bbd, cbd, abar_r, abar_i, dskip, E):
    T = proj.shape[0]
    NC, C, L2 = bbd.shape
    L = L2 // 2
    CH = GROUPS_PER_CHUNK * C
    tT = _t(256, T)
    nT = T // tT
    tn = (((0,), (0,)), ((), ()))
    nt = (((1,), (1,)), ((), ()))

    def body(dy_ref, u_ref, h_ref, bc_ref, cc_ref, ar_ref, ai_ref, d_ref, du_ref, db_ref, dc_ref, da_ref, dd_ref,
             gb, carry, mult, b_bd, c_bd, db_acc, dc_acc):
        tb = pl.program_id(1)

        @pl.when(tb == 0)
        def _():
            carry[...] = jnp.zeros_like(carry)
            db_acc[...] = jnp.zeros_like(db_acc)
            dc_acc[...] = jnp.zeros_like(dc_acc)
            da_ref[...] = jnp.zeros_like(da_ref)
            dd_ref[...] = jnp.zeros_like(dd_ref)
            _blockdiag_fill(b_bd, bc_ref, C, L)
            _blockdiag_fill(c_bd, cc_ref, C, L)

        dy = dy_ref[...]
        u = u_ref[...]
        dy16 = dy.astype(BF16)
        dc_acc[...] += lax.dot_general(dy16, h_ref[...].astype(BF16), tn, preferred_element_type=F32)
        gb[...] = jnp.dot(dy16, c_bd[...], preferred_element_type=F32)
        _scan_mults(mult, ar_ref[...], -ai_ref[...], True)
        row = lax.broadcasted_iota(jnp.int32, (SUB, L), 0)
        nblk = tT // SUB

        def step(jj, c):
            cr, ci, sr, si = c
            r0 = pl.multiple_of((nblk - 1 - jj) * SUB, SUB)
            gr, gi = _scan8(gb[pl.ds(r0, SUB), 0:L], gb[pl.ds(r0, SUB), L:L2], mult, cr, ci, True)
            gb[pl.ds(r0, SUB), 0:L] = gr
            gb[pl.ds(r0, SUB), L:L2] = gi
            nr = jnp.where(row == SUB - 1, cr, pltpu.roll(gr, SUB - 1, 0))
            ni = jnp.where(row == SUB - 1, ci, pltpu.roll(gi, SUB - 1, 0))
            hr, hi = h_ref[pl.ds(r0, SUB), 0:L], h_ref[pl.ds(r0, SUB), L:L2]
            sr = sr + nr * hr + ni * hi
            si = si + ni * hr - nr * hi
            return (jnp.broadcast_to(gr[0:1, :], (SUB, L)), jnp.broadcast_to(gi[0:1, :], (SUB, L)), sr, si)

        z = jnp.zeros((SUB, L), F32)
        cr, ci, sr, si = lax.fori_loop(0, nblk, step, (carry[:, 0:L], carry[:, L:L2], z, z))
        carry[:, 0:L] = cr
        carry[:, L:L2] = ci
        da_ref[:, 0:L] += sr
        da_ref[:, L:L2] += si
        g16 = gb[...].astype(BF16)
        du = lax.dot_general(g16, b_bd[...], nt, preferred_element_type=F32) + d_ref[...] * dy
        du_ref[...] = du.astype(BF16)
        db_acc[...] += lax.dot_general(u.astype(BF16), g16, tn, preferred_element_type=F32)
        dd_ref[...] += _colsum(dy * u)

        @pl.when(tb == nT - 1)
        def _():
            _blockdiag_take(db_ref, db_acc, C, L)
            _blockdiag_take(dc_ref, dc_acc, C, L)

    rev = lambda c, t: (nT - 1 - t, c)
    return pl.pallas_call(
        body, name=name, grid=(NC, nT),
        in_specs=[_bs((tT, CH), rev), _bs((tT, CH), rev), _bs((None, tT, L2), lambda c, t: (c, nT - 1 - t, 0)),
                  _bs((None, C, L2), lambda c, t: (c, 0, 0)), _bs((None, C, L2), lambda c, t: (c, 0, 0)),
                  _bs((None, 1, L), lambda c, t: (c, 0, 0)), _bs((None, 1, L), lambda c, t: (c, 0, 0)),
                  _bs((1, CH), lambda c, t: (0, c))],
        out_specs=[_bs((tT, CH), rev), _bs((None, C, L2), lambda c, t: (c, 0, 0)), _bs((None, C, L2), lambda c, t: (c, 0, 0)),
                   _bs((None, SUB, L2), lambda c, t: (c, 0, 0)), _bs((None, 1, CH), lambda c, t: (c, 0, 0))],
        out_shape=[jax.ShapeDtypeStruct((T, E), BF16), jax.ShapeDtypeStruct((NC, C, L2), F32),
                   jax.ShapeDtypeStruct((NC, C, L2), F32), jax.ShapeDtypeStruct((NC, SUB, L2), F32),
                   jax.ShapeDtypeStruct((NC, 1, CH), F32)],
        scratch_shapes=[pltpu.VMEM((tT, L2), F32), pltpu.VMEM((SUB, L2), F32), pltpu.VMEM((8, SUB, L), F32),
                        pltpu.VMEM((CH, L2), BF16), pltpu.VMEM((CH, L2), BF16), pltpu.VMEM((CH, L2), F32), pltpu.VMEM((CH, L2), F32)],
        compiler_params=_params(("parallel", "arbitrary")),
    )(dy1, proj, hs, bbd, cbd, abar_r, abar_i, dskip)


def _compact(v, NC):
    G, P, C = v.shape
    return jnp.transpose(v.reshape(NC, G // NC, P, C), (0, 3, 1, 2)).reshape(NC, C, (G // NC) * P)


def _uncompact(d, G):
    NC, C, L = d.shape
    gpc = G // NC
    return jnp.transpose(d.reshape(NC, C, gpc, L // gpc), (0, 2, 3, 1)).reshape(G, L // gpc, C)


def _cum_rows(name, x, bias, reverse, log_sig):
    T, L = x.shape

    def body(x_ref, b_ref, o_ref):
        row = lax.broadcasted_iota(jnp.int32, (SUB, L), 0)
        if reverse:
            row = (SUB - 1) - row
        nblk = T // SUB

        def step(jj, c):
            r0 = pl.multiple_of(((nblk - 1 - jj) if reverse else jj) * SUB, SUB)
            v = x_ref[pl.ds(r0, SUB), :] + b_ref[...]
            if log_sig:
                v = _log_sigmoid(v)
            for d in (1, 2, 4):
                v = v + jnp.where(row >= d, pltpu.roll(v, (SUB - d) if reverse else d, 0), 0.0)
            v = v + c
            o_ref[pl.ds(r0, SUB), :] = v
            e = 0 if reverse else SUB - 1
            return jnp.broadcast_to(v[e:e + 1, :], (SUB, L))

        lax.fori_loop(0, nblk, step, jnp.zeros((SUB, L), F32))

    return pl.pallas_call(body, name=name, out_shape=jax.ShapeDtypeStruct((T, L), F32),
                          compiler_params=pltpu.CompilerParams(vmem_limit_bytes=VMEM_LIMIT))(x, bias)


def _qk_norm(name, proj, wq, wk, H):
    T = proj.shape[0]
    Dh = FOX_HEAD_DIM
    tT = _t(512, T)

    def body(q_ref, k_ref, wq_ref, wk_ref, qn_ref, kn_ref):
        q, k = q_ref[...], k_ref[...]
        qn_ref[...] = ((q * _rms(q)) * wq_ref[...]).astype(BF16)
        kn_ref[...] = ((k * _rms(k)) * wk_ref[...]).astype(BF16)

    blk = lambda off: _bs((tT, Dh), lambda t, h: (t, h + off))
    return pl.pallas_call(
        body, name=name, grid=(T // tT, H),
        in_specs=[blk(0), blk(H), _bs((1, Dh), lambda t, h: (0, 0)), _bs((1, Dh), lambda t, h: (0, 0))],
        out_specs=[blk(0), blk(0)], out_shape=[jax.ShapeDtypeStruct((T, H * Dh), BF16)] * 2,
        compiler_params=_params(("parallel", "parallel")))(proj, proj, wq, wk)


def _qk_norm_bwd(name, proj, wq, wk, dqn, dkn, H):
    T = proj.shape[0]
    Dh = FOX_HEAD_DIM
    tT = _t(512, T)

    def body(q_ref, k_ref, wq_ref, wk_ref, dqn_ref, dkn_ref, dq_ref, dk_ref, dwq_ref, dwk_ref):
        @pl.when((pl.program_id(0) == 0) & (pl.program_id(1) == 0))
        def _():
            dwq_ref[...] = jnp.zeros_like(dwq_ref)
            dwk_ref[...] = jnp.zeros_like(dwk_ref)

        dq, tq = _rms_bwd(q_ref[...], wq_ref[...], dqn_ref[...])
        dk, tk = _rms_bwd(k_ref[...], wk_ref[...], dkn_ref[...])
        dq_ref[...] = dq.astype(BF16)
        dk_ref[...] = dk.astype(BF16)
        dwq_ref[...] += _colsum(tq)
        dwk_ref[...] += _colsum(tk)

    blk = lambda off: _bs((tT, Dh), lambda t, h: (t, h + off))
    one = _bs((1, Dh), lambda t, h: (0, 0))
    return pl.pallas_call(
        body, name=name, grid=(T // tT, H),
        in_specs=[blk(0), blk(H), one, one, blk(0), blk(0)],
        out_specs=[blk(0), blk(0), one, one],
        out_shape=[jax.ShapeDtypeStruct((T, H * Dh), BF16)] * 2 + [jax.ShapeDtypeStruct((1, Dh), F32)] * 2,
        compiler_params=_params(("arbitrary", "arbitrary")))(proj, proj, wq, wk, dqn, dkn)


def _attn_fwd(name, qn, kn, proj, cum_q, cum_k, H):
    T = qn.shape[0]
    Dh = FOX_HEAD_DIM
    tq = cum_k.shape[3]
    nq = T // tq
    scale = Dh ** -0.5
    nt = (((1,), (1,)), ((), ()))

    sq = _t(ATTN_SUB, tq)
    rep = tq // LANES
    HP = ATTN_HEADS
    assert H % HP == 0 and Dh == LANES

    def body(q_ref, k_ref, v_ref, cq_ref, ck_ref, o_ref, lse_ref, m_sc, l_sc, acc_sc):
        i = pl.program_id(1)
        m_sc[...] = jnp.full_like(m_sc, NEG)
        l_sc[...] = jnp.zeros_like(l_sc)
        acc_sc[...] = jnp.zeros_like(acc_sc)
        kloc = lax.broadcasted_iota(jnp.int32, (sq, tq), 1)
        qloc = lax.broadcasted_iota(jnp.int32, (sq, tq), 0)

        def chunk(kc, masked):
            ks = pl.multiple_of(kc * tq, tq)
            for hh in range(HP):
                lanes = slice(hh * Dh, (hh + 1) * Dh)
                k = k_ref[pl.ds(ks, tq), lanes]
                v16 = v_ref[pl.ds(ks, tq), lanes].astype(BF16)
                ck = ck_ref[hh, kc]
                for r in range(tq // sq):
                    rows = pl.ds(r * sq, sq)
                    s = lax.dot_general(q_ref[rows, lanes], k, nt, preferred_element_type=F32) * scale + (jnp.tile(cq_ref[hh, rows, :], (1, rep)) - ck)
                    if masked:
                        s = jnp.where(kloc <= qloc + r * sq, s, NEG)
                    m_old = m_sc[rows, lanes]
                    m_new = jnp.maximum(m_old, jnp.max(s, axis=1, keepdims=True))
                    alpha = jnp.exp(m_old - m_new)
                    p = jnp.exp(s - jnp.tile(m_new, (1, rep)))
                    l_sc[rows, lanes] = alpha * l_sc[rows, lanes] + jnp.sum(p, axis=1, keepdims=True)
                    acc_sc[rows, lanes] = alpha * acc_sc[rows, lanes] + jnp.dot(p.astype(BF16), v16, preferred_element_type=F32)
                    m_sc[rows, lanes] = m_new

        def below(kc, c):
            chunk(kc, False)
            return c

        lax.fori_loop(0, i, below, 0)
        chunk(i, True)
        o_ref[...] = acc_sc[...] / l_sc[...]
        for hh in range(HP):
            lanes = slice(hh * Dh, (hh + 1) * Dh)
            lse_ref[hh] = m_sc[:, lanes] + jnp.log(l_sc[:, lanes])

    W2 = HP * Dh
    return pl.pallas_call(
        body, name=name, grid=(H // HP, nq),
        in_specs=[_bs((tq, W2), lambda h, i: (i, h)), _bs((T, W2), lambda h, i: (0, h)), _bs((T, W2), lambda h, i: (0, 2 * (H // HP) + h)),
                  _bs((HP, tq, LANES), lambda h, i: (h, i, 0)), _bs((HP, nq, 1, tq), lambda h, i: (h, 0, 0, 0))],
        out_specs=[_bs((tq, W2), lambda h, i: (i, h)), _bs((HP, tq, LANES), lambda h, i: (h, i, 0))],
        out_shape=[jax.ShapeDtypeStruct((T, H * Dh), F32), jax.ShapeDtypeStruct((H, T, LANES), F32)],
        scratch_shapes=[pltpu.VMEM((tq, W2), F32), pltpu.VMEM((tq, W2), F32), pltpu.VMEM((tq, W2), F32)],
        compiler_params=_params(("parallel", "parallel")))(qn, kn, proj, cum_q, cum_k)


def _attn_bwd(name, qn, kn, proj, do, o, lse, cum_q, cum_k, H):
    T = qn.shape[0]
    Dh = FOX_HEAD_DIM
    tq = cum_k.shape[3]
    nq = T // tq
    scale = Dh ** -0.5
    nt = (((1,), (1,)), ((), ()))
    tn = (((0,), (0,)), ((), ()))
    assert H <= LANES

    sq = _t(ATTN_SUB, tq)
    rep = tq // LANES
    HP = ATTN_HEADS
    W2 = HP * Dh
    assert H % HP == 0 and Dh == LANES

    def body(q_ref, k_ref, v_ref, do_ref, o_ref, lse_ref, cq_ref, ck_ref, dq_ref, dk_ref, dv_ref, dcq_ref, dck_ref,
             delta, cql, dk_sc, dv_sc, dck_sc):
        h, j = pl.program_id(0), pl.program_id(1)

        @pl.when((h == 0) & (j == 0))
        def _():
            dcq_ref[...] = jnp.zeros_like(dcq_ref)

        @pl.when(j == 0)
        def _():
            dq_ref[...] = jnp.zeros_like(dq_ref)
            for hh in range(HP):
                lanes = slice(hh * Dh, (hh + 1) * Dh)
                delta[hh] = jnp.broadcast_to(jnp.sum(do_ref[:, lanes] * o_ref[:, lanes], axis=1, keepdims=True), (T, LANES))
            cql[...] = cq_ref[...] - lse_ref[...]

        lane_id = lax.broadcasted_iota(jnp.int32, (sq, LANES), 1)
        dk_sc[...] = jnp.zeros_like(dk_sc)
        dv_sc[...] = jnp.zeros_like(dv_sc)
        dck_sc[...] = jnp.zeros_like(dck_sc)
        kloc = lax.broadcasted_iota(jnp.int32, (sq, tq), 1)
        qloc = lax.broadcasted_iota(jnp.int32, (sq, tq), 0)

        def qblk(i, masked):
            for hh in range(HP):
                lanes = slice(hh * Dh, (hh + 1) * Dh)
                k = k_ref[:, lanes]
                v16 = v_ref[:, lanes].astype(BF16)
                ck = ck_ref[hh]
                for r in range(tq // sq):
                    rows = pl.ds(pl.multiple_of(i * tq + r * sq, sq), sq)
                    q = q_ref[rows, lanes]
                    do16 = do_ref[rows, lanes].astype(BF16)
                    e = lax.dot_general(q, k, nt, preferred_element_type=F32) * scale + (jnp.tile(cql[hh, rows, :], (1, rep)) - ck)
                    p = jnp.exp(e)
                    if masked:
                        p = jnp.where(kloc <= qloc + r * sq, p, 0.0)
                    dv_sc[:, lanes] += lax.dot_general(p.astype(BF16), do16, tn, preferred_element_type=F32)
                    dp = lax.dot_general(do16, v16, nt, preferred_element_type=F32)
                    ds = p * (dp - jnp.tile(delta[hh, rows, :], (1, rep)))
                    ds16 = ds.astype(BF16)
                    dk_sc[:, lanes] += lax.dot_general(ds16, q, tn, preferred_element_type=F32)
                    dq_ref[rows, lanes] += jnp.dot(ds16, k, preferred_element_type=F32) * scale
                    dcq_ref[rows, :] += jnp.where(lane_id == h * HP + hh, jnp.sum(ds, axis=1, keepdims=True), 0.0)
                    dck_sc[hh] += jnp.sum(ds, axis=0, keepdims=True)

        def above(i, c):
            qblk(i, False)
            return c

        qblk(j, True)
        lax.fori_loop(j + 1, nq, above, 0)
        dk_ref[...] = dk_sc[...] * scale
        dv_ref[...] = dv_sc[...].astype(BF16)
        for hh in range(HP):
            dck_ref[hh] = -dck_sc[hh]

    whole = lambda off: _bs((T, W2), lambda h, j: (0, h + off))
    blk = lambda off: _bs((tq, W2), lambda h, j: (j, h + off))
    return pl.pallas_call(
        body, name=name, grid=(H // HP, nq),
        in_specs=[whole(0), blk(0), blk(2 * (H // HP)), whole(0), whole(0), _bs((HP, T, LANES), lambda h, j: (h, 0, 0)),
                  _bs((HP, T, LANES), lambda h, j: (h, 0, 0)), _bs((HP, None, 1, tq), lambda h, j: (h, j, 0, 0))],
        out_specs=[whole(0), blk(0), blk(0), _bs((T, LANES), lambda h, j: (0, 0)),
                   _bs((HP, None, 1, tq), lambda h, j: (h, j, 0, 0))],
        out_shape=[jax.ShapeDtypeStruct((T, H * Dh), F32), jax.ShapeDtypeStruct((T, H * Dh), F32), jax.ShapeDtypeStruct((T, H * Dh), BF16),
                   jax.ShapeDtypeStruct((T, LANES), F32), jax.ShapeDtypeStruct((H, nq, 1, tq), F32)],
        scratch_shapes=[pltpu.VMEM((HP, T, LANES), F32), pltpu.VMEM((HP, T, LANES), F32), pltpu.VMEM((tq, W2), F32), pltpu.VMEM((tq, W2), F32),
                        pltpu.VMEM((HP, 1, tq), F32)],
        compiler_params=_params(("arbitrary", "arbitrary")))(qn, kn, proj, do, o, lse, cum_q, cum_k)


def _pool_fwd(name, proj, E):
    T = proj.shape[0]
    PG = len(POOL_WINDOWS)
    PD = E // PG
    tT = _t(256, T)
    hb = tT // POOL_HALO

    def body(u_ref, halo_ref, o_ref, buf):
        g, tb = pl.program_id(0), pl.program_id(1)
        u = u_ref[...]
        buf[pl.ds(POOL_HALO, tT), :] = u
        buf[pl.ds(0, POOL_HALO), :] = jnp.where(tb == 0, 0.0, halo_ref[...])
        t = tb * tT + lax.broadcasted_iota(jnp.int32, (tT, 1), 0)
        for gi, w in enumerate(POOL_WINDOWS):
            @pl.when(g == gi)
            def _():
                acc = u
                for d in range(1, w):
                    acc = acc + buf[pl.ds(POOL_HALO - d, tT), :]
                cnt = jnp.minimum(t + 1, w).astype(F32)
                o_ref[...] = (acc / cnt - u).astype(BF16)

    return pl.pallas_call(
        body, name=name, grid=(PG, T // tT),
        in_specs=[_bs((tT, PD), lambda g, t: (t, g)), _bs((POOL_HALO, PD), lambda g, t: (jnp.maximum(t * hb - 1, 0), g))],
        out_specs=_bs((tT, PD), lambda g, t: (t, g)), out_shape=jax.ShapeDtypeStruct((T, E), BF16),
        scratch_shapes=[pltpu.VMEM((tT + POOL_HALO, PD), F32)],
        compiler_params=_params(("parallel", "parallel")))(proj, proj)


def _pool_bwd(name, dpm, E):
    T = dpm.shape[0]
    PG = len(POOL_WINDOWS)
    PD = E // PG
    tT = _t(256, T)
    hb = tT // POOL_HALO
    nT = T // tT

    def body(d_ref, halo_ref, o_ref, buf):
        g, tb = pl.program_id(0), pl.program_id(1)
        d = d_ref[...]
        t = tb * tT + lax.broadcasted_iota(jnp.int32, (tT, 1), 0)
        th = (tb + 1) * tT + lax.broadcasted_iota(jnp.int32, (POOL_HALO, 1), 0)
        for gi, w in enumerate(POOL_WINDOWS):
            @pl.when(g == gi)
            def _():
                dn = d / jnp.minimum(t + 1, w).astype(F32)
                buf[pl.ds(0, tT), :] = dn
                buf[pl.ds(tT, POOL_HALO), :] = jnp.where(tb == nT - 1, 0.0, halo_ref[...] / jnp.minimum(th + 1, w).astype(F32))
                acc = dn
                for s in range(1, w):
                    acc = acc + buf[pl.ds(s, tT), :]
                o_ref[...] = (acc - d).astype(BF16)

    return pl.pallas_call(
        body, name=name, grid=(PG, nT),
        in_specs=[_bs((tT, PD), lambda g, t: (t, g)), _bs((POOL_HALO, PD), lambda g, t: (jnp.minimum((t + 1) * hb, T // POOL_HALO - 1), g))],
        out_specs=_bs((tT, PD), lambda g, t: (t, g)), out_shape=jax.ShapeDtypeStruct((T, E), BF16),
        scratch_shapes=[pltpu.VMEM((tT + POOL_HALO, PD), F32)],
        compiler_params=_params(("parallel", "parallel")))(dpm, dpm)


def _coords():
    x, y, c = lax.axis_index("x"), lax.axis_index("y"), lax.axis_index("c")
    chips = [(1 - x, y), (x, 1 - y), (1 - x, 1 - y)]
    return x, y, c, 2 * x + y, (x, y, 1 - c), chips


def _chip_allgather(name, bufs):
    n = len(bufs)

    def body(*refs):
        outs = refs[n:2 * n]
        send, recv, fsend, frecv = refs[2 * n:]
        x, y, c, p, sib, chips = _coords()

        def direct(t, j, chip):
            return pltpu.make_async_remote_copy(src_ref=outs[t].at[p, c], dst_ref=outs[t].at[p, c], send_sem=send.at[t, j],
                                                recv_sem=recv.at[t, j], device_id=(*chip, c), device_id_type=MESH)

        def landed(t, j, chip):
            blk = outs[t].at[2 * chip[0] + chip[1], c]
            return pltpu.make_async_remote_copy(src_ref=blk, dst_ref=blk, send_sem=send.at[t, j],
                                                recv_sem=recv.at[t, j], device_id=(*chip, c), device_id_type=MESH)

        def passed(t, j, chip, half):
            blk = outs[t].at[2 * chip[0] + chip[1], half]
            return pltpu.make_async_remote_copy(src_ref=blk, dst_ref=blk, send_sem=fsend.at[t, j], recv_sem=frecv.at[t, j],
                                                device_id=sib, device_id_type=MESH)

        first = [direct(t, j, chip) for t in range(n) for j, chip in enumerate(chips)]
        for cp in first:
            cp.start()
        fwd = []
        for j, chip in enumerate(chips):
            for t in range(n):
                landed(t, j, chip).wait_recv()
                f = passed(t, j, chip, c)
                f.start()
                fwd.append(f)
        for j, chip in enumerate(chips):
            for t in range(n):
                passed(t, j, chip, 1 - c).wait_recv()
        for cp in first + fwd:
            cp.wait_send()

    return pl.pallas_call(
        body, name=name, in_specs=[ANY] * n, out_specs=[ANY] * n,
        out_shape=[jax.ShapeDtypeStruct(a.shape, a.dtype) for a in bufs],
        input_output_aliases={t: t for t in range(n)},
        scratch_shapes=[pltpu.SemaphoreType.DMA((n, 3))] * 4,
    )(*bufs)


SEM = pl.BlockSpec(memory_space=pltpu.SEMAPHORE)
TOKEN = jax.ShapeDtypeStruct((SUB, LANES), F32)


def _split_params():
    return pltpu.CompilerParams(has_side_effects=pltpu.SideEffectType.DATAFLOW_SIDE_EFFECTING)


def _struct(a):
    return jax.ShapeDtypeStruct(a.shape, a.dtype)


def _gather_start(name, bufs, deps):
    n, nd = len(bufs), len(deps)

    def body(*refs):
        outs = refs[n + nd:2 * n + nd]
        send, recv, token = refs[2 * n + nd:]
        x, y, c, p, sib, chips = _coords()
        for t in range(n):
            for j, chip in enumerate(chips):
                pltpu.make_async_remote_copy(src_ref=outs[t].at[p, c], dst_ref=outs[t].at[p, c], send_sem=send.at[3 * t + j],
                                             recv_sem=recv.at[3 * t + j], device_id=(*chip, c), device_id_type=MESH).start()
        token[...] = jnp.zeros_like(token)

    res = pl.pallas_call(
        body, name=name, in_specs=[ANY] * (n + nd), out_specs=[ANY] * n + [SEM, SEM, pl.BlockSpec(memory_space=pltpu.VMEM)],
        out_shape=[_struct(a) for a in bufs] + [pltpu.SemaphoreType.DMA((3 * n,)), pltpu.SemaphoreType.DMA((3 * n,)), TOKEN],
        input_output_aliases={t: t for t in range(n)}, compiler_params=_split_params(),
    )(*bufs, *deps)
    return list(res[:n]), res[n], res[n + 1], res[n + 2]


def _gather_wait(name, bufs, send, recv, after):
    n = len(bufs)

    def body(*refs):
        send_r, recv_r = refs[n], refs[n + 1]
        outs = refs[n + 3:2 * n + 3]
        x, y, c, p, sib, chips = _coords()
        for t in range(n):
            for j, chip in enumerate(chips):
                cp = pltpu.make_async_remote_copy(src_ref=outs[t].at[p, c], dst_ref=outs[t].at[2 * chip[0] + chip[1], c], send_sem=send_r.at[3 * t + j],
                                                  recv_sem=recv_r.at[3 * t + j], device_id=(*chip, c), device_id_type=MESH)
                cp.wait_send()
                cp.wait_recv()

    return list(pl.pallas_call(
        body, name=name, in_specs=[ANY] * n + [SEM, SEM, ANY], out_specs=[ANY] * n, out_shape=[_struct(a) for a in bufs],
        input_output_aliases={t: t for t in range(n)}, compiler_params=_split_params(),
    )(*bufs, send, recv, after))


def _gather_forward(name, bufs):
    n = len(bufs)

    def body(*refs):
        outs = refs[n:2 * n]
        fsend, frecv = refs[2 * n:]
        x, y, c, p, sib, chips = _coords()

        def passed(t, j, chip, half):
            blk = outs[t].at[2 * chip[0] + chip[1], half]
            return pltpu.make_async_remote_copy(src_ref=blk, dst_ref=blk, send_sem=fsend.at[t, j], recv_sem=frecv.at[t, j],
                                                device_id=sib, device_id_type=MESH)

        fwd = [passed(t, j, chip, c) for t in range(n) for j, chip in enumerate(chips)]
        for cp in fwd:
            cp.start()
        for t in range(n):
            for j, chip in enumerate(chips):
                passed(t, j, chip, 1 - c).wait_recv()
        for cp in fwd:
            cp.wait_send()

    return list(pl.pallas_call(
        body, name=name, in_specs=[ANY] * n, out_specs=[ANY] * n, out_shape=[_struct(a) for a in bufs],
        input_output_aliases={t: t for t in range(n)}, scratch_shapes=[pltpu.SemaphoreType.DMA((n, 3))] * 2,
    )(*bufs))


def _relations():
    x, y, c = lax.axis_index("x"), lax.axis_index("y"), lax.axis_index("c")
    out = []
    for code in range(1, 8):
        tx = 1 - x if code & 4 else x
        ty = 1 - y if code & 2 else y
        tc = 1 - c if code & 1 else c
        out.append((code - 1, (tx, ty, tc), 2 * tx + ty, tc))
    return out


def _full_exchange_start(name, parts):
    n = len(parts)
    lands = [lax.empty((7,) + a.shape[2:], a.dtype) for a in parts]

    def body(*refs):
        src, dst = refs[2 * n:3 * n], refs[3 * n:4 * n]
        send, recv, token = refs[4 * n:]
        for t in range(n):
            for k, dev, q, half in _relations():
                pltpu.make_async_remote_copy(src_ref=src[t].at[half, q], dst_ref=dst[t].at[k], send_sem=send.at[7 * t + k],
                                             recv_sem=recv.at[7 * t + k], device_id=dev, device_id_type=MESH).start()
        token[...] = jnp.zeros_like(token)

    res = pl.pallas_call(
        body, name=name, in_specs=[ANY] * (2 * n), out_specs=[ANY] * (2 * n) + [SEM, SEM, pl.BlockSpec(memory_space=pltpu.VMEM)],
        out_shape=[_struct(a) for a in parts + lands] + [pltpu.SemaphoreType.DMA((7 * n,)), pltpu.SemaphoreType.DMA((7 * n,)), TOKEN],
        input_output_aliases={t: t for t in range(2 * n)}, compiler_params=_split_params(),
    )(*parts, *lands)
    return list(res[:n]), list(res[n:2 * n]), res[2 * n], res[2 * n + 1], res[2 * n + 2]


def _full_exchange_wait(name, parts, lands, send, recv, after):
    n = len(parts)

    def body(*refs):
        send_r, recv_r = refs[2 * n], refs[2 * n + 1]
        src, dst = refs[2 * n + 3:3 * n + 3], refs[3 * n + 3:4 * n + 3]
        for t in range(n):
            for k, dev, q, half in _relations():
                cp = pltpu.make_async_remote_copy(src_ref=src[t].at[half, q], dst_ref=dst[t].at[k], send_sem=send_r.at[7 * t + k],
                                                  recv_sem=recv_r.at[7 * t + k], device_id=dev, device_id_type=MESH)
                cp.wait_send()
                cp.wait_recv()

    res = pl.pallas_call(
        body, name=name, in_specs=[ANY] * (2 * n) + [SEM, SEM, ANY], out_specs=[ANY] * (2 * n),
        out_shape=[_struct(a) for a in parts + lands], input_output_aliases={t: t for t in range(2 * n)},
        compiler_params=_split_params(),
    )(*parts, *lands, send, recv, after)
    return list(res[:n]), list(res[n:])


def _chip_exchange_start(name, sums):
    n = len(sums)
    lands = [lax.empty((3,) + a.shape[1:], a.dtype) for a in sums]

    def body(*refs):
        src, dst = refs[2 * n:3 * n], refs[3 * n:4 * n]
        send, recv, token = refs[4 * n:]
        x, y, c, p, sib, chips = _coords()
        for t in range(n):
            for j, chip in enumerate(chips):
                pltpu.make_async_remote_copy(src_ref=src[t].at[2 * chip[0] + chip[1]], dst_ref=dst[t].at[j], send_sem=send.at[3 * t + j],
                                             recv_sem=recv.at[3 * t + j], device_id=(*chip, c), device_id_type=MESH).start()
        token[...] = jnp.zeros_like(token)

    res = pl.pallas_call(
        body, name=name, in_specs=[ANY] * (2 * n), out_specs=[ANY] * (2 * n) + [SEM, SEM, pl.BlockSpec(memory_space=pltpu.VMEM)],
        out_shape=[_struct(a) for a in sums + lands] + [pltpu.SemaphoreType.DMA((3 * n,)), pltpu.SemaphoreType.DMA((3 * n,)), TOKEN],
        input_output_aliases={t: t for t in range(2 * n)}, compiler_params=_split_params(),
    )(*sums, *lands)
    return list(res[:n]), list(res[n:2 * n]), res[2 * n], res[2 * n + 1], res[2 * n + 2]


def _chip_exchange_wait(name, sums, lands, send, recv, after):
    n = len(sums)

    def body(*refs):
        send_r, recv_r = refs[2 * n], refs[2 * n + 1]
        src, dst = refs[2 * n + 3:3 * n + 3], refs[3 * n + 3:4 * n + 3]
        x, y, c, p, sib, chips = _coords()
        for t in range(n):
            for j, chip in enumerate(chips):
                cp = pltpu.make_async_remote_copy(src_ref=src[t].at[2 * chip[0] + chip[1]], dst_ref=dst[t].at[j], send_sem=send_r.at[3 * t + j],
                                                  recv_sem=recv_r.at[3 * t + j], device_id=(*chip, c), device_id_type=MESH)
                cp.wait_send()
                cp.wait_recv()

    res = pl.pallas_call(
        body, name=name, in_specs=[ANY] * (2 * n) + [SEM, SEM, ANY], out_specs=[ANY] * (2 * n),
        out_shape=[_struct(a) for a in sums + lands], input_output_aliases={t: t for t in range(2 * n)},
        compiler_params=_split_params(),
    )(*sums, *lands, send, recv, after)
    return list(res[:n]), list(res[n:])


def _pair_exchange(name, parts):
    n = len(parts)

    def body(*refs):
        ins, outs = refs[:n], refs[n:2 * n]
        send, recv = refs[2 * n:]
        x, y, c, p, sib, chips = _coords()
        cps = [pltpu.make_async_remote_copy(src_ref=ins[t].at[1 - c], dst_ref=outs[t], send_sem=send.at[t], recv_sem=recv.at[t],
                                            device_id=sib, device_id_type=MESH) for t in range(n)]
        for cp in cps:
            cp.start()
        for cp in cps:
            cp.wait()

    return pl.pallas_call(
        body, name=name, in_specs=[ANY] * n, out_specs=[ANY] * n,
        out_shape=[jax.ShapeDtypeStruct(a.shape[1:], a.dtype) for a in parts],
        scratch_shapes=[pltpu.SemaphoreType.DMA((n,))] * 2,
    )(*parts)


def _chip_exchange(name, sums):
    n = len(sums)

    def body(*refs):
        ins, outs = refs[:n], refs[n:2 * n]
        send, recv = refs[2 * n:]
        x, y, c, p, sib, chips = _coords()
        cps = [pltpu.make_async_remote_copy(src_ref=ins[t].at[2 * chip[0] + chip[1]], dst_ref=outs[t].at[j], send_sem=send.at[t, j],
                                            recv_sem=recv.at[t, j], device_id=(*chip, c), device_id_type=MESH)
               for t in range(n) for j, chip in enumerate(chips)]
        for cp in cps:
            cp.start()
        for cp in cps:
            cp.wait()

    return pl.pallas_call(
        body, name=name, in_specs=[ANY] * n, out_specs=[ANY] * n,
        out_shape=[jax.ShapeDtypeStruct((3,) + a.shape[1:], a.dtype) for a in sums],
        scratch_shapes=[pltpu.SemaphoreType.DMA((n, 3))] * 2,
    )(*sums)


def _pair_share(name, bufs, items, deps=()):
    n = len(items)
    nb = len(bufs)
    nd = len(deps)

    def body(*refs):
        outs = refs[nb + nd:2 * nb + nd]
        send, recv = refs[2 * nb + nd:]
        x, y, c, p, sib, chips = _coords()

        def blk(t, half):
            o, lead = items[t]
            return outs[o].at[p if lead == 'chip' else lead, half]

        def swap(t, half):
            return pltpu.make_async_remote_copy(src_ref=blk(t, half), dst_ref=blk(t, half), send_sem=send.at[t], recv_sem=recv.at[t],
                                                device_id=sib, device_id_type=MESH)

        cps = [swap(t, c) for t in range(n)]
        for cp in cps:
            cp.start()
        for t in range(n):
            swap(t, 1 - c).wait_recv()
        for cp in cps:
            cp.wait_send()

    return list(pl.pallas_call(
        body, name=name, in_specs=[ANY] * (nb + nd), out_specs=[ANY] * nb,
        out_shape=[jax.ShapeDtypeStruct(b.shape, b.dtype) for b in bufs],
        input_output_aliases={t: t for t in range(nb)},
        scratch_shapes=[pltpu.SemaphoreType.DMA((n,))] * 2,
    )(*bufs, *deps))


def _flat2(a, lead):
    return a.reshape(a.shape[:lead] + (-1, a.shape[-1]))


def _reduce_begin(tag, parts):
    parts, lands, send, recv, token = _full_exchange_start(f"rs_start_{tag}", parts)
    return (parts, lands, send, recv), token


def _reduce_end(tag, state, after, dests, bufs, buf_shapes):
    c = lax.axis_index("c").astype(jnp.int32)
    p = (2 * lax.axis_index("x") + lax.axis_index("y")).astype(jnp.int32)
    parts, lands = _full_exchange_wait(f"rs_wait_{tag}", *state, after)

    def total(a, *others):
        s = a.astype(F32)
        for b in others:
            s = s + b.astype(F32)
        return (s,)

    for t, (mine, theirs) in enumerate(zip(parts, lands)):
        o, lead = dests[t]
        shape = buf_shapes[o]
        rows, cols = shape[2], shape[3]
        m3, t3 = mine.reshape(2 * N_CHIPS, rows, cols), theirs.reshape(7, rows, cols)
        pre = jnp.stack([c * N_CHIPS + p] + [jnp.int32(k) for k in range(7)] + [c, p if lead == 'chip' else jnp.int32(lead)])
        out = ('x', shape, F32, (None, None, 'tr', cols), lambda r, pr: (pr[9], pr[8], r, 0))
        bufs[o] = _rows(f"rs_sum_{tag}_{t}", total, [(m3, 's', cols, 0)] + [(t3, 's', cols, 1 + k) for k in range(7)], [out], 256,
                        pre=pre, into=bufs[o])[0]


def kernel(x, norm_w, out_proj, s5_in_proj, s5_a_re, s5_a_im, s5_log_dt, s5_b_re, s5_b_im, s5_c_re, s5_c_im, s5_d, s5_w_glu, s5_b_glu, fox_in_proj, fox_q_norm, fox_k_norm, fox_f_bias, pool_in_proj, pool_w_group, pool_scale, loss_target, m_norm_w, m_out_proj, m_s5_in_proj, m_s5_a_re, m_s5_a_im, m_s5_log_dt, m_s5_b_re, m_s5_b_im, m_s5_c_re, m_s5_c_im, m_s5_d, m_s5_w_glu, m_s5_b_glu, m_fox_in_proj, m_fox_q_norm, m_fox_k_norm, m_fox_f_bias, m_pool_in_proj, m_pool_w_group, m_pool_scale, v_norm_w, v_out_proj, v_s5_in_proj, v_s5_a_re, v_s5_a_im, v_s5_log_dt, v_s5_b_re, v_s5_b_im, v_s5_c_re, v_s5_c_im, v_s5_d, v_s5_w_glu, v_s5_b_glu, v_fox_in_proj, v_fox_q_norm, v_fox_k_norm, v_fox_f_bias, v_pool_in_proj, v_pool_w_group, v_pool_scale):
    weights = dict(norm_w=norm_w, out_proj=out_proj, s5_in_proj=s5_in_proj, s5_a_re=s5_a_re, s5_a_im=s5_a_im, s5_log_dt=s5_log_dt,
                   s5_b_re=s5_b_re, s5_b_im=s5_b_im, s5_c_re=s5_c_re, s5_c_im=s5_c_im, s5_d=s5_d, s5_w_glu=s5_w_glu, s5_b_glu=s5_b_glu,
                   fox_in_proj=fox_in_proj, fox_q_norm=fox_q_norm, fox_k_norm=fox_k_norm, fox_f_bias=fox_f_bias,
                   pool_in_proj=pool_in_proj, pool_w_group=pool_w_group, pool_scale=pool_scale)
    mom_m = dict(norm_w=m_norm_w, out_proj=m_out_proj, s5_in_proj=m_s5_in_proj, s5_a_re=m_s5_a_re, s5_a_im=m_s5_a_im, s5_log_dt=m_s5_log_dt,
                 s5_b_re=m_s5_b_re, s5_b_im=m_s5_b_im, s5_c_re=m_s5_c_re, s5_c_im=m_s5_c_im, s5_d=m_s5_d, s5_w_glu=m_s5_w_glu, s5_b_glu=m_s5_b_glu,
                 fox_in_proj=m_fox_in_proj, fox_q_norm=m_fox_q_norm, fox_k_norm=m_fox_k_norm, fox_f_bias=m_fox_f_bias,
                 pool_in_proj=m_pool_in_proj, pool_w_group=m_pool_w_group, pool_scale=m_pool_scale)
    mom_v = dict(norm_w=v_norm_w, out_proj=v_out_proj, s5_in_proj=v_s5_in_proj, s5_a_re=v_s5_a_re, s5_a_im=v_s5_a_im, s5_log_dt=v_s5_log_dt,
                 s5_b_re=v_s5_b_re, s5_b_im=v_s5_b_im, s5_c_re=v_s5_c_re, s5_c_im=v_s5_c_im, s5_d=v_s5_d, s5_w_glu=v_s5_w_glu, s5_b_glu=v_s5_b_glu,
                 fox_in_proj=v_fox_in_proj, fox_q_norm=v_fox_q_norm, fox_k_norm=v_fox_k_norm, fox_f_bias=v_fox_f_bias,
                 pool_in_proj=v_pool_in_proj, pool_w_group=v_pool_w_group, pool_scale=v_pool_scale)
    return _step(x, loss_target, weights, mom_m, mom_v)


BIG = ('out_proj', 's5_in_proj', 's5_w_glu', 'fox_in_proj', 'pool_in_proj', 'pool_w_group')
SMALL = ('norm_w', 's5_a_re', 's5_a_im', 's5_log_dt', 's5_b_re', 's5_b_im', 's5_c_re', 's5_c_im', 's5_d', 's5_b_glu',
         'fox_q_norm', 'fox_k_norm', 'fox_f_bias', 'pool_scale')
SMALL_SHARDED = ('s5_d', 's5_b_glu', 'pool_scale')
GROUP_AXIS_1 = ('s5_a_re', 's5_a_im', 's5_b_re', 's5_b_im', 's5_c_re', 's5_c_im')
ORDER = ('norm_w', 'out_proj', 's5_in_proj', 's5_a_re', 's5_a_im', 's5_log_dt', 's5_b_re', 's5_b_im', 's5_c_re', 's5_c_im', 's5_d',
         's5_w_glu', 's5_b_glu', 'fox_in_proj', 'fox_q_norm', 'fox_k_norm', 'fox_f_bias', 'pool_in_proj', 'pool_w_group', 'pool_scale')


def _split2(shape):
    if shape[0] % 2 == 0:
        return (2, shape[0] // 2) + tuple(shape[1:])
    assert shape[0] == 1 and shape[1] % 2 == 0
    return (2, shape[1] // 2) + tuple(shape[2:])


def _adamw_big(n, w, grads, mom_m, mom_v, delta, new_m, new_v):
    shape = w[n].shape
    if shape[-1] % LANES:
        f2 = lambda a: jnp.transpose(a.reshape(-1, shape[-1]))
        b2 = lambda a: jnp.transpose(a).reshape(shape)
    else:
        f2 = lambda a: a.reshape(-1, shape[-1])
        b2 = lambda a: a.reshape(shape)
    d_, m_, v_ = _adamw(f"adamw_{n}", f2(w[n]), f2(grads[n]), f2(mom_m[n]), f2(mom_v[n]))
    delta[n], new_m[n], new_v[n] = b2(d_), b2(m_), b2(v_)
    return d_


def _cast_weight(w, n, l, deps=()):
    p = (2 * lax.axis_index("x") + lax.axis_index("y")).astype(jnp.int32)
    a3 = w[n].reshape(w[n].shape[0], -1, w[n].shape[-1])
    layers, rows, cols = a3.shape
    out = ('x', (N_CHIPS, rows, cols), BF16, (None, 'tr', cols), lambda r, pr: (pr[0], r, 0))
    b = _rows(f"cast_{n}_{l}", lambda v: (v,), [(a3, 's', cols, 1)], [out], 256, pre=jnp.stack([p, jnp.int32(l)]), deps=deps)[0]
    return b.reshape(N_CHIPS, 2, rows // 2, cols)


def _step(x, loss_target, w, mom_m, mom_v):
    T, D = x.shape[1], x.shape[2]
    E = D
    G, P, C = w['s5_a_re'].shape[1], S5_STATE, S5_GROUP
    H = E // FOX_HEAD_DIM
    PG = len(POOL_WINDOWS)
    PD = E // PG
    NC = G // GROUPS_PER_CHUNK
    L = GROUPS_PER_CHUNK * P
    tq = _t(256, T)
    nq = T // tq

    phases = [[('s5_in_proj', 0)],
              [('s5_w_glu', 0), ('out_proj', 0)],
              [('out_proj', 1), ('fox_in_proj', 0)],
              [('out_proj', 2), ('pool_in_proj', 0), ('pool_w_group', 0), ('out_proj', 3), ('s5_in_proj', 1), ('s5_w_glu', 1)]]
    W = {}
    flight = {}

    def landed(keys, bufs):
        for k, b in zip(keys, bufs):
            W[k] = b.reshape(N_CHIPS, 2 * b.shape[2], b.shape[3])

    def take_phase(ph, after):
        bufs, send, recv, _ = flight.pop(ph)
        landed(phases[ph], _gather_forward(f"gather_{ph}_pass", _gather_wait(f"gather_{ph}_wait", bufs, send, recv, after)))

    small_full = {}
    chip = 2 * lax.axis_index("x") + lax.axis_index("y")
    sv = [lax.dynamic_update_index_in_dim(jnp.zeros((N_CHIPS, 2) + w[n].shape, F32), jnp.stack([w[n], w[n]]), chip, 0)
          for n in SMALL_SHARDED]
    got = _chip_allgather("gather_vectors", sv)
    for n, g in zip(SMALL_SHARDED, got):
        small_full[n] = jnp.transpose(g[:, 0], (1, 0, 2)).reshape(w[n].shape[0], E)
    after = [got[0]]
    for ph in range(len(phases)):
        flight[ph] = _gather_start(f"gather_{ph}_start", [_cast_weight(w, n, l, after if ph else ()) for n, l in phases[ph]], after)
        after = [flight[ph][3]]
    take_phase(0, after[0])
    gather_tokens = after

    norm_w = w['norm_w']
    h = x.reshape(T, D)
    saved = []
    dparts = {}

    def s5_consts(j):
        ar, ai, fr, fi = _s5_disc_fwd(f"s5_disc_{j}", w['s5_a_re'][j], w['s5_a_im'][j], w['s5_log_dt'][j].reshape(G, 1))
        br, bi = w['s5_b_re'][j].reshape(G * P, C), w['s5_b_im'][j].reshape(G * P, C)
        bbr, bbi = _s5_bbar(f"s5_bbar_{j}", fr.reshape(G * P, 1), fi.reshape(G * P, 1), br, bi)
        bbd = jnp.concatenate([_compact(bbr.reshape(G, P, C), NC), _compact(bbi.reshape(G, P, C), NC)], axis=2).astype(BF16)
        ct = lambda v: jnp.transpose(v, (0, 2, 1))
        cbd = jnp.concatenate([_compact(ct(w['s5_c_re'][j]), NC), -_compact(ct(w['s5_c_im'][j]), NC)], axis=2).astype(BF16)
        return dict(ar=ar, ai=ai, fr=fr, fi=fi, br=br, bi=bi, bbd=bbd, cbd=cbd,
                    ar3=ar.reshape(NC, 1, L), ai3=ai.reshape(NC, 1, L))

    for i in range(4):
        kind, j = i % 3, i // 3
        nw = norm_w[i].reshape(1, D)
        xn = _norm_fwd(f"norm_{i}", h, nw, deps=gather_tokens if i == 0 else ())
        if kind == 0:
            k5 = s5_consts(j)
            proj = _mm_proj(f"s5_proj_{i}", xn, W[('s5_in_proj', j)])
            dsk = small_full['s5_d'][j].reshape(1, E)
            y1, g, hs = _s5_fwd(f"s5_scan_{i}", proj, k5['bbd'], k5['cbd'], k5['ar3'], k5['ai3'], dsk, E)
            bglu = small_full['s5_b_glu'][j].reshape(1, E)
            if i == 0:
                take_phase(1, y1)

            def glu_epi(acc, b, y1t, z):
                lin = acc + b
                return lin, (_gelu(y1t) * _sigmoid(lin)) * _silu(z)

            lin, a = _mm_rowsharded(
                f"s5_glu_{i}", g, W[('s5_w_glu', j)], epi=glu_epi,
                extras=lambda tm, tn: [(bglu, _rowvec(tn)), (y1, _tile(tm, tn)), (proj, _tile(tm, tn, E // tn))],
                outs_fn=lambda tm, tn: [((T, E), F32, _tile(tm, tn)), ((T, E), BF16, _tile(tm, tn))])
            saved.append(dict(h=h, xn=xn, proj=proj, y1=y1, g=g, hs=hs, lin=lin, a=a, k5=k5, dsk=dsk))
        elif kind == 1:
            fox_w = jnp.transpose(W[('fox_in_proj', j)], (1, 0, 2)).reshape(D, -1)
            w_qkvz = fox_w[:, :4 * E]
            w_f = jnp.pad(fox_w[:, 4 * E:], ((0, 0), (0, LANES - H)))
            proj = _mm_plain(f"fox_proj_{i}", xn, w_qkvz)[0]
            flog = _mm_plain(f"fox_gate_proj_{i}", xn, w_f)[0]
            fb = jnp.pad(w['fox_f_bias'][j].reshape(1, H), ((0, 0), (0, LANES - H)))
            wq, wk = w['fox_q_norm'][j].reshape(1, FOX_HEAD_DIM), w['fox_k_norm'][j].reshape(1, FOX_HEAD_DIM)
            qn, kn = _qk_norm(f"fox_qk_norm_{i}", proj, wq, wk, H)
            cum = _cum_rows(f"fox_cum_{i}", flog, fb, False, True)
            cum_t = jnp.transpose(cum)[:H]
            cum_q = jnp.broadcast_to(cum_t[:, :, None], (H, T, LANES))
            cum_k = cum_t.reshape(H, nq, 1, tq)
            y, lse = _attn_fwd(f"fox_attn_{i}", qn, kn, proj, cum_q, cum_k, H)
            a = _rows(f"fox_gate_{i}", lambda yt, z: (yt * _silu(z),), [(y, 'r', E, 0), (proj, 'r', E, 3)], [('r', E, BF16)], 256)[0]
            saved.append(dict(h=h, xn=xn, proj=proj, flog=flog, fb=fb, wq=wq, wk=wk, qn=qn, kn=kn, cum_q=cum_q, cum_k=cum_k, y=y, lse=lse, a=a,
                              w_qkvz=w_qkvz, w_f=w_f))
        else:
            w_pg = W[('pool_w_group', j)].reshape(N_CHIPS, PG, PD // N_CHIPS, PD)
            proj = _mm_proj(f"pool_proj_{i}", xn, W[('pool_in_proj', j)])
            pm = _pool_fwd(f"pool_win_{i}", proj, E)
            scale = small_full['pool_scale'][j].reshape(1, E)
            tm, tn, tk = _t(512, T), _t(512, PD), w_pg.shape[2]
            kb, nb = PD // tk, PD // tn
            mixed, a = _mm(
                f"pool_mix_{i}", pm, w_pg, M=T, N=PD, K=PD, tm=tm, tn=tn, tk=tk, groups=PG,
                a_spec=_bs((tm, tk), lambda g, m, n, k: (m, g * kb + k)),
                b_spec=_bs((None, None, tk, tn), lambda g, m, n, k: (k, g, 0, n)),
                extras=[(scale, _bs((1, tn), lambda g, m, n, k: (0, g * nb + n))),
                        (proj, _bs((tm, tn), lambda g, m, n, k: (m, E // tn + g * nb + n)))],
                epi=lambda acc, sc, z: (acc, (acc * sc) * _silu(z)),
                outs=[((T, E), F32, _bs((tm, tn), lambda g, m, n, k: (m, g * nb + n))),
                      ((T, E), BF16, _bs((tm, tn), lambda g, m, n, k: (m, g * nb + n)))])
            saved.append(dict(h=h, xn=xn, proj=proj, pm=pm, mixed=mixed, scale=scale, a=a, w_pg=w_pg))
        h = _mm_rowsharded(f"out_proj_{i}", saved[-1]['a'], W[('out_proj', i)], epi=lambda acc, r: (r + acc,),
                           extras=lambda tm, tn: [(h, _tile(tm, tn))],
                           outs_fn=lambda tm, tn: [((T, D), F32, _tile(tm, tn))])[0]
        if i < 2:
            take_phase(i + 2, h)

    dh, dh16, loss_cols = _loss(h, loss_target.reshape(T, D))
    loss = lax.psum(jnp.sum(loss_cols), ("x", "y", "c"))

    gsmall = {n: [None] * w[n].shape[0] for n in SMALL}
    big_index = {n: o for o, n in enumerate(BIG)}
    rs_shapes = [None] * (len(BIG) + 1)
    rs_bufs = [None] * (len(BIG) + 1)
    rs_dests_all = []
    pending = None

    def reduce_layer(tag, named_parts):
        parts, dests = [], []
        for n, l, pt in named_parts:
            o = big_index[n] if n in big_index else len(BIG)
            half = pt.shape[2:]
            rs_shapes[o] = (N_CHIPS if l == 'chip' else w[n].shape[0], 2, math.prod(half[:-1]), half[-1])
            parts.append(pt)
            dests.append((o, l))
        rs_dests_all.extend(dests)
        state, token = _reduce_begin(tag, parts)
        return (tag, state, dests), token

    token = loss.reshape(1, 1)
    for i in reversed(range(4)):
        kind, j = i % 3, i // 3
        sv_ = saved[i]
        nw = norm_w[i].reshape(1, D)
        w_out = W[('out_proj', i)]
        after_start = [token] if token is not None else ()
        layer_parts = [('out_proj', i, _mm_dw_rows(f"d_out_proj_{i}", sv_['a'], dh16, deps=after_start))]
        if kind == 0:
            w_glu = W[('s5_w_glu', j)]
            proj, y1, lin, k5 = sv_['proj'], sv_['y1'], sv_['lin'], sv_['k5']

            def da_epi(da, y1t, lint, z):
                gt, sg = _gelu(y1t), _sigmoid(lint)
                dy2 = da * _silu(z)
                dlin = (dy2 * gt) * (sg * (1.0 - sg))
                return da * (gt * sg) * _dsilu(z), dlin, dy2 * sg, _colsum(dlin)

            nm = T // _t(512, T)
            dz, dlin, dgd, dbg = _mm_rowsharded_t(
                f"d_s5_act_{i}", dh16, w_out, epi=da_epi, deps=after_start,
                extras=lambda tm, tn: [(y1, _tile(tm, tn)), (lin, _tile(tm, tn)), (proj, _tile(tm, tn, E // tn))],
                outs_fn=lambda tm, tn: [((T, E), BF16, _tile(tm, tn)), ((T, E), BF16, _tile(tm, tn)), ((T, E), F32, _tile(tm, tn)),
                                        ((nm, 1, E), F32, _bs((None, 1, tn), lambda g, m, n, k: (m, 0, n)))])
            gsmall['s5_b_glu'][j] = jnp.sum(dbg, axis=(0, 1))
            layer_parts.append(('s5_w_glu', j, _mm_dw_rows(f"d_s5_w_glu_{i}", sv_['g'], dlin)))
            glu_deps = ()
            if i == 0:
                early, early_token = reduce_layer("l0a", layer_parts)
                layer_parts, glu_deps = [], [early_token]
            dy1 = _mm_rowsharded_t(
                f"d_s5_glu_{i}", dlin, w_glu, epi=lambda acc, d, y1t: ((acc + d) * _dgelu(y1t),), deps=glu_deps,
                extras=lambda tm, tn: [(dgd, _tile(tm, tn)), (y1, _tile(tm, tn))],
                outs_fn=lambda tm, tn: [((T, E), F32, _tile(tm, tn))])[0]
            du, dbd, dcd, dab, ddk = _s5_bwd(f"d_s5_scan_{i}", dy1, proj, sv_['hs'], k5['bbd'], k5['cbd'], k5['ar3'], k5['ai3'], sv_['dsk'], E)
            gsmall['s5_d'][j] = ddk.reshape(E)
            gsmall['s5_c_re'][j] = jnp.transpose(_uncompact(dcd[:, :, :L], G), (0, 2, 1))
            gsmall['s5_c_im'][j] = -jnp.transpose(_uncompact(dcd[:, :, L:], G), (0, 2, 1))
            dbbr = _uncompact(dbd[:, :, :L], G).reshape(G * P, C)
            dbbi = _uncompact(dbd[:, :, L:], G).reshape(G * P, C)
            dbr, dbi, dfr, dfi = _s5_bbar_bwd(f"d_s5_bbar_{i}", k5['fr'].reshape(G * P, 1), k5['fi'].reshape(G * P, 1), k5['br'], k5['bi'], dbbr, dbbi)
            gsmall['s5_b_re'][j] = dbr.reshape(G, P, C)
            gsmall['s5_b_im'][j] = dbi.reshape(G, P, C)
            dab = jnp.sum(dab, axis=1)
            dare, daim, dldt = _s5_disc_bwd(f"d_s5_disc_{i}", w['s5_a_re'][j], w['s5_a_im'][j], w['s5_log_dt'][j].reshape(G, 1),
                                            (dab[:, :L].reshape(G, P), dab[:, L:].reshape(G, P), dfr.reshape(G, P), dfi.reshape(G, P)))
            gsmall['s5_a_re'][j], gsmall['s5_a_im'][j], gsmall['s5_log_dt'][j] = dare, daim, dldt.reshape(G)
            dproj = jnp.concatenate([du, dz], axis=1)
            layer_parts.append(('s5_in_proj', j, _mm_dw_cols(f"d_s5_in_proj_{i}", sv_['xn'], dproj)))
            dxn = _mm_colsharded_t(f"d_s5_xn_{i}", dproj, W[('s5_in_proj', j)])
        elif kind == 1:
            proj, y = sv_['proj'], sv_['y']
            do, dz = _mm_rowsharded_t(
                f"d_fox_act_{i}", dh16, w_out, epi=lambda da, yt, z: (da * _silu(z), (da * yt) * _dsilu(z)), deps=after_start,
                extras=lambda tm, tn: [(y, _tile(tm, tn)), (proj, _tile(tm, tn, 3 * E // tn))],
                outs_fn=lambda tm, tn: [((T, E), F32, _tile(tm, tn)), ((T, E), BF16, _tile(tm, tn))])
            dqn, dkn, dv, dcq, dck = _attn_bwd(f"d_fox_attn_{i}", sv_['qn'], sv_['kn'], proj, do, y, sv_['lse'], sv_['cum_q'], sv_['cum_k'], H)
            dq, dk, dwq, dwk = _qk_norm_bwd(f"d_fox_qk_norm_{i}", proj, sv_['wq'], sv_['wk'], dqn, dkn, H)
            gsmall['fox_q_norm'][j], gsmall['fox_k_norm'][j] = dwq.reshape(-1), dwk.reshape(-1)
            dcum = dcq + jnp.pad(jnp.transpose(dck.reshape(H, T)), ((0, 0), (0, LANES - H)))
            dls = _cum_rows(f"d_fox_cum_{i}", dcum, jnp.zeros((1, LANES), F32), True, False)
            dflog, dfb = _rows(f"d_fox_gate_{i}", lambda d, f, b: ((lambda r: (r, _colsum(r)))(d * _sigmoid(-(f + b)))),
                               [(dls, 'r', LANES, 0), (sv_['flog'], 'r', LANES, 0), (sv_['fb'], 'b', LANES, 0)],
                               [('r', LANES, BF16), ('a', LANES, F32)], 256)
            gsmall['fox_f_bias'][j] = dfb[0, :H]
            dproj = jnp.concatenate([dq, dk, dv, dz], axis=1)
            tkT = _t(K_STEP, T)
            dw_qkvz = _mm(f"d_fox_in_proj_{i}", sv_['xn'], dproj, M=D, N=4 * E, K=T, tm=_t(512, D), tn=_t(1024, 4 * E), tk=tkT, ta=True,
                          a_spec=_bs((tkT, _t(512, D)), lambda g, m, n, k: (k, m)),
                          b_spec=_bs((tkT, _t(1024, 4 * E)), lambda g, m, n, k: (k, n)),
                          outs=[((D, 4 * E), BF16, _tile(_t(512, D), _t(1024, 4 * E)))])[0]
            dw_f = _mm(f"d_fox_gate_proj_{i}", sv_['xn'], dflog, M=D, N=LANES, K=T, tm=_t(512, D), tn=LANES, tk=tkT, ta=True,
                       a_spec=_bs((tkT, _t(512, D)), lambda g, m, n, k: (k, m)),
                       b_spec=_bs((tkT, LANES), lambda g, m, n, k: (k, n)),
                       outs=[((D, LANES), BF16, _tile(_t(512, D), LANES))])[0]
            dw_fox = jnp.concatenate([dw_qkvz, dw_f[:, :H]], axis=1)
            sw = dw_fox.shape[1] // N_CHIPS
            layer_parts.append(('fox_in_proj', j, jnp.transpose(dw_fox.reshape(2, D // 2, N_CHIPS, sw), (0, 2, 1, 3))))
            w_qkvz, w_f = sv_['w_qkvz'], sv_['w_f']
            dxn_f = _mm(f"d_fox_xn_gate_{i}", dflog, w_f, M=T, N=D, K=LANES, tm=_t(512, T), tn=_t(1024, D), tk=LANES, tb=True,
                        a_spec=_bs((_t(512, T), LANES), lambda g, m, n, k: (m, k)),
                        b_spec=_bs((_t(1024, D), LANES), lambda g, m, n, k: (n, k)),
                        outs=[((T, D), F32, _tile(_t(512, T), _t(1024, D)))])[0]
            tm, tn, tk = _t(512, T), _t(1024, D), _t(K_STEP, 4 * E)
            dxn = _mm(f"d_fox_xn_{i}", dproj, w_qkvz, M=T, N=D, K=4 * E, tm=tm, tn=tn, tk=tk, tb=True,
                      a_spec=_bs((tm, tk), lambda g, m, n, k: (m, k)), b_spec=_bs((tn, tk), lambda g, m, n, k: (n, k)),
                      extras=[(dxn_f, _tile(tm, tn))], epi=lambda acc, e: (acc + e,),
                      outs=[((T, D), F32, _tile(tm, tn))])[0]
        else:
            proj, mixed, scale = sv_['proj'], sv_['mixed'], sv_['scale']
            nm = T // _t(512, T)

            def pool_epi(da, mx, sc, z):
                dy = da * _silu(z)
                return (da * (mx * sc)) * _dsilu(z), dy * sc, _colsum(dy * mx)

            dz, dmix, dsc = _mm_rowsharded_t(
                f"d_pool_act_{i}", dh16, w_out, epi=pool_epi, deps=after_start,
                extras=lambda tm, tn: [(mixed, _tile(tm, tn)), (scale, _rowvec(tn)), (proj, _tile(tm, tn, E // tn))],
                outs_fn=lambda tm, tn: [((T, E), BF16, _tile(tm, tn)), ((T, E), BF16, _tile(tm, tn)),
                                        ((nm, 1, E), F32, _bs((None, 1, tn), lambda g, m, n, k: (m, 0, n)))])
            gsmall['pool_scale'][j] = jnp.sum(dsc, axis=(0, 1))
            w_pg = sv_['w_pg']
            tkw = w_pg.shape[2]
            tk = _t(K_STEP, T)
            layer_parts.append(('pool_w_group', j, _mm(
                f"d_pool_w_group_{i}", sv_['pm'], dmix, M=PD, N=PD, K=T, tm=tkw, tn=PD, tk=tk, groups=PG, ta=True,
                a_spec=_bs((tk, tkw), lambda g, m, n, k: (k, g * (PD // tkw) + m)),
                b_spec=_bs((tk, PD), lambda g, m, n, k: (k, g)),
                outs=[((2, N_CHIPS, PG // 2, tkw, PD), BF16, _bs((None, None, None, tkw, PD), lambda g, m, n, k: (g // (PG // 2), m, g % (PG // 2), 0, 0)))])[0]))
            tm, tk2 = _t(512, T), _t(512, PD)
            dpm = _mm(f"d_pool_mix_{i}", dmix, w_pg, M=T, N=PD, K=PD, tm=tm, tn=tkw, tk=tk2, groups=PG, tb=True,
                      a_spec=_bs((tm, tk2), lambda g, m, n, k: (m, g * (PD // tk2) + k)),
                      b_spec=_bs((None, None, tkw, tk2), lambda g, m, n, k: (n, g, 0, k)),
                      outs=[((T, E), F32, _bs((tm, tkw), lambda g, m, n, k: (m, g * (PD // tkw) + n)))])[0]
            du = _pool_bwd(f"d_pool_win_{i}", dpm, E)
            dproj = jnp.concatenate([du, dz], axis=1)
            layer_parts.append(('pool_in_proj', j, _mm_dw_cols(f"d_pool_in_proj_{i}", sv_['xn'], dproj)))
            dxn = _mm_colsharded_t(f"d_pool_xn_{i}", dproj, W[('pool_in_proj', j)])
        dh, dh16, dnw = _norm_bwd(f"d_norm_{i}", dxn, sv_['h'], nw, dh)
        gsmall['norm_w'][i] = dnw.reshape(D)
        if pending is not None:
            _reduce_end(pending[0], pending[1], dh16, pending[2], rs_bufs, rs_shapes)
        if i > 0:
            pending, token = reduce_layer(f"l{i}", layer_parts)
    grad_x = dh.reshape(x.shape)

    small_flat = jnp.concatenate([jnp.stack(gsmall[n]).reshape(-1) for n in SMALL])
    n_small = small_flat.shape[0]
    unit = 2 * N_CHIPS * 16 * LANES
    n_pad = -(-n_small // unit) * unit
    R = n_pad // (2 * N_CHIPS * LANES)
    small_part = jnp.pad(small_flat, (0, n_pad - n_small)).astype(BF16).reshape(2, N_CHIPS, R, LANES)
    pending, token = reduce_layer("l0", layer_parts + [('small', 'chip', small_part)])
    _reduce_end(early[0], early[1], token, early[2], rs_bufs, rs_shapes)
    nb = len(BIG)
    done_items = [d for d in rs_dests_all if d not in pending[2]]
    rs_bufs[:nb] = _pair_share("rs_pair_share_a", rs_bufs[:nb], done_items, deps=[token])
    late = [o for o, _ in pending[2]]
    delta, new_m, new_v = {}, {}, {}
    grads = {}
    last = token[:1, :1]
    for o, n in enumerate(BIG):
        if o not in late:
            grads[n] = rs_bufs[o].reshape(w[n].shape)
            last = last + _adamw_big(n, w, grads, mom_m, mom_v, delta, new_m, new_v)[:1, :1]
    _reduce_end(pending[0], pending[1], last, pending[2], rs_bufs, rs_shapes)
    shared = _pair_share("rs_pair_share_b", [rs_bufs[o] for o in late], [(k, l) for k, (_, l) in enumerate(pending[2])])
    for k, o in enumerate(late):
        rs_bufs[o] = shared[k]
        if o < nb:
            grads[BIG[o]] = shared[k].reshape(w[BIG[o]].shape)
            _adamw_big(BIG[o], w, grads, mom_m, mom_v, delta, new_m, new_v)
    small_all = _chip_allgather("gather_small_grads", [rs_bufs[nb]])[0]
    small_all = jnp.transpose(small_all, (1, 0, 2, 3)).reshape(-1)[:n_small]
    off = 0
    p = 2 * lax.axis_index("x") + lax.axis_index("y")
    for n in SMALL:
        full_shape = (w[n].shape[0], E) if n in SMALL_SHARDED else w[n].shape
        size = math.prod(full_shape)
        gfull = small_all[off:off + size].reshape(full_shape)
        off += size
        if n in SMALL_SHARDED:
            gfull = lax.dynamic_slice_in_dim(gfull, p * (E // N_CHIPS), E // N_CHIPS, axis=1)
        grads[n] = gfull

    for n in SMALL:
        shape = w[n].shape
        if n in GROUP_AXIS_1:
            perm = (0,) + tuple(range(2, len(shape))) + (1,)
            inv = (0, len(shape) - 1) + tuple(range(1, len(shape) - 1))
            view = lambda a: jnp.transpose(a, perm).reshape(-1, shape[1])
            back = lambda a: jnp.transpose(a.reshape(tuple(shape[k] for k in perm)), inv)
        else:
            view = lambda a: a.reshape(-1, shape[-1])
            back = lambda a: a.reshape(shape)
        d_, m_, v_ = _adamw(f"adamw_{n}", view(w[n]), view(grads[n]), view(mom_m[n]), view(mom_v[n]))
        delta[n], new_m[n], new_v[n] = back(d_), back(m_), back(v_)
    return (loss, grad_x, *[grads[n] for n in ORDER], *[delta[n] for n in ORDER], *[new_m[n] for n in ORDER], *[new_v[n] for n in ORDER])
```

```python
import functools
import math

import jax
import jax.numpy as jnp
from jax import lax
from jax.experimental import pallas as pl
from jax.experimental.pallas import tpu as pltpu

F32 = jnp.float32
BF16 = jnp.bfloat16
MESH = pl.DeviceIdType.MESH

N_CHIPS = 4
VMEM_LIMIT = 56 * 1024 * 1024
LANES = 128
SUB = 8

EPS = 1e-6
S5_GROUP = 16
S5_STATE = 64
GROUPS_PER_CHUNK = 16
FOX_HEAD_DIM = 128
NORM_HEADS = 4
ATTN_SUB = 256
ATTN_HEADS = 2
POOL_WINDOWS = (2, 4, 8, 16)
POOL_HALO = 16
ADAM_LR, ADAM_B1, ADAM_B2, ADAM_EPS, ADAM_WD, ADAM_STEP = 0.001, 0.9, 0.999, 1e-08, 0.01, 10
NEG = -1e30
K_STEP = 2048


ANY = pl.BlockSpec(memory_space=pl.ANY)


def _t(pref, dim):
    if dim <= pref:
        return dim
    t = pref - pref % 16
    while t > 16 and dim % t:
        t -= 16
    assert dim % t == 0, (pref, dim)
    return t


def _params(sem):
    return pltpu.CompilerParams(dimension_semantics=sem, vmem_limit_bytes=VMEM_LIMIT)


def _sigmoid(x):
    return 1.0 / (1.0 + jnp.exp(-x))


def _silu(z):
    return z * _sigmoid(z)


def _dsilu(z):
    s = _sigmoid(z)
    return s * (1.0 + z * (1.0 - s))


_GELU_C = math.sqrt(2.0 / math.pi)


def _gelu(x):
    return 0.5 * x * (1.0 + jnp.tanh(_GELU_C * (x + 0.044715 * (x * x * x))))


def _dgelu(x):
    t = jnp.tanh(_GELU_C * (x + 0.044715 * (x * x * x)))
    return 0.5 * (1.0 + t) + 0.5 * x * (1.0 - t * t) * (_GELU_C * (1.0 + 3.0 * 0.044715 * x * x))


def _log_sigmoid(x):
    return jnp.minimum(x, 0.0) - jnp.log(1.0 + jnp.exp(-jnp.abs(x)))


def _rms(x):
    return lax.rsqrt(jnp.mean(x * x, axis=-1, keepdims=True) + EPS)


def _rms_bwd(x, w, dy):
    r = _rms(x)
    xhat = x * r
    dxh = dy * w
    dx = r * (dxh - xhat * jnp.mean(dxh * xhat, axis=-1, keepdims=True))
    return dx, dy * xhat


def _rows(name, fn, ins, outs, tr, pre=None, into=None, deps=()):
    rows = None
    for arr, kind, cols, cb in ins:
        if kind == 'r':
            rows = arr.shape[0]
        elif kind == 's' and rows is None:
            rows = arr.shape[1]
    tr = _t(tr, rows)
    n_in = len(ins)
    has_acc = any(o[0] == 'a' for o in outs)

    def spec(kind, cols, cb):
        if kind == 'r':
            return pl.BlockSpec((tr, cols), lambda r, *p: (r, cb))
        if kind == 'b':
            return pl.BlockSpec((1, cols), lambda r, *p: (0, cb))
        return pl.BlockSpec((None, tr, cols), lambda r, p: (p[cb], r, 0))

    in_specs = [spec(kind, cols, cb) for _, kind, cols, cb in ins]
    out_specs, out_shape = [], []
    for o in outs:
        if o[0] == 'r':
            out_specs.append(pl.BlockSpec((tr, o[1]), lambda r, *p: (r, 0)))
            out_shape.append(jax.ShapeDtypeStruct((rows, o[1]), o[2]))
        elif o[0] == 'a':
            out_specs.append(pl.BlockSpec((1, o[1]), lambda r, *p: (0, 0)))
            out_shape.append(jax.ShapeDtypeStruct((1, o[1]), o[2]))
        else:
            blk = tuple(tr if d == 'tr' else d for d in o[3])
            out_specs.append(pl.BlockSpec(blk, o[4]))
            out_shape.append(jax.ShapeDtypeStruct(o[1], o[2]))
    n_pre = 0 if pre is None else 1
    args = [a[0] for a in ins]
    aliases = {}
    if into is not None:
        in_specs.append(ANY)
        args.append(into)
        aliases = {n_pre + n_in: 0}
    in_specs += [ANY] * len(deps)
    args += list(deps)
    n_all = len(args)

    def body(*refs):
        refs = refs[n_pre:]
        res = fn(*[r[...] for r in refs[:n_in]])
        for spec_o, o, v in zip(outs, refs[n_all:], res):
            if spec_o[0] == 'a':
                @pl.when(pl.program_id(0) == 0)
                def _():
                    o[...] = jnp.zeros_like(o)
                o[...] += v.astype(o.dtype)
            else:
                o[...] = v.astype(o.dtype)

    grid_spec = pltpu.PrefetchScalarGridSpec(num_scalar_prefetch=n_pre, grid=(rows // tr,), in_specs=in_specs, out_specs=out_specs)
    if pre is not None:
        args = [pre] + args
    return pl.pallas_call(body, name=name, grid_spec=grid_spec, out_shape=out_shape, input_output_aliases=aliases,
                          compiler_params=_params(("arbitrary" if has_acc else "parallel",)))(*args)


def _colsum(v):
    return jnp.sum(v, axis=0, keepdims=True)


def _mm(name, a, b, *, M, N, K, tm, tn, tk, a_spec, b_spec, outs, epi=None, extras=(), groups=1, ta=False, tb=False, deps=()):
    nk = K // tk
    assert M % tm == 0 and N % tn == 0 and K % tk == 0, (name, M, N, K, tm, tn, tk)
    dims = (((0 if ta else 1,), (1 if tb else 0,)), ((), ()))
    n_ex = len(extras)

    def body(*refs):
        a_ref, b_ref = refs[0], refs[1]
        ex = refs[2:2 + n_ex]
        out_refs = refs[2 + n_ex + len(deps):2 + n_ex + len(deps) + len(outs)]

        def finish(r):
            res = (r,) if epi is None else epi(r, *[e[...] for e in ex])
            for o, v in zip(out_refs, res):
                o[...] = v.astype(o.dtype)

        part = lax.dot_general(a_ref[...].astype(BF16), b_ref[...].astype(BF16), dims, preferred_element_type=F32)
        if nk == 1:
            finish(part)
            return
        acc = refs[-1]
        k = pl.program_id(3)

        @pl.when(k == 0)
        def _():
            acc[...] = part

        @pl.when(k > 0)
        def _():
            acc[...] += part

        @pl.when(k == nk - 1)
        def _():
            finish(acc[...])

    return pl.pallas_call(
        body, name=name, grid=(groups, M // tm, N // tn, nk),
        in_specs=[a_spec, b_spec] + [s for _, s in extras] + [ANY] * len(deps),
        out_specs=[s for _, _, s in outs],
        out_shape=[jax.ShapeDtypeStruct(sh, dt) for sh, dt, _ in outs],
        scratch_shapes=[] if nk == 1 else [pltpu.VMEM((tm, tn), F32)],
        compiler_params=_params(("parallel", "parallel", "parallel", "arbitrary")),
    )(a, b, *[e for e, _ in extras], *deps)


def _bs(shape, f):
    return pl.BlockSpec(shape, f)


def _tile(tm, tn, coff=0):
    return _bs((tm, tn), lambda g, m, n, k: (m, n + coff))


def _rowvec(tn, coff=0):
    return _bs((1, tn), lambda g, m, n, k: (0, n + coff))


def _mm_proj(name, xn, w, *, epi=None, extras=(), out_dtype=F32):
    T, D = xn.shape
    sw = w.shape[2]
    N = N_CHIPS * sw
    tm, tn, tk = _t(512, T), _t(1024, sw), _t(K_STEP, D)
    nb = sw // tn
    return _mm(name, xn, w, M=T, N=N, K=D, tm=tm, tn=tn, tk=tk,
               a_spec=_bs((tm, tk), lambda g, m, n, k: (m, k)),
               b_spec=_bs((None, tk, tn), lambda g, m, n, k: (n // nb, k, n % nb)),
               outs=[((T, N), out_dtype, _tile(tm, tn))], epi=epi, extras=extras)[0]


def _mm_plain(name, a, b, *, out_dtype=F32, epi=None, extras=(), outs=None, tn_pref=1024):
    M, K = a.shape
    N = b.shape[1]
    tm, tn, tk = _t(512, M), _t(tn_pref, N), _t(K_STEP, K)
    if outs is None:
        outs = [((M, N), out_dtype, _tile(tm, tn))]
    return _mm(name, a, b, M=M, N=N, K=K, tm=tm, tn=tn, tk=tk,
               a_spec=_bs((tm, tk), lambda g, m, n, k: (m, k)),
               b_spec=_bs((tk, tn), lambda g, m, n, k: (k, n)),
               outs=outs, epi=epi, extras=extras)


def _mm_rowsharded(name, a, w, *, epi, extras, outs_fn, deps=()):
    T, E = a.shape
    N = w.shape[2]
    tm, tn, tk = _t(512, T), _t(1024, N), _t(K_STEP, E)
    return _mm(name, a, w.reshape(E, N), M=T, N=N, K=E, tm=tm, tn=tn, tk=tk, deps=deps,
               a_spec=_bs((tm, tk), lambda g, m, n, k: (m, k)),
               b_spec=_bs((tk, tn), lambda g, m, n, k: (k, n)),
               outs=outs_fn(tm, tn), epi=epi, extras=extras(tm, tn))


def _mm_rowsharded_t(name, d, w, *, epi, extras, outs_fn, deps=()):
    T, N = d.shape
    tn = w.shape[1]
    E = N_CHIPS * tn
    tm, tk = _t(512, T), _t(K_STEP, N)
    return _mm(name, d, w, M=T, N=E, K=N, tm=tm, tn=tn, tk=tk, tb=True, deps=deps,
               a_spec=_bs((tm, tk), lambda g, m, n, k: (m, k)),
               b_spec=_bs((None, tn, tk), lambda g, m, n, k: (n, 0, k)),
               outs=outs_fn(tm, tn), epi=epi, extras=extras(tm, tn))


def _mm_colsharded_t(name, d, w):
    T, N = d.shape
    D, sw = w.shape[1], w.shape[2]
    tm, tn, tk = _t(512, T), _t(1024, D), _t(1024, sw)
    kb = sw // tk
    return _mm(name, d, w, M=T, N=D, K=N, tm=tm, tn=tn, tk=tk, tb=True,
               a_spec=_bs((tm, tk), lambda g, m, n, k: (m, k)),
               b_spec=_bs((None, tn, tk), lambda g, m, n, k: (k // kb, n, k % kb)),
               outs=[((T, D), F32, _tile(tm, tn))])[0]


def _mm_dw_rows(name, a, d, deps=()):
    T, E = a.shape
    N = d.shape[1]
    tm, tn, tk = E // (2 * N_CHIPS), _t(2048, N), _t(K_STEP, T)
    return _mm(name, a, d, M=E, N=N, K=T, tm=tm, tn=tn, tk=tk, ta=True, deps=deps,
               a_spec=_bs((tk, tm), lambda g, m, n, k: (k, m)),
               b_spec=_bs((tk, tn), lambda g, m, n, k: (k, n)),
               outs=[((2, N_CHIPS, tm, N), BF16, _bs((None, None, tm, tn), lambda g, m, n, k: (m % 2, m // 2, 0, n)))])[0]


def _mm_dw_cols(name, xn, d):
    T, D = xn.shape
    N = d.shape[1]
    sw = N // N_CHIPS
    tm, tn, tk = _t(512, D // 2), _t(1024, sw), _t(K_STEP, T)
    mh, nb = (D // 2) // tm, sw // tn
    return _mm(name, xn, d, M=D, N=N, K=T, tm=tm, tn=tn, tk=tk, ta=True,
               a_spec=_bs((tk, tm), lambda g, m, n, k: (k, m)),
               b_spec=_bs((tk, tn), lambda g, m, n, k: (k, n)),
               outs=[((2, N_CHIPS, D // 2, sw), BF16,
                      _bs((None, None, tm, tn), lambda g, m, n, k: (m // mh, n // nb, m % mh, n % nb)))])[0]


def _norm_fwd(name, h, w, deps=()):
    D = h.shape[1]
    return _rows(name, lambda x, g: ((x * _rms(x)) * g,), [(h, 'r', D, 0), (w, 'b', D, 0)], [('r', D, BF16)], 256, deps=deps)[0]


def _norm_bwd(name, dxn, h, w, dh):
    D = h.shape[1]

    def fn(dy, x, g, up):
        dx, dwt = _rms_bwd(x, g, dy)
        r = up + dx
        return r, r, _colsum(dwt)

    return _rows(name, fn, [(dxn, 'r', D, 0), (h, 'r', D, 0), (w, 'b', D, 0), (dh, 'r', D, 0)],
                 [('r', D, F32), ('r', D, BF16), ('a', D, F32)], 256)


def _loss(h, target):
    D = h.shape[1]

    def fn(y, t):
        e = y - t
        d = e * (1.0 / D)
        return d, d, _colsum(e * e) * (0.5 / D)

    return _rows("loss", fn, [(h, 'r', D, 0), (target, 'r', D, 0)], [('r', D, F32), ('r', D, BF16), ('a', D, F32)], 256)


def _adamw(name, w, g, m, v):
    cols = w.shape[1]

    def fn(w, g, m, v):
        m = ADAM_B1 * m + (1.0 - ADAM_B1) * g
        v = ADAM_B2 * v + (1.0 - ADAM_B2) * (g * g)
        m_hat = m / (1.0 - ADAM_B1 ** ADAM_STEP)
        v_hat = v / (1.0 - ADAM_B2 ** ADAM_STEP)
        delta = -ADAM_LR * (m_hat / (jnp.sqrt(v_hat) + ADAM_EPS) + ADAM_WD * w)
        return delta, m, v

    rows = w.shape[0]
    if rows % SUB == 0 or rows <= 256:
        return _rows(name, fn, [(x, 'r', cols, 0) for x in (w, g, m, v)], [('r', cols, F32)] * 3, 256)
    tc = _t(256, cols)
    assert tc % LANES == 0, (rows, cols)

    def body(w_ref, g_ref, m_ref, v_ref, d_out, m_out, v_out):
        for o, r in zip((d_out, m_out, v_out), fn(w_ref[...], g_ref[...], m_ref[...], v_ref[...])):
            o[...] = r

    blk = pl.BlockSpec((rows, tc), lambda j: (0, j))
    return pl.pallas_call(body, name=name, grid=(cols // tc,), in_specs=[blk] * 4, out_specs=[blk] * 3,
                          out_shape=[jax.ShapeDtypeStruct((rows, cols), F32)] * 3, compiler_params=_params(("parallel",)))(w, g, m, v)


def _s5_disc(a_re, a_im, log_dt):
    dt = jnp.exp(log_dt)
    mag = jnp.exp(a_re * dt)
    abar_r = mag * jnp.cos(a_im * dt)
    abar_i = mag * jnp.sin(a_im * dt)
    den = a_re * a_re + a_im * a_im
    xr = abar_r - 1.0
    fr = (xr * a_re + abar_i * a_im) / den
    fi = (abar_i * a_re - xr * a_im) / den
    return abar_r, abar_i, fr, fi


def _s5_disc_fwd(name, a_re, a_im, log_dt):
    G, P = a_re.shape

    def body(ar, ai, ld, o0, o1, o2, o3):
        for o, v in zip((o0, o1, o2, o3), _s5_disc(ar[...], ai[...], ld[...])):
            o[...] = v

    return pl.pallas_call(body, name=name, out_shape=[jax.ShapeDtypeStruct((G, P), F32)] * 4)(a_re, a_im, log_dt)


def _s5_disc_bwd(name, a_re, a_im, log_dt, cts):
    G, P = a_re.shape

    def body(ar, ai, ld, c0, c1, c2, c3, d0, d1, d2):
        _, vjp = jax.vjp(_s5_disc, ar[...], ai[...], ld[...])
        g0, g1, g2 = vjp((c0[...], c1[...], c2[...], c3[...]))
        d0[...] = g0
        d1[...] = g1
        d2[...] = g2

    return pl.pallas_call(body, name=name, out_shape=[jax.ShapeDtypeStruct((G, P), F32)] * 2 + [jax.ShapeDtypeStruct((G, 1), F32)])(
        a_re, a_im, log_dt, *cts)


def _s5_bbar(name, fr, fi, br, bi):
    return _rows(name, lambda fr, fi, br, bi: (fr * br - fi * bi, fr * bi + fi * br),
                 [(fr, 'r', 1, 0), (fi, 'r', 1, 0), (br, 'r', S5_GROUP, 0), (bi, 'r', S5_GROUP, 0)],
                 [('r', S5_GROUP, F32)] * 2, 2048)


def _s5_bbar_bwd(name, fr, fi, br, bi, dr, di):
    def fn(fr, fi, br, bi, dr, di):
        return (fr * dr + fi * di, fr * di - fi * dr,
                jnp.sum(br * dr + bi * di, axis=1, keepdims=True), jnp.sum(br * di - bi * dr, axis=1, keepdims=True))

    return _rows(name, fn, [(fr, 'r', 1, 0), (fi, 'r', 1, 0)] + [(x, 'r', S5_GROUP, 0) for x in (br, bi, dr, di)],
                 [('r', S5_GROUP, F32)] * 2 + [('r', 1, F32)] * 2, 2048)


def _scan_mults(m_ref, ar, ai, reverse):
    L = ar.shape[1]
    row = lax.broadcasted_iota(jnp.int32, (SUB, L), 0)
    if reverse:
        row = (SUB - 1) - row
    ar = jnp.broadcast_to(ar, (SUB, L))
    ai = jnp.broadcast_to(ai, (SUB, L))
    a2r, a2i = ar * ar - ai * ai, 2.0 * ar * ai
    a4r, a4i = a2r * a2r - a2i * a2i, 2.0 * a2r * a2i
    zero = jnp.zeros((SUB, L), F32)
    for s, (pr, pi, d) in enumerate(((ar, ai, 1), (a2r, a2i, 2), (a4r, a4i, 4))):
        m_ref[2 * s] = jnp.where(row >= d, pr, zero)
        m_ref[2 * s + 1] = jnp.where(row >= d, pi, zero)
    pr, pi = ar, ai
    for bit, (qr, qi) in ((1, (ar, ai)), (2, (a2r, a2i)), (4, (a4r, a4i))):
        on = (row & bit) != 0
        nr, ni = pr * qr - pi * qi, pr * qi + pi * qr
        pr, pi = jnp.where(on, nr, pr), jnp.where(on, ni, pi)
    m_ref[6] = pr
    m_ref[7] = pi


def _scan8(xr, xi, m_ref, cr, ci, reverse):
    for s, d in enumerate((1, 2, 4)):
        sh = (SUB - d) if reverse else d
        sr, si = pltpu.roll(xr, sh, 0), pltpu.roll(xi, sh, 0)
        mr, mi = m_ref[2 * s], m_ref[2 * s + 1]
        xr, xi = xr + mr * sr - mi * si, xi + mr * si + mi * sr
    pr, pi = m_ref[6], m_ref[7]
    return xr + pr * cr - pi * ci, xi + pr * ci + pi * cr


def _blockdiag_fill(bd_ref, c_ref, C, L):
    P = S5_STATE
    bd_ref[...] = jnp.zeros_like(bd_ref)
    for g in range(L // P):
        for half in (0, L):
            bd_ref[g * C:(g + 1) * C, half + g * P:half + (g + 1) * P] = c_ref[:, half + g * P:half + (g + 1) * P]


def _blockdiag_take(out_ref, dense_ref, C, L):
    P = S5_STATE
    for g in range(L // P):
        for half in (0, L):
            out_ref[:, half + g * P:half + (g + 1) * P] = dense_ref[g * C:(g + 1) * C, half + g * P:half + (g + 1) * P]


def _s5_fwd(name, proj, bbd, cbd, abar_r, abar_i, dskip, E):
    T = proj.shape[0]
    NC, C, L2 = bbd.shape
    L = L2 // 2
    CH = GROUPS_PER_CHUNK * C
    tT = _t(256, T)
    nt = (((1,), (1,)), ((), ()))

    def body(u_ref, bc_ref, cc_ref, ar_ref, ai_ref, d_ref, y_ref, g_ref, h_ref, bu, carry, mult, b_bd, c_bd):
        tb = pl.program_id(1)

        @pl.when(tb == 0)
        def _():
            carry[...] = jnp.zeros_like(carry)
            _blockdiag_fill(b_bd, bc_ref, C, L)
            _blockdiag_fill(c_bd, cc_ref, C, L)

        u = u_ref[...]
        bu[...] = jnp.dot(u.astype(BF16), b_bd[...], preferred_element_type=F32)
        _scan_mults(mult, ar_ref[...], ai_ref[...], False)

        def step(jb, c):
            cr, ci = c
            r0 = pl.multiple_of(jb * SUB, SUB)
            hr, hi = _scan8(bu[pl.ds(r0, SUB), 0:L], bu[pl.ds(r0, SUB), L:L2], mult, cr, ci, False)
            h_ref[pl.ds(r0, SUB), 0:L] = hr
            h_ref[pl.ds(r0, SUB), L:L2] = hi
            return (jnp.broadcast_to(hr[SUB - 1:SUB, :], (SUB, L)), jnp.broadcast_to(hi[SUB - 1:SUB, :], (SUB, L)))

        cr, ci = lax.fori_loop(0, tT // SUB, step, (carry[:, 0:L], carry[:, L:L2]))
        carry[:, 0:L] = cr
        carry[:, L:L2] = ci
        y1 = lax.dot_general(h_ref[...].astype(BF16), c_bd[...], nt, preferred_element_type=F32) + d_ref[...] * u
        y_ref[...] = y1
        g_ref[...] = _gelu(y1).astype(BF16)

    return pl.pallas_call(
        body, name=name, grid=(NC, T // tT),
        in_specs=[_bs((tT, CH), lambda c, t: (t, c)), _bs((None, C, L2), lambda c, t: (c, 0, 0)),
                  _bs((None, C, L2), lambda c, t: (c, 0, 0)), _bs((None, 1, L), lambda c, t: (c, 0, 0)),
                  _bs((None, 1, L), lambda c, t: (c, 0, 0)), _bs((1, CH), lambda c, t: (0, c))],
        out_specs=[_bs((tT, CH), lambda c, t: (t, c)), _bs((tT, CH), lambda c, t: (t, c)),
                   _bs((None, tT, L2), lambda c, t: (c, t, 0))],
        out_shape=[jax.ShapeDtypeStruct((T, E), F32), jax.ShapeDtypeStruct((T, E), BF16),
                   jax.ShapeDtypeStruct((NC, T, L2), F32)],
        scratch_shapes=[pltpu.VMEM((tT, L2), F32), pltpu.VMEM((SUB, L2), F32), pltpu.VMEM((8, SUB, L), F32),
                        pltpu.VMEM((CH, L2), BF16), pltpu.VMEM((CH, L2), BF16)],
        compiler_params=_params(("parallel", "arbitrary")),
    )(proj, bbd, cbd, abar_r, abar_i, dskip)


def _s5_bwd(name, dy1, proj, hs, bbd, cbd, abar_r, abar_i, dskip, E):
    T = proj.shape[0]
    NC, C, L2 = bbd.shape
    L = L2 // 2
    CH = GROUPS_PER_CHUNK * C
    tT = _t(256, T)
    nT = T // tT
    tn = (((0,), (0,)), ((), ()))
    nt = (((1,), (1,)), ((), ()))

    def body(dy_ref, u_ref, h_ref, bc_ref, cc_ref, ar_ref, ai_ref, d_ref, du_ref, db_ref, dc_ref, da_ref, dd_ref,
             gb, carry, mult, b_bd, c_bd, db_acc, dc_acc):
        tb = pl.program_id(1)

        @pl.when(tb == 0)
        def _():
            carry[...] = jnp.zeros_like(carry)
            db_acc[...] = jnp.zeros_like(db_acc)
            dc_acc[...] = jnp.zeros_like(dc_acc)
            da_ref[...] = jnp.zeros_like(da_ref)
            dd_ref[...] = jnp.zeros_like(dd_ref)
            _blockdiag_fill(b_bd, bc_ref, C, L)
            _blockdiag_fill(c_bd, cc_ref, C, L)

        dy = dy_ref[...]
        u = u_ref[...]
        dy16 = dy.astype(BF16)
        dc_acc[...] += lax.dot_general(dy16, h_ref[...].astype(BF16), tn, preferred_element_type=F32)
        gb[...] = jnp.dot(dy16, c_bd[...], preferred_element_type=F32)
        _scan_mults(mult, ar_ref[...], -ai_ref[...], True)
        row = lax.broadcasted_iota(jnp.int32, (SUB, L), 0)
        nblk = tT // SUB

        def step(jj, c):
            cr, ci, sr, si = c
            r0 = pl.multiple_of((nblk - 1 - jj) * SUB, SUB)
            gr, gi = _scan8(gb[pl.ds(r0, SUB), 0:L], gb[pl.ds(r0, SUB), L:L2], mult, cr, ci, True)
            gb[pl.ds(r0, SUB), 0:L] = gr
            gb[pl.ds(r0, SUB), L:L2] = gi
            nr = jnp.where(row == SUB - 1, cr, pltpu.roll(gr, SUB - 1, 0))
            ni = jnp.where(row == SUB - 1, ci, pltpu.roll(gi, SUB - 1, 0))
            hr, hi = h_ref[pl.ds(r0, SUB), 0:L], h_ref[pl.ds(r0, SUB), L:L2]
            sr = sr + nr * hr + ni * hi
            si = si + ni * hr - nr * hi
            return (jnp.broadcast_to(gr[0:1, :], (SUB, L)), jnp.broadcast_to(gi[0:1, :], (SUB, L)), sr, si)

        z = jnp.zeros((SUB, L), F32)
        cr, ci, sr, si = lax.fori_loop(0, nblk, step, (carry[:, 0:L], carry[:, L:L2], z, z))
        carry[:, 0:L] = cr
        carry[:, L:L2] = ci
        da_ref[:, 0:L] += sr
        da_ref[:, L:L2] += si
        g16 = gb[...].astype(BF16)
        du = lax.dot_general(g16, b_bd[...], nt, preferred_element_type=F32) + d_ref[...] * dy
        du_ref[...] = du.astype(BF16)
        db_acc[...] += lax.dot_general(u.astype(BF16), g16, tn, preferred_element_type=F32)
        dd_ref[...] += _colsum(dy * u)

        @pl.when(tb == nT - 1)
        def _():
            _blockdiag_take(db_ref, db_acc, C, L)
            _blockdiag_take(dc_ref, dc_acc, C, L)

    rev = lambda c, t: (nT - 1 - t, c)
    return pl.pallas_call(
        body, name=name, grid=(NC, nT),
        in_specs=[_bs((tT, CH), rev), _bs((tT, CH), rev), _bs((None, tT, L2), lambda c, t: (c, nT - 1 - t, 0)),
                  _bs((None, C, L2), lambda c, t: (c, 0, 0)), _bs((None, C, L2), lambda c, t: (c, 0, 0)),
                  _bs((None, 1, L), lambda c, t: (c, 0, 0)), _bs((None, 1, L), lambda c, t: (c, 0, 0)),
                  _bs((1, CH), lambda c, t: (0, c))],
        out_specs=[_bs((tT, CH), rev), _bs((None, C, L2), lambda c, t: (c, 0, 0)), _bs((None, C, L2), lambda c, t: (c, 0, 0)),
                   _bs((None, SUB, L2), lambda c, t: (c, 0, 0)), _bs((None, 1, CH), lambda c, t: (c, 0, 0))],
        out_shape=[jax.ShapeDtypeStruct((T, E), BF16), jax.ShapeDtypeStruct((NC, C, L2), F32),
                   jax.ShapeDtypeStruct((NC, C, L2), F32), jax.ShapeDtypeStruct((NC, SUB, L2), F32),
                   jax.ShapeDtypeStruct((NC, 1, CH), F32)],
        scratch_shapes=[pltpu.VMEM((tT, L2), F32), pltpu.VMEM((SUB, L2), F32), pltpu.VMEM((8, SUB, L), F32),
                        pltpu.VMEM((CH, L2), BF16), pltpu.VMEM((CH, L2), BF16), pltpu.VMEM((CH, L2), F32), pltpu.VMEM((CH, L2), F32)],
        compiler_params=_params(("parallel", "arbitrary")),
    )(dy1, proj, hs, bbd, cbd, abar_r, abar_i, dskip)


def _compact(v, NC):
    G, P, C = v.shape
    return jnp.transpose(v.reshape(NC, G // NC, P, C), (0, 3, 1, 2)).reshape(NC, C, (G // NC) * P)


def _uncompact(d, G):
    NC, C, L = d.shape
    gpc = G // NC
    return jnp.transpose(d.reshape(NC, C, gpc, L // gpc), (0, 2, 3, 1)).reshape(G, L // gpc, C)


def _cum_rows(name, x, bias, reverse, log_sig):
    T, L = x.shape

    def body(x_ref, b_ref, o_ref):
        row = lax.broadcasted_iota(jnp.int32, (SUB, L), 0)
        if reverse:
            row = (SUB - 1) - row
        nblk = T // SUB

        def step(jj, c):
            r0 = pl.multiple_of(((nblk - 1 - jj) if reverse else jj) * SUB, SUB)
            v = x_ref[pl.ds(r0, SUB), :] + b_ref[...]
            if log_sig:
                v = _log_sigmoid(v)
            for d in (1, 2, 4):
                v = v + jnp.where(row >= d, pltpu.roll(v, (SUB - d) if reverse else d, 0), 0.0)
            v = v + c
            o_ref[pl.ds(r0, SUB), :] = v
            e = 0 if reverse else SUB - 1
            return jnp.broadcast_to(v[e:e + 1, :], (SUB, L))

        lax.fori_loop(0, nblk, step, jnp.zeros((SUB, L), F32))

    return pl.pallas_call(body, name=name, out_shape=jax.ShapeDtypeStruct((T, L), F32),
                          compiler_params=pltpu.CompilerParams(vmem_limit_bytes=VMEM_LIMIT))(x, bias)


def _qk_norm(name, proj, wq, wk, H):
    T = proj.shape[0]
    Dh = FOX_HEAD_DIM
    tT = _t(512, T)
    HB = math.gcd(NORM_HEADS, H)

    def body(q_ref, k_ref, wq_ref, wk_ref, qn_ref, kn_ref):
        for hh in range(HB):
            lanes = slice(hh * Dh, (hh + 1) * Dh)
            q, k = q_ref[:, lanes], k_ref[:, lanes]
            qn_ref[:, lanes] = ((q * _rms(q)) * wq_ref[...]).astype(BF16)
            kn_ref[:, lanes] = ((k * _rms(k)) * wk_ref[...]).astype(BF16)

    blk = lambda off: _bs((tT, HB * Dh), lambda t, h: (t, h + off))
    return pl.pallas_call(
        body, name=name, grid=(T // tT, H // HB),
        in_specs=[blk(0), blk(H // HB), _bs((1, Dh), lambda t, h: (0, 0)), _bs((1, Dh), lambda t, h: (0, 0))],
        out_specs=[blk(0), blk(0)], out_shape=[jax.ShapeDtypeStruct((T, H * Dh), BF16)] * 2,
        compiler_params=_params(("parallel", "parallel")))(proj, proj, wq, wk)


def _qk_norm_bwd(name, proj, wq, wk, dqn, dkn, H):
    T = proj.shape[0]
    Dh = FOX_HEAD_DIM
    tT = _t(512, T)
    HB = math.gcd(NORM_HEADS, H)

    def body(q_ref, k_ref, wq_ref, wk_ref, dqn_ref, dkn_ref, dq_ref, dk_ref, dwq_ref, dwk_ref):
        @pl.when((pl.program_id(0) == 0) & (pl.program_id(1) == 0))
        def _():
            dwq_ref[...] = jnp.zeros_like(dwq_ref)
            dwk_ref[...] = jnp.zeros_like(dwk_ref)

        for hh in range(HB):
            lanes = slice(hh * Dh, (hh + 1) * Dh)
            dq, tq = _rms_bwd(q_ref[:, lanes], wq_ref[...], dqn_ref[:, lanes])
            dk, tk = _rms_bwd(k_ref[:, lanes], wk_ref[...], dkn_ref[:, lanes])
            dq_ref[:, lanes] = dq.astype(BF16)
            dk_ref[:, lanes] = dk.astype(BF16)
            dwq_ref[...] += _colsum(tq)
            dwk_ref[...] += _colsum(tk)

    blk = lambda off: _bs((tT, HB * Dh), lambda t, h: (t, h + off))
    one = _bs((1, Dh), lambda t, h: (0, 0))
    return pl.pallas_call(
        body, name=name, grid=(T // tT, H // HB),
        in_specs=[blk(0), blk(H // HB), one, one, blk(0), blk(0)],
        out_specs=[blk(0), blk(0), one, one],
        out_shape=[jax.ShapeDtypeStruct((T, H * Dh), BF16)] * 2 + [jax.ShapeDtypeStruct((1, Dh), F32)] * 2,
        compiler_params=_params(("arbitrary", "arbitrary")))(proj, proj, wq, wk, dqn, dkn)


def _attn_fwd(name, qn, kn, proj, cum_q, cum_k, H):
    T = qn.shape[0]
    Dh = FOX_HEAD_DIM
    tq = cum_k.shape[3]
    nq = T // tq
    scale = Dh ** -0.5
    nt = (((1,), (1,)), ((), ()))

    sq = _t(ATTN_SUB, tq)
    rep = tq // LANES
    HP = ATTN_HEADS
    assert H % HP == 0 and Dh == LANES

    def body(q_ref, k_ref, v_ref, cq_ref, ck_ref, o_ref, lse_ref, m_sc, l_sc, acc_sc):
        i = pl.program_id(1)
        m_sc[...] = jnp.full_like(m_sc, NEG)
        l_sc[...] = jnp.zeros_like(l_sc)
        acc_sc[...] = jnp.zeros_like(acc_sc)
        kloc = lax.broadcasted_iota(jnp.int32, (sq, tq), 1)
        qloc = lax.broadcasted_iota(jnp.int32, (sq, tq), 0)

        def chunk(kc, masked):
            ks = pl.multiple_of(kc * tq, tq)
            for hh in range(HP):
                lanes = slice(hh * Dh, (hh + 1) * Dh)
                k = k_ref[pl.ds(ks, tq), lanes]
                v16 = v_ref[pl.ds(ks, tq), lanes].astype(BF16)
                ck = ck_ref[hh, kc]
                for r in range(tq // sq):
                    rows = pl.ds(r * sq, sq)
                    s = lax.dot_general(q_ref[rows, lanes], k, nt, preferred_element_type=F32) * scale + (jnp.tile(cq_ref[hh, rows, :], (1, rep)) - ck)
                    if masked:
                        s = jnp.where(kloc <= qloc + r * sq, s, NEG)
                    m_old = m_sc[rows, lanes]
                    m_new = jnp.maximum(m_old, jnp.max(s, axis=1, keepdims=True))
                    alpha = jnp.exp(m_old - m_new)
                    p = jnp.exp(s - jnp.tile(m_new, (1, rep)))
                    l_sc[rows, lanes] = alpha * l_sc[rows, lanes] + jnp.sum(p, axis=1, keepdims=True)
                    acc_sc[rows, lanes] = alpha * acc_sc[rows, lanes] + jnp.dot(p.astype(BF16), v16, preferred_element_type=F32)
                    m_sc[rows, lanes] = m_new

        def below(kc, c):
            chunk(kc, False)
            return c

        lax.fori_loop(0, i, below, 0)
        chunk(i, True)
        o_ref[...] = acc_sc[...] / l_sc[...]
        for hh in range(HP):
            lanes = slice(hh * Dh, (hh + 1) * Dh)
            lse_ref[hh] = m_sc[:, lanes] + jnp.log(l_sc[:, lanes])

    W2 = HP * Dh
    return pl.pallas_call(
        body, name=name, grid=(H // HP, nq),
        in_specs=[_bs((tq, W2), lambda h, i: (i, h)), _bs((T, W2), lambda h, i: (0, h)), _bs((T, W2), lambda h, i: (0, 2 * (H // HP) + h)),
                  _bs((HP, tq, LANES), lambda h, i: (h, i, 0)), _bs((HP, nq, 1, tq), lambda h, i: (h, 0, 0, 0))],
        out_specs=[_bs((tq, W2), lambda h, i: (i, h)), _bs((HP, tq, LANES), lambda h, i: (h, i, 0))],
        out_shape=[jax.ShapeDtypeStruct((T, H * Dh), F32), jax.ShapeDtypeStruct((H, T, LANES), F32)],
        scratch_shapes=[pltpu.VMEM((tq, W2), F32), pltpu.VMEM((tq, W2), F32), pltpu.VMEM((tq, W2), F32)],
        compiler_params=_params(("parallel", "parallel")))(qn, kn, proj, cum_q, cum_k)


def _attn_bwd(name, qn, kn, proj, do, o, lse, cum_q, cum_k, H):
    T = qn.shape[0]
    Dh = FOX_HEAD_DIM
    tq = cum_k.shape[3]
    nq = T // tq
    scale = Dh ** -0.5
    nt = (((1,), (1,)), ((), ()))
    tn = (((0,), (0,)), ((), ()))
    assert H <= LANES

    sq = _t(ATTN_SUB, tq)
    rep = tq // LANES
    HP = ATTN_HEADS
    W2 = HP * Dh
    assert H % HP == 0 and Dh == LANES

    def body(q_ref, k_ref, v_ref, do_ref, o_ref, lse_ref, cq_ref, ck_ref, dq_ref, dk_ref, dv_ref, dcq_ref, dck_ref,
             delta, cql, dk_sc, dv_sc, dck_sc):
        h, j = pl.program_id(0), pl.program_id(1)

        @pl.when((h == 0) & (j == 0))
        def _():
            dcq_ref[...] = jnp.zeros_like(dcq_ref)

        @pl.when(j == 0)
        def _():
            dq_ref[...] = jnp.zeros_like(dq_ref)
            for hh in range(HP):
                lanes = slice(hh * Dh, (hh + 1) * Dh)
                delta[hh] = jnp.broadcast_to(jnp.sum(do_ref[:, lanes] * o_ref[:, lanes], axis=1, keepdims=True), (T, LANES))
            cql[...] = cq_ref[...] - lse_ref[...]

        lane_id = lax.broadcasted_iota(jnp.int32, (sq, LANES), 1)
        dk_sc[...] = jnp.zeros_like(dk_sc)
        dv_sc[...] = jnp.zeros_like(dv_sc)
        dck_sc[...] = jnp.zeros_like(dck_sc)
        kloc = lax.broadcasted_iota(jnp.int32, (sq, tq), 1)
        qloc = lax.broadcasted_iota(jnp.int32, (sq, tq), 0)

        def qblk(i, masked):
            for hh in range(HP):
                lanes = slice(hh * Dh, (hh + 1) * Dh)
                k = k_ref[:, lanes]
                v16 = v_ref[:, lanes].astype(BF16)
                ck = ck_ref[hh]
                for r in range(tq // sq):
                    rows = pl.ds(pl.multiple_of(i * tq + r * sq, sq), sq)
                    q = q_ref[rows, lanes]
                    do16 = do_ref[rows, lanes].astype(BF16)
                    e = lax.dot_general(q, k, nt, preferred_element_type=F32) * scale + (jnp.tile(cql[hh, rows, :], (1, rep)) - ck)
                    p = jnp.exp(e)
                    if masked:
                        p = jnp.where(kloc <= qloc + r * sq, p, 0.0)
                    dv_sc[:, lanes] += lax.dot_general(p.astype(BF16), do16, tn, preferred_element_type=F32)
                    dp = lax.dot_general(do16, v16, nt, preferred_element_type=F32)
                    ds = p * (dp - jnp.tile(delta[hh, rows, :], (1, rep)))
                    ds16 = ds.astype(BF16)
                    dk_sc[:, lanes] += lax.dot_general(ds16, q, tn, preferred_element_type=F32)
                    dq_ref[rows, lanes] += jnp.dot(ds16, k, preferred_element_type=F32) * scale
                    dcq_ref[rows, :] += jnp.where(lane_id == h * HP + hh, jnp.sum(ds, axis=1, keepdims=True), 0.0)
                    dck_sc[hh] += jnp.sum(ds, axis=0, keepdims=True)

        def above(i, c):
            qblk(i, False)
            return c

        qblk(j, True)
        lax.fori_loop(j + 1, nq, above, 0)
        dk_ref[...] = dk_sc[...] * scale
        dv_ref[...] = dv_sc[...].astype(BF16)
        for hh in range(HP):
            dck_ref[hh] = -dck_sc[hh]

    whole = lambda off: _bs((T, W2), lambda h, j: (0, h + off))
    blk = lambda off: _bs((tq, W2), lambda h, j: (j, h + off))
    return pl.pallas_call(
        body, name=name, grid=(H // HP, nq),
        in_specs=[whole(0), blk(0), blk(2 * (H // HP)), whole(0), whole(0), _bs((HP, T, LANES), lambda h, j: (h, 0, 0)),
                  _bs((HP, T, LANES), lambda h, j: (h, 0, 0)), _bs((HP, None, 1, tq), lambda h, j: (h, j, 0, 0))],
        out_specs=[whole(0), blk(0), blk(0), _bs((T, LANES), lambda h, j: (0, 0)),
                   _bs((HP, None, 1, tq), lambda h, j: (h, j, 0, 0))],
        out_shape=[jax.ShapeDtypeStruct((T, H * Dh), F32), jax.ShapeDtypeStruct((T, H * Dh), F32), jax.ShapeDtypeStruct((T, H * Dh), BF16),
                   jax.ShapeDtypeStruct((T, LANES), F32), jax.ShapeDtypeStruct((H, nq, 1, tq), F32)],
        scratch_shapes=[pltpu.VMEM((HP, T, LANES), F32), pltpu.VMEM((HP, T, LANES), F32), pltpu.VMEM((tq, W2), F32), pltpu.VMEM((tq, W2), F32),
                        pltpu.VMEM((HP, 1, tq), F32)],
        compiler_params=_params(("arbitrary", "arbitrary")))(qn, kn, proj, do, o, lse, cum_q, cum_k)


def _pool_fwd(name, proj, E):
    T = proj.shape[0]
    PG = len(POOL_WINDOWS)
    PD = E // PG
    tT = _t(256, T)
    hb = tT // POOL_HALO

    def body(u_ref, halo_ref, o_ref, buf):
        g, tb = pl.program_id(0), pl.program_id(1)
        u = u_ref[...]
        buf[pl.ds(POOL_HALO, tT), :] = u
        buf[pl.ds(0, POOL_HALO), :] = jnp.where(tb == 0, 0.0, halo_ref[...])
        t = tb * tT + lax.broadcasted_iota(jnp.int32, (tT, 1), 0)
        for gi, w in enumerate(POOL_WINDOWS):
            @pl.when(g == gi)
            def _():
                acc = u
                for d in range(1, w):
                    acc = acc + buf[pl.ds(POOL_HALO - d, tT), :]
                cnt = jnp.minimum(t + 1, w).astype(F32)
                o_ref[...] = (acc / cnt - u).astype(BF16)

    return pl.pallas_call(
        body, name=name, grid=(PG, T // tT),
        in_specs=[_bs((tT, PD), lambda g, t: (t, g)), _bs((POOL_HALO, PD), lambda g, t: (jnp.maximum(t * hb - 1, 0), g))],
        out_specs=_bs((tT, PD), lambda g, t: (t, g)), out_shape=jax.ShapeDtypeStruct((T, E), BF16),
        scratch_shapes=[pltpu.VMEM((tT + POOL_HALO, PD), F32)],
        compiler_params=_params(("parallel", "parallel")))(proj, proj)


def _pool_bwd(name, dpm, E):
    T = dpm.shape[0]
    PG = len(POOL_WINDOWS)
    PD = E // PG
    tT = _t(256, T)
    hb = tT // POOL_HALO
    nT = T // tT

    def body(d_ref, halo_ref, o_ref, buf):
        g, tb = pl.program_id(0), pl.program_id(1)
        d = d_ref[...]
        t = tb * tT + lax.broadcasted_iota(jnp.int32, (tT, 1), 0)
        th = (tb + 1) * tT + lax.broadcasted_iota(jnp.int32, (POOL_HALO, 1), 0)
        for gi, w in enumerate(POOL_WINDOWS):
            @pl.when(g == gi)
            def _():
                dn = d / jnp.minimum(t + 1, w).astype(F32)
                buf[pl.ds(0, tT), :] = dn
                buf[pl.ds(tT, POOL_HALO), :] = jnp.where(tb == nT - 1, 0.0, halo_ref[...] / jnp.minimum(th + 1, w).astype(F32))
                acc = dn
                for s in range(1, w):
                    acc = acc + buf[pl.ds(s, tT), :]
                o_ref[...] = (acc - d).astype(BF16)

    return pl.pallas_call(
        body, name=name, grid=(PG, nT),
        in_specs=[_bs((tT, PD), lambda g, t: (t, g)), _bs((POOL_HALO, PD), lambda g, t: (jnp.minimum((t + 1) * hb, T // POOL_HALO - 1), g))],
        out_specs=_bs((tT, PD), lambda g, t: (t, g)), out_shape=jax.ShapeDtypeStruct((T, E), BF16),
        scratch_shapes=[pltpu.VMEM((tT + POOL_HALO, PD), F32)],
        compiler_params=_params(("parallel", "parallel")))(dpm, dpm)


def _coords():
    x, y, c = lax.axis_index("x"), lax.axis_index("y"), lax.axis_index("c")
    chips = [(1 - x, y), (x, 1 - y), (1 - x, 1 - y)]
    return x, y, c, 2 * x + y, (x, y, 1 - c), chips


def _chip_allgather(name, bufs):
    n = len(bufs)

    def body(*refs):
        outs = refs[n:2 * n]
        send, recv, fsend, frecv = refs[2 * n:]
        x, y, c, p, sib, chips = _coords()

        def direct(t, j, chip):
            return pltpu.make_async_remote_copy(src_ref=outs[t].at[p, c], dst_ref=outs[t].at[p, c], send_sem=send.at[t, j],
                                                recv_sem=recv.at[t, j], device_id=(*chip, c), device_id_type=MESH)

        def landed(t, j, chip):
            blk = outs[t].at[2 * chip[0] + chip[1], c]
            return pltpu.make_async_remote_copy(src_ref=blk, dst_ref=blk, send_sem=send.at[t, j],
                                                recv_sem=recv.at[t, j], device_id=(*chip, c), device_id_type=MESH)

        def passed(t, j, chip, half):
            blk = outs[t].at[2 * chip[0] + chip[1], half]
            return pltpu.make_async_remote_copy(src_ref=blk, dst_ref=blk, send_sem=fsend.at[t, j], recv_sem=frecv.at[t, j],
                                                device_id=sib, device_id_type=MESH)

        first = [direct(t, j, chip) for t in range(n) for j, chip in enumerate(chips)]
        for cp in first:
            cp.start()
        fwd = []
        for j, chip in enumerate(chips):
            for t in range(n):
                landed(t, j, chip).wait_recv()
                f = passed(t, j, chip, c)
                f.start()
                fwd.append(f)
        for j, chip in enumerate(chips):
            for t in range(n):
                passed(t, j, chip, 1 - c).wait_recv()
        for cp in first + fwd:
            cp.wait_send()

    return pl.pallas_call(
        body, name=name, in_specs=[ANY] * n, out_specs=[ANY] * n,
        out_shape=[jax.ShapeDtypeStruct(a.shape, a.dtype) for a in bufs],
        input_output_aliases={t: t for t in range(n)},
        scratch_shapes=[pltpu.SemaphoreType.DMA((n, 3))] * 4,
    )(*bufs)


SEM = pl.BlockSpec(memory_space=pltpu.SEMAPHORE)
TOKEN = jax.ShapeDtypeStruct((SUB, LANES), F32)


def _split_params():
    return pltpu.CompilerParams(has_side_effects=pltpu.SideEffectType.DATAFLOW_SIDE_EFFECTING)


def _struct(a):
    return jax.ShapeDtypeStruct(a.shape, a.dtype)


def _gather_start(name, bufs, deps):
    n, nd = len(bufs), len(deps)

    def body(*refs):
        outs = refs[n + nd:2 * n + nd]
        send, recv, token = refs[2 * n + nd:]
        x, y, c, p, sib, chips = _coords()
        for t in range(n):
            for j, chip in enumerate(chips):
                pltpu.make_async_remote_copy(src_ref=outs[t].at[p, c], dst_ref=outs[t].at[p, c], send_sem=send.at[3 * t + j],
                                             recv_sem=recv.at[3 * t + j], device_id=(*chip, c), device_id_type=MESH).start()
        token[...] = jnp.zeros_like(token)

    res = pl.pallas_call(
        body, name=name, in_specs=[ANY] * (n + nd), out_specs=[ANY] * n + [SEM, SEM, pl.BlockSpec(memory_space=pltpu.VMEM)],
        out_shape=[_struct(a) for a in bufs] + [pltpu.SemaphoreType.DMA((3 * n,)), pltpu.SemaphoreType.DMA((3 * n,)), TOKEN],
        input_output_aliases={t: t for t in range(n)}, compiler_params=_split_params(),
    )(*bufs, *deps)
    return list(res[:n]), res[n], res[n + 1], res[n + 2]


def _gather_wait(name, bufs, send, recv, after):
    n = len(bufs)

    def body(*refs):
        send_r, recv_r = refs[n], refs[n + 1]
        outs = refs[n + 3:2 * n + 3]
        x, y, c, p, sib, chips = _coords()
        for t in range(n):
            for j, chip in enumerate(chips):
                cp = pltpu.make_async_remote_copy(src_ref=outs[t].at[p, c], dst_ref=outs[t].at[2 * chip[0] + chip[1], c], send_sem=send_r.at[3 * t + j],
                                                  recv_sem=recv_r.at[3 * t + j], device_id=(*chip, c), device_id_type=MESH)
                cp.wait_send()
                cp.wait_recv()

    return list(pl.pallas_call(
        body, name=name, in_specs=[ANY] * n + [SEM, SEM, ANY], out_specs=[ANY] * n, out_shape=[_struct(a) for a in bufs],
        input_output_aliases={t: t for t in range(n)}, compiler_params=_split_params(),
    )(*bufs, send, recv, after))


def _gather_forward(name, bufs):
    n = len(bufs)

    def body(*refs):
        outs = refs[n:2 * n]
        fsend, frecv = refs[2 * n:]
        x, y, c, p, sib, chips = _coords()

        def passed(t, j, chip, half):
            blk = outs[t].at[2 * chip[0] + chip[1], half]
            return pltpu.make_async_remote_copy(src_ref=blk, dst_ref=blk, send_sem=fsend.at[t, j], recv_sem=frecv.at[t, j],
                                                device_id=sib, device_id_type=MESH)

        fwd = [passed(t, j, chip, c) for t in range(n) for j, chip in enumerate(chips)]
        for cp in fwd:
            cp.start()
        for t in range(n):
            for j, chip in enumerate(chips):
                passed(t, j, chip, 1 - c).wait_recv()
        for cp in fwd:
            cp.wait_send()

    return list(pl.pallas_call(
        body, name=name, in_specs=[ANY] * n, out_specs=[ANY] * n, out_shape=[_struct(a) for a in bufs],
        input_output_aliases={t: t for t in range(n)}, scratch_shapes=[pltpu.SemaphoreType.DMA((n, 3))] * 2,
    )(*bufs))


def _relations():
    x, y, c = lax.axis_index("x"), lax.axis_index("y"), lax.axis_index("c")
    out = []
    for code in range(1, 8):
        tx = 1 - x if code & 4 else x
        ty = 1 - y if code & 2 else y
        tc = 1 - c if code & 1 else c
        out.append((code - 1, (tx, ty, tc), 2 * tx + ty, tc))
    return out


def _full_exchange_start(name, parts):
    n = len(parts)
    lands = [lax.empty((7,) + a.shape[2:], a.dtype) for a in parts]

    def body(*refs):
        src, dst = refs[2 * n:3 * n], refs[3 * n:4 * n]
        send, recv, token = refs[4 * n:]
        for t in range(n):
            for k, dev, q, half in _relations():
                pltpu.make_async_remote_copy(src_ref=src[t].at[half, q], dst_ref=dst[t].at[k], send_sem=send.at[7 * t + k],
                                             recv_sem=recv.at[7 * t + k], device_id=dev, device_id_type=MESH).start()
        token[...] = jnp.zeros_like(token)

    res = pl.pallas_call(
        body, name=name, in_specs=[ANY] * (2 * n), out_specs=[ANY] * (2 * n) + [SEM, SEM, pl.BlockSpec(memory_space=pltpu.VMEM)],
        out_shape=[_struct(a) for a in parts + lands] + [pltpu.SemaphoreType.DMA((7 * n,)), pltpu.SemaphoreType.DMA((7 * n,)), TOKEN],
        input_output_aliases={t: t for t in range(2 * n)}, compiler_params=_split_params(),
    )(*parts, *lands)
    return list(res[:n]), list(res[n:2 * n]), res[2 * n], res[2 * n + 1], res[2 * n + 2]


def _full_exchange_wait(name, parts, lands, send, recv, after):
    n = len(parts)

    def body(*refs):
        send_r, recv_r = refs[2 * n], refs[2 * n + 1]
        src, dst = refs[2 * n + 3:3 * n + 3], refs[3 * n + 3:4 * n + 3]
        for t in range(n):
            for k, dev, q, half in _relations():
                cp = pltpu.make_async_remote_copy(src_ref=src[t].at[half, q], dst_ref=dst[t].at[k], send_sem=send_r.at[7 * t + k],
                                                  recv_sem=recv_r.at[7 * t + k], device_id=dev, device_id_type=MESH)
                cp.wait_send()
                cp.wait_recv()

    res = pl.pallas_call(
        body, name=name, in_specs=[ANY] * (2 * n) + [SEM, SEM, ANY], out_specs=[ANY] * (2 * n),
        out_shape=[_struct(a) for a in parts + lands], input_output_aliases={t: t for t in range(2 * n)},
        compiler_params=_split_params(),
    )(*parts, *lands, send, recv, after)
    return list(res[:n]), list(res[n:])


def _chip_exchange_start(name, sums):
    n = len(sums)
    lands = [lax.empty((3,) + a.shape[1:], a.dtype) for a in sums]

    def body(*refs):
        src, dst = refs[2 * n:3 * n], refs[3 * n:4 * n]
        send, recv, token = refs[4 * n:]
        x, y, c, p, sib, chips = _coords()
        for t in range(n):
            for j, chip in enumerate(chips):
                pltpu.make_async_remote_copy(src_ref=src[t].at[2 * chip[0] + chip[1]], dst_ref=dst[t].at[j], send_sem=send.at[3 * t + j],
                                             recv_sem=recv.at[3 * t + j], device_id=(*chip, c), device_id_type=MESH).start()
        token[...] = jnp.zeros_like(token)

    res = pl.pallas_call(
        body, name=name, in_specs=[ANY] * (2 * n), out_specs=[ANY] * (2 * n) + [SEM, SEM, pl.BlockSpec(memory_space=pltpu.VMEM)],
        out_shape=[_struct(a) for a in sums + lands] + [pltpu.SemaphoreType.DMA((3 * n,)), pltpu.SemaphoreType.DMA((3 * n,)), TOKEN],
        input_output_aliases={t: t for t in range(2 * n)}, compiler_params=_split_params(),
    )(*sums, *lands)
    return list(res[:n]), list(res[n:2 * n]), res[2 * n], res[2 * n + 1], res[2 * n + 2]


def _chip_exchange_wait(name, sums, lands, send, recv, after):
    n = len(sums)

    def body(*refs):
        send_r, recv_r = refs[2 * n], refs[2 * n + 1]
        src, dst = refs[2 * n + 3:3 * n + 3], refs[3 * n + 3:4 * n + 3]
        x, y, c, p, sib, chips = _coords()
        for t in range(n):
            for j, chip in enumerate(chips):
                cp = pltpu.make_async_remote_copy(src_ref=src[t].at[2 * chip[0] + chip[1]], dst_ref=dst[t].at[j], send_sem=send_r.at[3 * t + j],
                                                  recv_sem=recv_r.at[3 * t + j], device_id=(*chip, c), device_id_type=MESH)
                cp.wait_send()
                cp.wait_recv()

    res = pl.pallas_call(
        body, name=name, in_specs=[ANY] * (2 * n) + [SEM, SEM, ANY], out_specs=[ANY] * (2 * n),
        out_shape=[_struct(a) for a in sums + lands], input_output_aliases={t: t for t in range(2 * n)},
        compiler_params=_split_params(),
    )(*sums, *lands, send, recv, after)
    return list(res[:n]), list(res[n:])


def _pair_exchange(name, parts):
    n = len(parts)

    def body(*refs):
        ins, outs = refs[:n], refs[n:2 * n]
        send, recv = refs[2 * n:]
        x, y, c, p, sib, chips = _coords()
        cps = [pltpu.make_async_remote_copy(src_ref=ins[t].at[1 - c], dst_ref=outs[t], send_sem=send.at[t], recv_sem=recv.at[t],
                                            device_id=sib, device_id_type=MESH) for t in range(n)]
        for cp in cps:
            cp.start()
        for cp in cps:
            cp.wait()

    return pl.pallas_call(
        body, name=name, in_specs=[ANY] * n, out_specs=[ANY] * n,
        out_shape=[jax.ShapeDtypeStruct(a.shape[1:], a.dtype) for a in parts],
        scratch_shapes=[pltpu.SemaphoreType.DMA((n,))] * 2,
    )(*parts)


def _chip_exchange(name, sums):
    n = len(sums)

    def body(*refs):
        ins, outs = refs[:n], refs[n:2 * n]
        send, recv = refs[2 * n:]
        x, y, c, p, sib, chips = _coords()
        cps = [pltpu.make_async_remote_copy(src_ref=ins[t].at[2 * chip[0] + chip[1]], dst_ref=outs[t].at[j], send_sem=send.at[t, j],
                                            recv_sem=recv.at[t, j], device_id=(*chip, c), device_id_type=MESH)
               for t in range(n) for j, chip in enumerate(chips)]
        for cp in cps:
            cp.start()
        for cp in cps:
            cp.wait()

    return pl.pallas_call(
        body, name=name, in_specs=[ANY] * n, out_specs=[ANY] * n,
        out_shape=[jax.ShapeDtypeStruct((3,) + a.shape[1:], a.dtype) for a in sums],
        scratch_shapes=[pltpu.SemaphoreType.DMA((n, 3))] * 2,
    )(*sums)


def _pair_share(name, bufs, items, deps=()):
    n = len(items)
    nb = len(bufs)
    nd = len(deps)

    def body(*refs):
        outs = refs[nb + nd:2 * nb + nd]
        send, recv = refs[2 * nb + nd:]
        x, y, c, p, sib, chips = _coords()

        def blk(t, half):
            o, lead = items[t]
            return outs[o].at[p if lead == 'chip' else lead, half]

        def swap(t, half):
            return pltpu.make_async_remote_copy(src_ref=blk(t, half), dst_ref=blk(t, half), send_sem=send.at[t], recv_sem=recv.at[t],
                                                device_id=sib, device_id_type=MESH)

        cps = [swap(t, c) for t in range(n)]
        for cp in cps:
            cp.start()
        for t in range(n):
            swap(t, 1 - c).wait_recv()
        for cp in cps:
            cp.wait_send()

    return list(pl.pallas_call(
        body, name=name, in_specs=[ANY] * (nb + nd), out_specs=[ANY] * nb,
        out_shape=[jax.ShapeDtypeStruct(b.shape, b.dtype) for b in bufs],
        input_output_aliases={t: t for t in range(nb)},
        scratch_shapes=[pltpu.SemaphoreType.DMA((n,))] * 2,
    )(*bufs, *deps))


def _flat2(a, lead):
    return a.reshape(a.shape[:lead] + (-1, a.shape[-1]))


def _reduce_begin(tag, parts):
    parts, lands, send, recv, token = _full_exchange_start(f"rs_start_{tag}", parts)
    return (parts, lands, send, recv), token


def _reduce_end(tag, state, after, dests, bufs, buf_shapes):
    c = lax.axis_index("c").astype(jnp.int32)
    p = (2 * lax.axis_index("x") + lax.axis_index("y")).astype(jnp.int32)
    parts, lands = _full_exchange_wait(f"rs_wait_{tag}", *state, after)

    def total(a, *others):
        s = a.astype(F32)
        for b in others:
            s = s + b.astype(F32)
        return (s,)

    for t, (mine, theirs) in enumerate(zip(parts, lands)):
        o, lead = dests[t]
        shape = buf_shapes[o]
        rows, cols = shape[2], shape[3]
        m3, t3 = mine.reshape(2 * N_CHIPS, rows, cols), theirs.reshape(7, rows, cols)
        pre = jnp.stack([c * N_CHIPS + p] + [jnp.int32(k) for k in range(7)] + [c, p if lead == 'chip' else jnp.int32(lead)])
        out = ('x', shape, F32, (None, None, 'tr', cols), lambda r, pr: (pr[9], pr[8], r, 0))
        bufs[o] = _rows(f"rs_sum_{tag}_{t}", total, [(m3, 's', cols, 0)] + [(t3, 's', cols, 1 + k) for k in range(7)], [out], 256,
                        pre=pre, into=bufs[o])[0]


def kernel(x, norm_w, out_proj, s5_in_proj, s5_a_re, s5_a_im, s5_log_dt, s5_b_re, s5_b_im, s5_c_re, s5_c_im, s5_d, s5_w_glu, s5_b_glu, fox_in_proj, fox_q_norm, fox_k_norm, fox_f_bias, pool_in_proj, pool_w_group, pool_scale, loss_target, m_norm_w, m_out_proj, m_s5_in_proj, m_s5_a_re, m_s5_a_im, m_s5_log_dt, m_s5_b_re, m_s5_b_im, m_s5_c_re, m_s5_c_im, m_s5_d, m_s5_w_glu, m_s5_b_glu, m_fox_in_proj, m_fox_q_norm, m_fox_k_norm, m_fox_f_bias, m_pool_in_proj, m_pool_w_group, m_pool_scale, v_norm_w, v_out_proj, v_s5_in_proj, v_s5_a_re, v_s5_a_im, v_s5_log_dt, v_s5_b_re, v_s5_b_im, v_s5_c_re, v_s5_c_im, v_s5_d, v_s5_w_glu, v_s5_b_glu, v_fox_in_proj, v_fox_q_norm, v_fox_k_norm, v_fox_f_bias, v_pool_in_proj, v_pool_w_group, v_pool_scale):
    weights = dict(norm_w=norm_w, out_proj=out_proj, s5_in_proj=s5_in_proj, s5_a_re=s5_a_re, s5_a_im=s5_a_im, s5_log_dt=s5_log_dt,
                   s5_b_re=s5_b_re, s5_b_im=s5_b_im, s5_c_re=s5_c_re, s5_c_im=s5_c_im, s5_d=s5_d, s5_w_glu=s5_w_glu, s5_b_glu=s5_b_glu,
                   fox_in_proj=fox_in_proj, fox_q_norm=fox_q_norm, fox_k_norm=fox_k_norm, fox_f_bias=fox_f_bias,
                   pool_in_proj=pool_in_proj, pool_w_group=pool_w_group, pool_scale=pool_scale)
    mom_m = dict(norm_w=m_norm_w, out_proj=m_out_proj, s5_in_proj=m_s5_in_proj, s5_a_re=m_s5_a_re, s5_a_im=m_s5_a_im, s5_log_dt=m_s5_log_dt,
                 s5_b_re=m_s5_b_re, s5_b_im=m_s5_b_im, s5_c_re=m_s5_c_re, s5_c_im=m_s5_c_im, s5_d=m_s5_d, s5_w_glu=m_s5_w_glu, s5_b_glu=m_s5_b_glu,
                 fox_in_proj=m_fox_in_proj, fox_q_norm=m_fox_q_norm, fox_k_norm=m_fox_k_norm, fox_f_bias=m_fox_f_bias,
                 pool_in_proj=m_pool_in_proj, pool_w_group=m_pool_w_group, pool_scale=m_pool_scale)
    mom_v = dict(norm_w=v_norm_w, out_proj=v_out_proj, s5_in_proj=v_s5_in_proj, s5_a_re=v_s5_a_re, s5_a_im=v_s5_a_im, s5_log_dt=v_s5_log_dt,
                 s5_b_re=v_s5_b_re, s5_b_im=v_s5_b_im, s5_c_re=v_s5_c_re, s5_c_im=v_s5_c_im, s5_d=v_s5_d, s5_w_glu=v_s5_w_glu, s5_b_glu=v_s5_b_glu,
                 fox_in_proj=v_fox_in_proj, fox_q_norm=v_fox_q_norm, fox_k_norm=v_fox_k_norm, fox_f_bias=v_fox_f_bias,
                 pool_in_proj=v_pool_in_proj, pool_w_group=v_pool_w_group, pool_scale=v_pool_scale)
    return _step(x, loss_target, weights, mom_m, mom_v)


BIG = ('out_proj', 's5_in_proj', 's5_w_glu', 'fox_in_proj', 'pool_in_proj', 'pool_w_group')
SMALL = ('norm_w', 's5_a_re', 's5_a_im', 's5_log_dt', 's5_b_re', 's5_b_im', 's5_c_re', 's5_c_im', 's5_d', 's5_b_glu',
         'fox_q_norm', 'fox_k_norm', 'fox_f_bias', 'pool_scale')
SMALL_SHARDED = ('s5_d', 's5_b_glu', 'pool_scale')
GROUP_AXIS_1 = ('s5_a_re', 's5_a_im', 's5_b_re', 's5_b_im', 's5_c_re', 's5_c_im')
ORDER = ('norm_w', 'out_proj', 's5_in_proj', 's5_a_re', 's5_a_im', 's5_log_dt', 's5_b_re', 's5_b_im', 's5_c_re', 's5_c_im', 's5_d',
         's5_w_glu', 's5_b_glu', 'fox_in_proj', 'fox_q_norm', 'fox_k_norm', 'fox_f_bias', 'pool_in_proj', 'pool_w_group', 'pool_scale')


def _split2(shape):
    if shape[0] % 2 == 0:
        return (2, shape[0] // 2) + tuple(shape[1:])
    assert shape[0] == 1 and shape[1] % 2 == 0
    return (2, shape[1] // 2) + tuple(shape[2:])


def _adamw_big(n, w, grads, mom_m, mom_v, delta, new_m, new_v):
    shape = w[n].shape
    if shape[-1] % LANES:
        f2 = lambda a: jnp.transpose(a.reshape(-1, shape[-1]))
        b2 = lambda a: jnp.transpose(a).reshape(shape)
    else:
        f2 = lambda a: a.reshape(-1, shape[-1])
        b2 = lambda a: a.reshape(shape)
    d_, m_, v_ = _adamw(f"adamw_{n}", f2(w[n]), f2(grads[n]), f2(mom_m[n]), f2(mom_v[n]))
    delta[n], new_m[n], new_v[n] = b2(d_), b2(m_), b2(v_)
    return d_


def _cast_weight(w, n, l, deps=()):
    p = (2 * lax.axis_index("x") + lax.axis_index("y")).astype(jnp.int32)
    a3 = w[n].reshape(w[n].shape[0], -1, w[n].shape[-1])
    layers, rows, cols = a3.shape
    out = ('x', (N_CHIPS, rows, cols), BF16, (None, 'tr', cols), lambda r, pr: (pr[0], r, 0))
    b = _rows(f"cast_{n}_{l}", lambda v: (v,), [(a3, 's', cols, 1)], [out], 256, pre=jnp.stack([p, jnp.int32(l)]), deps=deps)[0]
    return b.reshape(N_CHIPS, 2, rows // 2, cols)


def _step(x, loss_target, w, mom_m, mom_v):
    T, D = x.shape[1], x.shape[2]
    E = D
    G, P, C = w['s5_a_re'].shape[1], S5_STATE, S5_GROUP
    H = E // FOX_HEAD_DIM
    PG = len(POOL_WINDOWS)
    PD = E // PG
    NC = G // GROUPS_PER_CHUNK
    L = GROUPS_PER_CHUNK * P
    tq = _t(256, T)
    nq = T // tq

    phases = [[('s5_in_proj', 0)],
              [('s5_w_glu', 0), ('out_proj', 0)],
              [('out_proj', 1), ('fox_in_proj', 0)],
              [('out_proj', 2), ('pool_in_proj', 0), ('pool_w_group', 0), ('out_proj', 3), ('s5_in_proj', 1), ('s5_w_glu', 1)]]
    W = {}
    flight = {}

    def landed(keys, bufs):
        for k, b in zip(keys, bufs):
            W[k] = b.reshape(N_CHIPS, 2 * b.shape[2], b.shape[3])

    def take_phase(ph, after):
        bufs, send, recv, _ = flight.pop(ph)
        landed(phases[ph], _gather_forward(f"gather_{ph}_pass", _gather_wait(f"gather_{ph}_wait", bufs, send, recv, after)))

    small_full = {}
    chip = 2 * lax.axis_index("x") + lax.axis_index("y")
    sv = [lax.dynamic_update_index_in_dim(jnp.zeros((N_CHIPS, 2) + w[n].shape, F32), jnp.stack([w[n], w[n]]), chip, 0)
          for n in SMALL_SHARDED]
    got = _chip_allgather("gather_vectors", sv)
    for n, g in zip(SMALL_SHARDED, got):
        small_full[n] = jnp.transpose(g[:, 0], (1, 0, 2)).reshape(w[n].shape[0], E)
    after = [got[0]]
    for ph in range(len(phases)):
        flight[ph] = _gather_start(f"gather_{ph}_start", [_cast_weight(w, n, l, after if ph else ()) for n, l in phases[ph]], after)
        after = [flight[ph][3]]
    take_phase(0, after[0])
    gather_tokens = after

    norm_w = w['norm_w']
    h = x.reshape(T, D)
    saved = []
    dparts = {}

    def s5_consts(j):
        ar, ai, fr, fi = _s5_disc_fwd(f"s5_disc_{j}", w['s5_a_re'][j], w['s5_a_im'][j], w['s5_log_dt'][j].reshape(G, 1))
        br, bi = w['s5_b_re'][j].reshape(G * P, C), w['s5_b_im'][j].reshape(G * P, C)
        bbr, bbi = _s5_bbar(f"s5_bbar_{j}", fr.reshape(G * P, 1), fi.reshape(G * P, 1), br, bi)
        bbd = jnp.concatenate([_compact(bbr.reshape(G, P, C), NC), _compact(bbi.reshape(G, P, C), NC)], axis=2).astype(BF16)
        ct = lambda v: jnp.transpose(v, (0, 2, 1))
        cbd = jnp.concatenate([_compact(ct(w['s5_c_re'][j]), NC), -_compact(ct(w['s5_c_im'][j]), NC)], axis=2).astype(BF16)
        return dict(ar=ar, ai=ai, fr=fr, fi=fi, br=br, bi=bi, bbd=bbd, cbd=cbd,
                    ar3=ar.reshape(NC, 1, L), ai3=ai.reshape(NC, 1, L))

    for i in range(4):
        kind, j = i % 3, i // 3
        nw = norm_w[i].reshape(1, D)
        xn = _norm_fwd(f"norm_{i}", h, nw, deps=gather_tokens if i == 0 else ())
        if kind == 0:
            k5 = s5_consts(j)
            proj = _mm_proj(f"s5_proj_{i}", xn, W[('s5_in_proj', j)])
            dsk = small_full['s5_d'][j].reshape(1, E)
            y1, g, hs = _s5_fwd(f"s5_scan_{i}", proj, k5['bbd'], k5['cbd'], k5['ar3'], k5['ai3'], dsk, E)
            bglu = small_full['s5_b_glu'][j].reshape(1, E)
            if i == 0:
                take_phase(1, y1)

            def glu_epi(acc, b, y1t, z):
                lin = acc + b
                return lin, (_gelu(y1t) * _sigmoid(lin)) * _silu(z)

            lin, a = _mm_rowsharded(
                f"s5_glu_{i}", g, W[('s5_w_glu', j)], epi=glu_epi,
                extras=lambda tm, tn: [(bglu, _rowvec(tn)), (y1, _tile(tm, tn)), (proj, _tile(tm, tn, E // tn))],
                outs_fn=lambda tm, tn: [((T, E), F32, _tile(tm, tn)), ((T, E), BF16, _tile(tm, tn))])
            saved.append(dict(h=h, xn=xn, proj=proj, y1=y1, g=g, hs=hs, lin=lin, a=a, k5=k5, dsk=dsk))
        elif kind == 1:
            fox_w = jnp.transpose(W[('fox_in_proj', j)], (1, 0, 2)).reshape(D, -1)
            w_qkvz = fox_w[:, :4 * E]
            w_f = jnp.pad(fox_w[:, 4 * E:], ((0, 0), (0, LANES - H)))
            proj = _mm_plain(f"fox_proj_{i}", xn, w_qkvz)[0]
            flog = _mm_plain(f"fox_gate_proj_{i}", xn, w_f)[0]
            fb = jnp.pad(w['fox_f_bias'][j].reshape(1, H), ((0, 0), (0, LANES - H)))
            wq, wk = w['fox_q_norm'][j].reshape(1, FOX_HEAD_DIM), w['fox_k_norm'][j].reshape(1, FOX_HEAD_DIM)
            qn, kn = _qk_norm(f"fox_qk_norm_{i}", proj, wq, wk, H)
            cum = _cum_rows(f"fox_cum_{i}", flog, fb, False, True)
            cum_t = jnp.transpose(cum)[:H]
            cum_q = jnp.broadcast_to(cum_t[:, :, None], (H, T, LANES))
            cum_k = cum_t.reshape(H, nq, 1, tq)
            y, lse = _attn_fwd(f"fox_attn_{i}", qn, kn, proj, cum_q, cum_k, H)
            a = _rows(f"fox_gate_{i}", lambda yt, z: (yt * _silu(z),), [(y, 'r', E, 0), (proj, 'r', E, 3)], [('r', E, BF16)], 256)[0]
            saved.append(dict(h=h, xn=xn, proj=proj, flog=flog, fb=fb, wq=wq, wk=wk, qn=qn, kn=kn, cum_q=cum_q, cum_k=cum_k, y=y, lse=lse, a=a,
                              w_qkvz=w_qkvz, w_f=w_f))
        else:
            w_pg = jnp.transpose(W[('pool_w_group', j)].reshape(N_CHIPS, PG, PD // N_CHIPS, PD), (1, 0, 2, 3)).reshape(PG, PD, PD)
            proj = _mm_proj(f"pool_proj_{i}", xn, W[('pool_in_proj', j)])
            pm = _pool_fwd(f"pool_win_{i}", proj, E)
            scale = small_full['pool_scale'][j].reshape(1, E)
            tm, tn, tk = _t(512, T), _t(512, PD), _t(K_STEP, PD)
            kb, nb = PD // tk, PD // tn
            mixed, a = _mm(
                f"pool_mix_{i}", pm, w_pg, M=T, N=PD, K=PD, tm=tm, tn=tn, tk=tk, groups=PG,
                a_spec=_bs((tm, tk), lambda g, m, n, k: (m, g * kb + k)),
                b_spec=_bs((None, tk, tn), lambda g, m, n, k: (g, k, n)),
                extras=[(scale, _bs((1, tn), lambda g, m, n, k: (0, g * nb + n))),
                        (proj, _bs((tm, tn), lambda g, m, n, k: (m, E // tn + g * nb + n)))],
                epi=lambda acc, sc, z: (acc, (acc * sc) * _silu(z)),
                outs=[((T, E), F32, _bs((tm, tn), lambda g, m, n, k: (m, g * nb + n))),
                      ((T, E), BF16, _bs((tm, tn), lambda g, m, n, k: (m, g * nb + n)))])
            saved.append(dict(h=h, xn=xn, proj=proj, pm=pm, mixed=mixed, scale=scale, a=a, w_pg=w_pg))
        h = _mm_rowsharded(f"out_proj_{i}", saved[-1]['a'], W[('out_proj', i)], epi=lambda acc, r: (r + acc,),
                           extras=lambda tm, tn: [(h, _tile(tm, tn))],
                           outs_fn=lambda tm, tn: [((T, D), F32, _tile(tm, tn))])[0]
        if i < 2:
            take_phase(i + 2, h)

    dh, dh16, loss_cols = _loss(h, loss_target.reshape(T, D))
    loss = lax.psum(jnp.sum(loss_cols), ("x", "y", "c"))

    gsmall = {n: [None] * w[n].shape[0] for n in SMALL}
    big_index = {n: o for o, n in enumerate(BIG)}
    rs_shapes = [None] * (len(BIG) + 1)
    rs_bufs = [None] * (len(BIG) + 1)
    rs_dests_all = []
    pending = None

    def reduce_layer(tag, named_parts):
        parts, dests = [], []
        for n, l, pt in named_parts:
            o = big_index[n] if n in big_index else len(BIG)
            half = pt.shape[2:]
            rs_shapes[o] = (N_CHIPS if l == 'chip' else w[n].shape[0], 2, math.prod(half[:-1]), half[-1])
            parts.append(pt)
            dests.append((o, l))
        rs_dests_all.extend(dests)
        state, token = _reduce_begin(tag, parts)
        return (tag, state, dests), token

    token = loss.reshape(1, 1)
    for i in reversed(range(4)):
        kind, j = i % 3, i // 3
        sv_ = saved[i]
        nw = norm_w[i].reshape(1, D)
        w_out = W[('out_proj', i)]
        after_start = [token] if token is not None else ()
        layer_parts = [('out_proj', i, _mm_dw_rows(f"d_out_proj_{i}", sv_['a'], dh16, deps=after_start))]
        if kind == 0:
            w_glu = W[('s5_w_glu', j)]
            proj, y1, lin, k5 = sv_['proj'], sv_['y1'], sv_['lin'], sv_['k5']

            def da_epi(da, y1t, lint, z):
                gt, sg = _gelu(y1t), _sigmoid(lint)
                dy2 = da * _silu(z)
                dlin = (dy2 * gt) * (sg * (1.0 - sg))
                return da * (gt * sg) * _dsilu(z), dlin, dy2 * sg, _colsum(dlin)

            nm = T // _t(512, T)
            dz, dlin, dgd, dbg = _mm_rowsharded_t(
                f"d_s5_act_{i}", dh16, w_out, epi=da_epi, deps=after_start,
                extras=lambda tm, tn: [(y1, _tile(tm, tn)), (lin, _tile(tm, tn)), (proj, _tile(tm, tn, E // tn))],
                outs_fn=lambda tm, tn: [((T, E), BF16, _tile(tm, tn)), ((T, E), BF16, _tile(tm, tn)), ((T, E), F32, _tile(tm, tn)),
                                        ((nm, 1, E), F32, _bs((None, 1, tn), lambda g, m, n, k: (m, 0, n)))])
            gsmall['s5_b_glu'][j] = jnp.sum(dbg, axis=(0, 1))
            layer_parts.append(('s5_w_glu', j, _mm_dw_rows(f"d_s5_w_glu_{i}", sv_['g'], dlin)))
            glu_deps = ()
            if i == 0:
                early, early_token = reduce_layer("l0a", layer_parts)
                layer_parts, glu_deps = [], [early_token]
            dy1 = _mm_rowsharded_t(
                f"d_s5_glu_{i}", dlin, w_glu, epi=lambda acc, d, y1t: ((acc + d) * _dgelu(y1t),), deps=glu_deps,
                extras=lambda tm, tn: [(dgd, _tile(tm, tn)), (y1, _tile(tm, tn))],
                outs_fn=lambda tm, tn: [((T, E), F32, _tile(tm, tn))])[0]
            du, dbd, dcd, dab, ddk = _s5_bwd(f"d_s5_scan_{i}", dy1, proj, sv_['hs'], k5['bbd'], k5['cbd'], k5['ar3'], k5['ai3'], sv_['dsk'], E)
            gsmall['s5_d'][j] = ddk.reshape(E)
            gsmall['s5_c_re'][j] = jnp.transpose(_uncompact(dcd[:, :, :L], G), (0, 2, 1))
            gsmall['s5_c_im'][j] = -jnp.transpose(_uncompact(dcd[:, :, L:], G), (0, 2, 1))
            dbbr = _uncompact(dbd[:, :, :L], G).reshape(G * P, C)
            dbbi = _uncompact(dbd[:, :, L:], G).reshape(G * P, C)
            dbr, dbi, dfr, dfi = _s5_bbar_bwd(f"d_s5_bbar_{i}", k5['fr'].reshape(G * P, 1), k5['fi'].reshape(G * P, 1), k5['br'], k5['bi'], dbbr, dbbi)
            gsmall['s5_b_re'][j] = dbr.reshape(G, P, C)
            gsmall['s5_b_im'][j] = dbi.reshape(G, P, C)
            dab = jnp.sum(dab, axis=1)
            dare, daim, dldt = _s5_disc_bwd(f"d_s5_disc_{i}", w['s5_a_re'][j], w['s5_a_im'][j], w['s5_log_dt'][j].reshape(G, 1),
                                            (dab[:, :L].reshape(G, P), dab[:, L:].reshape(G, P), dfr.reshape(G, P), dfi.reshape(G, P)))
            gsmall['s5_a_re'][j], gsmall['s5_a_im'][j], gsmall['s5_log_dt'][j] = dare, daim, dldt.reshape(G)
            dproj = jnp.concatenate([du, dz], axis=1)
            layer_parts.append(('s5_in_proj', j, _mm_dw_cols(f"d_s5_in_proj_{i}", sv_['xn'], dproj)))
            dxn = _mm_colsharded_t(f"d_s5_xn_{i}", dproj, W[('s5_in_proj', j)])
        elif kind == 1:
            proj, y = sv_['proj'], sv_['y']
            do, dz = _mm_rowsharded_t(
                f"d_fox_act_{i}", dh16, w_out, epi=lambda da, yt, z: (da * _silu(z), (da * yt) * _dsilu(z)), deps=after_start,
                extras=lambda tm, tn: [(y, _tile(tm, tn)), (proj, _tile(tm, tn, 3 * E // tn))],
                outs_fn=lambda tm, tn: [((T, E), F32, _tile(tm, tn)), ((T, E), BF16, _tile(tm, tn))])
            dqn, dkn, dv, dcq, dck = _attn_bwd(f"d_fox_attn_{i}", sv_['qn'], sv_['kn'], proj, do, y, sv_['lse'], sv_['cum_q'], sv_['cum_k'], H)
            dq, dk, dwq, dwk = _qk_norm_bwd(f"d_fox_qk_norm_{i}", proj, sv_['wq'], sv_['wk'], dqn, dkn, H)
            gsmall['fox_q_norm'][j], gsmall['fox_k_norm'][j] = dwq.reshape(-1), dwk.reshape(-1)
            dcum = dcq + jnp.pad(jnp.transpose(dck.reshape(H, T)), ((0, 0), (0, LANES - H)))
            dls = _cum_rows(f"d_fox_cum_{i}", dcum, jnp.zeros((1, LANES), F32), True, False)
            dflog, dfb = _rows(f"d_fox_gate_{i}", lambda d, f, b: ((lambda r: (r, _colsum(r)))(d * _sigmoid(-(f + b)))),
                               [(dls, 'r', LANES, 0), (sv_['flog'], 'r', LANES, 0), (sv_['fb'], 'b', LANES, 0)],
                               [('r', LANES, BF16), ('a', LANES, F32)], 256)
            gsmall['fox_f_bias'][j] = dfb[0, :H]
            dproj = jnp.concatenate([dq, dk, dv, dz], axis=1)
            tkT = _t(K_STEP, T)
            dw_qkvz = _mm(f"d_fox_in_proj_{i}", sv_['xn'], dproj, M=D, N=4 * E, K=T, tm=_t(512, D), tn=_t(1024, 4 * E), tk=tkT, ta=True,
                          a_spec=_bs((tkT, _t(512, D)), lambda g, m, n, k: (k, m)),
                          b_spec=_bs((tkT, _t(1024, 4 * E)), lambda g, m, n, k: (k, n)),
                          outs=[((D, 4 * E), BF16, _tile(_t(512, D), _t(1024, 4 * E)))])[0]
            dw_f = _mm(f"d_fox_gate_proj_{i}", sv_['xn'], dflog, M=D, N=LANES, K=T, tm=_t(512, D), tn=LANES, tk=tkT, ta=True,
                       a_spec=_bs((tkT, _t(512, D)), lambda g, m, n, k: (k, m)),
                       b_spec=_bs((tkT, LANES), lambda g, m, n, k: (k, n)),
                       outs=[((D, LANES), BF16, _tile(_t(512, D), LANES))])[0]
            dw_fox = jnp.concatenate([dw_qkvz, dw_f[:, :H]], axis=1)
            sw = dw_fox.shape[1] // N_CHIPS
            layer_parts.append(('fox_in_proj', j, jnp.transpose(dw_fox.reshape(2, D // 2, N_CHIPS, sw), (0, 2, 1, 3))))
            w_qkvz, w_f = sv_['w_qkvz'], sv_['w_f']
            dxn_f = _mm(f"d_fox_xn_gate_{i}", dflog, w_f, M=T, N=D, K=LANES, tm=_t(512, T), tn=_t(1024, D), tk=LANES, tb=True,
                        a_spec=_bs((_t(512, T), LANES), lambda g, m, n, k: (m, k)),
                        b_spec=_bs((_t(1024, D), LANES), lambda g, m, n, k: (n, k)),
                        outs=[((T, D), F32, _tile(_t(512, T), _t(1024, D)))])[0]
            tm, tn, tk = _t(512, T), _t(1024, D), _t(K_STEP, 4 * E)
            dxn = _mm(f"d_fox_xn_{i}", dproj, w_qkvz, M=T, N=D, K=4 * E, tm=tm, tn=tn, tk=tk, tb=True,
                      a_spec=_bs((tm, tk), lambda g, m, n, k: (m, k)), b_spec=_bs((tn, tk), lambda g, m, n, k: (n, k)),
                      extras=[(dxn_f, _tile(tm, tn))], epi=lambda acc, e: (acc + e,),
                      outs=[((T, D), F32, _tile(tm, tn))])[0]
        else:
            proj, mixed, scale = sv_['proj'], sv_['mixed'], sv_['scale']
            nm = T // _t(512, T)

            def pool_epi(da, mx, sc, z):
                dy = da * _silu(z)
                return (da * (mx * sc)) * _dsilu(z), dy * sc, _colsum(dy * mx)

            dz, dmix, dsc = _mm_rowsharded_t(
                f"d_pool_act_{i}", dh16, w_out, epi=pool_epi, deps=after_start,
                extras=lambda tm, tn: [(mixed, _tile(tm, tn)), (scale, _rowvec(tn)), (proj, _tile(tm, tn, E // tn))],
                outs_fn=lambda tm, tn: [((T, E), BF16, _tile(tm, tn)), ((T, E), BF16, _tile(tm, tn)),
                                        ((nm, 1, E), F32, _bs((None, 1, tn), lambda g, m, n, k: (m, 0, n)))])
            gsmall['pool_scale'][j] = jnp.sum(dsc, axis=(0, 1))
            w_pg = sv_['w_pg']
            tkw = PD // N_CHIPS
            tk = _t(K_STEP, T)
            layer_parts.append(('pool_w_group', j, _mm(
                f"d_pool_w_group_{i}", sv_['pm'], dmix, M=PD, N=PD, K=T, tm=tkw, tn=PD, tk=tk, groups=PG, ta=True,
                a_spec=_bs((tk, tkw), lambda g, m, n, k: (k, g * (PD // tkw) + m)),
                b_spec=_bs((tk, PD), lambda g, m, n, k: (k, g)),
                outs=[((2, N_CHIPS, PG // 2, tkw, PD), BF16, _bs((None, None, None, tkw, PD), lambda g, m, n, k: (g // (PG // 2), m, g % (PG // 2), 0, 0)))])[0]))
            tm, tn2, tk2 = _t(512, T), _t(512, PD), _t(K_STEP, PD)
            dpm = _mm(f"d_pool_mix_{i}", dmix, w_pg, M=T, N=PD, K=PD, tm=tm, tn=tn2, tk=tk2, groups=PG, tb=True,
                      a_spec=_bs((tm, tk2), lambda g, m, n, k: (m, g * (PD // tk2) + k)),
                      b_spec=_bs((None, tn2, tk2), lambda g, m, n, k: (g, n, k)),
                      outs=[((T, E), F32, _bs((tm, tn2), lambda g, m, n, k: (m, g * (PD // tn2) + n)))])[0]
            du = _pool_bwd(f"d_pool_win_{i}", dpm, E)
            dproj = jnp.concatenate([du, dz], axis=1)
            layer_parts.append(('pool_in_proj', j, _mm_dw_cols(f"d_pool_in_proj_{i}", sv_['xn'], dproj)))
            dxn = _mm_colsharded_t(f"d_pool_xn_{i}", dproj, W[('pool_in_proj', j)])
        dh, dh16, dnw = _norm_bwd(f"d_norm_{i}", dxn, sv_['h'], nw, dh)
        gsmall['norm_w'][i] = dnw.reshape(D)
        if pending is not None:
            _reduce_end(pending[0], pending[1], dh16, pending[2], rs_bufs, rs_shapes)
        if i > 0:
            pending, token = reduce_layer(f"l{i}", layer_parts)
    grad_x = dh.reshape(x.shape)

    small_flat = jnp.concatenate([jnp.stack(gsmall[n]).reshape(-1) for n in SMALL])
    n_small = small_flat.shape[0]
    unit = 2 * N_CHIPS * 16 * LANES
    n_pad = -(-n_small // unit) * unit
    R = n_pad // (2 * N_CHIPS * LANES)
    small_part = jnp.pad(small_flat, (0, n_pad - n_small)).astype(BF16).reshape(2, N_CHIPS, R, LANES)
    pending, token = reduce_layer("l0", layer_parts + [('small', 'chip', small_part)])
    _reduce_end(early[0], early[1], token, early[2], rs_bufs, rs_shapes)
    nb = len(BIG)
    done_items = [d for d in rs_dests_all if d not in pending[2]]
    rs_bufs[:nb] = _pair_share("rs_pair_share_a", rs_bufs[:nb], done_items, deps=[token])
    late = [o for o, _ in pending[2]]
    delta, new_m, new_v = {}, {}, {}
    grads = {}
    last = token[:1, :1]
    for o, n in enumerate(BIG):
        if o not in late:
            grads[n] = rs_bufs[o].reshape(w[n].shape)
            last = last + _adamw_big(n, w, grads, mom_m, mom_v, delta, new_m, new_v)[:1, :1]
    _reduce_end(pending[0], pending[1], last, pending[2], rs_bufs, rs_shapes)
    shared = _pair_share("rs_pair_share_b", [rs_bufs[o] for o in late], [(k, l) for k, (_, l) in enumerate(pending[2])])
    for k, o in enumerate(late):
        rs_bufs[o] = shared[k]
        if o < nb:
            grads[BIG[o]] = shared[k].reshape(w[BIG[o]].shape)
            _adamw_big(BIG[o], w, grads, mom_m, mom_v, delta, new_m, new_v)
    small_all = _chip_allgather("gather_small_grads", [rs_bufs[nb]])[0]
    small_all = jnp.transpose(small_all, (1, 0, 2, 3)).reshape(-1)[:n_small]
    off = 0
    p = 2 * lax.axis_index("x") + lax.axis_index("y")
    for n in SMALL:
        full_shape = (w[n].shape[0], E) if n in SMALL_SHARDED else w[n].shape
        size = math.prod(full_shape)
        gfull = small_all[off:off + size].reshape(full_shape)
        off += size
        if n in SMALL_SHARDED:
            gfull = lax.dynamic_slice_in_dim(gfull, p * (E // N_CHIPS), E // N_CHIPS, axis=1)
        grads[n] = gfull

    for n in SMALL:
        shape = w[n].shape
        if n in GROUP_AXIS_1:
            perm = (0,) + tuple(range(2, len(shape))) + (1,)
            inv = (0, len(shape) - 1) + tuple(range(1, len(shape) - 1))
            view = lambda a: jnp.transpose(a, perm).reshape(-1, shape[1])
            back = lambda a: jnp.transpose(a.reshape(tuple(shape[k] for k in perm)), inv)
        else:
            view = lambda a: a.reshape(-1, shape[-1])
            back = lambda a: a.reshape(shape)
        d_, m_, v_ = _adamw(f"adamw_{n}", view(w[n]), view(grads[n]), view(mom_m[n]), view(mom_v[n]))
        delta[n], new_m[n], new_v[n] = back(d_), back(m_), back(v_)
    return (loss, grad_x, *[grads[n] for n in ORDER], *[delta[n] for n in ORDER], *[new_m[n] for n in ORDER], *[new_v[n] for n in ORDER])
```

```python
import functools
import math

import jax
import jax.numpy as jnp
from jax import lax
from jax.experimental import pallas as pl
from jax.experimental.pallas import tpu as pltpu

F32 = jnp.float32
BF16 = jnp.bfloat16
MESH = pl.DeviceIdType.MESH

N_CHIPS = 4
VMEM_LIMIT = 56 * 1024 * 1024
LANES = 128
SUB = 8

EPS = 1e-6
S5_GROUP = 16
S5_STATE = 64
GROUPS_PER_CHUNK = 16
S5_TIME_BLOCK = 512
FOX_HEAD_DIM = 128
NORM_HEADS = 4
ATTN_SUB = 256
ATTN_HEADS = 2
POOL_WINDOWS = (2, 4, 8, 16)
POOL_HALO = 16
ADAM_LR, ADAM_B1, ADAM_B2, ADAM_EPS, ADAM_WD, ADAM_STEP = 0.001, 0.9, 0.999, 1e-08, 0.01, 10
NEG = -1e30
K_STEP = 2048


ANY = pl.BlockSpec(memory_space=pl.ANY)


def _t(pref, dim):
    if dim <= pref:
        return dim
    t = pref - pref % 16
    while t > 16 and dim % t:
        t -= 16
    assert dim % t == 0, (pref, dim)
    return t


def _params(sem):
    return pltpu.CompilerParams(dimension_semantics=sem, vmem_limit_bytes=VMEM_LIMIT)


def _sigmoid(x):
    return 1.0 / (1.0 + jnp.exp(-x))


def _silu(z):
    return z * _sigmoid(z)


def _dsilu(z):
    s = _sigmoid(z)
    return s * (1.0 + z * (1.0 - s))


_GELU_C = math.sqrt(2.0 / math.pi)


def _gelu(x):
    return 0.5 * x * (1.0 + jnp.tanh(_GELU_C * (x + 0.044715 * (x * x * x))))


def _dgelu(x):
    t = jnp.tanh(_GELU_C * (x + 0.044715 * (x * x * x)))
    return 0.5 * (1.0 + t) + 0.5 * x * (1.0 - t * t) * (_GELU_C * (1.0 + 3.0 * 0.044715 * x * x))


def _log_sigmoid(x):
    return jnp.minimum(x, 0.0) - jnp.log(1.0 + jnp.exp(-jnp.abs(x)))


def _rms(x):
    return lax.rsqrt(jnp.mean(x * x, axis=-1, keepdims=True) + EPS)


def _rms_bwd(x, w, dy):
    r = _rms(x)
    xhat = x * r
    dxh = dy * w
    dx = r * (dxh - xhat * jnp.mean(dxh * xhat, axis=-1, keepdims=True))
    return dx, dy * xhat


def _rows(name, fn, ins, outs, tr, pre=None, into=None, deps=()):
    rows = None
    for arr, kind, cols, cb in ins:
        if kind == 'r':
            rows = arr.shape[0]
        elif kind == 's' and rows is None:
            rows = arr.shape[1]
    tr = _t(tr, rows)
    n_in = len(ins)
    has_acc = any(o[0] == 'a' for o in outs)

    def spec(kind, cols, cb):
        if kind == 'r':
            return pl.BlockSpec((tr, cols), lambda r, *p: (r, cb))
        if kind == 'b':
            return pl.BlockSpec((1, cols), lambda r, *p: (0, cb))
        return pl.BlockSpec((None, tr, cols), lambda r, p: (p[cb], r, 0))

    in_specs = [spec(kind, cols, cb) for _, kind, cols, cb in ins]
    out_specs, out_shape = [], []
    for o in outs:
        if o[0] == 'r':
            out_specs.append(pl.BlockSpec((tr, o[1]), lambda r, *p: (r, 0)))
            out_shape.append(jax.ShapeDtypeStruct((rows, o[1]), o[2]))
        elif o[0] == 'a':
            out_specs.append(pl.BlockSpec((1, o[1]), lambda r, *p: (0, 0)))
            out_shape.append(jax.ShapeDtypeStruct((1, o[1]), o[2]))
        else:
            blk = tuple(tr if d == 'tr' else d for d in o[3])
            out_specs.append(pl.BlockSpec(blk, o[4]))
            out_shape.append(jax.ShapeDtypeStruct(o[1], o[2]))
    n_pre = 0 if pre is None else 1
    args = [a[0] for a in ins]
    aliases = {}
    if into is not None:
        in_specs.append(ANY)
        args.append(into)
        aliases = {n_pre + n_in: 0}
    in_specs += [ANY] * len(deps)
    args += list(deps)
    n_all = len(args)

    def body(*refs):
        refs = refs[n_pre:]
        res = fn(*[r[...] for r in refs[:n_in]])
        for spec_o, o, v in zip(outs, refs[n_all:], res):
            if spec_o[0] == 'a':
                @pl.when(pl.program_id(0) == 0)
                def _():
                    o[...] = jnp.zeros_like(o)
                o[...] += v.astype(o.dtype)
            else:
                o[...] = v.astype(o.dtype)

    grid_spec = pltpu.PrefetchScalarGridSpec(num_scalar_prefetch=n_pre, grid=(rows // tr,), in_specs=in_specs, out_specs=out_specs)
    if pre is not None:
        args = [pre] + args
    return pl.pallas_call(body, name=name, grid_spec=grid_spec, out_shape=out_shape, input_output_aliases=aliases,
                          compiler_params=_params(("arbitrary" if has_acc else "parallel",)))(*args)


def _colsum(v):
    return jnp.sum(v, axis=0, keepdims=True)


def _mm(name, a, b, *, M, N, K, tm, tn, tk, a_spec, b_spec, outs, epi=None, extras=(), groups=1, ta=False, tb=False, deps=()):
    nk = K // tk
    assert M % tm == 0 and N % tn == 0 and K % tk == 0, (name, M, N, K, tm, tn, tk)
    dims = (((0 if ta else 1,), (1 if tb else 0,)), ((), ()))
    n_ex = len(extras)

    def body(*refs):
        a_ref, b_ref = refs[0], refs[1]
        ex = refs[2:2 + n_ex]
        out_refs = refs[2 + n_ex + len(deps):2 + n_ex + len(deps) + len(outs)]

        def finish(r):
            res = (r,) if epi is None else epi(r, *[e[...] for e in ex])
            for o, v in zip(out_refs, res):
                o[...] = v.astype(o.dtype)

        part = lax.dot_general(a_ref[...].astype(BF16), b_ref[...].astype(BF16), dims, preferred_element_type=F32)
        if nk == 1:
            finish(part)
            return
        acc = refs[-1]
        k = pl.program_id(3)

        @pl.when(k == 0)
        def _():
            acc[...] = part

        @pl.when(k > 0)
        def _():
            acc[...] += part

        @pl.when(k == nk - 1)
        def _():
            finish(acc[...])

    return pl.pallas_call(
        body, name=name, grid=(groups, M // tm, N // tn, nk),
        in_specs=[a_spec, b_spec] + [s for _, s in extras] + [ANY] * len(deps),
        out_specs=[s for _, _, s in outs],
        out_shape=[jax.ShapeDtypeStruct(sh, dt) for sh, dt, _ in outs],
        scratch_shapes=[] if nk == 1 else [pltpu.VMEM((tm, tn), F32)],
        compiler_params=_params(("parallel", "parallel", "parallel", "arbitrary")),
    )(a, b, *[e for e, _ in extras], *deps)


def _bs(shape, f):
    return pl.BlockSpec(shape, f)


def _tile(tm, tn, coff=0):
    return _bs((tm, tn), lambda g, m, n, k: (m, n + coff))


def _rowvec(tn, coff=0):
    return _bs((1, tn), lambda g, m, n, k: (0, n + coff))


def _mm_proj(name, xn, w, *, epi=None, extras=(), out_dtype=F32):
    T, D = xn.shape
    sw = w.shape[2]
    N = N_CHIPS * sw
    tm, tn, tk = _t(512, T), _t(1024, sw), _t(K_STEP, D)
    nb = sw // tn
    return _mm(name, xn, w, M=T, N=N, K=D, tm=tm, tn=tn, tk=tk,
               a_spec=_bs((tm, tk), lambda g, m, n, k: (m, k)),
               b_spec=_bs((None, tk, tn), lambda g, m, n, k: (n // nb, k, n % nb)),
               outs=[((T, N), out_dtype, _tile(tm, tn))], epi=epi, extras=extras)[0]


def _mm_plain(name, a, b, *, out_dtype=F32, epi=None, extras=(), outs=None, tn_pref=1024):
    M, K = a.shape
    N = b.shape[1]
    tm, tn, tk = _t(512, M), _t(tn_pref, N), _t(K_STEP, K)
    if outs is None:
        outs = [((M, N), out_dtype, _tile(tm, tn))]
    return _mm(name, a, b, M=M, N=N, K=K, tm=tm, tn=tn, tk=tk,
               a_spec=_bs((tm, tk), lambda g, m, n, k: (m, k)),
               b_spec=_bs((tk, tn), lambda g, m, n, k: (k, n)),
               outs=outs, epi=epi, extras=extras)


def _mm_rowsharded(name, a, w, *, epi, extras, outs_fn, deps=()):
    T, E = a.shape
    N = w.shape[2]
    tm, tn, tk = _t(512, T), _t(1024, N), _t(K_STEP, E)
    return _mm(name, a, w.reshape(E, N), M=T, N=N, K=E, tm=tm, tn=tn, tk=tk, deps=deps,
               a_spec=_bs((tm, tk), lambda g, m, n, k: (m, k)),
               b_spec=_bs((tk, tn), lambda g, m, n, k: (k, n)),
               outs=outs_fn(tm, tn), epi=epi, extras=extras(tm, tn))


def _mm_rowsharded_t(name, d, w, *, epi, extras, outs_fn, deps=()):
    T, N = d.shape
    tn = w.shape[1]
    E = N_CHIPS * tn
    tm, tk = _t(512, T), _t(K_STEP, N)
    return _mm(name, d, w, M=T, N=E, K=N, tm=tm, tn=tn, tk=tk, tb=True, deps=deps,
               a_spec=_bs((tm, tk), lambda g, m, n, k: (m, k)),
               b_spec=_bs((None, tn, tk), lambda g, m, n, k: (n, 0, k)),
               outs=outs_fn(tm, tn), epi=epi, extras=extras(tm, tn))


def _mm_colsharded_t(name, d, w):
    T, N = d.shape
    D, sw = w.shape[1], w.shape[2]
    tm, tn, tk = _t(512, T), _t(1024, D), _t(1024, sw)
    kb = sw // tk
    return _mm(name, d, w, M=T, N=D, K=N, tm=tm, tn=tn, tk=tk, tb=True,
               a_spec=_bs((tm, tk), lambda g, m, n, k: (m, k)),
               b_spec=_bs((None, tn, tk), lambda g, m, n, k: (k // kb, n, k % kb)),
               outs=[((T, D), F32, _tile(tm, tn))])[0]


def _mm_dw_rows(name, a, d, deps=()):
    T, E = a.shape
    N = d.shape[1]
    tm, tn, tk = E // (2 * N_CHIPS), _t(2048, N), _t(K_STEP, T)
    return _mm(name, a, d, M=E, N=N, K=T, tm=tm, tn=tn, tk=tk, ta=True, deps=deps,
               a_spec=_bs((tk, tm), lambda g, m, n, k: (k, m)),
               b_spec=_bs((tk, tn), lambda g, m, n, k: (k, n)),
               outs=[((2, N_CHIPS, tm, N), BF16, _bs((None, None, tm, tn), lambda g, m, n, k: (m % 2, m // 2, 0, n)))])[0]


def _mm_dw_cols(name, xn, d):
    T, D = xn.shape
    N = d.shape[1]
    sw = N // N_CHIPS
    tm, tn, tk = _t(512, D // 2), _t(1024, sw), _t(K_STEP, T)
    mh, nb = (D // 2) // tm, sw // tn
    return _mm(name, xn, d, M=D, N=N, K=T, tm=tm, tn=tn, tk=tk, ta=True,
               a_spec=_bs((tk, tm), lambda g, m, n, k: (k, m)),
               b_spec=_bs((tk, tn), lambda g, m, n, k: (k, n)),
               outs=[((2, N_CHIPS, D // 2, sw), BF16,
                      _bs((None, None, tm, tn), lambda g, m, n, k: (m // mh, n // nb, m % mh, n % nb)))])[0]


def _norm_fwd(name, h, w, deps=()):
    D = h.shape[1]
    return _rows(name, lambda x, g: ((x * _rms(x)) * g,), [(h, 'r', D, 0), (w, 'b', D, 0)], [('r', D, BF16)], 256, deps=deps)[0]


def _norm_bwd(name, dxn, h, w, dh):
    D = h.shape[1]

    def fn(dy, x, g, up):
        dx, dwt = _rms_bwd(x, g, dy)
        r = up + dx
        return r, r, _colsum(dwt)

    return _rows(name, fn, [(dxn, 'r', D, 0), (h, 'r', D, 0), (w, 'b', D, 0), (dh, 'r', D, 0)],
                 [('r', D, F32), ('r', D, BF16), ('a', D, F32)], 256)


def _loss(h, target):
    D = h.shape[1]

    def fn(y, t):
        e = y - t
        d = e * (1.0 / D)
        return d, d, _colsum(e * e) * (0.5 / D)

    return _rows("loss", fn, [(h, 'r', D, 0), (target, 'r', D, 0)], [('r', D, F32), ('r', D, BF16), ('a', D, F32)], 256)


def _adamw(name, w, g, m, v):
    cols = w.shape[1]

    def fn(w, g, m, v):
        m = ADAM_B1 * m + (1.0 - ADAM_B1) * g
        v = ADAM_B2 * v + (1.0 - ADAM_B2) * (g * g)
        m_hat = m / (1.0 - ADAM_B1 ** ADAM_STEP)
        v_hat = v / (1.0 - ADAM_B2 ** ADAM_STEP)
        delta = -ADAM_LR * (m_hat / (jnp.sqrt(v_hat) + ADAM_EPS) + ADAM_WD * w)
        return delta, m, v

    rows = w.shape[0]
    if rows % SUB == 0 or rows <= 256:
        return _rows(name, fn, [(x, 'r', cols, 0) for x in (w, g, m, v)], [('r', cols, F32)] * 3, 256)
    tc = _t(256, cols)
    assert tc % LANES == 0, (rows, cols)

    def body(w_ref, g_ref, m_ref, v_ref, d_out, m_out, v_out):
        for o, r in zip((d_out, m_out, v_out), fn(w_ref[...], g_ref[...], m_ref[...], v_ref[...])):
            o[...] = r

    blk = pl.BlockSpec((rows, tc), lambda j: (0, j))
    return pl.pallas_call(body, name=name, grid=(cols // tc,), in_specs=[blk] * 4, out_specs=[blk] * 3,
                          out_shape=[jax.ShapeDtypeStruct((rows, cols), F32)] * 3, compiler_params=_params(("parallel",)))(w, g, m, v)


def _s5_disc(a_re, a_im, log_dt):
    dt = jnp.exp(log_dt)
    mag = jnp.exp(a_re * dt)
    abar_r = mag * jnp.cos(a_im * dt)
    abar_i = mag * jnp.sin(a_im * dt)
    den = a_re * a_re + a_im * a_im
    xr = abar_r - 1.0
    fr = (xr * a_re + abar_i * a_im) / den
    fi = (abar_i * a_re - xr * a_im) / den
    return abar_r, abar_i, fr, fi


def _s5_disc_fwd(name, a_re, a_im, log_dt):
    G, P = a_re.shape

    def body(ar, ai, ld, o0, o1, o2, o3):
        for o, v in zip((o0, o1, o2, o3), _s5_disc(ar[...], ai[...], ld[...])):
            o[...] = v

    return pl.pallas_call(body, name=name, out_shape=[jax.ShapeDtypeStruct((G, P), F32)] * 4)(a_re, a_im, log_dt)


def _s5_disc_bwd(name, a_re, a_im, log_dt, cts):
    G, P = a_re.shape

    def body(ar, ai, ld, c0, c1, c2, c3, d0, d1, d2):
        _, vjp = jax.vjp(_s5_disc, ar[...], ai[...], ld[...])
        g0, g1, g2 = vjp((c0[...], c1[...], c2[...], c3[...]))
        d0[...] = g0
        d1[...] = g1
        d2[...] = g2

    return pl.pallas_call(body, name=name, out_shape=[jax.ShapeDtypeStruct((G, P), F32)] * 2 + [jax.ShapeDtypeStruct((G, 1), F32)])(
        a_re, a_im, log_dt, *cts)


def _s5_bbar(name, fr, fi, br, bi):
    return _rows(name, lambda fr, fi, br, bi: (fr * br - fi * bi, fr * bi + fi * br),
                 [(fr, 'r', 1, 0), (fi, 'r', 1, 0), (br, 'r', S5_GROUP, 0), (bi, 'r', S5_GROUP, 0)],
                 [('r', S5_GROUP, F32)] * 2, 2048)


def _s5_bbar_bwd(name, fr, fi, br, bi, dr, di):
    def fn(fr, fi, br, bi, dr, di):
        return (fr * dr + fi * di, fr * di - fi * dr,
                jnp.sum(br * dr + bi * di, axis=1, keepdims=True), jnp.sum(br * di - bi * dr, axis=1, keepdims=True))

    return _rows(name, fn, [(fr, 'r', 1, 0), (fi, 'r', 1, 0)] + [(x, 'r', S5_GROUP, 0) for x in (br, bi, dr, di)],
                 [('r', S5_GROUP, F32)] * 2 + [('r', 1, F32)] * 2, 2048)


def _scan_mults(m_ref, ar, ai, reverse):
    L = ar.shape[1]
    row = lax.broadcasted_iota(jnp.int32, (SUB, L), 0)
    if reverse:
        row = (SUB - 1) - row
    ar = jnp.broadcast_to(ar, (SUB, L))
    ai = jnp.broadcast_to(ai, (SUB, L))
    a2r, a2i = ar * ar - ai * ai, 2.0 * ar * ai
    a4r, a4i = a2r * a2r - a2i * a2i, 2.0 * a2r * a2i
    zero = jnp.zeros((SUB, L), F32)
    for s, (pr, pi, d) in enumerate(((ar, ai, 1), (a2r, a2i, 2), (a4r, a4i, 4))):
        m_ref[2 * s] = jnp.where(row >= d, pr, zero)
        m_ref[2 * s + 1] = jnp.where(row >= d, pi, zero)
    pr, pi = ar, ai
    for bit, (qr, qi) in ((1, (ar, ai)), (2, (a2r, a2i)), (4, (a4r, a4i))):
        on = (row & bit) != 0
        nr, ni = pr * qr - pi * qi, pr * qi + pi * qr
        pr, pi = jnp.where(on, nr, pr), jnp.where(on, ni, pi)
    m_ref[6] = pr
    m_ref[7] = pi


def _scan8(xr, xi, m_ref, cr, ci, reverse):
    for s, d in enumerate((1, 2, 4)):
        sh = (SUB - d) if reverse else d
        sr, si = pltpu.roll(xr, sh, 0), pltpu.roll(xi, sh, 0)
        mr, mi = m_ref[2 * s], m_ref[2 * s + 1]
        xr, xi = xr + mr * sr - mi * si, xi + mr * si + mi * sr
    pr, pi = m_ref[6], m_ref[7]
    return xr + pr * cr - pi * ci, xi + pr * ci + pi * cr


def _blockdiag_fill(bd_ref, c_ref, C, L):
    P = S5_STATE
    bd_ref[...] = jnp.zeros_like(bd_ref)
    for g in range(L // P):
        for half in (0, L):
            bd_ref[g * C:(g + 1) * C, half + g * P:half + (g + 1) * P] = c_ref[:, half + g * P:half + (g + 1) * P]


def _blockdiag_take(out_ref, dense_ref, C, L):
    P = S5_STATE
    for g in range(L // P):
        for half in (0, L):
            out_ref[:, half + g * P:half + (g + 1) * P] = dense_ref[g * C:(g + 1) * C, half + g * P:half + (g + 1) * P]


def _s5_fwd(name, proj, bbd, cbd, abar_r, abar_i, dskip, E):
    T = proj.shape[0]
    NC, C, L2 = bbd.shape
    L = L2 // 2
    CH = GROUPS_PER_CHUNK * C
    tT = _t(S5_TIME_BLOCK, T)
    nt = (((1,), (1,)), ((), ()))

    def body(u_ref, bc_ref, cc_ref, ar_ref, ai_ref, d_ref, y_ref, g_ref, h_ref, bu, carry, mult, b_bd, c_bd):
        tb = pl.program_id(1)

        @pl.when(tb == 0)
        def _():
            carry[...] = jnp.zeros_like(carry)
            _blockdiag_fill(b_bd, bc_ref, C, L)
            _blockdiag_fill(c_bd, cc_ref, C, L)

        u = u_ref[...]
        bu[...] = jnp.dot(u.astype(BF16), b_bd[...], preferred_element_type=F32)
        _scan_mults(mult, ar_ref[...], ai_ref[...], False)

        def step(jb, c):
            cr, ci = c
            r0 = pl.multiple_of(jb * SUB, SUB)
            hr, hi = _scan8(bu[pl.ds(r0, SUB), 0:L], bu[pl.ds(r0, SUB), L:L2], mult, cr, ci, False)
            h_ref[pl.ds(r0, SUB), 0:L] = hr
            h_ref[pl.ds(r0, SUB), L:L2] = hi
            return (jnp.broadcast_to(hr[SUB - 1:SUB, :], (SUB, L)), jnp.broadcast_to(hi[SUB - 1:SUB, :], (SUB, L)))

        cr, ci = lax.fori_loop(0, tT // SUB, step, (carry[:, 0:L], carry[:, L:L2]))
        carry[:, 0:L] = cr
        carry[:, L:L2] = ci
        y1 = lax.dot_general(h_ref[...].astype(BF16), c_bd[...], nt, preferred_element_type=F32) + d_ref[...] * u
        y_ref[...] = y1
        g_ref[...] = _gelu(y1).astype(BF16)

    return pl.pallas_call(
        body, name=name, grid=(NC, T // tT),
        in_specs=[_bs((tT, CH), lambda c, t: (t, c)), _bs((None, C, L2), lambda c, t: (c, 0, 0)),
                  _bs((None, C, L2), lambda c, t: (c, 0, 0)), _bs((None, 1, L), lambda c, t: (c, 0, 0)),
                  _bs((None, 1, L), lambda c, t: (c, 0, 0)), _bs((1, CH), lambda c, t: (0, c))],
        out_specs=[_bs((tT, CH), lambda c, t: (t, c)), _bs((tT, CH), lambda c, t: (t, c)),
                   _bs((None, tT, L2), lambda c, t: (c, t, 0))],
        out_shape=[jax.ShapeDtypeStruct((T, E), F32), jax.ShapeDtypeStruct((T, E), BF16),
                   jax.ShapeDtypeStruct((NC, T, L2), F32)],
        scratch_shapes=[pltpu.VMEM((tT, L2), F32), pltpu.VMEM((SUB, L2), F32), pltpu.VMEM((8, SUB, L), F32),
                        pltpu.VMEM((CH, L2), BF16), pltpu.VMEM((CH, L2), BF16)],
        compiler_params=_params(("parallel", "arbitrary")),
    )(proj, bbd, cbd, abar_r, abar_i, dskip)


def _s5_bwd(name, dy1, proj, hs, bbd, cbd, abar_r, abar_i, dskip, E):
    T = proj.shape[0]
    NC, C, L2 = bbd.shape
    L = L2 // 2
    CH = GROUPS_PER_CHUNK * C
    tT = _t(S5_TIME_BLOCK, T)
    nT = T // tT
    tn = (((0,), (0,)), ((), ()))
    nt = (((1,), (1,)), ((), ()))

    def body(dy_ref, u_ref, h_ref, bc_ref, cc_ref, ar_ref, ai_ref, d_ref, du_ref, db_ref, dc_ref, da_ref, dd_ref,
             gb, carry, mult, b_bd, c_bd, db_acc, dc_acc):
        tb = pl.program_id(1)

        @pl.when(tb == 0)
        def _():
            carry[...] = jnp.zeros_like(carry)
            db_acc[...] = jnp.zeros_like(db_acc)
            dc_acc[...] = jnp.zeros_like(dc_acc)
            da_ref[...] = jnp.zeros_like(da_ref)
            dd_ref[...] = jnp.zeros_like(dd_ref)
            _blockdiag_fill(b_bd, bc_ref, C, L)
            _blockdiag_fill(c_bd, cc_ref, C, L)

        dy = dy_ref[...]
        u = u_ref[...]
        dy16 = dy.astype(BF16)
        dc_acc[...] += lax.dot_general(dy16, h_ref[...].astype(BF16), tn, preferred_element_type=F32)
        gb[...] = jnp.dot(dy16, c_bd[...], preferred_element_type=F32)
        _scan_mults(mult, ar_ref[...], -ai_ref[...], True)
        row = lax.broadcasted_iota(jnp.int32, (SUB, L), 0)
        nblk = tT // SUB

        def step(jj, c):
            cr, ci, sr, si = c
            r0 = pl.multiple_of((nblk - 1 - jj) * SUB, SUB)
            gr, gi = _scan8(gb[pl.ds(r0, SUB), 0:L], gb[pl.ds(r0, SUB), L:L2], mult, cr, ci, True)
            gb[pl.ds(r0, SUB), 0:L] = gr
            gb[pl.ds(r0, SUB), L:L2] = gi
            nr = jnp.where(row == SUB - 1, cr, pltpu.roll(gr, SUB - 1, 0))
            ni = jnp.where(row == SUB - 1, ci, pltpu.roll(gi, SUB - 1, 0))
            hr, hi = h_ref[pl.ds(r0, SUB), 0:L], h_ref[pl.ds(r0, SUB), L:L2]
            sr = sr + nr * hr + ni * hi
            si = si + ni * hr - nr * hi
            return (jnp.broadcast_to(gr[0:1, :], (SUB, L)), jnp.broadcast_to(gi[0:1, :], (SUB, L)), sr, si)

        z = jnp.zeros((SUB, L), F32)
        cr, ci, sr, si = lax.fori_loop(0, nblk, step, (carry[:, 0:L], carry[:, L:L2], z, z))
        carry[:, 0:L] = cr
        carry[:, L:L2] = ci
        da_ref[:, 0:L] += sr
        da_ref[:, L:L2] += si
        g16 = gb[...].astype(BF16)
        du = lax.dot_general(g16, b_bd[...], nt, preferred_element_type=F32) + d_ref[...] * dy
        du_ref[...] = du.astype(BF16)
        db_acc[...] += lax.dot_general(u.astype(BF16), g16, tn, preferred_element_type=F32)
        dd_ref[...] += _colsum(dy * u)

        @pl.when(tb == nT - 1)
        def _():
            _blockdiag_take(db_ref, db_acc, C, L)
            _blockdiag_take(dc_ref, dc_acc, C, L)

    rev = lambda c, t: (nT - 1 - t, c)
    return pl.pallas_call(
        body, name=name, grid=(NC, nT),
        in_specs=[_bs((tT, CH), rev), _bs((tT, CH), rev), _bs((None, tT, L2), lambda c, t: (c, nT - 1 - t, 0)),
                  _bs((None, C, L2), lambda c, t: (c, 0, 0)), _bs((None, C, L2), lambda c, t: (c, 0, 0)),
                  _bs((None, 1, L), lambda c, t: (c, 0, 0)), _bs((None, 1, L), lambda c, t: (c, 0, 0)),
                  _bs((1, CH), lambda c, t: (0, c))],
        out_specs=[_bs((tT, CH), rev), _bs((None, C, L2), lambda c, t: (c, 0, 0)), _bs((None, C, L2), lambda c, t: (c, 0, 0)),
                   _bs((None, SUB, L2), lambda c, t: (c, 0, 0)), _bs((None, 1, CH), lambda c, t: (c, 0, 0))],
        out_shape=[jax.ShapeDtypeStruct((T, E), BF16), jax.ShapeDtypeStruct((NC, C, L2), F32),
                   jax.ShapeDtypeStruct((NC, C, L2), F32), jax.ShapeDtypeStruct((NC, SUB, L2), F32),
                   jax.ShapeDtypeStruct((NC, 1, CH), F32)],
        scratch_shapes=[pltpu.VMEM((tT, L2), F32), pltpu.VMEM((SUB, L2), F32), pltpu.VMEM((8, SUB, L), F32),
                        pltpu.VMEM((CH, L2), BF16), pltpu.VMEM((CH, L2), BF16), pltpu.VMEM((CH, L2), F32), pltpu.VMEM((CH, L2), F32)],
        compiler_params=_params(("parallel", "arbitrary")),
    )(dy1, proj, hs, bbd, cbd, abar_r, abar_i, dskip)


def _compact(v, NC):
    G, P, C = v.shape
    return jnp.transpose(v.reshape(NC, G // NC, P, C), (0, 3, 1, 2)).reshape(NC, C, (G // NC) * P)


def _uncompact(d, G):
    NC, C, L = d.shape
    gpc = G // NC
    return jnp.transpose(d.reshape(NC, C, gpc, L // gpc), (0, 2, 3, 1)).reshape(G, L // gpc, C)


def _cum_rows(name, x, bias, reverse, log_sig):
    T, L = x.shape

    def body(x_ref, b_ref, o_ref):
        row = lax.broadcasted_iota(jnp.int32, (SUB, L), 0)
        if reverse:
            row = (SUB - 1) - row
        nblk = T // SUB

        def step(jj, c):
            r0 = pl.multiple_of(((nblk - 1 - jj) if reverse else jj) * SUB, SUB)
            v = x_ref[pl.ds(r0, SUB), :] + b_ref[...]
            if log_sig:
                v = _log_sigmoid(v)
            for d in (1, 2, 4):
                v = v + jnp.where(row >= d, pltpu.roll(v, (SUB - d) if reverse else d, 0), 0.0)
            v = v + c
            o_ref[pl.ds(r0, SUB), :] = v
            e = 0 if reverse else SUB - 1
            return jnp.broadcast_to(v[e:e + 1, :], (SUB, L))

        lax.fori_loop(0, nblk, step, jnp.zeros((SUB, L), F32))

    return pl.pallas_call(body, name=name, out_shape=jax.ShapeDtypeStruct((T, L), F32),
                          compiler_params=pltpu.CompilerParams(vmem_limit_bytes=VMEM_LIMIT))(x, bias)


def _qk_norm(name, proj, wq, wk, H):
    T = proj.shape[0]
    Dh = FOX_HEAD_DIM
    tT = _t(512, T)
    HB = math.gcd(NORM_HEADS, H)

    def body(q_ref, k_ref, wq_ref, wk_ref, qn_ref, kn_ref):
        for hh in range(HB):
            lanes = slice(hh * Dh, (hh + 1) * Dh)
            q, k = q_ref[:, lanes], k_ref[:, lanes]
            qn_ref[:, lanes] = ((q * _rms(q)) * wq_ref[...]).astype(BF16)
            kn_ref[:, lanes] = ((k * _rms(k)) * wk_ref[...]).astype(BF16)

    blk = lambda off: _bs((tT, HB * Dh), lambda t, h: (t, h + off))
    return pl.pallas_call(
        body, name=name, grid=(T // tT, H // HB),
        in_specs=[blk(0), blk(H // HB), _bs((1, Dh), lambda t, h: (0, 0)), _bs((1, Dh), lambda t, h: (0, 0))],
        out_specs=[blk(0), blk(0)], out_shape=[jax.ShapeDtypeStruct((T, H * Dh), BF16)] * 2,
        compiler_params=_params(("parallel", "parallel")))(proj, proj, wq, wk)


def _qk_norm_bwd(name, proj, wq, wk, dqn, dkn, H):
    T = proj.shape[0]
    Dh = FOX_HEAD_DIM
    tT = _t(512, T)
    HB = math.gcd(NORM_HEADS, H)

    def body(q_ref, k_ref, wq_ref, wk_ref, dqn_ref, dkn_ref, dq_ref, dk_ref, dwq_ref, dwk_ref):
        @pl.when((pl.program_id(0) == 0) & (pl.program_id(1) == 0))
        def _():
            dwq_ref[...] = jnp.zeros_like(dwq_ref)
            dwk_ref[...] = jnp.zeros_like(dwk_ref)

        for hh in range(HB):
            lanes = slice(hh * Dh, (hh + 1) * Dh)
            dq, tq = _rms_bwd(q_ref[:, lanes], wq_ref[...], dqn_ref[:, lanes])
            dk, tk = _rms_bwd(k_ref[:, lanes], wk_ref[...], dkn_ref[:, lanes])
            dq_ref[:, lanes] = dq.astype(BF16)
            dk_ref[:, lanes] = dk.astype(BF16)
            dwq_ref[...] += _colsum(tq)
            dwk_ref[...] += _colsum(tk)

    blk = lambda off: _bs((tT, HB * Dh), lambda t, h: (t, h + off))
    one = _bs((1, Dh), lambda t, h: (0, 0))
    return pl.pallas_call(
        body, name=name, grid=(T // tT, H // HB),
        in_specs=[blk(0), blk(H // HB), one, one, blk(0), blk(0)],
        out_specs=[blk(0), blk(0), one, one],
        out_shape=[jax.ShapeDtypeStruct((T, H * Dh), BF16)] * 2 + [jax.ShapeDtypeStruct((1, Dh), F32)] * 2,
        compiler_params=_params(("arbitrary", "arbitrary")))(proj, proj, wq, wk, dqn, dkn)


def _attn_fwd(name, qn, kn, proj, cum_q, cum_k, H):
    T = qn.shape[0]
    Dh = FOX_HEAD_DIM
    tq = cum_k.shape[3]
    nq = T // tq
    scale = Dh ** -0.5
    nt = (((1,), (1,)), ((), ()))

    sq = _t(ATTN_SUB, tq)
    rep = tq // LANES
    HP = ATTN_HEADS
    assert H % HP == 0 and Dh == LANES

    def body(q_ref, k_ref, v_ref, cq_ref, ck_ref, o_ref, lse_ref, m_sc, l_sc, acc_sc):
        i = pl.program_id(1)
        m_sc[...] = jnp.full_like(m_sc, NEG)
        l_sc[...] = jnp.zeros_like(l_sc)
        acc_sc[...] = jnp.zeros_like(acc_sc)
        kloc = lax.broadcasted_iota(jnp.int32, (sq, tq), 1)
        qloc = lax.broadcasted_iota(jnp.int32, (sq, tq), 0)

        def chunk(kc, masked):
            ks = pl.multiple_of(kc * tq, tq)
            for hh in range(HP):
                lanes = slice(hh * Dh, (hh + 1) * Dh)
                k = k_ref[pl.ds(ks, tq), lanes]
                v16 = v_ref[pl.ds(ks, tq), lanes].astype(BF16)
                ck = ck_ref[hh, kc]
                for r in range(tq // sq):
                    rows = pl.ds(r * sq, sq)
                    s = lax.dot_general(q_ref[rows, lanes], k, nt, preferred_element_type=F32) * scale + (jnp.tile(cq_ref[hh, rows, :], (1, rep)) - ck)
                    if masked:
                        s = jnp.where(kloc <= qloc + r * sq, s, NEG)
                    m_old = m_sc[rows, lanes]
                    m_new = jnp.maximum(m_old, jnp.max(s, axis=1, keepdims=True))
                    alpha = jnp.exp(m_old - m_new)
                    p = jnp.exp(s - jnp.tile(m_new, (1, rep)))
                    l_sc[rows, lanes] = alpha * l_sc[rows, lanes] + jnp.sum(p, axis=1, keepdims=True)
                    acc_sc[rows, lanes] = alpha * acc_sc[rows, lanes] + jnp.dot(p.astype(BF16), v16, preferred_element_type=F32)
                    m_sc[rows, lanes] = m_new

        def below(kc, c):
            chunk(kc, False)
            return c

        lax.fori_loop(0, i, below, 0)
        chunk(i, True)
        o_ref[...] = acc_sc[...] / l_sc[...]
        for hh in range(HP):
            lanes = slice(hh * Dh, (hh + 1) * Dh)
            lse_ref[hh] = m_sc[:, lanes] + jnp.log(l_sc[:, lanes])

    W2 = HP * Dh
    return pl.pallas_call(
        body, name=name, grid=(H // HP, nq),
        in_specs=[_bs((tq, W2), lambda h, i: (i, h)), _bs((T, W2), lambda h, i: (0, h)), _bs((T, W2), lambda h, i: (0, 2 * (H // HP) + h)),
                  _bs((HP, tq, LANES), lambda h, i: (h, i, 0)), _bs((HP, nq, 1, tq), lambda h, i: (h, 0, 0, 0))],
        out_specs=[_bs((tq, W2), lambda h, i: (i, h)), _bs((HP, tq, LANES), lambda h, i: (h, i, 0))],
        out_shape=[jax.ShapeDtypeStruct((T, H * Dh), F32), jax.ShapeDtypeStruct((H, T, LANES), F32)],
        scratch_shapes=[pltpu.VMEM((tq, W2), F32), pltpu.VMEM((tq, W2), F32), pltpu.VMEM((tq, W2), F32)],
        compiler_params=_params(("parallel", "parallel")))(qn, kn, proj, cum_q, cum_k)


def _attn_bwd(name, qn, kn, proj, do, o, lse, cum_q, cum_k, H):
    T = qn.shape[0]
    Dh = FOX_HEAD_DIM
    tq = cum_k.shape[3]
    nq = T // tq
    scale = Dh ** -0.5
    nt = (((1,), (1,)), ((), ()))
    tn = (((0,), (0,)), ((), ()))
    assert H <= LANES

    sq = _t(ATTN_SUB, tq)
    rep = tq // LANES
    HP = ATTN_HEADS
    W2 = HP * Dh
    assert H % HP == 0 and Dh == LANES

    def body(q_ref, k_ref, v_ref, do_ref, o_ref, lse_ref, cq_ref, ck_ref, dq_ref, dk_ref, dv_ref, dcq_ref, dck_ref,
             delta, cql, dk_sc, dv_sc, dck_sc):
        h, j = pl.program_id(0), pl.program_id(1)

        @pl.when((h == 0) & (j == 0))
        def _():
            dcq_ref[...] = jnp.zeros_like(dcq_ref)

        @pl.when(j == 0)
        def _():
            dq_ref[...] = jnp.zeros_like(dq_ref)
            for hh in range(HP):
                lanes = slice(hh * Dh, (hh + 1) * Dh)
                delta[hh] = jnp.broadcast_to(jnp.sum(do_ref[:, lanes] * o_ref[:, lanes], axis=1, keepdims=True), (T, LANES))
            cql[...] = cq_ref[...] - lse_ref[...]

        lane_id = lax.broadcasted_iota(jnp.int32, (sq, LANES), 1)
        dk_sc[...] = jnp.zeros_like(dk_sc)
        dv_sc[...] = jnp.zeros_like(dv_sc)
        dck_sc[...] = jnp.zeros_like(dck_sc)
        kloc = lax.broadcasted_iota(jnp.int32, (sq, tq), 1)
        qloc = lax.broadcasted_iota(jnp.int32, (sq, tq), 0)

        def qblk(i, masked):
            for hh in range(HP):
                lanes = slice(hh * Dh, (hh + 1) * Dh)
                k = k_ref[:, lanes]
                v16 = v_ref[:, lanes].astype(BF16)
                ck = ck_ref[hh]
                for r in range(tq // sq):
                    rows = pl.ds(pl.multiple_of(i * tq + r * sq, sq), sq)
                    q = q_ref[rows, lanes]
                    do16 = do_ref[rows, lanes].astype(BF16)
                    e = lax.dot_general(q, k, nt, preferred_element_type=F32) * scale + (jnp.tile(cql[hh, rows, :], (1, rep)) - ck)
                    p = jnp.exp(e)
                    if masked:
                        p = jnp.where(kloc <= qloc + r * sq, p, 0.0)
                    dv_sc[:, lanes] += lax.dot_general(p.astype(BF16), do16, tn, preferred_element_type=F32)
                    dp = lax.dot_general(do16, v16, nt, preferred_element_type=F32)
                    ds = p * (dp - jnp.tile(delta[hh, rows, :], (1, rep)))
                    ds16 = ds.astype(BF16)
                    dk_sc[:, lanes] += lax.dot_general(ds16, q, tn, preferred_element_type=F32)
                    dq_ref[rows, lanes] += jnp.dot(ds16, k, preferred_element_type=F32) * scale
                    dcq_ref[rows, :] += jnp.where(lane_id == h * HP + hh, jnp.sum(ds, axis=1, keepdims=True), 0.0)
                    dck_sc[hh] += jnp.sum(ds, axis=0, keepdims=True)

        def above(i, c):
            qblk(i, False)
            return c

        qblk(j, True)
        lax.fori_loop(j + 1, nq, above, 0)
        dk_ref[...] = dk_sc[...] * scale
        dv_ref[...] = dv_sc[...].astype(BF16)
        for hh in range(HP):
            dck_ref[hh] = -dck_sc[hh]

    whole = lambda off: _bs((T, W2), lambda h, j: (0, h + off))
    blk = lambda off: _bs((tq, W2), lambda h, j: (j, h + off))
    return pl.pallas_call(
        body, name=name, grid=(H // HP, nq),
        in_specs=[whole(0), blk(0), blk(2 * (H // HP)), whole(0), whole(0), _bs((HP, T, LANES), lambda h, j: (h, 0, 0)),
                  _bs((HP, T, LANES), lambda h, j: (h, 0, 0)), _bs((HP, None, 1, tq), lambda h, j: (h, j, 0, 0))],
        out_specs=[whole(0), blk(0), blk(0), _bs((T, LANES), lambda h, j: (0, 0)),
                   _bs((HP, None, 1, tq), lambda h, j: (h, j, 0, 0))],
        out_shape=[jax.ShapeDtypeStruct((T, H * Dh), F32), jax.ShapeDtypeStruct((T, H * Dh), F32), jax.ShapeDtypeStruct((T, H * Dh), BF16),
                   jax.ShapeDtypeStruct((T, LANES), F32), jax.ShapeDtypeStruct((H, nq, 1, tq), F32)],
        scratch_shapes=[pltpu.VMEM((HP, T, LANES), F32), pltpu.VMEM((HP, T, LANES), F32), pltpu.VMEM((tq, W2), F32), pltpu.VMEM((tq, W2), F32),
                        pltpu.VMEM((HP, 1, tq), F32)],
        compiler_params=_params(("arbitrary", "arbitrary")))(qn, kn, proj, do, o, lse, cum_q, cum_k)


def _pool_fwd(name, proj, E):
    T = proj.shape[0]
    PG = len(POOL_WINDOWS)
    PD = E // PG
    tT = _t(256, T)
    hb = tT // POOL_HALO

    def body(u_ref, halo_ref, o_ref, buf):
        g, tb = pl.program_id(0), pl.program_id(1)
        u = u_ref[...]
        buf[pl.ds(POOL_HALO, tT), :] = u
        buf[pl.ds(0, POOL_HALO), :] = jnp.where(tb == 0, 0.0, halo_ref[...])
        t = tb * tT + lax.broadcasted_iota(jnp.int32, (tT, 1), 0)
        for gi, w in enumerate(POOL_WINDOWS):
            @pl.when(g == gi)
            def _():
                acc = u
                for d in range(1, w):
                    acc = acc + buf[pl.ds(POOL_HALO - d, tT), :]
                cnt = jnp.minimum(t + 1, w).astype(F32)
                o_ref[...] = (acc / cnt - u).astype(BF16)

    return pl.pallas_call(
        body, name=name, grid=(PG, T // tT),
        in_specs=[_bs((tT, PD), lambda g, t: (t, g)), _bs((POOL_HALO, PD), lambda g, t: (jnp.maximum(t * hb - 1, 0), g))],
        out_specs=_bs((tT, PD), lambda g, t: (t, g)), out_shape=jax.ShapeDtypeStruct((T, E), BF16),
        scratch_shapes=[pltpu.VMEM((tT + POOL_HALO, PD), F32)],
        compiler_params=_params(("parallel", "parallel")))(proj, proj)


def _pool_bwd(name, dpm, E):
    T = dpm.shape[0]
    PG = len(POOL_WINDOWS)
    PD = E // PG
    tT = _t(256, T)
    hb = tT // POOL_HALO
    nT = T // tT

    def body(d_ref, halo_ref, o_ref, buf):
        g, tb = pl.program_id(0), pl.program_id(1)
        d = d_ref[...]
        t = tb * tT + lax.broadcasted_iota(jnp.int32, (tT, 1), 0)
        th = (tb + 1) * tT + lax.broadcasted_iota(jnp.int32, (POOL_HALO, 1), 0)
        for gi, w in enumerate(POOL_WINDOWS):
            @pl.when(g == gi)
            def _():
                dn = d / jnp.minimum(t + 1, w).astype(F32)
                buf[pl.ds(0, tT), :] = dn
                buf[pl.ds(tT, POOL_HALO), :] = jnp.where(tb == nT - 1, 0.0, halo_ref[...] / jnp.minimum(th + 1, w).astype(F32))
                acc = dn
                for s in range(1, w):
                    acc = acc + buf[pl.ds(s, tT), :]
                o_ref[...] = (acc - d).astype(BF16)

    return pl.pallas_call(
        body, name=name, grid=(PG, nT),
        in_specs=[_bs((tT, PD), lambda g, t: (t, g)), _bs((POOL_HALO, PD), lambda g, t: (jnp.minimum((t + 1) * hb, T // POOL_HALO - 1), g))],
        out_specs=_bs((tT, PD), lambda g, t: (t, g)), out_shape=jax.ShapeDtypeStruct((T, E), BF16),
        scratch_shapes=[pltpu.VMEM((tT + POOL_HALO, PD), F32)],
        compiler_params=_params(("parallel", "parallel")))(dpm, dpm)


def _coords():
    x, y, c = lax.axis_index("x"), lax.axis_index("y"), lax.axis_index("c")
    chips = [(1 - x, y), (x, 1 - y), (1 - x, 1 - y)]
    return x, y, c, 2 * x + y, (x, y, 1 - c), chips


def _chip_allgather(name, bufs):
    n = len(bufs)

    def body(*refs):
        outs = refs[n:2 * n]
        send, recv, fsend, frecv = refs[2 * n:]
        x, y, c, p, sib, chips = _coords()

        def direct(t, j, chip):
            return pltpu.make_async_remote_copy(src_ref=outs[t].at[p, c], dst_ref=outs[t].at[p, c], send_sem=send.at[t, j],
                                                recv_sem=recv.at[t, j], device_id=(*chip, c), device_id_type=MESH)

        def landed(t, j, chip):
            blk = outs[t].at[2 * chip[0] + chip[1], c]
            return pltpu.make_async_remote_copy(src_ref=blk, dst_ref=blk, send_sem=send.at[t, j],
                                                recv_sem=recv.at[t, j], device_id=(*chip, c), device_id_type=MESH)

        def passed(t, j, chip, half):
            blk = outs[t].at[2 * chip[0] + chip[1], half]
            return pltpu.make_async_remote_copy(src_ref=blk, dst_ref=blk, send_sem=fsend.at[t, j], recv_sem=frecv.at[t, j],
                                                device_id=sib, device_id_type=MESH)

        first = [direct(t, j, chip) for t in range(n) for j, chip in enumerate(chips)]
        for cp in first:
            cp.start()
        fwd = []
        for j, chip in enumerate(chips):
            for t in range(n):
                landed(t, j, chip).wait_recv()
                f = passed(t, j, chip, c)
                f.start()
                fwd.append(f)
        for j, chip in enumerate(chips):
            for t in range(n):
                passed(t, j, chip, 1 - c).wait_recv()
        for cp in first + fwd:
            cp.wait_send()

    return pl.pallas_call(
        body, name=name, in_specs=[ANY] * n, out_specs=[ANY] * n,
        out_shape=[jax.ShapeDtypeStruct(a.shape, a.dtype) for a in bufs],
        input_output_aliases={t: t for t in range(n)},
        scratch_shapes=[pltpu.SemaphoreType.DMA((n, 3))] * 4,
    )(*bufs)


SEM = pl.BlockSpec(memory_space=pltpu.SEMAPHORE)
TOKEN = jax.ShapeDtypeStruct((SUB, LANES), F32)


def _split_params():
    return pltpu.CompilerParams(has_side_effects=pltpu.SideEffectType.DATAFLOW_SIDE_EFFECTING)


def _struct(a):
    return jax.ShapeDtypeStruct(a.shape, a.dtype)


def _gather_start(name, bufs, deps):
    n, nd = len(bufs), len(deps)

    def body(*refs):
        outs = refs[n + nd:2 * n + nd]
        send, recv, token = refs[2 * n + nd:]
        x, y, c, p, sib, chips = _coords()
        for t in range(n):
            for j, chip in enumerate(chips):
                pltpu.make_async_remote_copy(src_ref=outs[t].at[p, c], dst_ref=outs[t].at[p, c], send_sem=send.at[3 * t + j],
                                             recv_sem=recv.at[3 * t + j], device_id=(*chip, c), device_id_type=MESH).start()
        token[...] = jnp.zeros_like(token)

    res = pl.pallas_call(
        body, name=name, in_specs=[ANY] * (n + nd), out_specs=[ANY] * n + [SEM, SEM, pl.BlockSpec(memory_space=pltpu.VMEM)],
        out_shape=[_struct(a) for a in bufs] + [pltpu.SemaphoreType.DMA((3 * n,)), pltpu.SemaphoreType.DMA((3 * n,)), TOKEN],
        input_output_aliases={t: t for t in range(n)}, compiler_params=_split_params(),
    )(*bufs, *deps)
    return list(res[:n]), res[n], res[n + 1], res[n + 2]


def _gather_wait(name, bufs, send, recv, after):
    n = len(bufs)

    def body(*refs):
        send_r, recv_r = refs[n], refs[n + 1]
        outs = refs[n + 3:2 * n + 3]
        x, y, c, p, sib, chips = _coords()
        for t in range(n):
            for j, chip in enumerate(chips):
                cp = pltpu.make_async_remote_copy(src_ref=outs[t].at[p, c], dst_ref=outs[t].at[2 * chip[0] + chip[1], c], send_sem=send_r.at[3 * t + j],
                                                  recv_sem=recv_r.at[3 * t + j], device_id=(*chip, c), device_id_type=MESH)
                cp.wait_send()
                cp.wait_recv()

    return list(pl.pallas_call(
        body, name=name, in_specs=[ANY] * n + [SEM, SEM, ANY], out_specs=[ANY] * n, out_shape=[_struct(a) for a in bufs],
        input_output_aliases={t: t for t in range(n)}, compiler_params=_split_params(),
    )(*bufs, send, recv, after))


def _gather_forward(name, bufs):
    n = len(bufs)

    def body(*refs):
        outs = refs[n:2 * n]
        fsend, frecv = refs[2 * n:]
        x, y, c, p, sib, chips = _coords()

        def passed(t, j, chip, half):
            blk = outs[t].at[2 * chip[0] + chip[1], half]
            return pltpu.make_async_remote_copy(src_ref=blk, dst_ref=blk, send_sem=fsend.at[t, j], recv_sem=frecv.at[t, j],
                                                device_id=sib, device_id_type=MESH)

        fwd = [passed(t, j, chip, c) for t in range(n) for j, chip in enumerate(chips)]
        for cp in fwd:
            cp.start()
        for t in range(n):
            for j, chip in enumerate(chips):
                passed(t, j, chip, 1 - c).wait_recv()
        for cp in fwd:
            cp.wait_send()

    return list(pl.pallas_call(
        body, name=name, in_specs=[ANY] * n, out_specs=[ANY] * n, out_shape=[_struct(a) for a in bufs],
        input_output_aliases={t: t for t in range(n)}, scratch_shapes=[pltpu.SemaphoreType.DMA((n, 3))] * 2,
    )(*bufs))


def _relations():
    x, y, c = lax.axis_index("x"), lax.axis_index("y"), lax.axis_index("c")
    out = []
    for code in range(1, 8):
        tx = 1 - x if code & 4 else x
        ty = 1 - y if code & 2 else y
        tc = 1 - c if code & 1 else c
        out.append((code - 1, (tx, ty, tc), 2 * tx + ty, tc))
    return out


def _full_exchange_start(name, parts):
    n = len(parts)
    lands = [lax.empty((7,) + a.shape[2:], a.dtype) for a in parts]

    def body(*refs):
        src, dst = refs[2 * n:3 * n], refs[3 * n:4 * n]
        send, recv, token = refs[4 * n:]
        for t in range(n):
            for k, dev, q, half in _relations():
                pltpu.make_async_remote_copy(src_ref=src[t].at[half, q], dst_ref=dst[t].at[k], send_sem=send.at[7 * t + k],
                                             recv_sem=recv.at[7 * t + k], device_id=dev, device_id_type=MESH).start()
        token[...] = jnp.zeros_like(token)

    res = pl.pallas_call(
        body, name=name, in_specs=[ANY] * (2 * n), out_specs=[ANY] * (2 * n) + [SEM, SEM, pl.BlockSpec(memory_space=pltpu.VMEM)],
        out_shape=[_struct(a) for a in parts + lands] + [pltpu.SemaphoreType.DMA((7 * n,)), pltpu.SemaphoreType.DMA((7 * n,)), TOKEN],
        input_output_aliases={t: t for t in range(2 * n)}, compiler_params=_split_params(),
    )(*parts, *lands)
    return list(res[:n]), list(res[n:2 * n]), res[2 * n], res[2 * n + 1], res[2 * n + 2]


def _full_exchange_wait(name, parts, lands, send, recv, after):
    n = len(parts)

    def body(*refs):
        send_r, recv_r = refs[2 * n], refs[2 * n + 1]
        src, dst = refs[2 * n + 3:3 * n + 3], refs[3 * n + 3:4 * n + 3]
        for t in range(n):
            for k, dev, q, half in _relations():
                cp = pltpu.make_async_remote_copy(src_ref=src[t].at[half, q], dst_ref=dst[t].at[k], send_sem=send_r.at[7 * t + k],
                                                  recv_sem=recv_r.at[7 * t + k], device_id=dev, device_id_type=MESH)
                cp.wait_send()
                cp.wait_recv()

    res = pl.pallas_call(
        body, name=name, in_specs=[ANY] * (2 * n) + [SEM, SEM, ANY], out_specs=[ANY] * (2 * n),
        out_shape=[_struct(a) for a in parts + lands], input_output_aliases={t: t for t in range(2 * n)},
        compiler_params=_split_params(),
    )(*parts, *lands, send, recv, after)
    return list(res[:n]), list(res[n:])


def _chip_exchange_start(name, sums):
    n = len(sums)
    lands = [lax.empty((3,) + a.shape[1:], a.dtype) for a in sums]

    def body(*refs):
        src, dst = refs[2 * n:3 * n], refs[3 * n:4 * n]
        send, recv, token = refs[4 * n:]
        x, y, c, p, sib, chips = _coords()
        for t in range(n):
            for j, chip in enumerate(chips):
                pltpu.make_async_remote_copy(src_ref=src[t].at[2 * chip[0] + chip[1]], dst_ref=dst[t].at[j], send_sem=send.at[3 * t + j],
                                             recv_sem=recv.at[3 * t + j], device_id=(*chip, c), device_id_type=MESH).start()
        token[...] = jnp.zeros_like(token)

    res = pl.pallas_call(
        body, name=name, in_specs=[ANY] * (2 * n), out_specs=[ANY] * (2 * n) + [SEM, SEM, pl.BlockSpec(memory_space=pltpu.VMEM)],
        out_shape=[_struct(a) for a in sums + lands] + [pltpu.SemaphoreType.DMA((3 * n,)), pltpu.SemaphoreType.DMA((3 * n,)), TOKEN],
        input_output_aliases={t: t for t in range(2 * n)}, compiler_params=_split_params(),
    )(*sums, *lands)
    return list(res[:n]), list(res[n:2 * n]), res[2 * n], res[2 * n + 1], res[2 * n + 2]


def _chip_exchange_wait(name, sums, lands, send, recv, after):
    n = len(sums)

    def body(*refs):
        send_r, recv_r = refs[2 * n], refs[2 * n + 1]
        src, dst = refs[2 * n + 3:3 * n + 3], refs[3 * n + 3:4 * n + 3]
        x, y, c, p, sib, chips = _coords()
        for t in range(n):
            for j, chip in enumerate(chips):
                cp = pltpu.make_async_remote_copy(src_ref=src[t].at[2 * chip[0] + chip[1]], dst_ref=dst[t].at[j], send_sem=send_r.at[3 * t + j],
                                                  recv_sem=recv_r.at[3 * t + j], device_id=(*chip, c), device_id_type=MESH)
                cp.wait_send()
                cp.wait_recv()

    res = pl.pallas_call(
        body, name=name, in_specs=[ANY] * (2 * n) + [SEM, SEM, ANY], out_specs=[ANY] * (2 * n),
        out_shape=[_struct(a) for a in sums + lands], input_output_aliases={t: t for t in range(2 * n)},
        compiler_params=_split_params(),
    )(*sums, *lands, send, recv, after)
    return list(res[:n]), list(res[n:])


def _pair_exchange(name, parts):
    n = len(parts)

    def body(*refs):
        ins, outs = refs[:n], refs[n:2 * n]
        send, recv = refs[2 * n:]
        x, y, c, p, sib, chips = _coords()
        cps = [pltpu.make_async_remote_copy(src_ref=ins[t].at[1 - c], dst_ref=outs[t], send_sem=send.at[t], recv_sem=recv.at[t],
                                            device_id=sib, device_id_type=MESH) for t in range(n)]
        for cp in cps:
            cp.start()
        for cp in cps:
            cp.wait()

    return pl.pallas_call(
        body, name=name, in_specs=[ANY] * n, out_specs=[ANY] * n,
        out_shape=[jax.ShapeDtypeStruct(a.shape[1:], a.dtype) for a in parts],
        scratch_shapes=[pltpu.SemaphoreType.DMA((n,))] * 2,
    )(*parts)


def _chip_exchange(name, sums):
    n = len(sums)

    def body(*refs):
        ins, outs = refs[:n], refs[n:2 * n]
        send, recv = refs[2 * n:]
        x, y, c, p, sib, chips = _coords()
        cps = [pltpu.make_async_remote_copy(src_ref=ins[t].at[2 * chip[0] + chip[1]], dst_ref=outs[t].at[j], send_sem=send.at[t, j],
                                            recv_sem=recv.at[t, j], device_id=(*chip, c), device_id_type=MESH)
               for t in range(n) for j, chip in enumerate(chips)]
        for cp in cps:
            cp.start()
        for cp in cps:
            cp.wait()

    return pl.pallas_call(
        body, name=name, in_specs=[ANY] * n, out_specs=[ANY] * n,
        out_shape=[jax.ShapeDtypeStruct((3,) + a.shape[1:], a.dtype) for a in sums],
        scratch_shapes=[pltpu.SemaphoreType.DMA((n, 3))] * 2,
    )(*sums)


def _pair_share(name, bufs, items, deps=()):
    n = len(items)
    nb = len(bufs)
    nd = len(deps)

    def body(*refs):
        outs = refs[nb + nd:2 * nb + nd]
        send, recv = refs[2 * nb + nd:]
        x, y, c, p, sib, chips = _coords()

        def blk(t, half):
            o, lead = items[t]
            return outs[o].at[p if lead == 'chip' else lead, half]

        def swap(t, half):
            return pltpu.make_async_remote_copy(src_ref=blk(t, half), dst_ref=blk(t, half), send_sem=send.at[t], recv_sem=recv.at[t],
                                                device_id=sib, device_id_type=MESH)

        cps = [swap(t, c) for t in range(n)]
        for cp in cps:
            cp.start()
        for t in range(n):
            swap(t, 1 - c).wait_recv()
        for cp in cps:
            cp.wait_send()

    return list(pl.pallas_call(
        body, name=name, in_specs=[ANY] * (nb + nd), out_specs=[ANY] * nb,
        out_shape=[jax.ShapeDtypeStruct(b.shape, b.dtype) for b in bufs],
        input_output_aliases={t: t for t in range(nb)},
        scratch_shapes=[pltpu.SemaphoreType.DMA((n,))] * 2,
    )(*bufs, *deps))


def _flat2(a, lead):
    return a.reshape(a.shape[:lead] + (-1, a.shape[-1]))


def _reduce_begin(tag, parts):
    parts, lands, send, recv, token = _full_exchange_start(f"rs_start_{tag}", parts)
    return (parts, lands, send, recv), token


def _reduce_end(tag, state, after, dests, bufs, buf_shapes):
    c = lax.axis_index("c").astype(jnp.int32)
    p = (2 * lax.axis_index("x") + lax.axis_index("y")).astype(jnp.int32)
    parts, lands = _full_exchange_wait(f"rs_wait_{tag}", *state, after)

    def total(a, *others):
        s = a.astype(F32)
        for b in others:
            s = s + b.astype(F32)
        return (s,)

    for t, (mine, theirs) in enumerate(zip(parts, lands)):
        o, lead = dests[t]
        shape = buf_shapes[o]
        rows, cols = shape[2], shape[3]
        m3, t3 = mine.reshape(2 * N_CHIPS, rows, cols), theirs.reshape(7, rows, cols)
        pre = jnp.stack([c * N_CHIPS + p] + [jnp.int32(k) for k in range(7)] + [c, p if lead == 'chip' else jnp.int32(lead)])
        out = ('x', shape, F32, (None, None, 'tr', cols), lambda r, pr: (pr[9], pr[8], r, 0))
        bufs[o] = _rows(f"rs_sum_{tag}_{t}", total, [(m3, 's', cols, 0)] + [(t3, 's', cols, 1 + k) for k in range(7)], [out], 256,
                        pre=pre, into=bufs[o])[0]


def kernel(x, norm_w, out_proj, s5_in_proj, s5_a_re, s5_a_im, s5_log_dt, s5_b_re, s5_b_im, s5_c_re, s5_c_im, s5_d, s5_w_glu, s5_b_glu, fox_in_proj, fox_q_norm, fox_k_norm, fox_f_bias, pool_in_proj, pool_w_group, pool_scale, loss_target, m_norm_w, m_out_proj, m_s5_in_proj, m_s5_a_re, m_s5_a_im, m_s5_log_dt, m_s5_b_re, m_s5_b_im, m_s5_c_re, m_s5_c_im, m_s5_d, m_s5_w_glu, m_s5_b_glu, m_fox_in_proj, m_fox_q_norm, m_fox_k_norm, m_fox_f_bias, m_pool_in_proj, m_pool_w_group, m_pool_scale, v_norm_w, v_out_proj, v_s5_in_proj, v_s5_a_re, v_s5_a_im, v_s5_log_dt, v_s5_b_re, v_s5_b_im, v_s5_c_re, v_s5_c_im, v_s5_d, v_s5_w_glu, v_s5_b_glu, v_fox_in_proj, v_fox_q_norm, v_fox_k_norm, v_fox_f_bias, v_pool_in_proj, v_pool_w_group, v_pool_scale):
    weights = dict(norm_w=norm_w, out_proj=out_proj, s5_in_proj=s5_in_proj, s5_a_re=s5_a_re, s5_a_im=s5_a_im, s5_log_dt=s5_log_dt,
                   s5_b_re=s5_b_re, s5_b_im=s5_b_im, s5_c_re=s5_c_re, s5_c_im=s5_c_im, s5_d=s5_d, s5_w_glu=s5_w_glu, s5_b_glu=s5_b_glu,
                   fox_in_proj=fox_in_proj, fox_q_norm=fox_q_norm, fox_k_norm=fox_k_norm, fox_f_bias=fox_f_bias,
                   pool_in_proj=pool_in_proj, pool_w_group=pool_w_group, pool_scale=pool_scale)
    mom_m = dict(norm_w=m_norm_w, out_proj=m_out_proj, s5_in_proj=m_s5_in_proj, s5_a_re=m_s5_a_re, s5_a_im=m_s5_a_im, s5_log_dt=m_s5_log_dt,
                 s5_b_re=m_s5_b_re, s5_b_im=m_s5_b_im, s5_c_re=m_s5_c_re, s5_c_im=m_s5_c_im, s5_d=m_s5_d, s5_w_glu=m_s5_w_glu, s5_b_glu=m_s5_b_glu,
                 fox_in_proj=m_fox_in_proj, fox_q_norm=m_fox_q_norm, fox_k_norm=m_fox_k_norm, fox_f_bias=m_fox_f_bias,
                 pool_in_proj=m_pool_in_proj, pool_w_group=m_pool_w_group, pool_scale=m_pool_scale)
    mom_v = dict(norm_w=v_norm_w, out_proj=v_out_proj, s5_in_proj=v_s5_in_proj, s5_a_re=v_s5_a_re, s5_a_im=v_s5_a_im, s5_log_dt=v_s5_log_dt,
                 s5_b_re=v_s5_b_re, s5_b_im=v_s5_b_im, s5_c_re=v_s5_c_re, s5_c_im=v_s5_c_im, s5_d=v_s5_d, s5_w_glu=v_s5_w_glu, s5_b_glu=v_s5_b_glu,
                 fox_in_proj=v_fox_in_proj, fox_q_norm=v_fox_q_norm, fox_k_norm=v_fox_k_norm, fox_f_bias=v_fox_f_bias,
                 pool_in_proj=v_pool_in_proj, pool_w_group=v_pool_w_group, pool_scale=v_pool_scale)
    return _step(x, loss_target, weights, mom_m, mom_v)


BIG = ('out_proj', 's5_in_proj', 's5_w_glu', 'fox_in_proj', 'pool_in_proj', 'pool_w_group')
SMALL = ('norm_w', 's5_a_re', 's5_a_im', 's5_log_dt', 's5_b_re', 's5_b_im', 's5_c_re', 's5_c_im', 's5_d', 's5_b_glu',
         'fox_q_norm', 'fox_k_norm', 'fox_f_bias', 'pool_scale')
SMALL_SHARDED = ('s5_d', 's5_b_glu', 'pool_scale')
GROUP_AXIS_1 = ('s5_a_re', 's5_a_im', 's5_b_re', 's5_b_im', 's5_c_re', 's5_c_im')
ORDER = ('norm_w', 'out_proj', 's5_in_proj', 's5_a_re', 's5_a_im', 's5_log_dt', 's5_b_re', 's5_b_im', 's5_c_re', 's5_c_im', 's5_d',
         's5_w_glu', 's5_b_glu', 'fox_in_proj', 'fox_q_norm', 'fox_k_norm', 'fox_f_bias', 'pool_in_proj', 'pool_w_group', 'pool_scale')


def _split2(shape):
    if shape[0] % 2 == 0:
        return (2, shape[0] // 2) + tuple(shape[1:])
    assert shape[0] == 1 and shape[1] % 2 == 0
    return (2, shape[1] // 2) + tuple(shape[2:])


def _adamw_big(n, w, grads, mom_m, mom_v, delta, new_m, new_v):
    shape = w[n].shape
    if shape[-1] % LANES:
        f2 = lambda a: jnp.transpose(a.reshape(-1, shape[-1]))
        b2 = lambda a: jnp.transpose(a).reshape(shape)
    else:
        f2 = lambda a: a.reshape(-1, shape[-1])
        b2 = lambda a: a.reshape(shape)
    d_, m_, v_ = _adamw(f"adamw_{n}", f2(w[n]), f2(grads[n]), f2(mom_m[n]), f2(mom_v[n]))
    delta[n], new_m[n], new_v[n] = b2(d_), b2(m_), b2(v_)
    return d_


def _cast_weight(w, n, l, deps=()):
    p = (2 * lax.axis_index("x") + lax.axis_index("y")).astype(jnp.int32)
    a3 = w[n].reshape(w[n].shape[0], -1, w[n].shape[-1])
    layers, rows, cols = a3.shape
    out = ('x', (N_CHIPS, rows, cols), BF16, (None, 'tr', cols), lambda r, pr: (pr[0], r, 0))
    b = _rows(f"cast_{n}_{l}", lambda v: (v,), [(a3, 's', cols, 1)], [out], 256, pre=jnp.stack([p, jnp.int32(l)]), deps=deps)[0]
    return b.reshape(N_CHIPS, 2, rows // 2, cols)


def _step(x, loss_target, w, mom_m, mom_v):
    T, D = x.shape[1], x.shape[2]
    E = D
    G, P, C = w['s5_a_re'].shape[1], S5_STATE, S5_GROUP
    H = E // FOX_HEAD_DIM
    PG = len(POOL_WINDOWS)
    PD = E // PG
    NC = G // GROUPS_PER_CHUNK
    L = GROUPS_PER_CHUNK * P
    tq = _t(256, T)
    nq = T // tq

    phases = [[('s5_in_proj', 0)],
              [('s5_w_glu', 0), ('out_proj', 0)],
              [('out_proj', 1), ('fox_in_proj', 0)],
              [('out_proj', 2), ('pool_in_proj', 0), ('pool_w_group', 0), ('out_proj', 3), ('s5_in_proj', 1), ('s5_w_glu', 1)]]
    W = {}
    flight = {}

    def landed(keys, bufs):
        for k, b in zip(keys, bufs):
            W[k] = b.reshape(N_CHIPS, 2 * b.shape[2], b.shape[3])

    def take_phase(ph, after):
        bufs, send, recv, _ = flight.pop(ph)
        landed(phases[ph], _gather_forward(f"gather_{ph}_pass", _gather_wait(f"gather_{ph}_wait", bufs, send, recv, after)))

    small_full = {}
    chip = 2 * lax.axis_index("x") + lax.axis_index("y")
    sv = [lax.dynamic_update_index_in_dim(jnp.zeros((N_CHIPS, 2) + w[n].shape, F32), jnp.stack([w[n], w[n]]), chip, 0)
          for n in SMALL_SHARDED]
    got = _chip_allgather("gather_vectors", sv)
    for n, g in zip(SMALL_SHARDED, got):
        small_full[n] = jnp.transpose(g[:, 0], (1, 0, 2)).reshape(w[n].shape[0], E)
    after = [got[0]]
    for ph in range(len(phases)):
        flight[ph] = _gather_start(f"gather_{ph}_start", [_cast_weight(w, n, l, after if ph else ()) for n, l in phases[ph]], after)
        after = [flight[ph][3]]
    take_phase(0, after[0])
    gather_tokens = after

    norm_w = w['norm_w']
    h = x.reshape(T, D)
    saved = []
    dparts = {}

    def s5_consts(j):
        ar, ai, fr, fi = _s5_disc_fwd(f"s5_disc_{j}", w['s5_a_re'][j], w['s5_a_im'][j], w['s5_log_dt'][j].reshape(G, 1))
        br, bi = w['s5_b_re'][j].reshape(G * P, C), w['s5_b_im'][j].reshape(G * P, C)
        bbr, bbi = _s5_bbar(f"s5_bbar_{j}", fr.reshape(G * P, 1), fi.reshape(G * P, 1), br, bi)
        bbd = jnp.concatenate([_compact(bbr.reshape(G, P, C), NC), _compact(bbi.reshape(G, P, C), NC)], axis=2).astype(BF16)
        ct = lambda v: jnp.transpose(v, (0, 2, 1))
        cbd = jnp.concatenate([_compact(ct(w['s5_c_re'][j]), NC), -_compact(ct(w['s5_c_im'][j]), NC)], axis=2).astype(BF16)
        return dict(ar=ar, ai=ai, fr=fr, fi=fi, br=br, bi=bi, bbd=bbd, cbd=cbd,
                    ar3=ar.reshape(NC, 1, L), ai3=ai.reshape(NC, 1, L))

    for i in range(4):
        kind, j = i % 3, i // 3
        nw = norm_w[i].reshape(1, D)
        xn = _norm_fwd(f"norm_{i}", h, nw, deps=gather_tokens if i == 0 else ())
        if kind == 0:
            k5 = s5_consts(j)
            proj = _mm_proj(f"s5_proj_{i}", xn, W[('s5_in_proj', j)])
            dsk = small_full['s5_d'][j].reshape(1, E)
            y1, g, hs = _s5_fwd(f"s5_scan_{i}", proj, k5['bbd'], k5['cbd'], k5['ar3'], k5['ai3'], dsk, E)
            bglu = small_full['s5_b_glu'][j].reshape(1, E)
            if i == 0:
                take_phase(1, y1)

            def glu_epi(acc, b, y1t, z):
                lin = acc + b
                return lin, (_gelu(y1t) * _sigmoid(lin)) * _silu(z)

            lin, a = _mm_rowsharded(
                f"s5_glu_{i}", g, W[('s5_w_glu', j)], epi=glu_epi,
                extras=lambda tm, tn: [(bglu, _rowvec(tn)), (y1, _tile(tm, tn)), (proj, _tile(tm, tn, E // tn))],
                outs_fn=lambda tm, tn: [((T, E), F32, _tile(tm, tn)), ((T, E), BF16, _tile(tm, tn))])
            saved.append(dict(h=h, xn=xn, proj=proj, y1=y1, g=g, hs=hs, lin=lin, a=a, k5=k5, dsk=dsk))
        elif kind == 1:
            fox_w = jnp.transpose(W[('fox_in_proj', j)], (1, 0, 2)).reshape(D, -1)
            w_qkvz = fox_w[:, :4 * E]
            w_f = jnp.pad(fox_w[:, 4 * E:], ((0, 0), (0, LANES - H)))
            proj = _mm_plain(f"fox_proj_{i}", xn, w_qkvz)[0]
            flog = _mm_plain(f"fox_gate_proj_{i}", xn, w_f)[0]
            fb = jnp.pad(w['fox_f_bias'][j].reshape(1, H), ((0, 0), (0, LANES - H)))
            wq, wk = w['fox_q_norm'][j].reshape(1, FOX_HEAD_DIM), w['fox_k_norm'][j].reshape(1, FOX_HEAD_DIM)
            qn, kn = _qk_norm(f"fox_qk_norm_{i}", proj, wq, wk, H)
            cum = _cum_rows(f"fox_cum_{i}", flog, fb, False, True)
            cum_t = jnp.transpose(cum)[:H]
            cum_q = jnp.broadcast_to(cum_t[:, :, None], (H, T, LANES))
            cum_k = cum_t.reshape(H, nq, 1, tq)
            y, lse = _attn_fwd(f"fox_attn_{i}", qn, kn, proj, cum_q, cum_k, H)
            a = _rows(f"fox_gate_{i}", lambda yt, z: (yt * _silu(z),), [(y, 'r', E, 0), (proj, 'r', E, 3)], [('r', E, BF16)], 256)[0]
            saved.append(dict(h=h, xn=xn, proj=proj, flog=flog, fb=fb, wq=wq, wk=wk, qn=qn, kn=kn, cum_q=cum_q, cum_k=cum_k, y=y, lse=lse, a=a,
                              w_qkvz=w_qkvz, w_f=w_f))
        else:
            w_pg = jnp.transpose(W[('pool_w_group', j)].reshape(N_CHIPS, PG, PD // N_CHIPS, PD), (1, 0, 2, 3)).reshape(PG, PD, PD)
            proj = _mm_proj(f"pool_proj_{i}", xn, W[('pool_in_proj', j)])
            pm = _pool_fwd(f"pool_win_{i}", proj, E)
            scale = small_full['pool_scale'][j].reshape(1, E)
            tm, tn, tk = _t(512, T), _t(512, PD), _t(K_STEP, PD)
            kb, nb = PD // tk, PD // tn
            mixed, a = _mm(
                f"pool_mix_{i}", pm, w_pg, M=T, N=PD, K=PD, tm=tm, tn=tn, tk=tk, groups=PG,
                a_spec=_bs((tm, tk), lambda g, m, n, k: (m, g * kb + k)),
                b_spec=_bs((None, tk, tn), lambda g, m, n, k: (g, k, n)),
                extras=[(scale, _bs((1, tn), lambda g, m, n, k: (0, g * nb + n))),
                        (proj, _bs((tm, tn), lambda g, m, n, k: (m, E // tn + g * nb + n)))],
                epi=lambda acc, sc, z: (acc, (acc * sc) * _silu(z)),
                outs=[((T, E), F32, _bs((tm, tn), lambda g, m, n, k: (m, g * nb + n))),
                      ((T, E), BF16, _bs((tm, tn), lambda g, m, n, k: (m, g * nb + n)))])
            saved.append(dict(h=h, xn=xn, proj=proj, pm=pm, mixed=mixed, scale=scale, a=a, w_pg=w_pg))
        h = _mm_rowsharded(f"out_proj_{i}", saved[-1]['a'], W[('out_proj', i)], epi=lambda acc, r: (r + acc,),
                           extras=lambda tm, tn: [(h, _tile(tm, tn))],
                           outs_fn=lambda tm, tn: [((T, D), F32, _tile(tm, tn))])[0]
        if i < 2:
            take_phase(i + 2, h)

    dh, dh16, loss_cols = _loss(h, loss_target.reshape(T, D))
    loss = lax.psum(jnp.sum(loss_cols), ("x", "y", "c"))

    gsmall = {n: [None] * w[n].shape[0] for n in SMALL}
    big_index = {n: o for o, n in enumerate(BIG)}
    rs_shapes = [None] * (len(BIG) + 1)
    rs_bufs = [None] * (len(BIG) + 1)
    rs_dests_all = []
    pending = None

    def reduce_layer(tag, named_parts):
        parts, dests = [], []
        for n, l, pt in named_parts:
            o = big_index[n] if n in big_index else len(BIG)
            half = pt.shape[2:]
            rs_shapes[o] = (N_CHIPS if l == 'chip' else w[n].shape[0], 2, math.prod(half[:-1]), half[-1])
            parts.append(pt)
            dests.append((o, l))
        rs_dests_all.extend(dests)
        state, token = _reduce_begin(tag, parts)
        return (tag, state, dests), token

    token = loss.reshape(1, 1)
    for i in reversed(range(4)):
        kind, j = i % 3, i // 3
        sv_ = saved[i]
        nw = norm_w[i].reshape(1, D)
        w_out = W[('out_proj', i)]
        after_start = [token] if token is not None else ()
        layer_parts = [('out_proj', i, _mm_dw_rows(f"d_out_proj_{i}", sv_['a'], dh16, deps=after_start))]
        if kind == 0:
            w_glu = W[('s5_w_glu', j)]
            proj, y1, lin, k5 = sv_['proj'], sv_['y1'], sv_['lin'], sv_['k5']

            def da_epi(da, y1t, lint, z):
                gt, sg = _gelu(y1t), _sigmoid(lint)
                dy2 = da * _silu(z)
                dlin = (dy2 * gt) * (sg * (1.0 - sg))
                return da * (gt * sg) * _dsilu(z), dlin, dy2 * sg, _colsum(dlin)

            nm = T // _t(512, T)
            dz, dlin, dgd, dbg = _mm_rowsharded_t(
                f"d_s5_act_{i}", dh16, w_out, epi=da_epi, deps=after_start,
                extras=lambda tm, tn: [(y1, _tile(tm, tn)), (lin, _tile(tm, tn)), (proj, _tile(tm, tn, E // tn))],
                outs_fn=lambda tm, tn: [((T, E), BF16, _tile(tm, tn)), ((T, E), BF16, _tile(tm, tn)), ((T, E), F32, _tile(tm, tn)),
                                        ((nm, 1, E), F32, _bs((None, 1, tn), lambda g, m, n, k: (m, 0, n)))])
            gsmall['s5_b_glu'][j] = jnp.sum(dbg, axis=(0, 1))
            layer_parts.append(('s5_w_glu', j, _mm_dw_rows(f"d_s5_w_glu_{i}", sv_['g'], dlin)))
            glu_deps = ()
            if i == 0:
                early, early_token = reduce_layer("l0a", layer_parts)
                layer_parts, glu_deps = [], [early_token]
            dy1 = _mm_rowsharded_t(
                f"d_s5_glu_{i}", dlin, w_glu, epi=lambda acc, d, y1t: ((acc + d) * _dgelu(y1t),), deps=glu_deps,
                extras=lambda tm, tn: [(dgd, _tile(tm, tn)), (y1, _tile(tm, tn))],
                outs_fn=lambda tm, tn: [((T, E), F32, _tile(tm, tn))])[0]
            du, dbd, dcd, dab, ddk = _s5_bwd(f"d_s5_scan_{i}", dy1, proj, sv_['hs'], k5['bbd'], k5['cbd'], k5['ar3'], k5['ai3'], sv_['dsk'], E)
            gsmall['s5_d'][j] = ddk.reshape(E)
            gsmall['s5_c_re'][j] = jnp.transpose(_uncompact(dcd[:, :, :L], G), (0, 2, 1))
            gsmall['s5_c_im'][j] = -jnp.transpose(_uncompact(dcd[:, :, L:], G), (0, 2, 1))
            dbbr = _uncompact(dbd[:, :, :L], G).reshape(G * P, C)
            dbbi = _uncompact(dbd[:, :, L:], G).reshape(G * P, C)
            dbr, dbi, dfr, dfi = _s5_bbar_bwd(f"d_s5_bbar_{i}", k5['fr'].reshape(G * P, 1), k5['fi'].reshape(G * P, 1), k5['br'], k5['bi'], dbbr, dbbi)
            gsmall['s5_b_re'][j] = dbr.reshape(G, P, C)
            gsmall['s5_b_im'][j] = dbi.reshape(G, P, C)
            dab = jnp.sum(dab, axis=1)
            dare, daim, dldt = _s5_disc_bwd(f"d_s5_disc_{i}", w['s5_a_re'][j], w['s5_a_im'][j], w['s5_log_dt'][j].reshape(G, 1),
                                            (dab[:, :L].reshape(G, P), dab[:, L:].reshape(G, P), dfr.reshape(G, P), dfi.reshape(G, P)))
            gsmall['s5_a_re'][j], gsmall['s5_a_im'][j], gsmall['s5_log_dt'][j] = dare, daim, dldt.reshape(G)
            dproj = jnp.concatenate([du, dz], axis=1)
            layer_parts.append(('s5_in_proj', j, _mm_dw_cols(f"d_s5_in_proj_{i}", sv_['xn'], dproj)))
            dxn = _mm_colsharded_t(f"d_s5_xn_{i}", dproj, W[('s5_in_proj', j)])
        elif kind == 1:
            proj, y = sv_['proj'], sv_['y']
            do, dz = _mm_rowsharded_t(
                f"d_fox_act_{i}", dh16, w_out, epi=lambda da, yt, z: (da * _silu(z), (da * yt) * _dsilu(z)), deps=after_start,
                extras=lambda tm, tn: [(y, _tile(tm, tn)), (proj, _tile(tm, tn, 3 * E // tn))],
                outs_fn=lambda tm, tn: [((T, E), F32, _tile(tm, tn)), ((T, E), BF16, _tile(tm, tn))])
            dqn, dkn, dv, dcq, dck = _attn_bwd(f"d_fox_attn_{i}", sv_['qn'], sv_['kn'], proj, do, y, sv_['lse'], sv_['cum_q'], sv_['cum_k'], H)
            dq, dk, dwq, dwk = _qk_norm_bwd(f"d_fox_qk_norm_{i}", proj, sv_['wq'], sv_['wk'], dqn, dkn, H)
            gsmall['fox_q_norm'][j], gsmall['fox_k_norm'][j] = dwq.reshape(-1), dwk.reshape(-1)
            dcum = dcq + jnp.pad(jnp.transpose(dck.reshape(H, T)), ((0, 0), (0, LANES - H)))
            dls = _cum_rows(f"d_fox_cum_{i}", dcum, jnp.zeros((1, LANES), F32), True, False)
            dflog, dfb = _rows(f"d_fox_gate_{i}", lambda d, f, b: ((lambda r: (r, _colsum(r)))(d * _sigmoid(-(f + b)))),
                               [(dls, 'r', LANES, 0), (sv_['flog'], 'r', LANES, 0), (sv_['fb'], 'b', LANES, 0)],
                               [('r', LANES, BF16), ('a', LANES, F32)], 256)
            gsmall['fox_f_bias'][j] = dfb[0, :H]
            dproj = jnp.concatenate([dq, dk, dv, dz], axis=1)
            tkT = _t(K_STEP, T)
            dw_qkvz = _mm(f"d_fox_in_proj_{i}", sv_['xn'], dproj, M=D, N=4 * E, K=T, tm=_t(512, D), tn=_t(1024, 4 * E), tk=tkT, ta=True,
                          a_spec=_bs((tkT, _t(512, D)), lambda g, m, n, k: (k, m)),
                          b_spec=_bs((tkT, _t(1024, 4 * E)), lambda g, m, n, k: (k, n)),
                          outs=[((D, 4 * E), BF16, _tile(_t(512, D), _t(1024, 4 * E)))])[0]
            dw_f = _mm(f"d_fox_gate_proj_{i}", sv_['xn'], dflog, M=D, N=LANES, K=T, tm=_t(512, D), tn=LANES, tk=tkT, ta=True,
                       a_spec=_bs((tkT, _t(512, D)), lambda g, m, n, k: (k, m)),
                       b_spec=_bs((tkT, LANES), lambda g, m, n, k: (k, n)),
                       outs=[((D, LANES), BF16, _tile(_t(512, D), LANES))])[0]
            dw_fox = jnp.concatenate([dw_qkvz, dw_f[:, :H]], axis=1)
            sw = dw_fox.shape[1] // N_CHIPS
            layer_parts.append(('fox_in_proj', j, jnp.transpose(dw_fox.reshape(2, D // 2, N_CHIPS, sw), (0, 2, 1, 3))))
            w_qkvz, w_f = sv_['w_qkvz'], sv_['w_f']
            dxn_f = _mm(f"d_fox_xn_gate_{i}", dflog, w_f, M=T, N=D, K=LANES, tm=_t(512, T), tn=_t(1024, D), tk=LANES, tb=True,
                        a_spec=_bs((_t(512, T), LANES), lambda g, m, n, k: (m, k)),
                        b_spec=_bs((_t(1024, D), LANES), lambda g, m, n, k: (n, k)),
                        outs=[((T, D), F32, _tile(_t(512, T), _t(1024, D)))])[0]
            tm, tn, tk = _t(512, T), _t(1024, D), _t(K_STEP, 4 * E)
            dxn = _mm(f"d_fox_xn_{i}", dproj, w_qkvz, M=T, N=D, K=4 * E, tm=tm, tn=tn, tk=tk, tb=True,
                      a_spec=_bs((tm, tk), lambda g, m, n, k: (m, k)), b_spec=_bs((tn, tk), lambda g, m, n, k: (n, k)),
                      extras=[(dxn_f, _tile(tm, tn))], epi=lambda acc, e: (acc + e,),
                      outs=[((T, D), F32, _tile(tm, tn))])[0]
        else:
            proj, mixed, scale = sv_['proj'], sv_['mixed'], sv_['scale']
            nm = T // _t(512, T)

            def pool_epi(da, mx, sc, z):
                dy = da * _silu(z)
                return (da * (mx * sc)) * _dsilu(z), dy * sc, _colsum(dy * mx)

            dz, dmix, dsc = _mm_rowsharded_t(
                f"d_pool_act_{i}", dh16, w_out, epi=pool_epi, deps=after_start,
                extras=lambda tm, tn: [(mixed, _tile(tm, tn)), (scale, _rowvec(tn)), (proj, _tile(tm, tn, E // tn))],
                outs_fn=lambda tm, tn: [((T, E), BF16, _tile(tm, tn)), ((T, E), BF16, _tile(tm, tn)),
                                        ((nm, 1, E), F32, _bs((None, 1, tn), lambda g, m, n, k: (m, 0, n)))])
            gsmall['pool_scale'][j] = jnp.sum(dsc, axis=(0, 1))
            w_pg = sv_['w_pg']
            tkw = PD // N_CHIPS
            tk = _t(K_STEP, T)
            layer_parts.append(('pool_w_group', j, _mm(
                f"d_pool_w_group_{i}", sv_['pm'], dmix, M=PD, N=PD, K=T, tm=tkw, tn=PD, tk=tk, groups=PG, ta=True,
                a_spec=_bs((tk, tkw), lambda g, m, n, k: (k, g * (PD // tkw) + m)),
                b_spec=_bs((tk, PD), lambda g, m, n, k: (k, g)),
                outs=[((2, N_CHIPS, PG // 2, tkw, PD), BF16, _bs((None, None, None, tkw, PD), lambda g, m, n, k: (g // (PG // 2), m, g % (PG // 2), 0, 0)))])[0]))
            tm, tn2, tk2 = _t(512, T), _t(512, PD), _t(K_STEP, PD)
            dpm = _mm(f"d_pool_mix_{i}", dmix, w_pg, M=T, N=PD, K=PD, tm=tm, tn=tn2, tk=tk2, groups=PG, tb=True,
                      a_spec=_bs((tm, tk2), lambda g, m, n, k: (m, g * (PD // tk2) + k)),
                      b_spec=_bs((None, tn2, tk2), lambda g, m, n, k: (g, n, k)),
                      outs=[((T, E), F32, _bs((tm, tn2), lambda g, m, n, k: (m, g * (PD // tn2) + n)))])[0]
            du = _pool_bwd(f"d_pool_win_{i}", dpm, E)
            dproj = jnp.concatenate([du, dz], axis=1)
            layer_parts.append(('pool_in_proj', j, _mm_dw_cols(f"d_pool_in_proj_{i}", sv_['xn'], dproj)))
            dxn = _mm_colsharded_t(f"d_pool_xn_{i}", dproj, W[('pool_in_proj', j)])
        dh, dh16, dnw = _norm_bwd(f"d_norm_{i}", dxn, sv_['h'], nw, dh)
        gsmall['norm_w'][i] = dnw.reshape(D)
        if pending is not None:
            _reduce_end(pending[0], pending[1], dh16, pending[2], rs_bufs, rs_shapes)
        if i > 0:
            pending, token = reduce_layer(f"l{i}", layer_parts)
    grad_x = dh.reshape(x.shape)

    small_flat = jnp.concatenate([jnp.stack(gsmall[n]).reshape(-1) for n in SMALL])
    n_small = small_flat.shape[0]
    unit = 2 * N_CHIPS * 16 * LANES
    n_pad = -(-n_small // unit) * unit
    R = n_pad // (2 * N_CHIPS * LANES)
    small_part = jnp.pad(small_flat, (0, n_pad - n_small)).astype(BF16).reshape(2, N_CHIPS, R, LANES)
    pending, token = reduce_layer("l0", layer_parts + [('small', 'chip', small_part)])
    _reduce_end(early[0], early[1], token, early[2], rs_bufs, rs_shapes)
    nb = len(BIG)
    done_items = [d for d in rs_dests_all if d not in pending[2]]
    rs_bufs[:nb] = _pair_share("rs_pair_share_a", rs_bufs[:nb], done_items, deps=[token])
    late = [o for o, _ in pending[2]]
    delta, new_m, new_v = {}, {}, {}
    grads = {}
    last = token[:1, :1]
    for o, n in enumerate(BIG):
        if o not in late:
            grads[n] = rs_bufs[o].reshape(w[n].shape)
            last = last + _adamw_big(n, w, grads, mom_m, mom_v, delta, new_m, new_v)[:1, :1]
    _reduce_end(pending[0], pending[1], last, pending[2], rs_bufs, rs_shapes)
    shared = _pair_share("rs_pair_share_b", [rs_bufs[o] for o in late], [(k, l) for k, (_, l) in enumerate(pending[2])])
    for k, o in enumerate(late):
        rs_bufs[o] = shared[k]
        if o < nb:
            grads[BIG[o]] = shared[k].reshape(w[BIG[o]].shape)
            _adamw_big(BIG[o], w, grads, mom_m, mom_v, delta, new_m, new_v)
    small_all = _chip_allgather("gather_small_grads", [rs_bufs[nb]])[0]
    small_all = jnp.transpose(small_all, (1, 0, 2, 3)).reshape(-1)[:n_small]
    off = 0
    p = 2 * lax.axis_index("x") + lax.axis_index("y")
    for n in SMALL:
        full_shape = (w[n].shape[0], E) if n in SMALL_SHARDED else w[n].shape
        size = math.prod(full_shape)
        gfull = small_all[off:off + size].reshape(full_shape)
        off += size
        if n in SMALL_SHARDED:
            gfull = lax.dynamic_slice_in_dim(gfull, p * (E // N_CHIPS), E // N_CHIPS, axis=1)
        grads[n] = gfull

    for n in SMALL:
        shape = w[n].shape
        if n in GROUP_AXIS_1:
            perm = (0,) + tuple(range(2, len(shape))) + (1,)
            inv = (0, len(shape) - 1) + tuple(range(1, len(shape) - 1))
            view = lambda a: jnp.transpose(a, perm).reshape(-1, shape[1])
            back = lambda a: jnp.transpose(a.reshape(tuple(shape[k] for k in perm)), inv)
        else:
            view = lambda a: a.reshape(-1, shape[-1])
            back = lambda a: a.reshape(shape)
        d_, m_, v_ = _adamw(f"adamw_{n}", view(w[n]), view(grads[n]), view(mom_m[n]), view(mom_v[n]))
        delta[n], new_m[n], new_v[n] = back(d_), back(m_), back(v_)
    return (loss, grad_x, *[grads[n] for n in ORDER], *[delta[n] for n in ORDER], *[new_m[n] for n in ORDER], *[new_v[n] for n in ORDER])
```

```python
import functools
import math

import jax
import jax.numpy as jnp
from jax import lax
from jax.experimental import pallas as pl
from jax.experimental.pallas import tpu as pltpu

F32 = jnp.float32
BF16 = jnp.bfloat16
MESH = pl.DeviceIdType.MESH

N_CHIPS = 4
VMEM_LIMIT = 56 * 1024 * 1024
LANES = 128
SUB = 8

EPS = 1e-6
S5_GROUP = 16
S5_STATE = 64
GROUPS_PER_CHUNK = 16
S5_TIME_BLOCK = 512
FOX_HEAD_DIM = 128
NORM_HEADS = 4
ATTN_SUB = 256
ATTN_HEADS = 2
POOL_WINDOWS = (2, 4, 8, 16)
POOL_HALO = 16
ADAM_LR, ADAM_B1, ADAM_B2, ADAM_EPS, ADAM_WD, ADAM_STEP = 0.001, 0.9, 0.999, 1e-08, 0.01, 10
NEG = -1e30
K_STEP = 2048


ANY = pl.BlockSpec(memory_space=pl.ANY)


def _t(pref, dim):
    if dim <= pref:
        return dim
    t = pref - pref % 16
    while t > 16 and dim % t:
        t -= 16
    assert dim % t == 0, (pref, dim)
    return t


def _params(sem):
    return pltpu.CompilerParams(dimension_semantics=sem, vmem_limit_bytes=VMEM_LIMIT)


def _sigmoid(x):
    return 1.0 / (1.0 + jnp.exp(-x))


def _silu(z):
    return z * _sigmoid(z)


def _dsilu(z):
    s = _sigmoid(z)
    return s * (1.0 + z * (1.0 - s))


_GELU_C = math.sqrt(2.0 / math.pi)


def _gelu(x):
    return 0.5 * x * (1.0 + jnp.tanh(_GELU_C * (x + 0.044715 * (x * x * x))))


def _dgelu(x):
    t = jnp.tanh(_GELU_C * (x + 0.044715 * (x * x * x)))
    return 0.5 * (1.0 + t) + 0.5 * x * (1.0 - t * t) * (_GELU_C * (1.0 + 3.0 * 0.044715 * x * x))


def _log_sigmoid(x):
    return jnp.minimum(x, 0.0) - jnp.log(1.0 + jnp.exp(-jnp.abs(x)))


def _rms(x):
    return lax.rsqrt(jnp.mean(x * x, axis=-1, keepdims=True) + EPS)


def _rms_bwd(x, w, dy):
    r = _rms(x)
    xhat = x * r
    dxh = dy * w
    dx = r * (dxh - xhat * jnp.mean(dxh * xhat, axis=-1, keepdims=True))
    return dx, dy * xhat


def _rows(name, fn, ins, outs, tr, pre=None, into=None, deps=()):
    rows = None
    for arr, kind, cols, cb in ins:
        if kind == 'r':
            rows = arr.shape[0]
        elif kind == 's' and rows is None:
            rows = arr.shape[1]
    tr = _t(tr, rows)
    n_in = len(ins)
    has_acc = any(o[0] == 'a' for o in outs)

    def spec(kind, cols, cb):
        if kind == 'r':
            return pl.BlockSpec((tr, cols), lambda r, *p: (r, cb))
        if kind == 'b':
            return pl.BlockSpec((1, cols), lambda r, *p: (0, cb))
        return pl.BlockSpec((None, tr, cols), lambda r, p: (p[cb], r, 0))

    in_specs = [spec(kind, cols, cb) for _, kind, cols, cb in ins]
    out_specs, out_shape = [], []
    for o in outs:
        if o[0] == 'r':
            out_specs.append(pl.BlockSpec((tr, o[1]), lambda r, *p: (r, 0)))
            out_shape.append(jax.ShapeDtypeStruct((rows, o[1]), o[2]))
        elif o[0] == 'a':
            out_specs.append(pl.BlockSpec((1, o[1]), lambda r, *p: (0, 0)))
            out_shape.append(jax.ShapeDtypeStruct((1, o[1]), o[2]))
        else:
            blk = tuple(tr if d == 'tr' else d for d in o[3])
            out_specs.append(pl.BlockSpec(blk, o[4]))
            out_shape.append(jax.ShapeDtypeStruct(o[1], o[2]))
    n_pre = 0 if pre is None else 1
    args = [a[0] for a in ins]
    aliases = {}
    if into is not None:
        in_specs.append(ANY)
        args.append(into)
        aliases = {n_pre + n_in: 0}
    in_specs += [ANY] * len(deps)
    args += list(deps)
    n_all = len(args)

    def body(*refs):
        refs = refs[n_pre:]
        res = fn(*[r[...] for r in refs[:n_in]])
        for spec_o, o, v in zip(outs, refs[n_all:], res):
            if spec_o[0] == 'a':
                @pl.when(pl.program_id(0) == 0)
                def _():
                    o[...] = jnp.zeros_like(o)
                o[...] += v.astype(o.dtype)
            else:
                o[...] = v.astype(o.dtype)

    grid_spec = pltpu.PrefetchScalarGridSpec(num_scalar_prefetch=n_pre, grid=(rows // tr,), in_specs=in_specs, out_specs=out_specs)
    if pre is not None:
        args = [pre] + args
    return pl.pallas_call(body, name=name, grid_spec=grid_spec, out_shape=out_shape, input_output_aliases=aliases,
                          compiler_params=_params(("arbitrary" if has_acc else "parallel",)))(*args)


def _colsum(v):
    return jnp.sum(v, axis=0, keepdims=True)


def _mm(name, a, b, *, M, N, K, tm, tn, tk, a_spec, b_spec, outs, epi=None, extras=(), groups=1, ta=False, tb=False, deps=()):
    nk = K // tk
    assert M % tm == 0 and N % tn == 0 and K % tk == 0, (name, M, N, K, tm, tn, tk)
    dims = (((0 if ta else 1,), (1 if tb else 0,)), ((), ()))
    n_ex = len(extras)

    def body(*refs):
        a_ref, b_ref = refs[0], refs[1]
        ex = refs[2:2 + n_ex]
        out_refs = refs[2 + n_ex + len(deps):2 + n_ex + len(deps) + len(outs)]

        def finish(r):
            res = (r,) if epi is None else epi(r, *[e[...] for e in ex])
            for o, v in zip(out_refs, res):
                o[...] = v.astype(o.dtype)

        part = lax.dot_general(a_ref[...].astype(BF16), b_ref[...].astype(BF16), dims, preferred_element_type=F32)
        if nk == 1:
            finish(part)
            return
        acc = refs[-1]
        k = pl.program_id(3)

        @pl.when(k == 0)
        def _():
            acc[...] = part

        @pl.when(k > 0)
        def _():
            acc[...] += part

        @pl.when(k == nk - 1)
        def _():
            finish(acc[...])

    return pl.pallas_call(
        body, name=name, grid=(groups, M // tm, N // tn, nk),
        in_specs=[a_spec, b_spec] + [s for _, s in extras] + [ANY] * len(deps),
        out_specs=[s for _, _, s in outs],
        out_shape=[jax.ShapeDtypeStruct(sh, dt) for sh, dt, _ in outs],
        scratch_shapes=[] if nk == 1 else [pltpu.VMEM((tm, tn), F32)],
        compiler_params=_params(("parallel", "parallel", "parallel", "arbitrary")),
    )(a, b, *[e for e, _ in extras], *deps)


def _bs(shape, f):
    return pl.BlockSpec(shape, f)


def _tile(tm, tn, coff=0):
    return _bs((tm, tn), lambda g, m, n, k: (m, n + coff))


def _rowvec(tn, coff=0):
    return _bs((1, tn), lambda g, m, n, k: (0, n + coff))


def _mm_proj(name, xn, w, *, epi=None, extras=(), out_dtype=F32):
    T, D = xn.shape
    sw = w.shape[2]
    N = N_CHIPS * sw
    tm, tn, tk = _t(512 if extras else 1024, T), _t(1024, sw), _t(K_STEP, D)
    nb = sw // tn
    return _mm(name, xn, w, M=T, N=N, K=D, tm=tm, tn=tn, tk=tk,
               a_spec=_bs((tm, tk), lambda g, m, n, k: (m, k)),
               b_spec=_bs((None, tk, tn), lambda g, m, n, k: (n // nb, k, n % nb)),
               outs=[((T, N), out_dtype, _tile(tm, tn))], epi=epi, extras=extras)[0]


def _mm_plain(name, a, b, *, out_dtype=F32, epi=None, extras=(), outs=None, tn_pref=1024):
    M, K = a.shape
    N = b.shape[1]
    tm, tn, tk = _t(512 if extras else 1024, M), _t(tn_pref, N), _t(K_STEP, K)
    if outs is None:
        outs = [((M, N), out_dtype, _tile(tm, tn))]
    return _mm(name, a, b, M=M, N=N, K=K, tm=tm, tn=tn, tk=tk,
               a_spec=_bs((tm, tk), lambda g, m, n, k: (m, k)),
               b_spec=_bs((tk, tn), lambda g, m, n, k: (k, n)),
               outs=outs, epi=epi, extras=extras)


def _mm_rowsharded(name, a, w, *, epi, extras, outs_fn, deps=()):
    T, E = a.shape
    N = w.shape[2]
    tm, tn, tk = _t(512, T), _t(1024, N), _t(K_STEP, E)
    return _mm(name, a, w.reshape(E, N), M=T, N=N, K=E, tm=tm, tn=tn, tk=tk, deps=deps,
               a_spec=_bs((tm, tk), lambda g, m, n, k: (m, k)),
               b_spec=_bs((tk, tn), lambda g, m, n, k: (k, n)),
               outs=outs_fn(tm, tn), epi=epi, extras=extras(tm, tn))


def _mm_rowsharded_t(name, d, w, *, epi, extras, outs_fn, deps=()):
    T, N = d.shape
    tn = w.shape[1]
    E = N_CHIPS * tn
    tm, tk = _t(512, T), _t(K_STEP, N)
    return _mm(name, d, w, M=T, N=E, K=N, tm=tm, tn=tn, tk=tk, tb=True, deps=deps,
               a_spec=_bs((tm, tk), lambda g, m, n, k: (m, k)),
               b_spec=_bs((None, tn, tk), lambda g, m, n, k: (n, 0, k)),
               outs=outs_fn(tm, tn), epi=epi, extras=extras(tm, tn))


def _mm_colsharded_t(name, d, w):
    T, N = d.shape
    D, sw = w.shape[1], w.shape[2]
    tm, tn, tk = _t(512, T), _t(1024, D), _t(1024, sw)
    kb = sw // tk
    return _mm(name, d, w, M=T, N=D, K=N, tm=tm, tn=tn, tk=tk, tb=True,
               a_spec=_bs((tm, tk), lambda g, m, n, k: (m, k)),
               b_spec=_bs((None, tn, tk), lambda g, m, n, k: (k // kb, n, k % kb)),
               outs=[((T, D), F32, _tile(tm, tn))])[0]


def _mm_dw_rows(name, a, d, deps=()):
    T, E = a.shape
    N = d.shape[1]
    tm, tn, tk = E // (2 * N_CHIPS), _t(2048, N), _t(K_STEP, T)
    return _mm(name, a, d, M=E, N=N, K=T, tm=tm, tn=tn, tk=tk, ta=True, deps=deps,
               a_spec=_bs((tk, tm), lambda g, m, n, k: (k, m)),
               b_spec=_bs((tk, tn), lambda g, m, n, k: (k, n)),
               outs=[((2, N_CHIPS, tm, N), BF16, _bs((None, None, tm, tn), lambda g, m, n, k: (m % 2, m // 2, 0, n)))])[0]


def _mm_dw_cols(name, xn, d):
    T, D = xn.shape
    N = d.shape[1]
    sw = N // N_CHIPS
    tm, tn, tk = _t(512, D // 2), _t(1024, sw), _t(K_STEP, T)
    mh, nb = (D // 2) // tm, sw // tn
    return _mm(name, xn, d, M=D, N=N, K=T, tm=tm, tn=tn, tk=tk, ta=True,
               a_spec=_bs((tk, tm), lambda g, m, n, k: (k, m)),
               b_spec=_bs((tk, tn), lambda g, m, n, k: (k, n)),
               outs=[((2, N_CHIPS, D // 2, sw), BF16,
                      _bs((None, None, tm, tn), lambda g, m, n, k: (m // mh, n // nb, m % mh, n % nb)))])[0]


def _norm_fwd(name, h, w, deps=()):
    D = h.shape[1]
    return _rows(name, lambda x, g: ((x * _rms(x)) * g,), [(h, 'r', D, 0), (w, 'b', D, 0)], [('r', D, BF16)], 256, deps=deps)[0]


def _norm_bwd(name, dxn, h, w, dh):
    D = h.shape[1]

    def fn(dy, x, g, up):
        dx, dwt = _rms_bwd(x, g, dy)
        r = up + dx
        return r, r, _colsum(dwt)

    return _rows(name, fn, [(dxn, 'r', D, 0), (h, 'r', D, 0), (w, 'b', D, 0), (dh, 'r', D, 0)],
                 [('r', D, F32), ('r', D, BF16), ('a', D, F32)], 256)


def _loss(h, target):
    D = h.shape[1]

    def fn(y, t):
        e = y - t
        d = e * (1.0 / D)
        return d, d, _colsum(e * e) * (0.5 / D)

    return _rows("loss", fn, [(h, 'r', D, 0), (target, 'r', D, 0)], [('r', D, F32), ('r', D, BF16), ('a', D, F32)], 256)


def _adamw(name, w, g, m, v):
    cols = w.shape[1]

    def fn(w, g, m, v):
        m = ADAM_B1 * m + (1.0 - ADAM_B1) * g
        v = ADAM_B2 * v + (1.0 - ADAM_B2) * (g * g)
        m_hat = m / (1.0 - ADAM_B1 ** ADAM_STEP)
        v_hat = v / (1.0 - ADAM_B2 ** ADAM_STEP)
        delta = -ADAM_LR * (m_hat / (jnp.sqrt(v_hat) + ADAM_EPS) + ADAM_WD * w)
        return delta, m, v

    rows = w.shape[0]
    if rows % SUB == 0 or rows <= 256:
        return _rows(name, fn, [(x, 'r', cols, 0) for x in (w, g, m, v)], [('r', cols, F32)] * 3, 256)
    tc = _t(256, cols)
    assert tc % LANES == 0, (rows, cols)

    def body(w_ref, g_ref, m_ref, v_ref, d_out, m_out, v_out):
        for o, r in zip((d_out, m_out, v_out), fn(w_ref[...], g_ref[...], m_ref[...], v_ref[...])):
            o[...] = r

    blk = pl.BlockSpec((rows, tc), lambda j: (0, j))
    return pl.pallas_call(body, name=name, grid=(cols // tc,), in_specs=[blk] * 4, out_specs=[blk] * 3,
                          out_shape=[jax.ShapeDtypeStruct((rows, cols), F32)] * 3, compiler_params=_params(("parallel",)))(w, g, m, v)


def _s5_disc(a_re, a_im, log_dt):
    dt = jnp.exp(log_dt)
    mag = jnp.exp(a_re * dt)
    abar_r = mag * jnp.cos(a_im * dt)
    abar_i = mag * jnp.sin(a_im * dt)
    den = a_re * a_re + a_im * a_im
    xr = abar_r - 1.0
    fr = (xr * a_re + abar_i * a_im) / den
    fi = (abar_i * a_re - xr * a_im) / den
    return abar_r, abar_i, fr, fi


def _s5_disc_fwd(name, a_re, a_im, log_dt):
    G, P = a_re.shape

    def body(ar, ai, ld, o0, o1, o2, o3):
        for o, v in zip((o0, o1, o2, o3), _s5_disc(ar[...], ai[...], ld[...])):
            o[...] = v

    return pl.pallas_call(body, name=name, out_shape=[jax.ShapeDtypeStruct((G, P), F32)] * 4)(a_re, a_im, log_dt)


def _s5_disc_bwd(name, a_re, a_im, log_dt, cts):
    G, P = a_re.shape

    def body(ar, ai, ld, c0, c1, c2, c3, d0, d1, d2):
        _, vjp = jax.vjp(_s5_disc, ar[...], ai[...], ld[...])
        g0, g1, g2 = vjp((c0[...], c1[...], c2[...], c3[...]))
        d0[...] = g0
        d1[...] = g1
        d2[...] = g2

    return pl.pallas_call(body, name=name, out_shape=[jax.ShapeDtypeStruct((G, P), F32)] * 2 + [jax.ShapeDtypeStruct((G, 1), F32)])(
        a_re, a_im, log_dt, *cts)


def _s5_bbar(name, fr, fi, br, bi):
    return _rows(name, lambda fr, fi, br, bi: (fr * br - fi * bi, fr * bi + fi * br),
                 [(fr, 'r', 1, 0), (fi, 'r', 1, 0), (br, 'r', S5_GROUP, 0), (bi, 'r', S5_GROUP, 0)],
                 [('r', S5_GROUP, F32)] * 2, 2048)


def _s5_bbar_bwd(name, fr, fi, br, bi, dr, di):
    def fn(fr, fi, br, bi, dr, di):
        return (fr * dr + fi * di, fr * di - fi * dr,
                jnp.sum(br * dr + bi * di, axis=1, keepdims=True), jnp.sum(br * di - bi * dr, axis=1, keepdims=True))

    return _rows(name, fn, [(fr, 'r', 1, 0), (fi, 'r', 1, 0)] + [(x, 'r', S5_GROUP, 0) for x in (br, bi, dr, di)],
                 [('r', S5_GROUP, F32)] * 2 + [('r', 1, F32)] * 2, 2048)


def _scan_mults(m_ref, ar, ai, reverse):
    L = ar.shape[1]
    row = lax.broadcasted_iota(jnp.int32, (SUB, L), 0)
    if reverse:
        row = (SUB - 1) - row
    ar = jnp.broadcast_to(ar, (SUB, L))
    ai = jnp.broadcast_to(ai, (SUB, L))
    a2r, a2i = ar * ar - ai * ai, 2.0 * ar * ai
    a4r, a4i = a2r * a2r - a2i * a2i, 2.0 * a2r * a2i
    zero = jnp.zeros((SUB, L), F32)
    for s, (pr, pi, d) in enumerate(((ar, ai, 1), (a2r, a2i, 2), (a4r, a4i, 4))):
        m_ref[2 * s] = jnp.where(row >= d, pr, zero)
        m_ref[2 * s + 1] = jnp.where(row >= d, pi, zero)
    pr, pi = ar, ai
    for bit, (qr, qi) in ((1, (ar, ai)), (2, (a2r, a2i)), (4, (a4r, a4i))):
        on = (row & bit) != 0
        nr, ni = pr * qr - pi * qi, pr * qi + pi * qr
        pr, pi = jnp.where(on, nr, pr), jnp.where(on, ni, pi)
    m_ref[6] = pr
    m_ref[7] = pi


def _scan8(xr, xi, m_ref, cr, ci, reverse):
    for s, d in enumerate((1, 2, 4)):
        sh = (SUB - d) if reverse else d
        sr, si = pltpu.roll(xr, sh, 0), pltpu.roll(xi, sh, 0)
        mr, mi = m_ref[2 * s], m_ref[2 * s + 1]
        xr, xi = xr + mr * sr - mi * si, xi + mr * si + mi * sr
    pr, pi = m_ref[6], m_ref[7]
    return xr + pr * cr - pi * ci, xi + pr * ci + pi * cr


def _blockdiag_fill(bd_ref, c_ref, C, L):
    P = S5_STATE
    bd_ref[...] = jnp.zeros_like(bd_ref)
    for g in range(L // P):
        for half in (0, L):
            bd_ref[g * C:(g + 1) * C, half + g * P:half + (g + 1) * P] = c_ref[:, half + g * P:half + (g + 1) * P]


def _blockdiag_take(out_ref, dense_ref, C, L):
    P = S5_STATE
    for g in range(L // P):
        for half in (0, L):
            out_ref[:, half + g * P:half + (g + 1) * P] = dense_ref[g * C:(g + 1) * C, half + g * P:half + (g + 1) * P]


def _s5_fwd(name, proj, bbd, cbd, abar_r, abar_i, dskip, E):
    T = proj.shape[0]
    NC, C, L2 = bbd.shape
    L = L2 // 2
    CH = GROUPS_PER_CHUNK * C
    tT = _t(S5_TIME_BLOCK, T)
    nt = (((1,), (1,)), ((), ()))

    def body(u_ref, bc_ref, cc_ref, ar_ref, ai_ref, d_ref, y_ref, g_ref, h_ref, bu, carry, mult, b_bd, c_bd):
        tb = pl.program_id(1)

        @pl.when(tb == 0)
        def _():
            carry[...] = jnp.zeros_like(carry)
            _blockdiag_fill(b_bd, bc_ref, C, L)
            _blockdiag_fill(c_bd, cc_ref, C, L)

        u = u_ref[...]
        bu[...] = jnp.dot(u.astype(BF16), b_bd[...], preferred_element_type=F32)
        _scan_mults(mult, ar_ref[...], ai_ref[...], False)

        def step(jb, c):
            cr, ci = c
            r0 = pl.multiple_of(jb * SUB, SUB)
            hr, hi = _scan8(bu[pl.ds(r0, SUB), 0:L], bu[pl.ds(r0, SUB), L:L2], mult, cr, ci, False)
            h_ref[pl.ds(r0, SUB), 0:L] = hr
            h_ref[pl.ds(r0, SUB), L:L2] = hi
            return (jnp.broadcast_to(hr[SUB - 1:SUB, :], (SUB, L)), jnp.broadcast_to(hi[SUB - 1:SUB, :], (SUB, L)))

        cr, ci = lax.fori_loop(0, tT // SUB, step, (carry[:, 0:L], carry[:, L:L2]))
        carry[:, 0:L] = cr
        carry[:, L:L2] = ci
        y1 = lax.dot_general(h_ref[...].astype(BF16), c_bd[...], nt, preferred_element_type=F32) + d_ref[...] * u
        y_ref[...] = y1
        g_ref[...] = _gelu(y1).astype(BF16)

    return pl.pallas_call(
        body, name=name, grid=(NC, T // tT),
        in_specs=[_bs((tT, CH), lambda c, t: (t, c)), _bs((None, C, L2), lambda c, t: (c, 0, 0)),
                  _bs((None, C, L2), lambda c, t: (c, 0, 0)), _bs((None, 1, L), lambda c, t: (c, 0, 0)),
                  _bs((None, 1, L), lambda c, t: (c, 0, 0)), _bs((1, CH), lambda c, t: (0, c))],
        out_specs=[_bs((tT, CH), lambda c, t: (t, c)), _bs((tT, CH), lambda c, t: (t, c)),
                   _bs((None, tT, L2), lambda c, t: (c, t, 0))],
        out_shape=[jax.ShapeDtypeStruct((T, E), F32), jax.ShapeDtypeStruct((T, E), BF16),
                   jax.ShapeDtypeStruct((NC, T, L2), F32)],
        scratch_shapes=[pltpu.VMEM((tT, L2), F32), pltpu.VMEM((SUB, L2), F32), pltpu.VMEM((8, SUB, L), F32),
                        pltpu.VMEM((CH, L2), BF16), pltpu.VMEM((CH, L2), BF16)],
        compiler_params=_params(("parallel", "arbitrary")),
    )(proj, bbd, cbd, abar_r, abar_i, dskip)


def _s5_bwd(name, dy1, proj, hs, bbd, cbd, abar_r, abar_i, dskip, E):
    T = proj.shape[0]
    NC, C, L2 = bbd.shape
    L = L2 // 2
    CH = GROUPS_PER_CHUNK * C
    tT = _t(S5_TIME_BLOCK, T)
    nT = T // tT
    tn = (((0,), (0,)), ((), ()))
    nt = (((1,), (1,)), ((), ()))

    def body(dy_ref, u_ref, h_ref, bc_ref, cc_ref, ar_ref, ai_ref, d_ref, du_ref, db_ref, dc_ref, da_ref, dd_ref,
             gb, carry, mult, b_bd, c_bd, db_acc, dc_acc):
        tb = pl.program_id(1)

        @pl.when(tb == 0)
        def _():
            carry[...] = jnp.zeros_like(carry)
            db_acc[...] = jnp.zeros_like(db_acc)
            dc_acc[...] = jnp.zeros_like(dc_acc)
            da_ref[...] = jnp.zeros_like(da_ref)
            dd_ref[...] = jnp.zeros_like(dd_ref)
            _blockdiag_fill(b_bd, bc_ref, C, L)
            _blockdiag_fill(c_bd, cc_ref, C, L)

        dy = dy_ref[...]
        u = u_ref[...]
        dy16 = dy.astype(BF16)
        dc_acc[...] += lax.dot_general(dy16, h_ref[...].astype(BF16), tn, preferred_element_type=F32)
        gb[...] = jnp.dot(dy16, c_bd[...], preferred_element_type=F32)
        _scan_mults(mult, ar_ref[...], -ai_ref[...], True)
        row = lax.broadcasted_iota(jnp.int32, (SUB, L), 0)
        nblk = tT // SUB

        def step(jj, c):
            cr, ci, sr, si = c
            r0 = pl.multiple_of((nblk - 1 - jj) * SUB, SUB)
            gr, gi = _scan8(gb[pl.ds(r0, SUB), 0:L], gb[pl.ds(r0, SUB), L:L2], mult, cr, ci, True)
            gb[pl.ds(r0, SUB), 0:L] = gr
            gb[pl.ds(r0, SUB), L:L2] = gi
            nr = jnp.where(row == SUB - 1, cr, pltpu.roll(gr, SUB - 1, 0))
            ni = jnp.where(row == SUB - 1, ci, pltpu.roll(gi, SUB - 1, 0))
            hr, hi = h_ref[pl.ds(r0, SUB), 0:L], h_ref[pl.ds(r0, SUB), L:L2]
            sr = sr + nr * hr + ni * hi
            si = si + ni * hr - nr * hi
            return (jnp.broadcast_to(gr[0:1, :], (SUB, L)), jnp.broadcast_to(gi[0:1, :], (SUB, L)), sr, si)

        z = jnp.zeros((SUB, L), F32)
        cr, ci, sr, si = lax.fori_loop(0, nblk, step, (carry[:, 0:L], carry[:, L:L2], z, z))
        carry[:, 0:L] = cr
        carry[:, L:L2] = ci
        da_ref[:, 0:L] += sr
        da_ref[:, L:L2] += si
        g16 = gb[...].astype(BF16)
        du = lax.dot_general(g16, b_bd[...], nt, preferred_element_type=F32) + d_ref[...] * dy
        du_ref[...] = du.astype(BF16)
        db_acc[...] += lax.dot_general(u.astype(BF16), g16, tn, preferred_element_type=F32)
        dd_ref[...] += _colsum(dy * u)

        @pl.when(tb == nT - 1)
        def _():
            _blockdiag_take(db_ref, db_acc, C, L)
            _blockdiag_take(dc_ref, dc_acc, C, L)

    rev = lambda c, t: (nT - 1 - t, c)
    return pl.pallas_call(
        body, name=name, grid=(NC, nT),
        in_specs=[_bs((tT, CH), rev), _bs((tT, CH), rev), _bs((None, tT, L2), lambda c, t: (c, nT - 1 - t, 0)),
                  _bs((None, C, L2), lambda c, t: (c, 0, 0)), _bs((None, C, L2), lambda c, t: (c, 0, 0)),
                  _bs((None, 1, L), lambda c, t: (c, 0, 0)), _bs((None, 1, L), lambda c, t: (c, 0, 0)),
                  _bs((1, CH), lambda c, t: (0, c))],
        out_specs=[_bs((tT, CH), rev), _bs((None, C, L2), lambda c, t: (c, 0, 0)), _bs((None, C, L2), lambda c, t: (c, 0, 0)),
                   _bs((None, SUB, L2), lambda c, t: (c, 0, 0)), _bs((None, 1, CH), lambda c, t: (c, 0, 0))],
        out_shape=[jax.ShapeDtypeStruct((T, E), BF16), jax.ShapeDtypeStruct((NC, C, L2), F32),
                   jax.ShapeDtypeStruct((NC, C, L2), F32), jax.ShapeDtypeStruct((NC, SUB, L2), F32),
                   jax.ShapeDtypeStruct((NC, 1, CH), F32)],
        scratch_shapes=[pltpu.VMEM((tT, L2), F32), pltpu.VMEM((SUB, L2), F32), pltpu.VMEM((8, SUB, L), F32),
                        pltpu.VMEM((CH, L2), BF16), pltpu.VMEM((CH, L2), BF16), pltpu.VMEM((CH, L2), F32), pltpu.VMEM((CH, L2), F32)],
        compiler_params=_params(("parallel", "arbitrary")),
    )(dy1, proj, hs, bbd, cbd, abar_r, abar_i, dskip)


def _compact(v, NC):
    G, P, C = v.shape
    return jnp.transpose(v.reshape(NC, G // NC, P, C), (0, 3, 1, 2)).reshape(NC, C, (G // NC) * P)


def _uncompact(d, G):
    NC, C, L = d.shape
    gpc = G // NC
    return jnp.transpose(d.reshape(NC, C, gpc, L // gpc), (0, 2, 3, 1)).reshape(G, L // gpc, C)


def _cum_rows(name, x, bias, reverse, log_sig):
    T, L = x.shape

    def body(x_ref, b_ref, o_ref):
        row = lax.broadcasted_iota(jnp.int32, (SUB, L), 0)
        if reverse:
            row = (SUB - 1) - row
        nblk = T // SUB

        def step(jj, c):
            r0 = pl.multiple_of(((nblk - 1 - jj) if reverse else jj) * SUB, SUB)
            v = x_ref[pl.ds(r0, SUB), :] + b_ref[...]
            if log_sig:
                v = _log_sigmoid(v)
            for d in (1, 2, 4):
                v = v + jnp.where(row >= d, pltpu.roll(v, (SUB - d) if reverse else d, 0), 0.0)
            v = v + c
            o_ref[pl.ds(r0, SUB), :] = v
            e = 0 if reverse else SUB - 1
            return jnp.broadcast_to(v[e:e + 1, :], (SUB, L))

        lax.fori_loop(0, nblk, step, jnp.zeros((SUB, L), F32))

    return pl.pallas_call(body, name=name, out_shape=jax.ShapeDtypeStruct((T, L), F32),
                          compiler_params=pltpu.CompilerParams(vmem_limit_bytes=VMEM_LIMIT))(x, bias)


def _qk_norm(name, proj, wq, wk, H):
    T = proj.shape[0]
    Dh = FOX_HEAD_DIM
    tT = _t(512, T)
    HB = math.gcd(NORM_HEADS, H)

    def body(q_ref, k_ref, wq_ref, wk_ref, qn_ref, kn_ref):
        for hh in range(HB):
            lanes = slice(hh * Dh, (hh + 1) * Dh)
            q, k = q_ref[:, lanes], k_ref[:, lanes]
            qn_ref[:, lanes] = ((q * _rms(q)) * wq_ref[...]).astype(BF16)
            kn_ref[:, lanes] = ((k * _rms(k)) * wk_ref[...]).astype(BF16)

    blk = lambda off: _bs((tT, HB * Dh), lambda t, h: (t, h + off))
    return pl.pallas_call(
        body, name=name, grid=(T // tT, H // HB),
        in_specs=[blk(0), blk(H // HB), _bs((1, Dh), lambda t, h: (0, 0)), _bs((1, Dh), lambda t, h: (0, 0))],
        out_specs=[blk(0), blk(0)], out_shape=[jax.ShapeDtypeStruct((T, H * Dh), BF16)] * 2,
        compiler_params=_params(("parallel", "parallel")))(proj, proj, wq, wk)


def _qk_norm_bwd(name, proj, wq, wk, dqn, dkn, H):
    T = proj.shape[0]
    Dh = FOX_HEAD_DIM
    tT = _t(512, T)
    HB = math.gcd(NORM_HEADS, H)

    def body(q_ref, k_ref, wq_ref, wk_ref, dqn_ref, dkn_ref, dq_ref, dk_ref, dwq_ref, dwk_ref):
        @pl.when((pl.program_id(0) == 0) & (pl.program_id(1) == 0))
        def _():
            dwq_ref[...] = jnp.zeros_like(dwq_ref)
            dwk_ref[...] = jnp.zeros_like(dwk_ref)

        for hh in range(HB):
            lanes = slice(hh * Dh, (hh + 1) * Dh)
            dq, tq = _rms_bwd(q_ref[:, lanes], wq_ref[...], dqn_ref[:, lanes])
            dk, tk = _rms_bwd(k_ref[:, lanes], wk_ref[...], dkn_ref[:, lanes])
            dq_ref[:, lanes] = dq.astype(BF16)
            dk_ref[:, lanes] = dk.astype(BF16)
            dwq_ref[...] += _colsum(tq)
            dwk_ref[...] += _colsum(tk)

    blk = lambda off: _bs((tT, HB * Dh), lambda t, h: (t, h + off))
    one = _bs((1, Dh), lambda t, h: (0, 0))
    return pl.pallas_call(
        body, name=name, grid=(T // tT, H // HB),
        in_specs=[blk(0), blk(H // HB), one, one, blk(0), blk(0)],
        out_specs=[blk(0), blk(0), one, one],
        out_shape=[jax.ShapeDtypeStruct((T, H * Dh), BF16)] * 2 + [jax.ShapeDtypeStruct((1, Dh), F32)] * 2,
        compiler_params=_params(("arbitrary", "arbitrary")))(proj, proj, wq, wk, dqn, dkn)


def _attn_fwd(name, qn, kn, proj, cum_q, cum_k, H):
    T = qn.shape[0]
    Dh = FOX_HEAD_DIM
    tq = cum_k.shape[3]
    nq = T // tq
    scale = Dh ** -0.5
    nt = (((1,), (1,)), ((), ()))

    sq = _t(ATTN_SUB, tq)
    rep = tq // LANES
    HP = ATTN_HEADS
    assert H % HP == 0 and Dh == LANES

    def body(q_ref, k_ref, v_ref, cq_ref, ck_ref, o_ref, lse_ref, m_sc, l_sc, acc_sc):
        i = pl.program_id(1)
        m_sc[...] = jnp.full_like(m_sc, NEG)
        l_sc[...] = jnp.zeros_like(l_sc)
        acc_sc[...] = jnp.zeros_like(acc_sc)
        kloc = lax.broadcasted_iota(jnp.int32, (sq, tq), 1)
        qloc = lax.broadcasted_iota(jnp.int32, (sq, tq), 0)

        def chunk(kc, masked):
            ks = pl.multiple_of(kc * tq, tq)
            for hh in range(HP):
                lanes = slice(hh * Dh, (hh + 1) * Dh)
                k = k_ref[pl.ds(ks, tq), lanes]
                v16 = v_ref[pl.ds(ks, tq), lanes].astype(BF16)
                ck = ck_ref[hh, kc]
                for r in range(tq // sq):
                    rows = pl.ds(r * sq, sq)
                    s = lax.dot_general(q_ref[rows, lanes], k, nt, preferred_element_type=F32) * scale + (jnp.tile(cq_ref[hh, rows, :], (1, rep)) - ck)
                    if masked:
                        s = jnp.where(kloc <= qloc + r * sq, s, NEG)
                    m_old = m_sc[rows, lanes]
                    m_new = jnp.maximum(m_old, jnp.max(s, axis=1, keepdims=True))
                    alpha = jnp.exp(m_old - m_new)
                    p = jnp.exp(s - jnp.tile(m_new, (1, rep)))
                    l_sc[rows, lanes] = alpha * l_sc[rows, lanes] + jnp.sum(p, axis=1, keepdims=True)
                    acc_sc[rows, lanes] = alpha * acc_sc[rows, lanes] + jnp.dot(p.astype(BF16), v16, preferred_element_type=F32)
                    m_sc[rows, lanes] = m_new

        def below(kc, c):
            chunk(kc, False)
            return c

        lax.fori_loop(0, i, below, 0)
        chunk(i, True)
        o_ref[...] = acc_sc[...] / l_sc[...]
        for hh in range(HP):
            lanes = slice(hh * Dh, (hh + 1) * Dh)
            lse_ref[hh] = m_sc[:, lanes] + jnp.log(l_sc[:, lanes])

    W2 = HP * Dh
    return pl.pallas_call(
        body, name=name, grid=(H // HP, nq),
        in_specs=[_bs((tq, W2), lambda h, i: (i, h)), _bs((T, W2), lambda h, i: (0, h)), _bs((T, W2), lambda h, i: (0, 2 * (H // HP) + h)),
                  _bs((HP, tq, LANES), lambda h, i: (h, i, 0)), _bs((HP, nq, 1, tq), lambda h, i: (h, 0, 0, 0))],
        out_specs=[_bs((tq, W2), lambda h, i: (i, h)), _bs((HP, tq, LANES), lambda h, i: (h, i, 0))],
        out_shape=[jax.ShapeDtypeStruct((T, H * Dh), F32), jax.ShapeDtypeStruct((H, T, LANES), F32)],
        scratch_shapes=[pltpu.VMEM((tq, W2), F32), pltpu.VMEM((tq, W2), F32), pltpu.VMEM((tq, W2), F32)],
        compiler_params=_params(("parallel", "parallel")))(qn, kn, proj, cum_q, cum_k)


def _attn_bwd(name, qn, kn, proj, do, o, lse, cum_q, cum_k, H):
    T = qn.shape[0]
    Dh = FOX_HEAD_DIM
    tq = cum_k.shape[3]
    nq = T // tq
    scale = Dh ** -0.5
    nt = (((1,), (1,)), ((), ()))
    tn = (((0,), (0,)), ((), ()))
    assert H <= LANES

    sq = _t(ATTN_SUB, tq)
    rep = tq // LANES
    HP = ATTN_HEADS
    W2 = HP * Dh
    assert H % HP == 0 and Dh == LANES

    def body(q_ref, k_ref, v_ref, do_ref, o_ref, lse_ref, cq_ref, ck_ref, dq_ref, dk_ref, dv_ref, dcq_ref, dck_ref,
             delta, cql, dk_sc, dv_sc, dck_sc):
        h, j = pl.program_id(0), pl.program_id(1)

        @pl.when((h == 0) & (j == 0))
        def _():
            dcq_ref[...] = jnp.zeros_like(dcq_ref)

        @pl.when(j == 0)
        def _():
            dq_ref[...] = jnp.zeros_like(dq_ref)
            for hh in range(HP):
                lanes = slice(hh * Dh, (hh + 1) * Dh)
                delta[hh] = jnp.broadcast_to(jnp.sum(do_ref[:, lanes] * o_ref[:, lanes], axis=1, keepdims=True), (T, LANES))
            cql[...] = cq_ref[...] - lse_ref[...]

        lane_id = lax.broadcasted_iota(jnp.int32, (sq, LANES), 1)
        dk_sc[...] = jnp.zeros_like(dk_sc)
        dv_sc[...] = jnp.zeros_like(dv_sc)
        dck_sc[...] = jnp.zeros_like(dck_sc)
        kloc = lax.broadcasted_iota(jnp.int32, (sq, tq), 1)
        qloc = lax.broadcasted_iota(jnp.int32, (sq, tq), 0)

        def qblk(i, masked):
            for hh in range(HP):
                lanes = slice(hh * Dh, (hh + 1) * Dh)
                k = k_ref[:, lanes]
                v16 = v_ref[:, lanes].astype(BF16)
                ck = ck_ref[hh]
                for r in range(tq // sq):
                    rows = pl.ds(pl.multiple_of(i * tq + r * sq, sq), sq)
                    q = q_ref[rows, lanes]
                    do16 = do_ref[rows, lanes].astype(BF16)
                    e = lax.dot_general(q, k, nt, preferred_element_type=F32) * scale + (jnp.tile(cql[hh, rows, :], (1, rep)) - ck)
                    p = jnp.exp(e)
                    if masked:
                        p = jnp.where(kloc <= qloc + r * sq, p, 0.0)
                    dv_sc[:, lanes] += lax.dot_general(p.astype(BF16), do16, tn, preferred_element_type=F32)
                    dp = lax.dot_general(do16, v16, nt, preferred_element_type=F32)
                    ds = p * (dp - jnp.tile(delta[hh, rows, :], (1, rep)))
                    ds16 = ds.astype(BF16)
                    dk_sc[:, lanes] += lax.dot_general(ds16, q, tn, preferred_element_type=F32)
                    dq_ref[rows, lanes] += jnp.dot(ds16, k, preferred_element_type=F32) * scale
                    dcq_ref[rows, :] += jnp.where(lane_id == h * HP + hh, jnp.sum(ds, axis=1, keepdims=True), 0.0)
                    dck_sc[hh] += jnp.sum(ds, axis=0, keepdims=True)

        def above(i, c):
            qblk(i, False)
            return c

        qblk(j, True)
        lax.fori_loop(j + 1, nq, above, 0)
        dk_ref[...] = dk_sc[...] * scale
        dv_ref[...] = dv_sc[...].astype(BF16)
        for hh in range(HP):
            dck_ref[hh] = -dck_sc[hh]

    whole = lambda off: _bs((T, W2), lambda h, j: (0, h + off))
    blk = lambda off: _bs((tq, W2), lambda h, j: (j, h + off))
    return pl.pallas_call(
        body, name=name, grid=(H // HP, nq),
        in_specs=[whole(0), blk(0), blk(2 * (H // HP)), whole(0), whole(0), _bs((HP, T, LANES), lambda h, j: (h, 0, 0)),
                  _bs((HP, T, LANES), lambda h, j: (h, 0, 0)), _bs((HP, None, 1, tq), lambda h, j: (h, j, 0, 0))],
        out_specs=[whole(0), blk(0), blk(0), _bs((T, LANES), lambda h, j: (0, 0)),
                   _bs((HP, None, 1, tq), lambda h, j: (h, j, 0, 0))],
        out_shape=[jax.ShapeDtypeStruct((T, H * Dh), F32), jax.ShapeDtypeStruct((T, H * Dh), F32), jax.ShapeDtypeStruct((T, H * Dh), BF16),
                   jax.ShapeDtypeStruct((T, LANES), F32), jax.ShapeDtypeStruct((H, nq, 1, tq), F32)],
        scratch_shapes=[pltpu.VMEM((HP, T, LANES), F32), pltpu.VMEM((HP, T, LANES), F32), pltpu.VMEM((tq, W2), F32), pltpu.VMEM((tq, W2), F32),
                        pltpu.VMEM((HP, 1, tq), F32)],
        compiler_params=_params(("arbitrary", "arbitrary")))(qn, kn, proj, do, o, lse, cum_q, cum_k)


def _pool_fwd(name, proj, E):
    T = proj.shape[0]
    PG = len(POOL_WINDOWS)
    PD = E // PG
    tT = _t(256, T)
    hb = tT // POOL_HALO

    def body(u_ref, halo_ref, o_ref, buf):
        g, tb = pl.program_id(0), pl.program_id(1)
        u = u_ref[...]
        buf[pl.ds(POOL_HALO, tT), :] = u
        buf[pl.ds(0, POOL_HALO), :] = jnp.where(tb == 0, 0.0, halo_ref[...])
        t = tb * tT + lax.broadcasted_iota(jnp.int32, (tT, 1), 0)
        for gi, w in enumerate(POOL_WINDOWS):
            @pl.when(g == gi)
            def _():
                acc = u
                for d in range(1, w):
                    acc = acc + buf[pl.ds(POOL_HALO - d, tT), :]
                cnt = jnp.minimum(t + 1, w).astype(F32)
                o_ref[...] = (acc / cnt - u).astype(BF16)

    return pl.pallas_call(
        body, name=name, grid=(PG, T // tT),
        in_specs=[_bs((tT, PD), lambda g, t: (t, g)), _bs((POOL_HALO, PD), lambda g, t: (jnp.maximum(t * hb - 1, 0), g))],
        out_specs=_bs((tT, PD), lambda g, t: (t, g)), out_shape=jax.ShapeDtypeStruct((T, E), BF16),
        scratch_shapes=[pltpu.VMEM((tT + POOL_HALO, PD), F32)],
        compiler_params=_params(("parallel", "parallel")))(proj, proj)


def _pool_bwd(name, dpm, E):
    T = dpm.shape[0]
    PG = len(POOL_WINDOWS)
    PD = E // PG
    tT = _t(256, T)
    hb = tT // POOL_HALO
    nT = T // tT

    def body(d_ref, halo_ref, o_ref, buf):
        g, tb = pl.program_id(0), pl.program_id(1)
        d = d_ref[...]
        t = tb * tT + lax.broadcasted_iota(jnp.int32, (tT, 1), 0)
        th = (tb + 1) * tT + lax.broadcasted_iota(jnp.int32, (POOL_HALO, 1), 0)
        for gi, w in enumerate(POOL_WINDOWS):
            @pl.when(g == gi)
            def _():
                dn = d / jnp.minimum(t + 1, w).astype(F32)
                buf[pl.ds(0, tT), :] = dn
                buf[pl.ds(tT, POOL_HALO), :] = jnp.where(tb == nT - 1, 0.0, halo_ref[...] / jnp.minimum(th + 1, w).astype(F32))
                acc = dn
                for s in range(1, w):
                    acc = acc + buf[pl.ds(s, tT), :]
                o_ref[...] = (acc - d).astype(BF16)

    return pl.pallas_call(
        body, name=name, grid=(PG, nT),
        in_specs=[_bs((tT, PD), lambda g, t: (t, g)), _bs((POOL_HALO, PD), lambda g, t: (jnp.minimum((t + 1) * hb, T // POOL_HALO - 1), g))],
        out_specs=_bs((tT, PD), lambda g, t: (t, g)), out_shape=jax.ShapeDtypeStruct((T, E), BF16),
        scratch_shapes=[pltpu.VMEM((tT + POOL_HALO, PD), F32)],
        compiler_params=_params(("parallel", "parallel")))(dpm, dpm)


def _coords():
    x, y, c = lax.axis_index("x"), lax.axis_index("y"), lax.axis_index("c")
    chips = [(1 - x, y), (x, 1 - y), (1 - x, 1 - y)]
    return x, y, c, 2 * x + y, (x, y, 1 - c), chips


def _chip_allgather(name, bufs):
    n = len(bufs)

    def body(*refs):
        outs = refs[n:2 * n]
        send, recv, fsend, frecv = refs[2 * n:]
        x, y, c, p, sib, chips = _coords()

        def direct(t, j, chip):
            return pltpu.make_async_remote_copy(src_ref=outs[t].at[p, c], dst_ref=outs[t].at[p, c], send_sem=send.at[t, j],
                                                recv_sem=recv.at[t, j], device_id=(*chip, c), device_id_type=MESH)

        def landed(t, j, chip):
            blk = outs[t].at[2 * chip[0] + chip[1], c]
            return pltpu.make_async_remote_copy(src_ref=blk, dst_ref=blk, send_sem=send.at[t, j],
                                                recv_sem=recv.at[t, j], device_id=(*chip, c), device_id_type=MESH)

        def passed(t, j, chip, half):
            blk = outs[t].at[2 * chip[0] + chip[1], half]
            return pltpu.make_async_remote_copy(src_ref=blk, dst_ref=blk, send_sem=fsend.at[t, j], recv_sem=frecv.at[t, j],
                                                device_id=sib, device_id_type=MESH)

        first = [direct(t, j, chip) for t in range(n) for j, chip in enumerate(chips)]
        for cp in first:
            cp.start()
        fwd = []
        for j, chip in enumerate(chips):
            for t in range(n):
                landed(t, j, chip).wait_recv()
                f = passed(t, j, chip, c)
                f.start()
                fwd.append(f)
        for j, chip in enumerate(chips):
            for t in range(n):
                passed(t, j, chip, 1 - c).wait_recv()
        for cp in first + fwd:
            cp.wait_send()

    return pl.pallas_call(
        body, name=name, in_specs=[ANY] * n, out_specs=[ANY] * n,
        out_shape=[jax.ShapeDtypeStruct(a.shape, a.dtype) for a in bufs],
        input_output_aliases={t: t for t in range(n)},
        scratch_shapes=[pltpu.SemaphoreType.DMA((n, 3))] * 4,
    )(*bufs)


SEM = pl.BlockSpec(memory_space=pltpu.SEMAPHORE)
TOKEN = jax.ShapeDtypeStruct((SUB, LANES), F32)


def _split_params():
    return pltpu.CompilerParams(has_side_effects=pltpu.SideEffectType.DATAFLOW_SIDE_EFFECTING)


def _struct(a):
    return jax.ShapeDtypeStruct(a.shape, a.dtype)


def _gather_start(name, bufs, deps):
    n, nd = len(bufs), len(deps)

    def body(*refs):
        outs = refs[n + nd:2 * n + nd]
        send, recv, token = refs[2 * n + nd:]
        x, y, c, p, sib, chips = _coords()
        for t in range(n):
            for j, chip in enumerate(chips):
                pltpu.make_async_remote_copy(src_ref=outs[t].at[p, c], dst_ref=outs[t].at[p, c], send_sem=send.at[3 * t + j],
                                             recv_sem=recv.at[3 * t + j], device_id=(*chip, c), device_id_type=MESH).start()
        token[...] = jnp.zeros_like(token)

    res = pl.pallas_call(
        body, name=name, in_specs=[ANY] * (n + nd), out_specs=[ANY] * n + [SEM, SEM, pl.BlockSpec(memory_space=pltpu.VMEM)],
        out_shape=[_struct(a) for a in bufs] + [pltpu.SemaphoreType.DMA((3 * n,)), pltpu.SemaphoreType.DMA((3 * n,)), TOKEN],
        input_output_aliases={t: t for t in range(n)}, compiler_params=_split_params(),
    )(*bufs, *deps)
    return list(res[:n]), res[n], res[n + 1], res[n + 2]


def _gather_wait(name, bufs, send, recv, after):
    n = len(bufs)

    def body(*refs):
        send_r, recv_r = refs[n], refs[n + 1]
        outs = refs[n + 3:2 * n + 3]
        x, y, c, p, sib, chips = _coords()
        for t in range(n):
            for j, chip in enumerate(chips):
                cp = pltpu.make_async_remote_copy(src_ref=outs[t].at[p, c], dst_ref=outs[t].at[2 * chip[0] + chip[1], c], send_sem=send_r.at[3 * t + j],
                                                  recv_sem=recv_r.at[3 * t + j], device_id=(*chip, c), device_id_type=MESH)
                cp.wait_send()
                cp.wait_recv()

    return list(pl.pallas_call(
        body, name=name, in_specs=[ANY] * n + [SEM, SEM, ANY], out_specs=[ANY] * n, out_shape=[_struct(a) for a in bufs],
        input_output_aliases={t: t for t in range(n)}, compiler_params=_split_params(),
    )(*bufs, send, recv, after))


def _gather_forward(name, bufs):
    n = len(bufs)

    def body(*refs):
        outs = refs[n:2 * n]
        fsend, frecv = refs[2 * n:]
        x, y, c, p, sib, chips = _coords()

        def passed(t, j, chip, half):
            blk = outs[t].at[2 * chip[0] + chip[1], half]
            return pltpu.make_async_remote_copy(src_ref=blk, dst_ref=blk, send_sem=fsend.at[t, j], recv_sem=frecv.at[t, j],
                                                device_id=sib, device_id_type=MESH)

        fwd = [passed(t, j, chip, c) for t in range(n) for j, chip in enumerate(chips)]
        for cp in fwd:
            cp.start()
        for t in range(n):
            for j, chip in enumerate(chips):
                passed(t, j, chip, 1 - c).wait_recv()
        for cp in fwd:
            cp.wait_send()

    return list(pl.pallas_call(
        body, name=name, in_specs=[ANY] * n, out_specs=[ANY] * n, out_shape=[_struct(a) for a in bufs],
        input_output_aliases={t: t for t in range(n)}, scratch_shapes=[pltpu.SemaphoreType.DMA((n, 3))] * 2,
    )(*bufs))


def _relations():
    x, y, c = lax.axis_index("x"), lax.axis_index("y"), lax.axis_index("c")
    out = []
    for code in range(1, 8):
        tx = 1 - x if code & 4 else x
        ty = 1 - y if code & 2 else y
        tc = 1 - c if code & 1 else c
        out.append((code - 1, (tx, ty, tc), 2 * tx + ty, tc))
    return out


def _full_exchange_start(name, parts):
    n = len(parts)
    lands = [lax.empty((7,) + a.shape[2:], a.dtype) for a in parts]

    def body(*refs):
        src, dst = refs[2 * n:3 * n], refs[3 * n:4 * n]
        send, recv, token = refs[4 * n:]
        for t in range(n):
            for k, dev, q, half in _relations():
                pltpu.make_async_remote_copy(src_ref=src[t].at[half, q], dst_ref=dst[t].at[k], send_sem=send.at[7 * t + k],
                                             recv_sem=recv.at[7 * t + k], device_id=dev, device_id_type=MESH).start()
        token[...] = jnp.zeros_like(token)

    res = pl.pallas_call(
        body, name=name, in_specs=[ANY] * (2 * n), out_specs=[ANY] * (2 * n) + [SEM, SEM, pl.BlockSpec(memory_space=pltpu.VMEM)],
        out_shape=[_struct(a) for a in parts + lands] + [pltpu.SemaphoreType.DMA((7 * n,)), pltpu.SemaphoreType.DMA((7 * n,)), TOKEN],
        input_output_aliases={t: t for t in range(2 * n)}, compiler_params=_split_params(),
    )(*parts, *lands)
    return list(res[:n]), list(res[n:2 * n]), res[2 * n], res[2 * n + 1], res[2 * n + 2]


def _full_exchange_wait(name, parts, lands, send, recv, after):
    n = len(parts)

    def body(*refs):
        send_r, recv_r = refs[2 * n], refs[2 * n + 1]
        src, dst = refs[2 * n + 3:3 * n + 3], refs[3 * n + 3:4 * n + 3]
        for t in range(n):
            for k, dev, q, half in _relations():
                cp = pltpu.make_async_remote_copy(src_ref=src[t].at[half, q], dst_ref=dst[t].at[k], send_sem=send_r.at[7 * t + k],
                                                  recv_sem=recv_r.at[7 * t + k], device_id=dev, device_id_type=MESH)
                cp.wait_send()
                cp.wait_recv()

    res = pl.pallas_call(
        body, name=name, in_specs=[ANY] * (2 * n) + [SEM, SEM, ANY], out_specs=[ANY] * (2 * n),
        out_shape=[_struct(a) for a in parts + lands], input_output_aliases={t: t for t in range(2 * n)},
        compiler_params=_split_params(),
    )(*parts, *lands, send, recv, after)
    return list(res[:n]), list(res[n:])


def _chip_exchange_start(name, sums):
    n = len(sums)
    lands = [lax.empty((3,) + a.shape[1:], a.dtype) for a in sums]

    def body(*refs):
        src, dst = refs[2 * n:3 * n], refs[3 * n:4 * n]
        send, recv, token = refs[4 * n:]
        x, y, c, p, sib, chips = _coords()
        for t in range(n):
            for j, chip in enumerate(chips):
                pltpu.make_async_remote_copy(src_ref=src[t].at[2 * chip[0] + chip[1]], dst_ref=dst[t].at[j], send_sem=send.at[3 * t + j],
                                             recv_sem=recv.at[3 * t + j], device_id=(*chip, c), device_id_type=MESH).start()
        token[...] = jnp.zeros_like(token)

    res = pl.pallas_call(
        body, name=name, in_specs=[ANY] * (2 * n), out_specs=[ANY] * (2 * n) + [SEM, SEM, pl.BlockSpec(memory_space=pltpu.VMEM)],
        out_shape=[_struct(a) for a in sums + lands] + [pltpu.SemaphoreType.DMA((3 * n,)), pltpu.SemaphoreType.DMA((3 * n,)), TOKEN],
        input_output_aliases={t: t for t in range(2 * n)}, compiler_params=_split_params(),
    )(*sums, *lands)
    return list(res[:n]), list(res[n:2 * n]), res[2 * n], res[2 * n + 1], res[2 * n + 2]


def _chip_exchange_wait(name, sums, lands, send, recv, after):
    n = len(sums)

    def body(*refs):
        send_r, recv_r = refs[2 * n], refs[2 * n + 1]
        src, dst = refs[2 * n + 3:3 * n + 3], refs[3 * n + 3:4 * n + 3]
        x, y, c, p, sib, chips = _coords()
        for t in range(n):
            for j, chip in enumerate(chips):
                cp = pltpu.make_async_remote_copy(src_ref=src[t].at[2 * chip[0] + chip[1]], dst_ref=dst[t].at[j], send_sem=send_r.at[3 * t + j],
                                                  recv_sem=recv_r.at[3 * t + j], device_id=(*chip, c), device_id_type=MESH)
                cp.wait_send()
                cp.wait_recv()

    res = pl.pallas_call(
        body, name=name, in_specs=[ANY] * (2 * n) + [SEM, SEM, ANY], out_specs=[ANY] * (2 * n),
        out_shape=[_struct(a) for a in sums + lands], input_output_aliases={t: t for t in range(2 * n)},
        compiler_params=_split_params(),
    )(*sums, *lands, send, recv, after)
    return list(res[:n]), list(res[n:])


def _pair_exchange(name, parts):
    n = len(parts)

    def body(*refs):
        ins, outs = refs[:n], refs[n:2 * n]
        send, recv = refs[2 * n:]
        x, y, c, p, sib, chips = _coords()
        cps = [pltpu.make_async_remote_copy(src_ref=ins[t].at[1 - c], dst_ref=outs[t], send_sem=send.at[t], recv_sem=recv.at[t],
                                            device_id=sib, device_id_type=MESH) for t in range(n)]
        for cp in cps:
            cp.start()
        for cp in cps:
            cp.wait()

    return pl.pallas_call(
        body, name=name, in_specs=[ANY] * n, out_specs=[ANY] * n,
        out_shape=[jax.ShapeDtypeStruct(a.shape[1:], a.dtype) for a in parts],
        scratch_shapes=[pltpu.SemaphoreType.DMA((n,))] * 2,
    )(*parts)


def _chip_exchange(name, sums):
    n = len(sums)

    def body(*refs):
        ins, outs = refs[:n], refs[n:2 * n]
        send, recv = refs[2 * n:]
        x, y, c, p, sib, chips = _coords()
        cps = [pltpu.make_async_remote_copy(src_ref=ins[t].at[2 * chip[0] + chip[1]], dst_ref=outs[t].at[j], send_sem=send.at[t, j],
                                            recv_sem=recv.at[t, j], device_id=(*chip, c), device_id_type=MESH)
               for t in range(n) for j, chip in enumerate(chips)]
        for cp in cps:
            cp.start()
        for cp in cps:
            cp.wait()

    return pl.pallas_call(
        body, name=name, in_specs=[ANY] * n, out_specs=[ANY] * n,
        out_shape=[jax.ShapeDtypeStruct((3,) + a.shape[1:], a.dtype) for a in sums],
        scratch_shapes=[pltpu.SemaphoreType.DMA((n, 3))] * 2,
    )(*sums)


def _pair_share(name, bufs, items, deps=()):
    n = len(items)
    nb = len(bufs)
    nd = len(deps)

    def body(*refs):
        outs = refs[nb + nd:2 * nb + nd]
        send, recv = refs[2 * nb + nd:]
        x, y, c, p, sib, chips = _coords()

        def blk(t, half):
            o, lead = items[t]
            return outs[o].at[p if lead == 'chip' else lead, half]

        def swap(t, half):
            return pltpu.make_async_remote_copy(src_ref=blk(t, half), dst_ref=blk(t, half), send_sem=send.at[t], recv_sem=recv.at[t],
                                                device_id=sib, device_id_type=MESH)

        cps = [swap(t, c) for t in range(n)]
        for cp in cps:
            cp.start()
        for t in range(n):
            swap(t, 1 - c).wait_recv()
        for cp in cps:
            cp.wait_send()

    return list(pl.pallas_call(
        body, name=name, in_specs=[ANY] * (nb + nd), out_specs=[ANY] * nb,
        out_shape=[jax.ShapeDtypeStruct(b.shape, b.dtype) for b in bufs],
        input_output_aliases={t: t for t in range(nb)},
        scratch_shapes=[pltpu.SemaphoreType.DMA((n,))] * 2,
    )(*bufs, *deps))


def _flat2(a, lead):
    return a.reshape(a.shape[:lead] + (-1, a.shape[-1]))


def _reduce_begin(tag, parts):
    parts, lands, send, recv, token = _full_exchange_start(f"rs_start_{tag}", parts)
    return (parts, lands, send, recv), token


def _reduce_end(tag, state, after, dests, bufs, buf_shapes):
    c = lax.axis_index("c").astype(jnp.int32)
    p = (2 * lax.axis_index("x") + lax.axis_index("y")).astype(jnp.int32)
    parts, lands = _full_exchange_wait(f"rs_wait_{tag}", *state, after)

    def total(a, *others):
        s = a.astype(F32)
        for b in others:
            s = s + b.astype(F32)
        return (s,)

    for t, (mine, theirs) in enumerate(zip(parts, lands)):
        o, lead = dests[t]
        shape = buf_shapes[o]
        rows, cols = shape[2], shape[3]
        m3, t3 = mine.reshape(2 * N_CHIPS, rows, cols), theirs.reshape(7, rows, cols)
        pre = jnp.stack([c * N_CHIPS + p] + [jnp.int32(k) for k in range(7)] + [c, p if lead == 'chip' else jnp.int32(lead)])
        out = ('x', shape, F32, (None, None, 'tr', cols), lambda r, pr: (pr[9], pr[8], r, 0))
        bufs[o] = _rows(f"rs_sum_{tag}_{t}", total, [(m3, 's', cols, 0)] + [(t3, 's', cols, 1 + k) for k in range(7)], [out], 256,
                        pre=pre, into=bufs[o])[0]


def kernel(x, norm_w, out_proj, s5_in_proj, s5_a_re, s5_a_im, s5_log_dt, s5_b_re, s5_b_im, s5_c_re, s5_c_im, s5_d, s5_w_glu, s5_b_glu, fox_in_proj, fox_q_norm, fox_k_norm, fox_f_bias, pool_in_proj, pool_w_group, pool_scale, loss_target, m_norm_w, m_out_proj, m_s5_in_proj, m_s5_a_re, m_s5_a_im, m_s5_log_dt, m_s5_b_re, m_s5_b_im, m_s5_c_re, m_s5_c_im, m_s5_d, m_s5_w_glu, m_s5_b_glu, m_fox_in_proj, m_fox_q_norm, m_fox_k_norm, m_fox_f_bias, m_pool_in_proj, m_pool_w_group, m_pool_scale, v_norm_w, v_out_proj, v_s5_in_proj, v_s5_a_re, v_s5_a_im, v_s5_log_dt, v_s5_b_re, v_s5_b_im, v_s5_c_re, v_s5_c_im, v_s5_d, v_s5_w_glu, v_s5_b_glu, v_fox_in_proj, v_fox_q_norm, v_fox_k_norm, v_fox_f_bias, v_pool_in_proj, v_pool_w_group, v_pool_scale):
    weights = dict(norm_w=norm_w, out_proj=out_proj, s5_in_proj=s5_in_proj, s5_a_re=s5_a_re, s5_a_im=s5_a_im, s5_log_dt=s5_log_dt,
                   s5_b_re=s5_b_re, s5_b_im=s5_b_im, s5_c_re=s5_c_re, s5_c_im=s5_c_im, s5_d=s5_d, s5_w_glu=s5_w_glu, s5_b_glu=s5_b_glu,
                   fox_in_proj=fox_in_proj, fox_q_norm=fox_q_norm, fox_k_norm=fox_k_norm, fox_f_bias=fox_f_bias,
                   pool_in_proj=pool_in_proj, pool_w_group=pool_w_group, pool_scale=pool_scale)
    mom_m = dict(norm_w=m_norm_w, out_proj=m_out_proj, s5_in_proj=m_s5_in_proj, s5_a_re=m_s5_a_re, s5_a_im=m_s5_a_im, s5_log_dt=m_s5_log_dt,
                 s5_b_re=m_s5_b_re, s5_b_im=m_s5_b_im, s5_c_re=m_s5_c_re, s5_c_im=m_s5_c_im, s5_d=m_s5_d, s5_w_glu=m_s5_w_glu, s5_b_glu=m_s5_b_glu,
                 fox_in_proj=m_fox_in_proj, fox_q_norm=m_fox_q_norm, fox_k_norm=m_fox_k_norm, fox_f_bias=m_fox_f_bias,
                 pool_in_proj=m_pool_in_proj, pool_w_group=m_pool_w_group, pool_scale=m_pool_scale)
    mom_v = dict(norm_w=v_norm_w, out_proj=v_out_proj, s5_in_proj=v_s5_in_proj, s5_a_re=v_s5_a_re, s5_a_im=v_s5_a_im, s5_log_dt=v_s5_log_dt,
                 s5_b_re=v_s5_b_re, s5_b_im=v_s5_b_im, s5_c_re=v_s5_c_re, s5_c_im=v_s5_c_im, s5_d=v_s5_d, s5_w_glu=v_s5_w_glu, s5_b_glu=v_s5_b_glu,
                 fox_in_proj=v_fox_in_proj, fox_q_norm=v_fox_q_norm, fox_k_norm=v_fox_k_norm, fox_f_bias=v_fox_f_bias,
                 pool_in_proj=v_pool_in_proj, pool_w_group=v_pool_w_group, pool_scale=v_pool_scale)
    return _step(x, loss_target, weights, mom_m, mom_v)


BIG = ('out_proj', 's5_in_proj', 's5_w_glu', 'fox_in_proj', 'pool_in_proj', 'pool_w_group')
SMALL = ('norm_w', 's5_a_re', 's5_a_im', 's5_log_dt', 's5_b_re', 's5_b_im', 's5_c_re', 's5_c_im', 's5_d', 's5_b_glu',
         'fox_q_norm', 'fox_k_norm', 'fox_f_bias', 'pool_scale')
SMALL_SHARDED = ('s5_d', 's5_b_glu', 'pool_scale')
GROUP_AXIS_1 = ('s5_a_re', 's5_a_im', 's5_b_re', 's5_b_im', 's5_c_re', 's5_c_im')
ORDER = ('norm_w', 'out_proj', 's5_in_proj', 's5_a_re', 's5_a_im', 's5_log_dt', 's5_b_re', 's5_b_im', 's5_c_re', 's5_c_im', 's5_d',
         's5_w_glu', 's5_b_glu', 'fox_in_proj', 'fox_q_norm', 'fox_k_norm', 'fox_f_bias', 'pool_in_proj', 'pool_w_group', 'pool_scale')


def _split2(shape):
    if shape[0] % 2 == 0:
        return (2, shape[0] // 2) + tuple(shape[1:])
    assert shape[0] == 1 and shape[1] % 2 == 0
    return (2, shape[1] // 2) + tuple(shape[2:])


def _adamw_big(n, w, grads, mom_m, mom_v, delta, new_m, new_v):
    shape = w[n].shape
    if shape[-1] % LANES:
        f2 = lambda a: jnp.transpose(a.reshape(-1, shape[-1]))
        b2 = lambda a: jnp.transpose(a).reshape(shape)
    else:
        f2 = lambda a: a.reshape(-1, shape[-1])
        b2 = lambda a: a.reshape(shape)
    d_, m_, v_ = _adamw(f"adamw_{n}", f2(w[n]), f2(grads[n]), f2(mom_m[n]), f2(mom_v[n]))
    delta[n], new_m[n], new_v[n] = b2(d_), b2(m_), b2(v_)
    return d_


def _cast_weight(w, n, l, deps=()):
    p = (2 * lax.axis_index("x") + lax.axis_index("y")).astype(jnp.int32)
    a3 = w[n].reshape(w[n].shape[0], -1, w[n].shape[-1])
    layers, rows, cols = a3.shape
    out = ('x', (N_CHIPS, rows, cols), BF16, (None, 'tr', cols), lambda r, pr: (pr[0], r, 0))
    b = _rows(f"cast_{n}_{l}", lambda v: (v,), [(a3, 's', cols, 1)], [out], 256, pre=jnp.stack([p, jnp.int32(l)]), deps=deps)[0]
    return b.reshape(N_CHIPS, 2, rows // 2, cols)


def _step(x, loss_target, w, mom_m, mom_v):
    T, D = x.shape[1], x.shape[2]
    E = D
    G, P, C = w['s5_a_re'].shape[1], S5_STATE, S5_GROUP
    H = E // FOX_HEAD_DIM
    PG = len(POOL_WINDOWS)
    PD = E // PG
    NC = G // GROUPS_PER_CHUNK
    L = GROUPS_PER_CHUNK * P
    tq = _t(256, T)
    nq = T // tq

    phases = [[('s5_in_proj', 0)],
              [('s5_w_glu', 0), ('out_proj', 0)],
              [('out_proj', 1), ('fox_in_proj', 0)],
              [('out_proj', 2), ('pool_in_proj', 0), ('pool_w_group', 0), ('out_proj', 3), ('s5_in_proj', 1), ('s5_w_glu', 1)]]
    W = {}
    flight = {}

    def landed(keys, bufs):
        for k, b in zip(keys, bufs):
            W[k] = b.reshape(N_CHIPS, 2 * b.shape[2], b.shape[3])

    def take_phase(ph, after):
        bufs, send, recv, _ = flight.pop(ph)
        landed(phases[ph], _gather_forward(f"gather_{ph}_pass", _gather_wait(f"gather_{ph}_wait", bufs, send, recv, after)))

    small_full = {}
    chip = 2 * lax.axis_index("x") + lax.axis_index("y")
    sv = [lax.dynamic_update_index_in_dim(jnp.zeros((N_CHIPS, 2) + w[n].shape, F32), jnp.stack([w[n], w[n]]), chip, 0)
          for n in SMALL_SHARDED]
    got = _chip_allgather("gather_vectors", sv)
    for n, g in zip(SMALL_SHARDED, got):
        small_full[n] = jnp.transpose(g[:, 0], (1, 0, 2)).reshape(w[n].shape[0], E)
    after = [got[0]]
    for ph in range(len(phases)):
        flight[ph] = _gather_start(f"gather_{ph}_start", [_cast_weight(w, n, l, after if ph else ()) for n, l in phases[ph]], after)
        after = [flight[ph][3]]
    take_phase(0, after[0])
    gather_tokens = after

    norm_w = w['norm_w']
    h = x.reshape(T, D)
    saved = []
    dparts = {}

    def s5_consts(j):
        ar, ai, fr, fi = _s5_disc_fwd(f"s5_disc_{j}", w['s5_a_re'][j], w['s5_a_im'][j], w['s5_log_dt'][j].reshape(G, 1))
        br, bi = w['s5_b_re'][j].reshape(G * P, C), w['s5_b_im'][j].reshape(G * P, C)
        bbr, bbi = _s5_bbar(f"s5_bbar_{j}", fr.reshape(G * P, 1), fi.reshape(G * P, 1), br, bi)
        bbd = jnp.concatenate([_compact(bbr.reshape(G, P, C), NC), _compact(bbi.reshape(G, P, C), NC)], axis=2).astype(BF16)
        ct = lambda v: jnp.transpose(v, (0, 2, 1))
        cbd = jnp.concatenate([_compact(ct(w['s5_c_re'][j]), NC), -_compact(ct(w['s5_c_im'][j]), NC)], axis=2).astype(BF16)
        return dict(ar=ar, ai=ai, fr=fr, fi=fi, br=br, bi=bi, bbd=bbd, cbd=cbd,
                    ar3=ar.reshape(NC, 1, L), ai3=ai.reshape(NC, 1, L))

    for i in range(4):
        kind, j = i % 3, i // 3
        nw = norm_w[i].reshape(1, D)
        xn = _norm_fwd(f"norm_{i}", h, nw, deps=gather_tokens if i == 0 else ())
        if kind == 0:
            k5 = s5_consts(j)
            proj = _mm_proj(f"s5_proj_{i}", xn, W[('s5_in_proj', j)])
            dsk = small_full['s5_d'][j].reshape(1, E)
            y1, g, hs = _s5_fwd(f"s5_scan_{i}", proj, k5['bbd'], k5['cbd'], k5['ar3'], k5['ai3'], dsk, E)
            bglu = small_full['s5_b_glu'][j].reshape(1, E)
            if i == 0:
                take_phase(1, y1)

            def glu_epi(acc, b, y1t, z):
                lin = acc + b
                return lin, (_gelu(y1t) * _sigmoid(lin)) * _silu(z)

            lin, a = _mm_rowsharded(
                f"s5_glu_{i}", g, W[('s5_w_glu', j)], epi=glu_epi,
                extras=lambda tm, tn: [(bglu, _rowvec(tn)), (y1, _tile(tm, tn)), (proj, _tile(tm, tn, E // tn))],
                outs_fn=lambda tm, tn: [((T, E), F32, _tile(tm, tn)), ((T, E), BF16, _tile(tm, tn))])
            saved.append(dict(h=h, xn=xn, proj=proj, y1=y1, g=g, hs=hs, lin=lin, a=a, k5=k5, dsk=dsk))
        elif kind == 1:
            fox_w = jnp.transpose(W[('fox_in_proj', j)], (1, 0, 2)).reshape(D, -1)
            w_qkvz = fox_w[:, :4 * E]
            w_f = jnp.pad(fox_w[:, 4 * E:], ((0, 0), (0, LANES - H)))
            proj = _mm_plain(f"fox_proj_{i}", xn, w_qkvz)[0]
            flog = _mm_plain(f"fox_gate_proj_{i}", xn, w_f)[0]
            fb = jnp.pad(w['fox_f_bias'][j].reshape(1, H), ((0, 0), (0, LANES - H)))
            wq, wk = w['fox_q_norm'][j].reshape(1, FOX_HEAD_DIM), w['fox_k_norm'][j].reshape(1, FOX_HEAD_DIM)
            qn, kn = _qk_norm(f"fox_qk_norm_{i}", proj, wq, wk, H)
            cum = _cum_rows(f"fox_cum_{i}", flog, fb, False, True)
            cum_t = jnp.transpose(cum)[:H]
            cum_q = jnp.broadcast_to(cum_t[:, :, None], (H, T, LANES))
            cum_k = cum_t.reshape(H, nq, 1, tq)
            y, lse = _attn_fwd(f"fox_attn_{i}", qn, kn, proj, cum_q, cum_k, H)
            a = _rows(f"fox_gate_{i}", lambda yt, z: (yt * _silu(z),), [(y, 'r', E, 0), (proj, 'r', E, 3)], [('r', E, BF16)], 256)[0]
            saved.append(dict(h=h, xn=xn, proj=proj, flog=flog, fb=fb, wq=wq, wk=wk, qn=qn, kn=kn, cum_q=cum_q, cum_k=cum_k, y=y, lse=lse, a=a,
                              w_qkvz=w_qkvz, w_f=w_f))
        else:
            w_pg = jnp.transpose(W[('pool_w_group', j)].reshape(N_CHIPS, PG, PD // N_CHIPS, PD), (1, 0, 2, 3)).reshape(PG, PD, PD)
            proj = _mm_proj(f"pool_proj_{i}", xn, W[('pool_in_proj', j)])
            pm = _pool_fwd(f"pool_win_{i}", proj, E)
            scale = small_full['pool_scale'][j].reshape(1, E)
            tm, tn, tk = _t(512, T), _t(512, PD), _t(K_STEP, PD)
            kb, nb = PD // tk, PD // tn
            mixed, a = _mm(
                f"pool_mix_{i}", pm, w_pg, M=T, N=PD, K=PD, tm=tm, tn=tn, tk=tk, groups=PG,
                a_spec=_bs((tm, tk), lambda g, m, n, k: (m, g * kb + k)),
                b_spec=_bs((None, tk, tn), lambda g, m, n, k: (g, k, n)),
                extras=[(scale, _bs((1, tn), lambda g, m, n, k: (0, g * nb + n))),
                        (proj, _bs((tm, tn), lambda g, m, n, k: (m, E // tn + g * nb + n)))],
                epi=lambda acc, sc, z: (acc, (acc * sc) * _silu(z)),
                outs=[((T, E), F32, _bs((tm, tn), lambda g, m, n, k: (m, g * nb + n))),
                      ((T, E), BF16, _bs((tm, tn), lambda g, m, n, k: (m, g * nb + n)))])
            saved.append(dict(h=h, xn=xn, proj=proj, pm=pm, mixed=mixed, scale=scale, a=a, w_pg=w_pg))
        h = _mm_rowsharded(f"out_proj_{i}", saved[-1]['a'], W[('out_proj', i)], epi=lambda acc, r: (r + acc,),
                           extras=lambda tm, tn: [(h, _tile(tm, tn))],
                           outs_fn=lambda tm, tn: [((T, D), F32, _tile(tm, tn))])[0]
        if i < 2:
            take_phase(i + 2, h)

    dh, dh16, loss_cols = _loss(h, loss_target.reshape(T, D))
    loss = lax.psum(jnp.sum(loss_cols), ("x", "y", "c"))

    gsmall = {n: [None] * w[n].shape[0] for n in SMALL}
    big_index = {n: o for o, n in enumerate(BIG)}
    rs_shapes = [None] * (len(BIG) + 1)
    rs_bufs = [None] * (len(BIG) + 1)
    rs_dests_all = []
    pending = None

    def reduce_layer(tag, named_parts):
        parts, dests = [], []
        for n, l, pt in named_parts:
            o = big_index[n] if n in big_index else len(BIG)
            half = pt.shape[2:]
            rs_shapes[o] = (N_CHIPS if l == 'chip' else w[n].shape[0], 2, math.prod(half[:-1]), half[-1])
            parts.append(pt)
            dests.append((o, l))
        rs_dests_all.extend(dests)
        state, token = _reduce_begin(tag, parts)
        return (tag, state, dests), token

    token = loss.reshape(1, 1)
    for i in reversed(range(4)):
        kind, j = i % 3, i // 3
        sv_ = saved[i]
        nw = norm_w[i].reshape(1, D)
        w_out = W[('out_proj', i)]
        after_start = [token] if token is not None else ()
        layer_parts = [('out_proj', i, _mm_dw_rows(f"d_out_proj_{i}", sv_['a'], dh16, deps=after_start))]
        if kind == 0:
            w_glu = W[('s5_w_glu', j)]
            proj, y1, lin, k5 = sv_['proj'], sv_['y1'], sv_['lin'], sv_['k5']

            def da_epi(da, y1t, lint, z):
                gt, sg = _gelu(y1t), _sigmoid(lint)
                dy2 = da * _silu(z)
                dlin = (dy2 * gt) * (sg * (1.0 - sg))
                return da * (gt * sg) * _dsilu(z), dlin, dy2 * sg, _colsum(dlin)

            nm = T // _t(512, T)
            dz, dlin, dgd, dbg = _mm_rowsharded_t(
                f"d_s5_act_{i}", dh16, w_out, epi=da_epi, deps=after_start,
                extras=lambda tm, tn: [(y1, _tile(tm, tn)), (lin, _tile(tm, tn)), (proj, _tile(tm, tn, E // tn))],
                outs_fn=lambda tm, tn: [((T, E), BF16, _tile(tm, tn)), ((T, E), BF16, _tile(tm, tn)), ((T, E), F32, _tile(tm, tn)),
                                        ((nm, 1, E), F32, _bs((None, 1, tn), lambda g, m, n, k: (m, 0, n)))])
            gsmall['s5_b_glu'][j] = jnp.sum(dbg, axis=(0, 1))
            layer_parts.append(('s5_w_glu', j, _mm_dw_rows(f"d_s5_w_glu_{i}", sv_['g'], dlin)))
            glu_deps = ()
            if i == 0:
                early, early_token = reduce_layer("l0a", layer_parts)
                layer_parts, glu_deps = [], [early_token]
            dy1 = _mm_rowsharded_t(
                f"d_s5_glu_{i}", dlin, w_glu, epi=lambda acc, d, y1t: ((acc + d) * _dgelu(y1t),), deps=glu_deps,
                extras=lambda tm, tn: [(dgd, _tile(tm, tn)), (y1, _tile(tm, tn))],
                outs_fn=lambda tm, tn: [((T, E), F32, _tile(tm, tn))])[0]
            du, dbd, dcd, dab, ddk = _s5_bwd(f"d_s5_scan_{i}", dy1, proj, sv_['hs'], k5['bbd'], k5['cbd'], k5['ar3'], k5['ai3'], sv_['dsk'], E)
            gsmall['s5_d'][j] = ddk.reshape(E)
            gsmall['s5_c_re'][j] = jnp.transpose(_uncompact(dcd[:, :, :L], G), (0, 2, 1))
            gsmall['s5_c_im'][j] = -jnp.transpose(_uncompact(dcd[:, :, L:], G), (0, 2, 1))
            dbbr = _uncompact(dbd[:, :, :L], G).reshape(G * P, C)
            dbbi = _uncompact(dbd[:, :, L:], G).reshape(G * P, C)
            dbr, dbi, dfr, dfi = _s5_bbar_bwd(f"d_s5_bbar_{i}", k5['fr'].reshape(G * P, 1), k5['fi'].reshape(G * P, 1), k5['br'], k5['bi'], dbbr, dbbi)
            gsmall['s5_b_re'][j] = dbr.reshape(G, P, C)
            gsmall['s5_b_im'][j] = dbi.reshape(G, P, C)
            dab = jnp.sum(dab, axis=1)
            dare, daim, dldt = _s5_disc_bwd(f"d_s5_disc_{i}", w['s5_a_re'][j], w['s5_a_im'][j], w['s5_log_dt'][j].reshape(G, 1),
                                            (dab[:, :L].reshape(G, P), dab[:, L:].reshape(G, P), dfr.reshape(G, P), dfi.reshape(G, P)))
            gsmall['s5_a_re'][j], gsmall['s5_a_im'][j], gsmall['s5_log_dt'][j] = dare, daim, dldt.reshape(G)
            dproj = jnp.concatenate([du, dz], axis=1)
            layer_parts.append(('s5_in_proj', j, _mm_dw_cols(f"d_s5_in_proj_{i}", sv_['xn'], dproj)))
            dxn = _mm_colsharded_t(f"d_s5_xn_{i}", dproj, W[('s5_in_proj', j)])
        elif kind == 1:
            proj, y = sv_['proj'], sv_['y']
            do, dz = _mm_rowsharded_t(
                f"d_fox_act_{i}", dh16, w_out, epi=lambda da, yt, z: (da * _silu(z), (da * yt) * _dsilu(z)), deps=after_start,
                extras=lambda tm, tn: [(y, _tile(tm, tn)), (proj, _tile(tm, tn, 3 * E // tn))],
                outs_fn=lambda tm, tn: [((T, E), F32, _tile(tm, tn)), ((T, E), BF16, _tile(tm, tn))])
            dqn, dkn, dv, dcq, dck = _attn_bwd(f"d_fox_attn_{i}", sv_['qn'], sv_['kn'], proj, do, y, sv_['lse'], sv_['cum_q'], sv_['cum_k'], H)
            dq, dk, dwq, dwk = _qk_norm_bwd(f"d_fox_qk_norm_{i}", proj, sv_['wq'], sv_['wk'], dqn, dkn, H)
            gsmall['fox_q_norm'][j], gsmall['fox_k_norm'][j] = dwq.reshape(-1), dwk.reshape(-1)
            dcum = dcq + jnp.pad(jnp.transpose(dck.reshape(H, T)), ((0, 0), (0, LANES - H)))
            dls = _cum_rows(f"d_fox_cum_{i}", dcum, jnp.zeros((1, LANES), F32), True, False)
            dflog, dfb = _rows(f"d_fox_gate_{i}", lambda d, f, b: ((lambda r: (r, _colsum(r)))(d * _sigmoid(-(f + b)))),
                               [(dls, 'r', LANES, 0), (sv_['flog'], 'r', LANES, 0), (sv_['fb'], 'b', LANES, 0)],
                               [('r', LANES, BF16), ('a', LANES, F32)], 256)
            gsmall['fox_f_bias'][j] = dfb[0, :H]
            dproj = jnp.concatenate([dq, dk, dv, dz], axis=1)
            tkT = _t(K_STEP, T)
            dw_qkvz = _mm(f"d_fox_in_proj_{i}", sv_['xn'], dproj, M=D, N=4 * E, K=T, tm=_t(512, D), tn=_t(1024, 4 * E), tk=tkT, ta=True,
                          a_spec=_bs((tkT, _t(512, D)), lambda g, m, n, k: (k, m)),
                          b_spec=_bs((tkT, _t(1024, 4 * E)), lambda g, m, n, k: (k, n)),
                          outs=[((D, 4 * E), BF16, _tile(_t(512, D), _t(1024, 4 * E)))])[0]
            dw_f = _mm(f"d_fox_gate_proj_{i}", sv_['xn'], dflog, M=D, N=LANES, K=T, tm=_t(512, D), tn=LANES, tk=tkT, ta=True,
                       a_spec=_bs((tkT, _t(512, D)), lambda g, m, n, k: (k, m)),
                       b_spec=_bs((tkT, LANES), lambda g, m, n, k: (k, n)),
                       outs=[((D, LANES), BF16, _tile(_t(512, D), LANES))])[0]
            dw_fox = jnp.concatenate([dw_qkvz, dw_f[:, :H]], axis=1)
            sw = dw_fox.shape[1] // N_CHIPS
            layer_parts.append(('fox_in_proj', j, jnp.transpose(dw_fox.reshape(2, D // 2, N_CHIPS, sw), (0, 2, 1, 3))))
            w_qkvz, w_f = sv_['w_qkvz'], sv_['w_f']
            dxn_f = _mm(f"d_fox_xn_gate_{i}", dflog, w_f, M=T, N=D, K=LANES, tm=_t(512, T), tn=_t(1024, D), tk=LANES, tb=True,
                        a_spec=_bs((_t(512, T), LANES), lambda g, m, n, k: (m, k)),
                        b_spec=_bs((_t(1024, D), LANES), lambda g, m, n, k: (n, k)),
                        outs=[((T, D), F32, _tile(_t(512, T), _t(1024, D)))])[0]
            tm, tn, tk = _t(512, T), _t(1024, D), _t(K_STEP, 4 * E)
            dxn = _mm(f"d_fox_xn_{i}", dproj, w_qkvz, M=T, N=D, K=4 * E, tm=tm, tn=tn, tk=tk, tb=True,
                      a_spec=_bs((tm, tk), lambda g, m, n, k: (m, k)), b_spec=_bs((tn, tk), lambda g, m, n, k: (n, k)),
                      extras=[(dxn_f, _tile(tm, tn))], epi=lambda acc, e: (acc + e,),
                      outs=[((T, D), F32, _tile(tm, tn))])[0]
        else:
            proj, mixed, scale = sv_['proj'], sv_['mixed'], sv_['scale']
            nm = T // _t(512, T)

            def pool_epi(da, mx, sc, z):
                dy = da * _silu(z)
                return (da * (mx * sc)) * _dsilu(z), dy * sc, _colsum(dy * mx)

            dz, dmix, dsc = _mm_rowsharded_t(
                f"d_pool_act_{i}", dh16, w_out, epi=pool_epi, deps=after_start,
                extras=lambda tm, tn: [(mixed, _tile(tm, tn)), (scale, _rowvec(tn)), (proj, _tile(tm, tn, E // tn))],
                outs_fn=lambda tm, tn: [((T, E), BF16, _tile(tm, tn)), ((T, E), BF16, _tile(tm, tn)),
                                        ((nm, 1, E), F32, _bs((None, 1, tn), lambda g, m, n, k: (m, 0, n)))])
            gsmall['pool_scale'][j] = jnp.sum(dsc, axis=(0, 1))
            w_pg = sv_['w_pg']
            tkw = PD // N_CHIPS
            tk = _t(K_STEP, T)
            layer_parts.append(('pool_w_group', j, _mm(
                f"d_pool_w_group_{i}", sv_['pm'], dmix, M=PD, N=PD, K=T, tm=tkw, tn=PD, tk=tk, groups=PG, ta=True,
                a_spec=_bs((tk, tkw), lambda g, m, n, k: (k, g * (PD // tkw) + m)),
                b_spec=_bs((tk, PD), lambda g, m, n, k: (k, g)),
                outs=[((2, N_CHIPS, PG // 2, tkw, PD), BF16, _bs((None, None, None, tkw, PD), lambda g, m, n, k: (g // (PG // 2), m, g % (PG // 2), 0, 0)))])[0]))
            tm, tn2, tk2 = _t(512, T), _t(512, PD), _t(K_STEP, PD)
            dpm = _mm(f"d_pool_mix_{i}", dmix, w_pg, M=T, N=PD, K=PD, tm=tm, tn=tn2, tk=tk2, groups=PG, tb=True,
                      a_spec=_bs((tm, tk2), lambda g, m, n, k: (m, g * (PD // tk2) + k)),
                      b_spec=_bs((None, tn2, tk2), lambda g, m, n, k: (g, n, k)),
                      outs=[((T, E), F32, _bs((tm, tn2), lambda g, m, n, k: (m, g * (PD // tn2) + n)))])[0]
            du = _pool_bwd(f"d_pool_win_{i}", dpm, E)
            dproj = jnp.concatenate([du, dz], axis=1)
            layer_parts.append(('pool_in_proj', j, _mm_dw_cols(f"d_pool_in_proj_{i}", sv_['xn'], dproj)))
            dxn = _mm_colsharded_t(f"d_pool_xn_{i}", dproj, W[('pool_in_proj', j)])
        dh, dh16, dnw = _norm_bwd(f"d_norm_{i}", dxn, sv_['h'], nw, dh)
        gsmall['norm_w'][i] = dnw.reshape(D)
        if pending is not None:
            _reduce_end(pending[0], pending[1], dh16, pending[2], rs_bufs, rs_shapes)
        if i > 0:
            pending, token = reduce_layer(f"l{i}", layer_parts)
    grad_x = dh.reshape(x.shape)

    small_flat = jnp.concatenate([jnp.stack(gsmall[n]).reshape(-1) for n in SMALL])
    n_small = small_flat.shape[0]
    unit = 2 * N_CHIPS * 16 * LANES
    n_pad = -(-n_small // unit) * unit
    R = n_pad // (2 * N_CHIPS * LANES)
    small_part = jnp.pad(small_flat, (0, n_pad - n_small)).astype(BF16).reshape(2, N_CHIPS, R, LANES)
    pending, token = reduce_layer("l0", layer_parts + [('small', 'chip', small_part)])
    _reduce_end(early[0], early[1], token, early[2], rs_bufs, rs_shapes)
    nb = len(BIG)
    done_items = [d for d in rs_dests_all if d not in pending[2]]
    rs_bufs[:nb] = _pair_share("rs_pair_share_a", rs_bufs[:nb], done_items, deps=[token])
    late = [o for o, _ in pending[2]]
    delta, new_m, new_v = {}, {}, {}
    grads = {}
    last = token[:1, :1]
    for o, n in enumerate(BIG):
        if o not in late:
            grads[n] = rs_bufs[o].reshape(w[n].shape)
            last = last + _adamw_big(n, w, grads, mom_m, mom_v, delta, new_m, new_v)[:1, :1]
    _reduce_end(pending[0], pending[1], last, pending[2], rs_bufs, rs_shapes)
    shared = _pair_share("rs_pair_share_b", [rs_bufs[o] for o in late], [(k, l) for k, (_, l) in enumerate(pending[2])])
    for k, o in enumerate(late):
        rs_bufs[o] = shared[k]
        if o < nb:
            grads[BIG[o]] = shared[k].reshape(w[BIG[o]].shape)
            _adamw_big(BIG[o], w, grads, mom_m, mom_v, delta, new_m, new_v)
    small_all = _chip_allgather("gather_small_grads", [rs_bufs[nb]])[0]
    small_all = jnp.transpose(small_all, (1, 0, 2, 3)).reshape(-1)[:n_small]
    off = 0
    p = 2 * lax.axis_index("x") + lax.axis_index("y")
    for n in SMALL:
        full_shape = (w[n].shape[0], E) if n in SMALL_SHARDED else w[n].shape
        size = math.prod(full_shape)
        gfull = small_all[off:off + size].reshape(full_shape)
        off += size
        if n in SMALL_SHARDED:
            gfull = lax.dynamic_slice_in_dim(gfull, p * (E // N_CHIPS), E // N_CHIPS, axis=1)
        grads[n] = gfull

    for n in SMALL:
        shape = w[n].shape
        if n in GROUP_AXIS_1:
            perm = (0,) + tuple(range(2, len(shape))) + (1,)
            inv = (0, len(shape) - 1) + tuple(range(1, len(shape) - 1))
            view = lambda a: jnp.transpose(a, perm).reshape(-1, shape[1])
            back = lambda a: jnp.transpose(a.reshape(tuple(shape[k] for k in perm)), inv)
        else:
            view = lambda a: a.reshape(-1, shape[-1])
            back = lambda a: a.reshape(shape)
        d_, m_, v_ = _adamw(f"adamw_{n}", view(w[n]), view(grads[n]), view(mom_m[n]), view(mom_v[n]))
        delta[n], new_m[n], new_v[n] = back(d_), back(m_), back(v_)
    return (loss, grad_x, *[grads[n] for n in ORDER], *[delta[n] for n in ORDER], *[new_m[n] for n in ORDER], *[new_v[n] for n in ORDER])
```

```python
import functools
import math

import jax
import jax.numpy as jnp
from jax import lax
from jax.experimental import pallas as pl
from jax.experimental.pallas import tpu as pltpu

F32 = jnp.float32
BF16 = jnp.bfloat16
MESH = pl.DeviceIdType.MESH

N_CHIPS = 4
VMEM_LIMIT = 56 * 1024 * 1024
LANES = 128
SUB = 8

EPS = 1e-6
S5_GROUP = 16
S5_STATE = 64
GROUPS_PER_CHUNK = 16
S5_TIME_BLOCK = 512
FOX_HEAD_DIM = 128
NORM_HEADS = 4
ATTN_SUB = 256
ATTN_HEADS = 2
POOL_WINDOWS = (2, 4, 8, 16)
POOL_HALO = 16
ADAM_LR, ADAM_B1, ADAM_B2, ADAM_EPS, ADAM_WD, ADAM_STEP = 0.001, 0.9, 0.999, 1e-08, 0.01, 10
NEG = -1e30
K_STEP = 2048


ANY = pl.BlockSpec(memory_space=pl.ANY)


def _t(pref, dim):
    if dim <= pref:
        return dim
    t = pref - pref % 16
    while t > 16 and dim % t:
        t -= 16
    assert dim % t == 0, (pref, dim)
    return t


def _params(sem):
    return pltpu.CompilerParams(dimension_semantics=sem, vmem_limit_bytes=VMEM_LIMIT)


def _sigmoid(x):
    return 1.0 / (1.0 + jnp.exp(-x))


def _silu(z):
    return z * _sigmoid(z)


def _dsilu(z):
    s = _sigmoid(z)
    return s * (1.0 + z * (1.0 - s))


_GELU_C = math.sqrt(2.0 / math.pi)


def _gelu(x):
    return 0.5 * x * (1.0 + jnp.tanh(_GELU_C * (x + 0.044715 * (x * x * x))))


def _dgelu(x):
    t = jnp.tanh(_GELU_C * (x + 0.044715 * (x * x * x)))
    return 0.5 * (1.0 + t) + 0.5 * x * (1.0 - t * t) * (_GELU_C * (1.0 + 3.0 * 0.044715 * x * x))


def _log_sigmoid(x):
    return jnp.minimum(x, 0.0) - jnp.log(1.0 + jnp.exp(-jnp.abs(x)))


def _rms(x):
    return lax.rsqrt(jnp.mean(x * x, axis=-1, keepdims=True) + EPS)


def _rms_bwd(x, w, dy):
    r = _rms(x)
    xhat = x * r
    dxh = dy * w
    dx = r * (dxh - xhat * jnp.mean(dxh * xhat, axis=-1, keepdims=True))
    return dx, dy * xhat


def _rows(name, fn, ins, outs, tr, pre=None, into=None, deps=()):
    rows = None
    for arr, kind, cols, cb in ins:
        if kind == 'r':
            rows = arr.shape[0]
        elif kind == 's' and rows is None:
            rows = arr.shape[1]
    tr = _t(tr, rows)
    n_in = len(ins)
    has_acc = any(o[0] == 'a' for o in outs)

    def spec(kind, cols, cb):
        if kind == 'r':
            return pl.BlockSpec((tr, cols), lambda r, *p: (r, cb))
        if kind == 'b':
            return pl.BlockSpec((1, cols), lambda r, *p: (0, cb))
        return pl.BlockSpec((None, tr, cols), lambda r, p: (p[cb], r, 0))

    in_specs = [spec(kind, cols, cb) for _, kind, cols, cb in ins]
    out_specs, out_shape = [], []
    for o in outs:
        if o[0] == 'r':
            out_specs.append(pl.BlockSpec((tr, o[1]), lambda r, *p: (r, 0)))
            out_shape.append(jax.ShapeDtypeStruct((rows, o[1]), o[2]))
        elif o[0] == 'a':
            out_specs.append(pl.BlockSpec((1, o[1]), lambda r, *p: (0, 0)))
            out_shape.append(jax.ShapeDtypeStruct((1, o[1]), o[2]))
        else:
            blk = tuple(tr if d == 'tr' else d for d in o[3])
            out_specs.append(pl.BlockSpec(blk, o[4]))
            out_shape.append(jax.ShapeDtypeStruct(o[1], o[2]))
    n_pre = 0 if pre is None else 1
    args = [a[0] for a in ins]
    aliases = {}
    if into is not None:
        in_specs.append(ANY)
        args.append(into)
        aliases = {n_pre + n_in: 0}
    in_specs += [ANY] * len(deps)
    args += list(deps)
    n_all = len(args)

    def body(*refs):
        refs = refs[n_pre:]
        res = fn(*[r[...] for r in refs[:n_in]])
        for spec_o, o, v in zip(outs, refs[n_all:], res):
            if spec_o[0] == 'a':
                @pl.when(pl.program_id(0) == 0)
                def _():
                    o[...] = jnp.zeros_like(o)
                o[...] += v.astype(o.dtype)
            else:
                o[...] = v.astype(o.dtype)

    grid_spec = pltpu.PrefetchScalarGridSpec(num_scalar_prefetch=n_pre, grid=(rows // tr,), in_specs=in_specs, out_specs=out_specs)
    if pre is not None:
        args = [pre] + args
    return pl.pallas_call(body, name=name, grid_spec=grid_spec, out_shape=out_shape, input_output_aliases=aliases,
                          compiler_params=_params(("arbitrary" if has_acc else "parallel",)))(*args)


def _colsum(v):
    return jnp.sum(v, axis=0, keepdims=True)


def _mm(name, a, b, *, M, N, K, tm, tn, tk, a_spec, b_spec, outs, epi=None, extras=(), groups=1, ta=False, tb=False, deps=()):
    nk = K // tk
    assert M % tm == 0 and N % tn == 0 and K % tk == 0, (name, M, N, K, tm, tn, tk)
    dims = (((0 if ta else 1,), (1 if tb else 0,)), ((), ()))
    n_ex = len(extras)

    def body(*refs):
        a_ref, b_ref = refs[0], refs[1]
        ex = refs[2:2 + n_ex]
        out_refs = refs[2 + n_ex + len(deps):2 + n_ex + len(deps) + len(outs)]

        def finish(r):
            res = (r,) if epi is None else epi(r, *[e[...] for e in ex])
            for o, v in zip(out_refs, res):
                o[...] = v.astype(o.dtype)

        part = lax.dot_general(a_ref[...].astype(BF16), b_ref[...].astype(BF16), dims, preferred_element_type=F32)
        if nk == 1:
            finish(part)
            return
        acc = refs[-1]
        k = pl.program_id(3)

        @pl.when(k == 0)
        def _():
            acc[...] = part

        @pl.when(k > 0)
        def _():
            acc[...] += part

        @pl.when(k == nk - 1)
        def _():
            finish(acc[...])

    return pl.pallas_call(
        body, name=name, grid=(groups, M // tm, N // tn, nk),
        in_specs=[a_spec, b_spec] + [s for _, s in extras] + [ANY] * len(deps),
        out_specs=[s for _, _, s in outs],
        out_shape=[jax.ShapeDtypeStruct(sh, dt) for sh, dt, _ in outs],
        scratch_shapes=[] if nk == 1 else [pltpu.VMEM((tm, tn), F32)],
        compiler_params=_params(("parallel", "parallel", "parallel", "arbitrary")),
    )(a, b, *[e for e, _ in extras], *deps)


def _bs(shape, f):
    return pl.BlockSpec(shape, f)


def _tile(tm, tn, coff=0):
    return _bs((tm, tn), lambda g, m, n, k: (m, n + coff))


def _rowvec(tn, coff=0):
    return _bs((1, tn), lambda g, m, n, k: (0, n + coff))


def _mm_proj(name, xn, w, *, epi=None, extras=(), out_dtype=F32):
    T, D = xn.shape
    sw = w.shape[2]
    N = N_CHIPS * sw
    tm, tn, tk = _t(512 if extras else 1024, T), _t(1024, sw), _t(K_STEP, D)
    nb = sw // tn
    return _mm(name, xn, w, M=T, N=N, K=D, tm=tm, tn=tn, tk=tk,
               a_spec=_bs((tm, tk), lambda g, m, n, k: (m, k)),
               b_spec=_bs((None, tk, tn), lambda g, m, n, k: (n // nb, k, n % nb)),
               outs=[((T, N), out_dtype, _tile(tm, tn))], epi=epi, extras=extras)[0]


def _mm_plain(name, a, b, *, out_dtype=F32, epi=None, extras=(), outs=None, tn_pref=1024):
    M, K = a.shape
    N = b.shape[1]
    tm, tn, tk = _t(512 if extras else 1024, M), _t(tn_pref, N), _t(K_STEP, K)
    if outs is None:
        outs = [((M, N), out_dtype, _tile(tm, tn))]
    return _mm(name, a, b, M=M, N=N, K=K, tm=tm, tn=tn, tk=tk,
               a_spec=_bs((tm, tk), lambda g, m, n, k: (m, k)),
               b_spec=_bs((tk, tn), lambda g, m, n, k: (k, n)),
               outs=outs, epi=epi, extras=extras)


def _mm_rowsharded(name, a, w, *, epi, extras, outs_fn, deps=()):
    T, E = a.shape
    N = w.shape[2]
    tm, tn, tk = _t(512, T), _t(1024, N), _t(K_STEP, E)
    return _mm(name, a, w.reshape(E, N), M=T, N=N, K=E, tm=tm, tn=tn, tk=tk, deps=deps,
               a_spec=_bs((tm, tk), lambda g, m, n, k: (m, k)),
               b_spec=_bs((tk, tn), lambda g, m, n, k: (k, n)),
               outs=outs_fn(tm, tn), epi=epi, extras=extras(tm, tn))


def _mm_rowsharded_t(name, d, w, *, epi, extras, outs_fn, deps=()):
    T, N = d.shape
    tn = w.shape[1]
    E = N_CHIPS * tn
    tm, tk = _t(512, T), _t(K_STEP, N)
    return _mm(name, d, w, M=T, N=E, K=N, tm=tm, tn=tn, tk=tk, tb=True, deps=deps,
               a_spec=_bs((tm, tk), lambda g, m, n, k: (m, k)),
               b_spec=_bs((None, tn, tk), lambda g, m, n, k: (n, 0, k)),
               outs=outs_fn(tm, tn), epi=epi, extras=extras(tm, tn))


def _mm_colsharded_t(name, d, w):
    T, N = d.shape
    D, sw = w.shape[1], w.shape[2]
    tm, tn, tk = _t(1024, T), _t(1024, D), _t(1024, sw)
    kb = sw // tk
    return _mm(name, d, w, M=T, N=D, K=N, tm=tm, tn=tn, tk=tk, tb=True,
               a_spec=_bs((tm, tk), lambda g, m, n, k: (m, k)),
               b_spec=_bs((None, tn, tk), lambda g, m, n, k: (k // kb, n, k % kb)),
               outs=[((T, D), F32, _tile(tm, tn))])[0]


def _mm_dw_rows(name, a, d, deps=()):
    T, E = a.shape
    N = d.shape[1]
    tm, tn, tk = E // (2 * N_CHIPS), _t(2048, N), _t(K_STEP, T)
    return _mm(name, a, d, M=E, N=N, K=T, tm=tm, tn=tn, tk=tk, ta=True, deps=deps,
               a_spec=_bs((tk, tm), lambda g, m, n, k: (k, m)),
               b_spec=_bs((tk, tn), lambda g, m, n, k: (k, n)),
               outs=[((2, N_CHIPS, tm, N), BF16, _bs((None, None, tm, tn), lambda g, m, n, k: (m % 2, m // 2, 0, n)))])[0]


def _mm_dw_cols(name, xn, d):
    T, D = xn.shape
    N = d.shape[1]
    sw = N // N_CHIPS
    tm, tn, tk = _t(512, D // 2), _t(1024, sw), _t(K_STEP, T)
    mh, nb = (D // 2) // tm, sw // tn
    return _mm(name, xn, d, M=D, N=N, K=T, tm=tm, tn=tn, tk=tk, ta=True,
               a_spec=_bs((tk, tm), lambda g, m, n, k: (k, m)),
               b_spec=_bs((tk, tn), lambda g, m, n, k: (k, n)),
               outs=[((2, N_CHIPS, D // 2, sw), BF16,
                      _bs((None, None, tm, tn), lambda g, m, n, k: (m // mh, n // nb, m % mh, n % nb)))])[0]


def _norm_fwd(name, h, w, deps=()):
    D = h.shape[1]
    return _rows(name, lambda x, g: ((x * _rms(x)) * g,), [(h, 'r', D, 0), (w, 'b', D, 0)], [('r', D, BF16)], 256, deps=deps)[0]


def _norm_bwd(name, dxn, h, w, dh):
    D = h.shape[1]

    def fn(dy, x, g, up):
        dx, dwt = _rms_bwd(x, g, dy)
        r = up + dx
        return r, r, _colsum(dwt)

    return _rows(name, fn, [(dxn, 'r', D, 0), (h, 'r', D, 0), (w, 'b', D, 0), (dh, 'r', D, 0)],
                 [('r', D, F32), ('r', D, BF16), ('a', D, F32)], 256)


def _loss(h, target):
    D = h.shape[1]

    def fn(y, t):
        e = y - t
        d = e * (1.0 / D)
        return d, d, _colsum(e * e) * (0.5 / D)

    return _rows("loss", fn, [(h, 'r', D, 0), (target, 'r', D, 0)], [('r', D, F32), ('r', D, BF16), ('a', D, F32)], 256)


def _adamw(name, w, g, m, v):
    cols = w.shape[1]

    def fn(w, g, m, v):
        m = ADAM_B1 * m + (1.0 - ADAM_B1) * g
        v = ADAM_B2 * v + (1.0 - ADAM_B2) * (g * g)
        m_hat = m / (1.0 - ADAM_B1 ** ADAM_STEP)
        v_hat = v / (1.0 - ADAM_B2 ** ADAM_STEP)
        delta = -ADAM_LR * (m_hat / (jnp.sqrt(v_hat) + ADAM_EPS) + ADAM_WD * w)
        return delta, m, v

    rows = w.shape[0]
    if rows % SUB == 0 or rows <= 256:
        return _rows(name, fn, [(x, 'r', cols, 0) for x in (w, g, m, v)], [('r', cols, F32)] * 3, 256)
    tc = _t(256, cols)
    assert tc % LANES == 0, (rows, cols)

    def body(w_ref, g_ref, m_ref, v_ref, d_out, m_out, v_out):
        for o, r in zip((d_out, m_out, v_out), fn(w_ref[...], g_ref[...], m_ref[...], v_ref[...])):
            o[...] = r

    blk = pl.BlockSpec((rows, tc), lambda j: (0, j))
    return pl.pallas_call(body, name=name, grid=(cols // tc,), in_specs=[blk] * 4, out_specs=[blk] * 3,
                          out_shape=[jax.ShapeDtypeStruct((rows, cols), F32)] * 3, compiler_params=_params(("parallel",)))(w, g, m, v)


def _s5_disc(a_re, a_im, log_dt):
    dt = jnp.exp(log_dt)
    mag = jnp.exp(a_re * dt)
    abar_r = mag * jnp.cos(a_im * dt)
    abar_i = mag * jnp.sin(a_im * dt)
    den = a_re * a_re + a_im * a_im
    xr = abar_r - 1.0
    fr = (xr * a_re + abar_i * a_im) / den
    fi = (abar_i * a_re - xr * a_im) / den
    return abar_r, abar_i, fr, fi


def _s5_disc_fwd(name, a_re, a_im, log_dt):
    G, P = a_re.shape

    def body(ar, ai, ld, o0, o1, o2, o3):
        for o, v in zip((o0, o1, o2, o3), _s5_disc(ar[...], ai[...], ld[...])):
            o[...] = v

    return pl.pallas_call(body, name=name, out_shape=[jax.ShapeDtypeStruct((G, P), F32)] * 4)(a_re, a_im, log_dt)


def _s5_disc_bwd(name, a_re, a_im, log_dt, cts):
    G, P = a_re.shape

    def body(ar, ai, ld, c0, c1, c2, c3, d0, d1, d2):
        _, vjp = jax.vjp(_s5_disc, ar[...], ai[...], ld[...])
        g0, g1, g2 = vjp((c0[...], c1[...], c2[...], c3[...]))
        d0[...] = g0
        d1[...] = g1
        d2[...] = g2

    return pl.pallas_call(body, name=name, out_shape=[jax.ShapeDtypeStruct((G, P), F32)] * 2 + [jax.ShapeDtypeStruct((G, 1), F32)])(
        a_re, a_im, log_dt, *cts)


def _s5_bbar(name, fr, fi, br, bi):
    return _rows(name, lambda fr, fi, br, bi: (fr * br - fi * bi, fr * bi + fi * br),
                 [(fr, 'r', 1, 0), (fi, 'r', 1, 0), (br, 'r', S5_GROUP, 0), (bi, 'r', S5_GROUP, 0)],
                 [('r', S5_GROUP, F32)] * 2, 2048)


def _s5_bbar_bwd(name, fr, fi, br, bi, dr, di):
    def fn(fr, fi, br, bi, dr, di):
        return (fr * dr + fi * di, fr * di - fi * dr,
                jnp.sum(br * dr + bi * di, axis=1, keepdims=True), jnp.sum(br * di - bi * dr, axis=1, keepdims=True))

    return _rows(name, fn, [(fr, 'r', 1, 0), (fi, 'r', 1, 0)] + [(x, 'r', S5_GROUP, 0) for x in (br, bi, dr, di)],
                 [('r', S5_GROUP, F32)] * 2 + [('r', 1, F32)] * 2, 2048)


def _scan_mults(m_ref, ar, ai, reverse):
    L = ar.shape[1]
    row = lax.broadcasted_iota(jnp.int32, (SUB, L), 0)
    if reverse:
        row = (SUB - 1) - row
    ar = jnp.broadcast_to(ar, (SUB, L))
    ai = jnp.broadcast_to(ai, (SUB, L))
    a2r, a2i = ar * ar - ai * ai, 2.0 * ar * ai
    a4r, a4i = a2r * a2r - a2i * a2i, 2.0 * a2r * a2i
    zero = jnp.zeros((SUB, L), F32)
    for s, (pr, pi, d) in enumerate(((ar, ai, 1), (a2r, a2i, 2), (a4r, a4i, 4))):
        m_ref[2 * s] = jnp.where(row >= d, pr, zero)
        m_ref[2 * s + 1] = jnp.where(row >= d, pi, zero)
    pr, pi = ar, ai
    for bit, (qr, qi) in ((1, (ar, ai)), (2, (a2r, a2i)), (4, (a4r, a4i))):
        on = (row & bit) != 0
        nr, ni = pr * qr - pi * qi, pr * qi + pi * qr
        pr, pi = jnp.where(on, nr, pr), jnp.where(on, ni, pi)
    m_ref[6] = pr
    m_ref[7] = pi


def _scan8(xr, xi, m_ref, cr, ci, reverse):
    for s, d in enumerate((1, 2, 4)):
        sh = (SUB - d) if reverse else d
        sr, si = pltpu.roll(xr, sh, 0), pltpu.roll(xi, sh, 0)
        mr, mi = m_ref[2 * s], m_ref[2 * s + 1]
        xr, xi = xr + mr * sr - mi * si, xi + mr * si + mi * sr
    pr, pi = m_ref[6], m_ref[7]
    return xr + pr * cr - pi * ci, xi + pr * ci + pi * cr


def _blockdiag_fill(bd_ref, c_ref, C, L):
    P = S5_STATE
    bd_ref[...] = jnp.zeros_like(bd_ref)
    for g in range(L // P):
        for half in (0, L):
            bd_ref[g * C:(g + 1) * C, half + g * P:half + (g + 1) * P] = c_ref[:, half + g * P:half + (g + 1) * P]


def _blockdiag_take(out_ref, dense_ref, C, L):
    P = S5_STATE
    for g in range(L // P):
        for half in (0, L):
            out_ref[:, half + g * P:half + (g + 1) * P] = dense_ref[g * C:(g + 1) * C, half + g * P:half + (g + 1) * P]


def _s5_fwd(name, proj, bbd, cbd, abar_r, abar_i, dskip, E):
    T = proj.shape[0]
    NC, C, L2 = bbd.shape
    L = L2 // 2
    CH = GROUPS_PER_CHUNK * C
    tT = _t(S5_TIME_BLOCK, T)
    nt = (((1,), (1,)), ((), ()))

    def body(u_ref, bc_ref, cc_ref, ar_ref, ai_ref, d_ref, y_ref, g_ref, h_ref, bu, carry, mult, b_bd, c_bd):
        tb = pl.program_id(1)

        @pl.when(tb == 0)
        def _():
            carry[...] = jnp.zeros_like(carry)
            _blockdiag_fill(b_bd, bc_ref, C, L)
            _blockdiag_fill(c_bd, cc_ref, C, L)

        u = u_ref[...]
        bu[...] = jnp.dot(u.astype(BF16), b_bd[...], preferred_element_type=F32)
        _scan_mults(mult, ar_ref[...], ai_ref[...], False)

        def step(jb, c):
            cr, ci = c
            r0 = pl.multiple_of(jb * SUB, SUB)
            hr, hi = _scan8(bu[pl.ds(r0, SUB), 0:L], bu[pl.ds(r0, SUB), L:L2], mult, cr, ci, False)
            h_ref[pl.ds(r0, SUB), 0:L] = hr
            h_ref[pl.ds(r0, SUB), L:L2] = hi
            return (jnp.broadcast_to(hr[SUB - 1:SUB, :], (SUB, L)), jnp.broadcast_to(hi[SUB - 1:SUB, :], (SUB, L)))

        cr, ci = lax.fori_loop(0, tT // SUB, step, (carry[:, 0:L], carry[:, L:L2]))
        carry[:, 0:L] = cr
        carry[:, L:L2] = ci
        y1 = lax.dot_general(h_ref[...].astype(BF16), c_bd[...], nt, preferred_element_type=F32) + d_ref[...] * u
        y_ref[...] = y1
        g_ref[...] = _gelu(y1).astype(BF16)

    return pl.pallas_call(
        body, name=name, grid=(NC, T // tT),
        in_specs=[_bs((tT, CH), lambda c, t: (t, c)), _bs((None, C, L2), lambda c, t: (c, 0, 0)),
                  _bs((None, C, L2), lambda c, t: (c, 0, 0)), _bs((None, 1, L), lambda c, t: (c, 0, 0)),
                  _bs((None, 1, L), lambda c, t: (c, 0, 0)), _bs((1, CH), lambda c, t: (0, c))],
        out_specs=[_bs((tT, CH), lambda c, t: (t, c)), _bs((tT, CH), lambda c, t: (t, c)),
                   _bs((None, tT, L2), lambda c, t: (c, t, 0))],
        out_shape=[jax.ShapeDtypeStruct((T, E), F32), jax.ShapeDtypeStruct((T, E), BF16),
                   jax.ShapeDtypeStruct((NC, T, L2), F32)],
        scratch_shapes=[pltpu.VMEM((tT, L2), F32), pltpu.VMEM((SUB, L2), F32), pltpu.VMEM((8, SUB, L), F32),
                        pltpu.VMEM((CH, L2), BF16), pltpu.VMEM((CH, L2), BF16)],
        compiler_params=_params(("parallel", "arbitrary")),
    )(proj, bbd, cbd, abar_r, abar_i, dskip)


def _s5_bwd(name, dy1, proj, hs, bbd, cbd, abar_r, abar_i, dskip, E):
    T = proj.shape[0]
    NC, C, L2 = bbd.shape
    L = L2 // 2
    CH = GROUPS_PER_CHUNK * C
    tT = _t(S5_TIME_BLOCK, T)
    nT = T // tT
    tn = (((0,), (0,)), ((), ()))
    nt = (((1,), (1,)), ((), ()))

    def body(dy_ref, u_ref, h_ref, bc_ref, cc_ref, ar_ref, ai_ref, d_ref, du_ref, db_ref, dc_ref, da_ref, dd_ref,
             gb, carry, mult, b_bd, c_bd, db_acc, dc_acc):
        tb = pl.program_id(1)

        @pl.when(tb == 0)
        def _():
            carry[...] = jnp.zeros_like(carry)
            db_acc[...] = jnp.zeros_like(db_acc)
            dc_acc[...] = jnp.zeros_like(dc_acc)
            da_ref[...] = jnp.zeros_like(da_ref)
            dd_ref[...] = jnp.zeros_like(dd_ref)
            _blockdiag_fill(b_bd, bc_ref, C, L)
            _blockdiag_fill(c_bd, cc_ref, C, L)

        dy = dy_ref[...]
        u = u_ref[...]
        dy16 = dy.astype(BF16)
        dc_acc[...] += lax.dot_general(dy16, h_ref[...].astype(BF16), tn, preferred_element_type=F32)
        gb[...] = jnp.dot(dy16, c_bd[...], preferred_element_type=F32)
        _scan_mults(mult, ar_ref[...], -ai_ref[...], True)
        row = lax.broadcasted_iota(jnp.int32, (SUB, L), 0)
        nblk = tT // SUB

        def step(jj, c):
            cr, ci, sr, si = c
            r0 = pl.multiple_of((nblk - 1 - jj) * SUB, SUB)
            gr, gi = _scan8(gb[pl.ds(r0, SUB), 0:L], gb[pl.ds(r0, SUB), L:L2], mult, cr, ci, True)
            gb[pl.ds(r0, SUB), 0:L] = gr
            gb[pl.ds(r0, SUB), L:L2] = gi
            nr = jnp.where(row == SUB - 1, cr, pltpu.roll(gr, SUB - 1, 0))
            ni = jnp.where(row == SUB - 1, ci, pltpu.roll(gi, SUB - 1, 0))
            hr, hi = h_ref[pl.ds(r0, SUB), 0:L], h_ref[pl.ds(r0, SUB), L:L2]
            sr = sr + nr * hr + ni * hi
            si = si + ni * hr - nr * hi
            return (jnp.broadcast_to(gr[0:1, :], (SUB, L)), jnp.broadcast_to(gi[0:1, :], (SUB, L)), sr, si)

        z = jnp.zeros((SUB, L), F32)
        cr, ci, sr, si = lax.fori_loop(0, nblk, step, (carry[:, 0:L], carry[:, L:L2], z, z))
        carry[:, 0:L] = cr
        carry[:, L:L2] = ci
        da_ref[:, 0:L] += sr
        da_ref[:, L:L2] += si
        g16 = gb[...].astype(BF16)
        du = lax.dot_general(g16, b_bd[...], nt, preferred_element_type=F32) + d_ref[...] * dy
        du_ref[...] = du.astype(BF16)
        db_acc[...] += lax.dot_general(u.astype(BF16), g16, tn, preferred_element_type=F32)
        dd_ref[...] += _colsum(dy * u)

        @pl.when(tb == nT - 1)
        def _():
            _blockdiag_take(db_ref, db_acc, C, L)
            _blockdiag_take(dc_ref, dc_acc, C, L)

    rev = lambda c, t: (nT - 1 - t, c)
    return pl.pallas_call(
        body, name=name, grid=(NC, nT),
        in_specs=[_bs((tT, CH), rev), _bs((tT, CH), rev), _bs((None, tT, L2), lambda c, t: (c, nT - 1 - t, 0)),
                  _bs((None, C, L2), lambda c, t: (c, 0, 0)), _bs((None, C, L2), lambda c, t: (c, 0, 0)),
                  _bs((None, 1, L), lambda c, t: (c, 0, 0)), _bs((None, 1, L), lambda c, t: (c, 0, 0)),
                  _bs((1, CH), lambda c, t: (0, c))],
        out_specs=[_bs((tT, CH), rev), _bs((None, C, L2), lambda c, t: (c, 0, 0)), _bs((None, C, L2), lambda c, t: (c, 0, 0)),
                   _bs((None, SUB, L2), lambda c, t: (c, 0, 0)), _bs((None, 1, CH), lambda c, t: (c, 0, 0))],
        out_shape=[jax.ShapeDtypeStruct((T, E), BF16), jax.ShapeDtypeStruct((NC, C, L2), F32),
                   jax.ShapeDtypeStruct((NC, C, L2), F32), jax.ShapeDtypeStruct((NC, SUB, L2), F32),
                   jax.ShapeDtypeStruct((NC, 1, CH), F32)],
        scratch_shapes=[pltpu.VMEM((tT, L2), F32), pltpu.VMEM((SUB, L2), F32), pltpu.VMEM((8, SUB, L), F32),
                        pltpu.VMEM((CH, L2), BF16), pltpu.VMEM((CH, L2), BF16), pltpu.VMEM((CH, L2), F32), pltpu.VMEM((CH, L2), F32)],
        compiler_params=_params(("parallel", "arbitrary")),
    )(dy1, proj, hs, bbd, cbd, abar_r, abar_i, dskip)


def _compact(v, NC):
    G, P, C = v.shape
    return jnp.transpose(v.reshape(NC, G // NC, P, C), (0, 3, 1, 2)).reshape(NC, C, (G // NC) * P)


def _uncompact(d, G):
    NC, C, L = d.shape
    gpc = G // NC
    return jnp.transpose(d.reshape(NC, C, gpc, L // gpc), (0, 2, 3, 1)).reshape(G, L // gpc, C)


def _cum_rows(name, x, bias, reverse, log_sig):
    T, L = x.shape

    def body(x_ref, b_ref, o_ref):
        row = lax.broadcasted_iota(jnp.int32, (SUB, L), 0)
        if reverse:
            row = (SUB - 1) - row
        nblk = T // SUB

        def step(jj, c):
            r0 = pl.multiple_of(((nblk - 1 - jj) if reverse else jj) * SUB, SUB)
            v = x_ref[pl.ds(r0, SUB), :] + b_ref[...]
            if log_sig:
                v = _log_sigmoid(v)
            for d in (1, 2, 4):
                v = v + jnp.where(row >= d, pltpu.roll(v, (SUB - d) if reverse else d, 0), 0.0)
            v = v + c
            o_ref[pl.ds(r0, SUB), :] = v
            e = 0 if reverse else SUB - 1
            return jnp.broadcast_to(v[e:e + 1, :], (SUB, L))

        lax.fori_loop(0, nblk, step, jnp.zeros((SUB, L), F32))

    return pl.pallas_call(body, name=name, out_shape=jax.ShapeDtypeStruct((T, L), F32),
                          compiler_params=pltpu.CompilerParams(vmem_limit_bytes=VMEM_LIMIT))(x, bias)


def _qk_norm(name, proj, wq, wk, H):
    T = proj.shape[0]
    Dh = FOX_HEAD_DIM
    tT = _t(512, T)
    HB = math.gcd(NORM_HEADS, H)

    def body(q_ref, k_ref, wq_ref, wk_ref, qn_ref, kn_ref):
        for hh in range(HB):
            lanes = slice(hh * Dh, (hh + 1) * Dh)
            q, k = q_ref[:, lanes], k_ref[:, lanes]
            qn_ref[:, lanes] = ((q * _rms(q)) * wq_ref[...]).astype(BF16)
            kn_ref[:, lanes] = ((k * _rms(k)) * wk_ref[...]).astype(BF16)

    blk = lambda off: _bs((tT, HB * Dh), lambda t, h: (t, h + off))
    return pl.pallas_call(
        body, name=name, grid=(T // tT, H // HB),
        in_specs=[blk(0), blk(H // HB), _bs((1, Dh), lambda t, h: (0, 0)), _bs((1, Dh), lambda t, h: (0, 0))],
        out_specs=[blk(0), blk(0)], out_shape=[jax.ShapeDtypeStruct((T, H * Dh), BF16)] * 2,
        compiler_params=_params(("parallel", "parallel")))(proj, proj, wq, wk)


def _qk_norm_bwd(name, proj, wq, wk, dqn, dkn, H):
    T = proj.shape[0]
    Dh = FOX_HEAD_DIM
    tT = _t(512, T)
    HB = math.gcd(NORM_HEADS, H)

    def body(q_ref, k_ref, wq_ref, wk_ref, dqn_ref, dkn_ref, dq_ref, dk_ref, dwq_ref, dwk_ref):
        @pl.when((pl.program_id(0) == 0) & (pl.program_id(1) == 0))
        def _():
            dwq_ref[...] = jnp.zeros_like(dwq_ref)
            dwk_ref[...] = jnp.zeros_like(dwk_ref)

        for hh in range(HB):
            lanes = slice(hh * Dh, (hh + 1) * Dh)
            dq, tq = _rms_bwd(q_ref[:, lanes], wq_ref[...], dqn_ref[:, lanes])
            dk, tk = _rms_bwd(k_ref[:, lanes], wk_ref[...], dkn_ref[:, lanes])
            dq_ref[:, lanes] = dq.astype(BF16)
            dk_ref[:, lanes] = dk.astype(BF16)
            dwq_ref[...] += _colsum(tq)
            dwk_ref[...] += _colsum(tk)

    blk = lambda off: _bs((tT, HB * Dh), lambda t, h: (t, h + off))
    one = _bs((1, Dh), lambda t, h: (0, 0))
    return pl.pallas_call(
        body, name=name, grid=(T // tT, H // HB),
        in_specs=[blk(0), blk(H // HB), one, one, blk(0), blk(0)],
        out_specs=[blk(0), blk(0), one, one],
        out_shape=[jax.ShapeDtypeStruct((T, H * Dh), BF16)] * 2 + [jax.ShapeDtypeStruct((1, Dh), F32)] * 2,
        compiler_params=_params(("arbitrary", "arbitrary")))(proj, proj, wq, wk, dqn, dkn)


def _attn_fwd(name, qn, kn, proj, cum_q, cum_k, H):
    T = qn.shape[0]
    Dh = FOX_HEAD_DIM
    tq = cum_k.shape[3]
    nq = T // tq
    scale = Dh ** -0.5
    nt = (((1,), (1,)), ((), ()))

    sq = _t(ATTN_SUB, tq)
    rep = tq // LANES
    HP = ATTN_HEADS
    assert H % HP == 0 and Dh == LANES

    def body(q_ref, k_ref, v_ref, cq_ref, ck_ref, o_ref, lse_ref, m_sc, l_sc, acc_sc):
        i = pl.program_id(1)
        m_sc[...] = jnp.full_like(m_sc, NEG)
        l_sc[...] = jnp.zeros_like(l_sc)
        acc_sc[...] = jnp.zeros_like(acc_sc)
        kloc = lax.broadcasted_iota(jnp.int32, (sq, tq), 1)
        qloc = lax.broadcasted_iota(jnp.int32, (sq, tq), 0)

        def chunk(kc, masked):
            ks = pl.multiple_of(kc * tq, tq)
            for hh in range(HP):
                lanes = slice(hh * Dh, (hh + 1) * Dh)
                k = k_ref[pl.ds(ks, tq), lanes]
                v16 = v_ref[pl.ds(ks, tq), lanes].astype(BF16)
                ck = ck_ref[hh, kc]
                for r in range(tq // sq):
                    rows = pl.ds(r * sq, sq)
                    s = lax.dot_general(q_ref[rows, lanes], k, nt, preferred_element_type=F32) * scale + (jnp.tile(cq_ref[hh, rows, :], (1, rep)) - ck)
                    if masked:
                        s = jnp.where(kloc <= qloc + r * sq, s, NEG)
                    m_old = m_sc[rows, lanes]
                    m_new = jnp.maximum(m_old, jnp.max(s, axis=1, keepdims=True))
                    alpha = jnp.exp(m_old - m_new)
                    p = jnp.exp(s - jnp.tile(m_new, (1, rep)))
                    l_sc[rows, lanes] = alpha * l_sc[rows, lanes] + jnp.sum(p, axis=1, keepdims=True)
                    acc_sc[rows, lanes] = alpha * acc_sc[rows, lanes] + jnp.dot(p.astype(BF16), v16, preferred_element_type=F32)
                    m_sc[rows, lanes] = m_new

        def below(kc, c):
            chunk(kc, False)
            return c

        lax.fori_loop(0, i, below, 0)
        chunk(i, True)
        o_ref[...] = acc_sc[...] / l_sc[...]
        for hh in range(HP):
            lanes = slice(hh * Dh, (hh + 1) * Dh)
            lse_ref[hh] = m_sc[:, lanes] + jnp.log(l_sc[:, lanes])

    W2 = HP * Dh
    return pl.pallas_call(
        body, name=name, grid=(H // HP, nq),
        in_specs=[_bs((tq, W2), lambda h, i: (i, h)), _bs((T, W2), lambda h, i: (0, h)), _bs((T, W2), lambda h, i: (0, 2 * (H // HP) + h)),
                  _bs((HP, tq, LANES), lambda h, i: (h, i, 0)), _bs((HP, nq, 1, tq), lambda h, i: (h, 0, 0, 0))],
        out_specs=[_bs((tq, W2), lambda h, i: (i, h)), _bs((HP, tq, LANES), lambda h, i: (h, i, 0))],
        out_shape=[jax.ShapeDtypeStruct((T, H * Dh), F32), jax.ShapeDtypeStruct((H, T, LANES), F32)],
        scratch_shapes=[pltpu.VMEM((tq, W2), F32), pltpu.VMEM((tq, W2), F32), pltpu.VMEM((tq, W2), F32)],
        compiler_params=_params(("parallel", "parallel")))(qn, kn, proj, cum_q, cum_k)


def _attn_bwd(name, qn, kn, proj, do, o, lse, cum_q, cum_k, H):
    T = qn.shape[0]
    Dh = FOX_HEAD_DIM
    tq = cum_k.shape[3]
    nq = T // tq
    scale = Dh ** -0.5
    nt = (((1,), (1,)), ((), ()))
    tn = (((0,), (0,)), ((), ()))
    assert H <= LANES

    sq = _t(ATTN_SUB, tq)
    rep = tq // LANES
    HP = ATTN_HEADS
    W2 = HP * Dh
    assert H % HP == 0 and Dh == LANES

    def body(q_ref, k_ref, v_ref, do_ref, o_ref, lse_ref, cq_ref, ck_ref, dq_ref, dk_ref, dv_ref, dcq_ref, dck_ref,
             delta, cql, dk_sc, dv_sc, dck_sc):
        h, j = pl.program_id(0), pl.program_id(1)

        @pl.when((h == 0) & (j == 0))
        def _():
            dcq_ref[...] = jnp.zeros_like(dcq_ref)

        @pl.when(j == 0)
        def _():
            dq_ref[...] = jnp.zeros_like(dq_ref)
            for hh in range(HP):
                lanes = slice(hh * Dh, (hh + 1) * Dh)
                delta[hh] = jnp.broadcast_to(jnp.sum(do_ref[:, lanes] * o_ref[:, lanes], axis=1, keepdims=True), (T, LANES))
            cql[...] = cq_ref[...] - lse_ref[...]

        lane_id = lax.broadcasted_iota(jnp.int32, (sq, LANES), 1)
        dk_sc[...] = jnp.zeros_like(dk_sc)
        dv_sc[...] = jnp.zeros_like(dv_sc)
        dck_sc[...] = jnp.zeros_like(dck_sc)
        kloc = lax.broadcasted_iota(jnp.int32, (sq, tq), 1)
        qloc = lax.broadcasted_iota(jnp.int32, (sq, tq), 0)

        def qblk(i, masked):
            for hh in range(HP):
                lanes = slice(hh * Dh, (hh + 1) * Dh)
                k = k_ref[:, lanes]
                v16 = v_ref[:, lanes].astype(BF16)
                ck = ck_ref[hh]
                for r in range(tq // sq):
                    rows = pl.ds(pl.multiple_of(i * tq + r * sq, sq), sq)
                    q = q_ref[rows, lanes]
                    do16 = do_ref[rows, lanes].astype(BF16)
                    e = lax.dot_general(q, k, nt, preferred_element_type=F32) * scale + (jnp.tile(cql[hh, rows, :], (1, rep)) - ck)
                    p = jnp.exp(e)
                    if masked:
                        p = jnp.where(kloc <= qloc + r * sq, p, 0.0)
                    dv_sc[:, lanes] += lax.dot_general(p.astype(BF16), do16, tn, preferred_element_type=F32)
                    dp = lax.dot_general(do16, v16, nt, preferred_element_type=F32)
                    ds = p * (dp - jnp.tile(delta[hh, rows, :], (1, rep)))
                    ds16 = ds.astype(BF16)
                    dk_sc[:, lanes] += lax.dot_general(ds16, q, tn, preferred_element_type=F32)
                    dq_ref[rows, lanes] += jnp.dot(ds16, k, preferred_element_type=F32) * scale
                    dcq_ref[rows, :] += jnp.where(lane_id == h * HP + hh, jnp.sum(ds, axis=1, keepdims=True), 0.0)
                    dck_sc[hh] += jnp.sum(ds, axis=0, keepdims=True)

        def above(i, c):
            qblk(i, False)
            return c

        qblk(j, True)
        lax.fori_loop(j + 1, nq, above, 0)
        dk_ref[...] = dk_sc[...] * scale
        dv_ref[...] = dv_sc[...].astype(BF16)
        for hh in range(HP):
            dck_ref[hh] = -dck_sc[hh]

    whole = lambda off: _bs((T, W2), lambda h, j: (0, h + off))
    blk = lambda off: _bs((tq, W2), lambda h, j: (j, h + off))
    return pl.pallas_call(
        body, name=name, grid=(H // HP, nq),
        in_specs=[whole(0), blk(0), blk(2 * (H // HP)), whole(0), whole(0), _bs((HP, T, LANES), lambda h, j: (h, 0, 0)),
                  _bs((HP, T, LANES), lambda h, j: (h, 0, 0)), _bs((HP, None, 1, tq), lambda h, j: (h, j, 0, 0))],
        out_specs=[whole(0), blk(0), blk(0), _bs((T, LANES), lambda h, j: (0, 0)),
                   _bs((HP, None, 1, tq), lambda h, j: (h, j, 0, 0))],
        out_shape=[jax.ShapeDtypeStruct((T, H * Dh), F32), jax.ShapeDtypeStruct((T, H * Dh), F32), jax.ShapeDtypeStruct((T, H * Dh), BF16),
                   jax.ShapeDtypeStruct((T, LANES), F32), jax.ShapeDtypeStruct((H, nq, 1, tq), F32)],
        scratch_shapes=[pltpu.VMEM((HP, T, LANES), F32), pltpu.VMEM((HP, T, LANES), F32), pltpu.VMEM((tq, W2), F32), pltpu.VMEM((tq, W2), F32),
                        pltpu.VMEM((HP, 1, tq), F32)],
        compiler_params=_params(("arbitrary", "arbitrary")))(qn, kn, proj, do, o, lse, cum_q, cum_k)


def _pool_fwd(name, proj, E):
    T = proj.shape[0]
    PG = len(POOL_WINDOWS)
    PD = E // PG
    tT = _t(256, T)
    hb = tT // POOL_HALO

    def body(u_ref, halo_ref, o_ref, buf):
        g, tb = pl.program_id(0), pl.program_id(1)
        u = u_ref[...]
        buf[pl.ds(POOL_HALO, tT), :] = u
        buf[pl.ds(0, POOL_HALO), :] = jnp.where(tb == 0, 0.0, halo_ref[...])
        t = tb * tT + lax.broadcasted_iota(jnp.int32, (tT, 1), 0)
        for gi, w in enumerate(POOL_WINDOWS):
            @pl.when(g == gi)
            def _():
                acc = u
                for d in range(1, w):
                    acc = acc + buf[pl.ds(POOL_HALO - d, tT), :]
                cnt = jnp.minimum(t + 1, w).astype(F32)
                o_ref[...] = (acc / cnt - u).astype(BF16)

    return pl.pallas_call(
        body, name=name, grid=(PG, T // tT),
        in_specs=[_bs((tT, PD), lambda g, t: (t, g)), _bs((POOL_HALO, PD), lambda g, t: (jnp.maximum(t * hb - 1, 0), g))],
        out_specs=_bs((tT, PD), lambda g, t: (t, g)), out_shape=jax.ShapeDtypeStruct((T, E), BF16),
        scratch_shapes=[pltpu.VMEM((tT + POOL_HALO, PD), F32)],
        compiler_params=_params(("parallel", "parallel")))(proj, proj)


def _pool_bwd(name, dpm, E):
    T = dpm.shape[0]
    PG = len(POOL_WINDOWS)
    PD = E // PG
    tT = _t(256, T)
    hb = tT // POOL_HALO
    nT = T // tT

    def body(d_ref, halo_ref, o_ref, buf):
        g, tb = pl.program_id(0), pl.program_id(1)
        d = d_ref[...]
        t = tb * tT + lax.broadcasted_iota(jnp.int32, (tT, 1), 0)
        th = (tb + 1) * tT + lax.broadcasted_iota(jnp.int32, (POOL_HALO, 1), 0)
        for gi, w in enumerate(POOL_WINDOWS):
            @pl.when(g == gi)
            def _():
                dn = d / jnp.minimum(t + 1, w).astype(F32)
                buf[pl.ds(0, tT), :] = dn
                buf[pl.ds(tT, POOL_HALO), :] = jnp.where(tb == nT - 1, 0.0, halo_ref[...] / jnp.minimum(th + 1, w).astype(F32))
                acc = dn
                for s in range(1, w):
                    acc = acc + buf[pl.ds(s, tT), :]
                o_ref[...] = (acc - d).astype(BF16)

    return pl.pallas_call(
        body, name=name, grid=(PG, nT),
        in_specs=[_bs((tT, PD), lambda g, t: (t, g)), _bs((POOL_HALO, PD), lambda g, t: (jnp.minimum((t + 1) * hb, T // POOL_HALO - 1), g))],
        out_specs=_bs((tT, PD), lambda g, t: (t, g)), out_shape=jax.ShapeDtypeStruct((T, E), BF16),
        scratch_shapes=[pltpu.VMEM((tT + POOL_HALO, PD), F32)],
        compiler_params=_params(("parallel", "parallel")))(dpm, dpm)


def _coords():
    x, y, c = lax.axis_index("x"), lax.axis_index("y"), lax.axis_index("c")
    chips = [(1 - x, y), (x, 1 - y), (1 - x, 1 - y)]
    return x, y, c, 2 * x + y, (x, y, 1 - c), chips


def _chip_allgather(name, bufs):
    n = len(bufs)

    def body(*refs):
        outs = refs[n:2 * n]
        send, recv, fsend, frecv = refs[2 * n:]
        x, y, c, p, sib, chips = _coords()

        def direct(t, j, chip):
            return pltpu.make_async_remote_copy(src_ref=outs[t].at[p, c], dst_ref=outs[t].at[p, c], send_sem=send.at[t, j],
                                                recv_sem=recv.at[t, j], device_id=(*chip, c), device_id_type=MESH)

        def landed(t, j, chip):
            blk = outs[t].at[2 * chip[0] + chip[1], c]
            return pltpu.make_async_remote_copy(src_ref=blk, dst_ref=blk, send_sem=send.at[t, j],
                                                recv_sem=recv.at[t, j], device_id=(*chip, c), device_id_type=MESH)

        def passed(t, j, chip, half):
            blk = outs[t].at[2 * chip[0] + chip[1], half]
            return pltpu.make_async_remote_copy(src_ref=blk, dst_ref=blk, send_sem=fsend.at[t, j], recv_sem=frecv.at[t, j],
                                                device_id=sib, device_id_type=MESH)

        first = [direct(t, j, chip) for t in range(n) for j, chip in enumerate(chips)]
        for cp in first:
            cp.start()
        fwd = []
        for j, chip in enumerate(chips):
            for t in range(n):
                landed(t, j, chip).wait_recv()
                f = passed(t, j, chip, c)
                f.start()
                fwd.append(f)
        for j, chip in enumerate(chips):
            for t in range(n):
                passed(t, j, chip, 1 - c).wait_recv()
        for cp in first + fwd:
            cp.wait_send()

    return pl.pallas_call(
        body, name=name, in_specs=[ANY] * n, out_specs=[ANY] * n,
        out_shape=[jax.ShapeDtypeStruct(a.shape, a.dtype) for a in bufs],
        input_output_aliases={t: t for t in range(n)},
        scratch_shapes=[pltpu.SemaphoreType.DMA((n, 3))] * 4,
    )(*bufs)


SEM = pl.BlockSpec(memory_space=pltpu.SEMAPHORE)
TOKEN = jax.ShapeDtypeStruct((SUB, LANES), F32)


def _split_params():
    return pltpu.CompilerParams(has_side_effects=pltpu.SideEffectType.DATAFLOW_SIDE_EFFECTING)


def _struct(a):
    return jax.ShapeDtypeStruct(a.shape, a.dtype)


def _gather_start(name, bufs, deps):
    n, nd = len(bufs), len(deps)

    def body(*refs):
        outs = refs[n + nd:2 * n + nd]
        send, recv, token = refs[2 * n + nd:]
        x, y, c, p, sib, chips = _coords()
        for t in range(n):
            for j, chip in enumerate(chips):
                pltpu.make_async_remote_copy(src_ref=outs[t].at[p, c], dst_ref=outs[t].at[p, c], send_sem=send.at[3 * t + j],
                                             recv_sem=recv.at[3 * t + j], device_id=(*chip, c), device_id_type=MESH).start()
        token[...] = jnp.zeros_like(token)

    res = pl.pallas_call(
        body, name=name, in_specs=[ANY] * (n + nd), out_specs=[ANY] * n + [SEM, SEM, pl.BlockSpec(memory_space=pltpu.VMEM)],
        out_shape=[_struct(a) for a in bufs] + [pltpu.SemaphoreType.DMA((3 * n,)), pltpu.SemaphoreType.DMA((3 * n,)), TOKEN],
        input_output_aliases={t: t for t in range(n)}, compiler_params=_split_params(),
    )(*bufs, *deps)
    return list(res[:n]), res[n], res[n + 1], res[n + 2]


def _gather_wait(name, bufs, send, recv, after):
    n = len(bufs)

    def body(*refs):
        send_r, recv_r = refs[n], refs[n + 1]
        outs = refs[n + 3:2 * n + 3]
        x, y, c, p, sib, chips = _coords()
        for t in range(n):
            for j, chip in enumerate(chips):
                cp = pltpu.make_async_remote_copy(src_ref=outs[t].at[p, c], dst_ref=outs[t].at[2 * chip[0] + chip[1], c], send_sem=send_r.at[3 * t + j],
                                                  recv_sem=recv_r.at[3 * t + j], device_id=(*chip, c), device_id_type=MESH)
                cp.wait_send()
                cp.wait_recv()

    return list(pl.pallas_call(
        body, name=name, in_specs=[ANY] * n + [SEM, SEM, ANY], out_specs=[ANY] * n, out_shape=[_struct(a) for a in bufs],
        input_output_aliases={t: t for t in range(n)}, compiler_params=_split_params(),
    )(*bufs, send, recv, after))


def _gather_forward(name, bufs):
    n = len(bufs)

    def body(*refs):
        outs = refs[n:2 * n]
        fsend, frecv = refs[2 * n:]
        x, y, c, p, sib, chips = _coords()

        def passed(t, j, chip, half):
            blk = outs[t].at[2 * chip[0] + chip[1], half]
            return pltpu.make_async_remote_copy(src_ref=blk, dst_ref=blk, send_sem=fsend.at[t, j], recv_sem=frecv.at[t, j],
                                                device_id=sib, device_id_type=MESH)

        fwd = [passed(t, j, chip, c) for t in range(n) for j, chip in enumerate(chips)]
        for cp in fwd:
            cp.start()
        for t in range(n):
            for j, chip in enumerate(chips):
                passed(t, j, chip, 1 - c).wait_recv()
        for cp in fwd:
            cp.wait_send()

    return list(pl.pallas_call(
        body, name=name, in_specs=[ANY] * n, out_specs=[ANY] * n, out_shape=[_struct(a) for a in bufs],
        input_output_aliases={t: t for t in range(n)}, scratch_shapes=[pltpu.SemaphoreType.DMA((n, 3))] * 2,
    )(*bufs))


def _relations():
    x, y, c = lax.axis_index("x"), lax.axis_index("y"), lax.axis_index("c")
    out = []
    for code in range(1, 8):
        tx = 1 - x if code & 4 else x
        ty = 1 - y if code & 2 else y
        tc = 1 - c if code & 1 else c
        out.append((code - 1, (tx, ty, tc), 2 * tx + ty, tc))
    return out


def _full_exchange_start(name, parts):
    n = len(parts)
    lands = [lax.empty((7,) + a.shape[2:], a.dtype) for a in parts]

    def body(*refs):
        src, dst = refs[2 * n:3 * n], refs[3 * n:4 * n]
        send, recv, token = refs[4 * n:]
        for t in range(n):
            for k, dev, q, half in _relations():
                pltpu.make_async_remote_copy(src_ref=src[t].at[half, q], dst_ref=dst[t].at[k], send_sem=send.at[7 * t + k],
                                             recv_sem=recv.at[7 * t + k], device_id=dev, device_id_type=MESH).start()
        token[...] = jnp.zeros_like(token)

    res = pl.pallas_call(
        body, name=name, in_specs=[ANY] * (2 * n), out_specs=[ANY] * (2 * n) + [SEM, SEM, pl.BlockSpec(memory_space=pltpu.VMEM)],
        out_shape=[_struct(a) for a in parts + lands] + [pltpu.SemaphoreType.DMA((7 * n,)), pltpu.SemaphoreType.DMA((7 * n,)), TOKEN],
        input_output_aliases={t: t for t in range(2 * n)}, compiler_params=_split_params(),
    )(*parts, *lands)
    return list(res[:n]), list(res[n:2 * n]), res[2 * n], res[2 * n + 1], res[2 * n + 2]


def _full_exchange_wait(name, parts, lands, send, recv, after):
    n = len(parts)

    def body(*refs):
        send_r, recv_r = refs[2 * n], refs[2 * n + 1]
        src, dst = refs[2 * n + 3:3 * n + 3], refs[3 * n + 3:4 * n + 3]
        for t in range(n):
            for k, dev, q, half in _relations():
                cp = pltpu.make_async_remote_copy(src_ref=src[t].at[half, q], dst_ref=dst[t].at[k], send_sem=send_r.at[7 * t + k],
                                                  recv_sem=recv_r.at[7 * t + k], device_id=dev, device_id_type=MESH)
                cp.wait_send()
                cp.wait_recv()

    res = pl.pallas_call(
        body, name=name, in_specs=[ANY] * (2 * n) + [SEM, SEM, ANY], out_specs=[ANY] * (2 * n),
        out_shape=[_struct(a) for a in parts + lands], input_output_aliases={t: t for t in range(2 * n)},
        compiler_params=_split_params(),
    )(*parts, *lands, send, recv, after)
    return list(res[:n]), list(res[n:])


def _chip_exchange_start(name, sums):
    n = len(sums)
    lands = [lax.empty((3,) + a.shape[1:], a.dtype) for a in sums]

    def body(*refs):
        src, dst = refs[2 * n:3 * n], refs[3 * n:4 * n]
        send, recv, token = refs[4 * n:]
        x, y, c, p, sib, chips = _coords()
        for t in range(n):
            for j, chip in enumerate(chips):
                pltpu.make_async_remote_copy(src_ref=src[t].at[2 * chip[0] + chip[1]], dst_ref=dst[t].at[j], send_sem=send.at[3 * t + j],
                                             recv_sem=recv.at[3 * t + j], device_id=(*chip, c), device_id_type=MESH).start()
        token[...] = jnp.zeros_like(token)

    res = pl.pallas_call(
        body, name=name, in_specs=[ANY] * (2 * n), out_specs=[ANY] * (2 * n) + [SEM, SEM, pl.BlockSpec(memory_space=pltpu.VMEM)],
        out_shape=[_struct(a) for a in sums + lands] + [pltpu.SemaphoreType.DMA((3 * n,)), pltpu.SemaphoreType.DMA((3 * n,)), TOKEN],
        input_output_aliases={t: t for t in range(2 * n)}, compiler_params=_split_params(),
    )(*sums, *lands)
    return list(res[:n]), list(res[n:2 * n]), res[2 * n], res[2 * n + 1], res[2 * n + 2]


def _chip_exchange_wait(name, sums, lands, send, recv, after):
    n = len(sums)

    def body(*refs):
        send_r, recv_r = refs[2 * n], refs[2 * n + 1]
        src, dst = refs[2 * n + 3:3 * n + 3], refs[3 * n + 3:4 * n + 3]
        x, y, c, p, sib, chips = _coords()
        for t in range(n):
            for j, chip in enumerate(chips):
                cp = pltpu.make_async_remote_copy(src_ref=src[t].at[2 * chip[0] + chip[1]], dst_ref=dst[t].at[j], send_sem=send_r.at[3 * t + j],
                                                  recv_sem=recv_r.at[3 * t + j], device_id=(*chip, c), device_id_type=MESH)
                cp.wait_send()
                cp.wait_recv()

    res = pl.pallas_call(
        body, name=name, in_specs=[ANY] * (2 * n) + [SEM, SEM, ANY], out_specs=[ANY] * (2 * n),
        out_shape=[_struct(a) for a in sums + lands], input_output_aliases={t: t for t in range(2 * n)},
        compiler_params=_split_params(),
    )(*sums, *lands, send, recv, after)
    return list(res[:n]), list(res[n:])


def _pair_exchange(name, parts):
    n = len(parts)

    def body(*refs):
        ins, outs = refs[:n], refs[n:2 * n]
        send, recv = refs[2 * n:]
        x, y, c, p, sib, chips = _coords()
        cps = [pltpu.make_async_remote_copy(src_ref=ins[t].at[1 - c], dst_ref=outs[t], send_sem=send.at[t], recv_sem=recv.at[t],
                                            device_id=sib, device_id_type=MESH) for t in range(n)]
        for cp in cps:
            cp.start()
        for cp in cps:
            cp.wait()

    return pl.pallas_call(
        body, name=name, in_specs=[ANY] * n, out_specs=[ANY] * n,
        out_shape=[jax.ShapeDtypeStruct(a.shape[1:], a.dtype) for a in parts],
        scratch_shapes=[pltpu.SemaphoreType.DMA((n,))] * 2,
    )(*parts)


def _chip_exchange(name, sums):
    n = len(sums)

    def body(*refs):
        ins, outs = refs[:n], refs[n:2 * n]
        send, recv = refs[2 * n:]
        x, y, c, p, sib, chips = _coords()
        cps = [pltpu.make_async_remote_copy(src_ref=ins[t].at[2 * chip[0] + chip[1]], dst_ref=outs[t].at[j], send_sem=send.at[t, j],
                                            recv_sem=recv.at[t, j], device_id=(*chip, c), device_id_type=MESH)
               for t in range(n) for j, chip in enumerate(chips)]
        for cp in cps:
            cp.start()
        for cp in cps:
            cp.wait()

    return pl.pallas_call(
        body, name=name, in_specs=[ANY] * n, out_specs=[ANY] * n,
        out_shape=[jax.ShapeDtypeStruct((3,) + a.shape[1:], a.dtype) for a in sums],
        scratch_shapes=[pltpu.SemaphoreType.DMA((n, 3))] * 2,
    )(*sums)


def _pair_share(name, bufs, items, deps=()):
    n = len(items)
    nb = len(bufs)
    nd = len(deps)

    def body(*refs):
        outs = refs[nb + nd:2 * nb + nd]
        send, recv = refs[2 * nb + nd:]
        x, y, c, p, sib, chips = _coords()

        def blk(t, half):
            o, lead = items[t]
            return outs[o].at[p if lead == 'chip' else lead, half]

        def swap(t, half):
            return pltpu.make_async_remote_copy(src_ref=blk(t, half), dst_ref=blk(t, half), send_sem=send.at[t], recv_sem=recv.at[t],
                                                device_id=sib, device_id_type=MESH)

        cps = [swap(t, c) for t in range(n)]
        for cp in cps:
            cp.start()
        for t in range(n):
            swap(t, 1 - c).wait_recv()
        for cp in cps:
            cp.wait_send()

    return list(pl.pallas_call(
        body, name=name, in_specs=[ANY] * (nb + nd), out_specs=[ANY] * nb,
        out_shape=[jax.ShapeDtypeStruct(b.shape, b.dtype) for b in bufs],
        input_output_aliases={t: t for t in range(nb)},
        scratch_shapes=[pltpu.SemaphoreType.DMA((n,))] * 2,
    )(*bufs, *deps))


def _flat2(a, lead):
    return a.reshape(a.shape[:lead] + (-1, a.shape[-1]))


def _reduce_begin(tag, parts):
    parts, lands, send, recv, token = _full_exchange_start(f"rs_start_{tag}", parts)
    return (parts, lands, send, recv), token


def _reduce_end(tag, state, after, dests, bufs, buf_shapes):
    c = lax.axis_index("c").astype(jnp.int32)
    p = (2 * lax.axis_index("x") + lax.axis_index("y")).astype(jnp.int32)
    parts, lands = _full_exchange_wait(f"rs_wait_{tag}", *state, after)

    def total(a, *others):
        s = a.astype(F32)
        for b in others:
            s = s + b.astype(F32)
        return (s,)

    for t, (mine, theirs) in enumerate(zip(parts, lands)):
        o, lead = dests[t]
        shape = buf_shapes[o]
        rows, cols = shape[2], shape[3]
        m3, t3 = mine.reshape(2 * N_CHIPS, rows, cols), theirs.reshape(7, rows, cols)
        pre = jnp.stack([c * N_CHIPS + p] + [jnp.int32(k) for k in range(7)] + [c, p if lead == 'chip' else jnp.int32(lead)])
        out = ('x', shape, F32, (None, None, 'tr', cols), lambda r, pr: (pr[9], pr[8], r, 0))
        bufs[o] = _rows(f"rs_sum_{tag}_{t}", total, [(m3, 's', cols, 0)] + [(t3, 's', cols, 1 + k) for k in range(7)], [out], 256,
                        pre=pre, into=bufs[o])[0]


def kernel(x, norm_w, out_proj, s5_in_proj, s5_a_re, s5_a_im, s5_log_dt, s5_b_re, s5_b_im, s5_c_re, s5_c_im, s5_d, s5_w_glu, s5_b_glu, fox_in_proj, fox_q_norm, fox_k_norm, fox_f_bias, pool_in_proj, pool_w_group, pool_scale, loss_target, m_norm_w, m_out_proj, m_s5_in_proj, m_s5_a_re, m_s5_a_im, m_s5_log_dt, m_s5_b_re, m_s5_b_im, m_s5_c_re, m_s5_c_im, m_s5_d, m_s5_w_glu, m_s5_b_glu, m_fox_in_proj, m_fox_q_norm, m_fox_k_norm, m_fox_f_bias, m_pool_in_proj, m_pool_w_group, m_pool_scale, v_norm_w, v_out_proj, v_s5_in_proj, v_s5_a_re, v_s5_a_im, v_s5_log_dt, v_s5_b_re, v_s5_b_im, v_s5_c_re, v_s5_c_im, v_s5_d, v_s5_w_glu, v_s5_b_glu, v_fox_in_proj, v_fox_q_norm, v_fox_k_norm, v_fox_f_bias, v_pool_in_proj, v_pool_w_group, v_pool_scale):
    weights = dict(norm_w=norm_w, out_proj=out_proj, s5_in_proj=s5_in_proj, s5_a_re=s5_a_re, s5_a_im=s5_a_im, s5_log_dt=s5_log_dt,
                   s5_b_re=s5_b_re, s5_b_im=s5_b_im, s5_c_re=s5_c_re, s5_c_im=s5_c_im, s5_d=s5_d, s5_w_glu=s5_w_glu, s5_b_glu=s5_b_glu,
                   fox_in_proj=fox_in_proj, fox_q_norm=fox_q_norm, fox_k_norm=fox_k_norm, fox_f_bias=fox_f_bias,
                   pool_in_proj=pool_in_proj, pool_w_group=pool_w_group, pool_scale=pool_scale)
    mom_m = dict(norm_w=m_norm_w, out_proj=m_out_proj, s5_in_proj=m_s5_in_proj, s5_a_re=m_s5_a_re, s5_a_im=m_s5_a_im, s5_log_dt=m_s5_log_dt,
                 s5_b_re=m_s5_b_re, s5_b_im=m_s5_b_im, s5_c_re=m_s5_c_re, s5_c_im=m_s5_c_im, s5_d=m_s5_d, s5_w_glu=m_s5_w_glu, s5_b_glu=m_s5_b_glu,
                 fox_in_proj=m_fox_in_proj, fox_q_norm=m_fox_q_norm, fox_k_norm=m_fox_k_norm, fox_f_bias=m_fox_f_bias,
                 pool_in_proj=m_pool_in_proj, pool_w_group=m_pool_w_group, pool_scale=m_pool_scale)
    mom_v = dict(norm_w=v_norm_w, out_proj=v_out_proj, s5_in_proj=v_s5_in_proj, s5_a_re=v_s5_a_re, s5_a_im=v_s5_a_im, s5_log_dt=v_s5_log_dt,
                 s5_b_re=v_s5_b_re, s5_b_im=v_s5_b_im, s5_c_re=v_s5_c_re, s5_c_im=v_s5_c_im, s5_d=v_s5_d, s5_w_glu=v_s5_w_glu, s5_b_glu=v_s5_b_glu,
                 fox_in_proj=v_fox_in_proj, fox_q_norm=v_fox_q_norm, fox_k_norm=v_fox_k_norm, fox_f_bias=v_fox_f_bias,
                 pool_in_proj=v_pool_in_proj, pool_w_group=v_pool_w_group, pool_scale=v_pool_scale)
    return _step(x, loss_target, weights, mom_m, mom_v)


BIG = ('out_proj', 's5_in_proj', 's5_w_glu', 'fox_in_proj', 'pool_in_proj', 'pool_w_group')
SMALL = ('norm_w', 's5_a_re', 's5_a_im', 's5_log_dt', 's5_b_re', 's5_b_im', 's5_c_re', 's5_c_im', 's5_d', 's5_b_glu',
         'fox_q_norm', 'fox_k_norm', 'fox_f_bias', 'pool_scale')
SMALL_SHARDED = ('s5_d', 's5_b_glu', 'pool_scale')
GROUP_AXIS_1 = ('s5_a_re', 's5_a_im', 's5_b_re', 's5_b_im', 's5_c_re', 's5_c_im')
ORDER = ('norm_w', 'out_proj', 's5_in_proj', 's5_a_re', 's5_a_im', 's5_log_dt', 's5_b_re', 's5_b_im', 's5_c_re', 's5_c_im', 's5_d',
         's5_w_glu', 's5_b_glu', 'fox_in_proj', 'fox_q_norm', 'fox_k_norm', 'fox_f_bias', 'pool_in_proj', 'pool_w_group', 'pool_scale')


def _split2(shape):
    if shape[0] % 2 == 0:
        return (2, shape[0] // 2) + tuple(shape[1:])
    assert shape[0] == 1 and shape[1] % 2 == 0
    return (2, shape[1] // 2) + tuple(shape[2:])


def _adamw_big(n, w, grads, mom_m, mom_v, delta, new_m, new_v):
    shape = w[n].shape
    if shape[-1] % LANES:
        f2 = lambda a: jnp.transpose(a.reshape(-1, shape[-1]))
        b2 = lambda a: jnp.transpose(a).reshape(shape)
    else:
        f2 = lambda a: a.reshape(-1, shape[-1])
        b2 = lambda a: a.reshape(shape)
    d_, m_, v_ = _adamw(f"adamw_{n}", f2(w[n]), f2(grads[n]), f2(mom_m[n]), f2(mom_v[n]))
    delta[n], new_m[n], new_v[n] = b2(d_), b2(m_), b2(v_)
    return d_


def _cast_weight(w, n, l, deps=()):
    p = (2 * lax.axis_index("x") + lax.axis_index("y")).astype(jnp.int32)
    a3 = w[n].reshape(w[n].shape[0], -1, w[n].shape[-1])
    layers, rows, cols = a3.shape
    out = ('x', (N_CHIPS, rows, cols), BF16, (None, 'tr', cols), lambda r, pr: (pr[0], r, 0))
    b = _rows(f"cast_{n}_{l}", lambda v: (v,), [(a3, 's', cols, 1)], [out], 256, pre=jnp.stack([p, jnp.int32(l)]), deps=deps)[0]
    return b.reshape(N_CHIPS, 2, rows // 2, cols)


def _step(x, loss_target, w, mom_m, mom_v):
    T, D = x.shape[1], x.shape[2]
    E = D
    G, P, C = w['s5_a_re'].shape[1], S5_STATE, S5_GROUP
    H = E // FOX_HEAD_DIM
    PG = len(POOL_WINDOWS)
    PD = E // PG
    NC = G // GROUPS_PER_CHUNK
    L = GROUPS_PER_CHUNK * P
    tq = _t(256, T)
    nq = T // tq

    phases = [[('s5_in_proj', 0)],
              [('s5_w_glu', 0), ('out_proj', 0)],
              [('out_proj', 1), ('fox_in_proj', 0)],
              [('out_proj', 2), ('pool_in_proj', 0), ('pool_w_group', 0), ('out_proj', 3), ('s5_in_proj', 1), ('s5_w_glu', 1)]]
    W = {}
    flight = {}

    def landed(keys, bufs):
        for k, b in zip(keys, bufs):
            W[k] = b.reshape(N_CHIPS, 2 * b.shape[2], b.shape[3])

    def take_phase(ph, after):
        bufs, send, recv, _ = flight.pop(ph)
        landed(phases[ph], _gather_forward(f"gather_{ph}_pass", _gather_wait(f"gather_{ph}_wait", bufs, send, recv, after)))

    small_full = {}
    chip = 2 * lax.axis_index("x") + lax.axis_index("y")
    sv = [lax.dynamic_update_index_in_dim(jnp.zeros((N_CHIPS, 2) + w[n].shape, F32), jnp.stack([w[n], w[n]]), chip, 0)
          for n in SMALL_SHARDED]
    got = _chip_allgather("gather_vectors", sv)
    for n, g in zip(SMALL_SHARDED, got):
        small_full[n] = jnp.transpose(g[:, 0], (1, 0, 2)).reshape(w[n].shape[0], E)
    after = [got[0]]
    for ph in range(len(phases)):
        flight[ph] = _gather_start(f"gather_{ph}_start", [_cast_weight(w, n, l, after if ph else ()) for n, l in phases[ph]], after)
        after = [flight[ph][3]]
    take_phase(0, after[0])
    gather_tokens = after

    norm_w = w['norm_w']
    h = x.reshape(T, D)
    saved = []
    dparts = {}

    def s5_consts(j):
        ar, ai, fr, fi = _s5_disc_fwd(f"s5_disc_{j}", w['s5_a_re'][j], w['s5_a_im'][j], w['s5_log_dt'][j].reshape(G, 1))
        br, bi = w['s5_b_re'][j].reshape(G * P, C), w['s5_b_im'][j].reshape(G * P, C)
        bbr, bbi = _s5_bbar(f"s5_bbar_{j}", fr.reshape(G * P, 1), fi.reshape(G * P, 1), br, bi)
        bbd = jnp.concatenate([_compact(bbr.reshape(G, P, C), NC), _compact(bbi.reshape(G, P, C), NC)], axis=2).astype(BF16)
        ct = lambda v: jnp.transpose(v, (0, 2, 1))
        cbd = jnp.concatenate([_compact(ct(w['s5_c_re'][j]), NC), -_compact(ct(w['s5_c_im'][j]), NC)], axis=2).astype(BF16)
        return dict(ar=ar, ai=ai, fr=fr, fi=fi, br=br, bi=bi, bbd=bbd, cbd=cbd,
                    ar3=ar.reshape(NC, 1, L), ai3=ai.reshape(NC, 1, L))

    for i in range(4):
        kind, j = i % 3, i // 3
        nw = norm_w[i].reshape(1, D)
        xn = _norm_fwd(f"norm_{i}", h, nw, deps=gather_tokens if i == 0 else ())
        if kind == 0:
            k5 = s5_consts(j)
            proj = _mm_proj(f"s5_proj_{i}", xn, W[('s5_in_proj', j)])
            dsk = small_full['s5_d'][j].reshape(1, E)
            y1, g, hs = _s5_fwd(f"s5_scan_{i}", proj, k5['bbd'], k5['cbd'], k5['ar3'], k5['ai3'], dsk, E)
            bglu = small_full['s5_b_glu'][j].reshape(1, E)
            if i == 0:
                take_phase(1, y1)

            def glu_epi(acc, b, y1t, z):
                lin = acc + b
                return lin, (_gelu(y1t) * _sigmoid(lin)) * _silu(z)

            lin, a = _mm_rowsharded(
                f"s5_glu_{i}", g, W[('s5_w_glu', j)], epi=glu_epi,
                extras=lambda tm, tn: [(bglu, _rowvec(tn)), (y1, _tile(tm, tn)), (proj, _tile(tm, tn, E // tn))],
                outs_fn=lambda tm, tn: [((T, E), F32, _tile(tm, tn)), ((T, E), BF16, _tile(tm, tn))])
            saved.append(dict(h=h, xn=xn, proj=proj, y1=y1, g=g, hs=hs, lin=lin, a=a, k5=k5, dsk=dsk))
        elif kind == 1:
            fox_w = jnp.transpose(W[('fox_in_proj', j)], (1, 0, 2)).reshape(D, -1)
            w_qkvz = fox_w[:, :4 * E]
            w_f = jnp.pad(fox_w[:, 4 * E:], ((0, 0), (0, LANES - H)))
            proj = _mm_plain(f"fox_proj_{i}", xn, w_qkvz)[0]
            flog = _mm_plain(f"fox_gate_proj_{i}", xn, w_f)[0]
            fb = jnp.pad(w['fox_f_bias'][j].reshape(1, H), ((0, 0), (0, LANES - H)))
            wq, wk = w['fox_q_norm'][j].reshape(1, FOX_HEAD_DIM), w['fox_k_norm'][j].reshape(1, FOX_HEAD_DIM)
            qn, kn = _qk_norm(f"fox_qk_norm_{i}", proj, wq, wk, H)
            cum = _cum_rows(f"fox_cum_{i}", flog, fb, False, True)
            cum_t = jnp.transpose(cum)[:H]
            cum_q = jnp.broadcast_to(cum_t[:, :, None], (H, T, LANES))
            cum_k = cum_t.reshape(H, nq, 1, tq)
            y, lse = _attn_fwd(f"fox_attn_{i}", qn, kn, proj, cum_q, cum_k, H)
            a = _rows(f"fox_gate_{i}", lambda yt, z: (yt * _silu(z),), [(y, 'r', E, 0), (proj, 'r', E, 3)], [('r', E, BF16)], 256)[0]
            saved.append(dict(h=h, xn=xn, proj=proj, flog=flog, fb=fb, wq=wq, wk=wk, qn=qn, kn=kn, cum_q=cum_q, cum_k=cum_k, y=y, lse=lse, a=a,
                              w_qkvz=w_qkvz, w_f=w_f))
        else:
            w_pg = jnp.transpose(W[('pool_w_group', j)].reshape(N_CHIPS, PG, PD // N_CHIPS, PD), (1, 0, 2, 3)).reshape(PG, PD, PD)
            proj = _mm_proj(f"pool_proj_{i}", xn, W[('pool_in_proj', j)])
            pm = _pool_fwd(f"pool_win_{i}", proj, E)
            scale = small_full['pool_scale'][j].reshape(1, E)
            tm, tn, tk = _t(512, T), _t(512, PD), _t(K_STEP, PD)
            kb, nb = PD // tk, PD // tn
            mixed, a = _mm(
                f"pool_mix_{i}", pm, w_pg, M=T, N=PD, K=PD, tm=tm, tn=tn, tk=tk, groups=PG,
                a_spec=_bs((tm, tk), lambda g, m, n, k: (m, g * kb + k)),
                b_spec=_bs((None, tk, tn), lambda g, m, n, k: (g, k, n)),
                extras=[(scale, _bs((1, tn), lambda g, m, n, k: (0, g * nb + n))),
                        (proj, _bs((tm, tn), lambda g, m, n, k: (m, E // tn + g * nb + n)))],
                epi=lambda acc, sc, z: (acc, (acc * sc) * _silu(z)),
                outs=[((T, E), F32, _bs((tm, tn), lambda g, m, n, k: (m, g * nb + n))),
                      ((T, E), BF16, _bs((tm, tn), lambda g, m, n, k: (m, g * nb + n)))])
            saved.append(dict(h=h, xn=xn, proj=proj, pm=pm, mixed=mixed, scale=scale, a=a, w_pg=w_pg))
        h = _mm_rowsharded(f"out_proj_{i}", saved[-1]['a'], W[('out_proj', i)], epi=lambda acc, r: (r + acc,),
                           extras=lambda tm, tn: [(h, _tile(tm, tn))],
                           outs_fn=lambda tm, tn: [((T, D), F32, _tile(tm, tn))])[0]
        if i < 2:
            take_phase(i + 2, h)

    dh, dh16, loss_cols = _loss(h, loss_target.reshape(T, D))
    loss = lax.psum(jnp.sum(loss_cols), ("x", "y", "c"))

    gsmall = {n: [None] * w[n].shape[0] for n in SMALL}
    big_index = {n: o for o, n in enumerate(BIG)}
    rs_shapes = [None] * (len(BIG) + 1)
    rs_bufs = [None] * (len(BIG) + 1)
    rs_dests_all = []
    pending = None

    def reduce_layer(tag, named_parts):
        parts, dests = [], []
        for n, l, pt in named_parts:
            o = big_index[n] if n in big_index else len(BIG)
            half = pt.shape[2:]
            rs_shapes[o] = (N_CHIPS if l == 'chip' else w[n].shape[0], 2, math.prod(half[:-1]), half[-1])
            parts.append(pt)
            dests.append((o, l))
        rs_dests_all.extend(dests)
        state, token = _reduce_begin(tag, parts)
        return (tag, state, dests), token

    token = loss.reshape(1, 1)
    for i in reversed(range(4)):
        kind, j = i % 3, i // 3
        sv_ = saved[i]
        nw = norm_w[i].reshape(1, D)
        w_out = W[('out_proj', i)]
        after_start = [token] if token is not None else ()
        layer_parts = [('out_proj', i, _mm_dw_rows(f"d_out_proj_{i}", sv_['a'], dh16, deps=after_start))]
        if kind == 0:
            w_glu = W[('s5_w_glu', j)]
            proj, y1, lin, k5 = sv_['proj'], sv_['y1'], sv_['lin'], sv_['k5']

            def da_epi(da, y1t, lint, z):
                gt, sg = _gelu(y1t), _sigmoid(lint)
                dy2 = da * _silu(z)
                dlin = (dy2 * gt) * (sg * (1.0 - sg))
                return da * (gt * sg) * _dsilu(z), dlin, dy2 * sg, _colsum(dlin)

            nm = T // _t(512, T)
            dz, dlin, dgd, dbg = _mm_rowsharded_t(
                f"d_s5_act_{i}", dh16, w_out, epi=da_epi, deps=after_start,
                extras=lambda tm, tn: [(y1, _tile(tm, tn)), (lin, _tile(tm, tn)), (proj, _tile(tm, tn, E // tn))],
                outs_fn=lambda tm, tn: [((T, E), BF16, _tile(tm, tn)), ((T, E), BF16, _tile(tm, tn)), ((T, E), F32, _tile(tm, tn)),
                                        ((nm, 1, E), F32, _bs((None, 1, tn), lambda g, m, n, k: (m, 0, n)))])
            gsmall['s5_b_glu'][j] = jnp.sum(dbg, axis=(0, 1))
            layer_parts.append(('s5_w_glu', j, _mm_dw_rows(f"d_s5_w_glu_{i}", sv_['g'], dlin)))
            glu_deps = ()
            if i == 0:
                early, early_token = reduce_layer("l0a", layer_parts)
                layer_parts, glu_deps = [], [early_token]
            dy1 = _mm_rowsharded_t(
                f"d_s5_glu_{i}", dlin, w_glu, epi=lambda acc, d, y1t: ((acc + d) * _dgelu(y1t),), deps=glu_deps,
                extras=lambda tm, tn: [(dgd, _tile(tm, tn)), (y1, _tile(tm, tn))],
                outs_fn=lambda tm, tn: [((T, E), F32, _tile(tm, tn))])[0]
            du, dbd, dcd, dab, ddk = _s5_bwd(f"d_s5_scan_{i}", dy1, proj, sv_['hs'], k5['bbd'], k5['cbd'], k5['ar3'], k5['ai3'], sv_['dsk'], E)
            gsmall['s5_d'][j] = ddk.reshape(E)
            gsmall['s5_c_re'][j] = jnp.transpose(_uncompact(dcd[:, :, :L], G), (0, 2, 1))
            gsmall['s5_c_im'][j] = -jnp.transpose(_uncompact(dcd[:, :, L:], G), (0, 2, 1))
            dbbr = _uncompact(dbd[:, :, :L], G).reshape(G * P, C)
            dbbi = _uncompact(dbd[:, :, L:], G).reshape(G * P, C)
            dbr, dbi, dfr, dfi = _s5_bbar_bwd(f"d_s5_bbar_{i}", k5['fr'].reshape(G * P, 1), k5['fi'].reshape(G * P, 1), k5['br'], k5['bi'], dbbr, dbbi)
            gsmall['s5_b_re'][j] = dbr.reshape(G, P, C)
            gsmall['s5_b_im'][j] = dbi.reshape(G, P, C)
            dab = jnp.sum(dab, axis=1)
            dare, daim, dldt = _s5_disc_bwd(f"d_s5_disc_{i}", w['s5_a_re'][j], w['s5_a_im'][j], w['s5_log_dt'][j].reshape(G, 1),
                                            (dab[:, :L].reshape(G, P), dab[:, L:].reshape(G, P), dfr.reshape(G, P), dfi.reshape(G, P)))
            gsmall['s5_a_re'][j], gsmall['s5_a_im'][j], gsmall['s5_log_dt'][j] = dare, daim, dldt.reshape(G)
            dproj = jnp.concatenate([du, dz], axis=1)
            layer_parts.append(('s5_in_proj', j, _mm_dw_cols(f"d_s5_in_proj_{i}", sv_['xn'], dproj)))
            dxn = _mm_colsharded_t(f"d_s5_xn_{i}", dproj, W[('s5_in_proj', j)])
        elif kind == 1:
            proj, y = sv_['proj'], sv_['y']
            do, dz = _mm_rowsharded_t(
                f"d_fox_act_{i}", dh16, w_out, epi=lambda da, yt, z: (da * _silu(z), (da * yt) * _dsilu(z)), deps=after_start,
                extras=lambda tm, tn: [(y, _tile(tm, tn)), (proj, _tile(tm, tn, 3 * E // tn))],
                outs_fn=lambda tm, tn: [((T, E), F32, _tile(tm, tn)), ((T, E), BF16, _tile(tm, tn))])
            dqn, dkn, dv, dcq, dck = _attn_bwd(f"d_fox_attn_{i}", sv_['qn'], sv_['kn'], proj, do, y, sv_['lse'], sv_['cum_q'], sv_['cum_k'], H)
            dq, dk, dwq, dwk = _qk_norm_bwd(f"d_fox_qk_norm_{i}", proj, sv_['wq'], sv_['wk'], dqn, dkn, H)
            gsmall['fox_q_norm'][j], gsmall['fox_k_norm'][j] = dwq.reshape(-1), dwk.reshape(-1)
            dcum = dcq + jnp.pad(jnp.transpose(dck.reshape(H, T)), ((0, 0), (0, LANES - H)))
            dls = _cum_rows(f"d_fox_cum_{i}", dcum, jnp.zeros((1, LANES), F32), True, False)
            dflog, dfb = _rows(f"d_fox_gate_{i}", lambda d, f, b: ((lambda r: (r, _colsum(r)))(d * _sigmoid(-(f + b)))),
                               [(dls, 'r', LANES, 0), (sv_['flog'], 'r', LANES, 0), (sv_['fb'], 'b', LANES, 0)],
                               [('r', LANES, BF16), ('a', LANES, F32)], 256)
            gsmall['fox_f_bias'][j] = dfb[0, :H]
            dproj = jnp.concatenate([dq, dk, dv, dz], axis=1)
            tkT = _t(K_STEP, T)
            dw_qkvz = _mm(f"d_fox_in_proj_{i}", sv_['xn'], dproj, M=D, N=4 * E, K=T, tm=_t(512, D), tn=_t(1024, 4 * E), tk=tkT, ta=True,
                          a_spec=_bs((tkT, _t(512, D)), lambda g, m, n, k: (k, m)),
                          b_spec=_bs((tkT, _t(1024, 4 * E)), lambda g, m, n, k: (k, n)),
                          outs=[((D, 4 * E), BF16, _tile(_t(512, D), _t(1024, 4 * E)))])[0]
            dw_f = _mm(f"d_fox_gate_proj_{i}", sv_['xn'], dflog, M=D, N=LANES, K=T, tm=_t(512, D), tn=LANES, tk=tkT, ta=True,
                       a_spec=_bs((tkT, _t(512, D)), lambda g, m, n, k: (k, m)),
                       b_spec=_bs((tkT, LANES), lambda g, m, n, k: (k, n)),
                       outs=[((D, LANES), BF16, _tile(_t(512, D), LANES))])[0]
            dw_fox = jnp.concatenate([dw_qkvz, dw_f[:, :H]], axis=1)
            sw = dw_fox.shape[1] // N_CHIPS
            layer_parts.append(('fox_in_proj', j, jnp.transpose(dw_fox.reshape(2, D // 2, N_CHIPS, sw), (0, 2, 1, 3))))
            w_qkvz, w_f = sv_['w_qkvz'], sv_['w_f']
            dxn_f = _mm(f"d_fox_xn_gate_{i}", dflog, w_f, M=T, N=D, K=LANES, tm=_t(512, T), tn=_t(1024, D), tk=LANES, tb=True,
                        a_spec=_bs((_t(512, T), LANES), lambda g, m, n, k: (m, k)),
                        b_spec=_bs((_t(1024, D), LANES), lambda g, m, n, k: (n, k)),
                        outs=[((T, D), F32, _tile(_t(512, T), _t(1024, D)))])[0]
            tm, tn, tk = _t(512, T), _t(1024, D), _t(K_STEP, 4 * E)
            dxn = _mm(f"d_fox_xn_{i}", dproj, w_qkvz, M=T, N=D, K=4 * E, tm=tm, tn=tn, tk=tk, tb=True,
                      a_spec=_bs((tm, tk), lambda g, m, n, k: (m, k)), b_spec=_bs((tn, tk), lambda g, m, n, k: (n, k)),
                      extras=[(dxn_f, _tile(tm, tn))], epi=lambda acc, e: (acc + e,),
                      outs=[((T, D), F32, _tile(tm, tn))])[0]
        else:
            proj, mixed, scale = sv_['proj'], sv_['mixed'], sv_['scale']
            nm = T // _t(512, T)

            def pool_epi(da, mx, sc, z):
                dy = da * _silu(z)
                return (da * (mx * sc)) * _dsilu(z), dy * sc, _colsum(dy * mx)

            dz, dmix, dsc = _mm_rowsharded_t(
                f"d_pool_act_{i}", dh16, w_out, epi=pool_epi, deps=after_start,
                extras=lambda tm, tn: [(mixed, _tile(tm, tn)), (scale, _rowvec(tn)), (proj, _tile(tm, tn, E // tn))],
                outs_fn=lambda tm, tn: [((T, E), BF16, _tile(tm, tn)), ((T, E), BF16, _tile(tm, tn)),
                                        ((nm, 1, E), F32, _bs((None, 1, tn), lambda g, m, n, k: (m, 0, n)))])
            gsmall['pool_scale'][j] = jnp.sum(dsc, axis=(0, 1))
            w_pg = sv_['w_pg']
            tkw = PD // N_CHIPS
            tk = _t(K_STEP, T)
            layer_parts.append(('pool_w_group', j, _mm(
                f"d_pool_w_group_{i}", sv_['pm'], dmix, M=PD, N=PD, K=T, tm=tkw, tn=PD, tk=tk, groups=PG, ta=True,
                a_spec=_bs((tk, tkw), lambda g, m, n, k: (k, g * (PD // tkw) + m)),
                b_spec=_bs((tk, PD), lambda g, m, n, k: (k, g)),
                outs=[((2, N_CHIPS, PG // 2, tkw, PD), BF16, _bs((None, None, None, tkw, PD), lambda g, m, n, k: (g // (PG // 2), m, g % (PG // 2), 0, 0)))])[0]))
            tm, tn2, tk2 = _t(512, T), _t(512, PD), _t(K_STEP, PD)
            dpm = _mm(f"d_pool_mix_{i}", dmix, w_pg, M=T, N=PD, K=PD, tm=tm, tn=tn2, tk=tk2, groups=PG, tb=True,
                      a_spec=_bs((tm, tk2), lambda g, m, n, k: (m, g * (PD // tk2) + k)),
                      b_spec=_bs((None, tn2, tk2), lambda g, m, n, k: (g, n, k)),
                      outs=[((T, E), F32, _bs((tm, tn2), lambda g, m, n, k: (m, g * (PD // tn2) + n)))])[0]
            du = _pool_bwd(f"d_pool_win_{i}", dpm, E)
            dproj = jnp.concatenate([du, dz], axis=1)
            layer_parts.append(('pool_in_proj', j, _mm_dw_cols(f"d_pool_in_proj_{i}", sv_['xn'], dproj)))
            dxn = _mm_colsharded_t(f"d_pool_xn_{i}", dproj, W[('pool_in_proj', j)])
        dh, dh16, dnw = _norm_bwd(f"d_norm_{i}", dxn, sv_['h'], nw, dh)
        gsmall['norm_w'][i] = dnw.reshape(D)
        if pending is not None:
            _reduce_end(pending[0], pending[1], dh16, pending[2], rs_bufs, rs_shapes)
        if i > 0:
            pending, token = reduce_layer(f"l{i}", layer_parts)
    grad_x = dh.reshape(x.shape)

    small_flat = jnp.concatenate([jnp.stack(gsmall[n]).reshape(-1) for n in SMALL])
    n_small = small_flat.shape[0]
    unit = 2 * N_CHIPS * 16 * LANES
    n_pad = -(-n_small // unit) * unit
    R = n_pad // (2 * N_CHIPS * LANES)
    small_part = jnp.pad(small_flat, (0, n_pad - n_small)).astype(BF16).reshape(2, N_CHIPS, R, LANES)
    pending, token = reduce_layer("l0", layer_parts + [('small', 'chip', small_part)])
    _reduce_end(early[0], early[1], token, early[2], rs_bufs, rs_shapes)
    nb = len(BIG)
    done_items = [d for d in rs_dests_all if d not in pending[2]]
    rs_bufs[:nb] = _pair_share("rs_pair_share_a", rs_bufs[:nb], done_items, deps=[token])
    late = [o for o, _ in pending[2]]
    delta, new_m, new_v = {}, {}, {}
    grads = {}
    last = token[:1, :1]
    for o, n in enumerate(BIG):
        if o not in late:
            grads[n] = rs_bufs[o].reshape(w[n].shape)
            last = last + _adamw_big(n, w, grads, mom_m, mom_v, delta, new_m, new_v)[:1, :1]
    _reduce_end(pending[0], pending[1], last, pending[2], rs_bufs, rs_shapes)
    shared = _pair_share("rs_pair_share_b", [rs_bufs[o] for o in late], [(k, l) for k, (_, l) in enumerate(pending[2])])
    for k, o in enumerate(late):
        rs_bufs[o] = shared[k]
        if o < nb:
            grads[BIG[o]] = shared[k].reshape(w[BIG[o]].shape)
            _adamw_big(BIG[o], w, grads, mom_m, mom_v, delta, new_m, new_v)
    small_all = _chip_allgather("gather_small_grads", [rs_bufs[nb]])[0]
    small_all = jnp.transpose(small_all, (1, 0, 2, 3)).reshape(-1)[:n_small]
    off = 0
    p = 2 * lax.axis_index("x") + lax.axis_index("y")
    for n in SMALL:
        full_shape = (w[n].shape[0], E) if n in SMALL_SHARDED else w[n].shape
        size = math.prod(full_shape)
        gfull = small_all[off:off + size].reshape(full_shape)
        off += size
        if n in SMALL_SHARDED:
            gfull = lax.dynamic_slice_in_dim(gfull, p * (E // N_CHIPS), E // N_CHIPS, axis=1)
        grads[n] = gfull

    for n in SMALL:
        shape = w[n].shape
        if n in GROUP_AXIS_1:
            perm = (0,) + tuple(range(2, len(shape))) + (1,)
            inv = (0, len(shape) - 1) + tuple(range(1, len(shape) - 1))
            view = lambda a: jnp.transpose(a, perm).reshape(-1, shape[1])
            back = lambda a: jnp.transpose(a.reshape(tuple(shape[k] for k in perm)), inv)
        else:
            view = lambda a: a.reshape(-1, shape[-1])
            back = lambda a: a.reshape(shape)
        d_, m_, v_ = _adamw(f"adamw_{n}", view(w[n]), view(grads[n]), view(mom_m[n]), view(mom_v[n]))
        delta[n], new_m[n], new_v[n] = back(d_), back(m_), back(v_)
    return (loss, grad_x, *[grads[n] for n in ORDER], *[delta[n] for n in ORDER], *[new_m[n] for n in ORDER], *[new_v[n] for n in ORDER])
```
